```python
import math
import jax, jax.numpy as jnp
from jax import lax
import numpy as np

D_MODEL = 2048
BATCH = 8
SEQ = 4096
DEPTH = 2

N_A_LAYERS = DEPTH // 2
N_B_LAYERS = DEPTH - N_A_LAYERS
S5_WIDTH = D_MODEL
S5_GROUP = 16
S5_GROUPS = S5_WIDTH // S5_GROUP
S5_STATE = 64
DT_MIN = 1e-3
DT_MAX = 1e-1
FOX_HEAD_DIM = 128
FOX_HEADS = D_MODEL // FOX_HEAD_DIM
FOX_WIDTH = FOX_HEADS * FOX_HEAD_DIM
Q_BLOCK = 128
RMS_EPS = 1e-6
NEG_INF = -1e30

kernel_name = "yoco_s5_fox_hybrid"

F32 = jnp.float32


def rmsnorm(x, g):
    xf = x.astype(F32)
    y = xf * lax.rsqrt(jnp.mean(xf * xf, axis=-1, keepdims=True) + RMS_EPS)
    return (y * g.astype(F32)).astype(x.dtype)


def s5_ssm(u, a_re, a_im, log_dt, b_re, b_im, c_re, c_im, d_skip):
    bsz, seq, _ = u.shape
    uf = u.astype(F32).reshape(bsz, seq, S5_GROUPS, S5_GROUP)
    dt = jnp.exp(log_dt.astype(F32))[:, None]
    ar = a_re.astype(F32)
    ai = a_im.astype(F32)
    mag = jnp.exp(ar * dt)
    abar_re = mag * jnp.cos(ai * dt)
    abar_im = mag * jnp.sin(ai * dt)
    den = ar * ar + ai * ai
    nr = abar_re - 1.0
    coef_re = (nr * ar + abar_im * ai) / den
    coef_im = (abar_im * ar - nr * ai) / den
    bu_re = jnp.einsum('bsgc,gpc->bsgp', uf, b_re.astype(F32))
    bu_im = jnp.einsum('bsgc,gpc->bsgp', uf, b_im.astype(F32))
    x_re = coef_re * bu_re - coef_im * bu_im
    x_im = coef_re * bu_im + coef_im * bu_re
    shape_a = (1, seq, S5_GROUPS, S5_STATE)
    a_seq_re = jnp.broadcast_to(abar_re, shape_a)
    a_seq_im = jnp.broadcast_to(abar_im, shape_a)

    def combine(left, right):
        a1r, a1i, b1r, b1i = left
        a2r, a2i, b2r, b2i = right
        return (a2r * a1r - a2i * a1i,
                a2r * a1i + a2i * a1r,
                a2r * b1r - a2i * b1i + b2r,
                a2r * b1i + a2i * b1r + b2i)

    _, _, h_re, h_im = lax.associative_scan(combine, (a_seq_re, a_seq_im, x_re, x_im), axis=1)
    y = (jnp.einsum('bsgp,gcp->bsgc', h_re, c_re.astype(F32))
         - jnp.einsum('bsgp,gcp->bsgc', h_im, c_im.astype(F32)))
    y = y + d_skip.astype(F32).reshape(S5_GROUPS, S5_GROUP) * uf
    return y.reshape(bsz, seq, S5_WIDTH)


def s5_layer(h, g_pre, g_post, w_in, a_re, a_im, log_dt, b_re, b_im, c_re, c_im, d_skip, w_glu, b_glu, w_out):
    xn = rmsnorm(h, g_pre)
    uz = xn @ w_in
    u, z = jnp.split(uz, 2, axis=-1)
    y = s5_ssm(u, a_re, a_im, log_dt, b_re, b_im, c_re, c_im, d_skip)
    y = jax.nn.gelu(y)
    y = y * jax.nn.sigmoid(y @ w_glu.astype(F32) + b_glu.astype(F32))
    y = y.astype(h.dtype) * jax.nn.silu(z)
    return h + rmsnorm(y @ w_out, g_post)


def shared_kv(h, g_kv, w_kv, b_f):
    bsz, seq, _ = h.shape
    kvf = rmsnorm(h, g_kv) @ w_kv
    k = kvf[..., :FOX_WIDTH]
    v = kvf[..., FOX_WIDTH:2 * FOX_WIDTH]
    f_logit = kvf[..., 2 * FOX_WIDTH:]
    k = k.reshape(bsz, seq, FOX_HEADS, FOX_HEAD_DIM).transpose(0, 2, 1, 3)
    v = v.reshape(bsz, seq, FOX_HEADS, FOX_HEAD_DIM).transpose(0, 2, 1, 3)
    log_f = jax.nn.log_sigmoid(f_logit.astype(F32) + b_f.astype(F32))
    cum = jnp.cumsum(log_f, axis=1).transpose(0, 2, 1)
    return k, v, cum


def fox_attention(q, k, v, cum):
    bsz, nh, seq, dh = q.shape
    nblk = seq // Q_BLOCK
    qb = q.reshape(bsz, nh, nblk, Q_BLOCK, dh).transpose(2, 0, 1, 3, 4)
    cb = cum.reshape(bsz, nh, nblk, Q_BLOCK).transpose(2, 0, 1, 3)
    kpos = jnp.arange(seq)
    scale = dh ** -0.5

    def block(args):
        q_i, c_i, i = args
        s = jnp.einsum('bhqd,bhkd->bhqk', q_i, k, preferred_element_type=F32) * scale
        s = s + c_i[..., :, None] - cum[:, :, None, :]
        qpos = i * Q_BLOCK + jnp.arange(Q_BLOCK)
        s = jnp.where(kpos[None, :] <= qpos[:, None], s, NEG_INF)
        p = jax.nn.softmax(s, axis=-1)
        return jnp.einsum('bhqk,bhkd->bhqd', p.astype(v.dtype), v)

    o = lax.map(block, (qb, cb, jnp.arange(nblk)))
    return o.transpose(1, 2, 0, 3, 4).reshape(bsz, nh, seq, dh)


def fox_layer(h, g_pre, g_post, w_in, w_out, k, v, cum):
    bsz, seq, _ = h.shape
    qz = rmsnorm(h, g_pre) @ w_in
    q, z = jnp.split(qz, 2, axis=-1)
    q = q.reshape(bsz, seq, FOX_HEADS, FOX_HEAD_DIM).transpose(0, 2, 1, 3)
    o = fox_attention(q, k, v, cum)
    o = o.transpose(0, 2, 1, 3).reshape(bsz, seq, FOX_WIDTH)
    o = o.astype(h.dtype) * jax.nn.silu(z)
    return h + rmsnorm(o @ w_out, g_post)


def _fwd_setup_inputs(seed: int = 0) -> dict:
    key = jax.random.key(seed)
    ks = jax.random.split(key, 24)
    nrm = lambda k, shp, s: jax.random.normal(k, shp, F32) * s
    n = jnp.arange(S5_STATE, dtype=F32)
    a_re = -0.5 + nrm(ks[4], (N_A_LAYERS, S5_GROUPS, S5_STATE), 0.01)
    a_im = math.pi * n + nrm(ks[5], (N_A_LAYERS, S5_GROUPS, S5_STATE), 0.01)
    log_dt = jax.random.uniform(ks[6], (N_A_LAYERS, S5_GROUPS), F32, math.log(DT_MIN), math.log(DT_MAX))
    return {
        "x": nrm(ks[0], (BATCH, SEQ, D_MODEL), 1.0),
        "norm_pre": 1.0 + nrm(ks[1], (DEPTH, D_MODEL), 0.02),
        "norm_post": 1.0 + nrm(ks[2], (DEPTH, D_MODEL), 0.02),
        "s5_w_in": nrm(ks[3], (N_A_LAYERS, D_MODEL, 2 * S5_WIDTH), D_MODEL ** -0.5),
        "s5_a_re": a_re,
        "s5_a_im": a_im,
        "s5_log_dt": log_dt,
        "s5_b_re": nrm(ks[7], (N_A_LAYERS, S5_GROUPS, S5_STATE, S5_GROUP), (2 * S5_GROUP) ** -0.5),
        "s5_b_im": nrm(ks[8], (N_A_LAYERS, S5_GROUPS, S5_STATE, S5_GROUP), (2 * S5_GROUP) ** -0.5),
        "s5_c_re": nrm(ks[9], (N_A_LAYERS, S5_GROUPS, S5_GROUP, S5_STATE), S5_STATE ** -0.5),
        "s5_c_im": nrm(ks[10], (N_A_LAYERS, S5_GROUPS, S5_GROUP, S5_STATE), S5_STATE ** -0.5),
        "s5_d": nrm(ks[11], (N_A_LAYERS, S5_WIDTH), 1.0),
        "s5_w_glu": nrm(ks[12], (N_A_LAYERS, S5_WIDTH, S5_WIDTH), S5_WIDTH ** -0.5),
        "s5_b_glu": nrm(ks[13], (N_A_LAYERS, S5_WIDTH), 0.01),
        "s5_w_out": nrm(ks[14], (N_A_LAYERS, S5_WIDTH, D_MODEL), S5_WIDTH ** -0.5),
        "kv_norm": 1.0 + nrm(ks[15], (D_MODEL,), 0.02),
        "kv_w": nrm(ks[16], (D_MODEL, 2 * FOX_WIDTH + FOX_HEADS), D_MODEL ** -0.5),
        "kv_b_f": jax.random.uniform(ks[17], (FOX_HEADS,), F32, 2.0, 7.0),
        "fox_w_in": nrm(ks[18], (N_B_LAYERS, D_MODEL, 2 * FOX_WIDTH), D_MODEL ** -0.5),
        "fox_w_out": nrm(ks[19], (N_B_LAYERS, FOX_WIDTH, D_MODEL), FOX_WIDTH ** -0.5),
    }


def _fwd_reference(x, norm_pre, norm_post, s5_w_in, s5_a_re, s5_a_im, s5_log_dt, s5_b_re, s5_b_im,
              s5_c_re, s5_c_im, s5_d, s5_w_glu, s5_b_glu, s5_w_out, kv_norm, kv_w, kv_b_f,
              fox_w_in, fox_w_out):
    h = x
    k = v = cum = None
    for layer in range(DEPTH):
        if layer < N_A_LAYERS:
            i = layer
            h = s5_layer(h, norm_pre[layer], norm_post[layer], s5_w_in[i], s5_a_re[i], s5_a_im[i],
                         s5_log_dt[i], s5_b_re[i], s5_b_im[i], s5_c_re[i], s5_c_im[i], s5_d[i],
                         s5_w_glu[i], s5_b_glu[i], s5_w_out[i])
        else:
            if layer == N_A_LAYERS:
                k, v, cum = shared_kv(h, kv_norm, kv_w, kv_b_f)
            j = layer - N_A_LAYERS
            h = fox_layer(h, norm_pre[layer], norm_post[layer], fox_w_in[j], fox_w_out[j], k, v, cum)
    return h


import jax as _jax
import jax.numpy as _jnp

TWIN_FORMAT = 'train_step'
FWD_PARAMS = ['x', 'norm_pre', 'norm_post', 's5_w_in', 's5_a_re', 's5_a_im', 's5_log_dt', 's5_b_re', 's5_b_im', 's5_c_re', 's5_c_im', 's5_d', 's5_w_glu', 's5_b_glu', 's5_w_out', 'kv_norm', 'kv_w', 'kv_b_f', 'fox_w_in', 'fox_w_out']
TWIN_WEIGHTS = ['norm_pre', 'norm_post', 's5_w_in', 's5_a_re', 's5_a_im', 's5_log_dt', 's5_b_re', 's5_b_im', 's5_c_re', 's5_c_im', 's5_d', 's5_w_glu', 's5_b_glu', 's5_w_out', 'kv_norm', 'kv_w', 'kv_b_f', 'fox_w_in', 'fox_w_out']
TWIN_DIFF_INPUT = 'x'
TWIN_INPUTS = ['x', 'norm_pre', 'norm_post', 's5_w_in', 's5_a_re', 's5_a_im', 's5_log_dt', 's5_b_re', 's5_b_im', 's5_c_re', 's5_c_im', 's5_d', 's5_w_glu', 's5_b_glu', 's5_w_out', 'kv_norm', 'kv_w', 'kv_b_f', 'fox_w_in', 'fox_w_out', 'loss_target', 'm_norm_pre', 'm_norm_post', 'm_s5_w_in', 'm_s5_a_re', 'm_s5_a_im', 'm_s5_log_dt', 'm_s5_b_re', 'm_s5_b_im', 'm_s5_c_re', 'm_s5_c_im', 'm_s5_d', 'm_s5_w_glu', 'm_s5_b_glu', 'm_s5_w_out', 'm_kv_norm', 'm_kv_w', 'm_kv_b_f', 'm_fox_w_in', 'm_fox_w_out', 'v_norm_pre', 'v_norm_post', 'v_s5_w_in', 'v_s5_a_re', 'v_s5_a_im', 'v_s5_log_dt', 'v_s5_b_re', 'v_s5_b_im', 'v_s5_c_re', 'v_s5_c_im', 'v_s5_d', 'v_s5_w_glu', 'v_s5_b_glu', 'v_s5_w_out', 'v_kv_norm', 'v_kv_w', 'v_kv_b_f', 'v_fox_w_in', 'v_fox_w_out']
TWIN_OUTPUTS = ['loss', 'grad_x', 'grad_norm_pre', 'grad_norm_post', 'grad_s5_w_in', 'grad_s5_a_re', 'grad_s5_a_im', 'grad_s5_log_dt', 'grad_s5_b_re', 'grad_s5_b_im', 'grad_s5_c_re', 'grad_s5_c_im', 'grad_s5_d', 'grad_s5_w_glu', 'grad_s5_b_glu', 'grad_s5_w_out', 'grad_kv_norm', 'grad_kv_w', 'grad_kv_b_f', 'grad_fox_w_in', 'grad_fox_w_out', 'delta_norm_pre', 'delta_norm_post', 'delta_s5_w_in', 'delta_s5_a_re', 'delta_s5_a_im', 'delta_s5_log_dt', 'delta_s5_b_re', 'delta_s5_b_im', 'delta_s5_c_re', 'delta_s5_c_im', 'delta_s5_d', 'delta_s5_w_glu', 'delta_s5_b_glu', 'delta_s5_w_out', 'delta_kv_norm', 'delta_kv_w', 'delta_kv_b_f', 'delta_fox_w_in', 'delta_fox_w_out', 'new_m_norm_pre', 'new_m_norm_post', 'new_m_s5_w_in', 'new_m_s5_a_re', 'new_m_s5_a_im', 'new_m_s5_log_dt', 'new_m_s5_b_re', 'new_m_s5_b_im', 'new_m_s5_c_re', 'new_m_s5_c_im', 'new_m_s5_d', 'new_m_s5_w_glu', 'new_m_s5_b_glu', 'new_m_s5_w_out', 'new_m_kv_norm', 'new_m_kv_w', 'new_m_kv_b_f', 'new_m_fox_w_in', 'new_m_fox_w_out', 'new_v_norm_pre', 'new_v_norm_post', 'new_v_s5_w_in', 'new_v_s5_a_re', 'new_v_s5_a_im', 'new_v_s5_log_dt', 'new_v_s5_b_re', 'new_v_s5_b_im', 'new_v_s5_c_re', 'new_v_s5_c_im', 'new_v_s5_d', 'new_v_s5_w_glu', 'new_v_s5_b_glu', 'new_v_s5_w_out', 'new_v_kv_norm', 'new_v_kv_w', 'new_v_kv_b_f', 'new_v_fox_w_in', 'new_v_fox_w_out']
TWIN_LEAF_KINDS = {'loss': 'loss', 'grad_x': 'grad_x', 'grad_norm_pre': 'grad_w', 'grad_norm_post': 'grad_w', 'grad_s5_w_in': 'grad_w', 'grad_s5_a_re': 'grad_w', 'grad_s5_a_im': 'grad_w', 'grad_s5_log_dt': 'grad_w', 'grad_s5_b_re': 'grad_w', 'grad_s5_b_im': 'grad_w', 'grad_s5_c_re': 'grad_w', 'grad_s5_c_im': 'grad_w', 'grad_s5_d': 'grad_w', 'grad_s5_w_glu': 'grad_w', 'grad_s5_b_glu': 'grad_w', 'grad_s5_w_out': 'grad_w', 'grad_kv_norm': 'grad_w', 'grad_kv_w': 'grad_w', 'grad_kv_b_f': 'grad_w', 'grad_fox_w_in': 'grad_w', 'grad_fox_w_out': 'grad_w', 'delta_norm_pre': 'delta_w', 'delta_norm_post': 'delta_w', 'delta_s5_w_in': 'delta_w', 'delta_s5_a_re': 'delta_w', 'delta_s5_a_im': 'delta_w', 'delta_s5_log_dt': 'delta_w', 'delta_s5_b_re': 'delta_w', 'delta_s5_b_im': 'delta_w', 'delta_s5_c_re': 'delta_w', 'delta_s5_c_im': 'delta_w', 'delta_s5_d': 'delta_w', 'delta_s5_w_glu': 'delta_w', 'delta_s5_b_glu': 'delta_w', 'delta_s5_w_out': 'delta_w', 'delta_kv_norm': 'delta_w', 'delta_kv_w': 'delta_w', 'delta_kv_b_f': 'delta_w', 'delta_fox_w_in': 'delta_w', 'delta_fox_w_out': 'delta_w', 'new_m_norm_pre': 'new_m', 'new_m_norm_post': 'new_m', 'new_m_s5_w_in': 'new_m', 'new_m_s5_a_re': 'new_m', 'new_m_s5_a_im': 'new_m', 'new_m_s5_log_dt': 'new_m', 'new_m_s5_b_re': 'new_m', 'new_m_s5_b_im': 'new_m', 'new_m_s5_c_re': 'new_m', 'new_m_s5_c_im': 'new_m', 'new_m_s5_d': 'new_m', 'new_m_s5_w_glu': 'new_m', 'new_m_s5_b_glu': 'new_m', 'new_m_s5_w_out': 'new_m', 'new_m_kv_norm': 'new_m', 'new_m_kv_w': 'new_m', 'new_m_kv_b_f': 'new_m', 'new_m_fox_w_in': 'new_m', 'new_m_fox_w_out': 'new_m', 'new_v_norm_pre': 'new_v', 'new_v_norm_post': 'new_v', 'new_v_s5_w_in': 'new_v', 'new_v_s5_a_re': 'new_v', 'new_v_s5_a_im': 'new_v', 'new_v_s5_log_dt': 'new_v', 'new_v_s5_b_re': 'new_v', 'new_v_s5_b_im': 'new_v', 'new_v_s5_c_re': 'new_v', 'new_v_s5_c_im': 'new_v', 'new_v_s5_d': 'new_v', 'new_v_s5_w_glu': 'new_v', 'new_v_s5_b_glu': 'new_v', 'new_v_s5_w_out': 'new_v', 'new_v_kv_norm': 'new_v', 'new_v_kv_w': 'new_v', 'new_v_kv_b_f': 'new_v', 'new_v_fox_w_in': 'new_v', 'new_v_fox_w_out': 'new_v'}


def _forward(args):
    return _fwd_reference(*[args[k] for k in FWD_PARAMS])


def _output_shape():
    def fwd():
        inp = _fwd_setup_inputs(0)
        return _fwd_reference(*[inp[k] for k in FWD_PARAMS])
    out = _jax.eval_shape(fwd)
    return out.shape, out.dtype

N_MICROBATCH = 1
ADAM_LR = 0.001
ADAM_B1 = 0.9
ADAM_B2 = 0.999
ADAM_EPS = 1e-08
ADAM_WD = 0.01
ADAM_STEP = 10
PER_EXAMPLE_BATCH_AXIS = {'x': 0, 'loss_target': 0}
SHARED_INPUTS = []
_WEIGHT_DTYPES = {'norm_pre': _jnp.float32, 'norm_post': _jnp.float32, 's5_w_in': _jnp.float32, 's5_a_re': _jnp.float32, 's5_a_im': _jnp.float32, 's5_log_dt': _jnp.float32, 's5_b_re': _jnp.float32, 's5_b_im': _jnp.float32, 's5_c_re': _jnp.float32, 's5_c_im': _jnp.float32, 's5_d': _jnp.float32, 's5_w_glu': _jnp.float32, 's5_b_glu': _jnp.float32, 's5_w_out': _jnp.float32, 'kv_norm': _jnp.float32, 'kv_w': _jnp.float32, 'kv_b_f': _jnp.float32, 'fox_w_in': _jnp.float32, 'fox_w_out': _jnp.float32}
MOMENT_SCALE = {'norm_pre': 3.412151e-01, 'norm_post': 1.606636e+01, 's5_w_in': 2.742967e-01, 's5_a_re': 2.077014e-02, 's5_a_im': 1.855462e-02, 's5_log_dt': 1.344336e+01, 's5_b_re': 1.269061e-02, 's5_b_im': 1.250384e-02, 's5_c_re': 1.798920e-02, 's5_c_im': 1.799417e-02, 's5_d': 7.119334e-01, 's5_w_glu': 1.082278e-01, 's5_b_glu': 3.182902e-01, 's5_w_out': 7.052450e-01, 'kv_norm': 4.856481e-01, 'kv_w': 3.171474e-01, 'kv_b_f': 1.207827e+00, 'fox_w_in': 2.113790e-01, 'fox_w_out': 4.781310e-01}


def _to_microbatches(a, axis):
    t = _jnp.moveaxis(a, axis, 0)
    t = t.reshape((N_MICROBATCH, t.shape[0] // N_MICROBATCH) + t.shape[1:])
    return _jnp.moveaxis(t, 1, axis + 1)


def setup_inputs(seed: int = 0) -> dict:
    inp = _fwd_setup_inputs(seed)
    key = _jax.random.fold_in(_jax.random.key(seed), 7919)
    shape, _ = _output_shape()
    out = dict(inp)
    out["loss_target"] = _jax.random.normal(_jax.random.fold_in(key, 0), shape, _jnp.float32)
    for i, name in enumerate(TWIN_WEIGHTS):
        w = inp[name].astype(_jnp.float32)
        if MOMENT_SCALE is None:
            s = _jnp.sqrt(_jnp.mean(_jnp.square(w)) + 1e-30)
        else:
            s = MOMENT_SCALE[name]
        km, kv = _jax.random.split(_jax.random.fold_in(key, i + 1))
        out[name] = w
        out["m_" + name] = s * _jax.random.normal(km, w.shape, _jnp.float32)
        out["v_" + name] = (s * s) * _jax.random.uniform(kv, w.shape, _jnp.float32, 0.5, 1.5)
    if N_MICROBATCH > 1:
        for name, axis in PER_EXAMPLE_BATCH_AXIS.items():
            out[name] = _to_microbatches(out[name], axis)
    return {'x': out['x'], 'norm_pre': out['norm_pre'], 'norm_post': out['norm_post'], 's5_w_in': out['s5_w_in'], 's5_a_re': out['s5_a_re'], 's5_a_im': out['s5_a_im'], 's5_log_dt': out['s5_log_dt'], 's5_b_re': out['s5_b_re'], 's5_b_im': out['s5_b_im'], 's5_c_re': out['s5_c_re'], 's5_c_im': out['s5_c_im'], 's5_d': out['s5_d'], 's5_w_glu': out['s5_w_glu'], 's5_b_glu': out['s5_b_glu'], 's5_w_out': out['s5_w_out'], 'kv_norm': out['kv_norm'], 'kv_w': out['kv_w'], 'kv_b_f': out['kv_b_f'], 'fox_w_in': out['fox_w_in'], 'fox_w_out': out['fox_w_out'], 'loss_target': out['loss_target'], 'm_norm_pre': out['m_norm_pre'], 'm_norm_post': out['m_norm_post'], 'm_s5_w_in': out['m_s5_w_in'], 'm_s5_a_re': out['m_s5_a_re'], 'm_s5_a_im': out['m_s5_a_im'], 'm_s5_log_dt': out['m_s5_log_dt'], 'm_s5_b_re': out['m_s5_b_re'], 'm_s5_b_im': out['m_s5_b_im'], 'm_s5_c_re': out['m_s5_c_re'], 'm_s5_c_im': out['m_s5_c_im'], 'm_s5_d': out['m_s5_d'], 'm_s5_w_glu': out['m_s5_w_glu'], 'm_s5_b_glu': out['m_s5_b_glu'], 'm_s5_w_out': out['m_s5_w_out'], 'm_kv_norm': out['m_kv_norm'], 'm_kv_w': out['m_kv_w'], 'm_kv_b_f': out['m_kv_b_f'], 'm_fox_w_in': out['m_fox_w_in'], 'm_fox_w_out': out['m_fox_w_out'], 'v_norm_pre': out['v_norm_pre'], 'v_norm_post': out['v_norm_post'], 'v_s5_w_in': out['v_s5_w_in'], 'v_s5_a_re': out['v_s5_a_re'], 'v_s5_a_im': out['v_s5_a_im'], 'v_s5_log_dt': out['v_s5_log_dt'], 'v_s5_b_re': out['v_s5_b_re'], 'v_s5_b_im': out['v_s5_b_im'], 'v_s5_c_re': out['v_s5_c_re'], 'v_s5_c_im': out['v_s5_c_im'], 'v_s5_d': out['v_s5_d'], 'v_s5_w_glu': out['v_s5_w_glu'], 'v_s5_b_glu': out['v_s5_b_glu'], 'v_s5_w_out': out['v_s5_w_out'], 'v_kv_norm': out['v_kv_norm'], 'v_kv_w': out['v_kv_w'], 'v_kv_b_f': out['v_kv_b_f'], 'v_fox_w_in': out['v_fox_w_in'], 'v_fox_w_out': out['v_fox_w_out']}


def _loss(weights, diff, rest, loss_target):
    with _jax.named_scope("forward"):
        args = {**rest, TWIN_DIFF_INPUT: diff, **{k: w.astype(_WEIGHT_DTYPES[k]) for k, w in weights.items()}}
        y = _forward(args)
    with _jax.named_scope("loss_head"):
        err = _jnp.square(y.astype(_jnp.float32) - loss_target)
        return 0.5 * _jnp.sum(_jnp.mean(err, axis=-1)) if err.ndim else 0.5 * err


def _adamw(w, g, m, v):
    m = ADAM_B1 * m + (1.0 - ADAM_B1) * g
    v = ADAM_B2 * v + (1.0 - ADAM_B2) * _jnp.square(g)
    m_hat = m / (1.0 - ADAM_B1 ** ADAM_STEP)
    v_hat = v / (1.0 - ADAM_B2 ** ADAM_STEP)
    delta = -ADAM_LR * (m_hat / (_jnp.sqrt(v_hat) + ADAM_EPS) + ADAM_WD * w)
    return delta, m, v


def reference(x, norm_pre, norm_post, s5_w_in, s5_a_re, s5_a_im, s5_log_dt, s5_b_re, s5_b_im, s5_c_re, s5_c_im, s5_d, s5_w_glu, s5_b_glu, s5_w_out, kv_norm, kv_w, kv_b_f, fox_w_in, fox_w_out, loss_target, m_norm_pre, m_norm_post, m_s5_w_in, m_s5_a_re, m_s5_a_im, m_s5_log_dt, m_s5_b_re, m_s5_b_im, m_s5_c_re, m_s5_c_im, m_s5_d, m_s5_w_glu, m_s5_b_glu, m_s5_w_out, m_kv_norm, m_kv_w, m_kv_b_f, m_fox_w_in, m_fox_w_out, v_norm_pre, v_norm_post, v_s5_w_in, v_s5_a_re, v_s5_a_im, v_s5_log_dt, v_s5_b_re, v_s5_b_im, v_s5_c_re, v_s5_c_im, v_s5_d, v_s5_w_glu, v_s5_b_glu, v_s5_w_out, v_kv_norm, v_kv_w, v_kv_b_f, v_fox_w_in, v_fox_w_out):
    given = dict(x=x, norm_pre=norm_pre, norm_post=norm_post, s5_w_in=s5_w_in, s5_a_re=s5_a_re, s5_a_im=s5_a_im, s5_log_dt=s5_log_dt, s5_b_re=s5_b_re, s5_b_im=s5_b_im, s5_c_re=s5_c_re, s5_c_im=s5_c_im, s5_d=s5_d, s5_w_glu=s5_w_glu, s5_b_glu=s5_b_glu, s5_w_out=s5_w_out, kv_norm=kv_norm, kv_w=kv_w, kv_b_f=kv_b_f, fox_w_in=fox_w_in, fox_w_out=fox_w_out, loss_target=loss_target, m_norm_pre=m_norm_pre, m_norm_post=m_norm_post, m_s5_w_in=m_s5_w_in, m_s5_a_re=m_s5_a_re, m_s5_a_im=m_s5_a_im, m_s5_log_dt=m_s5_log_dt, m_s5_b_re=m_s5_b_re, m_s5_b_im=m_s5_b_im, m_s5_c_re=m_s5_c_re, m_s5_c_im=m_s5_c_im, m_s5_d=m_s5_d, m_s5_w_glu=m_s5_w_glu, m_s5_b_glu=m_s5_b_glu, m_s5_w_out=m_s5_w_out, m_kv_norm=m_kv_norm, m_kv_w=m_kv_w, m_kv_b_f=m_kv_b_f, m_fox_w_in=m_fox_w_in, m_fox_w_out=m_fox_w_out, v_norm_pre=v_norm_pre, v_norm_post=v_norm_post, v_s5_w_in=v_s5_w_in, v_s5_a_re=v_s5_a_re, v_s5_a_im=v_s5_a_im, v_s5_log_dt=v_s5_log_dt, v_s5_b_re=v_s5_b_re, v_s5_b_im=v_s5_b_im, v_s5_c_re=v_s5_c_re, v_s5_c_im=v_s5_c_im, v_s5_d=v_s5_d, v_s5_w_glu=v_s5_w_glu, v_s5_b_glu=v_s5_b_glu, v_s5_w_out=v_s5_w_out, v_kv_norm=v_kv_norm, v_kv_w=v_kv_w, v_kv_b_f=v_kv_b_f, v_fox_w_in=v_fox_w_in, v_fox_w_out=v_fox_w_out)
    weights = {n: given[n] for n in TWIN_WEIGHTS}
    shared = {n: given[n] for n in SHARED_INPUTS}
    per_example = {n: given[n] for n in ['x']}
    grad_fn = _jax.value_and_grad(_loss, argnums=(0, 1))

    def one_microbatch(ex, loss_target):
        ex = dict(ex)
        diff = ex.pop(TWIN_DIFF_INPUT)
        return grad_fn(weights, diff, {**shared, **ex}, loss_target)

    if N_MICROBATCH == 1:
        loss, (grad_w, grad_x) = one_microbatch(per_example, given["loss_target"])
    else:
        def body(carry, xs):
            loss_sum, grad_sum = carry
            l_k, (gw_k, gx_k) = one_microbatch(xs[0], xs[1])
            with _jax.named_scope("update"):
                return (loss_sum + l_k, _jax.tree.map(_jnp.add, grad_sum, gw_k)), gx_k

        init = (_jnp.zeros((), _jnp.float32), _jax.tree.map(_jnp.zeros_like, weights))
        (loss, grad_w), grad_x = _jax.lax.scan(body, init, (per_example, given["loss_target"]))
    with _jax.named_scope("update"):
        delta_w, new_m, new_v = {}, {}, {}
        for n in TWIN_WEIGHTS:
            delta_w[n], new_m[n], new_v[n] = _adamw(weights[n], grad_w[n], given["m_" + n], given["v_" + n])
    return (loss, grad_x, *[grad_w[n] for n in TWIN_WEIGHTS], *[delta_w[n] for n in TWIN_WEIGHTS],
            *[new_m[n] for n in TWIN_WEIGHTS], *[new_v[n] for n in TWIN_WEIGHTS])
```

```python
import functools
import math

import jax
import jax.numpy as jnp
from jax import lax
from jax.experimental import pallas as pl
from jax.experimental.pallas import tpu as pltpu

F32 = jnp.float32
BF16 = jnp.bfloat16

D_MODEL = 2048
SEQ = 4096
S5_GROUP = 16
S5_STATE = 64
HEAD_DIM = 128
RMS_EPS = 1e-6
NEG_INF = -1e30
ADAM_LR = 0.001
ADAM_B1 = 0.9
ADAM_B2 = 0.999
ADAM_EPS = 1e-08
ADAM_WD = 0.01
ADAM_STEP = 10

LANES = 128
SUBLANES = 8
VMEM_LIMIT = 56 * 1024 * 1024
N_CHIPS = 4
MESH_AXES = ("x", "y", "c")
MESH_ID = pl.DeviceIdType.MESH

SSM_CH = 128
ROW_TILE = 256
SCAN_ROWS = 512
ATT_TILE = 512
CUM_TILE = 512


def _pcall(body, **kw):
    return pl.pallas_call(body, **kw)


def _params(sem=None):
    if sem is None:
        return pltpu.CompilerParams(vmem_limit_bytes=VMEM_LIMIT)
    return pltpu.CompilerParams(vmem_limit_bytes=VMEM_LIMIT, dimension_semantics=sem)


def _sigmoid(x):
    return 1.0 / (1.0 + jnp.exp(-x))


def _silu(z):
    return z * _sigmoid(z)


def _dsilu(z):
    s = _sigmoid(z)
    return s * (1.0 + z * (1.0 - s))


_GELU_C = math.sqrt(2.0 / math.pi)


def _gelu(x):
    return 0.5 * x * (1.0 + jnp.tanh(_GELU_C * (x + 0.044715 * x * x * x)))


def _dgelu(x):
    t = jnp.tanh(_GELU_C * (x + 0.044715 * x * x * x))
    return 0.5 * (1.0 + t) + 0.5 * x * (1.0 - t * t) * _GELU_C * (1.0 + 3.0 * 0.044715 * x * x)


def _rstd(x):
    return lax.rsqrt(jnp.mean(x * x, axis=-1, keepdims=True) + RMS_EPS)


def _rms_bwd(x, g, dy):
    r = _rstd(x)
    dyg = dy * g
    dx = r * dyg - x * (r * r * r) * jnp.mean(dyg * x, axis=-1, keepdims=True)
    return dx, dy * (x * r)


def _colsum(v):
    return jnp.sum(v, axis=0, keepdims=True)


def _mm(a, b, *, ta=False, tb=False, out_dtype=F32, add=None, tm=1024, tn=1024, tk=512, name):
    k_dim, m_dim = (a.shape if ta else a.shape[::-1])
    n_dim = b.shape[0] if tb else b.shape[1]
    assert (b.shape[1] if tb else b.shape[0]) == k_dim, (a.shape, b.shape)
    tm, tn, tk = min(tm, m_dim), min(tn, n_dim), min(tk, k_dim)
    assert m_dim % tm == 0 and n_dim % tn == 0 and k_dim % tk == 0, (a.shape, b.shape, tm, tn, tk)
    nk = k_dim // tk
    dims = (((0 if ta else 1,), (1 if tb else 0,)), ((), ()))

    def body(*refs):
        if add is None:
            a_ref, b_ref, o_ref, acc = refs
        else:
            a_ref, b_ref, c_ref, o_ref, acc = refs
        k = pl.program_id(2)

        @pl.when(k == 0)
        def _():
            acc[...] = jnp.zeros_like(acc)

        acc[...] += lax.dot_general(a_ref[...].astype(BF16), b_ref[...].astype(BF16), dims,
                                    preferred_element_type=F32)

        @pl.when(k == nk - 1)
        def _():
            res = acc[...]
            if add is not None:
                res = res + c_ref[...]
            o_ref[...] = res.astype(out_dtype)

    a_spec = pl.BlockSpec((tk, tm), lambda i, j, k: (k, i)) if ta else pl.BlockSpec((tm, tk), lambda i, j, k: (i, k))
    b_spec = pl.BlockSpec((tn, tk), lambda i, j, k: (j, k)) if tb else pl.BlockSpec((tk, tn), lambda i, j, k: (k, j))
    o_spec = pl.BlockSpec((tm, tn), lambda i, j, k: (i, j))
    in_specs = [a_spec, b_spec] + ([o_spec] if add is not None else [])
    args = (a, b) + ((add,) if add is not None else ())
    return _pcall(
        body, name=name, grid=(m_dim // tm, n_dim // tn, nk), in_specs=in_specs, out_specs=o_spec,
        out_shape=jax.ShapeDtypeStruct((m_dim, n_dim), out_dtype),
        scratch_shapes=[pltpu.VMEM((tm, tn), F32)],
        compiler_params=_params(("parallel", "parallel", "arbitrary")),
    )(*args)


def _rows(fn, ins, outs, *, name, consts=(), accs=()):
    n_rows = ins[0][0].shape[0]
    tr = min(ROW_TILE, n_rows)
    assert n_rows % tr == 0
    n_in, n_c, n_out = len(ins), len(consts), len(outs)

    def body(*refs):
        vals = [r[...] for r in refs[:n_in + n_c]]
        res = fn(*vals)
        res = res if isinstance(res, (tuple, list)) else (res,)
        o_refs = refs[n_in + n_c:]
        for r, v in zip(o_refs[:n_out], res[:n_out]):
            r[...] = v.astype(r.dtype)
        if accs:
            first = pl.program_id(0) == 0
            for r, v in zip(o_refs[n_out:], res[n_out:]):
                @pl.when(first)
                def _(r=r, v=v):
                    r[...] = v

                @pl.when(jnp.logical_not(first))
                def _(r=r, v=v):
                    r[...] += v

    in_specs = [pl.BlockSpec((tr, w), functools.partial(lambda i, cb: (i, cb), cb=cb)) for _, w, cb in ins]
    in_specs += [pl.BlockSpec(c.shape, functools.partial(lambda i, nd: (0,) * nd, nd=c.ndim)) for c in consts]
    out_specs = [pl.BlockSpec((tr, w), lambda i: (i, 0)) for w, _ in outs]
    out_specs += [pl.BlockSpec(s, lambda i: (0, 0)) for s in accs]
    out_shape = [jax.ShapeDtypeStruct((n_rows, w), dt) for w, dt in outs]
    out_shape += [jax.ShapeDtypeStruct(s, F32) for s in accs]
    res = _pcall(
        body, name=name, grid=(n_rows // tr,), in_specs=in_specs, out_specs=out_specs, out_shape=out_shape,
        compiler_params=_params(("arbitrary",) if accs else ("parallel",)),
    )(*[a for a, _, _ in ins], *consts)
    return res


def _full(a):
    return (a, a.shape[1], 0)


def _cmul(ar, ai, br, bi):
    return ar * br - ai * bi, ar * bi + ai * br


def _seg_scan(re_ref, im_ref, a_re, a_im, seg, reverse):
    n_k = len(a_re)
    assert seg & (seg - 1) == 0
    ab = [(jnp.broadcast_to(a_re[k], (SUBLANES, LANES)), jnp.broadcast_to(a_im[k], (SUBLANES, LANES)))
          for k in range(n_k)]

    def slab(i):
        j = seg - 1 - i if reverse else i
        return pl.ds(j, SUBLANES, stride=seg)

    def local(i, carry):
        out = []
        for k in range(n_k):
            hr, hi = _cmul(ab[k][0], ab[k][1], carry[2 * k], carry[2 * k + 1])
            hr = hr + re_ref[k, slab(i), :]
            hi = hi + im_ref[k, slab(i), :]
            re_ref[k, slab(i), :] = hr
            im_ref[k, slab(i), :] = hi
            out += [hr, hi]
        return tuple(out)

    zero = jnp.zeros((SUBLANES, LANES), F32)
    end = lax.fori_loop(0, seg, local, (zero,) * (2 * n_k))

    row = lax.broadcasted_iota(jnp.int32, (SUBLANES, LANES), 0)
    edge = SUBLANES - 1 if reverse else 0
    shift = SUBLANES - 1 if reverse else 1
    enter = []
    for k in range(n_k):
        pr, pi = ab[k]
        for _ in range(seg.bit_length() - 1):
            pr, pi = _cmul(pr, pi, pr, pi)
        tr_, ti_ = zero, zero
        for _ in range(SUBLANES - 1):
            vr, vi = _cmul(pr, pi, tr_, ti_)
            tr_ = jnp.where(row == edge, 0.0, pltpu.roll(vr + end[2 * k], shift, 0))
            ti_ = jnp.where(row == edge, 0.0, pltpu.roll(vi + end[2 * k + 1], shift, 0))
        enter += [tr_, ti_]

    def fix(i, carry):
        out = []
        for k in range(n_k):
            er, ei = _cmul(ab[k][0], ab[k][1], carry[2 * k], carry[2 * k + 1])
            re_ref[k, slab(i), :] += er
            im_ref[k, slab(i), :] += ei
            out += [er, ei]
        return tuple(out)

    lax.fori_loop(0, seg, fix, tuple(enter))
    return enter


def _lane_blocks(v, n_k):
    return [v[:, k * LANES:(k + 1) * LANES] for k in range(n_k)]


def _gather_k(ref, rows, n_k):
    return jnp.concatenate([ref[k, rows, :] for k in range(n_k)], axis=1)


def _dot(a, b, dims=(((1,), (0,)), ((), ()))):
    return lax.dot_general(a.astype(BF16), b.astype(BF16), dims, preferred_element_type=F32)


_NT = (((1,), (1,)), ((), ()))
_TN = (((0,), (0,)), ((), ()))


def _ssm_fwd(uz, b_blk, c_blk, par, d_skip):
    seq = uz.shape[0]
    width = d_skip.shape[1]
    ns = SSM_CH // S5_GROUP * S5_STATE
    n_k = ns // LANES
    seg = seq // SUBLANES
    tb = min(SCAN_ROWS, seq)

    def body(u_ref, b_ref, c_ref, par_ref, d_ref, y_ref, hre, him):
        coef_r, coef_i = par_ref[0, 2:3, :], par_ref[0, 3:4, :]
        for c0 in range(0, seq, tb):
            rows = pl.ds(c0, tb)
            ub = u_ref[rows, :]
            bur, bui = _dot(ub, b_ref[0, 0]), _dot(ub, b_ref[0, 1])
            xr, xi = coef_r * bur - coef_i * bui, coef_r * bui + coef_i * bur
            for k in range(n_k):
                hre[k, rows, :] = xr[:, k * LANES:(k + 1) * LANES]
                him[k, rows, :] = xi[:, k * LANES:(k + 1) * LANES]
        _seg_scan(hre, him, _lane_blocks(par_ref[0, 0:1, :], n_k), _lane_blocks(par_ref[0, 1:2, :], n_k), seg, False)
        for c0 in range(0, seq, tb):
            rows = pl.ds(c0, tb)
            y = _dot(_gather_k(hre, rows, n_k), c_ref[0, 0]) - _dot(_gather_k(him, rows, n_k), c_ref[0, 1])
            y_ref[rows, :] = y + d_ref[...] * u_ref[rows, :]

    n_cb = width // SSM_CH
    return _pcall(
        body, name="ssm_fwd", grid=(n_cb,),
        in_specs=[pl.BlockSpec((seq, SSM_CH), lambda i: (0, i)),
                  pl.BlockSpec((1, 2, SSM_CH, ns), lambda i: (i, 0, 0, 0)),
                  pl.BlockSpec((1, 2, ns, SSM_CH), lambda i: (i, 0, 0, 0)),
                  pl.BlockSpec((1, 4, ns), lambda i: (i, 0, 0)),
                  pl.BlockSpec((1, SSM_CH), lambda i: (0, i))],
        out_specs=pl.BlockSpec((seq, SSM_CH), lambda i: (0, i)),
        out_shape=jax.ShapeDtypeStruct((seq, width), F32),
        scratch_shapes=[pltpu.VMEM((n_k, seq, LANES), F32), pltpu.VMEM((n_k, seq, LANES), F32)],
        compiler_params=_params(("parallel",)),
    )(uz, b_blk, c_blk, par, d_skip)


def _ssm_bwd(uz, dy1, ys, b_blk, ct_blk, par, d_skip):
    seq = uz.shape[0]
    width = d_skip.shape[1]
    ns_all = SSM_CH // S5_GROUP * S5_STATE
    n_half = 2
    ns = ns_all // n_half
    n_k = ns // LANES
    seg = seq // SUBLANES
    tb = min(SCAN_ROWS, seq)

    def body(u_ref, dy_ref, ys_ref, b_ref, ct_ref, par_ref, d_ref,
             du_ref, dbt_ref, dct_ref, dpar_ref, dd_ref, hre, him, gre, gim):
        half = pl.program_id(1)
        a_r, a_i = par_ref[0, 0:1, :], par_ref[0, 1:2, :]
        coef_r, coef_i = par_ref[0, 2:3, :], par_ref[0, 3:4, :]

        def dys_of(rows):
            return dy_ref[rows, :] * _dgelu(ys_ref[rows, :])

        for c0 in range(0, seq, tb):
            rows = pl.ds(c0, tb)
            ub = u_ref[rows, :]
            bur, bui = _dot(ub, b_ref[0, 0]), _dot(ub, b_ref[0, 1])
            xr, xi = coef_r * bur - coef_i * bui, coef_r * bui + coef_i * bur
            dys = dys_of(rows)
            gr, gi = _dot(dys, ct_ref[0, 0]), -_dot(dys, ct_ref[0, 1])
            for k in range(n_k):
                lanes = slice(k * LANES, (k + 1) * LANES)
                hre[k, rows, :] = xr[:, lanes]
                him[k, rows, :] = xi[:, lanes]
                gre[k, rows, :] = gr[:, lanes]
                gim[k, rows, :] = gi[:, lanes]
        enter = _seg_scan(hre, him, _lane_blocks(a_r, n_k), _lane_blocks(a_i, n_k), seg, False)
        _seg_scan(gre, gim, _lane_blocks(a_r, n_k), _lane_blocks(-a_i, n_k), seg, True)

        def corr(j, carry):
            acc, prev = carry
            acc_o, prev_o = [], []
            for k in range(n_k):
                sl = pl.ds(j, SUBLANES, stride=seg)
                g_r, g_i = gre[k, sl, :], gim[k, sl, :]
                p_r, p_i = prev[2 * k], prev[2 * k + 1]
                acc_o += [acc[2 * k] + g_r * p_r + g_i * p_i, acc[2 * k + 1] + g_i * p_r - g_r * p_i]
                prev_o += [hre[k, sl, :], him[k, sl, :]]
            return tuple(acc_o), tuple(prev_o)

        zero = jnp.zeros((SUBLANES, LANES), F32)
        acc, _ = lax.fori_loop(0, seg, corr, ((zero,) * (2 * n_k), tuple(enter)))
        da_r = jnp.concatenate([_colsum(acc[2 * k]) for k in range(n_k)], axis=1)
        da_i = jnp.concatenate([_colsum(acc[2 * k + 1]) for k in range(n_k)], axis=1)

        zeros_cn = jnp.zeros((SSM_CH, ns), F32)
        qt_r, qt_i, dct_r, dct_i = zeros_cn, zeros_cn, zeros_cn, zeros_cn
        dd = jnp.zeros((1, SSM_CH), F32)
        first = half == 0
        for c0 in range(0, seq, tb):
            rows = pl.ds(c0, tb)
            ub = u_ref[rows, :]
            dys = dys_of(rows)
            dct_r = dct_r + _dot(dys, _gather_k(hre, rows, n_k), _TN)
            dct_i = dct_i - _dot(dys, _gather_k(him, rows, n_k), _TN)
            g_r, g_i = _gather_k(gre, rows, n_k), _gather_k(gim, rows, n_k)
            qt_r = qt_r + _dot(ub, g_r, _TN)
            qt_i = qt_i + _dot(ub, g_i, _TN)
            dbu_r, dbu_i = coef_r * g_r + coef_i * g_i, coef_r * g_i - coef_i * g_r
            du = _dot(dbu_r, b_ref[0, 0], _NT) + _dot(dbu_i, b_ref[0, 1], _NT)
            dd = dd + _colsum(dys * ub)

            @pl.when(first)
            def _(du=du, dys=dys, rows=rows):
                du_ref[rows, :] = du + d_ref[...] * dys

            @pl.when(jnp.logical_not(first))
            def _(du=du, rows=rows):
                du_ref[rows, :] += du

        @pl.when(first)
        def _():
            dd_ref[...] = dd

        b_r, b_i = b_ref[0, 0], b_ref[0, 1]
        dbt_ref[0, 0] = coef_r * qt_r + coef_i * qt_i
        dbt_ref[0, 1] = coef_r * qt_i - coef_i * qt_r
        dct_ref[0, 0] = dct_r
        dct_ref[0, 1] = dct_i
        dpar_ref[0, 0:1, :] = da_r
        dpar_ref[0, 1:2, :] = da_i
        dpar_ref[0, 2:3, :] = _colsum(b_r * qt_r + b_i * qt_i)
        dpar_ref[0, 3:4, :] = _colsum(b_r * qt_i - b_i * qt_r)

    n_cb = width // SSM_CH
    blk = lambda i, h: (0, i)
    return _pcall(
        body, name="ssm_bwd", grid=(n_cb, n_half),
        in_specs=[pl.BlockSpec((seq, SSM_CH), blk), pl.BlockSpec((seq, SSM_CH), blk), pl.BlockSpec((seq, SSM_CH), blk),
                  pl.BlockSpec((1, 2, SSM_CH, ns), lambda i, h: (i, 0, 0, h)),
                  pl.BlockSpec((1, 2, SSM_CH, ns), lambda i, h: (i, 0, 0, h)),
                  pl.BlockSpec((1, 4, ns), lambda i, h: (i, 0, h)),
                  pl.BlockSpec((1, SSM_CH), blk)],
        out_specs=[pl.BlockSpec((seq, SSM_CH), blk),
                   pl.BlockSpec((1, 2, SSM_CH, ns), lambda i, h: (i, 0, 0, h)),
                   pl.BlockSpec((1, 2, SSM_CH, ns), lambda i, h: (i, 0, 0, h)),
                   pl.BlockSpec((1, 4, ns), lambda i, h: (i, 0, h)),
                   pl.BlockSpec((1, SSM_CH), blk)],
        out_shape=[jax.ShapeDtypeStruct((seq, width), F32),
                   jax.ShapeDtypeStruct((n_cb, 2, SSM_CH, ns_all), F32),
                   jax.ShapeDtypeStruct((n_cb, 2, SSM_CH, ns_all), F32),
                   jax.ShapeDtypeStruct((n_cb, 4, ns_all), F32),
                   jax.ShapeDtypeStruct((1, width), F32)],
        scratch_shapes=[pltpu.VMEM((n_k, seq, LANES), F32) for _ in range(4)],
        compiler_params=_params(("parallel", "arbitrary")),
    )(uz, dy1, ys, b_blk, ct_blk, par, d_skip)


def _ssm_discretize(a_re, a_im, log_dt):
    dt = jnp.exp(log_dt)[:, None]
    mag = jnp.exp(a_re * dt)
    abar_re = mag * jnp.cos(a_im * dt)
    abar_im = mag * jnp.sin(a_im * dt)
    den = a_re * a_re + a_im * a_im
    nr = abar_re - 1.0
    coef_re = (nr * a_re + abar_im * a_im) / den
    coef_im = (abar_im * a_re - nr * a_im) / den
    return abar_re, abar_im, coef_re, coef_im


def _block_diag(w_gcp):
    gpb = SSM_CH // S5_GROUP
    n_cb = w_gcp.shape[0] // gpb
    w = w_gcp.reshape(n_cb, gpb, S5_GROUP, 1, S5_STATE)
    eye = jnp.eye(gpb, dtype=w.dtype)[None, :, None, :, None]
    return (w * eye).reshape(n_cb, SSM_CH, gpb * S5_STATE)


def _block_diag_extract(w_blk):
    gpb = SSM_CH // S5_GROUP
    n_cb = w_blk.shape[0]
    w = w_blk.reshape(n_cb, gpb, S5_GROUP, gpb, S5_STATE)
    w = jnp.moveaxis(jnp.diagonal(w, axis1=1, axis2=3), -1, 1)
    return w.reshape(n_cb * gpb, S5_GROUP, S5_STATE)


def _split3(x):
    hi = x.astype(BF16)
    mid = (x - hi.astype(F32)).astype(BF16)
    lo = (x - hi.astype(F32) - mid.astype(F32)).astype(BF16)
    return hi, mid, lo


def _tri_sum(tri, x):
    hi, mid, lo = _split3(x)
    return (jnp.dot(tri, hi, preferred_element_type=F32) + jnp.dot(tri, mid, preferred_element_type=F32)
            + jnp.dot(tri, lo, preferred_element_type=F32))


def _log_sigmoid(x):
    return jnp.minimum(x, 0.0) - jnp.log(1.0 + jnp.exp(-jnp.abs(x)))


def _cum_fwd(fl, b_f):
    seq = fl.shape[0]
    t = min(CUM_TILE, seq)

    def body(fl_ref, b_ref, o_ref, carry):
        @pl.when(pl.program_id(0) == 0)
        def _():
            carry[...] = jnp.zeros_like(carry)

        r = lax.broadcasted_iota(jnp.int32, (t, t), 0)
        c = lax.broadcasted_iota(jnp.int32, (t, t), 1)
        tri = (c <= r).astype(BF16)
        cum = _tri_sum(tri, _log_sigmoid(fl_ref[...] + b_ref[...])) + carry[...]
        o_ref[...] = cum
        carry[...] = cum[t - 1:t, :]

    return _pcall(
        body, name="cum_fwd", grid=(seq // t,),
        in_specs=[pl.BlockSpec((t, LANES), lambda i: (i, 0)), pl.BlockSpec((1, LANES), lambda i: (0, 0))],
        out_specs=pl.BlockSpec((t, LANES), lambda i: (i, 0)),
        out_shape=jax.ShapeDtypeStruct((seq, LANES), F32),
        scratch_shapes=[pltpu.VMEM((1, LANES), F32)],
        compiler_params=_params(("arbitrary",)),
    )(fl, b_f)


def _cum_bwd(dcum, fl, b_f):
    seq = fl.shape[0]
    t = min(CUM_TILE, seq)
    nb = seq // t

    def body(dc_ref, fl_ref, b_ref, o_ref, db_ref, carry):
        @pl.when(pl.program_id(0) == 0)
        def _():
            carry[...] = jnp.zeros_like(carry)
            db_ref[...] = jnp.zeros_like(db_ref)

        r = lax.broadcasted_iota(jnp.int32, (t, t), 0)
        c = lax.broadcasted_iota(jnp.int32, (t, t), 1)
        tri = (c >= r).astype(BF16)
        rev = _tri_sum(tri, dc_ref[...]) + carry[...]
        carry[...] = rev[0:1, :]
        dfl = rev * _sigmoid(-(fl_ref[...] + b_ref[...]))
        o_ref[...] = dfl
        db_ref[...] += _colsum(dfl)

    return _pcall(
        body, name="cum_bwd", grid=(nb,),
        in_specs=[pl.BlockSpec((t, LANES), lambda i: (nb - 1 - i, 0)), pl.BlockSpec((t, LANES), lambda i: (nb - 1 - i, 0)),
                  pl.BlockSpec((1, LANES), lambda i: (0, 0))],
        out_specs=[pl.BlockSpec((t, LANES), lambda i: (nb - 1 - i, 0)), pl.BlockSpec((1, LANES), lambda i: (0, 0))],
        out_shape=[jax.ShapeDtypeStruct((seq, LANES), F32), jax.ShapeDtypeStruct((1, LANES), F32)],
        scratch_shapes=[pltpu.VMEM((1, LANES), F32)],
        compiler_params=_params(("arbitrary",)),
    )(dcum, fl, b_f)


def _att_scores(q, kb, cq, ck, scale, row0, col0, masked):
    s = _dot(q, kb, _NT) * scale + cq - ck
    if masked:
        rows = row0 + lax.broadcasted_iota(jnp.int32, s.shape, 0)
        cols = col0 + lax.broadcasted_iota(jnp.int32, s.shape, 1)
        s = jnp.where(cols <= rows, s, NEG_INF)
    return s


def _att_fwd(qz, kv, cq, ck):
    seq = qz.shape[0]
    heads = cq.shape[0]
    t = min(ATT_TILE, seq)
    scale = HEAD_DIM ** -0.5

    def body(q_ref, k_ref, v_ref, cq_ref, ck_ref, o_ref, lse_ref):
        i = pl.program_id(1)
        q = q_ref[...].astype(BF16)
        cqv = cq_ref[0]

        def block(j, carry, masked):
            m, l, acc = carry
            rows = pl.ds(pl.multiple_of(j * t, t), t)
            s = _att_scores(q, k_ref[rows, :], cqv, ck_ref[0, j], scale, i * t, j * t, masked)
            m_new = jnp.maximum(m, jnp.max(s, axis=1, keepdims=True))
            p = jnp.exp(s - m_new)
            alpha = jnp.exp(m - m_new)
            return m_new, alpha * l + jnp.sum(p, axis=1, keepdims=True), alpha * acc + _dot(p, v_ref[rows, :])

        init = (jnp.full((t, 1), NEG_INF, F32), jnp.zeros((t, 1), F32), jnp.zeros((t, HEAD_DIM), F32))
        carry = lax.fori_loop(0, i, functools.partial(block, masked=False), init)
        m, l, acc = block(i, carry, True)
        o_ref[...] = acc / l
        lse_ref[0] = m + jnp.log(l)

    return _pcall(
        body, name="att_fwd", grid=(heads, seq // t),
        in_specs=[pl.BlockSpec((t, HEAD_DIM), lambda h, i: (i, h)),
                  pl.BlockSpec((seq, HEAD_DIM), lambda h, i: (0, h)),
                  pl.BlockSpec((seq, HEAD_DIM), lambda h, i: (0, heads + h)),
                  pl.BlockSpec((1, t, 1), lambda h, i: (h, i, 0)),
                  pl.BlockSpec((1, seq // t, 1, t), lambda h, i: (h, 0, 0, 0))],
        out_specs=[pl.BlockSpec((t, HEAD_DIM), lambda h, i: (i, h)), pl.BlockSpec((1, t, 1), lambda h, i: (h, i, 0))],
        out_shape=[jax.ShapeDtypeStruct((seq, heads * HEAD_DIM), F32), jax.ShapeDtypeStruct((heads, seq, 1), F32)],
        compiler_params=_params(("parallel", "parallel")),
    )(qz, kv, kv, cq, ck)


def _att_bwd_q(qz, kv, do, o, lse, cq, ck):
    seq = qz.shape[0]
    heads = cq.shape[0]
    t = min(ATT_TILE, seq)
    scale = HEAD_DIM ** -0.5

    def body(q_ref, k_ref, v_ref, do_ref, o_ref, lse_ref, cq_ref, ck_ref, dq_ref, delta_ref):
        i = pl.program_id(1)
        q = q_ref[...].astype(BF16)
        dob = do_ref[...].astype(BF16)
        delta = jnp.sum(do_ref[...] * o_ref[...], axis=1, keepdims=True)
        cqv, lse_v = cq_ref[0], lse_ref[0]

        def block(j, carry, masked):
            dq, pdp = carry
            rows = pl.ds(pl.multiple_of(j * t, t), t)
            kb = k_ref[rows, :]
            s = _att_scores(q, kb, cqv, ck_ref[0, j], scale, i * t, j * t, masked)
            p = jnp.exp(s - lse_v)
            dp = _dot(dob, v_ref[rows, :], _NT)
            ds = p * (dp - delta)
            return dq + _dot(ds * scale, kb), pdp + jnp.sum(p * dp, axis=1, keepdims=True)

        init = (jnp.zeros((t, HEAD_DIM), F32), jnp.zeros((t, 1), F32))
        carry = lax.fori_loop(0, i, functools.partial(block, masked=False), init)
        dq, pdp = block(i, carry, True)
        dq_ref[...] = dq.astype(dq_ref.dtype)
        delta_ref[0] = pdp

    qblk = pl.BlockSpec((t, HEAD_DIM), lambda h, i: (i, h))
    col = pl.BlockSpec((1, t, 1), lambda h, i: (h, i, 0))
    return _pcall(
        body, name="att_bwd_q", grid=(heads, seq // t),
        in_specs=[qblk, pl.BlockSpec((seq, HEAD_DIM), lambda h, i: (0, h)),
                  pl.BlockSpec((seq, HEAD_DIM), lambda h, i: (0, heads + h)), qblk, qblk, col, col,
                  pl.BlockSpec((1, seq // t, 1, t), lambda h, i: (h, 0, 0, 0))],
        out_specs=[qblk, col],
        out_shape=[jax.ShapeDtypeStruct((seq, heads * HEAD_DIM), BF16), jax.ShapeDtypeStruct((heads, seq, 1), F32)],
        compiler_params=_params(("parallel", "parallel")),
    )(qz, kv, kv, do, o, lse, cq, ck)


def _att_bwd_kv(qz, kv, do, lse, delta, cq, ck):
    seq = qz.shape[0]
    heads = cq.shape[0]
    t = min(ATT_TILE, seq)
    nq = seq // t
    scale = HEAD_DIM ** -0.5

    def body(q_ref, k_ref, v_ref, do_ref, lse_ref, delta_ref, cq_ref, ck_ref, dk_ref, dv_ref, dck_ref):
        j = pl.program_id(1)
        kb, vb = k_ref[...], v_ref[...]
        ckv = ck_ref[0, 0]

        def block(i, carry, masked):
            dk, dv, dck = carry
            rows = pl.ds(pl.multiple_of(i * t, t), t)
            qb = q_ref[rows, :].astype(BF16)
            dob = do_ref[rows, :].astype(BF16)
            s = _att_scores(qb, kb, cq_ref[0, rows, :], ckv, scale, i * t, j * t, masked)
            p = jnp.exp(s - lse_ref[0, rows, :])
            ds = p * (_dot(dob, vb, _NT) - delta_ref[0, rows, :])
            return dk + _dot(ds * scale, qb, _TN), dv + _dot(p, dob, _TN), dck - _colsum(ds)

        init = (jnp.zeros((t, HEAD_DIM), F32), jnp.zeros((t, HEAD_DIM), F32), jnp.zeros((1, t), F32))
        carry = block(j, init, True)
        dk, dv, dck = lax.fori_loop(j + 1, nq, functools.partial(block, masked=False), carry)
        dk_ref[...] = dk.astype(dk_ref.dtype)
        dv_ref[...] = dv.astype(dv_ref.dtype)
        dck_ref[0, 0] = dck

    head = pl.BlockSpec((seq, HEAD_DIM), lambda h, j: (0, h))
    col = pl.BlockSpec((1, seq, 1), lambda h, j: (h, 0, 0))
    kblk = pl.BlockSpec((t, HEAD_DIM), lambda h, j: (j, h))
    row = pl.BlockSpec((1, 1, 1, t), lambda h, j: (h, j, 0, 0))
    return _pcall(
        body, name="att_bwd_kv", grid=(heads, nq),
        in_specs=[head, kblk, pl.BlockSpec((t, HEAD_DIM), lambda h, j: (j, heads + h)), head, col, col, col, row],
        out_specs=[kblk, kblk, row],
        out_shape=[jax.ShapeDtypeStruct((seq, heads * HEAD_DIM), BF16), jax.ShapeDtypeStruct((seq, heads * HEAD_DIM), BF16),
                   jax.ShapeDtypeStruct((heads, nq, 1, t), F32)],
        compiler_params=_params(("parallel", "parallel")),
    )(qz, kv, kv, do, lse, delta, cq, ck)


ANY = pl.BlockSpec(memory_space=pl.ANY)


def _mesh_pos():
    return lax.axis_index("x"), lax.axis_index("y"), lax.axis_index("c")


def _other_chips(x, y):
    return [(1 - x, y), (x, 1 - y), (1 - x, 1 - y)]


def _all_gather_weights(big, small):
    nb, ns = len(big), len(small)
    n_remote = 3 * (nb + ns)

    def body(*refs):
        ins, outs = refs[:nb + ns], refs[nb + ns:2 * (nb + ns)]
        send_sems, recv_sems, fwd_send, fwd_recv, local_sems = refs[2 * (nb + ns):]
        x, y, c = _mesh_pos()
        me = 2 * x + y
        chips = _other_chips(x, y)
        sibling = (x, y, 1 - c)

        def half(ref, hc):
            rh = ref.shape[-2] // 2
            return ref.at[pl.ds(hc * rh, rh), :]

        local = [pltpu.make_async_copy(ins[i], outs[i].at[me], local_sems.at[i]) for i in range(nb + ns)]
        for cp in local:
            cp.start()

        def remote(i, j, src_chip, to, from_in):
            if i < nb:
                src = half(ins[i], c) if from_in else half(outs[i].at[src_chip], c)
                dst = half(outs[i].at[src_chip], c)
            else:
                src = ins[i] if from_in else outs[i].at[src_chip]
                dst = outs[i].at[src_chip]
            k = 3 * i + j
            return pltpu.make_async_remote_copy(src_ref=src, dst_ref=dst, send_sem=send_sems.at[k],
                                                recv_sem=recv_sems.at[k], device_id=to, device_id_type=MESH_ID)

        def forward(i, j, src_chip, hc):
            part = half(outs[i].at[src_chip], hc)
            k = 3 * i + j
            return pltpu.make_async_remote_copy(src_ref=part, dst_ref=part, send_sem=fwd_send.at[k],
                                                recv_sem=fwd_recv.at[k], device_id=sibling, device_id_type=MESH_ID)

        sends = [remote(i, j, me, (*chips[j], c), True) for i in range(nb + ns) for j in range(3)]
        for cp in sends:
            cp.start()
        fwds = []
        for i in range(nb + ns):
            for j in range(3):
                src_chip = 2 * chips[j][0] + chips[j][1]
                remote(i, j, src_chip, (*chips[j], c), False).wait_recv()
                if i < nb:
                    cp = forward(i, j, src_chip, c)
                    cp.start()
                    fwds.append(cp)
        for i in range(nb):
            for j in range(3):
                src_chip = 2 * chips[j][0] + chips[j][1]
                forward(i, j, src_chip, 1 - c).wait_recv()
        for cp in sends + fwds:
            cp.wait_send()
        for cp in local:
            cp.wait()

    arrays = list(big) + list(small)
    return _pcall(
        body, name="all_gather_weights",
        in_specs=[ANY] * (nb + ns), out_specs=[ANY] * (nb + ns),
        out_shape=[jax.ShapeDtypeStruct((N_CHIPS,) + a.shape, a.dtype) for a in arrays],
        scratch_shapes=[pltpu.SemaphoreType.DMA((n_remote,)), pltpu.SemaphoreType.DMA((n_remote,)),
                        pltpu.SemaphoreType.DMA((3 * nb,)), pltpu.SemaphoreType.DMA((3 * nb,)),
                        pltpu.SemaphoreType.DMA((nb + ns,))],
        compiler_params=pltpu.CompilerParams(has_side_effects=True),
    )(*arrays)


def _swap_halves(grads):
    n = len(grads)

    def body(*refs):
        ins, outs, send_sems, recv_sems = refs[:n], refs[n:2 * n], refs[2 * n], refs[2 * n + 1]
        x, y, c = _mesh_pos()
        cps = []
        for i in range(n):
            rh = ins[i].shape[1] // 2
            cps.append(pltpu.make_async_remote_copy(
                src_ref=ins[i].at[:, pl.ds((1 - c) * rh, rh), :], dst_ref=outs[i], send_sem=send_sems.at[i],
                recv_sem=recv_sems.at[i], device_id=(x, y, 1 - c), device_id_type=MESH_ID))
        for cp in cps:
            cp.start()
        for cp in cps:
            cp.wait()

    return _pcall(
        body, name="grad_swap_halves", in_specs=[ANY] * n, out_specs=[ANY] * n,
        out_shape=[jax.ShapeDtypeStruct((g.shape[0], g.shape[1] // 2, g.shape[2]), g.dtype) for g in grads],
        scratch_shapes=[pltpu.SemaphoreType.DMA((n,)), pltpu.SemaphoreType.DMA((n,))],
        compiler_params=pltpu.CompilerParams(has_side_effects=True),
    )(*grads)


def _scatter_to_owner(parts):
    n = len(parts)

    def body(*refs):
        ins, outs, send_sems, recv_sems = refs[:n], refs[n:2 * n], refs[2 * n], refs[2 * n + 1]
        x, y, c = _mesh_pos()
        chips = _other_chips(x, y)
        cps = []
        for i in range(n):
            for j in range(3):
                k = 3 * i + j
                cps.append(pltpu.make_async_remote_copy(
                    src_ref=ins[i].at[2 * chips[j][0] + chips[j][1]], dst_ref=outs[i].at[j],
                    send_sem=send_sems.at[k], recv_sem=recv_sems.at[k], device_id=(*chips[j], c),
                    device_id_type=MESH_ID))
        for cp in cps:
            cp.start()
        for cp in cps:
            cp.wait()

    return _pcall(
        body, name="grad_scatter", in_specs=[ANY] * n, out_specs=[ANY] * n,
        out_shape=[jax.ShapeDtypeStruct((3,) + p.shape[1:], p.dtype) for p in parts],
        scratch_shapes=[pltpu.SemaphoreType.DMA((3 * n,)), pltpu.SemaphoreType.DMA((3 * n,))],
        compiler_params=pltpu.CompilerParams(has_side_effects=True),
    )(*parts)


def _join_halves(halves):
    n = len(halves)

    def body(*refs):
        ins, outs, send_sems, recv_sems, local_sems = refs[:n], refs[n:2 * n], refs[2 * n], refs[2 * n + 1], refs[2 * n + 2]
        x, y, c = _mesh_pos()
        cps, loc = [], []
        for i in range(n):
            rh = ins[i].shape[0]
            mine = outs[i].at[pl.ds(c * rh, rh), :]
            loc.append(pltpu.make_async_copy(ins[i], mine, local_sems.at[i]))
            cps.append(pltpu.make_async_remote_copy(
                src_ref=ins[i], dst_ref=mine, send_sem=send_sems.at[i], recv_sem=recv_sems.at[i],
                device_id=(x, y, 1 - c), device_id_type=MESH_ID))
        for cp in loc + cps:
            cp.start()
        for cp in cps:
            cp.wait_send()
        for i in range(n):
            rh = ins[i].shape[0]
            theirs = outs[i].at[pl.ds((1 - c) * rh, rh), :]
            pltpu.make_async_remote_copy(src_ref=ins[i], dst_ref=theirs, send_sem=send_sems.at[i],
                                         recv_sem=recv_sems.at[i], device_id=(x, y, 1 - c),
                                         device_id_type=MESH_ID).wait_recv()
        for cp in loc:
            cp.wait()

    return _pcall(
        body, name="grad_join_halves", in_specs=[ANY] * n, out_specs=[ANY] * n,
        out_shape=[jax.ShapeDtypeStruct((2 * h.shape[0], h.shape[1]), h.dtype) for h in halves],
        scratch_shapes=[pltpu.SemaphoreType.DMA((n,)), pltpu.SemaphoreType.DMA((n,)), pltpu.SemaphoreType.DMA((n,))],
        compiler_params=pltpu.CompilerParams(has_side_effects=True),
    )(*halves)


def _all_reduce_small(v):
    def body(v_ref, o_ref, recv, send_sems, recv_sems):
        x, y, c = _mesh_pos()
        peers = [(x, y, 1 - c), (1 - x, y, c), (x, 1 - y, c)]
        for s, peer in enumerate(peers):
            src = v_ref if s == 0 else o_ref
            cp = pltpu.make_async_remote_copy(src_ref=src, dst_ref=recv.at[s], send_sem=send_sems.at[s],
                                              recv_sem=recv_sems.at[s], device_id=peer, device_id_type=MESH_ID)
            cp.start()
            cp.wait()
            o_ref[...] = src[...] + recv[s]

    vm = pl.BlockSpec(memory_space=pltpu.VMEM)
    return _pcall(
        body, name="all_reduce_small", in_specs=[vm], out_specs=vm,
        out_shape=jax.ShapeDtypeStruct(v.shape, v.dtype),
        scratch_shapes=[pltpu.VMEM((3,) + v.shape, v.dtype), pltpu.SemaphoreType.DMA((3,)), pltpu.SemaphoreType.DMA((3,))],
        compiler_params=pltpu.CompilerParams(vmem_limit_bytes=VMEM_LIMIT, has_side_effects=True),
    )(v)


def _adamw_math(w, g, m, v):
    m = ADAM_B1 * m + (1.0 - ADAM_B1) * g
    v = ADAM_B2 * v + (1.0 - ADAM_B2) * (g * g)
    m_hat = m / (1.0 - ADAM_B1 ** ADAM_STEP)
    v_hat = v / (1.0 - ADAM_B2 ** ADAM_STEP)
    delta = -ADAM_LR * (m_hat / (jnp.sqrt(v_hat) + ADAM_EPS) + ADAM_WD * w)
    return delta, m, v


def _adamw(w, g, m, v, name):
    wd = w.shape[1]
    return _rows(_adamw_math, [_full(w), _full(g), _full(m), _full(v)], [(wd, F32)] * 3, name=name)


def _pack(arrs, rows):
    flat = jnp.concatenate([a.reshape(-1) for a in arrs])
    return jnp.pad(flat, (0, rows * LANES - flat.shape[0])).reshape(rows, LANES)


def _unpack(buf, like):
    flat = buf.reshape(-1)
    out, off = [], 0
    for a in like:
        out.append(flat[off:off + a.size].reshape(a.shape))
        off += a.size
    return out


def kernel(x, norm_pre, norm_post, s5_w_in, s5_a_re, s5_a_im, s5_log_dt, s5_b_re, s5_b_im, s5_c_re, s5_c_im, s5_d, s5_w_glu, s5_b_glu, s5_w_out, kv_norm, kv_w, kv_b_f, fox_w_in, fox_w_out, loss_target, m_norm_pre, m_norm_post, m_s5_w_in, m_s5_a_re, m_s5_a_im, m_s5_log_dt, m_s5_b_re, m_s5_b_im, m_s5_c_re, m_s5_c_im, m_s5_d, m_s5_w_glu, m_s5_b_glu, m_s5_w_out, m_kv_norm, m_kv_w, m_kv_b_f, m_fox_w_in, m_fox_w_out, v_norm_pre, v_norm_post, v_s5_w_in, v_s5_a_re, v_s5_a_im, v_s5_log_dt, v_s5_b_re, v_s5_b_im, v_s5_c_re, v_s5_c_im, v_s5_d, v_s5_w_glu, v_s5_b_glu, v_s5_w_out, v_kv_norm, v_kv_w, v_kv_b_f, v_fox_w_in, v_fox_w_out):
    seq, dm = x.shape[1], x.shape[2]
    width = dm
    heads = dm // HEAD_DIM
    fw = heads * HEAD_DIM
    groups = width // S5_GROUP
    chip = 2 * lax.axis_index("x") + lax.axis_index("y")

    big_shards = [s5_w_in[0], s5_w_glu[0], s5_w_out[0], kv_w, fox_w_in[0], fox_w_out[0]]
    gathered = _all_gather_weights([w.astype(BF16) for w in big_shards], [s5_d, s5_b_glu])
    g_win, g_wglu, g_wout, g_kvw, g_fwin, g_fwout, g_d, g_bglu = gathered
    cols = lambda g: jnp.moveaxis(g, 0, 1).reshape(g.shape[1], -1)
    rows = lambda g: g.reshape(-1, g.shape[2])
    w_in, w_glu, w_out = cols(g_win), rows(g_wglu), rows(g_wout)
    kvw_full = cols(g_kvw)
    w_kv = kvw_full[:, :2 * fw]
    w_f = jnp.pad(kvw_full[:, 2 * fw:], ((0, 0), (0, LANES - heads)))
    fw_in, fw_out = cols(g_fwin), rows(g_fwout)
    d_skip, b_glu = cols(g_d), cols(g_bglu)
    b_f = jnp.pad(kv_b_f, (0, LANES - heads)).reshape(1, LANES)

    h0 = x[0]
    target = loss_target[0]
    g_pre0, g_pre1 = norm_pre[0:1], norm_pre[1:2]
    g_post0, g_post1 = norm_post[0:1], norm_post[1:2]
    g_kv = kv_norm.reshape(1, dm)

    a_re, a_im, log_dt = s5_a_re[0], s5_a_im[0], s5_log_dt[0]
    disc, disc_vjp = jax.vjp(_ssm_discretize, a_re, a_im, log_dt)
    gpb = SSM_CH // S5_GROUP
    n_cb = groups // gpb
    par = jnp.stack([p.reshape(n_cb, gpb * S5_STATE) for p in disc], axis=1)
    b_t = lambda b: jnp.swapaxes(b, 1, 2)
    b_blk = jnp.stack([_block_diag(b_t(s5_b_re[0])), _block_diag(b_t(s5_b_im[0]))], axis=1)
    ct_blk = jnp.stack([_block_diag(s5_c_re[0]), _block_diag(s5_c_im[0])], axis=1)
    c_blk = jnp.swapaxes(ct_blk, 2, 3)

    xn1 = _rows(lambda h, g: (h * _rstd(h) * g,), [_full(h0)], [(dm, BF16)], consts=[g_pre0], name="norm_pre0")[0]
    uz = _mm(xn1, w_in, name="s5_in")
    ys = _ssm_fwd(uz, b_blk, c_blk, par, d_skip)
    y1b = _rows(lambda y: (_gelu(y),), [_full(ys)], [(width, BF16)], name="gelu")[0]
    glu_a = _mm(y1b, w_glu, name="s5_glu")

    def gate_fn(y, a, z, b):
        return (_gelu(y) * _sigmoid(a + b) * _silu(z),)

    y3b = _rows(gate_fn, [_full(ys), _full(glu_a), (uz, width, 1)], [(width, BF16)], consts=[b_glu], name="s5_gate")[0]
    o1 = _mm(y3b, w_out, name="s5_out")

    def mid_fn(h, o, gp, gk, gq):
        h1 = h + o * _rstd(o) * gp
        r = _rstd(h1)
        return h1, h1 * r * gk, h1 * r * gq

    h1, xk, xn2 = _rows(mid_fn, [_full(h0), _full(o1)], [(dm, F32), (dm, BF16), (dm, BF16)],
                        consts=[g_post0, g_kv, g_pre1], name="mid_norms")

    kv = _mm(xk, w_kv, out_dtype=BF16, name="kv_proj")
    fl = _mm(xk, w_f, name="f_proj")
    qz = _mm(xn2, fw_in, name="fox_in")
    cum = _cum_fwd(fl, b_f)
    t_att = min(ATT_TILE, seq)
    cum_t = cum[:, :heads].T
    cq = cum_t.reshape(heads, seq, 1)
    ck = cum_t.reshape(heads, seq // t_att, 1, t_att)
    o, lse = _att_fwd(qz, kv, cq, ck)
    o2b = _rows(lambda a, z: (a * _silu(z),), [_full(o), (qz, fw, 1)], [(fw, BF16)], name="fox_gate")[0]
    o3 = _mm(o2b, fw_out, name="fox_out")

    def loss_fn(h, o, t, g):
        r = _rstd(o)
        err = h + o * r * g - t
        dh = err * (1.0 / dm)
        do, dg = _rms_bwd(o, g, dh)
        part = 0.5 * jnp.sum(jnp.mean(err * err, axis=-1, keepdims=True), axis=0, keepdims=True)
        return dh, do, jnp.broadcast_to(part, (1, LANES)), _colsum(dg)

    dh2, do3, loss_part, dg_post1 = _rows(loss_fn, [_full(h1), _full(o3), _full(target)], [(dm, F32), (dm, BF16)],
                                          consts=[g_post1], accs=[(1, LANES), (1, dm)], name="loss_head")
    loss = lax.psum(loss_part[0, 0], MESH_AXES)

    do2 = _mm(do3, fw_out, tb=True, name="fox_out_dx")
    dw_fout = _mm(o2b, do3, ta=True, name="fox_out_dw")

    def fox_gate_bwd(d, a, z):
        return d * _silu(z), d * a * _dsilu(z)

    do, dz2 = _rows(fox_gate_bwd, [_full(do2), _full(o), (qz, fw, 1)], [(fw, F32), (fw, BF16)], name="fox_gate_bwd")
    dq, delta = _att_bwd_q(qz, kv, do, o, lse, cq, ck)
    dk, dv, dck = _att_bwd_kv(qz, kv, do, lse, delta, cq, ck)
    dcum = jnp.pad(dck.reshape(heads, seq).T, ((0, 0), (0, LANES - heads)))
    dfl, db_f = _cum_bwd(dcum, fl, b_f)
    dqz = jnp.concatenate([dq, dz2], axis=1)
    dkv = jnp.concatenate([dk, dv], axis=1)
    dxn2 = _mm(dqz, fw_in, tb=True, name="fox_in_dx")
    dw_fin = _mm(xn2, dqz, ta=True, name="fox_in_dw")
    dxk_f = _mm(dfl, w_f, tb=True, name="f_proj_dx")
    dxk = _mm(dkv, w_kv, tb=True, add=dxk_f, name="kv_proj_dx")
    dw_kv = _mm(xk, dkv, ta=True, name="kv_proj_dw")
    dw_f = _mm(xk, dfl, ta=True, name="f_proj_dw")

    def mid_bwd(d2, h, dq_, dk_, o, gq, gk, gp):
        dxa, dga = _rms_bwd(h, gq, dq_)
        dxb, dgb = _rms_bwd(h, gk, dk_)
        dh = d2 + dxa + dxb
        do_, dgp = _rms_bwd(o, gp, dh)
        return dh, do_, _colsum(dga), _colsum(dgb), _colsum(dgp)

    dh1, do1, dg_pre1, dg_kv, dg_post0 = _rows(
        mid_bwd, [_full(dh2), _full(h1), _full(dxn2), _full(dxk), _full(o1)], [(dm, F32), (dm, BF16)],
        consts=[g_pre1, g_kv, g_post0], accs=[(1, dm)] * 3, name="mid_norms_bwd")

    dy3 = _mm(do1, w_out, tb=True, name="s5_out_dx")
    dw_out = _mm(y3b, do1, ta=True, name="s5_out_dw")

    def gate_bwd(d3, y, a, z, b):
        y1 = _gelu(y)
        gate = _sigmoid(a + b)
        dy2 = d3 * _silu(z)
        da = dy2 * y1 * gate * (1.0 - gate)
        return dy2 * gate, da, d3 * (y1 * gate) * _dsilu(z), _colsum(da)

    dy1_direct, da, dz, db_glu = _rows(gate_bwd, [_full(dy3), _full(ys), _full(glu_a), (uz, width, 1)],
                                       [(width, F32), (width, BF16), (width, BF16)], consts=[b_glu],
                                       accs=[(1, width)], name="s5_gate_bwd")
    dy1 = _mm(da, w_glu, tb=True, add=dy1_direct, name="s5_glu_dx")
    dw_glu = _mm(y1b, da, ta=True, name="s5_glu_dw")
    du, dbt_blk, dct_blk, dpar, dd = _ssm_bwd(uz, dy1, ys, b_blk, ct_blk, par, d_skip)
    duz = jnp.concatenate([du.astype(BF16), dz], axis=1)
    dxn1 = _mm(duz, w_in, tb=True, name="s5_in_dx")
    dw_in = _mm(xn1, duz, ta=True, name="s5_in_dw")

    def first_bwd(d1, h, dxn, g):
        dx, dg = _rms_bwd(h, g, dxn)
        return d1 + dx, _colsum(dg)

    grad_x, dg_pre0 = _rows(first_bwd, [_full(dh1), _full(h0), _full(dxn1)], [(dm, F32)], consts=[g_pre0],
                            accs=[(1, dm)], name="norm_pre0_bwd")

    dpar_g = [dpar[:, i, :].reshape(groups, S5_STATE) for i in range(4)]
    da_re, da_im, dlog_dt = disc_vjp(tuple(dpar_g))
    db_re = jnp.swapaxes(_block_diag_extract(dbt_blk[:, 0]), 1, 2)
    db_im = jnp.swapaxes(_block_diag_extract(dbt_blk[:, 1]), 1, 2)
    dc_re = _block_diag_extract(dct_blk[:, 0])
    dc_im = _block_diag_extract(dct_blk[:, 1])

    small_local = [jnp.concatenate([dg_pre0, dg_pre1]), jnp.concatenate([dg_post0, dg_post1]),
                   da_re[None], da_im[None], dlog_dt[None], db_re[None], db_im[None], dc_re[None], dc_im[None],
                   dd, db_glu, dg_kv.reshape(dm), db_f[0, :heads]]
    n_small = sum(a.size for a in small_local)
    small_rows = -(-n_small // (LANES * ROW_TILE)) * ROW_TILE
    small_sum = _unpack(_all_reduce_small(_pack(small_local, small_rows)), small_local)
    (g_norm_pre, g_norm_post, g_a_re, g_a_im, g_log_dt, g_b_re, g_b_im, g_c_re, g_c_im, g_d_full, g_bglu_full,
     g_kv_norm, g_b_f) = small_sum
    shard = width // N_CHIPS
    g_d_own = lax.dynamic_slice(g_d_full, (0, chip * shard), (1, shard))
    g_bglu_own = lax.dynamic_slice(g_bglu_full, (0, chip * shard), (1, shard))

    to_cols = lambda g: jnp.moveaxis(g.reshape(g.shape[0], N_CHIPS, -1), 1, 0)
    to_rows = lambda g: g.reshape(N_CHIPS, -1, g.shape[1])
    dw_kv_full = jnp.concatenate([dw_kv, dw_f[:, :heads]], axis=1)
    big_grads = [to_cols(dw_in), to_rows(dw_glu), to_rows(dw_out), to_cols(dw_kv_full), to_cols(dw_fin), to_rows(dw_fout)]
    theirs = _swap_halves(big_grads)
    c_idx = lax.axis_index("c")
    chip_sums = []
    for i, (g, t) in enumerate(zip(big_grads, theirs)):
        rh = g.shape[1] // 2
        mine = lax.dynamic_slice_in_dim(g, c_idx * rh, rh, axis=1)
        s = _rows(lambda a, b: (a + b,), [_full(mine.reshape(-1, g.shape[2])), _full(t.reshape(-1, g.shape[2]))],
                  [(g.shape[2], F32)], name=f"grad_pair_sum_{i}")[0]
        chip_sums.append(s.reshape(N_CHIPS, rh, g.shape[2]))
    received = _scatter_to_owner(chip_sums)
    halves = []
    for i, (s, r) in enumerate(zip(chip_sums, received)):
        own = lax.dynamic_index_in_dim(s, chip, axis=0, keepdims=False)
        halves.append(_rows(lambda a, b, c_, d_: (((a + b) + c_) + d_,), [_full(own), _full(r[0]), _full(r[1]), _full(r[2])],
                            [(s.shape[2], F32)], name=f"grad_chip_sum_{i}")[0])
    g_win_s, g_wglu_s, g_wout_s, g_kvw_s, g_fwin_s, g_fwout_s = _join_halves(halves)

    big_w = big_shards
    big_g = [g_win_s, g_wglu_s, g_wout_s, g_kvw_s, g_fwin_s, g_fwout_s]
    big_m = [m_s5_w_in[0], m_s5_w_glu[0], m_s5_w_out[0], m_kv_w, m_fox_w_in[0], m_fox_w_out[0]]
    big_v = [v_s5_w_in[0], v_s5_w_glu[0], v_s5_w_out[0], v_kv_w, v_fox_w_in[0], v_fox_w_out[0]]
    big_upd = [_adamw(w, g, m, v, f"adamw_{i}") for i, (w, g, m, v) in enumerate(zip(big_w, big_g, big_m, big_v))]

    small_names = ["norm_pre", "norm_post", "s5_a_re", "s5_a_im", "s5_log_dt", "s5_b_re", "s5_b_im", "s5_c_re", "s5_c_im",
                   "s5_d", "s5_b_glu", "kv_norm", "kv_b_f"]
    small_w = [norm_pre, norm_post, s5_a_re, s5_a_im, s5_log_dt, s5_b_re, s5_b_im, s5_c_re, s5_c_im, s5_d, s5_b_glu, kv_norm, kv_b_f]
    small_m = [m_norm_pre, m_norm_post, m_s5_a_re, m_s5_a_im, m_s5_log_dt, m_s5_b_re, m_s5_b_im, m_s5_c_re, m_s5_c_im, m_s5_d, m_s5_b_glu, m_kv_norm, m_kv_b_f]
    small_v = [v_norm_pre, v_norm_post, v_s5_a_re, v_s5_a_im, v_s5_log_dt, v_s5_b_re, v_s5_b_im, v_s5_c_re, v_s5_c_im, v_s5_d, v_s5_b_glu, v_kv_norm, v_kv_b_f]
    small_g = [g_norm_pre, g_norm_post, g_a_re, g_a_im, g_log_dt, g_b_re, g_b_im, g_c_re, g_c_im, g_d_own, g_bglu_own, g_kv_norm, g_b_f]
    small_g = [g.reshape(w.shape) for g, w in zip(small_g, small_w)]
    n_own = sum(a.size for a in small_w)
    own_rows = -(-n_own // (LANES * ROW_TILE)) * ROW_TILE
    pv = _pack(small_v, own_rows)
    pv = jnp.where(jnp.arange(own_rows * LANES).reshape(own_rows, LANES) < n_own, pv, 1.0)
    sd, sm, sv = _adamw(_pack(small_w, own_rows), _pack(small_g, own_rows), _pack(small_m, own_rows), pv, "adamw_small")
    small_delta, small_newm, small_newv = _unpack(sd, small_w), _unpack(sm, small_w), _unpack(sv, small_w)

    order = ["norm_pre", "norm_post", "s5_w_in", "s5_a_re", "s5_a_im", "s5_log_dt", "s5_b_re", "s5_b_im", "s5_c_re", "s5_c_im",
             "s5_d", "s5_w_glu", "s5_b_glu", "s5_w_out", "kv_norm", "kv_w", "kv_b_f", "fox_w_in", "fox_w_out"]
    big_names = ["s5_w_in", "s5_w_glu", "s5_w_out", "kv_w", "fox_w_in", "fox_w_out"]
    big_like = [s5_w_in, s5_w_glu, s5_w_out, kv_w, fox_w_in, fox_w_out]
    grads, deltas, new_m, new_v = {}, {}, {}, {}
    for i, n in enumerate(big_names):
        shp = big_like[i].shape
        grads[n] = big_g[i].reshape(shp)
        deltas[n], new_m[n], new_v[n] = (a.reshape(shp) for a in big_upd[i])
    for i, n in enumerate(small_names):
        grads[n], deltas[n], new_m[n], new_v[n] = small_g[i], small_delta[i], small_newm[i], small_newv[i]

    return (loss, grad_x[None], *[grads[n] for n in order], *[deltas[n] for n in order],
            *[new_m[n] for n in order], *[new_v[n] for n in order])
```

```python
import functools
import math

import jax
import jax.numpy as jnp
from jax import lax
from jax.experimental import pallas as pl
from jax.experimental.pallas import tpu as pltpu

F32 = jnp.float32
BF16 = jnp.bfloat16

D_MODEL = 2048
SEQ = 4096
S5_GROUP = 16
S5_STATE = 64
HEAD_DIM = 128
RMS_EPS = 1e-6
NEG_INF = -1e30
ADAM_LR = 0.001
ADAM_B1 = 0.9
ADAM_B2 = 0.999
ADAM_EPS = 1e-08
ADAM_WD = 0.01
ADAM_STEP = 10

LANES = 128
SUBLANES = 8
VMEM_LIMIT = 56 * 1024 * 1024
N_CHIPS = 4
MESH_AXES = ("x", "y", "c")
MESH_ID = pl.DeviceIdType.MESH

SSM_CH = 128
ROW_TILE = 256
SCAN_ROWS = 512
SCAN_UNROLL = 4
ATT_TILE = 512
CUM_TILE = 512


def _pcall(body, **kw):
    return pl.pallas_call(body, **kw)


def _params(sem=None):
    if sem is None:
        return pltpu.CompilerParams(vmem_limit_bytes=VMEM_LIMIT)
    return pltpu.CompilerParams(vmem_limit_bytes=VMEM_LIMIT, dimension_semantics=sem)


def _sigmoid(x):
    return 1.0 / (1.0 + jnp.exp(-x))


def _silu(z):
    return z * _sigmoid(z)


def _dsilu(z):
    s = _sigmoid(z)
    return s * (1.0 + z * (1.0 - s))


_GELU_C = math.sqrt(2.0 / math.pi)


def _gelu(x):
    return 0.5 * x * (1.0 + jnp.tanh(_GELU_C * (x + 0.044715 * x * x * x)))


def _dgelu(x):
    t = jnp.tanh(_GELU_C * (x + 0.044715 * x * x * x))
    return 0.5 * (1.0 + t) + 0.5 * x * (1.0 - t * t) * _GELU_C * (1.0 + 3.0 * 0.044715 * x * x)


def _rstd(x):
    return lax.rsqrt(jnp.mean(x * x, axis=-1, keepdims=True) + RMS_EPS)


def _rms_bwd(x, g, dy):
    r = _rstd(x)
    dyg = dy * g
    dx = r * dyg - x * (r * r * r) * jnp.mean(dyg * x, axis=-1, keepdims=True)
    return dx, dy * (x * r)


def _colsum(v):
    return jnp.sum(v, axis=0, keepdims=True)


def _mm(a, b, *, ta=False, tb=False, out_dtype=F32, add=None, out_split=1, tm=1024, tn=1024, tk=512, name):
    def describe(op):
        if isinstance(op, (tuple, list)):
            assert all(p.ndim == 2 and p.shape == op[0].shape for p in op)
            return list(op), op[0].shape[0], op[0].shape[1], False
        if op.ndim == 3:
            return [op], op.shape[1], op.shape[2], True
        return [op], op.shape[0], op.shape[1], False

    a_parts, a_rows, a_pc, a_stack = describe(a)
    b_parts, b_rows, b_pc, b_stack = describe(b)
    a_cols = a_pc * (a.shape[0] if a_stack else len(a_parts))
    b_cols = b_pc * (b.shape[0] if b_stack else len(b_parts))
    k_dim, m_dim = (a_rows, a_cols) if ta else (a_cols, a_rows)
    n_dim, kb = (b_rows, b_cols) if tb else (b_cols, b_rows)
    assert kb == k_dim, (k_dim, kb)
    tm = min(tm, a_pc) if ta else min(tm, m_dim)
    tk = min(tk, k_dim, k_dim if ta else a_pc, b_pc if tb else k_dim)
    tn = min(tn, n_dim // out_split, n_dim if tb else b_pc)
    a_ct, b_ct = (tm if ta else tk), (tk if tb else tn)
    assert m_dim % tm == 0 and n_dim % tn == 0 and k_dim % tk == 0 and a_pc % a_ct == 0 and b_pc % b_ct == 0
    assert (n_dim // out_split) % tn == 0
    nk = k_dim // tk
    dims = (((0 if ta else 1,), (1 if tb else 0,)), ((), ()))
    n_a, n_b = len(a_parts), len(b_parts)
    assert n_a == 1 or n_b == 1

    def operand_specs(parts, stack, rows_t, cols_t, per, row_of, col_of):
        specs = []
        for p in range(len(parts)):
            def col(i, j, k, p=p):
                return jnp.clip(col_of(i, j, k) - p * per, 0, per - 1) if len(parts) > 1 else col_of(i, j, k)
            if stack:
                specs.append(pl.BlockSpec((None, rows_t, cols_t),
                                          lambda i, j, k, col=col: (col(i, j, k) // per, row_of(i, j, k), col(i, j, k) % per)))
            else:
                specs.append(pl.BlockSpec((rows_t, cols_t), lambda i, j, k, col=col: (row_of(i, j, k), col(i, j, k))))
        return specs

    if ta:
        a_specs = operand_specs(a_parts, a_stack, tk, tm, a_pc // tm, lambda i, j, k: k, lambda i, j, k: i)
    else:
        a_specs = operand_specs(a_parts, a_stack, tm, tk, a_pc // tk, lambda i, j, k: i, lambda i, j, k: k)
    if tb:
        b_specs = operand_specs(b_parts, b_stack, tn, tk, b_pc // tk, lambda i, j, k: j, lambda i, j, k: k)
    else:
        b_specs = operand_specs(b_parts, b_stack, tk, tn, b_pc // tn, lambda i, j, k: k, lambda i, j, k: j)

    def body(*refs):
        a_refs, b_refs = refs[:n_a], refs[n_a:n_a + n_b]
        rest = refs[n_a + n_b:]
        if add is None:
            o_ref, acc = rest
        else:
            c_ref, o_ref, acc = rest
        i, j, k = pl.program_id(0), pl.program_id(1), pl.program_id(2)

        @pl.when(k == 0)
        def _():
            acc[...] = jnp.zeros_like(acc)

        def accumulate(a_ref, b_ref):
            acc[...] += lax.dot_general(a_ref[...].astype(BF16), b_ref[...].astype(BF16), dims,
                                        preferred_element_type=F32)

        if n_a == 1 and n_b == 1:
            accumulate(a_refs[0], b_refs[0])
        else:
            many, block, per = (a_refs, (i if ta else k), a_pc // a_ct) if n_a > 1 else (b_refs, (k if tb else j), b_pc // b_ct)
            for p, ref in enumerate(many):
                @pl.when(block // per == p)
                def _(ref=ref):
                    accumulate(ref, b_refs[0]) if n_a > 1 else accumulate(a_refs[0], ref)

        @pl.when(k == nk - 1)
        def _():
            res = acc[...]
            if add is not None:
                res = res + c_ref[...]
            o_ref[...] = res.astype(out_dtype)

    per_out = n_dim // out_split // tn
    if out_split > 1:
        o_spec = pl.BlockSpec((None, tm, tn), lambda i, j, k: (j // per_out, i, j % per_out))
        out_shape = jax.ShapeDtypeStruct((out_split, m_dim, n_dim // out_split), out_dtype)
    else:
        o_spec = pl.BlockSpec((tm, tn), lambda i, j, k: (i, j))
        out_shape = jax.ShapeDtypeStruct((m_dim, n_dim), out_dtype)
    in_specs = a_specs + b_specs + ([pl.BlockSpec((tm, tn), lambda i, j, k: (i, j))] if add is not None else [])
    args = tuple(a_parts) + tuple(b_parts) + ((add,) if add is not None else ())
    return _pcall(
        body, name=name, grid=(m_dim // tm, n_dim // tn, nk), in_specs=in_specs, out_specs=o_spec,
        out_shape=out_shape, scratch_shapes=[pltpu.VMEM((tm, tn), F32)],
        compiler_params=_params(("parallel", "parallel", "arbitrary")),
    )(*args)


def _rows(fn, ins, outs, *, name, consts=(), accs=()):
    n_rows = ins[0][0].shape[0]
    tr = min(ROW_TILE, n_rows)
    assert n_rows % tr == 0
    n_in, n_c, n_out = len(ins), len(consts), len(outs)

    def body(*refs):
        vals = [r[...] for r in refs[:n_in + n_c]]
        res = fn(*vals)
        res = res if isinstance(res, (tuple, list)) else (res,)
        o_refs = refs[n_in + n_c:]
        for r, v in zip(o_refs[:n_out], res[:n_out]):
            r[...] = v.astype(r.dtype)
        if accs:
            first = pl.program_id(0) == 0
            for r, v in zip(o_refs[n_out:], res[n_out:]):
                @pl.when(first)
                def _(r=r, v=v):
                    r[...] = v

                @pl.when(jnp.logical_not(first))
                def _(r=r, v=v):
                    r[...] += v

    in_specs = [pl.BlockSpec((tr, w), functools.partial(lambda i, cb: (i, cb), cb=cb)) for _, w, cb in ins]
    in_specs += [pl.BlockSpec(c.shape, functools.partial(lambda i, nd: (0,) * nd, nd=c.ndim)) for c in consts]
    out_specs = [pl.BlockSpec((tr, w), lambda i: (i, 0)) for w, _ in outs]
    out_specs += [pl.BlockSpec(s, lambda i: (0, 0)) for s in accs]
    out_shape = [jax.ShapeDtypeStruct((n_rows, w), dt) for w, dt in outs]
    out_shape += [jax.ShapeDtypeStruct(s, F32) for s in accs]
    res = _pcall(
        body, name=name, grid=(n_rows // tr,), in_specs=in_specs, out_specs=out_specs, out_shape=out_shape,
        compiler_params=_params(("arbitrary",) if accs else ("parallel",)),
    )(*[a for a, _, _ in ins], *consts)
    return res


def _full(a):
    return (a, a.shape[1], 0)


def _cmul(ar, ai, br, bi):
    return ar * br - ai * bi, ar * bi + ai * br


def _seg_scan(re_ref, im_ref, a_re, a_im, seg, reverse):
    n_k = len(a_re)
    assert seg & (seg - 1) == 0
    ab = [(jnp.broadcast_to(a_re[k], (SUBLANES, LANES)), jnp.broadcast_to(a_im[k], (SUBLANES, LANES)))
          for k in range(n_k)]

    def slab(i):
        j = seg - 1 - i if reverse else i
        return pl.ds(pl.multiple_of(j * SUBLANES, SUBLANES), SUBLANES)

    def local(i, carry):
        out = []
        for k in range(n_k):
            hr, hi = _cmul(ab[k][0], ab[k][1], carry[2 * k], carry[2 * k + 1])
            hr = hr + re_ref[k, slab(i), :]
            hi = hi + im_ref[k, slab(i), :]
            re_ref[k, slab(i), :] = hr
            im_ref[k, slab(i), :] = hi
            out += [hr, hi]
        return tuple(out)

    zero = jnp.zeros((SUBLANES, LANES), F32)
    end = lax.fori_loop(0, seg, local, (zero,) * (2 * n_k), unroll=SCAN_UNROLL)

    row = lax.broadcasted_iota(jnp.int32, (SUBLANES, LANES), 0)
    edge = SUBLANES - 1 if reverse else 0
    shift = SUBLANES - 1 if reverse else 1
    enter = []
    for k in range(n_k):
        pr, pi = ab[k]
        for _ in range(seg.bit_length() - 1):
            pr, pi = _cmul(pr, pi, pr, pi)
        tr_, ti_ = zero, zero
        for _ in range(SUBLANES - 1):
            vr, vi = _cmul(pr, pi, tr_, ti_)
            tr_ = jnp.where(row == edge, 0.0, pltpu.roll(vr + end[2 * k], shift, 0))
            ti_ = jnp.where(row == edge, 0.0, pltpu.roll(vi + end[2 * k + 1], shift, 0))
        enter += [tr_, ti_]

    def fix(i, carry):
        out = []
        for k in range(n_k):
            er, ei = _cmul(ab[k][0], ab[k][1], carry[2 * k], carry[2 * k + 1])
            re_ref[k, slab(i), :] += er
            im_ref[k, slab(i), :] += ei
            out += [er, ei]
        return tuple(out)

    lax.fori_loop(0, seg, fix, tuple(enter), unroll=SCAN_UNROLL)
    return enter


def _to_slab(a):
    s, w = a.shape
    return a.reshape(SUBLANES, s // SUBLANES, w).swapaxes(0, 1).reshape(s, w)


def _from_slab(a):
    s, w = a.shape
    return a.reshape(s // SUBLANES, SUBLANES, w).swapaxes(0, 1).reshape(s, w)


def _lane_blocks(v, n_k):
    return [v[:, k * LANES:(k + 1) * LANES] for k in range(n_k)]


def _gather_k(ref, rows, n_k):
    return jnp.concatenate([ref[k, rows, :] for k in range(n_k)], axis=1)


def _dot(a, b, dims=(((1,), (0,)), ((), ()))):
    return lax.dot_general(a.astype(BF16), b.astype(BF16), dims, preferred_element_type=F32)


_NT = (((1,), (1,)), ((), ()))
_TN = (((0,), (0,)), ((), ()))


def _ssm_fwd(uz, b_blk, c_blk, par, d_skip):
    seq = uz.shape[0]
    width = d_skip.shape[1]
    ns = SSM_CH // S5_GROUP * S5_STATE
    n_k = ns // LANES
    seg = seq // SUBLANES
    tb = min(SCAN_ROWS, seq)

    def body(u_ref, b_ref, c_ref, par_ref, d_ref, y_ref, hre, him):
        coef_r, coef_i = par_ref[0, 2:3, :], par_ref[0, 3:4, :]
        for c0 in range(0, seq, tb):
            rows = pl.ds(c0, tb)
            ub = u_ref[rows, :]
            bur, bui = _dot(ub, b_ref[0, 0]), _dot(ub, b_ref[0, 1])
            xr, xi = coef_r * bur - coef_i * bui, coef_r * bui + coef_i * bur
            for k in range(n_k):
                hre[k, rows, :] = xr[:, k * LANES:(k + 1) * LANES]
                him[k, rows, :] = xi[:, k * LANES:(k + 1) * LANES]
        _seg_scan(hre, him, _lane_blocks(par_ref[0, 0:1, :], n_k), _lane_blocks(par_ref[0, 1:2, :], n_k), seg, False)
        for c0 in range(0, seq, tb):
            rows = pl.ds(c0, tb)
            y = _dot(_gather_k(hre, rows, n_k), c_ref[0, 0]) - _dot(_gather_k(him, rows, n_k), c_ref[0, 1])
            y_ref[rows, :] = y + d_ref[...] * u_ref[rows, :]

    n_cb = width // SSM_CH
    return _pcall(
        body, name="ssm_fwd", grid=(n_cb,),
        in_specs=[pl.BlockSpec((seq, SSM_CH), lambda i: (0, i)),
                  pl.BlockSpec((1, 2, SSM_CH, ns), lambda i: (i, 0, 0, 0)),
                  pl.BlockSpec((1, 2, ns, SSM_CH), lambda i: (i, 0, 0, 0)),
                  pl.BlockSpec((1, 4, ns), lambda i: (i, 0, 0)),
                  pl.BlockSpec((1, SSM_CH), lambda i: (0, i))],
        out_specs=pl.BlockSpec((seq, SSM_CH), lambda i: (0, i)),
        out_shape=jax.ShapeDtypeStruct((seq, width), F32),
        scratch_shapes=[pltpu.VMEM((n_k, seq, LANES), F32), pltpu.VMEM((n_k, seq, LANES), F32)],
        compiler_params=_params(("parallel",)),
    )(uz, b_blk, c_blk, par, d_skip)


def _ssm_bwd(uz, dy1, ys, b_blk, ct_blk, par, d_skip):
    seq = uz.shape[0]
    width = d_skip.shape[1]
    ns_all = SSM_CH // S5_GROUP * S5_STATE
    n_half = 2
    ns = ns_all // n_half
    n_k = ns // LANES
    seg = seq // SUBLANES
    tb = min(SCAN_ROWS, seq)

    def body(u_ref, dy_ref, ys_ref, b_ref, ct_ref, par_ref, d_ref,
             du_ref, dbt_ref, dct_ref, dpar_ref, dd_ref, hre, him, gre, gim):
        half = pl.program_id(1)
        a_r, a_i = par_ref[0, 0:1, :], par_ref[0, 1:2, :]
        coef_r, coef_i = par_ref[0, 2:3, :], par_ref[0, 3:4, :]

        def dys_of(rows):
            return dy_ref[rows, :] * _dgelu(ys_ref[rows, :])

        for c0 in range(0, seq, tb):
            rows = pl.ds(c0, tb)
            ub = u_ref[rows, :]
            bur, bui = _dot(ub, b_ref[0, 0]), _dot(ub, b_ref[0, 1])
            xr, xi = coef_r * bur - coef_i * bui, coef_r * bui + coef_i * bur
            dys = dys_of(rows)
            gr, gi = _dot(dys, ct_ref[0, 0]), -_dot(dys, ct_ref[0, 1])
            for k in range(n_k):
                lanes = slice(k * LANES, (k + 1) * LANES)
                hre[k, rows, :] = xr[:, lanes]
                him[k, rows, :] = xi[:, lanes]
                gre[k, rows, :] = gr[:, lanes]
                gim[k, rows, :] = gi[:, lanes]
        enter = _seg_scan(hre, him, _lane_blocks(a_r, n_k), _lane_blocks(a_i, n_k), seg, False)
        _seg_scan(gre, gim, _lane_blocks(a_r, n_k), _lane_blocks(-a_i, n_k), seg, True)

        def corr(j, carry):
            acc, prev = carry
            acc_o, prev_o = [], []
            for k in range(n_k):
                sl = pl.ds(pl.multiple_of(j * SUBLANES, SUBLANES), SUBLANES)
                g_r, g_i = gre[k, sl, :], gim[k, sl, :]
                p_r, p_i = prev[2 * k], prev[2 * k + 1]
                acc_o += [acc[2 * k] + g_r * p_r + g_i * p_i, acc[2 * k + 1] + g_i * p_r - g_r * p_i]
                prev_o += [hre[k, sl, :], him[k, sl, :]]
            return tuple(acc_o), tuple(prev_o)

        zero = jnp.zeros((SUBLANES, LANES), F32)
        acc, _ = lax.fori_loop(0, seg, corr, ((zero,) * (2 * n_k), tuple(enter)), unroll=SCAN_UNROLL)
        da_r = jnp.concatenate([_colsum(acc[2 * k]) for k in range(n_k)], axis=1)
        da_i = jnp.concatenate([_colsum(acc[2 * k + 1]) for k in range(n_k)], axis=1)

        zeros_cn = jnp.zeros((SSM_CH, ns), F32)
        qt_r, qt_i, dct_r, dct_i = zeros_cn, zeros_cn, zeros_cn, zeros_cn
        dd = jnp.zeros((1, SSM_CH), F32)
        first = half == 0
        for c0 in range(0, seq, tb):
            rows = pl.ds(c0, tb)
            ub = u_ref[rows, :]
            dys = dys_of(rows)
            dct_r = dct_r + _dot(dys, _gather_k(hre, rows, n_k), _TN)
            dct_i = dct_i - _dot(dys, _gather_k(him, rows, n_k), _TN)
            g_r, g_i = _gather_k(gre, rows, n_k), _gather_k(gim, rows, n_k)
            qt_r = qt_r + _dot(ub, g_r, _TN)
            qt_i = qt_i + _dot(ub, g_i, _TN)
            dbu_r, dbu_i = coef_r * g_r + coef_i * g_i, coef_r * g_i - coef_i * g_r
            du = _dot(dbu_r, b_ref[0, 0], _NT) + _dot(dbu_i, b_ref[0, 1], _NT)
            dd = dd + _colsum(dys * ub)

            @pl.when(first)
            def _(du=du, dys=dys, rows=rows):
                du_ref[rows, :] = du + d_ref[...] * dys

            @pl.when(jnp.logical_not(first))
            def _(du=du, rows=rows):
                du_ref[rows, :] += du

        @pl.when(first)
        def _():
            dd_ref[...] = dd

        b_r, b_i = b_ref[0, 0], b_ref[0, 1]
        dbt_ref[0, 0] = coef_r * qt_r + coef_i * qt_i
        dbt_ref[0, 1] = coef_r * qt_i - coef_i * qt_r
        dct_ref[0, 0] = dct_r
        dct_ref[0, 1] = dct_i
        dpar_ref[0, 0:1, :] = da_r
        dpar_ref[0, 1:2, :] = da_i
        dpar_ref[0, 2:3, :] = _colsum(b_r * qt_r + b_i * qt_i)
        dpar_ref[0, 3:4, :] = _colsum(b_r * qt_i - b_i * qt_r)

    n_cb = width // SSM_CH
    blk = lambda i, h: (0, i)
    return _pcall(
        body, name="ssm_bwd", grid=(n_cb, n_half),
        in_specs=[pl.BlockSpec((seq, SSM_CH), blk), pl.BlockSpec((seq, SSM_CH), blk), pl.BlockSpec((seq, SSM_CH), blk),
                  pl.BlockSpec((1, 2, SSM_CH, ns), lambda i, h: (i, 0, 0, h)),
                  pl.BlockSpec((1, 2, SSM_CH, ns), lambda i, h: (i, 0, 0, h)),
                  pl.BlockSpec((1, 4, ns), lambda i, h: (i, 0, h)),
                  pl.BlockSpec((1, SSM_CH), blk)],
        out_specs=[pl.BlockSpec((seq, SSM_CH), blk),
                   pl.BlockSpec((1, 2, SSM_CH, ns), lambda i, h: (i, 0, 0, h)),
                   pl.BlockSpec((1, 2, SSM_CH, ns), lambda i, h: (i, 0, 0, h)),
                   pl.BlockSpec((1, 4, ns), lambda i, h: (i, 0, h)),
                   pl.BlockSpec((1, SSM_CH), blk)],
        out_shape=[jax.ShapeDtypeStruct((seq, width), F32),
                   jax.ShapeDtypeStruct((n_cb, 2, SSM_CH, ns_all), F32),
                   jax.ShapeDtypeStruct((n_cb, 2, SSM_CH, ns_all), F32),
                   jax.ShapeDtypeStruct((n_cb, 4, ns_all), F32),
                   jax.ShapeDtypeStruct((1, width), F32)],
        scratch_shapes=[pltpu.VMEM((n_k, seq, LANES), F32) for _ in range(4)],
        compiler_params=_params(("parallel", "arbitrary")),
    )(uz, dy1, ys, b_blk, ct_blk, par, d_skip)


def _ssm_discretize(a_re, a_im, log_dt):
    dt = jnp.exp(log_dt)[:, None]
    mag = jnp.exp(a_re * dt)
    abar_re = mag * jnp.cos(a_im * dt)
    abar_im = mag * jnp.sin(a_im * dt)
    den = a_re * a_re + a_im * a_im
    nr = abar_re - 1.0
    coef_re = (nr * a_re + abar_im * a_im) / den
    coef_im = (abar_im * a_re - nr * a_im) / den
    return abar_re, abar_im, coef_re, coef_im


def _block_diag(w_gcp):
    gpb = SSM_CH // S5_GROUP
    n_cb = w_gcp.shape[0] // gpb
    w = w_gcp.reshape(n_cb, gpb, S5_GROUP, 1, S5_STATE)
    eye = jnp.eye(gpb, dtype=w.dtype)[None, :, None, :, None]
    return (w * eye).reshape(n_cb, SSM_CH, gpb * S5_STATE)


def _block_diag_extract(w_blk):
    gpb = SSM_CH // S5_GROUP
    n_cb = w_blk.shape[0]
    w = w_blk.reshape(n_cb, gpb, S5_GROUP, gpb, S5_STATE)
    w = jnp.moveaxis(jnp.diagonal(w, axis1=1, axis2=3), -1, 1)
    return w.reshape(n_cb * gpb, S5_GROUP, S5_STATE)


def _split3(x):
    hi = x.astype(BF16)
    mid = (x - hi.astype(F32)).astype(BF16)
    lo = (x - hi.astype(F32) - mid.astype(F32)).astype(BF16)
    return hi, mid, lo


def _tri_sum(tri, x):
    hi, mid, lo = _split3(x)
    return (jnp.dot(tri, hi, preferred_element_type=F32) + jnp.dot(tri, mid, preferred_element_type=F32)
            + jnp.dot(tri, lo, preferred_element_type=F32))


def _log_sigmoid(x):
    return jnp.minimum(x, 0.0) - jnp.log(1.0 + jnp.exp(-jnp.abs(x)))


def _cum_fwd(fl, b_f):
    seq = fl.shape[0]
    t = min(CUM_TILE, seq)

    def body(fl_ref, b_ref, o_ref, carry):
        @pl.when(pl.program_id(0) == 0)
        def _():
            carry[...] = jnp.zeros_like(carry)

        r = lax.broadcasted_iota(jnp.int32, (t, t), 0)
        c = lax.broadcasted_iota(jnp.int32, (t, t), 1)
        tri = (c <= r).astype(BF16)
        cum = _tri_sum(tri, _log_sigmoid(fl_ref[...] + b_ref[...])) + carry[...]
        o_ref[...] = cum
        carry[...] = cum[t - 1:t, :]

    return _pcall(
        body, name="cum_fwd", grid=(seq // t,),
        in_specs=[pl.BlockSpec((t, LANES), lambda i: (i, 0)), pl.BlockSpec((1, LANES), lambda i: (0, 0))],
        out_specs=pl.BlockSpec((t, LANES), lambda i: (i, 0)),
        out_shape=jax.ShapeDtypeStruct((seq, LANES), F32),
        scratch_shapes=[pltpu.VMEM((1, LANES), F32)],
        compiler_params=_params(("arbitrary",)),
    )(fl, b_f)


def _cum_bwd(dcum, fl, b_f):
    seq = fl.shape[0]
    t = min(CUM_TILE, seq)
    nb = seq // t

    def body(dc_ref, fl_ref, b_ref, o_ref, db_ref, carry):
        @pl.when(pl.program_id(0) == 0)
        def _():
            carry[...] = jnp.zeros_like(carry)
            db_ref[...] = jnp.zeros_like(db_ref)

        r = lax.broadcasted_iota(jnp.int32, (t, t), 0)
        c = lax.broadcasted_iota(jnp.int32, (t, t), 1)
        tri = (c >= r).astype(BF16)
        rev = _tri_sum(tri, dc_ref[...]) + carry[...]
        carry[...] = rev[0:1, :]
        dfl = rev * _sigmoid(-(fl_ref[...] + b_ref[...]))
        o_ref[...] = dfl
        db_ref[...] += _colsum(dfl)

    return _pcall(
        body, name="cum_bwd", grid=(nb,),
        in_specs=[pl.BlockSpec((t, LANES), lambda i: (nb - 1 - i, 0)), pl.BlockSpec((t, LANES), lambda i: (nb - 1 - i, 0)),
                  pl.BlockSpec((1, LANES), lambda i: (0, 0))],
        out_specs=[pl.BlockSpec((t, LANES), lambda i: (nb - 1 - i, 0)), pl.BlockSpec((1, LANES), lambda i: (0, 0))],
        out_shape=[jax.ShapeDtypeStruct((seq, LANES), F32), jax.ShapeDtypeStruct((1, LANES), F32)],
        scratch_shapes=[pltpu.VMEM((1, LANES), F32)],
        compiler_params=_params(("arbitrary",)),
    )(dcum, fl, b_f)


def _att_scores(q, kb, cq, ck, scale, row0, col0, masked):
    s = _dot(q, kb, _NT) * scale + cq - ck
    if masked:
        rows = row0 + lax.broadcasted_iota(jnp.int32, s.shape, 0)
        cols = col0 + lax.broadcasted_iota(jnp.int32, s.shape, 1)
        s = jnp.where(cols <= rows, s, NEG_INF)
    return s


def _att_fwd(qz, kv, cq, ck):
    seq = qz.shape[0]
    heads = cq.shape[0]
    t = min(ATT_TILE, seq)
    scale = HEAD_DIM ** -0.5

    def body(q_ref, k_ref, v_ref, cq_ref, ck_ref, o_ref, lse_ref):
        i = pl.program_id(1)
        q = q_ref[...].astype(BF16)
        cqv = cq_ref[0]

        def block(j, carry, masked):
            m, l, acc = carry
            rows = pl.ds(pl.multiple_of(j * t, t), t)
            s = _att_scores(q, k_ref[rows, :], cqv, ck_ref[0, j], scale, i * t, j * t, masked)
            m_new = jnp.maximum(m, jnp.max(s, axis=1, keepdims=True))
            p = jnp.exp(s - m_new)
            alpha = jnp.exp(m - m_new)
            return m_new, alpha * l + jnp.sum(p, axis=1, keepdims=True), alpha * acc + _dot(p, v_ref[rows, :])

        init = (jnp.full((t, 1), NEG_INF, F32), jnp.zeros((t, 1), F32), jnp.zeros((t, HEAD_DIM), F32))
        carry = lax.fori_loop(0, i, functools.partial(block, masked=False), init)
        m, l, acc = block(i, carry, True)
        o_ref[...] = acc / l
        lse_ref[0] = m + jnp.log(l)

    return _pcall(
        body, name="att_fwd", grid=(heads, seq // t),
        in_specs=[pl.BlockSpec((t, HEAD_DIM), lambda h, i: (i, h)),
                  pl.BlockSpec((seq, HEAD_DIM), lambda h, i: (0, h)),
                  pl.BlockSpec((seq, HEAD_DIM), lambda h, i: (0, heads + h)),
                  pl.BlockSpec((1, t, 1), lambda h, i: (h, i, 0)),
                  pl.BlockSpec((1, seq // t, 1, t), lambda h, i: (h, 0, 0, 0))],
        out_specs=[pl.BlockSpec((t, HEAD_DIM), lambda h, i: (i, h)), pl.BlockSpec((1, t, 1), lambda h, i: (h, i, 0))],
        out_shape=[jax.ShapeDtypeStruct((seq, heads * HEAD_DIM), F32), jax.ShapeDtypeStruct((heads, seq, 1), F32)],
        compiler_params=_params(("parallel", "parallel")),
    )(qz, kv, kv, cq, ck)


def _att_bwd_q(qz, kv, do, o, lse, cq, ck):
    seq = qz.shape[0]
    heads = cq.shape[0]
    t = min(ATT_TILE, seq)
    scale = HEAD_DIM ** -0.5

    def body(q_ref, k_ref, v_ref, do_ref, o_ref, lse_ref, cq_ref, ck_ref, dq_ref, delta_ref):
        i = pl.program_id(1)
        q = q_ref[...].astype(BF16)
        dob = do_ref[...].astype(BF16)
        delta = jnp.sum(do_ref[...] * o_ref[...], axis=1, keepdims=True)
        cqv, lse_v = cq_ref[0], lse_ref[0]

        def block(j, carry, masked):
            dq, pdp = carry
            rows = pl.ds(pl.multiple_of(j * t, t), t)
            kb = k_ref[rows, :]
            s = _att_scores(q, kb, cqv, ck_ref[0, j], scale, i * t, j * t, masked)
            p = jnp.exp(s - lse_v)
            dp = _dot(dob, v_ref[rows, :], _NT)
            ds = p * (dp - delta)
            return dq + _dot(ds * scale, kb), pdp + jnp.sum(p * dp, axis=1, keepdims=True)

        init = (jnp.zeros((t, HEAD_DIM), F32), jnp.zeros((t, 1), F32))
        carry = lax.fori_loop(0, i, functools.partial(block, masked=False), init)
        dq, pdp = block(i, carry, True)
        dq_ref[...] = dq.astype(dq_ref.dtype)
        delta_ref[0] = pdp

    qblk = pl.BlockSpec((t, HEAD_DIM), lambda h, i: (i, h))
    col = pl.BlockSpec((1, t, 1), lambda h, i: (h, i, 0))
    return _pcall(
        body, name="att_bwd_q", grid=(heads, seq // t),
        in_specs=[qblk, pl.BlockSpec((seq, HEAD_DIM), lambda h, i: (0, h)),
                  pl.BlockSpec((seq, HEAD_DIM), lambda h, i: (0, heads + h)), qblk, qblk, col, col,
                  pl.BlockSpec((1, seq // t, 1, t), lambda h, i: (h, 0, 0, 0))],
        out_specs=[qblk, col],
        out_shape=[jax.ShapeDtypeStruct((seq, heads * HEAD_DIM), BF16), jax.ShapeDtypeStruct((heads, seq, 1), F32)],
        compiler_params=_params(("parallel", "parallel")),
    )(qz, kv, kv, do, o, lse, cq, ck)


def _att_bwd_kv(qz, kv, do, lse, delta, cq, ck):
    seq = qz.shape[0]
    heads = cq.shape[0]
    t = min(ATT_TILE, seq)
    nq = seq // t
    scale = HEAD_DIM ** -0.5

    def body(q_ref, k_ref, v_ref, do_ref, lse_ref, delta_ref, cq_ref, ck_ref, dk_ref, dv_ref, dck_ref):
        j = pl.program_id(1)
        kb, vb = k_ref[...], v_ref[...]
        ckv = ck_ref[0, 0]

        def block(i, carry, masked):
            dk, dv, dck = carry
            rows = pl.ds(pl.multiple_of(i * t, t), t)
            qb = q_ref[rows, :].astype(BF16)
            dob = do_ref[rows, :].astype(BF16)
            s = _att_scores(qb, kb, cq_ref[0, rows, :], ckv, scale, i * t, j * t, masked)
            p = jnp.exp(s - lse_ref[0, rows, :])
            ds = p * (_dot(dob, vb, _NT) - delta_ref[0, rows, :])
            return dk + _dot(ds * scale, qb, _TN), dv + _dot(p, dob, _TN), dck - _colsum(ds)

        init = (jnp.zeros((t, HEAD_DIM), F32), jnp.zeros((t, HEAD_DIM), F32), jnp.zeros((1, t), F32))
        carry = block(j, init, True)
        dk, dv, dck = lax.fori_loop(j + 1, nq, functools.partial(block, masked=False), carry)
        dk_ref[...] = dk.astype(dk_ref.dtype)
        dv_ref[...] = dv.astype(dv_ref.dtype)
        dck_ref[0, 0] = dck

    head = pl.BlockSpec((seq, HEAD_DIM), lambda h, j: (0, h))
    col = pl.BlockSpec((1, seq, 1), lambda h, j: (h, 0, 0))
    kblk = pl.BlockSpec((t, HEAD_DIM), lambda h, j: (j, h))
    row = pl.BlockSpec((1, 1, 1, t), lambda h, j: (h, j, 0, 0))
    return _pcall(
        body, name="att_bwd_kv", grid=(heads, nq),
        in_specs=[head, kblk, pl.BlockSpec((t, HEAD_DIM), lambda h, j: (j, heads + h)), head, col, col, col, row],
        out_specs=[kblk, kblk, row],
        out_shape=[jax.ShapeDtypeStruct((seq, heads * HEAD_DIM), BF16), jax.ShapeDtypeStruct((seq, heads * HEAD_DIM), BF16),
                   jax.ShapeDtypeStruct((heads, nq, 1, t), F32)],
        compiler_params=_params(("parallel", "parallel")),
    )(qz, kv, kv, do, lse, delta, cq, ck)


ANY = pl.BlockSpec(memory_space=pl.ANY)


def _mesh_pos():
    return lax.axis_index("x"), lax.axis_index("y"), lax.axis_index("c")


def _other_chips(x, y):
    return [(1 - x, y), (x, 1 - y), (1 - x, 1 - y)]


def _all_gather_weights(big, small):
    nb, ns = len(big), len(small)
    n_remote = 3 * (nb + ns)

    def body(*refs):
        ins, outs = refs[:nb + ns], refs[nb + ns:2 * (nb + ns)]
        send_sems, recv_sems, fwd_send, fwd_recv = refs[2 * (nb + ns):]
        x, y, c = _mesh_pos()
        me = 2 * x + y
        chips = _other_chips(x, y)
        sibling = (x, y, 1 - c)

        def half(ref, hc):
            rh = ref.shape[-2] // 2
            return ref.at[pl.ds(hc * rh, rh), :]

        def remote(i, j, src_chip, to, from_in):
            if i < nb:
                src = half(ins[i], c) if from_in else half(outs[i].at[src_chip], c)
                dst = half(outs[i].at[src_chip], c)
            else:
                src = ins[i] if from_in else outs[i].at[src_chip]
                dst = outs[i].at[src_chip]
            k = 3 * i + j
            return pltpu.make_async_remote_copy(src_ref=src, dst_ref=dst, send_sem=send_sems.at[k],
                                                recv_sem=recv_sems.at[k], device_id=to, device_id_type=MESH_ID)

        def forward(i, j, src_chip, hc):
            part = half(outs[i].at[src_chip], hc)
            k = 3 * i + j
            return pltpu.make_async_remote_copy(src_ref=part, dst_ref=part, send_sem=fwd_send.at[k],
                                                recv_sem=fwd_recv.at[k], device_id=sibling, device_id_type=MESH_ID)

        sends = [remote(i, j, me, (*chips[j], c), True) for i in range(nb + ns) for j in range(3)]
        for cp in sends:
            cp.start()
        fwds = []
        for i in range(nb + ns):
            for j in range(3):
                src_chip = 2 * chips[j][0] + chips[j][1]
                remote(i, j, src_chip, (*chips[j], c), False).wait_recv()
                if i < nb:
                    cp = forward(i, j, src_chip, c)
                    cp.start()
                    fwds.append(cp)
        for i in range(nb):
            for j in range(3):
                src_chip = 2 * chips[j][0] + chips[j][1]
                forward(i, j, src_chip, 1 - c).wait_recv()
        for cp in sends + fwds:
            cp.wait_send()

    arrays = list(big) + list(small)
    return _pcall(
        body, name="all_gather_weights",
        in_specs=[ANY] * (nb + ns), out_specs=[ANY] * (nb + ns),
        out_shape=[jax.ShapeDtypeStruct((N_CHIPS,) + a.shape, a.dtype) for a in arrays],
        scratch_shapes=[pltpu.SemaphoreType.DMA((n_remote,)), pltpu.SemaphoreType.DMA((n_remote,)),
                        pltpu.SemaphoreType.DMA((3 * nb,)), pltpu.SemaphoreType.DMA((3 * nb,))],
        compiler_params=pltpu.CompilerParams(has_side_effects=True),
    )(*arrays)


def _swap_halves(grads):
    n = len(grads)

    def body(*refs):
        ins, outs, send_sems, recv_sems = refs[:n], refs[n:2 * n], refs[2 * n], refs[2 * n + 1]
        x, y, c = _mesh_pos()
        cps = []
        for i in range(n):
            rh = ins[i].shape[1] // 2
            cps.append(pltpu.make_async_remote_copy(
                src_ref=ins[i].at[:, pl.ds((1 - c) * rh, rh), :], dst_ref=outs[i], send_sem=send_sems.at[i],
                recv_sem=recv_sems.at[i], device_id=(x, y, 1 - c), device_id_type=MESH_ID))
        for cp in cps:
            cp.start()
        for cp in cps:
            cp.wait()

    return _pcall(
        body, name="grad_swap_halves", in_specs=[ANY] * n, out_specs=[ANY] * n,
        out_shape=[jax.ShapeDtypeStruct((g.shape[0], g.shape[1] // 2, g.shape[2]), g.dtype) for g in grads],
        scratch_shapes=[pltpu.SemaphoreType.DMA((n,)), pltpu.SemaphoreType.DMA((n,))],
        compiler_params=pltpu.CompilerParams(has_side_effects=True),
    )(*grads)


def _scatter_to_owner(parts):
    n = len(parts)

    def body(*refs):
        ins, outs, send_sems, recv_sems = refs[:n], refs[n:2 * n], refs[2 * n], refs[2 * n + 1]
        x, y, c = _mesh_pos()
        chips = _other_chips(x, y)
        cps = []
        for i in range(n):
            for j in range(3):
                k = 3 * i + j
                cps.append(pltpu.make_async_remote_copy(
                    src_ref=ins[i].at[2 * chips[j][0] + chips[j][1]], dst_ref=outs[i].at[j],
                    send_sem=send_sems.at[k], recv_sem=recv_sems.at[k], device_id=(*chips[j], c),
                    device_id_type=MESH_ID))
        for cp in cps:
            cp.start()
        for cp in cps:
            cp.wait()

    return _pcall(
        body, name="grad_scatter", in_specs=[ANY] * n, out_specs=[ANY] * n,
        out_shape=[jax.ShapeDtypeStruct((3,) + p.shape[1:], p.dtype) for p in parts],
        scratch_shapes=[pltpu.SemaphoreType.DMA((3 * n,)), pltpu.SemaphoreType.DMA((3 * n,))],
        compiler_params=pltpu.CompilerParams(has_side_effects=True),
    )(*parts)


def _join_halves(halves):
    n = len(halves)

    def body(*refs):
        ins, outs, send_sems, recv_sems = refs[:n], refs[n:2 * n], refs[2 * n], refs[2 * n + 1]
        x, y, c = _mesh_pos()
        cps = []
        for i in range(n):
            rh = ins[i].shape[0]
            cps.append(pltpu.make_async_remote_copy(
                src_ref=ins[i], dst_ref=outs[i].at[pl.ds(c * rh, rh), :], send_sem=send_sems.at[i],
                recv_sem=recv_sems.at[i], device_id=(x, y, 1 - c), device_id_type=MESH_ID))
        for cp in cps:
            cp.start()
        for cp in cps:
            cp.wait_send()
        for i in range(n):
            rh = ins[i].shape[0]
            theirs = outs[i].at[pl.ds((1 - c) * rh, rh), :]
            pltpu.make_async_remote_copy(src_ref=ins[i], dst_ref=theirs, send_sem=send_sems.at[i],
                                         recv_sem=recv_sems.at[i], device_id=(x, y, 1 - c),
                                         device_id_type=MESH_ID).wait_recv()

    return _pcall(
        body, name="grad_join_halves", in_specs=[ANY] * n, out_specs=[ANY] * n,
        out_shape=[jax.ShapeDtypeStruct((2 * h.shape[0], h.shape[1]), h.dtype) for h in halves],
        scratch_shapes=[pltpu.SemaphoreType.DMA((n,)), pltpu.SemaphoreType.DMA((n,))],
        compiler_params=pltpu.CompilerParams(has_side_effects=True),
    )(*halves)


def _all_reduce_small(v):
    def body(v_ref, o_ref, recv, send_sems, recv_sems):
        x, y, c = _mesh_pos()
        peers = [(x, y, 1 - c), (1 - x, y, c), (x, 1 - y, c)]
        for s, peer in enumerate(peers):
            src = v_ref if s == 0 else o_ref
            cp = pltpu.make_async_remote_copy(src_ref=src, dst_ref=recv.at[s], send_sem=send_sems.at[s],
                                              recv_sem=recv_sems.at[s], device_id=peer, device_id_type=MESH_ID)
            cp.start()
            cp.wait()
            o_ref[...] = src[...] + recv[s]

    vm = pl.BlockSpec(memory_space=pltpu.VMEM)
    return _pcall(
        body, name="all_reduce_small", in_specs=[vm], out_specs=vm,
        out_shape=jax.ShapeDtypeStruct(v.shape, v.dtype),
        scratch_shapes=[pltpu.VMEM((3,) + v.shape, v.dtype), pltpu.SemaphoreType.DMA((3,)), pltpu.SemaphoreType.DMA((3,))],
        compiler_params=pltpu.CompilerParams(vmem_limit_bytes=VMEM_LIMIT, has_side_effects=True),
    )(v)


def _adamw_math(w, g, m, v):
    m = ADAM_B1 * m + (1.0 - ADAM_B1) * g
    v = ADAM_B2 * v + (1.0 - ADAM_B2) * (g * g)
    m_hat = m / (1.0 - ADAM_B1 ** ADAM_STEP)
    v_hat = v / (1.0 - ADAM_B2 ** ADAM_STEP)
    delta = -ADAM_LR * (m_hat / (jnp.sqrt(v_hat) + ADAM_EPS) + ADAM_WD * w)
    return delta, m, v


def _adamw(w, g, m, v, name):
    wd = w.shape[1]
    return _rows(_adamw_math, [_full(w), _full(g), _full(m), _full(v)], [(wd, F32)] * 3, name=name)


def _pack(arrs, rows):
    flat = jnp.concatenate([a.reshape(-1) for a in arrs])
    return jnp.pad(flat, (0, rows * LANES - flat.shape[0])).reshape(rows, LANES)


def _unpack(buf, like):
    flat = buf.reshape(-1)
    out, off = [], 0
    for a in like:
        out.append(flat[off:off + a.size].reshape(a.shape))
        off += a.size
    return out


def kernel(x, norm_pre, norm_post, s5_w_in, s5_a_re, s5_a_im, s5_log_dt, s5_b_re, s5_b_im, s5_c_re, s5_c_im, s5_d, s5_w_glu, s5_b_glu, s5_w_out, kv_norm, kv_w, kv_b_f, fox_w_in, fox_w_out, loss_target, m_norm_pre, m_norm_post, m_s5_w_in, m_s5_a_re, m_s5_a_im, m_s5_log_dt, m_s5_b_re, m_s5_b_im, m_s5_c_re, m_s5_c_im, m_s5_d, m_s5_w_glu, m_s5_b_glu, m_s5_w_out, m_kv_norm, m_kv_w, m_kv_b_f, m_fox_w_in, m_fox_w_out, v_norm_pre, v_norm_post, v_s5_w_in, v_s5_a_re, v_s5_a_im, v_s5_log_dt, v_s5_b_re, v_s5_b_im, v_s5_c_re, v_s5_c_im, v_s5_d, v_s5_w_glu, v_s5_b_glu, v_s5_w_out, v_kv_norm, v_kv_w, v_kv_b_f, v_fox_w_in, v_fox_w_out):
    seq, dm = x.shape[1], x.shape[2]
    width = dm
    heads = dm // HEAD_DIM
    fw = heads * HEAD_DIM
    groups = width // S5_GROUP
    chip = 2 * lax.axis_index("x") + lax.axis_index("y")

    big_shards = [s5_w_in[0], s5_w_glu[0], s5_w_out[0], kv_w, fox_w_in[0], fox_w_out[0]]
    own_shards = [w.astype(BF16) for w in big_shards] + [s5_d, s5_b_glu]
    gathered = _all_gather_weights(own_shards[:6], own_shards[6:])
    gathered = [lax.dynamic_update_slice(g, own[None], (chip, 0, 0)) for g, own in zip(gathered, own_shards)]
    g_win, g_wglu, g_wout, g_kvw, g_fwin, g_fwout, g_d, g_bglu = gathered
    cols = lambda g: jnp.moveaxis(g, 0, 1).reshape(g.shape[1], -1)
    rows = lambda g: g.reshape(-1, g.shape[2])
    w_in, w_glu, w_out = g_win, rows(g_wglu), rows(g_wout)
    kvw_full = cols(g_kvw)
    w_kv = kvw_full[:, :2 * fw]
    w_f = jnp.pad(kvw_full[:, 2 * fw:], ((0, 0), (0, LANES - heads)))
    fw_in, fw_out = g_fwin, rows(g_fwout)
    d_skip, b_glu = cols(g_d), cols(g_bglu)
    b_f = jnp.pad(kv_b_f, (0, LANES - heads)).reshape(1, LANES)

    h0 = x[0]
    target = loss_target[0]
    g_pre0, g_pre1 = norm_pre[0:1], norm_pre[1:2]
    g_post0, g_post1 = norm_post[0:1], norm_post[1:2]
    g_kv = kv_norm.reshape(1, dm)

    a_re, a_im, log_dt = s5_a_re[0], s5_a_im[0], s5_log_dt[0]
    disc, disc_vjp = jax.vjp(_ssm_discretize, a_re, a_im, log_dt)
    gpb = SSM_CH // S5_GROUP
    n_cb = groups // gpb
    par = jnp.stack([p.reshape(n_cb, gpb * S5_STATE) for p in disc], axis=1)
    b_t = lambda b: jnp.swapaxes(b, 1, 2)
    b_blk = jnp.stack([_block_diag(b_t(s5_b_re[0])), _block_diag(b_t(s5_b_im[0]))], axis=1)
    ct_blk = jnp.stack([_block_diag(s5_c_re[0]), _block_diag(s5_c_im[0])], axis=1)
    c_blk = jnp.swapaxes(ct_blk, 2, 3)

    xn1 = _to_slab(_rows(lambda h, g: (h * _rstd(h) * g,), [_full(h0)], [(dm, BF16)], consts=[g_pre0], name="norm_pre0")[0])
    uz = _mm(xn1, w_in, name="s5_in")
    ys = _ssm_fwd(uz, b_blk, c_blk, par, d_skip)
    y1b = _rows(lambda y: (_gelu(y),), [_full(ys)], [(width, BF16)], name="gelu")[0]
    glu_a = _mm(y1b, w_glu, name="s5_glu")

    def gate_fn(y, a, z, b):
        return (_gelu(y) * _sigmoid(a + b) * _silu(z),)

    y3b = _rows(gate_fn, [_full(ys), _full(glu_a), (uz, width, 1)], [(width, BF16)], consts=[b_glu], name="s5_gate")[0]
    o1 = _from_slab(_mm(y3b, w_out, name="s5_out"))

    def mid_fn(h, o, gp, gk, gq):
        h1 = h + o * _rstd(o) * gp
        r = _rstd(h1)
        return h1, h1 * r * gk, h1 * r * gq

    h1, xk, xn2 = _rows(mid_fn, [_full(h0), _full(o1)], [(dm, F32), (dm, BF16), (dm, BF16)],
                        consts=[g_post0, g_kv, g_pre1], name="mid_norms")

    kv = _mm(xk, w_kv, out_dtype=BF16, name="kv_proj")
    fl = _mm(xk, w_f, name="f_proj")
    qz = _mm(xn2, fw_in, name="fox_in")
    cum = _cum_fwd(fl, b_f)
    t_att = min(ATT_TILE, seq)
    cum_t = cum[:, :heads].T
    cq = cum_t.reshape(heads, seq, 1)
    ck = cum_t.reshape(heads, seq // t_att, 1, t_att)
    o, lse = _att_fwd(qz, kv, cq, ck)
    o2b = _rows(lambda a, z: (a * _silu(z),), [_full(o), (qz, fw, 1)], [(fw, BF16)], name="fox_gate")[0]
    o3 = _mm(o2b, fw_out, name="fox_out")

    def loss_fn(h, o, t, g):
        r = _rstd(o)
        err = h + o * r * g - t
        dh = err * (1.0 / dm)
        do, dg = _rms_bwd(o, g, dh)
        part = 0.5 * jnp.sum(jnp.mean(err * err, axis=-1, keepdims=True), axis=0, keepdims=True)
        return dh, do, jnp.broadcast_to(part, (1, LANES)), _colsum(dg)

    dh2, do3, loss_part, dg_post1 = _rows(loss_fn, [_full(h1), _full(o3), _full(target)], [(dm, F32), (dm, BF16)],
                                          consts=[g_post1], accs=[(1, LANES), (1, dm)], name="loss_head")
    loss = lax.psum(loss_part[0, 0], MESH_AXES)

    do2 = _mm(do3, fw_out, tb=True, name="fox_out_dx")
    dw_fout = _mm(o2b, do3, ta=True, name="fox_out_dw")

    def fox_gate_bwd(d, a, z):
        return d * _silu(z), d * a * _dsilu(z)

    do, dz2 = _rows(fox_gate_bwd, [_full(do2), _full(o), (qz, fw, 1)], [(fw, F32), (fw, BF16)], name="fox_gate_bwd")
    dq, delta = _att_bwd_q(qz, kv, do, o, lse, cq, ck)
    dk, dv, dck = _att_bwd_kv(qz, kv, do, lse, delta, cq, ck)
    dcum = jnp.pad(dck.reshape(heads, seq).T, ((0, 0), (0, LANES - heads)))
    dfl, db_f = _cum_bwd(dcum, fl, b_f)
    dqz = (dq, dz2)
    dkv = (dk, dv)
    dxn2 = _mm(dqz, fw_in, tb=True, name="fox_in_dx")
    dw_fin = _mm(xn2, dqz, ta=True, out_split=N_CHIPS, name="fox_in_dw")
    dxk_f = _mm(dfl, w_f, tb=True, name="f_proj_dx")
    dxk = _mm(dkv, w_kv, tb=True, add=dxk_f, name="kv_proj_dx")
    dw_kv = _mm(xk, dkv, ta=True, name="kv_proj_dw")
    dw_f = _mm(xk, dfl, ta=True, name="f_proj_dw")

    def mid_bwd(d2, h, dq_, dk_, o, gq, gk, gp):
        dxa, dga = _rms_bwd(h, gq, dq_)
        dxb, dgb = _rms_bwd(h, gk, dk_)
        dh = d2 + dxa + dxb
        do_, dgp = _rms_bwd(o, gp, dh)
        return dh, do_, _colsum(dga), _colsum(dgb), _colsum(dgp)

    dh1, do1, dg_pre1, dg_kv, dg_post0 = _rows(
        mid_bwd, [_full(dh2), _full(h1), _full(dxn2), _full(dxk), _full(o1)], [(dm, F32), (dm, BF16)],
        consts=[g_pre1, g_kv, g_post0], accs=[(1, dm)] * 3, name="mid_norms_bwd")

    do1 = _to_slab(do1)
    dy3 = _mm(do1, w_out, tb=True, name="s5_out_dx")
    dw_out = _mm(y3b, do1, ta=True, name="s5_out_dw")

    def gate_bwd(d3, y, a, z, b):
        y1 = _gelu(y)
        gate = _sigmoid(a + b)
        dy2 = d3 * _silu(z)
        da = dy2 * y1 * gate * (1.0 - gate)
        return dy2 * gate, da, d3 * (y1 * gate) * _dsilu(z), _colsum(da)

    dy1_direct, da, dz, db_glu = _rows(gate_bwd, [_full(dy3), _full(ys), _full(glu_a), (uz, width, 1)],
                                       [(width, F32), (width, BF16), (width, BF16)], consts=[b_glu],
                                       accs=[(1, width)], name="s5_gate_bwd")
    dy1 = _mm(da, w_glu, tb=True, add=dy1_direct, name="s5_glu_dx")
    dw_glu = _mm(y1b, da, ta=True, name="s5_glu_dw")
    du, dbt_blk, dct_blk, dpar, dd = _ssm_bwd(uz, dy1, ys, b_blk, ct_blk, par, d_skip)
    duz = (du, dz)
    dxn1 = _from_slab(_mm(duz, w_in, tb=True, name="s5_in_dx"))
    dw_in = _mm(xn1, duz, ta=True, out_split=N_CHIPS, name="s5_in_dw")

    def first_bwd(d1, h, dxn, g):
        dx, dg = _rms_bwd(h, g, dxn)
        return d1 + dx, _colsum(dg)

    grad_x, dg_pre0 = _rows(first_bwd, [_full(dh1), _full(h0), _full(dxn1)], [(dm, F32)], consts=[g_pre0],
                            accs=[(1, dm)], name="norm_pre0_bwd")

    dpar_g = [dpar[:, i, :].reshape(groups, S5_STATE) for i in range(4)]
    da_re, da_im, dlog_dt = disc_vjp(tuple(dpar_g))
    db_re = jnp.swapaxes(_block_diag_extract(dbt_blk[:, 0]), 1, 2)
    db_im = jnp.swapaxes(_block_diag_extract(dbt_blk[:, 1]), 1, 2)
    dc_re = _block_diag_extract(dct_blk[:, 0])
    dc_im = _block_diag_extract(dct_blk[:, 1])

    small_local = [jnp.concatenate([dg_pre0, dg_pre1]), jnp.concatenate([dg_post0, dg_post1]),
                   da_re[None], da_im[None], dlog_dt[None], db_re[None], db_im[None], dc_re[None], dc_im[None],
                   dd, db_glu, dg_kv.reshape(dm), db_f[0, :heads]]
    n_small = sum(a.size for a in small_local)
    small_rows = -(-n_small // (LANES * ROW_TILE)) * ROW_TILE
    small_sum = _unpack(_all_reduce_small(_pack(small_local, small_rows)), small_local)
    (g_norm_pre, g_norm_post, g_a_re, g_a_im, g_log_dt, g_b_re, g_b_im, g_c_re, g_c_im, g_d_full, g_bglu_full,
     g_kv_norm, g_b_f) = small_sum
    shard = width // N_CHIPS
    g_d_own = lax.dynamic_slice(g_d_full, (0, chip * shard), (1, shard))
    g_bglu_own = lax.dynamic_slice(g_bglu_full, (0, chip * shard), (1, shard))

    to_cols = lambda g: jnp.moveaxis(g.reshape(g.shape[0], N_CHIPS, -1), 1, 0)
    to_rows = lambda g: g.reshape(N_CHIPS, -1, g.shape[1])
    dw_kv_full = jnp.concatenate([dw_kv, dw_f[:, :heads]], axis=1)
    big_grads = [dw_in, to_rows(dw_glu), to_rows(dw_out), to_cols(dw_kv_full), dw_fin, to_rows(dw_fout)]
    theirs = _swap_halves(big_grads)
    c_idx = lax.axis_index("c")
    chip_sums = []
    for i, (g, t) in enumerate(zip(big_grads, theirs)):
        rh = g.shape[1] // 2
        mine = lax.dynamic_slice_in_dim(g, c_idx * rh, rh, axis=1)
        s = _rows(lambda a, b: (a + b,), [_full(mine.reshape(-1, g.shape[2])), _full(t.reshape(-1, g.shape[2]))],
                  [(g.shape[2], F32)], name=f"grad_pair_sum_{i}")[0]
        chip_sums.append(s.reshape(N_CHIPS, rh, g.shape[2]))
    received = _scatter_to_owner(chip_sums)
    halves = []
    for i, (s, r) in enumerate(zip(chip_sums, received)):
        own = lax.dynamic_index_in_dim(s, chip, axis=0, keepdims=False)
        halves.append(_rows(lambda a, b, c_, d_: (((a + b) + c_) + d_,), [_full(own), _full(r[0]), _full(r[1]), _full(r[2])],
                            [(s.shape[2], F32)], name=f"grad_chip_sum_{i}")[0])
    joined = _join_halves(halves)
    g_win_s, g_wglu_s, g_wout_s, g_kvw_s, g_fwin_s, g_fwout_s = [
        lax.dynamic_update_slice(j, h, (c_idx * h.shape[0], 0)) for j, h in zip(joined, halves)]

    big_w = big_shards
    big_g = [g_win_s, g_wglu_s, g_wout_s, g_kvw_s, g_fwin_s, g_fwout_s]
    big_m = [m_s5_w_in[0], m_s5_w_glu[0], m_s5_w_out[0], m_kv_w, m_fox_w_in[0], m_fox_w_out[0]]
    big_v = [v_s5_w_in[0], v_s5_w_glu[0], v_s5_w_out[0], v_kv_w, v_fox_w_in[0], v_fox_w_out[0]]
    big_upd = [_adamw(w, g, m, v, f"adamw_{i}") for i, (w, g, m, v) in enumerate(zip(big_w, big_g, big_m, big_v))]

    small_names = ["norm_pre", "norm_post", "s5_a_re", "s5_a_im", "s5_log_dt", "s5_b_re", "s5_b_im", "s5_c_re", "s5_c_im",
                   "s5_d", "s5_b_glu", "kv_norm", "kv_b_f"]
    small_w = [norm_pre, norm_post, s5_a_re, s5_a_im, s5_log_dt, s5_b_re, s5_b_im, s5_c_re, s5_c_im, s5_d, s5_b_glu, kv_norm, kv_b_f]
    small_m = [m_norm_pre, m_norm_post, m_s5_a_re, m_s5_a_im, m_s5_log_dt, m_s5_b_re, m_s5_b_im, m_s5_c_re, m_s5_c_im, m_s5_d, m_s5_b_glu, m_kv_norm, m_kv_b_f]
    small_v = [v_norm_pre, v_norm_post, v_s5_a_re, v_s5_a_im, v_s5_log_dt, v_s5_b_re, v_s5_b_im, v_s5_c_re, v_s5_c_im, v_s5_d, v_s5_b_glu, v_kv_norm, v_kv_b_f]
    small_g = [g_norm_pre, g_norm_post, g_a_re, g_a_im, g_log_dt, g_b_re, g_b_im, g_c_re, g_c_im, g_d_own, g_bglu_own, g_kv_norm, g_b_f]
    small_g = [g.reshape(w.shape) for g, w in zip(small_g, small_w)]
    n_own = sum(a.size for a in small_w)
    own_rows = -(-n_own // (LANES * ROW_TILE)) * ROW_TILE
    pv = _pack(small_v, own_rows)
    pv = jnp.where(jnp.arange(own_rows * LANES).reshape(own_rows, LANES) < n_own, pv, 1.0)
    sd, sm, sv = _adamw(_pack(small_w, own_rows), _pack(small_g, own_rows), _pack(small_m, own_rows), pv, "adamw_small")
    small_delta, small_newm, small_newv = _unpack(sd, small_w), _unpack(sm, small_w), _unpack(sv, small_w)

    order = ["norm_pre", "norm_post", "s5_w_in", "s5_a_re", "s5_a_im", "s5_log_dt", "s5_b_re", "s5_b_im", "s5_c_re", "s5_c_im",
             "s5_d", "s5_w_glu", "s5_b_glu", "s5_w_out", "kv_norm", "kv_w", "kv_b_f", "fox_w_in", "fox_w_out"]
    big_names = ["s5_w_in", "s5_w_glu", "s5_w_out", "kv_w", "fox_w_in", "fox_w_out"]
    big_like = [s5_w_in, s5_w_glu, s5_w_out, kv_w, fox_w_in, fox_w_out]
    grads, deltas, new_m, new_v = {}, {}, {}, {}
    for i, n in enumerate(big_names):
        shp = big_like[i].shape
        grads[n] = big_g[i].reshape(shp)
        deltas[n], new_m[n], new_v[n] = (a.reshape(shp) for a in big_upd[i])
    for i, n in enumerate(small_names):
        grads[n], deltas[n], new_m[n], new_v[n] = small_g[i], small_delta[i], small_newm[i], small_newv[i]

    return (loss, grad_x[None], *[grads[n] for n in order], *[deltas[n] for n in order],
            *[new_m[n] for n in order], *[new_v[n] for n in order])
```

```python
import functools
import math

import jax
import jax.numpy as jnp
from jax import lax
from jax.experimental import pallas as pl
from jax.experimental.pallas import tpu as pltpu

F32 = jnp.float32
BF16 = jnp.bfloat16

D_MODEL = 2048
SEQ = 4096
S5_GROUP = 16
S5_STATE = 64
HEAD_DIM = 128
RMS_EPS = 1e-6
NEG_INF = -1e30
ADAM_LR = 0.001
ADAM_B1 = 0.9
ADAM_B2 = 0.999
ADAM_EPS = 1e-08
ADAM_WD = 0.01
ADAM_STEP = 10

LANES = 128
SUBLANES = 8
VMEM_LIMIT = 56 * 1024 * 1024
N_CHIPS = 4
MESH_AXES = ("x", "y", "c")
MESH_ID = pl.DeviceIdType.MESH

SSM_CH = 128
ROW_TILE = 256
SCAN_ROWS = 512
SCAN_UNROLL = 4
ATT_TILE = 512
CUM_TILE = 512


def _pcall(body, **kw):
    return pl.pallas_call(body, **kw)


def _params(sem=None):
    if sem is None:
        return pltpu.CompilerParams(vmem_limit_bytes=VMEM_LIMIT)
    return pltpu.CompilerParams(vmem_limit_bytes=VMEM_LIMIT, dimension_semantics=sem)


def _sigmoid(x):
    return 1.0 / (1.0 + jnp.exp(-x))


def _silu(z):
    return z * _sigmoid(z)


def _dsilu(z):
    s = _sigmoid(z)
    return s * (1.0 + z * (1.0 - s))


_GELU_C = math.sqrt(2.0 / math.pi)


def _gelu(x):
    return 0.5 * x * (1.0 + jnp.tanh(_GELU_C * (x + 0.044715 * x * x * x)))


def _dgelu(x):
    t = jnp.tanh(_GELU_C * (x + 0.044715 * x * x * x))
    return 0.5 * (1.0 + t) + 0.5 * x * (1.0 - t * t) * _GELU_C * (1.0 + 3.0 * 0.044715 * x * x)


def _rstd(x):
    return lax.rsqrt(jnp.mean(x * x, axis=-1, keepdims=True) + RMS_EPS)


def _rms_bwd(x, g, dy):
    r = _rstd(x)
    dyg = dy * g
    dx = r * dyg - x * (r * r * r) * jnp.mean(dyg * x, axis=-1, keepdims=True)
    return dx, dy * (x * r)


def _colsum(v):
    return jnp.sum(v, axis=0, keepdims=True)


def _mm(a, b, *, ta=False, tb=False, out_dtype=F32, add=None, out_split=1, tm=1024, tn=1024, tk=2048, name):
    def describe(op):
        if isinstance(op, (tuple, list)):
            assert all(p.ndim == 2 and p.shape == op[0].shape for p in op)
            return list(op), op[0].shape[0], op[0].shape[1], False
        if op.ndim == 3:
            return [op], op.shape[1], op.shape[2], True
        return [op], op.shape[0], op.shape[1], False

    a_parts, a_rows, a_pc, a_stack = describe(a)
    b_parts, b_rows, b_pc, b_stack = describe(b)
    a_cols = a_pc * (a.shape[0] if a_stack else len(a_parts))
    b_cols = b_pc * (b.shape[0] if b_stack else len(b_parts))
    k_dim, m_dim = (a_rows, a_cols) if ta else (a_cols, a_rows)
    n_dim, kb = (b_rows, b_cols) if tb else (b_cols, b_rows)
    assert kb == k_dim, (k_dim, kb)
    tm = min(tm, a_pc) if ta else min(tm, m_dim)
    tk = min(tk, k_dim, k_dim if ta else a_pc, b_pc if tb else k_dim)
    tn = min(tn, n_dim // out_split, n_dim if tb else b_pc)
    a_ct, b_ct = (tm if ta else tk), (tk if tb else tn)
    assert m_dim % tm == 0 and n_dim % tn == 0 and k_dim % tk == 0 and a_pc % a_ct == 0 and b_pc % b_ct == 0
    assert (n_dim // out_split) % tn == 0
    nk = k_dim // tk
    dims = (((0 if ta else 1,), (1 if tb else 0,)), ((), ()))
    n_a, n_b = len(a_parts), len(b_parts)
    assert n_a == 1 or n_b == 1

    def operand_specs(parts, stack, rows_t, cols_t, per, row_of, col_of):
        specs = []
        for p in range(len(parts)):
            def col(i, j, k, p=p):
                return jnp.clip(col_of(i, j, k) - p * per, 0, per - 1) if len(parts) > 1 else col_of(i, j, k)
            if stack:
                specs.append(pl.BlockSpec((None, rows_t, cols_t),
                                          lambda i, j, k, col=col: (col(i, j, k) // per, row_of(i, j, k), col(i, j, k) % per)))
            else:
                specs.append(pl.BlockSpec((rows_t, cols_t), lambda i, j, k, col=col: (row_of(i, j, k), col(i, j, k))))
        return specs

    if ta:
        a_specs = operand_specs(a_parts, a_stack, tk, tm, a_pc // tm, lambda i, j, k: k, lambda i, j, k: i)
    else:
        a_specs = operand_specs(a_parts, a_stack, tm, tk, a_pc // tk, lambda i, j, k: i, lambda i, j, k: k)
    if tb:
        b_specs = operand_specs(b_parts, b_stack, tn, tk, b_pc // tk, lambda i, j, k: j, lambda i, j, k: k)
    else:
        b_specs = operand_specs(b_parts, b_stack, tk, tn, b_pc // tn, lambda i, j, k: k, lambda i, j, k: j)

    def body(*refs):
        a_refs, b_refs = refs[:n_a], refs[n_a:n_a + n_b]
        rest = refs[n_a + n_b:]
        c_ref = rest[0] if add is not None else None
        o_ref = rest[1] if add is not None else rest[0]
        acc = None if nk == 1 else rest[-1]
        i, j, k = pl.program_id(0), pl.program_id(1), pl.program_id(2)

        def finish(res):
            if add is not None:
                res = res + c_ref[...]
            o_ref[...] = res.astype(out_dtype)

        def accumulate(a_ref, b_ref):
            prod = lax.dot_general(a_ref[...].astype(BF16), b_ref[...].astype(BF16), dims,
                                   preferred_element_type=F32)
            if nk == 1:
                finish(prod)
                return

            @pl.when(k == 0)
            def _():
                acc[...] = prod

            @pl.when(jnp.logical_and(k > 0, k < nk - 1))
            def _():
                acc[...] += prod

            @pl.when(k == nk - 1)
            def _():
                finish(acc[...] + prod)

        if n_a == 1 and n_b == 1:
            accumulate(a_refs[0], b_refs[0])
        else:
            many, block, per = (a_refs, (i if ta else k), a_pc // a_ct) if n_a > 1 else (b_refs, (k if tb else j), b_pc // b_ct)
            for p, ref in enumerate(many):
                @pl.when(block // per == p)
                def _(ref=ref):
                    accumulate(ref, b_refs[0]) if n_a > 1 else accumulate(a_refs[0], ref)

    per_out = n_dim // out_split // tn
    if out_split > 1:
        o_spec = pl.BlockSpec((None, tm, tn), lambda i, j, k: (j // per_out, i, j % per_out))
        out_shape = jax.ShapeDtypeStruct((out_split, m_dim, n_dim // out_split), out_dtype)
    else:
        o_spec = pl.BlockSpec((tm, tn), lambda i, j, k: (i, j))
        out_shape = jax.ShapeDtypeStruct((m_dim, n_dim), out_dtype)
    in_specs = a_specs + b_specs + ([pl.BlockSpec((tm, tn), lambda i, j, k: (i, j))] if add is not None else [])
    args = tuple(a_parts) + tuple(b_parts) + ((add,) if add is not None else ())
    return _pcall(
        body, name=name, grid=(m_dim // tm, n_dim // tn, nk), in_specs=in_specs, out_specs=o_spec,
        out_shape=out_shape, scratch_shapes=[pltpu.VMEM((tm, tn), F32)] if nk > 1 else [],
        compiler_params=_params(("parallel", "parallel", "arbitrary")),
    )(*args)


def _rows(fn, ins, outs, *, name, consts=(), accs=()):
    n_rows = ins[0][0].shape[0]
    tr = min(ROW_TILE, n_rows)
    assert n_rows % tr == 0
    n_in, n_c, n_out = len(ins), len(consts), len(outs)

    def body(*refs):
        vals = [r[...] for r in refs[:n_in + n_c]]
        res = fn(*vals)
        res = res if isinstance(res, (tuple, list)) else (res,)
        o_refs = refs[n_in + n_c:]
        for r, v in zip(o_refs[:n_out], res[:n_out]):
            r[...] = v.astype(r.dtype)
        if accs:
            first = pl.program_id(0) == 0
            for r, v in zip(o_refs[n_out:], res[n_out:]):
                @pl.when(first)
                def _(r=r, v=v):
                    r[...] = v

                @pl.when(jnp.logical_not(first))
                def _(r=r, v=v):
                    r[...] += v

    in_specs = [pl.BlockSpec((tr, w), functools.partial(lambda i, cb: (i, cb), cb=cb)) for _, w, cb in ins]
    in_specs += [pl.BlockSpec(c.shape, functools.partial(lambda i, nd: (0,) * nd, nd=c.ndim)) for c in consts]
    out_specs = [pl.BlockSpec((tr, w), lambda i: (i, 0)) for w, _ in outs]
    out_specs += [pl.BlockSpec(s, lambda i: (0, 0)) for s in accs]
    out_shape = [jax.ShapeDtypeStruct((n_rows, w), dt) for w, dt in outs]
    out_shape += [jax.ShapeDtypeStruct(s, F32) for s in accs]
    res = _pcall(
        body, name=name, grid=(n_rows // tr,), in_specs=in_specs, out_specs=out_specs, out_shape=out_shape,
        compiler_params=_params(("arbitrary",) if accs else ("parallel",)),
    )(*[a for a, _, _ in ins], *consts)
    return res


def _full(a):
    return (a, a.shape[1], 0)


def _cmul(ar, ai, br, bi):
    return ar * br - ai * bi, ar * bi + ai * br


def _seg_scan(re_ref, im_ref, a_re, a_im, seg, reverse):
    n_k = len(a_re)
    assert seg & (seg - 1) == 0
    ab = [(jnp.broadcast_to(a_re[k], (SUBLANES, LANES)), jnp.broadcast_to(a_im[k], (SUBLANES, LANES)))
          for k in range(n_k)]

    def slab(i):
        j = seg - 1 - i if reverse else i
        return pl.ds(pl.multiple_of(j * SUBLANES, SUBLANES), SUBLANES)

    def local(i, carry):
        out = []
        for k in range(n_k):
            hr, hi = _cmul(ab[k][0], ab[k][1], carry[2 * k], carry[2 * k + 1])
            hr = hr + re_ref[k, slab(i), :]
            hi = hi + im_ref[k, slab(i), :]
            re_ref[k, slab(i), :] = hr
            im_ref[k, slab(i), :] = hi
            out += [hr, hi]
        return tuple(out)

    zero = jnp.zeros((SUBLANES, LANES), F32)
    end = lax.fori_loop(0, seg, local, (zero,) * (2 * n_k), unroll=SCAN_UNROLL)

    row = lax.broadcasted_iota(jnp.int32, (SUBLANES, LANES), 0)
    edge = SUBLANES - 1 if reverse else 0
    shift = SUBLANES - 1 if reverse else 1
    enter = []
    for k in range(n_k):
        pr, pi = ab[k]
        for _ in range(seg.bit_length() - 1):
            pr, pi = _cmul(pr, pi, pr, pi)
        tr_, ti_ = zero, zero
        for _ in range(SUBLANES - 1):
            vr, vi = _cmul(pr, pi, tr_, ti_)
            tr_ = jnp.where(row == edge, 0.0, pltpu.roll(vr + end[2 * k], shift, 0))
            ti_ = jnp.where(row == edge, 0.0, pltpu.roll(vi + end[2 * k + 1], shift, 0))
        enter += [tr_, ti_]

    def fix(i, carry):
        out = []
        for k in range(n_k):
            er, ei = _cmul(ab[k][0], ab[k][1], carry[2 * k], carry[2 * k + 1])
            re_ref[k, slab(i), :] += er
            im_ref[k, slab(i), :] += ei
            out += [er, ei]
        return tuple(out)

    lax.fori_loop(0, seg, fix, tuple(enter), unroll=SCAN_UNROLL)
    return enter


def _to_slab(a):
    s, w = a.shape
    return a.reshape(SUBLANES, s // SUBLANES, w).swapaxes(0, 1).reshape(s, w)


def _from_slab(a):
    s, w = a.shape
    return a.reshape(s // SUBLANES, SUBLANES, w).swapaxes(0, 1).reshape(s, w)


def _lane_blocks(v, n_k):
    return [v[:, k * LANES:(k + 1) * LANES] for k in range(n_k)]


def _gather_k(ref, rows, n_k):
    return jnp.concatenate([ref[k, rows, :] for k in range(n_k)], axis=1)


def _dot(a, b, dims=(((1,), (0,)), ((), ()))):
    return lax.dot_general(a.astype(BF16), b.astype(BF16), dims, preferred_element_type=F32)


_NT = (((1,), (1,)), ((), ()))
_TN = (((0,), (0,)), ((), ()))


def _ssm_fwd(uz, b_blk, c_blk, par, d_skip):
    seq = uz.shape[0]
    width = d_skip.shape[1]
    ns = SSM_CH // S5_GROUP * S5_STATE
    n_k = ns // LANES
    seg = seq // SUBLANES
    tb = min(SCAN_ROWS, seq)

    def body(u_ref, b_ref, c_ref, par_ref, d_ref, y_ref, hre, him):
        coef_r, coef_i = par_ref[0, 2:3, :], par_ref[0, 3:4, :]
        for c0 in range(0, seq, tb):
            rows = pl.ds(c0, tb)
            ub = u_ref[rows, :]
            bur, bui = _dot(ub, b_ref[0, 0]), _dot(ub, b_ref[0, 1])
            xr, xi = coef_r * bur - coef_i * bui, coef_r * bui + coef_i * bur
            for k in range(n_k):
                hre[k, rows, :] = xr[:, k * LANES:(k + 1) * LANES]
                him[k, rows, :] = xi[:, k * LANES:(k + 1) * LANES]
        _seg_scan(hre, him, _lane_blocks(par_ref[0, 0:1, :], n_k), _lane_blocks(par_ref[0, 1:2, :], n_k), seg, False)
        for c0 in range(0, seq, tb):
            rows = pl.ds(c0, tb)
            y = _dot(_gather_k(hre, rows, n_k), c_ref[0, 0]) - _dot(_gather_k(him, rows, n_k), c_ref[0, 1])
            y_ref[rows, :] = y + d_ref[...] * u_ref[rows, :]

    n_cb = width // SSM_CH
    return _pcall(
        body, name="ssm_fwd", grid=(n_cb,),
        in_specs=[pl.BlockSpec((seq, SSM_CH), lambda i: (0, i)),
                  pl.BlockSpec((1, 2, SSM_CH, ns), lambda i: (i, 0, 0, 0)),
                  pl.BlockSpec((1, 2, ns, SSM_CH), lambda i: (i, 0, 0, 0)),
                  pl.BlockSpec((1, 4, ns), lambda i: (i, 0, 0)),
                  pl.BlockSpec((1, SSM_CH), lambda i: (0, i))],
        out_specs=pl.BlockSpec((seq, SSM_CH), lambda i: (0, i)),
        out_shape=jax.ShapeDtypeStruct((seq, width), F32),
        scratch_shapes=[pltpu.VMEM((n_k, seq, LANES), F32), pltpu.VMEM((n_k, seq, LANES), F32)],
        compiler_params=_params(("parallel",)),
    )(uz, b_blk, c_blk, par, d_skip)


def _ssm_bwd(uz, dy1, ys, b_blk, ct_blk, par, d_skip):
    seq = uz.shape[0]
    width = d_skip.shape[1]
    ns_all = SSM_CH // S5_GROUP * S5_STATE
    n_half = 2
    ns = ns_all // n_half
    n_k = ns // LANES
    seg = seq // SUBLANES
    tb = min(SCAN_ROWS, seq)

    def body(u_ref, dy_ref, ys_ref, b_ref, ct_ref, par_ref, d_ref,
             du_ref, dbt_ref, dct_ref, dpar_ref, dd_ref, hre, him, gre, gim):
        half = pl.program_id(1)
        a_r, a_i = par_ref[0, 0:1, :], par_ref[0, 1:2, :]
        coef_r, coef_i = par_ref[0, 2:3, :], par_ref[0, 3:4, :]

        def dys_of(rows):
            return dy_ref[rows, :] * _dgelu(ys_ref[rows, :])

        for c0 in range(0, seq, tb):
            rows = pl.ds(c0, tb)
            ub = u_ref[rows, :]
            bur, bui = _dot(ub, b_ref[0, 0]), _dot(ub, b_ref[0, 1])
            xr, xi = coef_r * bur - coef_i * bui, coef_r * bui + coef_i * bur
            dys = dys_of(rows)
            gr, gi = _dot(dys, ct_ref[0, 0]), -_dot(dys, ct_ref[0, 1])
            for k in range(n_k):
                lanes = slice(k * LANES, (k + 1) * LANES)
                hre[k, rows, :] = xr[:, lanes]
                him[k, rows, :] = xi[:, lanes]
                gre[k, rows, :] = gr[:, lanes]
                gim[k, rows, :] = gi[:, lanes]
        enter = _seg_scan(hre, him, _lane_blocks(a_r, n_k), _lane_blocks(a_i, n_k), seg, False)
        _seg_scan(gre, gim, _lane_blocks(a_r, n_k), _lane_blocks(-a_i, n_k), seg, True)

        def corr(j, carry):
            acc, prev = carry
            acc_o, prev_o = [], []
            for k in range(n_k):
                sl = pl.ds(pl.multiple_of(j * SUBLANES, SUBLANES), SUBLANES)
                g_r, g_i = gre[k, sl, :], gim[k, sl, :]
                p_r, p_i = prev[2 * k], prev[2 * k + 1]
                acc_o += [acc[2 * k] + g_r * p_r + g_i * p_i, acc[2 * k + 1] + g_i * p_r - g_r * p_i]
                prev_o += [hre[k, sl, :], him[k, sl, :]]
            return tuple(acc_o), tuple(prev_o)

        zero = jnp.zeros((SUBLANES, LANES), F32)
        acc, _ = lax.fori_loop(0, seg, corr, ((zero,) * (2 * n_k), tuple(enter)), unroll=SCAN_UNROLL)
        da_r = jnp.concatenate([_colsum(acc[2 * k]) for k in range(n_k)], axis=1)
        da_i = jnp.concatenate([_colsum(acc[2 * k + 1]) for k in range(n_k)], axis=1)

        zeros_cn = jnp.zeros((SSM_CH, ns), F32)
        qt_r, qt_i, dct_r, dct_i = zeros_cn, zeros_cn, zeros_cn, zeros_cn
        dd = jnp.zeros((1, SSM_CH), F32)
        first = half == 0
        for c0 in range(0, seq, tb):
            rows = pl.ds(c0, tb)
            ub = u_ref[rows, :]
            dys = dys_of(rows)
            dct_r = dct_r + _dot(dys, _gather_k(hre, rows, n_k), _TN)
            dct_i = dct_i - _dot(dys, _gather_k(him, rows, n_k), _TN)
            g_r, g_i = _gather_k(gre, rows, n_k), _gather_k(gim, rows, n_k)
            qt_r = qt_r + _dot(ub, g_r, _TN)
            qt_i = qt_i + _dot(ub, g_i, _TN)
            dbu_r, dbu_i = coef_r * g_r + coef_i * g_i, coef_r * g_i - coef_i * g_r
            du = _dot(dbu_r, b_ref[0, 0], _NT) + _dot(dbu_i, b_ref[0, 1], _NT)
            dd = dd + _colsum(dys * ub)

            @pl.when(first)
            def _(du=du, dys=dys, rows=rows):
                du_ref[rows, :] = du + d_ref[...] * dys

            @pl.when(jnp.logical_not(first))
            def _(du=du, rows=rows):
                du_ref[rows, :] += du

        @pl.when(first)
        def _():
            dd_ref[...] = dd

        b_r, b_i = b_ref[0, 0], b_ref[0, 1]
        dbt_ref[0, 0] = coef_r * qt_r + coef_i * qt_i
        dbt_ref[0, 1] = coef_r * qt_i - coef_i * qt_r
        dct_ref[0, 0] = dct_r
        dct_ref[0, 1] = dct_i
        dpar_ref[0, 0:1, :] = da_r
        dpar_ref[0, 1:2, :] = da_i
        dpar_ref[0, 2:3, :] = _colsum(b_r * qt_r + b_i * qt_i)
        dpar_ref[0, 3:4, :] = _colsum(b_r * qt_i - b_i * qt_r)

    n_cb = width // SSM_CH
    blk = lambda i, h: (0, i)
    return _pcall(
        body, name="ssm_bwd", grid=(n_cb, n_half),
        in_specs=[pl.BlockSpec((seq, SSM_CH), blk), pl.BlockSpec((seq, SSM_CH), blk), pl.BlockSpec((seq, SSM_CH), blk),
                  pl.BlockSpec((1, 2, SSM_CH, ns), lambda i, h: (i, 0, 0, h)),
                  pl.BlockSpec((1, 2, SSM_CH, ns), lambda i, h: (i, 0, 0, h)),
                  pl.BlockSpec((1, 4, ns), lambda i, h: (i, 0, h)),
                  pl.BlockSpec((1, SSM_CH), blk)],
        out_specs=[pl.BlockSpec((seq, SSM_CH), blk),
                   pl.BlockSpec((1, 2, SSM_CH, ns), lambda i, h: (i, 0, 0, h)),
                   pl.BlockSpec((1, 2, SSM_CH, ns), lambda i, h: (i, 0, 0, h)),
                   pl.BlockSpec((1, 4, ns), lambda i, h: (i, 0, h)),
                   pl.BlockSpec((1, SSM_CH), blk)],
        out_shape=[jax.ShapeDtypeStruct((seq, width), F32),
                   jax.ShapeDtypeStruct((n_cb, 2, SSM_CH, ns_all), F32),
                   jax.ShapeDtypeStruct((n_cb, 2, SSM_CH, ns_all), F32),
                   jax.ShapeDtypeStruct((n_cb, 4, ns_all), F32),
                   jax.ShapeDtypeStruct((1, width), F32)],
        scratch_shapes=[pltpu.VMEM((n_k, seq, LANES), F32) for _ in range(4)],
        compiler_params=_params(("parallel", "arbitrary")),
    )(uz, dy1, ys, b_blk, ct_blk, par, d_skip)


def _ssm_discretize(a_re, a_im, log_dt):
    dt = jnp.exp(log_dt)[:, None]
    mag = jnp.exp(a_re * dt)
    abar_re = mag * jnp.cos(a_im * dt)
    abar_im = mag * jnp.sin(a_im * dt)
    den = a_re * a_re + a_im * a_im
    nr = abar_re - 1.0
    coef_re = (nr * a_re + abar_im * a_im) / den
    coef_im = (abar_im * a_re - nr * a_im) / den
    return abar_re, abar_im, coef_re, coef_im


def _block_diag(w_gcp):
    gpb = SSM_CH // S5_GROUP
    n_cb = w_gcp.shape[0] // gpb
    w = w_gcp.reshape(n_cb, gpb, S5_GROUP, 1, S5_STATE)
    eye = jnp.eye(gpb, dtype=w.dtype)[None, :, None, :, None]
    return (w * eye).reshape(n_cb, SSM_CH, gpb * S5_STATE)


def _block_diag_extract(w_blk):
    gpb = SSM_CH // S5_GROUP
    n_cb = w_blk.shape[0]
    w = w_blk.reshape(n_cb, gpb, S5_GROUP, gpb, S5_STATE)
    w = jnp.moveaxis(jnp.diagonal(w, axis1=1, axis2=3), -1, 1)
    return w.reshape(n_cb * gpb, S5_GROUP, S5_STATE)


def _split3(x):
    hi = x.astype(BF16)
    mid = (x - hi.astype(F32)).astype(BF16)
    lo = (x - hi.astype(F32) - mid.astype(F32)).astype(BF16)
    return hi, mid, lo


def _tri_sum(tri, x):
    hi, mid, lo = _split3(x)
    return (jnp.dot(tri, hi, preferred_element_type=F32) + jnp.dot(tri, mid, preferred_element_type=F32)
            + jnp.dot(tri, lo, preferred_element_type=F32))


def _log_sigmoid(x):
    return jnp.minimum(x, 0.0) - jnp.log(1.0 + jnp.exp(-jnp.abs(x)))


def _cum_fwd(fl, b_f):
    seq = fl.shape[0]
    t = min(CUM_TILE, seq)

    def body(fl_ref, b_ref, o_ref, carry):
        @pl.when(pl.program_id(0) == 0)
        def _():
            carry[...] = jnp.zeros_like(carry)

        r = lax.broadcasted_iota(jnp.int32, (t, t), 0)
        c = lax.broadcasted_iota(jnp.int32, (t, t), 1)
        tri = (c <= r).astype(BF16)
        cum = _tri_sum(tri, _log_sigmoid(fl_ref[...] + b_ref[...])) + carry[...]
        o_ref[...] = cum
        carry[...] = cum[t - 1:t, :]

    return _pcall(
        body, name="cum_fwd", grid=(seq // t,),
        in_specs=[pl.BlockSpec((t, LANES), lambda i: (i, 0)), pl.BlockSpec((1, LANES), lambda i: (0, 0))],
        out_specs=pl.BlockSpec((t, LANES), lambda i: (i, 0)),
        out_shape=jax.ShapeDtypeStruct((seq, LANES), F32),
        scratch_shapes=[pltpu.VMEM((1, LANES), F32)],
        compiler_params=_params(("arbitrary",)),
    )(fl, b_f)


def _cum_bwd(dcum, fl, b_f):
    seq = fl.shape[0]
    t = min(CUM_TILE, seq)
    nb = seq // t

    def body(dc_ref, fl_ref, b_ref, o_ref, db_ref, carry):
        @pl.when(pl.program_id(0) == 0)
        def _():
            carry[...] = jnp.zeros_like(carry)
            db_ref[...] = jnp.zeros_like(db_ref)

        r = lax.broadcasted_iota(jnp.int32, (t, t), 0)
        c = lax.broadcasted_iota(jnp.int32, (t, t), 1)
        tri = (c >= r).astype(BF16)
        rev = _tri_sum(tri, dc_ref[...]) + carry[...]
        carry[...] = rev[0:1, :]
        dfl = rev * _sigmoid(-(fl_ref[...] + b_ref[...]))
        o_ref[...] = dfl
        db_ref[...] += _colsum(dfl)

    return _pcall(
        body, name="cum_bwd", grid=(nb,),
        in_specs=[pl.BlockSpec((t, LANES), lambda i: (nb - 1 - i, 0)), pl.BlockSpec((t, LANES), lambda i: (nb - 1 - i, 0)),
                  pl.BlockSpec((1, LANES), lambda i: (0, 0))],
        out_specs=[pl.BlockSpec((t, LANES), lambda i: (nb - 1 - i, 0)), pl.BlockSpec((1, LANES), lambda i: (0, 0))],
        out_shape=[jax.ShapeDtypeStruct((seq, LANES), F32), jax.ShapeDtypeStruct((1, LANES), F32)],
        scratch_shapes=[pltpu.VMEM((1, LANES), F32)],
        compiler_params=_params(("arbitrary",)),
    )(dcum, fl, b_f)


def _att_scores(q, kb, ck, scale, row0, col0, masked):
    s = _dot(q, kb, _NT) * scale - ck
    if masked:
        rows = row0 + lax.broadcasted_iota(jnp.int32, s.shape, 0)
        cols = col0 + lax.broadcasted_iota(jnp.int32, s.shape, 1)
        s = jnp.where(cols <= rows, s, NEG_INF)
    return s


def _att_fwd(qz, kv, ck):
    seq = qz.shape[0]
    heads = ck.shape[0]
    t = min(ATT_TILE, seq)
    scale = HEAD_DIM ** -0.5

    def body(q_ref, k_ref, v_ref, ck_ref, o_ref, lse_ref):
        i = pl.program_id(1)
        q = q_ref[...].astype(BF16)

        def block(j, carry, masked):
            m, l, acc = carry
            rows = pl.ds(pl.multiple_of(j * t, t), t)
            s = _att_scores(q, k_ref[rows, :], ck_ref[0, j], scale, i * t, j * t, masked)
            m_new = jnp.maximum(m, jnp.max(s, axis=1, keepdims=True))
            p = jnp.exp(s - m_new)
            alpha = jnp.exp(m - m_new)
            return m_new, alpha * l + jnp.sum(p, axis=1, keepdims=True), alpha * acc + _dot(p, v_ref[rows, :])

        init = (jnp.full((t, 1), NEG_INF, F32), jnp.zeros((t, 1), F32), jnp.zeros((t, HEAD_DIM), F32))
        carry = lax.fori_loop(0, i, functools.partial(block, masked=False), init)
        m, l, acc = block(i, carry, True)
        o_ref[...] = acc / l
        lse_ref[0] = m + jnp.log(l)

    return _pcall(
        body, name="att_fwd", grid=(heads, seq // t),
        in_specs=[pl.BlockSpec((t, HEAD_DIM), lambda h, i: (i, h)),
                  pl.BlockSpec((seq, HEAD_DIM), lambda h, i: (0, h)),
                  pl.BlockSpec((seq, HEAD_DIM), lambda h, i: (0, heads + h)),
                  pl.BlockSpec((1, seq // t, 1, t), lambda h, i: (h, 0, 0, 0))],
        out_specs=[pl.BlockSpec((t, HEAD_DIM), lambda h, i: (i, h)), pl.BlockSpec((1, t, 1), lambda h, i: (h, i, 0))],
        out_shape=[jax.ShapeDtypeStruct((seq, heads * HEAD_DIM), F32), jax.ShapeDtypeStruct((heads, seq, 1), F32)],
        compiler_params=_params(("parallel", "parallel")),
    )(qz, kv, kv, ck)


def _att_bwd_q(qz, kv, do, o, lse, ck):
    seq = qz.shape[0]
    heads = ck.shape[0]
    t = min(ATT_TILE, seq)
    scale = HEAD_DIM ** -0.5

    def body(q_ref, k_ref, v_ref, do_ref, o_ref, lse_ref, ck_ref, dq_ref, delta_ref):
        i = pl.program_id(1)
        q = q_ref[...].astype(BF16)
        dob = do_ref[...].astype(BF16)
        delta = jnp.sum(do_ref[...] * o_ref[...], axis=1, keepdims=True)
        lse_v = lse_ref[0]

        def block(j, carry, masked):
            dq, pdp = carry
            rows = pl.ds(pl.multiple_of(j * t, t), t)
            kb = k_ref[rows, :]
            s = _att_scores(q, kb, ck_ref[0, j], scale, i * t, j * t, masked)
            p = jnp.exp(s - lse_v)
            dp = _dot(dob, v_ref[rows, :], _NT)
            ds = p * (dp - delta)
            return dq + _dot(ds * scale, kb), pdp + jnp.sum(p * dp, axis=1, keepdims=True)

        init = (jnp.zeros((t, HEAD_DIM), F32), jnp.zeros((t, 1), F32))
        carry = lax.fori_loop(0, i, functools.partial(block, masked=False), init)
        dq, pdp = block(i, carry, True)
        dq_ref[...] = dq.astype(dq_ref.dtype)
        delta_ref[0] = pdp

    qblk = pl.BlockSpec((t, HEAD_DIM), lambda h, i: (i, h))
    col = pl.BlockSpec((1, t, 1), lambda h, i: (h, i, 0))
    return _pcall(
        body, name="att_bwd_q", grid=(heads, seq // t),
        in_specs=[qblk, pl.BlockSpec((seq, HEAD_DIM), lambda h, i: (0, h)),
                  pl.BlockSpec((seq, HEAD_DIM), lambda h, i: (0, heads + h)), qblk, qblk, col,
                  pl.BlockSpec((1, seq // t, 1, t), lambda h, i: (h, 0, 0, 0))],
        out_specs=[qblk, col],
        out_shape=[jax.ShapeDtypeStruct((seq, heads * HEAD_DIM), BF16), jax.ShapeDtypeStruct((heads, seq, 1), F32)],
        compiler_params=_params(("parallel", "parallel")),
    )(qz, kv, kv, do, o, lse, ck)


def _att_bwd_kv(qz, kv, do, lse, delta, ck):
    seq = qz.shape[0]
    heads = ck.shape[0]
    t = min(ATT_TILE, seq)
    nq = seq // t
    scale = HEAD_DIM ** -0.5

    def body(q_ref, k_ref, v_ref, do_ref, lse_ref, delta_ref, ck_ref, dk_ref, dv_ref, dck_ref):
        j = pl.program_id(1)
        kb, vb = k_ref[...], v_ref[...]
        ckv = ck_ref[0, 0]

        def block(i, carry, masked):
            dk, dv, dck = carry
            rows = pl.ds(pl.multiple_of(i * t, t), t)
            qb = q_ref[rows, :].astype(BF16)
            dob = do_ref[rows, :].astype(BF16)
            s = _att_scores(qb, kb, ckv, scale, i * t, j * t, masked)
            p = jnp.exp(s - lse_ref[0, rows, :])
            ds = p * (_dot(dob, vb, _NT) - delta_ref[0, rows, :])
            return dk + _dot(ds * scale, qb, _TN), dv + _dot(p, dob, _TN), dck - _colsum(ds)

        init = (jnp.zeros((t, HEAD_DIM), F32), jnp.zeros((t, HEAD_DIM), F32), jnp.zeros((1, t), F32))
        carry = block(j, init, True)
        dk, dv, dck = lax.fori_loop(j + 1, nq, functools.partial(block, masked=False), carry)
        dk_ref[...] = dk.astype(dk_ref.dtype)
        dv_ref[...] = dv.astype(dv_ref.dtype)
        dck_ref[0, 0] = dck

    head = pl.BlockSpec((seq, HEAD_DIM), lambda h, j: (0, h))
    col = pl.BlockSpec((1, seq, 1), lambda h, j: (h, 0, 0))
    kblk = pl.BlockSpec((t, HEAD_DIM), lambda h, j: (j, h))
    row = pl.BlockSpec((1, 1, 1, t), lambda h, j: (h, j, 0, 0))
    return _pcall(
        body, name="att_bwd_kv", grid=(heads, nq),
        in_specs=[head, kblk, pl.BlockSpec((t, HEAD_DIM), lambda h, j: (j, heads + h)), head, col, col, row],
        out_specs=[kblk, kblk, row],
        out_shape=[jax.ShapeDtypeStruct((seq, heads * HEAD_DIM), BF16), jax.ShapeDtypeStruct((seq, heads * HEAD_DIM), BF16),
                   jax.ShapeDtypeStruct((heads, nq, 1, t), F32)],
        compiler_params=_params(("parallel", "parallel")),
    )(qz, kv, kv, do, lse, delta, ck)


ANY = pl.BlockSpec(memory_space=pl.ANY)


def _mesh_pos():
    return lax.axis_index("x"), lax.axis_index("y"), lax.axis_index("c")


def _other_chips(x, y):
    return [(1 - x, y), (x, 1 - y), (1 - x, 1 - y)]


def _all_gather_weights(big, small):
    nb, ns = len(big), len(small)
    n_remote = 3 * (nb + ns)

    def body(*refs):
        ins, outs = refs[:nb + ns], refs[nb + ns:2 * (nb + ns)]
        send_sems, recv_sems, fwd_send, fwd_recv = refs[2 * (nb + ns):]
        x, y, c = _mesh_pos()
        me = 2 * x + y
        chips = _other_chips(x, y)
        sibling = (x, y, 1 - c)

        def half(ref, hc):
            rh = ref.shape[-2] // 2
            return ref.at[pl.ds(hc * rh, rh), :]

        def remote(i, j, src_chip, to, from_in):
            if i < nb:
                src = half(ins[i], c) if from_in else half(outs[i].at[src_chip], c)
                dst = half(outs[i].at[src_chip], c)
            else:
                src = ins[i] if from_in else outs[i].at[src_chip]
                dst = outs[i].at[src_chip]
            k = 3 * i + j
            return pltpu.make_async_remote_copy(src_ref=src, dst_ref=dst, send_sem=send_sems.at[k],
                                                recv_sem=recv_sems.at[k], device_id=to, device_id_type=MESH_ID)

        def forward(i, j, src_chip, hc):
            part = half(outs[i].at[src_chip], hc)
            k = 3 * i + j
            return pltpu.make_async_remote_copy(src_ref=part, dst_ref=part, send_sem=fwd_send.at[k],
                                                recv_sem=fwd_recv.at[k], device_id=sibling, device_id_type=MESH_ID)

        sends = [remote(i, j, me, (*chips[j], c), True) for i in range(nb + ns) for j in range(3)]
        for cp in sends:
            cp.start()
        fwds = []
        for i in range(nb + ns):
            for j in range(3):
                src_chip = 2 * chips[j][0] + chips[j][1]
                remote(i, j, src_chip, (*chips[j], c), False).wait_recv()
                if i < nb:
                    cp = forward(i, j, src_chip, c)
                    cp.start()
                    fwds.append(cp)
        for i in range(nb):
            for j in range(3):
                src_chip = 2 * chips[j][0] + chips[j][1]
                forward(i, j, src_chip, 1 - c).wait_recv()
        for cp in sends + fwds:
            cp.wait_send()

    arrays = list(big) + list(small)
    return _pcall(
        body, name="all_gather_weights",
        in_specs=[ANY] * (nb + ns), out_specs=[ANY] * (nb + ns),
        out_shape=[jax.ShapeDtypeStruct((N_CHIPS,) + a.shape, a.dtype) for a in arrays],
        scratch_shapes=[pltpu.SemaphoreType.DMA((n_remote,)), pltpu.SemaphoreType.DMA((n_remote,)),
                        pltpu.SemaphoreType.DMA((3 * nb,)), pltpu.SemaphoreType.DMA((3 * nb,))],
        compiler_params=pltpu.CompilerParams(has_side_effects=True),
    )(*arrays)


def _swap_halves(grads):
    n = len(grads)

    def body(*refs):
        ins, outs, send_sems, recv_sems = refs[:n], refs[n:2 * n], refs[2 * n], refs[2 * n + 1]
        x, y, c = _mesh_pos()
        cps = []
        for i in range(n):
            rh = ins[i].shape[1] // 2
            cps.append(pltpu.make_async_remote_copy(
                src_ref=ins[i].at[:, pl.ds((1 - c) * rh, rh), :], dst_ref=outs[i], send_sem=send_sems.at[i],
                recv_sem=recv_sems.at[i], device_id=(x, y, 1 - c), device_id_type=MESH_ID))
        for cp in cps:
            cp.start()
        for cp in cps:
            cp.wait()

    return _pcall(
        body, name="grad_swap_halves", in_specs=[ANY] * n, out_specs=[ANY] * n,
        out_shape=[jax.ShapeDtypeStruct((g.shape[0], g.shape[1] // 2, g.shape[2]), g.dtype) for g in grads],
        scratch_shapes=[pltpu.SemaphoreType.DMA((n,)), pltpu.SemaphoreType.DMA((n,))],
        compiler_params=pltpu.CompilerParams(has_side_effects=True),
    )(*grads)


def _pair_sum_bf16(g, theirs, pos, name):
    n, rh, cdim = theirs.shape
    tr = min(ROW_TILE, rh)
    nb = rh // tr

    def body(pos_ref, g_ref, t_ref, o_ref):
        o_ref[...] = (g_ref[...] + t_ref[...]).astype(BF16)

    grid_spec = pltpu.PrefetchScalarGridSpec(
        num_scalar_prefetch=1, grid=(n, nb),
        in_specs=[pl.BlockSpec((None, tr, cdim), lambda s, i, pos: (s, pos[1] * nb + i, 0)),
                  pl.BlockSpec((None, tr, cdim), lambda s, i, pos: (s, i, 0))],
        out_specs=pl.BlockSpec((None, tr, cdim), lambda s, i, pos: (s, i, 0)))
    return _pcall(body, name=name, grid_spec=grid_spec, out_shape=jax.ShapeDtypeStruct(theirs.shape, BF16),
                  compiler_params=_params(("parallel", "parallel")))(pos, g, theirs)


def _chip_sum(g, theirs, recv, pos, name):
    n, rh, cdim = theirs.shape
    tr = min(ROW_TILE, rh)
    nb = rh // tr

    def body(pos_ref, g_ref, t_ref, r0, r1, r2, o_ref):
        o_ref[...] = (((g_ref[...] + t_ref[...]) + r0[...]) + r1[...]) + r2[...]

    grid_spec = pltpu.PrefetchScalarGridSpec(
        num_scalar_prefetch=1, grid=(nb,),
        in_specs=[pl.BlockSpec((None, tr, cdim), lambda i, pos: (pos[0], pos[1] * nb + i, 0)),
                  pl.BlockSpec((None, tr, cdim), lambda i, pos: (pos[0], i, 0))]
        + [pl.BlockSpec((None, tr, cdim), functools.partial(lambda i, pos, j: (j, i, 0), j=j)) for j in range(3)],
        out_specs=pl.BlockSpec((tr, cdim), lambda i, pos: (i, 0)))
    return _pcall(body, name=name, grid_spec=grid_spec, out_shape=jax.ShapeDtypeStruct((rh, cdim), F32),
                  compiler_params=_params(("parallel",)))(pos, g, theirs, recv, recv, recv)


def _scatter_to_owner(parts):
    n = len(parts)

    def body(*refs):
        ins, outs, send_sems, recv_sems = refs[:n], refs[n:2 * n], refs[2 * n], refs[2 * n + 1]
        x, y, c = _mesh_pos()
        chips = _other_chips(x, y)
        cps = []
        for i in range(n):
            for j in range(3):
                k = 3 * i + j
                cps.append(pltpu.make_async_remote_copy(
                    src_ref=ins[i].at[2 * chips[j][0] + chips[j][1]], dst_ref=outs[i].at[j],
                    send_sem=send_sems.at[k], recv_sem=recv_sems.at[k], device_id=(*chips[j], c),
                    device_id_type=MESH_ID))
        for cp in cps:
            cp.start()
        for cp in cps:
            cp.wait()

    return _pcall(
        body, name="grad_scatter", in_specs=[ANY] * n, out_specs=[ANY] * n,
        out_shape=[jax.ShapeDtypeStruct((3,) + p.shape[1:], p.dtype) for p in parts],
        scratch_shapes=[pltpu.SemaphoreType.DMA((3 * n,)), pltpu.SemaphoreType.DMA((3 * n,))],
        compiler_params=pltpu.CompilerParams(has_side_effects=True),
    )(*parts)


def _join_halves(halves):
    n = len(halves)

    def body(*refs):
        ins, outs, send_sems, recv_sems = refs[:n], refs[n:2 * n], refs[2 * n], refs[2 * n + 1]
        x, y, c = _mesh_pos()
        cps = []
        for i in range(n):
            rh = ins[i].shape[0]
            cps.append(pltpu.make_async_remote_copy(
                src_ref=ins[i], dst_ref=outs[i].at[pl.ds(c * rh, rh), :], send_sem=send_sems.at[i],
                recv_sem=recv_sems.at[i], device_id=(x, y, 1 - c), device_id_type=MESH_ID))
        for cp in cps:
            cp.start()
        for cp in cps:
            cp.wait_send()
        for i in range(n):
            rh = ins[i].shape[0]
            theirs = outs[i].at[pl.ds((1 - c) * rh, rh), :]
            pltpu.make_async_remote_copy(src_ref=ins[i], dst_ref=theirs, send_sem=send_sems.at[i],
                                         recv_sem=recv_sems.at[i], device_id=(x, y, 1 - c),
                                         device_id_type=MESH_ID).wait_recv()

    return _pcall(
        body, name="grad_join_halves", in_specs=[ANY] * n, out_specs=[ANY] * n,
        out_shape=[jax.ShapeDtypeStruct((2 * h.shape[0], h.shape[1]), h.dtype) for h in halves],
        scratch_shapes=[pltpu.SemaphoreType.DMA((n,)), pltpu.SemaphoreType.DMA((n,))],
        compiler_params=pltpu.CompilerParams(has_side_effects=True),
    )(*halves)


def _all_reduce_small(v):
    def body(v_ref, o_ref, recv, send_sems, recv_sems):
        x, y, c = _mesh_pos()
        peers = [(x, y, 1 - c), (1 - x, y, c), (x, 1 - y, c)]
        for s, peer in enumerate(peers):
            src = v_ref if s == 0 else o_ref
            cp = pltpu.make_async_remote_copy(src_ref=src, dst_ref=recv.at[s], send_sem=send_sems.at[s],
                                              recv_sem=recv_sems.at[s], device_id=peer, device_id_type=MESH_ID)
            cp.start()
            cp.wait()
            o_ref[...] = src[...] + recv[s]

    vm = pl.BlockSpec(memory_space=pltpu.VMEM)
    return _pcall(
        body, name="all_reduce_small", in_specs=[vm], out_specs=vm,
        out_shape=jax.ShapeDtypeStruct(v.shape, v.dtype),
        scratch_shapes=[pltpu.VMEM((3,) + v.shape, v.dtype), pltpu.SemaphoreType.DMA((3,)), pltpu.SemaphoreType.DMA((3,))],
        compiler_params=pltpu.CompilerParams(vmem_limit_bytes=VMEM_LIMIT, has_side_effects=True),
    )(v)


def _adamw_math(w, g, m, v):
    m = ADAM_B1 * m + (1.0 - ADAM_B1) * g
    v = ADAM_B2 * v + (1.0 - ADAM_B2) * (g * g)
    m_hat = m / (1.0 - ADAM_B1 ** ADAM_STEP)
    v_hat = v / (1.0 - ADAM_B2 ** ADAM_STEP)
    delta = -ADAM_LR * (m_hat / (jnp.sqrt(v_hat) + ADAM_EPS) + ADAM_WD * w)
    return delta, m, v


def _adamw(w, g, m, v, name):
    wd = w.shape[1]
    return _rows(_adamw_math, [_full(w), _full(g), _full(m), _full(v)], [(wd, F32)] * 3, name=name)


def _pack(arrs, rows):
    flat = jnp.concatenate([a.reshape(-1) for a in arrs])
    return jnp.pad(flat, (0, rows * LANES - flat.shape[0])).reshape(rows, LANES)


def _unpack(buf, like):
    flat = buf.reshape(-1)
    out, off = [], 0
    for a in like:
        out.append(flat[off:off + a.size].reshape(a.shape))
        off += a.size
    return out


def kernel(x, norm_pre, norm_post, s5_w_in, s5_a_re, s5_a_im, s5_log_dt, s5_b_re, s5_b_im, s5_c_re, s5_c_im, s5_d, s5_w_glu, s5_b_glu, s5_w_out, kv_norm, kv_w, kv_b_f, fox_w_in, fox_w_out, loss_target, m_norm_pre, m_norm_post, m_s5_w_in, m_s5_a_re, m_s5_a_im, m_s5_log_dt, m_s5_b_re, m_s5_b_im, m_s5_c_re, m_s5_c_im, m_s5_d, m_s5_w_glu, m_s5_b_glu, m_s5_w_out, m_kv_norm, m_kv_w, m_kv_b_f, m_fox_w_in, m_fox_w_out, v_norm_pre, v_norm_post, v_s5_w_in, v_s5_a_re, v_s5_a_im, v_s5_log_dt, v_s5_b_re, v_s5_b_im, v_s5_c_re, v_s5_c_im, v_s5_d, v_s5_w_glu, v_s5_b_glu, v_s5_w_out, v_kv_norm, v_kv_w, v_kv_b_f, v_fox_w_in, v_fox_w_out):
    seq, dm = x.shape[1], x.shape[2]
    width = dm
    heads = dm // HEAD_DIM
    fw = heads * HEAD_DIM
    groups = width // S5_GROUP
    chip = 2 * lax.axis_index("x") + lax.axis_index("y")

    big_shards = [s5_w_in[0], s5_w_glu[0], s5_w_out[0], kv_w, fox_w_in[0], fox_w_out[0]]
    own_shards = [w.astype(BF16) for w in big_shards] + [s5_d, s5_b_glu]
    gathered = _all_gather_weights(own_shards[:6], own_shards[6:])
    gathered = [lax.dynamic_update_slice(g, own[None], (chip, 0, 0)) for g, own in zip(gathered, own_shards)]
    g_win, g_wglu, g_wout, g_kvw, g_fwin, g_fwout, g_d, g_bglu = gathered
    cols = lambda g: jnp.moveaxis(g, 0, 1).reshape(g.shape[1], -1)
    rows = lambda g: g.reshape(-1, g.shape[2])
    w_in, w_glu, w_out = g_win, rows(g_wglu), rows(g_wout)
    kvw_full = cols(g_kvw)
    w_kv = kvw_full[:, :2 * fw]
    w_f = jnp.pad(kvw_full[:, 2 * fw:], ((0, 0), (0, LANES - heads)))
    fw_in, fw_out = g_fwin, rows(g_fwout)
    d_skip, b_glu = cols(g_d), cols(g_bglu)
    b_f = jnp.pad(kv_b_f, (0, LANES - heads)).reshape(1, LANES)

    h0 = x[0]
    target = loss_target[0]
    g_pre0, g_pre1 = norm_pre[0:1], norm_pre[1:2]
    g_post0, g_post1 = norm_post[0:1], norm_post[1:2]
    g_kv = kv_norm.reshape(1, dm)

    a_re, a_im, log_dt = s5_a_re[0], s5_a_im[0], s5_log_dt[0]
    disc, disc_vjp = jax.vjp(_ssm_discretize, a_re, a_im, log_dt)
    gpb = SSM_CH // S5_GROUP
    n_cb = groups // gpb
    par = jnp.stack([p.reshape(n_cb, gpb * S5_STATE) for p in disc], axis=1)
    b_t = lambda b: jnp.swapaxes(b, 1, 2)
    b_blk = jnp.stack([_block_diag(b_t(s5_b_re[0])), _block_diag(b_t(s5_b_im[0]))], axis=1)
    ct_blk = jnp.stack([_block_diag(s5_c_re[0]), _block_diag(s5_c_im[0])], axis=1)
    c_blk = jnp.swapaxes(ct_blk, 2, 3)

    xn1 = _to_slab(_rows(lambda h, g: (h * _rstd(h) * g,), [_full(h0)], [(dm, BF16)], consts=[g_pre0], name="norm_pre0")[0])
    uz = _mm(xn1, w_in, name="s5_in")
    ys = _ssm_fwd(uz, b_blk, c_blk, par, d_skip)
    y1b = _rows(lambda y: (_gelu(y),), [_full(ys)], [(width, BF16)], name="gelu")[0]
    glu_a = _mm(y1b, w_glu, name="s5_glu")

    def gate_fn(y, a, z, b):
        return (_gelu(y) * _sigmoid(a + b) * _silu(z),)

    y3b = _rows(gate_fn, [_full(ys), _full(glu_a), (uz, width, 1)], [(width, BF16)], consts=[b_glu], name="s5_gate")[0]
    o1 = _from_slab(_mm(y3b, w_out, name="s5_out"))

    def mid_fn(h, o, gp, gk, gq):
        h1 = h + o * _rstd(o) * gp
        r = _rstd(h1)
        return h1, h1 * r * gk, h1 * r * gq

    h1, xk, xn2 = _rows(mid_fn, [_full(h0), _full(o1)], [(dm, F32), (dm, BF16), (dm, BF16)],
                        consts=[g_post0, g_kv, g_pre1], name="mid_norms")

    kv = _mm(xk, w_kv, out_dtype=BF16, name="kv_proj")
    fl = _mm(xk, w_f, name="f_proj")
    qz = _mm(xn2, fw_in, name="fox_in")
    cum = _cum_fwd(fl, b_f)
    t_att = min(ATT_TILE, seq)
    cum_t = cum[:, :heads].T
    ck = cum_t.reshape(heads, seq // t_att, 1, t_att)
    o, lse = _att_fwd(qz, kv, ck)
    o2b = _rows(lambda a, z: (a * _silu(z),), [_full(o), (qz, fw, 1)], [(fw, BF16)], name="fox_gate")[0]
    o3 = _mm(o2b, fw_out, name="fox_out")

    def loss_fn(h, o, t, g):
        r = _rstd(o)
        err = h + o * r * g - t
        dh = err * (1.0 / dm)
        do, dg = _rms_bwd(o, g, dh)
        part = 0.5 * jnp.sum(jnp.mean(err * err, axis=-1, keepdims=True), axis=0, keepdims=True)
        return dh, do, jnp.broadcast_to(part, (1, LANES)), _colsum(dg)

    dh2, do3, loss_part, dg_post1 = _rows(loss_fn, [_full(h1), _full(o3), _full(target)], [(dm, F32), (dm, BF16)],
                                          consts=[g_post1], accs=[(1, LANES), (1, dm)], name="loss_head")
    loss = lax.psum(loss_part[0, 0], MESH_AXES)

    do2 = _mm(do3, fw_out, tb=True, name="fox_out_dx")
    dw_fout = _mm(o2b, do3, ta=True, name="fox_out_dw")

    def fox_gate_bwd(d, a, z):
        return d * _silu(z), d * a * _dsilu(z)

    do, dz2 = _rows(fox_gate_bwd, [_full(do2), _full(o), (qz, fw, 1)], [(fw, F32), (fw, BF16)], name="fox_gate_bwd")
    dq, delta = _att_bwd_q(qz, kv, do, o, lse, ck)
    dk, dv, dck = _att_bwd_kv(qz, kv, do, lse, delta, ck)
    dcum = jnp.pad(dck.reshape(heads, seq).T, ((0, 0), (0, LANES - heads)))
    dfl, db_f = _cum_bwd(dcum, fl, b_f)
    dqz = (dq, dz2)
    dkv = (dk, dv)
    dxn2 = _mm(dqz, fw_in, tb=True, name="fox_in_dx")
    dw_fin = _mm(xn2, dqz, ta=True, out_split=N_CHIPS, name="fox_in_dw")
    dxk_f = _mm(dfl, w_f, tb=True, name="f_proj_dx")
    dxk = _mm(dkv, w_kv, tb=True, add=dxk_f, name="kv_proj_dx")
    dw_kv = _mm(xk, dkv, ta=True, name="kv_proj_dw")
    dw_f = _mm(xk, dfl, ta=True, name="f_proj_dw")

    def mid_bwd(d2, h, dq_, dk_, o, gq, gk, gp):
        dxa, dga = _rms_bwd(h, gq, dq_)
        dxb, dgb = _rms_bwd(h, gk, dk_)
        dh = d2 + dxa + dxb
        do_, dgp = _rms_bwd(o, gp, dh)
        return dh, do_, _colsum(dga), _colsum(dgb), _colsum(dgp)

    dh1, do1, dg_pre1, dg_kv, dg_post0 = _rows(
        mid_bwd, [_full(dh2), _full(h1), _full(dxn2), _full(dxk), _full(o1)], [(dm, F32), (dm, BF16)],
        consts=[g_pre1, g_kv, g_post0], accs=[(1, dm)] * 3, name="mid_norms_bwd")

    do1 = _to_slab(do1)
    dy3 = _mm(do1, w_out, tb=True, name="s5_out_dx")
    dw_out = _mm(y3b, do1, ta=True, name="s5_out_dw")

    def gate_bwd(d3, y, a, z, b):
        y1 = _gelu(y)
        gate = _sigmoid(a + b)
        dy2 = d3 * _silu(z)
        da = dy2 * y1 * gate * (1.0 - gate)
        return dy2 * gate, da, d3 * (y1 * gate) * _dsilu(z), _colsum(da)

    dy1_direct, da, dz, db_glu = _rows(gate_bwd, [_full(dy3), _full(ys), _full(glu_a), (uz, width, 1)],
                                       [(width, F32), (width, BF16), (width, BF16)], consts=[b_glu],
                                       accs=[(1, width)], name="s5_gate_bwd")
    dy1 = _mm(da, w_glu, tb=True, add=dy1_direct, name="s5_glu_dx")
    dw_glu = _mm(y1b, da, ta=True, name="s5_glu_dw")
    du, dbt_blk, dct_blk, dpar, dd = _ssm_bwd(uz, dy1, ys, b_blk, ct_blk, par, d_skip)
    duz = (du, dz)
    dxn1 = _from_slab(_mm(duz, w_in, tb=True, name="s5_in_dx"))
    dw_in = _mm(xn1, duz, ta=True, out_split=N_CHIPS, name="s5_in_dw")

    def first_bwd(d1, h, dxn, g):
        dx, dg = _rms_bwd(h, g, dxn)
        return d1 + dx, _colsum(dg)

    grad_x, dg_pre0 = _rows(first_bwd, [_full(dh1), _full(h0), _full(dxn1)], [(dm, F32)], consts=[g_pre0],
                            accs=[(1, dm)], name="norm_pre0_bwd")

    dpar_g = [dpar[:, i, :].reshape(groups, S5_STATE) for i in range(4)]
    da_re, da_im, dlog_dt = disc_vjp(tuple(dpar_g))
    db_re = jnp.swapaxes(_block_diag_extract(dbt_blk[:, 0]), 1, 2)
    db_im = jnp.swapaxes(_block_diag_extract(dbt_blk[:, 1]), 1, 2)
    dc_re = _block_diag_extract(dct_blk[:, 0])
    dc_im = _block_diag_extract(dct_blk[:, 1])

    small_local = [jnp.concatenate([dg_pre0, dg_pre1]), jnp.concatenate([dg_post0, dg_post1]),
                   da_re[None], da_im[None], dlog_dt[None], db_re[None], db_im[None], dc_re[None], dc_im[None],
                   dd, db_glu, dg_kv.reshape(dm), db_f[0, :heads]]
    n_small = sum(a.size for a in small_local)
    small_rows = -(-n_small // (LANES * ROW_TILE)) * ROW_TILE
    small_sum = _unpack(_all_reduce_small(_pack(small_local, small_rows)), small_local)
    (g_norm_pre, g_norm_post, g_a_re, g_a_im, g_log_dt, g_b_re, g_b_im, g_c_re, g_c_im, g_d_full, g_bglu_full,
     g_kv_norm, g_b_f) = small_sum
    shard = width // N_CHIPS
    g_d_own = lax.dynamic_slice(g_d_full, (0, chip * shard), (1, shard))
    g_bglu_own = lax.dynamic_slice(g_bglu_full, (0, chip * shard), (1, shard))

    to_cols = lambda g: jnp.moveaxis(g.reshape(g.shape[0], N_CHIPS, -1), 1, 0)
    to_rows = lambda g: g.reshape(N_CHIPS, -1, g.shape[1])
    dw_kv_full = jnp.concatenate([dw_kv, dw_f[:, :heads]], axis=1)
    big_grads = [dw_in, to_rows(dw_glu), to_rows(dw_out), to_cols(dw_kv_full), dw_fin, to_rows(dw_fout)]
    theirs = _swap_halves(big_grads)
    c_idx = lax.axis_index("c")
    pos = jnp.stack([chip, c_idx]).astype(jnp.int32)
    chip_sums = [_pair_sum_bf16(g, t, pos, f"grad_pair_sum_{i}") for i, (g, t) in enumerate(zip(big_grads, theirs))]
    received = _scatter_to_owner(chip_sums)
    halves = [_chip_sum(g, t, r, pos, f"grad_chip_sum_{i}") for i, (g, t, r) in enumerate(zip(big_grads, theirs, received))]
    joined = _join_halves(halves)
    g_win_s, g_wglu_s, g_wout_s, g_kvw_s, g_fwin_s, g_fwout_s = [
        lax.dynamic_update_slice(j, h, (c_idx * h.shape[0], 0)) for j, h in zip(joined, halves)]

    big_w = big_shards
    big_g = [g_win_s, g_wglu_s, g_wout_s, g_kvw_s, g_fwin_s, g_fwout_s]
    big_m = [m_s5_w_in[0], m_s5_w_glu[0], m_s5_w_out[0], m_kv_w, m_fox_w_in[0], m_fox_w_out[0]]
    big_v = [v_s5_w_in[0], v_s5_w_glu[0], v_s5_w_out[0], v_kv_w, v_fox_w_in[0], v_fox_w_out[0]]
    big_upd = [_adamw(w, g, m, v, f"adamw_{i}") for i, (w, g, m, v) in enumerate(zip(big_w, big_g, big_m, big_v))]

    small_names = ["norm_pre", "norm_post", "s5_a_re", "s5_a_im", "s5_log_dt", "s5_b_re", "s5_b_im", "s5_c_re", "s5_c_im",
                   "s5_d", "s5_b_glu", "kv_norm", "kv_b_f"]
    small_w = [norm_pre, norm_post, s5_a_re, s5_a_im, s5_log_dt, s5_b_re, s5_b_im, s5_c_re, s5_c_im, s5_d, s5_b_glu, kv_norm, kv_b_f]
    small_m = [m_norm_pre, m_norm_post, m_s5_a_re, m_s5_a_im, m_s5_log_dt, m_s5_b_re, m_s5_b_im, m_s5_c_re, m_s5_c_im, m_s5_d, m_s5_b_glu, m_kv_norm, m_kv_b_f]
    small_v = [v_norm_pre, v_norm_post, v_s5_a_re, v_s5_a_im, v_s5_log_dt, v_s5_b_re, v_s5_b_im, v_s5_c_re, v_s5_c_im, v_s5_d, v_s5_b_glu, v_kv_norm, v_kv_b_f]
    small_g = [g_norm_pre, g_norm_post, g_a_re, g_a_im, g_log_dt, g_b_re, g_b_im, g_c_re, g_c_im, g_d_own, g_bglu_own, g_kv_norm, g_b_f]
    small_g = [g.reshape(w.shape) for g, w in zip(small_g, small_w)]
    n_own = sum(a.size for a in small_w)
    own_rows = -(-n_own // (LANES * ROW_TILE)) * ROW_TILE
    pv = _pack(small_v, own_rows)
    pv = jnp.where(jnp.arange(own_rows * LANES).reshape(own_rows, LANES) < n_own, pv, 1.0)
    sd, sm, sv = _adamw(_pack(small_w, own_rows), _pack(small_g, own_rows), _pack(small_m, own_rows), pv, "adamw_small")
    small_delta, small_newm, small_newv = _unpack(sd, small_w), _unpack(sm, small_w), _unpack(sv, small_w)

    order = ["norm_pre", "norm_post", "s5_w_in", "s5_a_re", "s5_a_im", "s5_log_dt", "s5_b_re", "s5_b_im", "s5_c_re", "s5_c_im",
             "s5_d", "s5_w_glu", "s5_b_glu", "s5_w_out", "kv_norm", "kv_w", "kv_b_f", "fox_w_in", "fox_w_out"]
    big_names = ["s5_w_in", "s5_w_glu", "s5_w_out", "kv_w", "fox_w_in", "fox_w_out"]
    big_like = [s5_w_in, s5_w_glu, s5_w_out, kv_w, fox_w_in, fox_w_out]
    grads, deltas, new_m, new_v = {}, {}, {}, {}
    for i, n in enumerate(big_names):
        shp = big_like[i].shape
        grads[n] = big_g[i].reshape(shp)
        deltas[n], new_m[n], new_v[n] = (a.reshape(shp) for a in big_upd[i])
    for i, n in enumerate(small_names):
        grads[n], deltas[n], new_m[n], new_v[n] = small_g[i], small_delta[i], small_newm[i], small_newv[i]

    return (loss, grad_x[None], *[grads[n] for n in order], *[deltas[n] for n in order],
            *[new_m[n] for n in order], *[new_v[n] for n in order])
```

```python
import functools
import math
from typing import Callable, NamedTuple

import jax
import jax.numpy as jnp
from jax import lax
from jax.experimental import pallas as pl
from jax.experimental.pallas import tpu as pltpu

F32 = jnp.float32
BF16 = jnp.bfloat16

D_MODEL = 2048
SEQ = 4096
S5_GROUP = 16
S5_STATE = 64
HEAD_DIM = 128
RMS_EPS = 1e-6
NEG_INF = -1e30
ADAM_LR = 0.001
ADAM_B1 = 0.9
ADAM_B2 = 0.999
ADAM_EPS = 1e-08
ADAM_WD = 0.01
ADAM_STEP = 10

LANES = 128
SUBLANES = 8
VMEM_LIMIT = 56 * 1024 * 1024
N_CHIPS = 4
MESH_AXES = ("x", "y", "c")
MESH_ID = pl.DeviceIdType.MESH

SSM_CH = 128
ROW_TILE = 256
SCAN_ROWS = 512
SCAN_UNROLL = 4
ATT_TILE = 512
CUM_TILE = 512


def _pcall(body, **kw):
    return pl.pallas_call(body, **kw)


def _params(sem=None):
    if sem is None:
        return pltpu.CompilerParams(vmem_limit_bytes=VMEM_LIMIT)
    return pltpu.CompilerParams(vmem_limit_bytes=VMEM_LIMIT, dimension_semantics=sem)


def _sigmoid(x):
    return 1.0 / (1.0 + jnp.exp(-x))


def _silu(z):
    return z * _sigmoid(z)


def _dsilu(z):
    s = _sigmoid(z)
    return s * (1.0 + z * (1.0 - s))


_GELU_C = math.sqrt(2.0 / math.pi)


def _gelu(x):
    return 0.5 * x * (1.0 + jnp.tanh(_GELU_C * (x + 0.044715 * x * x * x)))


def _dgelu(x):
    t = jnp.tanh(_GELU_C * (x + 0.044715 * x * x * x))
    return 0.5 * (1.0 + t) + 0.5 * x * (1.0 - t * t) * _GELU_C * (1.0 + 3.0 * 0.044715 * x * x)


def _rstd(x):
    return lax.rsqrt(jnp.mean(x * x, axis=-1, keepdims=True) + RMS_EPS)


def _rms_bwd(x, g, dy):
    r = _rstd(x)
    dyg = dy * g
    dx = r * dyg - x * (r * r * r) * jnp.mean(dyg * x, axis=-1, keepdims=True)
    return dx, dy * (x * r)


def _colsum(v):
    return jnp.sum(v, axis=0, keepdims=True)


ANY = pl.BlockSpec(memory_space=pl.ANY)


class _Exchange(NamedTuple):
    arrays: tuple
    out_shapes: tuple
    scratch: tuple
    start: Callable
    finish: Callable


def _exchange_call(ex, name):
    n_in, n_out = len(ex.arrays), len(ex.out_shapes)

    def body(*refs):
        ins, outs, sems = refs[:n_in], refs[n_in:n_in + n_out], refs[n_in + n_out:]
        ex.start(ins, outs, sems)
        ex.finish(ins, outs, sems)

    return _pcall(body, name=name, in_specs=[ANY] * n_in, out_specs=[ANY] * n_out, out_shape=list(ex.out_shapes),
                  scratch_shapes=list(ex.scratch), compiler_params=pltpu.CompilerParams(has_side_effects=True))(*ex.arrays)


def _carry(ex, refs, n_fixed_in, n_fixed_out):
    if ex is None:
        return refs, lambda cond: None, lambda cond: None
    n_in, n_out, n_sem = len(ex.arrays), len(ex.out_shapes), len(ex.scratch)
    fixed_in = refs[:n_fixed_in]
    ex_in = refs[n_fixed_in:n_fixed_in + n_in]
    rest = refs[n_fixed_in + n_in:]
    fixed_out = rest[:n_fixed_out]
    ex_out = rest[n_fixed_out:n_fixed_out + n_out]
    scratch = rest[n_fixed_out + n_out:]
    sems = scratch[len(scratch) - n_sem:]

    def start_when(cond):
        pl.when(cond)(lambda: ex.start(ex_in, ex_out, sems))

    def finish_when(cond):
        pl.when(cond)(lambda: ex.finish(ex_in, ex_out, sems))

    return tuple(fixed_in) + tuple(fixed_out) + tuple(scratch[:len(scratch) - n_sem]), start_when, finish_when


def _carry_specs(ex):
    if ex is None:
        return (), [], [], [], []
    return ex.arrays, [ANY] * len(ex.arrays), [ANY] * len(ex.out_shapes), list(ex.out_shapes), list(ex.scratch)


def _mm(a, b, *, ta=False, tb=False, out_dtype=F32, add=None, out_split=1, tm=1024, tn=1024, tk=2048, name):
    def describe(op):
        if isinstance(op, (tuple, list)):
            assert all(p.ndim == 2 and p.shape == op[0].shape for p in op)
            return list(op), op[0].shape[0], op[0].shape[1], False
        if op.ndim == 3:
            return [op], op.shape[1], op.shape[2], True
        return [op], op.shape[0], op.shape[1], False

    a_parts, a_rows, a_pc, a_stack = describe(a)
    b_parts, b_rows, b_pc, b_stack = describe(b)
    a_cols = a_pc * (a.shape[0] if a_stack else len(a_parts))
    b_cols = b_pc * (b.shape[0] if b_stack else len(b_parts))
    k_dim, m_dim = (a_rows, a_cols) if ta else (a_cols, a_rows)
    n_dim, kb = (b_rows, b_cols) if tb else (b_cols, b_rows)
    assert kb == k_dim, (k_dim, kb)
    tm = min(tm, a_pc) if ta else min(tm, m_dim)
    tk = min(tk, k_dim, k_dim if ta else a_pc, b_pc if tb else k_dim)
    tn = min(tn, n_dim // out_split, n_dim if tb else b_pc)
    a_ct, b_ct = (tm if ta else tk), (tk if tb else tn)
    assert m_dim % tm == 0 and n_dim % tn == 0 and k_dim % tk == 0 and a_pc % a_ct == 0 and b_pc % b_ct == 0
    assert (n_dim // out_split) % tn == 0
    nk = k_dim // tk
    dims = (((0 if ta else 1,), (1 if tb else 0,)), ((), ()))
    n_a, n_b = len(a_parts), len(b_parts)
    assert n_a == 1 or n_b == 1

    def operand_specs(parts, stack, rows_t, cols_t, per, row_of, col_of):
        specs = []
        for p in range(len(parts)):
            def col(i, j, k, p=p):
                return jnp.clip(col_of(i, j, k) - p * per, 0, per - 1) if len(parts) > 1 else col_of(i, j, k)
            if stack:
                specs.append(pl.BlockSpec((None, rows_t, cols_t),
                                          lambda i, j, k, col=col: (col(i, j, k) // per, row_of(i, j, k), col(i, j, k) % per)))
            else:
                specs.append(pl.BlockSpec((rows_t, cols_t), lambda i, j, k, col=col: (row_of(i, j, k), col(i, j, k))))
        return specs

    if ta:
        a_specs = operand_specs(a_parts, a_stack, tk, tm, a_pc // tm, lambda i, j, k: k, lambda i, j, k: i)
    else:
        a_specs = operand_specs(a_parts, a_stack, tm, tk, a_pc // tk, lambda i, j, k: i, lambda i, j, k: k)
    if tb:
        b_specs = operand_specs(b_parts, b_stack, tn, tk, b_pc // tk, lambda i, j, k: j, lambda i, j, k: k)
    else:
        b_specs = operand_specs(b_parts, b_stack, tk, tn, b_pc // tn, lambda i, j, k: k, lambda i, j, k: j)

    def body(*refs):
        a_refs, b_refs = refs[:n_a], refs[n_a:n_a + n_b]
        rest = refs[n_a + n_b:]
        c_ref = rest[0] if add is not None else None
        o_ref = rest[1] if add is not None else rest[0]
        acc = None if nk == 1 else rest[-1]
        i, j, k = pl.program_id(0), pl.program_id(1), pl.program_id(2)

        def finish(res):
            if add is not None:
                res = res + c_ref[...]
            o_ref[...] = res.astype(out_dtype)

        def accumulate(a_ref, b_ref):
            prod = lax.dot_general(a_ref[...].astype(BF16), b_ref[...].astype(BF16), dims,
                                   preferred_element_type=F32)
            if nk == 1:
                finish(prod)
                return

            @pl.when(k == 0)
            def _():
                acc[...] = prod

            @pl.when(jnp.logical_and(k > 0, k < nk - 1))
            def _():
                acc[...] += prod

            @pl.when(k == nk - 1)
            def _():
                finish(acc[...] + prod)

        if n_a == 1 and n_b == 1:
            accumulate(a_refs[0], b_refs[0])
        else:
            many, block, per = (a_refs, (i if ta else k), a_pc // a_ct) if n_a > 1 else (b_refs, (k if tb else j), b_pc // b_ct)
            for p, ref in enumerate(many):
                @pl.when(block // per == p)
                def _(ref=ref):
                    accumulate(ref, b_refs[0]) if n_a > 1 else accumulate(a_refs[0], ref)

    per_out = n_dim // out_split // tn
    if out_split > 1:
        o_spec = pl.BlockSpec((None, tm, tn), lambda i, j, k: (j // per_out, i, j % per_out))
        out_shape = jax.ShapeDtypeStruct((out_split, m_dim, n_dim // out_split), out_dtype)
    else:
        o_spec = pl.BlockSpec((tm, tn), lambda i, j, k: (i, j))
        out_shape = jax.ShapeDtypeStruct((m_dim, n_dim), out_dtype)
    in_specs = a_specs + b_specs + ([pl.BlockSpec((tm, tn), lambda i, j, k: (i, j))] if add is not None else [])
    args = tuple(a_parts) + tuple(b_parts) + ((add,) if add is not None else ())
    return _pcall(
        body, name=name, grid=(m_dim // tm, n_dim // tn, nk), in_specs=in_specs, out_specs=o_spec,
        out_shape=out_shape, scratch_shapes=[pltpu.VMEM((tm, tn), F32)] if nk > 1 else [],
        compiler_params=_params(("parallel", "parallel", "arbitrary")),
    )(*args)


def _rows(fn, ins, outs, *, name, consts=(), accs=(), carry=None):
    n_rows = ins[0][0].shape[0]
    tr = min(ROW_TILE, n_rows)
    assert n_rows % tr == 0
    n_in, n_c, n_out = len(ins), len(consts), len(outs)
    n_steps = n_rows // tr
    ex_args, ex_in_specs, ex_out_specs, ex_out_shapes, ex_scratch = _carry_specs(carry)

    def body(*refs):
        refs, start_when, finish_when = _carry(carry, refs, n_in + n_c, n_out + len(accs))
        start_when(pl.program_id(0) == 0)
        _compute(*refs)
        finish_when(pl.program_id(0) == n_steps - 1)

    def _compute(*refs):
        vals = [r[...] for r in refs[:n_in + n_c]]
        res = fn(*vals)
        res = res if isinstance(res, (tuple, list)) else (res,)
        o_refs = refs[n_in + n_c:]
        for r, v in zip(o_refs[:n_out], res[:n_out]):
            r[...] = v.astype(r.dtype)
        if accs:
            first = pl.program_id(0) == 0
            for r, v in zip(o_refs[n_out:], res[n_out:]):
                @pl.when(first)
                def _(r=r, v=v):
                    r[...] = v

                @pl.when(jnp.logical_not(first))
                def _(r=r, v=v):
                    r[...] += v

    in_specs = [pl.BlockSpec((tr, w), functools.partial(lambda i, cb: (i, cb), cb=cb)) for _, w, cb in ins]
    in_specs += [pl.BlockSpec(c.shape, functools.partial(lambda i, nd: (0,) * nd, nd=c.ndim)) for c in consts]
    out_specs = [pl.BlockSpec((tr, w), lambda i: (i, 0)) for w, _ in outs]
    out_specs += [pl.BlockSpec(s, lambda i: (0, 0)) for s in accs]
    out_shape = [jax.ShapeDtypeStruct((n_rows, w), dt) for w, dt in outs]
    out_shape += [jax.ShapeDtypeStruct(s, F32) for s in accs]
    sequential = bool(accs) or carry is not None
    params = _params(("arbitrary",) if sequential else ("parallel",))
    if carry is not None:
        params = pltpu.CompilerParams(vmem_limit_bytes=VMEM_LIMIT, dimension_semantics=("arbitrary",), has_side_effects=True)
    return _pcall(
        body, name=name, grid=(n_steps,), in_specs=in_specs + ex_in_specs, out_specs=out_specs + ex_out_specs,
        out_shape=out_shape + ex_out_shapes, scratch_shapes=ex_scratch, compiler_params=params,
    )(*[a for a, _, _ in ins], *consts, *ex_args)


def _full(a):
    return (a, a.shape[1], 0)


def _cmul(ar, ai, br, bi):
    return ar * br - ai * bi, ar * bi + ai * br


def _seg_scan(re_ref, im_ref, a_re, a_im, seg, reverse):
    n_k = len(a_re)
    assert seg & (seg - 1) == 0
    ab = [(jnp.broadcast_to(a_re[k], (SUBLANES, LANES)), jnp.broadcast_to(a_im[k], (SUBLANES, LANES)))
          for k in range(n_k)]

    def slab(i):
        j = seg - 1 - i if reverse else i
        return pl.ds(pl.multiple_of(j * SUBLANES, SUBLANES), SUBLANES)

    def local(i, carry):
        out = []
        for k in range(n_k):
            hr, hi = _cmul(ab[k][0], ab[k][1], carry[2 * k], carry[2 * k + 1])
            hr = hr + re_ref[k, slab(i), :]
            hi = hi + im_ref[k, slab(i), :]
            re_ref[k, slab(i), :] = hr
            im_ref[k, slab(i), :] = hi
            out += [hr, hi]
        return tuple(out)

    zero = jnp.zeros((SUBLANES, LANES), F32)
    end = lax.fori_loop(0, seg, local, (zero,) * (2 * n_k), unroll=SCAN_UNROLL)

    row = lax.broadcasted_iota(jnp.int32, (SUBLANES, LANES), 0)
    edge = SUBLANES - 1 if reverse else 0
    shift = SUBLANES - 1 if reverse else 1
    enter = []
    for k in range(n_k):
        pr, pi = ab[k]
        for _ in range(seg.bit_length() - 1):
            pr, pi = _cmul(pr, pi, pr, pi)
        tr_, ti_ = zero, zero
        for _ in range(SUBLANES - 1):
            vr, vi = _cmul(pr, pi, tr_, ti_)
            tr_ = jnp.where(row == edge, 0.0, pltpu.roll(vr + end[2 * k], shift, 0))
            ti_ = jnp.where(row == edge, 0.0, pltpu.roll(vi + end[2 * k + 1], shift, 0))
        enter += [tr_, ti_]

    def fix(i, carry):
        out = []
        for k in range(n_k):
            er, ei = _cmul(ab[k][0], ab[k][1], carry[2 * k], carry[2 * k + 1])
            re_ref[k, slab(i), :] += er
            im_ref[k, slab(i), :] += ei
            out += [er, ei]
        return tuple(out)

    lax.fori_loop(0, seg, fix, tuple(enter), unroll=SCAN_UNROLL)
    return enter


def _to_slab(a):
    s, w = a.shape
    return a.reshape(SUBLANES, s // SUBLANES, w).swapaxes(0, 1).reshape(s, w)


def _from_slab(a):
    s, w = a.shape
    return a.reshape(s // SUBLANES, SUBLANES, w).swapaxes(0, 1).reshape(s, w)


def _lane_blocks(v, n_k):
    return [v[:, k * LANES:(k + 1) * LANES] for k in range(n_k)]


def _gather_k(ref, rows, n_k):
    return jnp.concatenate([ref[k, rows, :] for k in range(n_k)], axis=1)


def _dot(a, b, dims=(((1,), (0,)), ((), ()))):
    return lax.dot_general(a.astype(BF16), b.astype(BF16), dims, preferred_element_type=F32)


_NT = (((1,), (1,)), ((), ()))
_TN = (((0,), (0,)), ((), ()))


def _ssm_fwd(uz, b_blk, c_blk, par, d_skip, carry=None):
    seq = uz.shape[0]
    width = d_skip.shape[1]
    ns = SSM_CH // S5_GROUP * S5_STATE
    n_k = ns // LANES
    seg = seq // SUBLANES
    tb = min(SCAN_ROWS, seq)
    n_cb = width // SSM_CH
    ex_args, ex_in_specs, ex_out_specs, ex_out_shapes, ex_scratch = _carry_specs(carry)

    def body(*refs):
        refs, start_when, finish_when = _carry(carry, refs, 5, 2)
        start_when(pl.program_id(0) == 0)
        compute(*refs)
        finish_when(pl.program_id(0) == n_cb - 1)

    def compute(u_ref, b_ref, c_ref, par_ref, d_ref, y_ref, g_ref, hre, him):
        coef_r, coef_i = par_ref[0, 2:3, :], par_ref[0, 3:4, :]
        for c0 in range(0, seq, tb):
            rows = pl.ds(c0, tb)
            ub = u_ref[rows, :]
            bur, bui = _dot(ub, b_ref[0, 0]), _dot(ub, b_ref[0, 1])
            xr, xi = coef_r * bur - coef_i * bui, coef_r * bui + coef_i * bur
            for k in range(n_k):
                hre[k, rows, :] = xr[:, k * LANES:(k + 1) * LANES]
                him[k, rows, :] = xi[:, k * LANES:(k + 1) * LANES]
        _seg_scan(hre, him, _lane_blocks(par_ref[0, 0:1, :], n_k), _lane_blocks(par_ref[0, 1:2, :], n_k), seg, False)
        for c0 in range(0, seq, tb):
            rows = pl.ds(c0, tb)
            y = _dot(_gather_k(hre, rows, n_k), c_ref[0, 0]) - _dot(_gather_k(him, rows, n_k), c_ref[0, 1])
            y = y + d_ref[...] * u_ref[rows, :]
            y_ref[rows, :] = y
            g_ref[rows, :] = _gelu(y).astype(BF16)

    blk = pl.BlockSpec((seq, SSM_CH), lambda i: (0, i))
    params = _params(("parallel",)) if carry is None else pltpu.CompilerParams(
        vmem_limit_bytes=VMEM_LIMIT, dimension_semantics=("arbitrary",), has_side_effects=True)
    return _pcall(
        body, name="ssm_fwd", grid=(n_cb,),
        in_specs=[blk,
                  pl.BlockSpec((1, 2, SSM_CH, ns), lambda i: (i, 0, 0, 0)),
                  pl.BlockSpec((1, 2, ns, SSM_CH), lambda i: (i, 0, 0, 0)),
                  pl.BlockSpec((1, 4, ns), lambda i: (i, 0, 0)),
                  pl.BlockSpec((1, SSM_CH), lambda i: (0, i))] + ex_in_specs,
        out_specs=[blk, blk] + ex_out_specs,
        out_shape=[jax.ShapeDtypeStruct((seq, width), F32), jax.ShapeDtypeStruct((seq, width), BF16)] + ex_out_shapes,
        scratch_shapes=[pltpu.VMEM((n_k, seq, LANES), F32), pltpu.VMEM((n_k, seq, LANES), F32)] + ex_scratch,
        compiler_params=params,
    )(uz, b_blk, c_blk, par, d_skip, *ex_args)


def _ssm_bwd(uz, dy1, ys, b_blk, ct_blk, par, d_skip, carry=None):
    seq = uz.shape[0]
    width = d_skip.shape[1]
    ns_all = SSM_CH // S5_GROUP * S5_STATE
    n_half = 2
    ns = ns_all // n_half
    n_k = ns // LANES
    seg = seq // SUBLANES
    tb = min(SCAN_ROWS, seq)
    n_cb = width // SSM_CH
    ex_args, ex_in_specs, ex_out_specs, ex_out_shapes, ex_scratch = _carry_specs(carry)

    def body(*refs):
        refs, start_when, finish_when = _carry(carry, refs, 7, 5)
        step = pl.program_id(0) * n_half + pl.program_id(1)
        start_when(step == 0)
        compute(*refs)
        finish_when(step == n_cb * n_half - 1)

    def compute(u_ref, dy_ref, ys_ref, b_ref, ct_ref, par_ref, d_ref,
                du_ref, dbt_ref, dct_ref, dpar_ref, dd_ref, hre, him, gre, gim):
        half = pl.program_id(1)
        a_r, a_i = par_ref[0, 0:1, :], par_ref[0, 1:2, :]
        coef_r, coef_i = par_ref[0, 2:3, :], par_ref[0, 3:4, :]

        def dys_of(rows):
            return dy_ref[rows, :] * _dgelu(ys_ref[rows, :])

        for c0 in range(0, seq, tb):
            rows = pl.ds(c0, tb)
            ub = u_ref[rows, :]
            bur, bui = _dot(ub, b_ref[0, 0]), _dot(ub, b_ref[0, 1])
            xr, xi = coef_r * bur - coef_i * bui, coef_r * bui + coef_i * bur
            dys = dys_of(rows)
            gr, gi = _dot(dys, ct_ref[0, 0]), -_dot(dys, ct_ref[0, 1])
            for k in range(n_k):
                lanes = slice(k * LANES, (k + 1) * LANES)
                hre[k, rows, :] = xr[:, lanes]
                him[k, rows, :] = xi[:, lanes]
                gre[k, rows, :] = gr[:, lanes]
                gim[k, rows, :] = gi[:, lanes]
        enter = _seg_scan(hre, him, _lane_blocks(a_r, n_k), _lane_blocks(a_i, n_k), seg, False)
        _seg_scan(gre, gim, _lane_blocks(a_r, n_k), _lane_blocks(-a_i, n_k), seg, True)

        def corr(j, carry):
            acc, prev = carry
            acc_o, prev_o = [], []
            for k in range(n_k):
                sl = pl.ds(pl.multiple_of(j * SUBLANES, SUBLANES), SUBLANES)
                g_r, g_i = gre[k, sl, :], gim[k, sl, :]
                p_r, p_i = prev[2 * k], prev[2 * k + 1]
                acc_o += [acc[2 * k] + g_r * p_r + g_i * p_i, acc[2 * k + 1] + g_i * p_r - g_r * p_i]
                prev_o += [hre[k, sl, :], him[k, sl, :]]
            return tuple(acc_o), tuple(prev_o)

        zero = jnp.zeros((SUBLANES, LANES), F32)
        acc, _ = lax.fori_loop(0, seg, corr, ((zero,) * (2 * n_k), tuple(enter)), unroll=SCAN_UNROLL)
        da_r = jnp.concatenate([_colsum(acc[2 * k]) for k in range(n_k)], axis=1)
        da_i = jnp.concatenate([_colsum(acc[2 * k + 1]) for k in range(n_k)], axis=1)

        zeros_cn = jnp.zeros((SSM_CH, ns), F32)
        qt_r, qt_i, dct_r, dct_i = zeros_cn, zeros_cn, zeros_cn, zeros_cn
        dd = jnp.zeros((1, SSM_CH), F32)
        first = half == 0
        for c0 in range(0, seq, tb):
            rows = pl.ds(c0, tb)
            ub = u_ref[rows, :]
            dys = dys_of(rows)
            dct_r = dct_r + _dot(dys, _gather_k(hre, rows, n_k), _TN)
            dct_i = dct_i - _dot(dys, _gather_k(him, rows, n_k), _TN)
            g_r, g_i = _gather_k(gre, rows, n_k), _gather_k(gim, rows, n_k)
            qt_r = qt_r + _dot(ub, g_r, _TN)
            qt_i = qt_i + _dot(ub, g_i, _TN)
            dbu_r, dbu_i = coef_r * g_r + coef_i * g_i, coef_r * g_i - coef_i * g_r
            du = _dot(dbu_r, b_ref[0, 0], _NT) + _dot(dbu_i, b_ref[0, 1], _NT)
            dd = dd + _colsum(dys * ub)

            @pl.when(first)
            def _(du=du, dys=dys, rows=rows):
                du_ref[rows, :] = du + d_ref[...] * dys

            @pl.when(jnp.logical_not(first))
            def _(du=du, rows=rows):
                du_ref[rows, :] += du

        @pl.when(first)
        def _():
            dd_ref[...] = dd

        b_r, b_i = b_ref[0, 0], b_ref[0, 1]
        dbt_ref[0, 0] = coef_r * qt_r + coef_i * qt_i
        dbt_ref[0, 1] = coef_r * qt_i - coef_i * qt_r
        dct_ref[0, 0] = dct_r
        dct_ref[0, 1] = dct_i
        dpar_ref[0, 0:1, :] = da_r
        dpar_ref[0, 1:2, :] = da_i
        dpar_ref[0, 2:3, :] = _colsum(b_r * qt_r + b_i * qt_i)
        dpar_ref[0, 3:4, :] = _colsum(b_r * qt_i - b_i * qt_r)

    blk = lambda i, h: (0, i)
    params = _params(("parallel", "arbitrary")) if carry is None else pltpu.CompilerParams(
        vmem_limit_bytes=VMEM_LIMIT, dimension_semantics=("arbitrary", "arbitrary"), has_side_effects=True)
    return _pcall(
        body, name="ssm_bwd", grid=(n_cb, n_half),
        in_specs=[pl.BlockSpec((seq, SSM_CH), blk), pl.BlockSpec((seq, SSM_CH), blk), pl.BlockSpec((seq, SSM_CH), blk),
                  pl.BlockSpec((1, 2, SSM_CH, ns), lambda i, h: (i, 0, 0, h)),
                  pl.BlockSpec((1, 2, SSM_CH, ns), lambda i, h: (i, 0, 0, h)),
                  pl.BlockSpec((1, 4, ns), lambda i, h: (i, 0, h)),
                  pl.BlockSpec((1, SSM_CH), blk)] + ex_in_specs,
        out_specs=[pl.BlockSpec((seq, SSM_CH), blk),
                   pl.BlockSpec((1, 2, SSM_CH, ns), lambda i, h: (i, 0, 0, h)),
                   pl.BlockSpec((1, 2, SSM_CH, ns), lambda i, h: (i, 0, 0, h)),
                   pl.BlockSpec((1, 4, ns), lambda i, h: (i, 0, h)),
                   pl.BlockSpec((1, SSM_CH), blk)] + ex_out_specs,
        out_shape=[jax.ShapeDtypeStruct((seq, width), F32),
                   jax.ShapeDtypeStruct((n_cb, 2, SSM_CH, ns_all), F32),
                   jax.ShapeDtypeStruct((n_cb, 2, SSM_CH, ns_all), F32),
                   jax.ShapeDtypeStruct((n_cb, 4, ns_all), F32),
                   jax.ShapeDtypeStruct((1, width), F32)] + ex_out_shapes,
        scratch_shapes=[pltpu.VMEM((n_k, seq, LANES), F32) for _ in range(4)] + ex_scratch,
        compiler_params=params,
    )(uz, dy1, ys, b_blk, ct_blk, par, d_skip, *ex_args)


def _ssm_discretize(a_re, a_im, log_dt):
    dt = jnp.exp(log_dt)[:, None]
    mag = jnp.exp(a_re * dt)
    abar_re = mag * jnp.cos(a_im * dt)
    abar_im = mag * jnp.sin(a_im * dt)
    den = a_re * a_re + a_im * a_im
    nr = abar_re - 1.0
    coef_re = (nr * a_re + abar_im * a_im) / den
    coef_im = (abar_im * a_re - nr * a_im) / den
    return abar_re, abar_im, coef_re, coef_im


def _block_diag(w_gcp):
    gpb = SSM_CH // S5_GROUP
    n_cb = w_gcp.shape[0] // gpb
    w = w_gcp.reshape(n_cb, gpb, S5_GROUP, 1, S5_STATE)
    eye = jnp.eye(gpb, dtype=w.dtype)[None, :, None, :, None]
    return (w * eye).reshape(n_cb, SSM_CH, gpb * S5_STATE)


def _block_diag_extract(w_blk):
    gpb = SSM_CH // S5_GROUP
    n_cb = w_blk.shape[0]
    w = w_blk.reshape(n_cb, gpb, S5_GROUP, gpb, S5_STATE)
    w = jnp.moveaxis(jnp.diagonal(w, axis1=1, axis2=3), -1, 1)
    return w.reshape(n_cb * gpb, S5_GROUP, S5_STATE)


def _split3(x):
    hi = x.astype(BF16)
    mid = (x - hi.astype(F32)).astype(BF16)
    lo = (x - hi.astype(F32) - mid.astype(F32)).astype(BF16)
    return hi, mid, lo


def _tri_sum(tri, x):
    hi, mid, lo = _split3(x)
    return (jnp.dot(tri, hi, preferred_element_type=F32) + jnp.dot(tri, mid, preferred_element_type=F32)
            + jnp.dot(tri, lo, preferred_element_type=F32))


def _log_sigmoid(x):
    return jnp.minimum(x, 0.0) - jnp.log(1.0 + jnp.exp(-jnp.abs(x)))


def _cum_fwd(fl, b_f):
    seq = fl.shape[0]
    t = min(CUM_TILE, seq)

    def body(fl_ref, b_ref, o_ref, carry):
        @pl.when(pl.program_id(0) == 0)
        def _():
            carry[...] = jnp.zeros_like(carry)

        r = lax.broadcasted_iota(jnp.int32, (t, t), 0)
        c = lax.broadcasted_iota(jnp.int32, (t, t), 1)
        tri = (c <= r).astype(BF16)
        cum = _tri_sum(tri, _log_sigmoid(fl_ref[...] + b_ref[...])) + carry[...]
        o_ref[...] = cum
        carry[...] = cum[t - 1:t, :]

    return _pcall(
        body, name="cum_fwd", grid=(seq // t,),
        in_specs=[pl.BlockSpec((t, LANES), lambda i: (i, 0)), pl.BlockSpec((1, LANES), lambda i: (0, 0))],
        out_specs=pl.BlockSpec((t, LANES), lambda i: (i, 0)),
        out_shape=jax.ShapeDtypeStruct((seq, LANES), F32),
        scratch_shapes=[pltpu.VMEM((1, LANES), F32)],
        compiler_params=_params(("arbitrary",)),
    )(fl, b_f)


def _cum_bwd(dcum, fl, b_f):
    seq = fl.shape[0]
    t = min(CUM_TILE, seq)
    nb = seq // t

    def body(dc_ref, fl_ref, b_ref, o_ref, db_ref, carry):
        @pl.when(pl.program_id(0) == 0)
        def _():
            carry[...] = jnp.zeros_like(carry)
            db_ref[...] = jnp.zeros_like(db_ref)

        r = lax.broadcasted_iota(jnp.int32, (t, t), 0)
        c = lax.broadcasted_iota(jnp.int32, (t, t), 1)
        tri = (c >= r).astype(BF16)
        rev = _tri_sum(tri, dc_ref[...]) + carry[...]
        carry[...] = rev[0:1, :]
        dfl = rev * _sigmoid(-(fl_ref[...] + b_ref[...]))
        o_ref[...] = dfl
        db_ref[...] += _colsum(dfl)

    return _pcall(
        body, name="cum_bwd", grid=(nb,),
        in_specs=[pl.BlockSpec((t, LANES), lambda i: (nb - 1 - i, 0)), pl.BlockSpec((t, LANES), lambda i: (nb - 1 - i, 0)),
                  pl.BlockSpec((1, LANES), lambda i: (0, 0))],
        out_specs=[pl.BlockSpec((t, LANES), lambda i: (nb - 1 - i, 0)), pl.BlockSpec((1, LANES), lambda i: (0, 0))],
        out_shape=[jax.ShapeDtypeStruct((seq, LANES), F32), jax.ShapeDtypeStruct((1, LANES), F32)],
        scratch_shapes=[pltpu.VMEM((1, LANES), F32)],
        compiler_params=_params(("arbitrary",)),
    )(dcum, fl, b_f)


def _att_scores(q, kb, ck, row0, col0, masked):
    s = _dot(q, kb, _NT) - ck
    if masked:
        rows = row0 + lax.broadcasted_iota(jnp.int32, s.shape, 0)
        cols = col0 + lax.broadcasted_iota(jnp.int32, s.shape, 1)
        s = jnp.where(cols <= rows, s, NEG_INF)
    return s


def _att_fwd(qz, kv, ck):
    seq = qz.shape[0]
    heads = ck.shape[0]
    t = min(ATT_TILE, seq)
    scale = HEAD_DIM ** -0.5

    def body(q_ref, k_ref, v_ref, ck_ref, o_ref, lse_ref):
        i = pl.program_id(1)
        q = (q_ref[...] * scale).astype(BF16)

        def block(j, carry, masked):
            m, l, acc = carry
            rows = pl.ds(pl.multiple_of(j * t, t), t)
            s = _att_scores(q, k_ref[rows, :], ck_ref[0, j], i * t, j * t, masked)
            m_new = jnp.maximum(m, jnp.max(s, axis=1, keepdims=True))
            p = jnp.exp(s - m_new)
            alpha = jnp.exp(m - m_new)
            return m_new, alpha * l + jnp.sum(p, axis=1, keepdims=True), alpha * acc + _dot(p, v_ref[rows, :])

        init = (jnp.full((t, 1), NEG_INF, F32), jnp.zeros((t, 1), F32), jnp.zeros((t, HEAD_DIM), F32))
        carry = lax.fori_loop(0, i, functools.partial(block, masked=False), init)
        m, l, acc = block(i, carry, True)
        o_ref[...] = acc / l
        lse_ref[0] = m + jnp.log(l)

    return _pcall(
        body, name="att_fwd", grid=(heads, seq // t),
        in_specs=[pl.BlockSpec((t, HEAD_DIM), lambda h, i: (i, h)),
                  pl.BlockSpec((seq, HEAD_DIM), lambda h, i: (0, h)),
                  pl.BlockSpec((seq, HEAD_DIM), lambda h, i: (0, heads + h)),
                  pl.BlockSpec((1, seq // t, 1, t), lambda h, i: (h, 0, 0, 0))],
        out_specs=[pl.BlockSpec((t, HEAD_DIM), lambda h, i: (i, h)), pl.BlockSpec((1, t, 1), lambda h, i: (h, i, 0))],
        out_shape=[jax.ShapeDtypeStruct((seq, heads * HEAD_DIM), F32), jax.ShapeDtypeStruct((heads, seq, 1), F32)],
        compiler_params=_params(("parallel", "parallel")),
    )(qz, kv, kv, ck)


def _att_bwd_q(qz, kv, do, o, lse, ck):
    seq = qz.shape[0]
    heads = ck.shape[0]
    t = min(ATT_TILE, seq)
    scale = HEAD_DIM ** -0.5

    def body(q_ref, k_ref, v_ref, do_ref, o_ref, lse_ref, ck_ref, dq_ref, delta_ref):
        i = pl.program_id(1)
        q = (q_ref[...] * scale).astype(BF16)
        dob = do_ref[...].astype(BF16)
        delta = jnp.sum(do_ref[...] * o_ref[...], axis=1, keepdims=True)
        lse_v = lse_ref[0]

        def block(j, carry, masked):
            dq, pdp = carry
            rows = pl.ds(pl.multiple_of(j * t, t), t)
            kb = k_ref[rows, :]
            s = _att_scores(q, kb, ck_ref[0, j], i * t, j * t, masked)
            p = jnp.exp(s - lse_v)
            dp = _dot(dob, v_ref[rows, :], _NT)
            ds = p * (dp - delta)
            return dq + _dot(ds, kb), pdp + jnp.sum(p * dp, axis=1, keepdims=True)

        init = (jnp.zeros((t, HEAD_DIM), F32), jnp.zeros((t, 1), F32))
        carry = lax.fori_loop(0, i, functools.partial(block, masked=False), init)
        dq, pdp = block(i, carry, True)
        dq_ref[...] = (dq * scale).astype(dq_ref.dtype)
        delta_ref[0] = pdp

    qblk = pl.BlockSpec((t, HEAD_DIM), lambda h, i: (i, h))
    col = pl.BlockSpec((1, t, 1), lambda h, i: (h, i, 0))
    return _pcall(
        body, name="att_bwd_q", grid=(heads, seq // t),
        in_specs=[qblk, pl.BlockSpec((seq, HEAD_DIM), lambda h, i: (0, h)),
                  pl.BlockSpec((seq, HEAD_DIM), lambda h, i: (0, heads + h)), qblk, qblk, col,
                  pl.BlockSpec((1, seq // t, 1, t), lambda h, i: (h, 0, 0, 0))],
        out_specs=[qblk, col],
        out_shape=[jax.ShapeDtypeStruct((seq, heads * HEAD_DIM), BF16), jax.ShapeDtypeStruct((heads, seq, 1), F32)],
        compiler_params=_params(("parallel", "parallel")),
    )(qz, kv, kv, do, o, lse, ck)


def _att_bwd_kv(qz, kv, do, lse, delta, ck):
    seq = qz.shape[0]
    heads = ck.shape[0]
    t = min(ATT_TILE, seq)
    nq = seq // t
    scale = HEAD_DIM ** -0.5

    def body(q_ref, k_ref, v_ref, do_ref, lse_ref, delta_ref, ck_ref, dk_ref, dv_ref, dck_ref):
        j = pl.program_id(1)
        kb, vb = k_ref[...], v_ref[...]
        ckv = ck_ref[0, 0]

        def block(i, carry, masked):
            dk, dv, dck = carry
            rows = pl.ds(pl.multiple_of(i * t, t), t)
            qb = (q_ref[rows, :] * scale).astype(BF16)
            dob = do_ref[rows, :].astype(BF16)
            s = _att_scores(qb, kb, ckv, i * t, j * t, masked)
            p = jnp.exp(s - lse_ref[0, rows, :])
            ds = p * (_dot(dob, vb, _NT) - delta_ref[0, rows, :])
            return dk + _dot(ds, qb, _TN), dv + _dot(p, dob, _TN), dck - _colsum(ds)

        init = (jnp.zeros((t, HEAD_DIM), F32), jnp.zeros((t, HEAD_DIM), F32), jnp.zeros((1, t), F32))
        carry = block(j, init, True)
        dk, dv, dck = lax.fori_loop(j + 1, nq, functools.partial(block, masked=False), carry)
        dk_ref[...] = dk.astype(dk_ref.dtype)
        dv_ref[...] = dv.astype(dv_ref.dtype)
        dck_ref[0, 0] = dck

    head = pl.BlockSpec((seq, HEAD_DIM), lambda h, j: (0, h))
    col = pl.BlockSpec((1, seq, 1), lambda h, j: (h, 0, 0))
    kblk = pl.BlockSpec((t, HEAD_DIM), lambda h, j: (j, h))
    row = pl.BlockSpec((1, 1, 1, t), lambda h, j: (h, j, 0, 0))
    return _pcall(
        body, name="att_bwd_kv", grid=(heads, nq),
        in_specs=[head, kblk, pl.BlockSpec((t, HEAD_DIM), lambda h, j: (j, heads + h)), head, col, col, row],
        out_specs=[kblk, kblk, row],
        out_shape=[jax.ShapeDtypeStruct((seq, heads * HEAD_DIM), BF16), jax.ShapeDtypeStruct((seq, heads * HEAD_DIM), BF16),
                   jax.ShapeDtypeStruct((heads, nq, 1, t), F32)],
        compiler_params=_params(("parallel", "parallel")),
    )(qz, kv, kv, do, lse, delta, ck)


def _mesh_pos():
    return lax.axis_index("x"), lax.axis_index("y"), lax.axis_index("c")


def _other_chips(x, y):
    return [(1 - x, y), (x, 1 - y), (1 - x, 1 - y)]


def _all_gather_weights(big, small):
    nb, ns = len(big), len(small)
    n_remote = 3 * (nb + ns)

    def plan(ins, outs, sems):
        send_sems, recv_sems, fwd_send, fwd_recv = sems
        x, y, c = _mesh_pos()
        chips = _other_chips(x, y)
        slots = [2 * cx + cy for cx, cy in chips]

        def half(ref, hc):
            rh = ref.shape[-2] // 2
            return ref.at[pl.ds(hc * rh, rh), :]

        def remote(i, j, src_chip, from_in):
            if i < nb:
                src = half(ins[i], c) if from_in else half(outs[i].at[src_chip], c)
                dst = half(outs[i].at[src_chip], c)
            else:
                src = ins[i] if from_in else outs[i].at[src_chip]
                dst = outs[i].at[src_chip]
            k = 3 * i + j
            return pltpu.make_async_remote_copy(src_ref=src, dst_ref=dst, send_sem=send_sems.at[k],
                                                recv_sem=recv_sems.at[k], device_id=(*chips[j], c),
                                                device_id_type=MESH_ID)

        def forward(i, j, hc):
            part = half(outs[i].at[slots[j]], hc)
            k = 3 * i + j
            return pltpu.make_async_remote_copy(src_ref=part, dst_ref=part, send_sem=fwd_send.at[k],
                                                recv_sem=fwd_recv.at[k], device_id=(x, y, 1 - c),
                                                device_id_type=MESH_ID)

        return remote, forward, 2 * x + y, slots, c

    def start(ins, outs, sems):
        remote, _, me, _, _ = plan(ins, outs, sems)
        for i in range(nb + ns):
            for j in range(3):
                remote(i, j, me, True).start()

    def finish(ins, outs, sems):
        remote, forward, me, slots, c = plan(ins, outs, sems)
        for i in range(nb + ns):
            for j in range(3):
                remote(i, j, slots[j], False).wait_recv()
                if i < nb:
                    forward(i, j, c).start()
        for i in range(nb):
            for j in range(3):
                forward(i, j, 1 - c).wait_recv()
        for i in range(nb + ns):
            for j in range(3):
                remote(i, j, me, True).wait_send()
                if i < nb:
                    forward(i, j, c).wait_send()

    arrays = tuple(big) + tuple(small)
    return _Exchange(
        arrays=arrays,
        out_shapes=tuple(jax.ShapeDtypeStruct((N_CHIPS,) + a.shape, a.dtype) for a in arrays),
        scratch=(pltpu.SemaphoreType.DMA((n_remote,)), pltpu.SemaphoreType.DMA((n_remote,)),
                 pltpu.SemaphoreType.DMA((3 * max(nb, 1),)), pltpu.SemaphoreType.DMA((3 * max(nb, 1),))),
        start=start, finish=finish)


def _swap_halves(grads):
    n = len(grads)

    def copies(ins, outs, sems):
        x, y, c = _mesh_pos()
        cps = []
        for i in range(n):
            rh = ins[i].shape[1] // 2
            cps.append(pltpu.make_async_remote_copy(
                src_ref=ins[i].at[:, pl.ds((1 - c) * rh, rh), :], dst_ref=outs[i], send_sem=sems[0].at[i],
                recv_sem=sems[1].at[i], device_id=(x, y, 1 - c), device_id_type=MESH_ID))
        return cps

    def start(ins, outs, sems):
        for cp in copies(ins, outs, sems):
            cp.start()

    def finish(ins, outs, sems):
        for cp in copies(ins, outs, sems):
            cp.wait()

    return _Exchange(
        arrays=tuple(grads),
        out_shapes=tuple(jax.ShapeDtypeStruct((g.shape[0], g.shape[1] // 2, g.shape[2]), g.dtype) for g in grads),
        scratch=(pltpu.SemaphoreType.DMA((n,)), pltpu.SemaphoreType.DMA((n,))),
        start=start, finish=finish)


def _pair_sum_bf16(g, theirs, pos, name):
    n, rh, cdim = theirs.shape
    tr = min(ROW_TILE, rh)
    nb = rh // tr

    def body(pos_ref, g_ref, t_ref, o_ref):
        o_ref[...] = (g_ref[...] + t_ref[...]).astype(BF16)

    grid_spec = pltpu.PrefetchScalarGridSpec(
        num_scalar_prefetch=1, grid=(n, nb),
        in_specs=[pl.BlockSpec((None, tr, cdim), lambda s, i, pos: (s, pos[1] * nb + i, 0)),
                  pl.BlockSpec((None, tr, cdim), lambda s, i, pos: (s, i, 0))],
        out_specs=pl.BlockSpec((None, tr, cdim), lambda s, i, pos: (s, i, 0)))
    return _pcall(body, name=name, grid_spec=grid_spec, out_shape=jax.ShapeDtypeStruct(theirs.shape, BF16),
                  compiler_params=_params(("parallel", "parallel")))(pos, g, theirs)


def _chip_sum(g, theirs, recv, pos, name):
    n, rh, cdim = theirs.shape
    tr = min(ROW_TILE, rh)
    nb = rh // tr

    def body(pos_ref, g_ref, t_ref, r0, r1, r2, o_ref):
        o_ref[...] = (((g_ref[...] + t_ref[...]) + r0[...]) + r1[...]) + r2[...]

    grid_spec = pltpu.PrefetchScalarGridSpec(
        num_scalar_prefetch=1, grid=(nb,),
        in_specs=[pl.BlockSpec((None, tr, cdim), lambda i, pos: (pos[0], pos[1] * nb + i, 0)),
                  pl.BlockSpec((None, tr, cdim), lambda i, pos: (pos[0], i, 0))]
        + [pl.BlockSpec((None, tr, cdim), functools.partial(lambda i, pos, j: (j, i, 0), j=j)) for j in range(3)],
        out_specs=pl.BlockSpec((tr, cdim), lambda i, pos: (i, 0)))
    return _pcall(body, name=name, grid_spec=grid_spec, out_shape=jax.ShapeDtypeStruct((rh, cdim), F32),
                  compiler_params=_params(("parallel",)))(pos, g, theirs, recv, recv, recv)


def _scatter_to_owner(parts):
    n = len(parts)

    def copies(ins, outs, sems):
        x, y, c = _mesh_pos()
        chips = _other_chips(x, y)
        cps = []
        for i in range(n):
            for j in range(3):
                k = 3 * i + j
                cps.append(pltpu.make_async_remote_copy(
                    src_ref=ins[i].at[2 * chips[j][0] + chips[j][1]], dst_ref=outs[i].at[j],
                    send_sem=sems[0].at[k], recv_sem=sems[1].at[k], device_id=(*chips[j], c),
                    device_id_type=MESH_ID))
        return cps

    def start(ins, outs, sems):
        for cp in copies(ins, outs, sems):
            cp.start()

    def finish(ins, outs, sems):
        for cp in copies(ins, outs, sems):
            cp.wait()

    return _Exchange(
        arrays=tuple(parts),
        out_shapes=tuple(jax.ShapeDtypeStruct((3,) + p.shape[1:], p.dtype) for p in parts),
        scratch=(pltpu.SemaphoreType.DMA((3 * n,)), pltpu.SemaphoreType.DMA((3 * n,))),
        start=start, finish=finish)


def _join_halves(halves):
    n = len(halves)

    def body(*refs):
        ins, outs, send_sems, recv_sems = refs[:n], refs[n:2 * n], refs[2 * n], refs[2 * n + 1]
        x, y, c = _mesh_pos()
        cps = []
        for i in range(n):
            rh = ins[i].shape[0]
            cps.append(pltpu.make_async_remote_copy(
                src_ref=ins[i], dst_ref=outs[i].at[pl.ds(c * rh, rh), :], send_sem=send_sems.at[i],
                recv_sem=recv_sems.at[i], device_id=(x, y, 1 - c), device_id_type=MESH_ID))
        for cp in cps:
            cp.start()
        for cp in cps:
            cp.wait_send()
        for i in range(n):
            rh = ins[i].shape[0]
            theirs = outs[i].at[pl.ds((1 - c) * rh, rh), :]
            pltpu.make_async_remote_copy(src_ref=ins[i], dst_ref=theirs, send_sem=send_sems.at[i],
                                         recv_sem=recv_sems.at[i], device_id=(x, y, 1 - c),
                                         device_id_type=MESH_ID).wait_recv()

    return _pcall(
        body, name="grad_join_halves", in_specs=[ANY] * n, out_specs=[ANY] * n,
        out_shape=[jax.ShapeDtypeStruct((2 * h.shape[0], h.shape[1]), h.dtype) for h in halves],
        scratch_shapes=[pltpu.SemaphoreType.DMA((n,)), pltpu.SemaphoreType.DMA((n,))],
        compiler_params=pltpu.CompilerParams(has_side_effects=True),
    )(*halves)


def _all_reduce_small(v):
    def body(v_ref, o_ref, recv, send_sems, recv_sems):
        x, y, c = _mesh_pos()
        peers = [(x, y, 1 - c), (1 - x, y, c), (x, 1 - y, c)]
        for s, peer in enumerate(peers):
            src = v_ref if s == 0 else o_ref
            cp = pltpu.make_async_remote_copy(src_ref=src, dst_ref=recv.at[s], send_sem=send_sems.at[s],
                                              recv_sem=recv_sems.at[s], device_id=peer, device_id_type=MESH_ID)
            cp.start()
            cp.wait()
            o_ref[...] = src[...] + recv[s]

    vm = pl.BlockSpec(memory_space=pltpu.VMEM)
    return _pcall(
        body, name="all_reduce_small", in_specs=[vm], out_specs=vm,
        out_shape=jax.ShapeDtypeStruct(v.shape, v.dtype),
        scratch_shapes=[pltpu.VMEM((3,) + v.shape, v.dtype), pltpu.SemaphoreType.DMA((3,)), pltpu.SemaphoreType.DMA((3,))],
        compiler_params=pltpu.CompilerParams(vmem_limit_bytes=VMEM_LIMIT, has_side_effects=True),
    )(v)


def _adamw_math(w, g, m, v):
    m = ADAM_B1 * m + (1.0 - ADAM_B1) * g
    v = ADAM_B2 * v + (1.0 - ADAM_B2) * (g * g)
    m_hat = m / (1.0 - ADAM_B1 ** ADAM_STEP)
    v_hat = v / (1.0 - ADAM_B2 ** ADAM_STEP)
    delta = -ADAM_LR * (m_hat / (jnp.sqrt(v_hat) + ADAM_EPS) + ADAM_WD * w)
    return delta, m, v


def _adamw(w, g, m, v, name):
    wd = w.shape[1]
    return _rows(_adamw_math, [_full(w), _full(g), _full(m), _full(v)], [(wd, F32)] * 3, name=name)


def _pack(arrs, rows):
    flat = jnp.concatenate([a.reshape(-1) for a in arrs])
    return jnp.pad(flat, (0, rows * LANES - flat.shape[0])).reshape(rows, LANES)


def _unpack(buf, like):
    flat = buf.reshape(-1)
    out, off = [], 0
    for a in like:
        out.append(flat[off:off + a.size].reshape(a.shape))
        off += a.size
    return out


def kernel(x, norm_pre, norm_post, s5_w_in, s5_a_re, s5_a_im, s5_log_dt, s5_b_re, s5_b_im, s5_c_re, s5_c_im, s5_d, s5_w_glu, s5_b_glu, s5_w_out, kv_norm, kv_w, kv_b_f, fox_w_in, fox_w_out, loss_target, m_norm_pre, m_norm_post, m_s5_w_in, m_s5_a_re, m_s5_a_im, m_s5_log_dt, m_s5_b_re, m_s5_b_im, m_s5_c_re, m_s5_c_im, m_s5_d, m_s5_w_glu, m_s5_b_glu, m_s5_w_out, m_kv_norm, m_kv_w, m_kv_b_f, m_fox_w_in, m_fox_w_out, v_norm_pre, v_norm_post, v_s5_w_in, v_s5_a_re, v_s5_a_im, v_s5_log_dt, v_s5_b_re, v_s5_b_im, v_s5_c_re, v_s5_c_im, v_s5_d, v_s5_w_glu, v_s5_b_glu, v_s5_w_out, v_kv_norm, v_kv_w, v_kv_b_f, v_fox_w_in, v_fox_w_out):
    seq, dm = x.shape[1], x.shape[2]
    width = dm
    heads = dm // HEAD_DIM
    fw = heads * HEAD_DIM
    groups = width // S5_GROUP
    chip = 2 * lax.axis_index("x") + lax.axis_index("y")

    big_shards = [s5_w_in[0], s5_w_glu[0], s5_w_out[0], kv_w, fox_w_in[0], fox_w_out[0]]
    own_shards = [w.astype(BF16) for w in big_shards] + [s5_d, s5_b_glu]
    fill_own = lambda gs, owns: [lax.dynamic_update_slice(g, own[None], (chip, 0, 0)) for g, own in zip(gs, owns)]
    c_idx = lax.axis_index("c")
    pos = jnp.stack([chip, c_idx]).astype(jnp.int32)
    first_owns = [own_shards[0], s5_d, s5_b_glu]
    g_win, g_d, g_bglu = fill_own(
        _exchange_call(_all_gather_weights(first_owns[:1], first_owns[1:]), "all_gather_first"), first_owns)
    gather_rest = _all_gather_weights(own_shards[1:6], [])
    cols = lambda g: jnp.moveaxis(g, 0, 1).reshape(g.shape[1], -1)
    rows = lambda g: g.reshape(-1, g.shape[2])
    w_in = g_win
    d_skip, b_glu = cols(g_d), cols(g_bglu)
    b_f = jnp.pad(kv_b_f, (0, LANES - heads)).reshape(1, LANES)

    h0 = x[0]
    target = loss_target[0]
    g_pre0, g_pre1 = norm_pre[0:1], norm_pre[1:2]
    g_post0, g_post1 = norm_post[0:1], norm_post[1:2]
    g_kv = kv_norm.reshape(1, dm)

    a_re, a_im, log_dt = s5_a_re[0], s5_a_im[0], s5_log_dt[0]
    disc, disc_vjp = jax.vjp(_ssm_discretize, a_re, a_im, log_dt)
    gpb = SSM_CH // S5_GROUP
    n_cb = groups // gpb
    par = jnp.stack([p.reshape(n_cb, gpb * S5_STATE) for p in disc], axis=1)
    b_t = lambda b: jnp.swapaxes(b, 1, 2)
    b_blk = jnp.stack([_block_diag(b_t(s5_b_re[0])), _block_diag(b_t(s5_b_im[0]))], axis=1)
    ct_blk = jnp.stack([_block_diag(s5_c_re[0]), _block_diag(s5_c_im[0])], axis=1)
    c_blk = jnp.swapaxes(ct_blk, 2, 3)

    xn1 = _to_slab(_rows(lambda h, g: (h * _rstd(h) * g,), [_full(h0)], [(dm, BF16)], consts=[g_pre0], name="norm_pre0")[0])
    uz = _mm(xn1, w_in, name="s5_in")
    ys, y1b, *rest = _ssm_fwd(uz, b_blk, c_blk, par, d_skip, carry=gather_rest)
    g_wglu, g_wout, g_kvw, g_fwin, g_fwout = fill_own(rest, own_shards[1:6])
    w_glu, w_out = rows(g_wglu), rows(g_wout)
    kvw_full = cols(g_kvw)
    w_kv = kvw_full[:, :2 * fw]
    w_f = jnp.pad(kvw_full[:, 2 * fw:], ((0, 0), (0, LANES - heads)))
    fw_in, fw_out = g_fwin, rows(g_fwout)
    glu_a = _mm(y1b, w_glu, name="s5_glu")

    def gate_fn(y, a, z, b):
        return (_gelu(y) * _sigmoid(a + b) * _silu(z),)

    y3b = _rows(gate_fn, [_full(ys), _full(glu_a), (uz, width, 1)], [(width, BF16)], consts=[b_glu], name="s5_gate")[0]
    o1 = _from_slab(_mm(y3b, w_out, name="s5_out"))

    def mid_fn(h, o, gp, gk, gq):
        h1 = h + o * _rstd(o) * gp
        r = _rstd(h1)
        return h1, h1 * r * gk, h1 * r * gq

    h1, xk, xn2 = _rows(mid_fn, [_full(h0), _full(o1)], [(dm, F32), (dm, BF16), (dm, BF16)],
                        consts=[g_post0, g_kv, g_pre1], name="mid_norms")

    kv = _mm(xk, w_kv, out_dtype=BF16, name="kv_proj")
    fl = _mm(xk, w_f, name="f_proj")
    qz = _mm(xn2, fw_in, name="fox_in")
    cum = _cum_fwd(fl, b_f)
    t_att = min(ATT_TILE, seq)
    cum_t = cum[:, :heads].T
    ck = cum_t.reshape(heads, seq // t_att, 1, t_att)
    o, lse = _att_fwd(qz, kv, ck)
    o2b = _rows(lambda a, z: (a * _silu(z),), [_full(o), (qz, fw, 1)], [(fw, BF16)], name="fox_gate")[0]
    o3 = _mm(o2b, fw_out, name="fox_out")

    def loss_fn(h, o, t, g):
        r = _rstd(o)
        err = h + o * r * g - t
        dh = err * (1.0 / dm)
        do, dg = _rms_bwd(o, g, dh)
        part = 0.5 * jnp.sum(jnp.mean(err * err, axis=-1, keepdims=True), axis=0, keepdims=True)
        return dh, do, jnp.broadcast_to(part, (1, LANES)), _colsum(dg)

    dh2, do3, loss_part, dg_post1 = _rows(loss_fn, [_full(h1), _full(o3), _full(target)], [(dm, F32), (dm, BF16)],
                                          consts=[g_post1], accs=[(1, LANES), (1, dm)], name="loss_head")
    loss = lax.psum(loss_part[0, 0], MESH_AXES)

    do2 = _mm(do3, fw_out, tb=True, name="fox_out_dx")
    dw_fout = _mm(o2b, do3, ta=True, name="fox_out_dw")

    def fox_gate_bwd(d, a, z):
        return d * _silu(z), d * a * _dsilu(z)

    do, dz2 = _rows(fox_gate_bwd, [_full(do2), _full(o), (qz, fw, 1)], [(fw, F32), (fw, BF16)], name="fox_gate_bwd")
    dq, delta = _att_bwd_q(qz, kv, do, o, lse, ck)
    dk, dv, dck = _att_bwd_kv(qz, kv, do, lse, delta, ck)
    dcum = jnp.pad(dck.reshape(heads, seq).T, ((0, 0), (0, LANES - heads)))
    dfl, db_f = _cum_bwd(dcum, fl, b_f)
    dqz = (dq, dz2)
    dkv = (dk, dv)
    dxn2 = _mm(dqz, fw_in, tb=True, name="fox_in_dx")
    dw_fin = _mm(xn2, dqz, ta=True, out_split=N_CHIPS, name="fox_in_dw")
    dxk_f = _mm(dfl, w_f, tb=True, name="f_proj_dx")
    dxk = _mm(dkv, w_kv, tb=True, add=dxk_f, name="kv_proj_dx")
    dw_kv = _mm(xk, dkv, ta=True, name="kv_proj_dw")
    dw_f = _mm(xk, dfl, ta=True, name="f_proj_dw")

    def mid_bwd(d2, h, dq_, dk_, o, gq, gk, gp):
        dxa, dga = _rms_bwd(h, gq, dq_)
        dxb, dgb = _rms_bwd(h, gk, dk_)
        dh = d2 + dxa + dxb
        do_, dgp = _rms_bwd(o, gp, dh)
        return dh, do_, _colsum(dga), _colsum(dgb), _colsum(dgp)

    to_cols = lambda g: jnp.moveaxis(g.reshape(g.shape[0], N_CHIPS, -1), 1, 0)
    to_rows = lambda g: g.reshape(N_CHIPS, -1, g.shape[1])
    dw_kv_full = jnp.concatenate([dw_kv, dw_f[:, :heads]], axis=1)
    early_grads = [to_cols(dw_kv_full), dw_fin, to_rows(dw_fout)]
    dh1, do1, dg_pre1, dg_kv, dg_post0, *early_theirs = _rows(
        mid_bwd, [_full(dh2), _full(h1), _full(dxn2), _full(dxk), _full(o1)], [(dm, F32), (dm, BF16)],
        consts=[g_pre1, g_kv, g_post0], accs=[(1, dm)] * 3, name="mid_norms_bwd", carry=_swap_halves(early_grads))
    early_sums = [_pair_sum_bf16(g, t, pos, f"grad_pair_sum_{3 + i}") for i, (g, t) in enumerate(zip(early_grads, early_theirs))]

    do1 = _to_slab(do1)
    dy3 = _mm(do1, w_out, tb=True, name="s5_out_dx")
    dw_out = _mm(y3b, do1, ta=True, name="s5_out_dw")

    def gate_bwd(d3, y, a, z, b):
        y1 = _gelu(y)
        gate = _sigmoid(a + b)
        dy2 = d3 * _silu(z)
        da = dy2 * y1 * gate * (1.0 - gate)
        return dy2 * gate, da, d3 * (y1 * gate) * _dsilu(z), _colsum(da)

    dy1_direct, da, dz, db_glu = _rows(gate_bwd, [_full(dy3), _full(ys), _full(glu_a), (uz, width, 1)],
                                       [(width, F32), (width, BF16), (width, BF16)], consts=[b_glu],
                                       accs=[(1, width)], name="s5_gate_bwd")
    dy1 = _mm(da, w_glu, tb=True, add=dy1_direct, name="s5_glu_dx")
    dw_glu = _mm(y1b, da, ta=True, name="s5_glu_dw")
    du, dbt_blk, dct_blk, dpar, dd, *early_recv = _ssm_bwd(uz, dy1, ys, b_blk, ct_blk, par, d_skip,
                                                           carry=_scatter_to_owner(early_sums))
    duz = (du, dz)
    dxn1 = _from_slab(_mm(duz, w_in, tb=True, name="s5_in_dx"))
    dw_in = _mm(xn1, duz, ta=True, out_split=N_CHIPS, name="s5_in_dw")

    def first_bwd(d1, h, dxn, g):
        dx, dg = _rms_bwd(h, g, dxn)
        return d1 + dx, _colsum(dg)

    grad_x, dg_pre0 = _rows(first_bwd, [_full(dh1), _full(h0), _full(dxn1)], [(dm, F32)], consts=[g_pre0],
                            accs=[(1, dm)], name="norm_pre0_bwd")

    dpar_g = [dpar[:, i, :].reshape(groups, S5_STATE) for i in range(4)]
    da_re, da_im, dlog_dt = disc_vjp(tuple(dpar_g))
    db_re = jnp.swapaxes(_block_diag_extract(dbt_blk[:, 0]), 1, 2)
    db_im = jnp.swapaxes(_block_diag_extract(dbt_blk[:, 1]), 1, 2)
    dc_re = _block_diag_extract(dct_blk[:, 0])
    dc_im = _block_diag_extract(dct_blk[:, 1])

    small_local = [jnp.concatenate([dg_pre0, dg_pre1]), jnp.concatenate([dg_post0, dg_post1]),
                   da_re[None], da_im[None], dlog_dt[None], db_re[None], db_im[None], dc_re[None], dc_im[None],
                   dd, db_glu, dg_kv.reshape(dm), db_f[0, :heads]]
    n_small = sum(a.size for a in small_local)
    small_rows = -(-n_small // (LANES * ROW_TILE)) * ROW_TILE
    small_sum = _unpack(_all_reduce_small(_pack(small_local, small_rows)), small_local)
    (g_norm_pre, g_norm_post, g_a_re, g_a_im, g_log_dt, g_b_re, g_b_im, g_c_re, g_c_im, g_d_full, g_bglu_full,
     g_kv_norm, g_b_f) = small_sum
    shard = width // N_CHIPS
    g_d_own = lax.dynamic_slice(g_d_full, (0, chip * shard), (1, shard))
    g_bglu_own = lax.dynamic_slice(g_bglu_full, (0, chip * shard), (1, shard))

    late_grads = [dw_in, to_rows(dw_glu), to_rows(dw_out)]
    late_theirs = _exchange_call(_swap_halves(late_grads), "grad_swap_halves")
    late_sums = [_pair_sum_bf16(g, t, pos, f"grad_pair_sum_{i}") for i, (g, t) in enumerate(zip(late_grads, late_theirs))]
    late_recv = _exchange_call(_scatter_to_owner(late_sums), "grad_scatter")
    big_grads = late_grads + early_grads
    theirs = list(late_theirs) + list(early_theirs)
    received = list(late_recv) + list(early_recv)
    halves = [_chip_sum(g, t, r, pos, f"grad_chip_sum_{i}") for i, (g, t, r) in enumerate(zip(big_grads, theirs, received))]
    joined = _join_halves(halves)
    g_win_s, g_wglu_s, g_wout_s, g_kvw_s, g_fwin_s, g_fwout_s = [
        lax.dynamic_update_slice(j, h, (c_idx * h.shape[0], 0)) for j, h in zip(joined, halves)]

    big_w = big_shards
    big_g = [g_win_s, g_wglu_s, g_wout_s, g_kvw_s, g_fwin_s, g_fwout_s]
    big_m = [m_s5_w_in[0], m_s5_w_glu[0], m_s5_w_out[0], m_kv_w, m_fox_w_in[0], m_fox_w_out[0]]
    big_v = [v_s5_w_in[0], v_s5_w_glu[0], v_s5_w_out[0], v_kv_w, v_fox_w_in[0], v_fox_w_out[0]]
    big_upd = [_adamw(w, g, m, v, f"adamw_{i}") for i, (w, g, m, v) in enumerate(zip(big_w, big_g, big_m, big_v))]

    small_names = ["norm_pre", "norm_post", "s5_a_re", "s5_a_im", "s5_log_dt", "s5_b_re", "s5_b_im", "s5_c_re", "s5_c_im",
                   "s5_d", "s5_b_glu", "kv_norm", "kv_b_f"]
    small_w = [norm_pre, norm_post, s5_a_re, s5_a_im, s5_log_dt, s5_b_re, s5_b_im, s5_c_re, s5_c_im, s5_d, s5_b_glu, kv_norm, kv_b_f]
    small_m = [m_norm_pre, m_norm_post, m_s5_a_re, m_s5_a_im, m_s5_log_dt, m_s5_b_re, m_s5_b_im, m_s5_c_re, m_s5_c_im, m_s5_d, m_s5_b_glu, m_kv_norm, m_kv_b_f]
    small_v = [v_norm_pre, v_norm_post, v_s5_a_re, v_s5_a_im, v_s5_log_dt, v_s5_b_re, v_s5_b_im, v_s5_c_re, v_s5_c_im, v_s5_d, v_s5_b_glu, v_kv_norm, v_kv_b_f]
    small_g = [g_norm_pre, g_norm_post, g_a_re, g_a_im, g_log_dt, g_b_re, g_b_im, g_c_re, g_c_im, g_d_own, g_bglu_own, g_kv_norm, g_b_f]
    small_g = [g.reshape(w.shape) for g, w in zip(small_g, small_w)]
    n_own = sum(a.size for a in small_w)
    own_rows = -(-n_own // (LANES * ROW_TILE)) * ROW_TILE
    pv = _pack(small_v, own_rows)
    pv = jnp.where(jnp.arange(own_rows * LANES).reshape(own_rows, LANES) < n_own, pv, 1.0)
    sd, sm, sv = _adamw(_pack(small_w, own_rows), _pack(small_g, own_rows), _pack(small_m, own_rows), pv, "adamw_small")
    small_delta, small_newm, small_newv = _unpack(sd, small_w), _unpack(sm, small_w), _unpack(sv, small_w)

    order = ["norm_pre", "norm_post", "s5_w_in", "s5_a_re", "s5_a_im", "s5_log_dt", "s5_b_re", "s5_b_im", "s5_c_re", "s5_c_im",
             "s5_d", "s5_w_glu", "s5_b_glu", "s5_w_out", "kv_norm", "kv_w", "kv_b_f", "fox_w_in", "fox_w_out"]
    big_names = ["s5_w_in", "s5_w_glu", "s5_w_out", "kv_w", "fox_w_in", "fox_w_out"]
    big_like = [s5_w_in, s5_w_glu, s5_w_out, kv_w, fox_w_in, fox_w_out]
    grads, deltas, new_m, new_v = {}, {}, {}, {}
    for i, n in enumerate(big_names):
        shp = big_like[i].shape
        grads[n] = big_g[i].reshape(shp)
        deltas[n], new_m[n], new_v[n] = (a.reshape(shp) for a in big_upd[i])
    for i, n in enumerate(small_names):
        grads[n], deltas[n], new_m[n], new_v[n] = small_g[i], small_delta[i], small_newm[i], small_newv[i]

    return (loss, grad_x[None], *[grads[n] for n in order], *[deltas[n] for n in order],
            *[new_m[n] for n in order], *[new_v[n] for n in order])
```

```python
import functools
import math
from typing import Callable, NamedTuple

import jax
import jax.numpy as jnp
from jax import lax
from jax.experimental import pallas as pl
from jax.experimental.pallas import tpu as pltpu

F32 = jnp.float32
BF16 = jnp.bfloat16

D_MODEL = 2048
SEQ = 4096
S5_GROUP = 16
S5_STATE = 64
HEAD_DIM = 128
RMS_EPS = 1e-6
NEG_INF = -1e30
ADAM_LR = 0.001
ADAM_B1 = 0.9
ADAM_B2 = 0.999
ADAM_EPS = 1e-08
ADAM_WD = 0.01
ADAM_STEP = 10

LANES = 128
SUBLANES = 8
VMEM_LIMIT = 56 * 1024 * 1024
N_CHIPS = 4
MESH_AXES = ("x", "y", "c")
MESH_ID = pl.DeviceIdType.MESH

SSM_CH = 128
ROW_TILE = 256
SCAN_ROWS = 512
SCAN_UNROLL = 4
ATT_TILE = 512
CUM_TILE = 512


def _pcall(body, **kw):
    return pl.pallas_call(body, **kw)


def _params(sem=None):
    if sem is None:
        return pltpu.CompilerParams(vmem_limit_bytes=VMEM_LIMIT)
    return pltpu.CompilerParams(vmem_limit_bytes=VMEM_LIMIT, dimension_semantics=sem)


def _sigmoid(x):
    return 1.0 / (1.0 + jnp.exp(-x))


def _silu(z):
    return z * _sigmoid(z)


def _dsilu(z):
    s = _sigmoid(z)
    return s * (1.0 + z * (1.0 - s))


_GELU_C = math.sqrt(2.0 / math.pi)


def _gelu(x):
    return 0.5 * x * (1.0 + jnp.tanh(_GELU_C * (x + 0.044715 * x * x * x)))


def _dgelu(x):
    t = jnp.tanh(_GELU_C * (x + 0.044715 * x * x * x))
    return 0.5 * (1.0 + t) + 0.5 * x * (1.0 - t * t) * _GELU_C * (1.0 + 3.0 * 0.044715 * x * x)


def _rstd(x):
    return lax.rsqrt(jnp.mean(x * x, axis=-1, keepdims=True) + RMS_EPS)


def _rms_bwd(x, g, dy):
    r = _rstd(x)
    dyg = dy * g
    dx = r * dyg - x * (r * r * r) * jnp.mean(dyg * x, axis=-1, keepdims=True)
    return dx, dy * (x * r)


def _colsum(v):
    return jnp.sum(v, axis=0, keepdims=True)


ANY = pl.BlockSpec(memory_space=pl.ANY)


class _Exchange(NamedTuple):
    arrays: tuple
    out_shapes: tuple
    scratch: tuple
    start: Callable
    finish: Callable


def _together(*exs):
    def parts(seq, field):
        out, off = [], 0
        for e in exs:
            n = len(getattr(e, field))
            out.append(seq[off:off + n])
            off += n
        return out

    def start(ins, outs, sems):
        for e, i, o, s in zip(exs, parts(ins, "arrays"), parts(outs, "out_shapes"), parts(sems, "scratch")):
            e.start(i, o, s)

    def finish(ins, outs, sems):
        for e, i, o, s in zip(exs, parts(ins, "arrays"), parts(outs, "out_shapes"), parts(sems, "scratch")):
            e.finish(i, o, s)

    return _Exchange(arrays=sum((tuple(e.arrays) for e in exs), ()), out_shapes=sum((tuple(e.out_shapes) for e in exs), ()),
                     scratch=sum((tuple(e.scratch) for e in exs), ()), start=start, finish=finish)


def _exchange_call(ex, name):
    n_in, n_out = len(ex.arrays), len(ex.out_shapes)

    def body(*refs):
        ins, outs, sems = refs[:n_in], refs[n_in:n_in + n_out], refs[n_in + n_out:]
        ex.start(ins, outs, sems)
        ex.finish(ins, outs, sems)

    return _pcall(body, name=name, in_specs=[ANY] * n_in, out_specs=[ANY] * n_out, out_shape=list(ex.out_shapes),
                  scratch_shapes=list(ex.scratch), compiler_params=pltpu.CompilerParams(has_side_effects=True))(*ex.arrays)


def _carry(ex, refs, n_fixed_in, n_fixed_out):
    if ex is None:
        return refs, lambda cond: None, lambda cond: None
    n_in, n_out, n_sem = len(ex.arrays), len(ex.out_shapes), len(ex.scratch)
    fixed_in = refs[:n_fixed_in]
    ex_in = refs[n_fixed_in:n_fixed_in + n_in]
    rest = refs[n_fixed_in + n_in:]
    fixed_out = rest[:n_fixed_out]
    ex_out = rest[n_fixed_out:n_fixed_out + n_out]
    scratch = rest[n_fixed_out + n_out:]
    sems = scratch[len(scratch) - n_sem:]

    def start_when(cond):
        pl.when(cond)(lambda: ex.start(ex_in, ex_out, sems))

    def finish_when(cond):
        pl.when(cond)(lambda: ex.finish(ex_in, ex_out, sems))

    return tuple(fixed_in) + tuple(fixed_out) + tuple(scratch[:len(scratch) - n_sem]), start_when, finish_when


def _carry_specs(ex):
    if ex is None:
        return (), [], [], [], []
    return ex.arrays, [ANY] * len(ex.arrays), [ANY] * len(ex.out_shapes), list(ex.out_shapes), list(ex.scratch)


def _mm(a, b, *, ta=False, tb=False, out_dtype=F32, add=None, out_split=1, tm=1024, tn=1024, tk=2048, name, carry=None):
    def describe(op):
        if isinstance(op, (tuple, list)):
            assert all(p.ndim == 2 and p.shape == op[0].shape for p in op)
            return list(op), op[0].shape[0], op[0].shape[1], False
        if op.ndim == 3:
            return [op], op.shape[1], op.shape[2], True
        return [op], op.shape[0], op.shape[1], False

    a_parts, a_rows, a_pc, a_stack = describe(a)
    b_parts, b_rows, b_pc, b_stack = describe(b)
    a_cols = a_pc * (a.shape[0] if a_stack else len(a_parts))
    b_cols = b_pc * (b.shape[0] if b_stack else len(b_parts))
    k_dim, m_dim = (a_rows, a_cols) if ta else (a_cols, a_rows)
    n_dim, kb = (b_rows, b_cols) if tb else (b_cols, b_rows)
    assert kb == k_dim, (k_dim, kb)
    tm = min(tm, a_pc) if ta else min(tm, m_dim)
    tk = min(tk, k_dim, k_dim if ta else a_pc, b_pc if tb else k_dim)
    tn = min(tn, n_dim // out_split, n_dim if tb else b_pc)
    a_ct, b_ct = (tm if ta else tk), (tk if tb else tn)
    assert m_dim % tm == 0 and n_dim % tn == 0 and k_dim % tk == 0 and a_pc % a_ct == 0 and b_pc % b_ct == 0
    assert (n_dim // out_split) % tn == 0
    nk = k_dim // tk
    dims = (((0 if ta else 1,), (1 if tb else 0,)), ((), ()))
    n_a, n_b = len(a_parts), len(b_parts)
    assert n_a == 1 or n_b == 1

    def operand_specs(parts, stack, rows_t, cols_t, per, row_of, col_of):
        specs = []
        for p in range(len(parts)):
            def col(i, j, k, p=p):
                return jnp.clip(col_of(i, j, k) - p * per, 0, per - 1) if len(parts) > 1 else col_of(i, j, k)
            if stack:
                specs.append(pl.BlockSpec((None, rows_t, cols_t),
                                          lambda i, j, k, col=col: (col(i, j, k) // per, row_of(i, j, k), col(i, j, k) % per)))
            else:
                specs.append(pl.BlockSpec((rows_t, cols_t), lambda i, j, k, col=col: (row_of(i, j, k), col(i, j, k))))
        return specs

    if ta:
        a_specs = operand_specs(a_parts, a_stack, tk, tm, a_pc // tm, lambda i, j, k: k, lambda i, j, k: i)
    else:
        a_specs = operand_specs(a_parts, a_stack, tm, tk, a_pc // tk, lambda i, j, k: i, lambda i, j, k: k)
    if tb:
        b_specs = operand_specs(b_parts, b_stack, tn, tk, b_pc // tk, lambda i, j, k: j, lambda i, j, k: k)
    else:
        b_specs = operand_specs(b_parts, b_stack, tk, tn, b_pc // tn, lambda i, j, k: k, lambda i, j, k: j)

    n_fixed_in = n_a + n_b + (1 if add is not None else 0)
    grid = (m_dim // tm, n_dim // tn, nk)
    ex_args, ex_in_specs, ex_out_specs, ex_out_shapes, ex_scratch = _carry_specs(carry)

    def body(*refs):
        refs, start_when, finish_when = _carry(carry, refs, n_fixed_in, 1)
        i, j, k = pl.program_id(0), pl.program_id(1), pl.program_id(2)
        step = (i * grid[1] + j) * grid[2] + k
        start_when(step == 0)
        compute(*refs)
        finish_when(step == grid[0] * grid[1] * grid[2] - 1)

    def compute(*refs):
        a_refs, b_refs = refs[:n_a], refs[n_a:n_a + n_b]
        rest = refs[n_a + n_b:]
        c_ref = rest[0] if add is not None else None
        o_ref = rest[1] if add is not None else rest[0]
        acc = None if nk == 1 else rest[-1]
        i, j, k = pl.program_id(0), pl.program_id(1), pl.program_id(2)

        def finish(res):
            if add is not None:
                res = res + c_ref[...]
            o_ref[...] = res.astype(out_dtype)

        def accumulate(a_ref, b_ref):
            prod = lax.dot_general(a_ref[...].astype(BF16), b_ref[...].astype(BF16), dims,
                                   preferred_element_type=F32)
            if nk == 1:
                finish(prod)
                return

            @pl.when(k == 0)
            def _():
                acc[...] = prod

            @pl.when(jnp.logical_and(k > 0, k < nk - 1))
            def _():
                acc[...] += prod

            @pl.when(k == nk - 1)
            def _():
                finish(acc[...] + prod)

        if n_a == 1 and n_b == 1:
            accumulate(a_refs[0], b_refs[0])
        else:
            many, block, per = (a_refs, (i if ta else k), a_pc // a_ct) if n_a > 1 else (b_refs, (k if tb else j), b_pc // b_ct)
            for p, ref in enumerate(many):
                @pl.when(block // per == p)
                def _(ref=ref):
                    accumulate(ref, b_refs[0]) if n_a > 1 else accumulate(a_refs[0], ref)

    per_out = n_dim // out_split // tn
    if out_split > 1:
        o_spec = pl.BlockSpec((None, tm, tn), lambda i, j, k: (j // per_out, i, j % per_out))
        out_shape = jax.ShapeDtypeStruct((out_split, m_dim, n_dim // out_split), out_dtype)
    else:
        o_spec = pl.BlockSpec((tm, tn), lambda i, j, k: (i, j))
        out_shape = jax.ShapeDtypeStruct((m_dim, n_dim), out_dtype)
    in_specs = a_specs + b_specs + ([pl.BlockSpec((tm, tn), lambda i, j, k: (i, j))] if add is not None else [])
    args = tuple(a_parts) + tuple(b_parts) + ((add,) if add is not None else ())
    acc_scratch = [pltpu.VMEM((tm, tn), F32)] if nk > 1 else []
    if carry is None:
        return _pcall(
            body, name=name, grid=grid, in_specs=in_specs, out_specs=o_spec, out_shape=out_shape,
            scratch_shapes=acc_scratch, compiler_params=_params(("parallel", "parallel", "arbitrary")),
        )(*args)
    return _pcall(
        body, name=name, grid=grid, in_specs=in_specs + ex_in_specs, out_specs=[o_spec] + ex_out_specs,
        out_shape=[out_shape] + ex_out_shapes, scratch_shapes=acc_scratch + ex_scratch,
        compiler_params=pltpu.CompilerParams(vmem_limit_bytes=VMEM_LIMIT, has_side_effects=True,
                                             dimension_semantics=("arbitrary", "arbitrary", "arbitrary")),
    )(*args, *ex_args)


def _rows(fn, ins, outs, *, name, consts=(), accs=(), carry=None):
    n_rows = ins[0][0].shape[0]
    tr = min(ROW_TILE, n_rows)
    assert n_rows % tr == 0
    n_in, n_c, n_out = len(ins), len(consts), len(outs)
    n_steps = n_rows // tr
    ex_args, ex_in_specs, ex_out_specs, ex_out_shapes, ex_scratch = _carry_specs(carry)

    def body(*refs):
        refs, start_when, finish_when = _carry(carry, refs, n_in + n_c, n_out + len(accs))
        start_when(pl.program_id(0) == 0)
        _compute(*refs)
        finish_when(pl.program_id(0) == n_steps - 1)

    def _compute(*refs):
        vals = [r[...] for r in refs[:n_in + n_c]]
        res = fn(*vals)
        res = res if isinstance(res, (tuple, list)) else (res,)
        o_refs = refs[n_in + n_c:]
        for r, v in zip(o_refs[:n_out], res[:n_out]):
            r[...] = v.astype(r.dtype)
        if accs:
            first = pl.program_id(0) == 0
            for r, v in zip(o_refs[n_out:], res[n_out:]):
                @pl.when(first)
                def _(r=r, v=v):
                    r[...] = v

                @pl.when(jnp.logical_not(first))
                def _(r=r, v=v):
                    r[...] += v

    in_specs = [pl.BlockSpec((tr, w), functools.partial(lambda i, cb: (i, cb), cb=cb)) for _, w, cb in ins]
    in_specs += [pl.BlockSpec(c.shape, functools.partial(lambda i, nd: (0,) * nd, nd=c.ndim)) for c in consts]
    out_specs = [pl.BlockSpec((tr, w), lambda i: (i, 0)) for w, _ in outs]
    out_specs += [pl.BlockSpec(s, lambda i: (0, 0)) for s in accs]
    out_shape = [jax.ShapeDtypeStruct((n_rows, w), dt) for w, dt in outs]
    out_shape += [jax.ShapeDtypeStruct(s, F32) for s in accs]
    sequential = bool(accs) or carry is not None
    params = _params(("arbitrary",) if sequential else ("parallel",))
    if carry is not None:
        params = pltpu.CompilerParams(vmem_limit_bytes=VMEM_LIMIT, dimension_semantics=("arbitrary",), has_side_effects=True)
    return _pcall(
        body, name=name, grid=(n_steps,), in_specs=in_specs + ex_in_specs, out_specs=out_specs + ex_out_specs,
        out_shape=out_shape + ex_out_shapes, scratch_shapes=ex_scratch, compiler_params=params,
    )(*[a for a, _, _ in ins], *consts, *ex_args)


def _full(a):
    return (a, a.shape[1], 0)


def _cmul(ar, ai, br, bi):
    return ar * br - ai * bi, ar * bi + ai * br


def _seg_scan(re_ref, im_ref, a_re, a_im, seg, reverse):
    n_k = len(a_re)
    assert seg & (seg - 1) == 0
    ab = [(jnp.broadcast_to(a_re[k], (SUBLANES, LANES)), jnp.broadcast_to(a_im[k], (SUBLANES, LANES)))
          for k in range(n_k)]

    def slab(i):
        j = seg - 1 - i if reverse else i
        return pl.ds(pl.multiple_of(j * SUBLANES, SUBLANES), SUBLANES)

    def local(i, carry):
        out = []
        for k in range(n_k):
            hr, hi = _cmul(ab[k][0], ab[k][1], carry[2 * k], carry[2 * k + 1])
            hr = hr + re_ref[k, slab(i), :]
            hi = hi + im_ref[k, slab(i), :]
            re_ref[k, slab(i), :] = hr
            im_ref[k, slab(i), :] = hi
            out += [hr, hi]
        return tuple(out)

    zero = jnp.zeros((SUBLANES, LANES), F32)
    end = lax.fori_loop(0, seg, local, (zero,) * (2 * n_k), unroll=SCAN_UNROLL)

    row = lax.broadcasted_iota(jnp.int32, (SUBLANES, LANES), 0)
    edge = SUBLANES - 1 if reverse else 0
    shift = SUBLANES - 1 if reverse else 1
    enter = []
    for k in range(n_k):
        pr, pi = ab[k]
        for _ in range(seg.bit_length() - 1):
            pr, pi = _cmul(pr, pi, pr, pi)
        tr_, ti_ = zero, zero
        for _ in range(SUBLANES - 1):
            vr, vi = _cmul(pr, pi, tr_, ti_)
            tr_ = jnp.where(row == edge, 0.0, pltpu.roll(vr + end[2 * k], shift, 0))
            ti_ = jnp.where(row == edge, 0.0, pltpu.roll(vi + end[2 * k + 1], shift, 0))
        enter += [tr_, ti_]

    def fix(i, carry):
        out = []
        for k in range(n_k):
            er, ei = _cmul(ab[k][0], ab[k][1], carry[2 * k], carry[2 * k + 1])
            re_ref[k, slab(i), :] += er
            im_ref[k, slab(i), :] += ei
            out += [er, ei]
        return tuple(out)

    lax.fori_loop(0, seg, fix, tuple(enter), unroll=SCAN_UNROLL)
    return enter


def _to_slab(a):
    s, w = a.shape
    return a.reshape(SUBLANES, s // SUBLANES, w).swapaxes(0, 1).reshape(s, w)


def _from_slab(a):
    s, w = a.shape
    return a.reshape(s // SUBLANES, SUBLANES, w).swapaxes(0, 1).reshape(s, w)


def _lane_blocks(v, n_k):
    return [v[:, k * LANES:(k + 1) * LANES] for k in range(n_k)]


def _gather_k(ref, rows, n_k):
    return jnp.concatenate([ref[k, rows, :] for k in range(n_k)], axis=1)


def _dot(a, b, dims=(((1,), (0,)), ((), ()))):
    return lax.dot_general(a.astype(BF16), b.astype(BF16), dims, preferred_element_type=F32)


_NT = (((1,), (1,)), ((), ()))
_TN = (((0,), (0,)), ((), ()))


def _ssm_fwd(uz, b_blk, c_blk, par, d_skip, carry=None):
    seq = uz.shape[0]
    width = d_skip.shape[1]
    ns = SSM_CH // S5_GROUP * S5_STATE
    n_k = ns // LANES
    seg = seq // SUBLANES
    tb = min(SCAN_ROWS, seq)
    n_cb = width // SSM_CH
    ex_args, ex_in_specs, ex_out_specs, ex_out_shapes, ex_scratch = _carry_specs(carry)

    def body(*refs):
        refs, start_when, finish_when = _carry(carry, refs, 5, 2)
        start_when(pl.program_id(0) == 0)
        compute(*refs)
        finish_when(pl.program_id(0) == n_cb - 1)

    def compute(u_ref, b_ref, c_ref, par_ref, d_ref, y_ref, g_ref, hre, him):
        coef_r, coef_i = par_ref[0, 2:3, :], par_ref[0, 3:4, :]
        for c0 in range(0, seq, tb):
            rows = pl.ds(c0, tb)
            ub = u_ref[rows, :]
            bur, bui = _dot(ub, b_ref[0, 0]), _dot(ub, b_ref[0, 1])
            xr, xi = coef_r * bur - coef_i * bui, coef_r * bui + coef_i * bur
            for k in range(n_k):
                hre[k, rows, :] = xr[:, k * LANES:(k + 1) * LANES]
                him[k, rows, :] = xi[:, k * LANES:(k + 1) * LANES]
        _seg_scan(hre, him, _lane_blocks(par_ref[0, 0:1, :], n_k), _lane_blocks(par_ref[0, 1:2, :], n_k), seg, False)
        for c0 in range(0, seq, tb):
            rows = pl.ds(c0, tb)
            y = _dot(_gather_k(hre, rows, n_k), c_ref[0, 0]) - _dot(_gather_k(him, rows, n_k), c_ref[0, 1])
            y = y + d_ref[...] * u_ref[rows, :]
            y_ref[rows, :] = y
            g_ref[rows, :] = _gelu(y).astype(BF16)

    blk = pl.BlockSpec((seq, SSM_CH), lambda i: (0, i))
    params = _params(("parallel",)) if carry is None else pltpu.CompilerParams(
        vmem_limit_bytes=VMEM_LIMIT, dimension_semantics=("arbitrary",), has_side_effects=True)
    return _pcall(
        body, name="ssm_fwd", grid=(n_cb,),
        in_specs=[blk,
                  pl.BlockSpec((1, 2, SSM_CH, ns), lambda i: (i, 0, 0, 0)),
                  pl.BlockSpec((1, 2, ns, SSM_CH), lambda i: (i, 0, 0, 0)),
                  pl.BlockSpec((1, 4, ns), lambda i: (i, 0, 0)),
                  pl.BlockSpec((1, SSM_CH), lambda i: (0, i))] + ex_in_specs,
        out_specs=[blk, blk] + ex_out_specs,
        out_shape=[jax.ShapeDtypeStruct((seq, width), F32), jax.ShapeDtypeStruct((seq, width), BF16)] + ex_out_shapes,
        scratch_shapes=[pltpu.VMEM((n_k, seq, LANES), F32), pltpu.VMEM((n_k, seq, LANES), F32)] + ex_scratch,
        compiler_params=params,
    )(uz, b_blk, c_blk, par, d_skip, *ex_args)


def _ssm_bwd(uz, dy1, ys, b_blk, ct_blk, par, d_skip, carry=None):
    seq = uz.shape[0]
    width = d_skip.shape[1]
    ns_all = SSM_CH // S5_GROUP * S5_STATE
    n_half = 2
    ns = ns_all // n_half
    n_k = ns // LANES
    seg = seq // SUBLANES
    tb = min(SCAN_ROWS, seq)
    n_cb = width // SSM_CH
    ex_args, ex_in_specs, ex_out_specs, ex_out_shapes, ex_scratch = _carry_specs(carry)

    def body(*refs):
        refs, start_when, finish_when = _carry(carry, refs, 7, 5)
        step = pl.program_id(0) * n_half + pl.program_id(1)
        start_when(step == 0)
        compute(*refs)
        finish_when(step == n_cb * n_half - 1)

    def compute(u_ref, dy_ref, ys_ref, b_ref, ct_ref, par_ref, d_ref,
                du_ref, dbt_ref, dct_ref, dpar_ref, dd_ref, hre, him, gre, gim):
        half = pl.program_id(1)
        a_r, a_i = par_ref[0, 0:1, :], par_ref[0, 1:2, :]
        coef_r, coef_i = par_ref[0, 2:3, :], par_ref[0, 3:4, :]

        def dys_of(rows):
            return dy_ref[rows, :] * _dgelu(ys_ref[rows, :])

        for c0 in range(0, seq, tb):
            rows = pl.ds(c0, tb)
            ub = u_ref[rows, :]
            bur, bui = _dot(ub, b_ref[0, 0]), _dot(ub, b_ref[0, 1])
            xr, xi = coef_r * bur - coef_i * bui, coef_r * bui + coef_i * bur
            dys = dys_of(rows)
            gr, gi = _dot(dys, ct_ref[0, 0]), -_dot(dys, ct_ref[0, 1])
            for k in range(n_k):
                lanes = slice(k * LANES, (k + 1) * LANES)
                hre[k, rows, :] = xr[:, lanes]
                him[k, rows, :] = xi[:, lanes]
                gre[k, rows, :] = gr[:, lanes]
                gim[k, rows, :] = gi[:, lanes]
        enter = _seg_scan(hre, him, _lane_blocks(a_r, n_k), _lane_blocks(a_i, n_k), seg, False)
        _seg_scan(gre, gim, _lane_blocks(a_r, n_k), _lane_blocks(-a_i, n_k), seg, True)

        def corr(j, carry):
            acc, prev = carry
            acc_o, prev_o = [], []
            for k in range(n_k):
                sl = pl.ds(pl.multiple_of(j * SUBLANES, SUBLANES), SUBLANES)
                g_r, g_i = gre[k, sl, :], gim[k, sl, :]
                p_r, p_i = prev[2 * k], prev[2 * k + 1]
                acc_o += [acc[2 * k] + g_r * p_r + g_i * p_i, acc[2 * k + 1] + g_i * p_r - g_r * p_i]
                prev_o += [hre[k, sl, :], him[k, sl, :]]
            return tuple(acc_o), tuple(prev_o)

        zero = jnp.zeros((SUBLANES, LANES), F32)
        acc, _ = lax.fori_loop(0, seg, corr, ((zero,) * (2 * n_k), tuple(enter)), unroll=SCAN_UNROLL)
        da_r = jnp.concatenate([_colsum(acc[2 * k]) for k in range(n_k)], axis=1)
        da_i = jnp.concatenate([_colsum(acc[2 * k + 1]) for k in range(n_k)], axis=1)

        zeros_cn = jnp.zeros((SSM_CH, ns), F32)
        qt_r, qt_i, dct_r, dct_i = zeros_cn, zeros_cn, zeros_cn, zeros_cn
        dd = jnp.zeros((1, SSM_CH), F32)
        first = half == 0
        for c0 in range(0, seq, tb):
            rows = pl.ds(c0, tb)
            ub = u_ref[rows, :]
            dys = dys_of(rows)
            dct_r = dct_r + _dot(dys, _gather_k(hre, rows, n_k), _TN)
            dct_i = dct_i - _dot(dys, _gather_k(him, rows, n_k), _TN)
            g_r, g_i = _gather_k(gre, rows, n_k), _gather_k(gim, rows, n_k)
            qt_r = qt_r + _dot(ub, g_r, _TN)
            qt_i = qt_i + _dot(ub, g_i, _TN)
            dbu_r, dbu_i = coef_r * g_r + coef_i * g_i, coef_r * g_i - coef_i * g_r
            du = _dot(dbu_r, b_ref[0, 0], _NT) + _dot(dbu_i, b_ref[0, 1], _NT)
            dd = dd + _colsum(dys * ub)

            @pl.when(first)
            def _(du=du, dys=dys, rows=rows):
                du_ref[rows, :] = du + d_ref[...] * dys

            @pl.when(jnp.logical_not(first))
            def _(du=du, rows=rows):
                du_ref[rows, :] += du

        @pl.when(first)
        def _():
            dd_ref[...] = dd

        b_r, b_i = b_ref[0, 0], b_ref[0, 1]
        dbt_ref[0, 0] = coef_r * qt_r + coef_i * qt_i
        dbt_ref[0, 1] = coef_r * qt_i - coef_i * qt_r
        dct_ref[0, 0] = dct_r
        dct_ref[0, 1] = dct_i
        dpar_ref[0, 0:1, :] = da_r
        dpar_ref[0, 1:2, :] = da_i
        dpar_ref[0, 2:3, :] = _colsum(b_r * qt_r + b_i * qt_i)
        dpar_ref[0, 3:4, :] = _colsum(b_r * qt_i - b_i * qt_r)

    blk = lambda i, h: (0, i)
    params = _params(("parallel", "arbitrary")) if carry is None else pltpu.CompilerParams(
        vmem_limit_bytes=VMEM_LIMIT, dimension_semantics=("arbitrary", "arbitrary"), has_side_effects=True)
    return _pcall(
        body, name="ssm_bwd", grid=(n_cb, n_half),
        in_specs=[pl.BlockSpec((seq, SSM_CH), blk), pl.BlockSpec((seq, SSM_CH), blk), pl.BlockSpec((seq, SSM_CH), blk),
                  pl.BlockSpec((1, 2, SSM_CH, ns), lambda i, h: (i, 0, 0, h)),
                  pl.BlockSpec((1, 2, SSM_CH, ns), lambda i, h: (i, 0, 0, h)),
                  pl.BlockSpec((1, 4, ns), lambda i, h: (i, 0, h)),
                  pl.BlockSpec((1, SSM_CH), blk)] + ex_in_specs,
        out_specs=[pl.BlockSpec((seq, SSM_CH), blk),
                   pl.BlockSpec((1, 2, SSM_CH, ns), lambda i, h: (i, 0, 0, h)),
                   pl.BlockSpec((1, 2, SSM_CH, ns), lambda i, h: (i, 0, 0, h)),
                   pl.BlockSpec((1, 4, ns), lambda i, h: (i, 0, h)),
                   pl.BlockSpec((1, SSM_CH), blk)] + ex_out_specs,
        out_shape=[jax.ShapeDtypeStruct((seq, width), F32),
                   jax.ShapeDtypeStruct((n_cb, 2, SSM_CH, ns_all), F32),
                   jax.ShapeDtypeStruct((n_cb, 2, SSM_CH, ns_all), F32),
                   jax.ShapeDtypeStruct((n_cb, 4, ns_all), F32),
                   jax.ShapeDtypeStruct((1, width), F32)] + ex_out_shapes,
        scratch_shapes=[pltpu.VMEM((n_k, seq, LANES), F32) for _ in range(4)] + ex_scratch,
        compiler_params=params,
    )(uz, dy1, ys, b_blk, ct_blk, par, d_skip, *ex_args)


def _ssm_discretize(a_re, a_im, log_dt):
    dt = jnp.exp(log_dt)[:, None]
    mag = jnp.exp(a_re * dt)
    abar_re = mag * jnp.cos(a_im * dt)
    abar_im = mag * jnp.sin(a_im * dt)
    den = a_re * a_re + a_im * a_im
    nr = abar_re - 1.0
    coef_re = (nr * a_re + abar_im * a_im) / den
    coef_im = (abar_im * a_re - nr * a_im) / den
    return abar_re, abar_im, coef_re, coef_im


def _block_diag(w_gcp):
    gpb = SSM_CH // S5_GROUP
    n_cb = w_gcp.shape[0] // gpb
    w = w_gcp.reshape(n_cb, gpb, S5_GROUP, 1, S5_STATE)
    eye = jnp.eye(gpb, dtype=w.dtype)[None, :, None, :, None]
    return (w * eye).reshape(n_cb, SSM_CH, gpb * S5_STATE)


def _block_diag_extract(w_blk):
    gpb = SSM_CH // S5_GROUP
    n_cb = w_blk.shape[0]
    w = w_blk.reshape(n_cb, gpb, S5_GROUP, gpb, S5_STATE)
    w = jnp.moveaxis(jnp.diagonal(w, axis1=1, axis2=3), -1, 1)
    return w.reshape(n_cb * gpb, S5_GROUP, S5_STATE)


def _split3(x):
    hi = x.astype(BF16)
    mid = (x - hi.astype(F32)).astype(BF16)
    lo = (x - hi.astype(F32) - mid.astype(F32)).astype(BF16)
    return hi, mid, lo


def _tri_sum(tri, x):
    hi, mid, lo = _split3(x)
    return (jnp.dot(tri, hi, preferred_element_type=F32) + jnp.dot(tri, mid, preferred_element_type=F32)
            + jnp.dot(tri, lo, preferred_element_type=F32))


def _log_sigmoid(x):
    return jnp.minimum(x, 0.0) - jnp.log(1.0 + jnp.exp(-jnp.abs(x)))


def _cum_fwd(fl, b_f):
    seq = fl.shape[0]
    t = min(CUM_TILE, seq)

    def body(fl_ref, b_ref, o_ref, carry):
        @pl.when(pl.program_id(0) == 0)
        def _():
            carry[...] = jnp.zeros_like(carry)

        r = lax.broadcasted_iota(jnp.int32, (t, t), 0)
        c = lax.broadcasted_iota(jnp.int32, (t, t), 1)
        tri = (c <= r).astype(BF16)
        cum = _tri_sum(tri, _log_sigmoid(fl_ref[...] + b_ref[...])) + carry[...]
        o_ref[...] = cum
        carry[...] = cum[t - 1:t, :]

    return _pcall(
        body, name="cum_fwd", grid=(seq // t,),
        in_specs=[pl.BlockSpec((t, LANES), lambda i: (i, 0)), pl.BlockSpec((1, LANES), lambda i: (0, 0))],
        out_specs=pl.BlockSpec((t, LANES), lambda i: (i, 0)),
        out_shape=jax.ShapeDtypeStruct((seq, LANES), F32),
        scratch_shapes=[pltpu.VMEM((1, LANES), F32)],
        compiler_params=_params(("arbitrary",)),
    )(fl, b_f)


def _cum_bwd(dcum, fl, b_f):
    seq = fl.shape[0]
    t = min(CUM_TILE, seq)
    nb = seq // t

    def body(dc_ref, fl_ref, b_ref, o_ref, db_ref, carry):
        @pl.when(pl.program_id(0) == 0)
        def _():
            carry[...] = jnp.zeros_like(carry)
            db_ref[...] = jnp.zeros_like(db_ref)

        r = lax.broadcasted_iota(jnp.int32, (t, t), 0)
        c = lax.broadcasted_iota(jnp.int32, (t, t), 1)
        tri = (c >= r).astype(BF16)
        rev = _tri_sum(tri, dc_ref[...]) + carry[...]
        carry[...] = rev[0:1, :]
        dfl = rev * _sigmoid(-(fl_ref[...] + b_ref[...]))
        o_ref[...] = dfl
        db_ref[...] += _colsum(dfl)

    return _pcall(
        body, name="cum_bwd", grid=(nb,),
        in_specs=[pl.BlockSpec((t, LANES), lambda i: (nb - 1 - i, 0)), pl.BlockSpec((t, LANES), lambda i: (nb - 1 - i, 0)),
                  pl.BlockSpec((1, LANES), lambda i: (0, 0))],
        out_specs=[pl.BlockSpec((t, LANES), lambda i: (nb - 1 - i, 0)), pl.BlockSpec((1, LANES), lambda i: (0, 0))],
        out_shape=[jax.ShapeDtypeStruct((seq, LANES), F32), jax.ShapeDtypeStruct((1, LANES), F32)],
        scratch_shapes=[pltpu.VMEM((1, LANES), F32)],
        compiler_params=_params(("arbitrary",)),
    )(dcum, fl, b_f)


def _att_scores(q, kb, ck, row0, col0, masked):
    s = _dot(q, kb, _NT) - ck
    if masked:
        rows = row0 + lax.broadcasted_iota(jnp.int32, s.shape, 0)
        cols = col0 + lax.broadcasted_iota(jnp.int32, s.shape, 1)
        s = jnp.where(cols <= rows, s, NEG_INF)
    return s


def _att_fwd(qz, kv, ck):
    seq = qz.shape[0]
    heads = ck.shape[0]
    t = min(ATT_TILE, seq)
    scale = HEAD_DIM ** -0.5

    def body(q_ref, k_ref, v_ref, ck_ref, o_ref, lse_ref):
        i = pl.program_id(1)
        q = (q_ref[...] * scale).astype(BF16)

        def block(j, carry, masked):
            m, l, acc = carry
            rows = pl.ds(pl.multiple_of(j * t, t), t)
            s = _att_scores(q, k_ref[rows, :], ck_ref[0, j], i * t, j * t, masked)
            m_new = jnp.maximum(m, jnp.max(s, axis=1, keepdims=True))
            p = jnp.exp(s - m_new)
            alpha = jnp.exp(m - m_new)
            return m_new, alpha * l + jnp.sum(p, axis=1, keepdims=True), alpha * acc + _dot(p, v_ref[rows, :])

        init = (jnp.full((t, 1), NEG_INF, F32), jnp.zeros((t, 1), F32), jnp.zeros((t, HEAD_DIM), F32))
        carry = lax.fori_loop(0, i, functools.partial(block, masked=False), init)
        m, l, acc = block(i, carry, True)
        o_ref[...] = acc / l
        lse_ref[0] = m + jnp.log(l)

    return _pcall(
        body, name="att_fwd", grid=(heads, seq // t),
        in_specs=[pl.BlockSpec((t, HEAD_DIM), lambda h, i: (i, h)),
                  pl.BlockSpec((seq, HEAD_DIM), lambda h, i: (0, h)),
                  pl.BlockSpec((seq, HEAD_DIM), lambda h, i: (0, heads + h)),
                  pl.BlockSpec((1, seq // t, 1, t), lambda h, i: (h, 0, 0, 0))],
        out_specs=[pl.BlockSpec((t, HEAD_DIM), lambda h, i: (i, h)), pl.BlockSpec((1, t, 1), lambda h, i: (h, i, 0))],
        out_shape=[jax.ShapeDtypeStruct((seq, heads * HEAD_DIM), F32), jax.ShapeDtypeStruct((heads, seq, 1), F32)],
        compiler_params=_params(("parallel", "parallel")),
    )(qz, kv, kv, ck)


def _att_bwd_q(qz, kv, do, o, lse, ck):
    seq = qz.shape[0]
    heads = ck.shape[0]
    t = min(ATT_TILE, seq)
    scale = HEAD_DIM ** -0.5

    def body(q_ref, k_ref, v_ref, do_ref, o_ref, lse_ref, ck_ref, dq_ref, delta_ref):
        i = pl.program_id(1)
        q = (q_ref[...] * scale).astype(BF16)
        dob = do_ref[...].astype(BF16)
        delta = jnp.sum(do_ref[...] * o_ref[...], axis=1, keepdims=True)
        lse_v = lse_ref[0]

        def block(j, carry, masked):
            dq, pdp = carry
            rows = pl.ds(pl.multiple_of(j * t, t), t)
            kb = k_ref[rows, :]
            s = _att_scores(q, kb, ck_ref[0, j], i * t, j * t, masked)
            p = jnp.exp(s - lse_v)
            dp = _dot(dob, v_ref[rows, :], _NT)
            ds = p * (dp - delta)
            return dq + _dot(ds, kb), pdp + jnp.sum(p * dp, axis=1, keepdims=True)

        init = (jnp.zeros((t, HEAD_DIM), F32), jnp.zeros((t, 1), F32))
        carry = lax.fori_loop(0, i, functools.partial(block, masked=False), init)
        dq, pdp = block(i, carry, True)
        dq_ref[...] = (dq * scale).astype(dq_ref.dtype)
        delta_ref[0] = pdp

    qblk = pl.BlockSpec((t, HEAD_DIM), lambda h, i: (i, h))
    col = pl.BlockSpec((1, t, 1), lambda h, i: (h, i, 0))
    return _pcall(
        body, name="att_bwd_q", grid=(heads, seq // t),
        in_specs=[qblk, pl.BlockSpec((seq, HEAD_DIM), lambda h, i: (0, h)),
                  pl.BlockSpec((seq, HEAD_DIM), lambda h, i: (0, heads + h)), qblk, qblk, col,
                  pl.BlockSpec((1, seq // t, 1, t), lambda h, i: (h, 0, 0, 0))],
        out_specs=[qblk, col],
        out_shape=[jax.ShapeDtypeStruct((seq, heads * HEAD_DIM), BF16), jax.ShapeDtypeStruct((heads, seq, 1), F32)],
        compiler_params=_params(("parallel", "parallel")),
    )(qz, kv, kv, do, o, lse, ck)


def _att_bwd_kv(qz, kv, do, lse, delta, ck):
    seq = qz.shape[0]
    heads = ck.shape[0]
    t = min(ATT_TILE, seq)
    nq = seq // t
    scale = HEAD_DIM ** -0.5

    def body(q_ref, k_ref, v_ref, do_ref, lse_ref, delta_ref, ck_ref, dk_ref, dv_ref, dck_ref):
        j = pl.program_id(1)
        kb, vb = k_ref[...], v_ref[...]
        ckv = ck_ref[0, 0]

        def block(i, carry, masked):
            dk, dv, dck = carry
            rows = pl.ds(pl.multiple_of(i * t, t), t)
            qb = (q_ref[rows, :] * scale).astype(BF16)
            dob = do_ref[rows, :].astype(BF16)
            s = _att_scores(qb, kb, ckv, i * t, j * t, masked)
            p = jnp.exp(s - lse_ref[0, rows, :])
            ds = p * (_dot(dob, vb, _NT) - delta_ref[0, rows, :])
            return dk + _dot(ds, qb, _TN), dv + _dot(p, dob, _TN), dck - _colsum(ds)

        init = (jnp.zeros((t, HEAD_DIM), F32), jnp.zeros((t, HEAD_DIM), F32), jnp.zeros((1, t), F32))
        carry = block(j, init, True)
        dk, dv, dck = lax.fori_loop(j + 1, nq, functools.partial(block, masked=False), carry)
        dk_ref[...] = dk.astype(dk_ref.dtype)
        dv_ref[...] = dv.astype(dv_ref.dtype)
        dck_ref[0, 0] = dck

    head = pl.BlockSpec((seq, HEAD_DIM), lambda h, j: (0, h))
    col = pl.BlockSpec((1, seq, 1), lambda h, j: (h, 0, 0))
    kblk = pl.BlockSpec((t, HEAD_DIM), lambda h, j: (j, h))
    row = pl.BlockSpec((1, 1, 1, t), lambda h, j: (h, j, 0, 0))
    return _pcall(
        body, name="att_bwd_kv", grid=(heads, nq),
        in_specs=[head, kblk, pl.BlockSpec((t, HEAD_DIM), lambda h, j: (j, heads + h)), head, col, col, row],
        out_specs=[kblk, kblk, row],
        out_shape=[jax.ShapeDtypeStruct((seq, heads * HEAD_DIM), BF16), jax.ShapeDtypeStruct((seq, heads * HEAD_DIM), BF16),
                   jax.ShapeDtypeStruct((heads, nq, 1, t), F32)],
        compiler_params=_params(("parallel", "parallel")),
    )(qz, kv, kv, do, lse, delta, ck)


def _mesh_pos():
    return lax.axis_index("x"), lax.axis_index("y"), lax.axis_index("c")


def _other_chips(x, y):
    return [(1 - x, y), (x, 1 - y), (1 - x, 1 - y)]


def _all_gather_weights(big, small):
    nb, ns = len(big), len(small)
    n_remote = 3 * (nb + ns)

    def plan(ins, outs, sems):
        send_sems, recv_sems, fwd_send, fwd_recv = sems
        x, y, c = _mesh_pos()
        chips = _other_chips(x, y)
        slots = [2 * cx + cy for cx, cy in chips]

        def half(ref, hc):
            rh = ref.shape[-2] // 2
            return ref.at[pl.ds(hc * rh, rh), :]

        def remote(i, j, src_chip, from_in):
            if i < nb:
                src = half(ins[i], c) if from_in else half(outs[i].at[src_chip], c)
                dst = half(outs[i].at[src_chip], c)
            else:
                src = ins[i] if from_in else outs[i].at[src_chip]
                dst = outs[i].at[src_chip]
            k = 3 * i + j
            return pltpu.make_async_remote_copy(src_ref=src, dst_ref=dst, send_sem=send_sems.at[k],
                                                recv_sem=recv_sems.at[k], device_id=(*chips[j], c),
                                                device_id_type=MESH_ID)

        def forward(i, j, hc):
            part = half(outs[i].at[slots[j]], hc)
            k = 3 * i + j
            return pltpu.make_async_remote_copy(src_ref=part, dst_ref=part, send_sem=fwd_send.at[k],
                                                recv_sem=fwd_recv.at[k], device_id=(x, y, 1 - c),
                                                device_id_type=MESH_ID)

        return remote, forward, 2 * x + y, slots, c

    def start(ins, outs, sems):
        remote, _, me, _, _ = plan(ins, outs, sems)
        for i in range(nb + ns):
            for j in range(3):
                remote(i, j, me, True).start()

    def finish(ins, outs, sems):
        remote, forward, me, slots, c = plan(ins, outs, sems)
        for i in range(nb + ns):
            for j in range(3):
                remote(i, j, slots[j], False).wait_recv()
                if i < nb:
                    forward(i, j, c).start()
        for i in range(nb):
            for j in range(3):
                forward(i, j, 1 - c).wait_recv()
        for i in range(nb + ns):
            for j in range(3):
                remote(i, j, me, True).wait_send()
                if i < nb:
                    forward(i, j, c).wait_send()

    arrays = tuple(big) + tuple(small)
    return _Exchange(
        arrays=arrays,
        out_shapes=tuple(jax.ShapeDtypeStruct((N_CHIPS,) + a.shape, a.dtype) for a in arrays),
        scratch=(pltpu.SemaphoreType.DMA((n_remote,)), pltpu.SemaphoreType.DMA((n_remote,)),
                 pltpu.SemaphoreType.DMA((3 * max(nb, 1),)), pltpu.SemaphoreType.DMA((3 * max(nb, 1),))),
        start=start, finish=finish)


def _swap_halves(grads):
    n = len(grads)

    def copies(ins, outs, sems):
        x, y, c = _mesh_pos()
        cps = []
        for i in range(n):
            rh = ins[i].shape[1] // 2
            cps.append(pltpu.make_async_remote_copy(
                src_ref=ins[i].at[:, pl.ds((1 - c) * rh, rh), :], dst_ref=outs[i], send_sem=sems[0].at[i],
                recv_sem=sems[1].at[i], device_id=(x, y, 1 - c), device_id_type=MESH_ID))
        return cps

    def start(ins, outs, sems):
        for cp in copies(ins, outs, sems):
            cp.start()

    def finish(ins, outs, sems):
        for cp in copies(ins, outs, sems):
            cp.wait()

    return _Exchange(
        arrays=tuple(grads),
        out_shapes=tuple(jax.ShapeDtypeStruct((g.shape[0], g.shape[1] // 2, g.shape[2]), g.dtype) for g in grads),
        scratch=(pltpu.SemaphoreType.DMA((n,)), pltpu.SemaphoreType.DMA((n,))),
        start=start, finish=finish)


def _pair_sum_bf16(g, theirs, pos, name):
    n, rh, cdim = theirs.shape
    tr = min(ROW_TILE, rh)
    nb = rh // tr

    def body(pos_ref, g_ref, t_ref, o_ref):
        o_ref[...] = (g_ref[...] + t_ref[...]).astype(BF16)

    grid_spec = pltpu.PrefetchScalarGridSpec(
        num_scalar_prefetch=1, grid=(n, nb),
        in_specs=[pl.BlockSpec((None, tr, cdim), lambda s, i, pos: (s, pos[1] * nb + i, 0)),
                  pl.BlockSpec((None, tr, cdim), lambda s, i, pos: (s, i, 0))],
        out_specs=pl.BlockSpec((None, tr, cdim), lambda s, i, pos: (s, i, 0)))
    return _pcall(body, name=name, grid_spec=grid_spec, out_shape=jax.ShapeDtypeStruct(theirs.shape, BF16),
                  compiler_params=_params(("parallel", "parallel")))(pos, g, theirs)


def _chip_sum(g, theirs, recv, pos, name):
    n, rh, cdim = theirs.shape
    tr = min(ROW_TILE, rh)
    nb = rh // tr

    def body(pos_ref, g_ref, t_ref, r0, r1, r2, o_ref):
        o_ref[...] = (((g_ref[...] + t_ref[...]) + r0[...]) + r1[...]) + r2[...]

    grid_spec = pltpu.PrefetchScalarGridSpec(
        num_scalar_prefetch=1, grid=(nb,),
        in_specs=[pl.BlockSpec((None, tr, cdim), lambda i, pos: (pos[0], pos[1] * nb + i, 0)),
                  pl.BlockSpec((None, tr, cdim), lambda i, pos: (pos[0], i, 0))]
        + [pl.BlockSpec((None, tr, cdim), functools.partial(lambda i, pos, j: (j, i, 0), j=j)) for j in range(3)],
        out_specs=pl.BlockSpec((tr, cdim), lambda i, pos: (i, 0)))
    return _pcall(body, name=name, grid_spec=grid_spec, out_shape=jax.ShapeDtypeStruct((rh, cdim), F32),
                  compiler_params=_params(("parallel",)))(pos, g, theirs, recv, recv, recv)


def _scatter_to_owner(parts):
    n = len(parts)

    def copies(ins, outs, sems):
        x, y, c = _mesh_pos()
        chips = _other_chips(x, y)
        cps = []
        for i in range(n):
            for j in range(3):
                k = 3 * i + j
                cps.append(pltpu.make_async_remote_copy(
                    src_ref=ins[i].at[2 * chips[j][0] + chips[j][1]], dst_ref=outs[i].at[j],
                    send_sem=sems[0].at[k], recv_sem=sems[1].at[k], device_id=(*chips[j], c),
                    device_id_type=MESH_ID))
        return cps

    def start(ins, outs, sems):
        for cp in copies(ins, outs, sems):
            cp.start()

    def finish(ins, outs, sems):
        for cp in copies(ins, outs, sems):
            cp.wait()

    return _Exchange(
        arrays=tuple(parts),
        out_shapes=tuple(jax.ShapeDtypeStruct((3,) + p.shape[1:], p.dtype) for p in parts),
        scratch=(pltpu.SemaphoreType.DMA((3 * n,)), pltpu.SemaphoreType.DMA((3 * n,))),
        start=start, finish=finish)


def _join_halves(halves):
    n = len(halves)

    def body(*refs):
        ins, outs, send_sems, recv_sems = refs[:n], refs[n:2 * n], refs[2 * n], refs[2 * n + 1]
        x, y, c = _mesh_pos()
        cps = []
        for i in range(n):
            rh = ins[i].shape[0]
            cps.append(pltpu.make_async_remote_copy(
                src_ref=ins[i], dst_ref=outs[i].at[pl.ds(c * rh, rh), :], send_sem=send_sems.at[i],
                recv_sem=recv_sems.at[i], device_id=(x, y, 1 - c), device_id_type=MESH_ID))
        for cp in cps:
            cp.start()
        for cp in cps:
            cp.wait_send()
        for i in range(n):
            rh = ins[i].shape[0]
            theirs = outs[i].at[pl.ds((1 - c) * rh, rh), :]
            pltpu.make_async_remote_copy(src_ref=ins[i], dst_ref=theirs, send_sem=send_sems.at[i],
                                         recv_sem=recv_sems.at[i], device_id=(x, y, 1 - c),
                                         device_id_type=MESH_ID).wait_recv()

    return _pcall(
        body, name="grad_join_halves", in_specs=[ANY] * n, out_specs=[ANY] * n,
        out_shape=[jax.ShapeDtypeStruct((2 * h.shape[0], h.shape[1]), h.dtype) for h in halves],
        scratch_shapes=[pltpu.SemaphoreType.DMA((n,)), pltpu.SemaphoreType.DMA((n,))],
        compiler_params=pltpu.CompilerParams(has_side_effects=True),
    )(*halves)


def _all_reduce_small(v):
    def body(v_ref, o_ref, recv, send_sems, recv_sems):
        x, y, c = _mesh_pos()
        peers = [(x, y, 1 - c), (1 - x, y, c), (x, 1 - y, c)]
        for s, peer in enumerate(peers):
            src = v_ref if s == 0 else o_ref
            cp = pltpu.make_async_remote_copy(src_ref=src, dst_ref=recv.at[s], send_sem=send_sems.at[s],
                                              recv_sem=recv_sems.at[s], device_id=peer, device_id_type=MESH_ID)
            cp.start()
            cp.wait()
            o_ref[...] = src[...] + recv[s]

    vm = pl.BlockSpec(memory_space=pltpu.VMEM)
    return _pcall(
        body, name="all_reduce_small", in_specs=[vm], out_specs=vm,
        out_shape=jax.ShapeDtypeStruct(v.shape, v.dtype),
        scratch_shapes=[pltpu.VMEM((3,) + v.shape, v.dtype), pltpu.SemaphoreType.DMA((3,)), pltpu.SemaphoreType.DMA((3,))],
        compiler_params=pltpu.CompilerParams(vmem_limit_bytes=VMEM_LIMIT, has_side_effects=True),
    )(v)


def _adamw_math(w, g, m, v):
    m = ADAM_B1 * m + (1.0 - ADAM_B1) * g
    v = ADAM_B2 * v + (1.0 - ADAM_B2) * (g * g)
    m_hat = m / (1.0 - ADAM_B1 ** ADAM_STEP)
    v_hat = v / (1.0 - ADAM_B2 ** ADAM_STEP)
    delta = -ADAM_LR * (m_hat / (jnp.sqrt(v_hat) + ADAM_EPS) + ADAM_WD * w)
    return delta, m, v


def _adamw(w, g, m, v, name):
    wd = w.shape[1]
    return _rows(_adamw_math, [_full(w), _full(g), _full(m), _full(v)], [(wd, F32)] * 3, name=name)


def _pack(arrs, rows):
    flat = jnp.concatenate([a.reshape(-1) for a in arrs])
    return jnp.pad(flat, (0, rows * LANES - flat.shape[0])).reshape(rows, LANES)


def _unpack(buf, like):
    flat = buf.reshape(-1)
    out, off = [], 0
    for a in like:
        out.append(flat[off:off + a.size].reshape(a.shape))
        off += a.size
    return out


def kernel(x, norm_pre, norm_post, s5_w_in, s5_a_re, s5_a_im, s5_log_dt, s5_b_re, s5_b_im, s5_c_re, s5_c_im, s5_d, s5_w_glu, s5_b_glu, s5_w_out, kv_norm, kv_w, kv_b_f, fox_w_in, fox_w_out, loss_target, m_norm_pre, m_norm_post, m_s5_w_in, m_s5_a_re, m_s5_a_im, m_s5_log_dt, m_s5_b_re, m_s5_b_im, m_s5_c_re, m_s5_c_im, m_s5_d, m_s5_w_glu, m_s5_b_glu, m_s5_w_out, m_kv_norm, m_kv_w, m_kv_b_f, m_fox_w_in, m_fox_w_out, v_norm_pre, v_norm_post, v_s5_w_in, v_s5_a_re, v_s5_a_im, v_s5_log_dt, v_s5_b_re, v_s5_b_im, v_s5_c_re, v_s5_c_im, v_s5_d, v_s5_w_glu, v_s5_b_glu, v_s5_w_out, v_kv_norm, v_kv_w, v_kv_b_f, v_fox_w_in, v_fox_w_out):
    seq, dm = x.shape[1], x.shape[2]
    width = dm
    heads = dm // HEAD_DIM
    fw = heads * HEAD_DIM
    groups = width // S5_GROUP
    chip = 2 * lax.axis_index("x") + lax.axis_index("y")

    big_shards = [s5_w_in[0], s5_w_glu[0], s5_w_out[0], kv_w, fox_w_in[0], fox_w_out[0]]
    own_shards = [w.astype(BF16) for w in big_shards] + [s5_d, s5_b_glu]
    fill_own = lambda gs, owns: [lax.dynamic_update_slice(g, own[None], (chip, 0, 0)) for g, own in zip(gs, owns)]
    c_idx = lax.axis_index("c")
    pos = jnp.stack([chip, c_idx]).astype(jnp.int32)
    first_owns = [own_shards[0], s5_d, s5_b_glu]
    g_win, g_d, g_bglu = fill_own(
        _exchange_call(_all_gather_weights(first_owns[:1], first_owns[1:]), "all_gather_first"), first_owns)
    gather_rest = _all_gather_weights(own_shards[1:5], [])
    gather_last = _all_gather_weights(own_shards[5:6], [])
    cols = lambda g: jnp.moveaxis(g, 0, 1).reshape(g.shape[1], -1)
    rows = lambda g: g.reshape(-1, g.shape[2])
    w_in = g_win
    d_skip, b_glu = cols(g_d), cols(g_bglu)
    b_f = jnp.pad(kv_b_f, (0, LANES - heads)).reshape(1, LANES)

    h0 = x[0]
    target = loss_target[0]
    g_pre0, g_pre1 = norm_pre[0:1], norm_pre[1:2]
    g_post0, g_post1 = norm_post[0:1], norm_post[1:2]
    g_kv = kv_norm.reshape(1, dm)

    a_re, a_im, log_dt = s5_a_re[0], s5_a_im[0], s5_log_dt[0]
    disc, disc_vjp = jax.vjp(_ssm_discretize, a_re, a_im, log_dt)
    gpb = SSM_CH // S5_GROUP
    n_cb = groups // gpb
    par = jnp.stack([p.reshape(n_cb, gpb * S5_STATE) for p in disc], axis=1)
    b_t = lambda b: jnp.swapaxes(b, 1, 2)
    b_blk = jnp.stack([_block_diag(b_t(s5_b_re[0])), _block_diag(b_t(s5_b_im[0]))], axis=1)
    ct_blk = jnp.stack([_block_diag(s5_c_re[0]), _block_diag(s5_c_im[0])], axis=1)
    c_blk = jnp.swapaxes(ct_blk, 2, 3)

    xn1 = _to_slab(_rows(lambda h, g: (h * _rstd(h) * g,), [_full(h0)], [(dm, BF16)], consts=[g_pre0], name="norm_pre0")[0])
    uz = _mm(xn1, w_in, name="s5_in")
    ys, y1b, *rest = _ssm_fwd(uz, b_blk, c_blk, par, d_skip, carry=gather_rest)
    g_wglu, g_wout, g_kvw, g_fwin = fill_own(rest, own_shards[1:5])
    w_glu, w_out = rows(g_wglu), rows(g_wout)
    kvw_full = cols(g_kvw)
    w_kv = kvw_full[:, :2 * fw]
    w_f = jnp.pad(kvw_full[:, 2 * fw:], ((0, 0), (0, LANES - heads)))
    fw_in = g_fwin
    glu_a = _mm(y1b, w_glu, name="s5_glu")

    def gate_fn(y, a, z, b):
        return (_gelu(y) * _sigmoid(a + b) * _silu(z),)

    y3b = _rows(gate_fn, [_full(ys), _full(glu_a), (uz, width, 1)], [(width, BF16)], consts=[b_glu], name="s5_gate")[0]
    o1 = _from_slab(_mm(y3b, w_out, name="s5_out"))

    def mid_fn(h, o, gp, gk, gq):
        h1 = h + o * _rstd(o) * gp
        r = _rstd(h1)
        return h1, h1 * r * gk, h1 * r * gq

    h1, xk, xn2 = _rows(mid_fn, [_full(h0), _full(o1)], [(dm, F32), (dm, BF16), (dm, BF16)],
                        consts=[g_post0, g_kv, g_pre1], name="mid_norms")

    kv, g_fwout = _mm(xk, w_kv, out_dtype=BF16, name="kv_proj", carry=gather_last)
    fw_out = rows(fill_own([g_fwout], own_shards[5:6])[0])
    fl = _mm(xk, w_f, name="f_proj")
    qz = _mm(xn2, fw_in, name="fox_in")
    cum = _cum_fwd(fl, b_f)
    t_att = min(ATT_TILE, seq)
    cum_t = cum[:, :heads].T
    ck = cum_t.reshape(heads, seq // t_att, 1, t_att)
    o, lse = _att_fwd(qz, kv, ck)
    o2b = _rows(lambda a, z: (a * _silu(z),), [_full(o), (qz, fw, 1)], [(fw, BF16)], name="fox_gate")[0]
    o3 = _mm(o2b, fw_out, name="fox_out")

    def loss_fn(h, o, t, g):
        r = _rstd(o)
        err = h + o * r * g - t
        dh = err * (1.0 / dm)
        do, dg = _rms_bwd(o, g, dh)
        part = 0.5 * jnp.sum(jnp.mean(err * err, axis=-1, keepdims=True), axis=0, keepdims=True)
        return dh, do, jnp.broadcast_to(part, (1, LANES)), _colsum(dg)

    dh2, do3, loss_part, dg_post1 = _rows(loss_fn, [_full(h1), _full(o3), _full(target)], [(dm, F32), (dm, BF16)],
                                          consts=[g_post1], accs=[(1, LANES), (1, dm)], name="loss_head")
    loss = lax.psum(loss_part[0, 0], MESH_AXES)

    do2 = _mm(do3, fw_out, tb=True, name="fox_out_dx")
    dw_fout = _mm(o2b, do3, ta=True, name="fox_out_dw")

    def fox_gate_bwd(d, a, z):
        return d * _silu(z), d * a * _dsilu(z)

    do, dz2 = _rows(fox_gate_bwd, [_full(do2), _full(o), (qz, fw, 1)], [(fw, F32), (fw, BF16)], name="fox_gate_bwd")
    dq, delta = _att_bwd_q(qz, kv, do, o, lse, ck)
    dk, dv, dck = _att_bwd_kv(qz, kv, do, lse, delta, ck)
    dcum = jnp.pad(dck.reshape(heads, seq).T, ((0, 0), (0, LANES - heads)))
    dfl, db_f = _cum_bwd(dcum, fl, b_f)
    dqz = (dq, dz2)
    dkv = (dk, dv)
    dxn2 = _mm(dqz, fw_in, tb=True, name="fox_in_dx")
    dw_fin = _mm(xn2, dqz, ta=True, out_split=N_CHIPS, name="fox_in_dw")
    dxk_f = _mm(dfl, w_f, tb=True, name="f_proj_dx")
    dxk = _mm(dkv, w_kv, tb=True, add=dxk_f, name="kv_proj_dx")
    dw_kv = _mm(xk, dkv, ta=True, name="kv_proj_dw")
    dw_f = _mm(xk, dfl, ta=True, name="f_proj_dw")

    def mid_bwd(d2, h, dq_, dk_, o, gq, gk, gp):
        dxa, dga = _rms_bwd(h, gq, dq_)
        dxb, dgb = _rms_bwd(h, gk, dk_)
        dh = d2 + dxa + dxb
        do_, dgp = _rms_bwd(o, gp, dh)
        return dh, do_, _colsum(dga), _colsum(dgb), _colsum(dgp)

    to_cols = lambda g: jnp.moveaxis(g.reshape(g.shape[0], N_CHIPS, -1), 1, 0)
    to_rows = lambda g: g.reshape(N_CHIPS, -1, g.shape[1])
    dw_kv_full = jnp.concatenate([dw_kv, dw_f[:, :heads]], axis=1)
    early_grads = [to_cols(dw_kv_full), dw_fin, to_rows(dw_fout)]
    dh1, do1, dg_pre1, dg_kv, dg_post0, *early_theirs = _rows(
        mid_bwd, [_full(dh2), _full(h1), _full(dxn2), _full(dxk), _full(o1)], [(dm, F32), (dm, BF16)],
        consts=[g_pre1, g_kv, g_post0], accs=[(1, dm)] * 3, name="mid_norms_bwd", carry=_swap_halves(early_grads))
    early_sums = [_pair_sum_bf16(g, t, pos, f"grad_pair_sum_{3 + i}") for i, (g, t) in enumerate(zip(early_grads, early_theirs))]

    do1 = _to_slab(do1)
    dy3 = _mm(do1, w_out, tb=True, name="s5_out_dx")
    dw_out = _mm(y3b, do1, ta=True, name="s5_out_dw")

    def gate_bwd(d3, y, a, z, b):
        y1 = _gelu(y)
        gate = _sigmoid(a + b)
        dy2 = d3 * _silu(z)
        da = dy2 * y1 * gate * (1.0 - gate)
        return dy2 * gate, da, d3 * (y1 * gate) * _dsilu(z), _colsum(da)

    dy1_direct, da, dz, db_glu = _rows(gate_bwd, [_full(dy3), _full(ys), _full(glu_a), (uz, width, 1)],
                                       [(width, F32), (width, BF16), (width, BF16)], consts=[b_glu],
                                       accs=[(1, width)], name="s5_gate_bwd")
    dy1 = _mm(da, w_glu, tb=True, add=dy1_direct, name="s5_glu_dx")
    dw_glu = _mm(y1b, da, ta=True, name="s5_glu_dw")
    mid_grads = [to_rows(dw_glu), to_rows(dw_out)]
    du, dbt_blk, dct_blk, dpar, dd, *carried = _ssm_bwd(
        uz, dy1, ys, b_blk, ct_blk, par, d_skip, carry=_together(_scatter_to_owner(early_sums), _swap_halves(mid_grads)))
    early_recv, mid_theirs = carried[:3], carried[3:]
    mid_sums = [_pair_sum_bf16(g, t, pos, f"grad_pair_sum_{1 + i}") for i, (g, t) in enumerate(zip(mid_grads, mid_theirs))]
    duz = (du, dz)
    dw_in, *mid_recv = _mm(xn1, duz, ta=True, out_split=N_CHIPS, name="s5_in_dw", carry=_scatter_to_owner(mid_sums))
    late_grads = [dw_in]
    dxn1, *late_theirs = _mm(duz, w_in, tb=True, name="s5_in_dx", carry=_swap_halves(late_grads))
    dxn1 = _from_slab(dxn1)
    late_sums = [_pair_sum_bf16(dw_in, late_theirs[0], pos, "grad_pair_sum_0")]

    def first_bwd(d1, h, dxn, g):
        dx, dg = _rms_bwd(h, g, dxn)
        return d1 + dx, _colsum(dg)

    grad_x, dg_pre0, *late_recv = _rows(first_bwd, [_full(dh1), _full(h0), _full(dxn1)], [(dm, F32)], consts=[g_pre0],
                                        accs=[(1, dm)], name="norm_pre0_bwd", carry=_scatter_to_owner(late_sums))

    dpar_g = [dpar[:, i, :].reshape(groups, S5_STATE) for i in range(4)]
    da_re, da_im, dlog_dt = disc_vjp(tuple(dpar_g))
    db_re = jnp.swapaxes(_block_diag_extract(dbt_blk[:, 0]), 1, 2)
    db_im = jnp.swapaxes(_block_diag_extract(dbt_blk[:, 1]), 1, 2)
    dc_re = _block_diag_extract(dct_blk[:, 0])
    dc_im = _block_diag_extract(dct_blk[:, 1])

    small_local = [jnp.concatenate([dg_pre0, dg_pre1]), jnp.concatenate([dg_post0, dg_post1]),
                   da_re[None], da_im[None], dlog_dt[None], db_re[None], db_im[None], dc_re[None], dc_im[None],
                   dd, db_glu, dg_kv.reshape(dm), db_f[0, :heads]]
    n_small = sum(a.size for a in small_local)
    small_rows = -(-n_small // (LANES * ROW_TILE)) * ROW_TILE
    small_sum = _unpack(_all_reduce_small(_pack(small_local, small_rows)), small_local)
    (g_norm_pre, g_norm_post, g_a_re, g_a_im, g_log_dt, g_b_re, g_b_im, g_c_re, g_c_im, g_d_full, g_bglu_full,
     g_kv_norm, g_b_f) = small_sum
    shard = width // N_CHIPS
    g_d_own = lax.dynamic_slice(g_d_full, (0, chip * shard), (1, shard))
    g_bglu_own = lax.dynamic_slice(g_bglu_full, (0, chip * shard), (1, shard))

    big_grads = late_grads + mid_grads + early_grads
    theirs = list(late_theirs) + list(mid_theirs) + list(early_theirs)
    received = list(late_recv) + list(mid_recv) + list(early_recv)
    halves = [_chip_sum(g, t, r, pos, f"grad_chip_sum_{i}") for i, (g, t, r) in enumerate(zip(big_grads, theirs, received))]
    joined = _join_halves(halves)
    g_win_s, g_wglu_s, g_wout_s, g_kvw_s, g_fwin_s, g_fwout_s = [
        lax.dynamic_update_slice(j, h, (c_idx * h.shape[0], 0)) for j, h in zip(joined, halves)]

    big_w = big_shards
    big_g = [g_win_s, g_wglu_s, g_wout_s, g_kvw_s, g_fwin_s, g_fwout_s]
    big_m = [m_s5_w_in[0], m_s5_w_glu[0], m_s5_w_out[0], m_kv_w, m_fox_w_in[0], m_fox_w_out[0]]
    big_v = [v_s5_w_in[0], v_s5_w_glu[0], v_s5_w_out[0], v_kv_w, v_fox_w_in[0], v_fox_w_out[0]]
    big_upd = [_adamw(w, g, m, v, f"adamw_{i}") for i, (w, g, m, v) in enumerate(zip(big_w, big_g, big_m, big_v))]

    small_names = ["norm_pre", "norm_post", "s5_a_re", "s5_a_im", "s5_log_dt", "s5_b_re", "s5_b_im", "s5_c_re", "s5_c_im",
                   "s5_d", "s5_b_glu", "kv_norm", "kv_b_f"]
    small_w = [norm_pre, norm_post, s5_a_re, s5_a_im, s5_log_dt, s5_b_re, s5_b_im, s5_c_re, s5_c_im, s5_d, s5_b_glu, kv_norm, kv_b_f]
    small_m = [m_norm_pre, m_norm_post, m_s5_a_re, m_s5_a_im, m_s5_log_dt, m_s5_b_re, m_s5_b_im, m_s5_c_re, m_s5_c_im, m_s5_d, m_s5_b_glu, m_kv_norm, m_kv_b_f]
    small_v = [v_norm_pre, v_norm_post, v_s5_a_re, v_s5_a_im, v_s5_log_dt, v_s5_b_re, v_s5_b_im, v_s5_c_re, v_s5_c_im, v_s5_d, v_s5_b_glu, v_kv_norm, v_kv_b_f]
    small_g = [g_norm_pre, g_norm_post, g_a_re, g_a_im, g_log_dt, g_b_re, g_b_im, g_c_re, g_c_im, g_d_own, g_bglu_own, g_kv_norm, g_b_f]
    small_g = [g.reshape(w.shape) for g, w in zip(small_g, small_w)]
    n_own = sum(a.size for a in small_w)
    own_rows = -(-n_own // (LANES * ROW_TILE)) * ROW_TILE
    pv = _pack(small_v, own_rows)
    pv = jnp.where(jnp.arange(own_rows * LANES).reshape(own_rows, LANES) < n_own, pv, 1.0)
    sd, sm, sv = _adamw(_pack(small_w, own_rows), _pack(small_g, own_rows), _pack(small_m, own_rows), pv, "adamw_small")
    small_delta, small_newm, small_newv = _unpack(sd, small_w), _unpack(sm, small_w), _unpack(sv, small_w)

    order = ["norm_pre", "norm_post", "s5_w_in", "s5_a_re", "s5_a_im", "s5_log_dt", "s5_b_re", "s5_b_im", "s5_c_re", "s5_c_im",
             "s5_d", "s5_w_glu", "s5_b_glu", "s5_w_out", "kv_norm", "kv_w", "kv_b_f", "fox_w_in", "fox_w_out"]
    big_names = ["s5_w_in", "s5_w_glu", "s5_w_out", "kv_w", "fox_w_in", "fox_w_out"]
    big_like = [s5_w_in, s5_w_glu, s5_w_out, kv_w, fox_w_in, fox_w_out]
    grads, deltas, new_m, new_v = {}, {}, {}, {}
    for i, n in enumerate(big_names):
        shp = big_like[i].shape
        grads[n] = big_g[i].reshape(shp)
        deltas[n], new_m[n], new_v[n] = (a.reshape(shp) for a in big_upd[i])
    for i, n in enumerate(small_names):
        grads[n], deltas[n], new_m[n], new_v[n] = small_g[i], small_delta[i], small_newm[i], small_newv[i]

    return (loss, grad_x[None], *[grads[n] for n in order], *[deltas[n] for n in order],
            *[new_m[n] for n in order], *[new_v[n] for n in order])
```

```python
import functools
import math
from typing import Callable, NamedTuple

import jax
import jax.numpy as jnp
from jax import lax
from jax.experimental import pallas as pl
from jax.experimental.pallas import tpu as pltpu

F32 = jnp.float32
BF16 = jnp.bfloat16

D_MODEL = 2048
SEQ = 4096
S5_GROUP = 16
S5_STATE = 64
HEAD_DIM = 128
RMS_EPS = 1e-6
NEG_INF = -1e30
ADAM_LR = 0.001
ADAM_B1 = 0.9
ADAM_B2 = 0.999
ADAM_EPS = 1e-08
ADAM_WD = 0.01
ADAM_STEP = 10

LANES = 128
SUBLANES = 8
VMEM_LIMIT = 56 * 1024 * 1024
N_CHIPS = 4
MESH_AXES = ("x", "y", "c")
MESH_ID = pl.DeviceIdType.MESH

SSM_CH = 128
ROW_TILE = 256
SCAN_ROWS = 512
SCAN_UNROLL = 4
ATT_TILE = 512
CUM_TILE = 512


def _pcall(body, **kw):
    return pl.pallas_call(body, **kw)


def _params(sem=None):
    if sem is None:
        return pltpu.CompilerParams(vmem_limit_bytes=VMEM_LIMIT)
    return pltpu.CompilerParams(vmem_limit_bytes=VMEM_LIMIT, dimension_semantics=sem)


def _sigmoid(x):
    return 1.0 / (1.0 + jnp.exp(-x))


def _silu(z):
    return z * _sigmoid(z)


def _dsilu(z):
    s = _sigmoid(z)
    return s * (1.0 + z * (1.0 - s))


_GELU_C = math.sqrt(2.0 / math.pi)


def _gelu(x):
    return 0.5 * x * (1.0 + jnp.tanh(_GELU_C * (x + 0.044715 * x * x * x)))


def _dgelu(x):
    t = jnp.tanh(_GELU_C * (x + 0.044715 * x * x * x))
    return 0.5 * (1.0 + t) + 0.5 * x * (1.0 - t * t) * _GELU_C * (1.0 + 3.0 * 0.044715 * x * x)


def _rstd(x):
    return lax.rsqrt(jnp.mean(x * x, axis=-1, keepdims=True) + RMS_EPS)


def _rms_bwd(x, g, dy):
    r = _rstd(x)
    dyg = dy * g
    dx = r * dyg - x * (r * r * r) * jnp.mean(dyg * x, axis=-1, keepdims=True)
    return dx, dy * (x * r)


def _colsum(v):
    return jnp.sum(v, axis=0, keepdims=True)


ANY = pl.BlockSpec(memory_space=pl.ANY)


class _Exchange(NamedTuple):
    arrays: tuple
    out_shapes: tuple
    scratch: tuple
    start: Callable
    finish: Callable


def _together(*exs):
    def parts(seq, field):
        out, off = [], 0
        for e in exs:
            n = len(getattr(e, field))
            out.append(seq[off:off + n])
            off += n
        return out

    def start(ins, outs, sems):
        for e, i, o, s in zip(exs, parts(ins, "arrays"), parts(outs, "out_shapes"), parts(sems, "scratch")):
            e.start(i, o, s)

    def finish(ins, outs, sems):
        for e, i, o, s in zip(exs, parts(ins, "arrays"), parts(outs, "out_shapes"), parts(sems, "scratch")):
            e.finish(i, o, s)

    return _Exchange(arrays=sum((tuple(e.arrays) for e in exs), ()), out_shapes=sum((tuple(e.out_shapes) for e in exs), ()),
                     scratch=sum((tuple(e.scratch) for e in exs), ()), start=start, finish=finish)


def _exchange_call(ex, name):
    n_in, n_out = len(ex.arrays), len(ex.out_shapes)

    def body(*refs):
        ins, outs, sems = refs[:n_in], refs[n_in:n_in + n_out], refs[n_in + n_out:]
        ex.start(ins, outs, sems)
        ex.finish(ins, outs, sems)

    return _pcall(body, name=name, in_specs=[ANY] * n_in, out_specs=[ANY] * n_out, out_shape=list(ex.out_shapes),
                  scratch_shapes=list(ex.scratch), compiler_params=pltpu.CompilerParams(has_side_effects=True))(*ex.arrays)


def _carry(ex, refs, n_fixed_in, n_fixed_out):
    if ex is None:
        return refs, lambda cond: None, lambda cond: None
    n_in, n_out, n_sem = len(ex.arrays), len(ex.out_shapes), len(ex.scratch)
    fixed_in = refs[:n_fixed_in]
    ex_in = refs[n_fixed_in:n_fixed_in + n_in]
    rest = refs[n_fixed_in + n_in:]
    fixed_out = rest[:n_fixed_out]
    ex_out = rest[n_fixed_out:n_fixed_out + n_out]
    scratch = rest[n_fixed_out + n_out:]
    sems = scratch[len(scratch) - n_sem:]

    def start_when(cond):
        pl.when(cond)(lambda: ex.start(ex_in, ex_out, sems))

    def finish_when(cond):
        pl.when(cond)(lambda: ex.finish(ex_in, ex_out, sems))

    return tuple(fixed_in) + tuple(fixed_out) + tuple(scratch[:len(scratch) - n_sem]), start_when, finish_when


def _carry_specs(ex):
    if ex is None:
        return (), [], [], [], []
    return ex.arrays, [ANY] * len(ex.arrays), [ANY] * len(ex.out_shapes), list(ex.out_shapes), list(ex.scratch)


def _mm(a, b, *, ta=False, tb=False, out_dtype=F32, add=None, out_split=1, tm=1024, tn=1024, tk=2048, name, carry=None):
    def describe(op):
        if isinstance(op, (tuple, list)):
            assert all(p.ndim == 2 and p.shape == op[0].shape for p in op)
            return list(op), op[0].shape[0], op[0].shape[1], False
        if op.ndim == 3:
            return [op], op.shape[1], op.shape[2], True
        return [op], op.shape[0], op.shape[1], False

    a_parts, a_rows, a_pc, a_stack = describe(a)
    b_parts, b_rows, b_pc, b_stack = describe(b)
    a_cols = a_pc * (a.shape[0] if a_stack else len(a_parts))
    b_cols = b_pc * (b.shape[0] if b_stack else len(b_parts))
    k_dim, m_dim = (a_rows, a_cols) if ta else (a_cols, a_rows)
    n_dim, kb = (b_rows, b_cols) if tb else (b_cols, b_rows)
    assert kb == k_dim, (k_dim, kb)
    tm = min(tm, a_pc) if ta else min(tm, m_dim)
    tk = min(tk, k_dim, k_dim if ta else a_pc, b_pc if tb else k_dim)
    tn = min(tn, n_dim // out_split, n_dim if tb else b_pc)
    a_ct, b_ct = (tm if ta else tk), (tk if tb else tn)
    assert m_dim % tm == 0 and n_dim % tn == 0 and k_dim % tk == 0 and a_pc % a_ct == 0 and b_pc % b_ct == 0
    assert (n_dim // out_split) % tn == 0
    nk = k_dim // tk
    dims = (((0 if ta else 1,), (1 if tb else 0,)), ((), ()))
    n_a, n_b = len(a_parts), len(b_parts)
    assert n_a == 1 or n_b == 1

    def operand_specs(parts, stack, rows_t, cols_t, per, row_of, col_of):
        specs = []
        for p in range(len(parts)):
            def col(i, j, k, p=p):
                return jnp.clip(col_of(i, j, k) - p * per, 0, per - 1) if len(parts) > 1 else col_of(i, j, k)
            if stack:
                specs.append(pl.BlockSpec((None, rows_t, cols_t),
                                          lambda i, j, k, col=col: (col(i, j, k) // per, row_of(i, j, k), col(i, j, k) % per)))
            else:
                specs.append(pl.BlockSpec((rows_t, cols_t), lambda i, j, k, col=col: (row_of(i, j, k), col(i, j, k))))
        return specs

    if ta:
        a_specs = operand_specs(a_parts, a_stack, tk, tm, a_pc // tm, lambda i, j, k: k, lambda i, j, k: i)
    else:
        a_specs = operand_specs(a_parts, a_stack, tm, tk, a_pc // tk, lambda i, j, k: i, lambda i, j, k: k)
    if tb:
        b_specs = operand_specs(b_parts, b_stack, tn, tk, b_pc // tk, lambda i, j, k: j, lambda i, j, k: k)
    else:
        b_specs = operand_specs(b_parts, b_stack, tk, tn, b_pc // tn, lambda i, j, k: k, lambda i, j, k: j)

    n_fixed_in = n_a + n_b + (1 if add is not None else 0)
    grid = (m_dim // tm, n_dim // tn, nk)
    ex_args, ex_in_specs, ex_out_specs, ex_out_shapes, ex_scratch = _carry_specs(carry)

    def body(*refs):
        refs, start_when, finish_when = _carry(carry, refs, n_fixed_in, 1)
        i, j, k = pl.program_id(0), pl.program_id(1), pl.program_id(2)
        step = (i * grid[1] + j) * grid[2] + k
        start_when(step == 0)
        compute(*refs)
        finish_when(step == grid[0] * grid[1] * grid[2] - 1)

    def compute(*refs):
        a_refs, b_refs = refs[:n_a], refs[n_a:n_a + n_b]
        rest = refs[n_a + n_b:]
        c_ref = rest[0] if add is not None else None
        o_ref = rest[1] if add is not None else rest[0]
        acc = None if nk == 1 else rest[-1]
        i, j, k = pl.program_id(0), pl.program_id(1), pl.program_id(2)

        def finish(res):
            if add is not None:
                res = res + c_ref[...]
            o_ref[...] = res.astype(out_dtype)

        def accumulate(a_ref, b_ref):
            prod = lax.dot_general(a_ref[...].astype(BF16), b_ref[...].astype(BF16), dims,
                                   preferred_element_type=F32)
            if nk == 1:
                finish(prod)
                return

            @pl.when(k == 0)
            def _():
                acc[...] = prod

            @pl.when(jnp.logical_and(k > 0, k < nk - 1))
            def _():
                acc[...] += prod

            @pl.when(k == nk - 1)
            def _():
                finish(acc[...] + prod)

        if n_a == 1 and n_b == 1:
            accumulate(a_refs[0], b_refs[0])
        else:
            many, block, per = (a_refs, (i if ta else k), a_pc // a_ct) if n_a > 1 else (b_refs, (k if tb else j), b_pc // b_ct)
            for p, ref in enumerate(many):
                @pl.when(block // per == p)
                def _(ref=ref):
                    accumulate(ref, b_refs[0]) if n_a > 1 else accumulate(a_refs[0], ref)

    per_out = n_dim // out_split // tn
    if out_split > 1:
        o_spec = pl.BlockSpec((None, tm, tn), lambda i, j, k: (j // per_out, i, j % per_out))
        out_shape = jax.ShapeDtypeStruct((out_split, m_dim, n_dim // out_split), out_dtype)
    else:
        o_spec = pl.BlockSpec((tm, tn), lambda i, j, k: (i, j))
        out_shape = jax.ShapeDtypeStruct((m_dim, n_dim), out_dtype)
    in_specs = a_specs + b_specs + ([pl.BlockSpec((tm, tn), lambda i, j, k: (i, j))] if add is not None else [])
    args = tuple(a_parts) + tuple(b_parts) + ((add,) if add is not None else ())
    acc_scratch = [pltpu.VMEM((tm, tn), F32)] if nk > 1 else []
    if carry is None:
        return _pcall(
            body, name=name, grid=grid, in_specs=in_specs, out_specs=o_spec, out_shape=out_shape,
            scratch_shapes=acc_scratch, compiler_params=_params(("parallel", "parallel", "arbitrary")),
        )(*args)
    return _pcall(
        body, name=name, grid=grid, in_specs=in_specs + ex_in_specs, out_specs=[o_spec] + ex_out_specs,
        out_shape=[out_shape] + ex_out_shapes, scratch_shapes=acc_scratch + ex_scratch,
        compiler_params=pltpu.CompilerParams(vmem_limit_bytes=VMEM_LIMIT, has_side_effects=True,
                                             dimension_semantics=("arbitrary", "arbitrary", "arbitrary")),
    )(*args, *ex_args)


def _rows(fn, ins, outs, *, name, consts=(), accs=(), carry=None):
    n_rows = ins[0][0].shape[0]
    tr = min(ROW_TILE, n_rows)
    assert n_rows % tr == 0
    n_in, n_c, n_out = len(ins), len(consts), len(outs)
    n_steps = n_rows // tr
    ex_args, ex_in_specs, ex_out_specs, ex_out_shapes, ex_scratch = _carry_specs(carry)

    def body(*refs):
        refs, start_when, finish_when = _carry(carry, refs, n_in + n_c, n_out + len(accs))
        start_when(pl.program_id(0) == 0)
        _compute(*refs)
        finish_when(pl.program_id(0) == n_steps - 1)

    def _compute(*refs):
        vals = [r[...] for r in refs[:n_in + n_c]]
        res = fn(*vals)
        res = res if isinstance(res, (tuple, list)) else (res,)
        o_refs = refs[n_in + n_c:]
        for r, v in zip(o_refs[:n_out], res[:n_out]):
            r[...] = v.astype(r.dtype)
        if accs:
            first = pl.program_id(0) == 0
            for r, v in zip(o_refs[n_out:], res[n_out:]):
                @pl.when(first)
                def _(r=r, v=v):
                    r[...] = v

                @pl.when(jnp.logical_not(first))
                def _(r=r, v=v):
                    r[...] += v

    in_specs = [pl.BlockSpec((tr, w), functools.partial(lambda i, cb: (i, cb), cb=cb)) for _, w, cb in ins]
    in_specs += [pl.BlockSpec(c.shape, functools.partial(lambda i, nd: (0,) * nd, nd=c.ndim)) for c in consts]
    out_specs = [pl.BlockSpec((tr, w), lambda i: (i, 0)) for w, _ in outs]
    out_specs += [pl.BlockSpec(s, lambda i: (0, 0)) for s in accs]
    out_shape = [jax.ShapeDtypeStruct((n_rows, w), dt) for w, dt in outs]
    out_shape += [jax.ShapeDtypeStruct(s, F32) for s in accs]
    sequential = bool(accs) or carry is not None
    params = _params(("arbitrary",) if sequential else ("parallel",))
    if carry is not None:
        params = pltpu.CompilerParams(vmem_limit_bytes=VMEM_LIMIT, dimension_semantics=("arbitrary",), has_side_effects=True)
    return _pcall(
        body, name=name, grid=(n_steps,), in_specs=in_specs + ex_in_specs, out_specs=out_specs + ex_out_specs,
        out_shape=out_shape + ex_out_shapes, scratch_shapes=ex_scratch, compiler_params=params,
    )(*[a for a, _, _ in ins], *consts, *ex_args)


def _full(a):
    return (a, a.shape[1], 0)


def _cmul(ar, ai, br, bi):
    return ar * br - ai * bi, ar * bi + ai * br


def _seg_scans(scans, seg):
    assert seg & (seg - 1) == 0
    chains = []
    for re_ref, im_ref, a_re, a_im, reverse in scans:
        for k in range(len(a_re)):
            chains.append((re_ref, im_ref, k, jnp.broadcast_to(a_re[k], (SUBLANES, LANES)),
                           jnp.broadcast_to(a_im[k], (SUBLANES, LANES)), reverse))

    def slab(i, reverse):
        j = seg - 1 - i if reverse else i
        return pl.ds(pl.multiple_of(j * SUBLANES, SUBLANES), SUBLANES)

    def local(i, carry):
        out = []
        for n, (re_ref, im_ref, k, ar, ai, reverse) in enumerate(chains):
            hr, hi = _cmul(ar, ai, carry[2 * n], carry[2 * n + 1])
            hr = hr + re_ref[k, slab(i, reverse), :]
            hi = hi + im_ref[k, slab(i, reverse), :]
            re_ref[k, slab(i, reverse), :] = hr
            im_ref[k, slab(i, reverse), :] = hi
            out += [hr, hi]
        return tuple(out)

    zero = jnp.zeros((SUBLANES, LANES), F32)
    end = lax.fori_loop(0, seg, local, (zero,) * (2 * len(chains)))

    row = lax.broadcasted_iota(jnp.int32, (SUBLANES, LANES), 0)
    enter = []
    for n, (_, _, _, ar, ai, reverse) in enumerate(chains):
        edge = SUBLANES - 1 if reverse else 0
        shift = SUBLANES - 1 if reverse else 1
        pr, pi = ar, ai
        for _ in range(seg.bit_length() - 1):
            pr, pi = _cmul(pr, pi, pr, pi)
        tr_, ti_ = zero, zero
        for _ in range(SUBLANES - 1):
            vr, vi = _cmul(pr, pi, tr_, ti_)
            tr_ = jnp.where(row == edge, 0.0, pltpu.roll(vr + end[2 * n], shift, 0))
            ti_ = jnp.where(row == edge, 0.0, pltpu.roll(vi + end[2 * n + 1], shift, 0))
        enter += [tr_, ti_]

    def fix(i, carry):
        out = []
        for n, (re_ref, im_ref, k, ar, ai, reverse) in enumerate(chains):
            er, ei = _cmul(ar, ai, carry[2 * n], carry[2 * n + 1])
            re_ref[k, slab(i, reverse), :] += er
            im_ref[k, slab(i, reverse), :] += ei
            out += [er, ei]
        return tuple(out)

    lax.fori_loop(0, seg, fix, tuple(enter))
    per_scan, off = [], 0
    for scan in scans:
        per_scan.append(enter[off:off + 2 * len(scan[2])])
        off += 2 * len(scan[2])
    return per_scan


def _to_slab(a):
    s, w = a.shape
    return a.reshape(SUBLANES, s // SUBLANES, w).swapaxes(0, 1).reshape(s, w)


def _from_slab(a):
    s, w = a.shape
    return a.reshape(s // SUBLANES, SUBLANES, w).swapaxes(0, 1).reshape(s, w)


def _lane_blocks(v, n_k):
    return [v[:, k * LANES:(k + 1) * LANES] for k in range(n_k)]


def _gather_k(ref, rows, n_k):
    return jnp.concatenate([ref[k, rows, :] for k in range(n_k)], axis=1)


def _dot(a, b, dims=(((1,), (0,)), ((), ()))):
    return lax.dot_general(a.astype(BF16), b.astype(BF16), dims, preferred_element_type=F32)


_NT = (((1,), (1,)), ((), ()))
_TN = (((0,), (0,)), ((), ()))


def _ssm_fwd(uz, b_blk, c_blk, par, d_skip, carry=None):
    seq = uz.shape[0]
    width = d_skip.shape[1]
    ns = SSM_CH // S5_GROUP * S5_STATE
    n_k = ns // LANES
    seg = seq // SUBLANES
    tb = min(SCAN_ROWS, seq)
    n_cb = width // SSM_CH
    ex_args, ex_in_specs, ex_out_specs, ex_out_shapes, ex_scratch = _carry_specs(carry)

    def body(*refs):
        refs, start_when, finish_when = _carry(carry, refs, 5, 2)
        start_when(pl.program_id(0) == 0)
        compute(*refs)
        finish_when(pl.program_id(0) == n_cb - 1)

    def compute(u_ref, b_ref, c_ref, par_ref, d_ref, y_ref, g_ref, hre, him):
        coef_r, coef_i = par_ref[0, 2:3, :], par_ref[0, 3:4, :]
        for c0 in range(0, seq, tb):
            rows = pl.ds(c0, tb)
            ub = u_ref[rows, :]
            bur, bui = _dot(ub, b_ref[0, 0]), _dot(ub, b_ref[0, 1])
            xr, xi = coef_r * bur - coef_i * bui, coef_r * bui + coef_i * bur
            for k in range(n_k):
                hre[k, rows, :] = xr[:, k * LANES:(k + 1) * LANES]
                him[k, rows, :] = xi[:, k * LANES:(k + 1) * LANES]
        _seg_scans([(hre, him, _lane_blocks(par_ref[0, 0:1, :], n_k), _lane_blocks(par_ref[0, 1:2, :], n_k), False)], seg)
        for c0 in range(0, seq, tb):
            rows = pl.ds(c0, tb)
            y = _dot(_gather_k(hre, rows, n_k), c_ref[0, 0]) - _dot(_gather_k(him, rows, n_k), c_ref[0, 1])
            y = y + d_ref[...] * u_ref[rows, :]
            y_ref[rows, :] = y
            g_ref[rows, :] = _gelu(y).astype(BF16)

    blk = pl.BlockSpec((seq, SSM_CH), lambda i: (0, i))
    params = _params(("parallel",)) if carry is None else pltpu.CompilerParams(
        vmem_limit_bytes=VMEM_LIMIT, dimension_semantics=("arbitrary",), has_side_effects=True)
    return _pcall(
        body, name="ssm_fwd", grid=(n_cb,),
        in_specs=[blk,
                  pl.BlockSpec((1, 2, SSM_CH, ns), lambda i: (i, 0, 0, 0)),
                  pl.BlockSpec((1, 2, ns, SSM_CH), lambda i: (i, 0, 0, 0)),
                  pl.BlockSpec((1, 4, ns), lambda i: (i, 0, 0)),
                  pl.BlockSpec((1, SSM_CH), lambda i: (0, i))] + ex_in_specs,
        out_specs=[blk, blk] + ex_out_specs,
        out_shape=[jax.ShapeDtypeStruct((seq, width), F32), jax.ShapeDtypeStruct((seq, width), BF16)] + ex_out_shapes,
        scratch_shapes=[pltpu.VMEM((n_k, seq, LANES), F32), pltpu.VMEM((n_k, seq, LANES), F32)] + ex_scratch,
        compiler_params=params,
    )(uz, b_blk, c_blk, par, d_skip, *ex_args)


def _ssm_bwd(uz, dy1, ys, b_blk, ct_blk, par, d_skip, carry=None):
    seq = uz.shape[0]
    width = d_skip.shape[1]
    ns_all = SSM_CH // S5_GROUP * S5_STATE
    n_half = 2
    ns = ns_all // n_half
    n_k = ns // LANES
    seg = seq // SUBLANES
    tb = min(SCAN_ROWS, seq)
    n_cb = width // SSM_CH
    ex_args, ex_in_specs, ex_out_specs, ex_out_shapes, ex_scratch = _carry_specs(carry)

    def body(*refs):
        refs, start_when, finish_when = _carry(carry, refs, 7, 5)
        step = pl.program_id(0) * n_half + pl.program_id(1)
        start_when(step == 0)
        compute(*refs)
        finish_when(step == n_cb * n_half - 1)

    def compute(u_ref, dy_ref, ys_ref, b_ref, ct_ref, par_ref, d_ref,
                du_ref, dbt_ref, dct_ref, dpar_ref, dd_ref, hre, him, gre, gim):
        half = pl.program_id(1)
        a_r, a_i = par_ref[0, 0:1, :], par_ref[0, 1:2, :]
        coef_r, coef_i = par_ref[0, 2:3, :], par_ref[0, 3:4, :]

        def dys_of(rows):
            return dy_ref[rows, :] * _dgelu(ys_ref[rows, :])

        for c0 in range(0, seq, tb):
            rows = pl.ds(c0, tb)
            ub = u_ref[rows, :]
            bur, bui = _dot(ub, b_ref[0, 0]), _dot(ub, b_ref[0, 1])
            xr, xi = coef_r * bur - coef_i * bui, coef_r * bui + coef_i * bur
            dys = dys_of(rows)
            gr, gi = _dot(dys, ct_ref[0, 0]), -_dot(dys, ct_ref[0, 1])
            for k in range(n_k):
                lanes = slice(k * LANES, (k + 1) * LANES)
                hre[k, rows, :] = xr[:, lanes]
                him[k, rows, :] = xi[:, lanes]
                gre[k, rows, :] = gr[:, lanes]
                gim[k, rows, :] = gi[:, lanes]
        enter, _ = _seg_scans([(hre, him, _lane_blocks(a_r, n_k), _lane_blocks(a_i, n_k), False),
                               (gre, gim, _lane_blocks(a_r, n_k), _lane_blocks(-a_i, n_k), True)], seg)

        def corr(j, carry):
            acc, prev = carry
            acc_o, prev_o = [], []
            for k in range(n_k):
                sl = pl.ds(pl.multiple_of(j * SUBLANES, SUBLANES), SUBLANES)
                g_r, g_i = gre[k, sl, :], gim[k, sl, :]
                p_r, p_i = prev[2 * k], prev[2 * k + 1]
                acc_o += [acc[2 * k] + g_r * p_r + g_i * p_i, acc[2 * k + 1] + g_i * p_r - g_r * p_i]
                prev_o += [hre[k, sl, :], him[k, sl, :]]
            return tuple(acc_o), tuple(prev_o)

        zero = jnp.zeros((SUBLANES, LANES), F32)
        acc, _ = lax.fori_loop(0, seg, corr, ((zero,) * (2 * n_k), tuple(enter)), unroll=SCAN_UNROLL)
        da_r = jnp.concatenate([_colsum(acc[2 * k]) for k in range(n_k)], axis=1)
        da_i = jnp.concatenate([_colsum(acc[2 * k + 1]) for k in range(n_k)], axis=1)

        zeros_cn = jnp.zeros((SSM_CH, ns), F32)
        qt_r, qt_i, dct_r, dct_i = zeros_cn, zeros_cn, zeros_cn, zeros_cn
        dd = jnp.zeros((1, SSM_CH), F32)
        first = half == 0
        for c0 in range(0, seq, tb):
            rows = pl.ds(c0, tb)
            ub = u_ref[rows, :]
            dys = dys_of(rows)
            dct_r = dct_r + _dot(dys, _gather_k(hre, rows, n_k), _TN)
            dct_i = dct_i - _dot(dys, _gather_k(him, rows, n_k), _TN)
            g_r, g_i = _gather_k(gre, rows, n_k), _gather_k(gim, rows, n_k)
            qt_r = qt_r + _dot(ub, g_r, _TN)
            qt_i = qt_i + _dot(ub, g_i, _TN)
            dbu_r, dbu_i = coef_r * g_r + coef_i * g_i, coef_r * g_i - coef_i * g_r
            du = _dot(dbu_r, b_ref[0, 0], _NT) + _dot(dbu_i, b_ref[0, 1], _NT)
            dd = dd + _colsum(dys * ub)

            @pl.when(first)
            def _(du=du, dys=dys, rows=rows):
                du_ref[rows, :] = du + d_ref[...] * dys

            @pl.when(jnp.logical_not(first))
            def _(du=du, rows=rows):
                du_ref[rows, :] += du

        @pl.when(first)
        def _():
            dd_ref[...] = dd

        b_r, b_i = b_ref[0, 0], b_ref[0, 1]
        dbt_ref[0, 0] = coef_r * qt_r + coef_i * qt_i
        dbt_ref[0, 1] = coef_r * qt_i - coef_i * qt_r
        dct_ref[0, 0] = dct_r
        dct_ref[0, 1] = dct_i
        dpar_ref[0, 0:1, :] = da_r
        dpar_ref[0, 1:2, :] = da_i
        dpar_ref[0, 2:3, :] = _colsum(b_r * qt_r + b_i * qt_i)
        dpar_ref[0, 3:4, :] = _colsum(b_r * qt_i - b_i * qt_r)

    blk = lambda i, h: (0, i)
    params = _params(("parallel", "arbitrary")) if carry is None else pltpu.CompilerParams(
        vmem_limit_bytes=VMEM_LIMIT, dimension_semantics=("arbitrary", "arbitrary"), has_side_effects=True)
    return _pcall(
        body, name="ssm_bwd", grid=(n_cb, n_half),
        in_specs=[pl.BlockSpec((seq, SSM_CH), blk), pl.BlockSpec((seq, SSM_CH), blk), pl.BlockSpec((seq, SSM_CH), blk),
                  pl.BlockSpec((1, 2, SSM_CH, ns), lambda i, h: (i, 0, 0, h)),
                  pl.BlockSpec((1, 2, SSM_CH, ns), lambda i, h: (i, 0, 0, h)),
                  pl.BlockSpec((1, 4, ns), lambda i, h: (i, 0, h)),
                  pl.BlockSpec((1, SSM_CH), blk)] + ex_in_specs,
        out_specs=[pl.BlockSpec((seq, SSM_CH), blk),
                   pl.BlockSpec((1, 2, SSM_CH, ns), lambda i, h: (i, 0, 0, h)),
                   pl.BlockSpec((1, 2, SSM_CH, ns), lambda i, h: (i, 0, 0, h)),
                   pl.BlockSpec((1, 4, ns), lambda i, h: (i, 0, h)),
                   pl.BlockSpec((1, SSM_CH), blk)] + ex_out_specs,
        out_shape=[jax.ShapeDtypeStruct((seq, width), F32),
                   jax.ShapeDtypeStruct((n_cb, 2, SSM_CH, ns_all), F32),
                   jax.ShapeDtypeStruct((n_cb, 2, SSM_CH, ns_all), F32),
                   jax.ShapeDtypeStruct((n_cb, 4, ns_all), F32),
                   jax.ShapeDtypeStruct((1, width), F32)] + ex_out_shapes,
        scratch_shapes=[pltpu.VMEM((n_k, seq, LANES), F32) for _ in range(4)] + ex_scratch,
        compiler_params=params,
    )(uz, dy1, ys, b_blk, ct_blk, par, d_skip, *ex_args)


def _ssm_discretize(a_re, a_im, log_dt):
    dt = jnp.exp(log_dt)[:, None]
    mag = jnp.exp(a_re * dt)
    abar_re = mag * jnp.cos(a_im * dt)
    abar_im = mag * jnp.sin(a_im * dt)
    den = a_re * a_re + a_im * a_im
    nr = abar_re - 1.0
    coef_re = (nr * a_re + abar_im * a_im) / den
    coef_im = (abar_im * a_re - nr * a_im) / den
    return abar_re, abar_im, coef_re, coef_im


def _block_diag(w_gcp):
    gpb = SSM_CH // S5_GROUP
    n_cb = w_gcp.shape[0] // gpb
    w = w_gcp.reshape(n_cb, gpb, S5_GROUP, 1, S5_STATE)
    eye = jnp.eye(gpb, dtype=w.dtype)[None, :, None, :, None]
    return (w * eye).reshape(n_cb, SSM_CH, gpb * S5_STATE)


def _block_diag_extract(w_blk):
    gpb = SSM_CH // S5_GROUP
    n_cb = w_blk.shape[0]
    w = w_blk.reshape(n_cb, gpb, S5_GROUP, gpb, S5_STATE)
    w = jnp.moveaxis(jnp.diagonal(w, axis1=1, axis2=3), -1, 1)
    return w.reshape(n_cb * gpb, S5_GROUP, S5_STATE)


def _split3(x):
    hi = x.astype(BF16)
    mid = (x - hi.astype(F32)).astype(BF16)
    lo = (x - hi.astype(F32) - mid.astype(F32)).astype(BF16)
    return hi, mid, lo


def _tri_sum(tri, x):
    hi, mid, lo = _split3(x)
    return (jnp.dot(tri, hi, preferred_element_type=F32) + jnp.dot(tri, mid, preferred_element_type=F32)
            + jnp.dot(tri, lo, preferred_element_type=F32))


def _log_sigmoid(x):
    return jnp.minimum(x, 0.0) - jnp.log(1.0 + jnp.exp(-jnp.abs(x)))


def _cum_fwd(fl, b_f):
    seq = fl.shape[0]
    t = min(CUM_TILE, seq)

    def body(fl_ref, b_ref, o_ref, carry):
        @pl.when(pl.program_id(0) == 0)
        def _():
            carry[...] = jnp.zeros_like(carry)

        r = lax.broadcasted_iota(jnp.int32, (t, t), 0)
        c = lax.broadcasted_iota(jnp.int32, (t, t), 1)
        tri = (c <= r).astype(BF16)
        cum = _tri_sum(tri, _log_sigmoid(fl_ref[...] + b_ref[...])) + carry[...]
        o_ref[...] = cum
        carry[...] = cum[t - 1:t, :]

    return _pcall(
        body, name="cum_fwd", grid=(seq // t,),
        in_specs=[pl.BlockSpec((t, LANES), lambda i: (i, 0)), pl.BlockSpec((1, LANES), lambda i: (0, 0))],
        out_specs=pl.BlockSpec((t, LANES), lambda i: (i, 0)),
        out_shape=jax.ShapeDtypeStruct((seq, LANES), F32),
        scratch_shapes=[pltpu.VMEM((1, LANES), F32)],
        compiler_params=_params(("arbitrary",)),
    )(fl, b_f)


def _cum_bwd(dcum, fl, b_f):
    seq = fl.shape[0]
    t = min(CUM_TILE, seq)
    nb = seq // t

    def body(dc_ref, fl_ref, b_ref, o_ref, db_ref, carry):
        @pl.when(pl.program_id(0) == 0)
        def _():
            carry[...] = jnp.zeros_like(carry)
            db_ref[...] = jnp.zeros_like(db_ref)

        r = lax.broadcasted_iota(jnp.int32, (t, t), 0)
        c = lax.broadcasted_iota(jnp.int32, (t, t), 1)
        tri = (c >= r).astype(BF16)
        rev = _tri_sum(tri, dc_ref[...]) + carry[...]
        carry[...] = rev[0:1, :]
        dfl = rev * _sigmoid(-(fl_ref[...] + b_ref[...]))
        o_ref[...] = dfl
        db_ref[...] += _colsum(dfl)

    return _pcall(
        body, name="cum_bwd", grid=(nb,),
        in_specs=[pl.BlockSpec((t, LANES), lambda i: (nb - 1 - i, 0)), pl.BlockSpec((t, LANES), lambda i: (nb - 1 - i, 0)),
                  pl.BlockSpec((1, LANES), lambda i: (0, 0))],
        out_specs=[pl.BlockSpec((t, LANES), lambda i: (nb - 1 - i, 0)), pl.BlockSpec((1, LANES), lambda i: (0, 0))],
        out_shape=[jax.ShapeDtypeStruct((seq, LANES), F32), jax.ShapeDtypeStruct((1, LANES), F32)],
        scratch_shapes=[pltpu.VMEM((1, LANES), F32)],
        compiler_params=_params(("arbitrary",)),
    )(dcum, fl, b_f)


def _att_scores(q, kb, ck, row0, col0, masked):
    s = _dot(q, kb, _NT) - ck
    if masked:
        rows = row0 + lax.broadcasted_iota(jnp.int32, s.shape, 0)
        cols = col0 + lax.broadcasted_iota(jnp.int32, s.shape, 1)
        s = jnp.where(cols <= rows, s, NEG_INF)
    return s


def _pairwise_loop(lo, hi, step_fn, init):
    n = hi - lo

    def two(p, carry):
        return step_fn(lo + 2 * p + 1, step_fn(lo + 2 * p, carry))

    carry = lax.fori_loop(0, n // 2, two, init)
    return lax.cond(n % 2 == 1, lambda c: step_fn(hi - 1, c), lambda c: c, carry)


def _att_fwd(qz, kv, ck):
    seq = qz.shape[0]
    heads = ck.shape[0]
    t = min(ATT_TILE, seq)
    scale = HEAD_DIM ** -0.5

    def body(q_ref, k_ref, v_ref, ck_ref, o_ref, lse_ref):
        i = pl.program_id(1)
        q = (q_ref[...] * scale).astype(BF16)

        def block(j, carry, masked):
            m, l, acc = carry
            rows = pl.ds(pl.multiple_of(j * t, t), t)
            s = _att_scores(q, k_ref[rows, :], ck_ref[0, j], i * t, j * t, masked)
            m_new = jnp.maximum(m, jnp.max(s, axis=1, keepdims=True))
            p = jnp.exp(s - m_new)
            alpha = jnp.exp(m - m_new)
            return m_new, alpha * l + jnp.sum(p, axis=1, keepdims=True), alpha * acc + _dot(p, v_ref[rows, :])

        init = (jnp.full((t, 1), NEG_INF, F32), jnp.zeros((t, 1), F32), jnp.zeros((t, HEAD_DIM), F32))
        carry = _pairwise_loop(0, i, functools.partial(block, masked=False), init)
        m, l, acc = block(i, carry, True)
        o_ref[...] = acc / l
        lse_ref[0] = m + jnp.log(l)

    return _pcall(
        body, name="att_fwd", grid=(heads, seq // t),
        in_specs=[pl.BlockSpec((t, HEAD_DIM), lambda h, i: (i, h)),
                  pl.BlockSpec((seq, HEAD_DIM), lambda h, i: (0, h)),
                  pl.BlockSpec((seq, HEAD_DIM), lambda h, i: (0, heads + h)),
                  pl.BlockSpec((1, seq // t, 1, t), lambda h, i: (h, 0, 0, 0))],
        out_specs=[pl.BlockSpec((t, HEAD_DIM), lambda h, i: (i, h)), pl.BlockSpec((1, t, 1), lambda h, i: (h, i, 0))],
        out_shape=[jax.ShapeDtypeStruct((seq, heads * HEAD_DIM), F32), jax.ShapeDtypeStruct((heads, seq, 1), F32)],
        compiler_params=_params(("parallel", "parallel")),
    )(qz, kv, kv, ck)


def _att_bwd_q(qz, kv, do, o, lse, ck):
    seq = qz.shape[0]
    heads = ck.shape[0]
    t = min(ATT_TILE, seq)
    scale = HEAD_DIM ** -0.5

    def body(q_ref, k_ref, v_ref, do_ref, o_ref, lse_ref, ck_ref, dq_ref, delta_ref):
        i = pl.program_id(1)
        q = (q_ref[...] * scale).astype(BF16)
        dob = do_ref[...].astype(BF16)
        delta = jnp.sum(do_ref[...] * o_ref[...], axis=1, keepdims=True)
        lse_v = lse_ref[0]

        def block(j, carry, masked):
            dq, pdp = carry
            rows = pl.ds(pl.multiple_of(j * t, t), t)
            kb = k_ref[rows, :]
            s = _att_scores(q, kb, ck_ref[0, j], i * t, j * t, masked)
            p = jnp.exp(s - lse_v)
            dp = _dot(dob, v_ref[rows, :], _NT)
            ds = p * (dp - delta)
            return dq + _dot(ds, kb), pdp + jnp.sum(p * dp, axis=1, keepdims=True)

        init = (jnp.zeros((t, HEAD_DIM), F32), jnp.zeros((t, 1), F32))
        carry = _pairwise_loop(0, i, functools.partial(block, masked=False), init)
        dq, pdp = block(i, carry, True)
        dq_ref[...] = (dq * scale).astype(dq_ref.dtype)
        delta_ref[0] = pdp

    qblk = pl.BlockSpec((t, HEAD_DIM), lambda h, i: (i, h))
    col = pl.BlockSpec((1, t, 1), lambda h, i: (h, i, 0))
    return _pcall(
        body, name="att_bwd_q", grid=(heads, seq // t),
        in_specs=[qblk, pl.BlockSpec((seq, HEAD_DIM), lambda h, i: (0, h)),
                  pl.BlockSpec((seq, HEAD_DIM), lambda h, i: (0, heads + h)), qblk, qblk, col,
                  pl.BlockSpec((1, seq // t, 1, t), lambda h, i: (h, 0, 0, 0))],
        out_specs=[qblk, col],
        out_shape=[jax.ShapeDtypeStruct((seq, heads * HEAD_DIM), BF16), jax.ShapeDtypeStruct((heads, seq, 1), F32)],
        compiler_params=_params(("parallel", "parallel")),
    )(qz, kv, kv, do, o, lse, ck)


def _att_bwd_kv(qz, kv, do, lse, delta, ck):
    seq = qz.shape[0]
    heads = ck.shape[0]
    t = min(ATT_TILE, seq)
    nq = seq // t
    scale = HEAD_DIM ** -0.5

    def body(q_ref, k_ref, v_ref, do_ref, lse_ref, delta_ref, ck_ref, dk_ref, dv_ref, dck_ref):
        j = pl.program_id(1)
        kb, vb = k_ref[...], v_ref[...]
        ckv = ck_ref[0, 0]

        def block(i, carry, masked):
            dk, dv, dck = carry
            rows = pl.ds(pl.multiple_of(i * t, t), t)
            qb = (q_ref[rows, :] * scale).astype(BF16)
            dob = do_ref[rows, :].astype(BF16)
            s = _att_scores(qb, kb, ckv, i * t, j * t, masked)
            p = jnp.exp(s - lse_ref[0, rows, :])
            ds = p * (_dot(dob, vb, _NT) - delta_ref[0, rows, :])
            return dk + _dot(ds, qb, _TN), dv + _dot(p, dob, _TN), dck - _colsum(ds)

        init = (jnp.zeros((t, HEAD_DIM), F32), jnp.zeros((t, HEAD_DIM), F32), jnp.zeros((1, t), F32))
        carry = block(j, init, True)
        dk, dv, dck = _pairwise_loop(j + 1, nq, functools.partial(block, masked=False), carry)
        dk_ref[...] = dk.astype(dk_ref.dtype)
        dv_ref[...] = dv.astype(dv_ref.dtype)
        dck_ref[0, 0] = dck

    head = pl.BlockSpec((seq, HEAD_DIM), lambda h, j: (0, h))
    col = pl.BlockSpec((1, seq, 1), lambda h, j: (h, 0, 0))
    kblk = pl.BlockSpec((t, HEAD_DIM), lambda h, j: (j, h))
    row = pl.BlockSpec((1, 1, 1, t), lambda h, j: (h, j, 0, 0))
    return _pcall(
        body, name="att_bwd_kv", grid=(heads, nq),
        in_specs=[head, kblk, pl.BlockSpec((t, HEAD_DIM), lambda h, j: (j, heads + h)), head, col, col, row],
        out_specs=[kblk, kblk, row],
        out_shape=[jax.ShapeDtypeStruct((seq, heads * HEAD_DIM), BF16), jax.ShapeDtypeStruct((seq, heads * HEAD_DIM), BF16),
                   jax.ShapeDtypeStruct((heads, nq, 1, t), F32)],
        compiler_params=_params(("parallel", "parallel")),
    )(qz, kv, kv, do, lse, delta, ck)


def _mesh_pos():
    return lax.axis_index("x"), lax.axis_index("y"), lax.axis_index("c")


def _other_chips(x, y):
    return [(1 - x, y), (x, 1 - y), (1 - x, 1 - y)]


def _all_gather_weights(big, small):
    nb, ns = len(big), len(small)
    n_remote = 3 * (nb + ns)

    def plan(ins, outs, sems):
        send_sems, recv_sems, fwd_send, fwd_recv = sems
        x, y, c = _mesh_pos()
        chips = _other_chips(x, y)
        slots = [2 * cx + cy for cx, cy in chips]

        def half(ref, hc):
            rh = ref.shape[-2] // 2
            return ref.at[pl.ds(hc * rh, rh), :]

        def remote(i, j, src_chip, from_in):
            if i < nb:
                src = half(ins[i], c) if from_in else half(outs[i].at[src_chip], c)
                dst = half(outs[i].at[src_chip], c)
            else:
                src = ins[i] if from_in else outs[i].at[src_chip]
                dst = outs[i].at[src_chip]
            k = 3 * i + j
            return pltpu.make_async_remote_copy(src_ref=src, dst_ref=dst, send_sem=send_sems.at[k],
                                                recv_sem=recv_sems.at[k], device_id=(*chips[j], c),
                                                device_id_type=MESH_ID)

        def forward(i, j, hc):
            part = half(outs[i].at[slots[j]], hc)
            k = 3 * i + j
            return pltpu.make_async_remote_copy(src_ref=part, dst_ref=part, send_sem=fwd_send.at[k],
                                                recv_sem=fwd_recv.at[k], device_id=(x, y, 1 - c),
                                                device_id_type=MESH_ID)

        return remote, forward, 2 * x + y, slots, c

    def start(ins, outs, sems):
        remote, _, me, _, _ = plan(ins, outs, sems)
        for i in range(nb + ns):
            for j in range(3):
                remote(i, j, me, True).start()

    def finish(ins, outs, sems):
        remote, forward, me, slots, c = plan(ins, outs, sems)
        for i in range(nb + ns):
            for j in range(3):
                remote(i, j, slots[j], False).wait_recv()
                if i < nb:
                    forward(i, j, c).start()
        for i in range(nb):
            for j in range(3):
                forward(i, j, 1 - c).wait_recv()
        for i in range(nb + ns):
            for j in range(3):
                remote(i, j, me, True).wait_send()
                if i < nb:
                    forward(i, j, c).wait_send()

    arrays = tuple(big) + tuple(small)
    return _Exchange(
        arrays=arrays,
        out_shapes=tuple(jax.ShapeDtypeStruct((N_CHIPS,) + a.shape, a.dtype) for a in arrays),
        scratch=(pltpu.SemaphoreType.DMA((n_remote,)), pltpu.SemaphoreType.DMA((n_remote,)),
                 pltpu.SemaphoreType.DMA((3 * max(nb, 1),)), pltpu.SemaphoreType.DMA((3 * max(nb, 1),))),
        start=start, finish=finish)


def _swap_halves(grads):
    n = len(grads)

    def copies(ins, outs, sems):
        x, y, c = _mesh_pos()
        cps = []
        for i in range(n):
            rh = ins[i].shape[1] // 2
            cps.append(pltpu.make_async_remote_copy(
                src_ref=ins[i].at[:, pl.ds((1 - c) * rh, rh), :], dst_ref=outs[i], send_sem=sems[0].at[i],
                recv_sem=sems[1].at[i], device_id=(x, y, 1 - c), device_id_type=MESH_ID))
        return cps

    def start(ins, outs, sems):
        for cp in copies(ins, outs, sems):
            cp.start()

    def finish(ins, outs, sems):
        for cp in copies(ins, outs, sems):
            cp.wait()

    return _Exchange(
        arrays=tuple(grads),
        out_shapes=tuple(jax.ShapeDtypeStruct((g.shape[0], g.shape[1] // 2, g.shape[2]), g.dtype) for g in grads),
        scratch=(pltpu.SemaphoreType.DMA((n,)), pltpu.SemaphoreType.DMA((n,))),
        start=start, finish=finish)


def _pair_sum_bf16(g, theirs, pos, name):
    n, rh, cdim = theirs.shape
    tr = min(ROW_TILE, rh)
    nb = rh // tr

    def body(pos_ref, g_ref, t_ref, o_ref):
        o_ref[...] = (g_ref[...] + t_ref[...]).astype(BF16)

    grid_spec = pltpu.PrefetchScalarGridSpec(
        num_scalar_prefetch=1, grid=(n, nb),
        in_specs=[pl.BlockSpec((None, tr, cdim), lambda s, i, pos: (s, pos[1] * nb + i, 0)),
                  pl.BlockSpec((None, tr, cdim), lambda s, i, pos: (s, i, 0))],
        out_specs=pl.BlockSpec((None, tr, cdim), lambda s, i, pos: (s, i, 0)))
    return _pcall(body, name=name, grid_spec=grid_spec, out_shape=jax.ShapeDtypeStruct(theirs.shape, BF16),
                  compiler_params=_params(("parallel", "parallel")))(pos, g, theirs)


def _chip_sum(g, theirs, recv, pos, name):
    n, rh, cdim = theirs.shape
    tr = min(ROW_TILE, rh)
    nb = rh // tr

    def body(pos_ref, g_ref, t_ref, r0, r1, r2, o_ref):
        o_ref[...] = (((g_ref[...] + t_ref[...]) + r0[...]) + r1[...]) + r2[...]

    grid_spec = pltpu.PrefetchScalarGridSpec(
        num_scalar_prefetch=1, grid=(nb,),
        in_specs=[pl.BlockSpec((None, tr, cdim), lambda i, pos: (pos[0], pos[1] * nb + i, 0)),
                  pl.BlockSpec((None, tr, cdim), lambda i, pos: (pos[0], i, 0))]
        + [pl.BlockSpec((None, tr, cdim), functools.partial(lambda i, pos, j: (j, i, 0), j=j)) for j in range(3)],
        out_specs=pl.BlockSpec((tr, cdim), lambda i, pos: (i, 0)))
    return _pcall(body, name=name, grid_spec=grid_spec, out_shape=jax.ShapeDtypeStruct((rh, cdim), F32),
                  compiler_params=_params(("parallel",)))(pos, g, theirs, recv, recv, recv)


def _scatter_to_owner(parts):
    n = len(parts)

    def copies(ins, outs, sems):
        x, y, c = _mesh_pos()
        chips = _other_chips(x, y)
        cps = []
        for i in range(n):
            for j in range(3):
                k = 3 * i + j
                cps.append(pltpu.make_async_remote_copy(
                    src_ref=ins[i].at[2 * chips[j][0] + chips[j][1]], dst_ref=outs[i].at[j],
                    send_sem=sems[0].at[k], recv_sem=sems[1].at[k], device_id=(*chips[j], c),
                    device_id_type=MESH_ID))
        return cps

    def start(ins, outs, sems):
        for cp in copies(ins, outs, sems):
            cp.start()

    def finish(ins, outs, sems):
        for cp in copies(ins, outs, sems):
            cp.wait()

    return _Exchange(
        arrays=tuple(parts),
        out_shapes=tuple(jax.ShapeDtypeStruct((3,) + p.shape[1:], p.dtype) for p in parts),
        scratch=(pltpu.SemaphoreType.DMA((3 * n,)), pltpu.SemaphoreType.DMA((3 * n,))),
        start=start, finish=finish)


def _join_halves(halves):
    n = len(halves)

    def body(*refs):
        ins, outs, send_sems, recv_sems = refs[:n], refs[n:2 * n], refs[2 * n], refs[2 * n + 1]
        x, y, c = _mesh_pos()
        cps = []
        for i in range(n):
            rh = ins[i].shape[0]
            cps.append(pltpu.make_async_remote_copy(
                src_ref=ins[i], dst_ref=outs[i].at[pl.ds(c * rh, rh), :], send_sem=send_sems.at[i],
                recv_sem=recv_sems.at[i], device_id=(x, y, 1 - c), device_id_type=MESH_ID))
        for cp in cps:
            cp.start()
        for cp in cps:
            cp.wait_send()
        for i in range(n):
            rh = ins[i].shape[0]
            theirs = outs[i].at[pl.ds((1 - c) * rh, rh), :]
            pltpu.make_async_remote_copy(src_ref=ins[i], dst_ref=theirs, send_sem=send_sems.at[i],
                                         recv_sem=recv_sems.at[i], device_id=(x, y, 1 - c),
                                         device_id_type=MESH_ID).wait_recv()

    return _pcall(
        body, name="grad_join_halves", in_specs=[ANY] * n, out_specs=[ANY] * n,
        out_shape=[jax.ShapeDtypeStruct((2 * h.shape[0], h.shape[1]), h.dtype) for h in halves],
        scratch_shapes=[pltpu.SemaphoreType.DMA((n,)), pltpu.SemaphoreType.DMA((n,))],
        compiler_params=pltpu.CompilerParams(has_side_effects=True),
    )(*halves)


def _all_reduce_small(v):
    def body(v_ref, o_ref, recv, send_sems, recv_sems):
        x, y, c = _mesh_pos()
        peers = [(x, y, 1 - c), (1 - x, y, c), (x, 1 - y, c)]
        for s, peer in enumerate(peers):
            src = v_ref if s == 0 else o_ref
            cp = pltpu.make_async_remote_copy(src_ref=src, dst_ref=recv.at[s], send_sem=send_sems.at[s],
                                              recv_sem=recv_sems.at[s], device_id=peer, device_id_type=MESH_ID)
            cp.start()
            cp.wait()
            o_ref[...] = src[...] + recv[s]

    vm = pl.BlockSpec(memory_space=pltpu.VMEM)
    return _pcall(
        body, name="all_reduce_small", in_specs=[vm], out_specs=vm,
        out_shape=jax.ShapeDtypeStruct(v.shape, v.dtype),
        scratch_shapes=[pltpu.VMEM((3,) + v.shape, v.dtype), pltpu.SemaphoreType.DMA((3,)), pltpu.SemaphoreType.DMA((3,))],
        compiler_params=pltpu.CompilerParams(vmem_limit_bytes=VMEM_LIMIT, has_side_effects=True),
    )(v)


def _adamw_math(w, g, m, v):
    m = ADAM_B1 * m + (1.0 - ADAM_B1) * g
    v = ADAM_B2 * v + (1.0 - ADAM_B2) * (g * g)
    m_hat = m / (1.0 - ADAM_B1 ** ADAM_STEP)
    v_hat = v / (1.0 - ADAM_B2 ** ADAM_STEP)
    delta = -ADAM_LR * (m_hat / (jnp.sqrt(v_hat) + ADAM_EPS) + ADAM_WD * w)
    return delta, m, v


def _adamw(w, g, m, v, name):
    wd = w.shape[1]
    return _rows(_adamw_math, [_full(w), _full(g), _full(m), _full(v)], [(wd, F32)] * 3, name=name)


def _pack(arrs, rows):
    flat = jnp.concatenate([a.reshape(-1) for a in arrs])
    return jnp.pad(flat, (0, rows * LANES - flat.shape[0])).reshape(rows, LANES)


def _unpack(buf, like):
    flat = buf.reshape(-1)
    out, off = [], 0
    for a in like:
        out.append(flat[off:off + a.size].reshape(a.shape))
        off += a.size
    return out


def kernel(x, norm_pre, norm_post, s5_w_in, s5_a_re, s5_a_im, s5_log_dt, s5_b_re, s5_b_im, s5_c_re, s5_c_im, s5_d, s5_w_glu, s5_b_glu, s5_w_out, kv_norm, kv_w, kv_b_f, fox_w_in, fox_w_out, loss_target, m_norm_pre, m_norm_post, m_s5_w_in, m_s5_a_re, m_s5_a_im, m_s5_log_dt, m_s5_b_re, m_s5_b_im, m_s5_c_re, m_s5_c_im, m_s5_d, m_s5_w_glu, m_s5_b_glu, m_s5_w_out, m_kv_norm, m_kv_w, m_kv_b_f, m_fox_w_in, m_fox_w_out, v_norm_pre, v_norm_post, v_s5_w_in, v_s5_a_re, v_s5_a_im, v_s5_log_dt, v_s5_b_re, v_s5_b_im, v_s5_c_re, v_s5_c_im, v_s5_d, v_s5_w_glu, v_s5_b_glu, v_s5_w_out, v_kv_norm, v_kv_w, v_kv_b_f, v_fox_w_in, v_fox_w_out):
    seq, dm = x.shape[1], x.shape[2]
    width = dm
    heads = dm // HEAD_DIM
    fw = heads * HEAD_DIM
    groups = width // S5_GROUP
    chip = 2 * lax.axis_index("x") + lax.axis_index("y")

    big_shards = [s5_w_in[0], s5_w_glu[0], s5_w_out[0], kv_w, fox_w_in[0], fox_w_out[0]]
    own_shards = [w.astype(BF16) for w in big_shards] + [s5_d, s5_b_glu]
    fill_own = lambda gs, owns: [lax.dynamic_update_slice(g, own[None], (chip, 0, 0)) for g, own in zip(gs, owns)]
    c_idx = lax.axis_index("c")
    pos = jnp.stack([chip, c_idx]).astype(jnp.int32)
    first_owns = [own_shards[0], s5_d, s5_b_glu]
    g_win, g_d, g_bglu = fill_own(
        _exchange_call(_all_gather_weights(first_owns[:1], first_owns[1:]), "all_gather_first"), first_owns)
    gather_rest = _all_gather_weights(own_shards[1:5], [])
    gather_last = _all_gather_weights(own_shards[5:6], [])
    cols = lambda g: jnp.moveaxis(g, 0, 1).reshape(g.shape[1], -1)
    rows = lambda g: g.reshape(-1, g.shape[2])
    w_in = g_win
    d_skip, b_glu = cols(g_d), cols(g_bglu)
    b_f = jnp.pad(kv_b_f, (0, LANES - heads)).reshape(1, LANES)

    h0 = x[0]
    target = loss_target[0]
    g_pre0, g_pre1 = norm_pre[0:1], norm_pre[1:2]
    g_post0, g_post1 = norm_post[0:1], norm_post[1:2]
    g_kv = kv_norm.reshape(1, dm)

    a_re, a_im, log_dt = s5_a_re[0], s5_a_im[0], s5_log_dt[0]
    disc, disc_vjp = jax.vjp(_ssm_discretize, a_re, a_im, log_dt)
    gpb = SSM_CH // S5_GROUP
    n_cb = groups // gpb
    par = jnp.stack([p.reshape(n_cb, gpb * S5_STATE) for p in disc], axis=1)
    b_t = lambda b: jnp.swapaxes(b, 1, 2)
    b_blk = jnp.stack([_block_diag(b_t(s5_b_re[0])), _block_diag(b_t(s5_b_im[0]))], axis=1)
    ct_blk = jnp.stack([_block_diag(s5_c_re[0]), _block_diag(s5_c_im[0])], axis=1)
    c_blk = jnp.swapaxes(ct_blk, 2, 3)

    xn1 = _to_slab(_rows(lambda h, g: (h * _rstd(h) * g,), [_full(h0)], [(dm, BF16)], consts=[g_pre0], name="norm_pre0")[0])
    uz = _mm(xn1, w_in, name="s5_in")
    ys, y1b, *rest = _ssm_fwd(uz, b_blk, c_blk, par, d_skip, carry=gather_rest)
    g_wglu, g_wout, g_kvw, g_fwin = fill_own(rest, own_shards[1:5])
    w_glu, w_out = rows(g_wglu), rows(g_wout)
    kvw_full = cols(g_kvw)
    w_kv = kvw_full[:, :2 * fw]
    w_f = jnp.pad(kvw_full[:, 2 * fw:], ((0, 0), (0, LANES - heads)))
    fw_in = g_fwin
    glu_a = _mm(y1b, w_glu, name="s5_glu")

    def gate_fn(y, a, z, b):
        return (_gelu(y) * _sigmoid(a + b) * _silu(z),)

    y3b = _rows(gate_fn, [_full(ys), _full(glu_a), (uz, width, 1)], [(width, BF16)], consts=[b_glu], name="s5_gate")[0]
    o1 = _from_slab(_mm(y3b, w_out, name="s5_out"))

    def mid_fn(h, o, gp, gk, gq):
        h1 = h + o * _rstd(o) * gp
        r = _rstd(h1)
        return h1, h1 * r * gk, h1 * r * gq

    h1, xk, xn2 = _rows(mid_fn, [_full(h0), _full(o1)], [(dm, F32), (dm, BF16), (dm, BF16)],
                        consts=[g_post0, g_kv, g_pre1], name="mid_norms")

    kv, g_fwout = _mm(xk, w_kv, out_dtype=BF16, name="kv_proj", carry=gather_last)
    fw_out = rows(fill_own([g_fwout], own_shards[5:6])[0])
    fl = _mm(xk, w_f, name="f_proj")
    qz = _mm(xn2, fw_in, name="fox_in")
    cum = _cum_fwd(fl, b_f)
    t_att = min(ATT_TILE, seq)
    cum_t = cum[:, :heads].T
    ck = cum_t.reshape(heads, seq // t_att, 1, t_att)
    o, lse = _att_fwd(qz, kv, ck)
    o2b = _rows(lambda a, z: (a * _silu(z),), [_full(o), (qz, fw, 1)], [(fw, BF16)], name="fox_gate")[0]
    o3 = _mm(o2b, fw_out, name="fox_out")

    def loss_fn(h, o, t, g):
        r = _rstd(o)
        err = h + o * r * g - t
        dh = err * (1.0 / dm)
        do, dg = _rms_bwd(o, g, dh)
        part = 0.5 * jnp.sum(jnp.mean(err * err, axis=-1, keepdims=True), axis=0, keepdims=True)
        return dh, do, jnp.broadcast_to(part, (1, LANES)), _colsum(dg)

    dh2, do3, loss_part, dg_post1 = _rows(loss_fn, [_full(h1), _full(o3), _full(target)], [(dm, F32), (dm, BF16)],
                                          consts=[g_post1], accs=[(1, LANES), (1, dm)], name="loss_head")
    loss = lax.psum(loss_part[0, 0], MESH_AXES)

    do2 = _mm(do3, fw_out, tb=True, name="fox_out_dx")
    dw_fout = _mm(o2b, do3, ta=True, name="fox_out_dw")

    def fox_gate_bwd(d, a, z):
        return d * _silu(z), d * a * _dsilu(z)

    do, dz2 = _rows(fox_gate_bwd, [_full(do2), _full(o), (qz, fw, 1)], [(fw, F32), (fw, BF16)], name="fox_gate_bwd")
    dq, delta = _att_bwd_q(qz, kv, do, o, lse, ck)
    dk, dv, dck = _att_bwd_kv(qz, kv, do, lse, delta, ck)
    dcum = jnp.pad(dck.reshape(heads, seq).T, ((0, 0), (0, LANES - heads)))
    dfl, db_f = _cum_bwd(dcum, fl, b_f)
    dqz = (dq, dz2)
    dkv = (dk, dv)
    dxn2 = _mm(dqz, fw_in, tb=True, name="fox_in_dx")
    dw_fin = _mm(xn2, dqz, ta=True, out_split=N_CHIPS, name="fox_in_dw")
    dxk_f = _mm(dfl, w_f, tb=True, name="f_proj_dx")
    dxk = _mm(dkv, w_kv, tb=True, add=dxk_f, name="kv_proj_dx")
    dw_kv = _mm(xk, dkv, ta=True, name="kv_proj_dw")
    dw_f = _mm(xk, dfl, ta=True, name="f_proj_dw")

    def mid_bwd(d2, h, dq_, dk_, o, gq, gk, gp):
        dxa, dga = _rms_bwd(h, gq, dq_)
        dxb, dgb = _rms_bwd(h, gk, dk_)
        dh = d2 + dxa + dxb
        do_, dgp = _rms_bwd(o, gp, dh)
        return dh, do_, _colsum(dga), _colsum(dgb), _colsum(dgp)

    to_cols = lambda g: jnp.moveaxis(g.reshape(g.shape[0], N_CHIPS, -1), 1, 0)
    to_rows = lambda g: g.reshape(N_CHIPS, -1, g.shape[1])
    dw_kv_full = jnp.concatenate([dw_kv, dw_f[:, :heads]], axis=1)
    early_grads = [to_cols(dw_kv_full), dw_fin, to_rows(dw_fout)]
    dh1, do1, dg_pre1, dg_kv, dg_post0, *early_theirs = _rows(
        mid_bwd, [_full(dh2), _full(h1), _full(dxn2), _full(dxk), _full(o1)], [(dm, F32), (dm, BF16)],
        consts=[g_pre1, g_kv, g_post0], accs=[(1, dm)] * 3, name="mid_norms_bwd", carry=_swap_halves(early_grads))
    early_sums = [_pair_sum_bf16(g, t, pos, f"grad_pair_sum_{3 + i}") for i, (g, t) in enumerate(zip(early_grads, early_theirs))]

    do1 = _to_slab(do1)
    dy3 = _mm(do1, w_out, tb=True, name="s5_out_dx")
    dw_out = _mm(y3b, do1, ta=True, name="s5_out_dw")

    def gate_bwd(d3, y, a, z, b):
        y1 = _gelu(y)
        gate = _sigmoid(a + b)
        dy2 = d3 * _silu(z)
        da = dy2 * y1 * gate * (1.0 - gate)
        return dy2 * gate, da, d3 * (y1 * gate) * _dsilu(z), _colsum(da)

    dy1_direct, da, dz, db_glu = _rows(gate_bwd, [_full(dy3), _full(ys), _full(glu_a), (uz, width, 1)],
                                       [(width, F32), (width, BF16), (width, BF16)], consts=[b_glu],
                                       accs=[(1, width)], name="s5_gate_bwd")
    dy1 = _mm(da, w_glu, tb=True, add=dy1_direct, name="s5_glu_dx")
    dw_glu = _mm(y1b, da, ta=True, name="s5_glu_dw")
    mid_grads = [to_rows(dw_glu), to_rows(dw_out)]
    du, dbt_blk, dct_blk, dpar, dd, *carried = _ssm_bwd(
        uz, dy1, ys, b_blk, ct_blk, par, d_skip, carry=_together(_scatter_to_owner(early_sums), _swap_halves(mid_grads)))
    early_recv, mid_theirs = carried[:3], carried[3:]
    mid_sums = [_pair_sum_bf16(g, t, pos, f"grad_pair_sum_{1 + i}") for i, (g, t) in enumerate(zip(mid_grads, mid_theirs))]
    duz = (du, dz)
    dw_in, *mid_recv = _mm(xn1, duz, ta=True, out_split=N_CHIPS, name="s5_in_dw", carry=_scatter_to_owner(mid_sums))
    late_grads = [dw_in]
    dxn1, *late_theirs = _mm(duz, w_in, tb=True, name="s5_in_dx", carry=_swap_halves(late_grads))
    dxn1 = _from_slab(dxn1)
    late_sums = [_pair_sum_bf16(dw_in, late_theirs[0], pos, "grad_pair_sum_0")]

    def first_bwd(d1, h, dxn, g):
        dx, dg = _rms_bwd(h, g, dxn)
        return d1 + dx, _colsum(dg)

    grad_x, dg_pre0, *late_recv = _rows(first_bwd, [_full(dh1), _full(h0), _full(dxn1)], [(dm, F32)], consts=[g_pre0],
                                        accs=[(1, dm)], name="norm_pre0_bwd", carry=_scatter_to_owner(late_sums))

    dpar_g = [dpar[:, i, :].reshape(groups, S5_STATE) for i in range(4)]
    da_re, da_im, dlog_dt = disc_vjp(tuple(dpar_g))
    db_re = jnp.swapaxes(_block_diag_extract(dbt_blk[:, 0]), 1, 2)
    db_im = jnp.swapaxes(_block_diag_extract(dbt_blk[:, 1]), 1, 2)
    dc_re = _block_diag_extract(dct_blk[:, 0])
    dc_im = _block_diag_extract(dct_blk[:, 1])

    small_local = [jnp.concatenate([dg_pre0, dg_pre1]), jnp.concatenate([dg_post0, dg_post1]),
                   da_re[None], da_im[None], dlog_dt[None], db_re[None], db_im[None], dc_re[None], dc_im[None],
                   dd, db_glu, dg_kv.reshape(dm), db_f[0, :heads]]
    n_small = sum(a.size for a in small_local)
    small_rows = -(-n_small // (LANES * ROW_TILE)) * ROW_TILE
    small_sum = _unpack(_all_reduce_small(_pack(small_local, small_rows)), small_local)
    (g_norm_pre, g_norm_post, g_a_re, g_a_im, g_log_dt, g_b_re, g_b_im, g_c_re, g_c_im, g_d_full, g_bglu_full,
     g_kv_norm, g_b_f) = small_sum
    shard = width // N_CHIPS
    g_d_own = lax.dynamic_slice(g_d_full, (0, chip * shard), (1, shard))
    g_bglu_own = lax.dynamic_slice(g_bglu_full, (0, chip * shard), (1, shard))

    big_grads = late_grads + mid_grads + early_grads
    theirs = list(late_theirs) + list(mid_theirs) + list(early_theirs)
    received = list(late_recv) + list(mid_recv) + list(early_recv)
    halves = [_chip_sum(g, t, r, pos, f"grad_chip_sum_{i}") for i, (g, t, r) in enumerate(zip(big_grads, theirs, received))]
    joined = _join_halves(halves)
    g_win_s, g_wglu_s, g_wout_s, g_kvw_s, g_fwin_s, g_fwout_s = [
        lax.dynamic_update_slice(j, h, (c_idx * h.shape[0], 0)) for j, h in zip(joined, halves)]

    big_w = big_shards
    big_g = [g_win_s, g_wglu_s, g_wout_s, g_kvw_s, g_fwin_s, g_fwout_s]
    big_m = [m_s5_w_in[0], m_s5_w_glu[0], m_s5_w_out[0], m_kv_w, m_fox_w_in[0], m_fox_w_out[0]]
    big_v = [v_s5_w_in[0], v_s5_w_glu[0], v_s5_w_out[0], v_kv_w, v_fox_w_in[0], v_fox_w_out[0]]
    big_upd = [_adamw(w, g, m, v, f"adamw_{i}") for i, (w, g, m, v) in enumerate(zip(big_w, big_g, big_m, big_v))]

    small_names = ["norm_pre", "norm_post", "s5_a_re", "s5_a_im", "s5_log_dt", "s5_b_re", "s5_b_im", "s5_c_re", "s5_c_im",
                   "s5_d", "s5_b_glu", "kv_norm", "kv_b_f"]
    small_w = [norm_pre, norm_post, s5_a_re, s5_a_im, s5_log_dt, s5_b_re, s5_b_im, s5_c_re, s5_c_im, s5_d, s5_b_glu, kv_norm, kv_b_f]
    small_m = [m_norm_pre, m_norm_post, m_s5_a_re, m_s5_a_im, m_s5_log_dt, m_s5_b_re, m_s5_b_im, m_s5_c_re, m_s5_c_im, m_s5_d, m_s5_b_glu, m_kv_norm, m_kv_b_f]
    small_v = [v_norm_pre, v_norm_post, v_s5_a_re, v_s5_a_im, v_s5_log_dt, v_s5_b_re, v_s5_b_im, v_s5_c_re, v_s5_c_im, v_s5_d, v_s5_b_glu, v_kv_norm, v_kv_b_f]
    small_g = [g_norm_pre, g_norm_post, g_a_re, g_a_im, g_log_dt, g_b_re, g_b_im, g_c_re, g_c_im, g_d_own, g_bglu_own, g_kv_norm, g_b_f]
    small_g = [g.reshape(w.shape) for g, w in zip(small_g, small_w)]
    n_own = sum(a.size for a in small_w)
    own_rows = -(-n_own // (LANES * ROW_TILE)) * ROW_TILE
    pv = _pack(small_v, own_rows)
    pv = jnp.where(jnp.arange(own_rows * LANES).reshape(own_rows, LANES) < n_own, pv, 1.0)
    sd, sm, sv = _adamw(_pack(small_w, own_rows), _pack(small_g, own_rows), _pack(small_m, own_rows), pv, "adamw_small")
    small_delta, small_newm, small_newv = _unpack(sd, small_w), _unpack(sm, small_w), _unpack(sv, small_w)

    order = ["norm_pre", "norm_post", "s5_w_in", "s5_a_re", "s5_a_im", "s5_log_dt", "s5_b_re", "s5_b_im", "s5_c_re", "s5_c_im",
             "s5_d", "s5_w_glu", "s5_b_glu", "s5_w_out", "kv_norm", "kv_w", "kv_b_f", "fox_w_in", "fox_w_out"]
    big_names = ["s5_w_in", "s5_w_glu", "s5_w_out", "kv_w", "fox_w_in", "fox_w_out"]
    big_like = [s5_w_in, s5_w_glu, s5_w_out, kv_w, fox_w_in, fox_w_out]
    grads, deltas, new_m, new_v = {}, {}, {}, {}
    for i, n in enumerate(big_names):
        shp = big_like[i].shape
        grads[n] = big_g[i].reshape(shp)
        deltas[n], new_m[n], new_v[n] = (a.reshape(shp) for a in big_upd[i])
    for i, n in enumerate(small_names):
        grads[n], deltas[n], new_m[n], new_v[n] = small_g[i], small_delta[i], small_newm[i], small_newv[i]

    return (loss, grad_x[None], *[grads[n] for n in order], *[deltas[n] for n in order],
            *[new_m[n] for n in order], *[new_v[n] for n in order])
```

```python
import functools
import math
from typing import Callable, NamedTuple

import jax
import jax.numpy as jnp
from jax import lax
from jax.experimental import pallas as pl
from jax.experimental.pallas import tpu as pltpu

F32 = jnp.float32
BF16 = jnp.bfloat16

D_MODEL = 2048
SEQ = 4096
S5_GROUP = 16
S5_STATE = 64
HEAD_DIM = 128
RMS_EPS = 1e-6
NEG_INF = -1e30
ADAM_LR = 0.001
ADAM_B1 = 0.9
ADAM_B2 = 0.999
ADAM_EPS = 1e-08
ADAM_WD = 0.01
ADAM_STEP = 10

LANES = 128
SUBLANES = 8
VMEM_LIMIT = 56 * 1024 * 1024
N_CHIPS = 4
MESH_AXES = ("x", "y", "c")
MESH_ID = pl.DeviceIdType.MESH

SSM_CH = 128
ROW_TILE = 256
SCAN_ROWS = 512
SCAN_UNROLL = 4
ATT_TILE = 512
ATT_UNROLL = 2
CUM_TILE = 512


def _pcall(body, **kw):
    return pl.pallas_call(body, **kw)


def _params(sem=None):
    if sem is None:
        return pltpu.CompilerParams(vmem_limit_bytes=VMEM_LIMIT)
    return pltpu.CompilerParams(vmem_limit_bytes=VMEM_LIMIT, dimension_semantics=sem)


def _sigmoid(x):
    return 1.0 / (1.0 + jnp.exp(-x))


def _silu(z):
    return z * _sigmoid(z)


def _dsilu(z):
    s = _sigmoid(z)
    return s * (1.0 + z * (1.0 - s))


_GELU_C = math.sqrt(2.0 / math.pi)


def _gelu(x):
    return 0.5 * x * (1.0 + jnp.tanh(_GELU_C * (x + 0.044715 * x * x * x)))


def _dgelu(x):
    t = jnp.tanh(_GELU_C * (x + 0.044715 * x * x * x))
    return 0.5 * (1.0 + t) + 0.5 * x * (1.0 - t * t) * _GELU_C * (1.0 + 3.0 * 0.044715 * x * x)


def _rstd(x):
    return lax.rsqrt(jnp.mean(x * x, axis=-1, keepdims=True) + RMS_EPS)


def _rms_bwd(x, g, dy):
    r = _rstd(x)
    dyg = dy * g
    dx = r * dyg - x * (r * r * r) * jnp.mean(dyg * x, axis=-1, keepdims=True)
    return dx, dy * (x * r)


def _colsum(v):
    return jnp.sum(v, axis=0, keepdims=True)


ANY = pl.BlockSpec(memory_space=pl.ANY)


class _Exchange(NamedTuple):
    arrays: tuple
    out_shapes: tuple
    scratch: tuple
    start: Callable
    finish: Callable


def _together(*exs):
    def parts(seq, field):
        out, off = [], 0
        for e in exs:
            n = len(getattr(e, field))
            out.append(seq[off:off + n])
            off += n
        return out

    def start(ins, outs, sems):
        for e, i, o, s in zip(exs, parts(ins, "arrays"), parts(outs, "out_shapes"), parts(sems, "scratch")):
            e.start(i, o, s)

    def finish(ins, outs, sems):
        for e, i, o, s in zip(exs, parts(ins, "arrays"), parts(outs, "out_shapes"), parts(sems, "scratch")):
            e.finish(i, o, s)

    return _Exchange(arrays=sum((tuple(e.arrays) for e in exs), ()), out_shapes=sum((tuple(e.out_shapes) for e in exs), ()),
                     scratch=sum((tuple(e.scratch) for e in exs), ()), start=start, finish=finish)


def _exchange_call(ex, name):
    n_in, n_out = len(ex.arrays), len(ex.out_shapes)

    def body(*refs):
        ins, outs, sems = refs[:n_in], refs[n_in:n_in + n_out], refs[n_in + n_out:]
        ex.start(ins, outs, sems)
        ex.finish(ins, outs, sems)

    return _pcall(body, name=name, in_specs=[ANY] * n_in, out_specs=[ANY] * n_out, out_shape=list(ex.out_shapes),
                  scratch_shapes=list(ex.scratch), compiler_params=pltpu.CompilerParams(has_side_effects=True))(*ex.arrays)


def _carry(ex, refs, n_fixed_in, n_fixed_out):
    if ex is None:
        return refs, lambda cond: None, lambda cond: None
    n_in, n_out, n_sem = len(ex.arrays), len(ex.out_shapes), len(ex.scratch)
    fixed_in = refs[:n_fixed_in]
    ex_in = refs[n_fixed_in:n_fixed_in + n_in]
    rest = refs[n_fixed_in + n_in:]
    fixed_out = rest[:n_fixed_out]
    ex_out = rest[n_fixed_out:n_fixed_out + n_out]
    scratch = rest[n_fixed_out + n_out:]
    sems = scratch[len(scratch) - n_sem:]

    def start_when(cond):
        pl.when(cond)(lambda: ex.start(ex_in, ex_out, sems))

    def finish_when(cond):
        pl.when(cond)(lambda: ex.finish(ex_in, ex_out, sems))

    return tuple(fixed_in) + tuple(fixed_out) + tuple(scratch[:len(scratch) - n_sem]), start_when, finish_when


def _carry_specs(ex):
    if ex is None:
        return (), [], [], [], []
    return ex.arrays, [ANY] * len(ex.arrays), [ANY] * len(ex.out_shapes), list(ex.out_shapes), list(ex.scratch)


def _mm(a, b, *, ta=False, tb=False, out_dtype=F32, add=None, out_split=1, tm=1024, tn=1024, tk=2048, name, carry=None):
    def describe(op):
        if isinstance(op, (tuple, list)):
            assert all(p.ndim == 2 and p.shape == op[0].shape for p in op)
            return list(op), op[0].shape[0], op[0].shape[1], False
        if op.ndim == 3:
            return [op], op.shape[1], op.shape[2], True
        return [op], op.shape[0], op.shape[1], False

    a_parts, a_rows, a_pc, a_stack = describe(a)
    b_parts, b_rows, b_pc, b_stack = describe(b)
    a_cols = a_pc * (a.shape[0] if a_stack else len(a_parts))
    b_cols = b_pc * (b.shape[0] if b_stack else len(b_parts))
    k_dim, m_dim = (a_rows, a_cols) if ta else (a_cols, a_rows)
    n_dim, kb = (b_rows, b_cols) if tb else (b_cols, b_rows)
    assert kb == k_dim, (k_dim, kb)
    tm = min(tm, a_pc) if ta else min(tm, m_dim)
    tk = min(tk, k_dim, k_dim if ta else a_pc, b_pc if tb else k_dim)
    tn = min(tn, n_dim // out_split, n_dim if tb else b_pc)
    a_ct, b_ct = (tm if ta else tk), (tk if tb else tn)
    assert m_dim % tm == 0 and n_dim % tn == 0 and k_dim % tk == 0 and a_pc % a_ct == 0 and b_pc % b_ct == 0
    assert (n_dim // out_split) % tn == 0
    nk = k_dim // tk
    dims = (((0 if ta else 1,), (1 if tb else 0,)), ((), ()))
    n_a, n_b = len(a_parts), len(b_parts)
    assert n_a == 1 or n_b == 1

    def operand_specs(parts, stack, rows_t, cols_t, per, row_of, col_of):
        specs = []
        for p in range(len(parts)):
            def col(i, j, k, p=p):
                return jnp.clip(col_of(i, j, k) - p * per, 0, per - 1) if len(parts) > 1 else col_of(i, j, k)
            if stack:
                specs.append(pl.BlockSpec((None, rows_t, cols_t),
                                          lambda i, j, k, col=col: (col(i, j, k) // per, row_of(i, j, k), col(i, j, k) % per)))
            else:
                specs.append(pl.BlockSpec((rows_t, cols_t), lambda i, j, k, col=col: (row_of(i, j, k), col(i, j, k))))
        return specs

    if ta:
        a_specs = operand_specs(a_parts, a_stack, tk, tm, a_pc // tm, lambda i, j, k: k, lambda i, j, k: i)
    else:
        a_specs = operand_specs(a_parts, a_stack, tm, tk, a_pc // tk, lambda i, j, k: i, lambda i, j, k: k)
    if tb:
        b_specs = operand_specs(b_parts, b_stack, tn, tk, b_pc // tk, lambda i, j, k: j, lambda i, j, k: k)
    else:
        b_specs = operand_specs(b_parts, b_stack, tk, tn, b_pc // tn, lambda i, j, k: k, lambda i, j, k: j)

    n_fixed_in = n_a + n_b + (1 if add is not None else 0)
    grid = (m_dim // tm, n_dim // tn, nk)
    ex_args, ex_in_specs, ex_out_specs, ex_out_shapes, ex_scratch = _carry_specs(carry)

    def body(*refs):
        refs, start_when, finish_when = _carry(carry, refs, n_fixed_in, 1)
        i, j, k = pl.program_id(0), pl.program_id(1), pl.program_id(2)
        step = (i * grid[1] + j) * grid[2] + k
        start_when(step == 0)
        compute(*refs)
        finish_when(step == grid[0] * grid[1] * grid[2] - 1)

    def compute(*refs):
        a_refs, b_refs = refs[:n_a], refs[n_a:n_a + n_b]
        rest = refs[n_a + n_b:]
        c_ref = rest[0] if add is not None else None
        o_ref = rest[1] if add is not None else rest[0]
        acc = None if nk == 1 else rest[-1]
        i, j, k = pl.program_id(0), pl.program_id(1), pl.program_id(2)

        def finish(res):
            if add is not None:
                res = res + c_ref[...]
            o_ref[...] = res.astype(out_dtype)

        def accumulate(a_ref, b_ref):
            prod = lax.dot_general(a_ref[...].astype(BF16), b_ref[...].astype(BF16), dims,
                                   preferred_element_type=F32)
            if nk == 1:
                finish(prod)
                return

            @pl.when(k == 0)
            def _():
                acc[...] = prod

            @pl.when(jnp.logical_and(k > 0, k < nk - 1))
            def _():
                acc[...] += prod

            @pl.when(k == nk - 1)
            def _():
                finish(acc[...] + prod)

        if n_a == 1 and n_b == 1:
            accumulate(a_refs[0], b_refs[0])
        else:
            many, block, per = (a_refs, (i if ta else k), a_pc // a_ct) if n_a > 1 else (b_refs, (k if tb else j), b_pc // b_ct)
            for p, ref in enumerate(many):
                @pl.when(block // per == p)
                def _(ref=ref):
                    accumulate(ref, b_refs[0]) if n_a > 1 else accumulate(a_refs[0], ref)

    per_out = n_dim // out_split // tn
    if out_split > 1:
        o_spec = pl.BlockSpec((None, tm, tn), lambda i, j, k: (j // per_out, i, j % per_out))
        out_shape = jax.ShapeDtypeStruct((out_split, m_dim, n_dim // out_split), out_dtype)
    else:
        o_spec = pl.BlockSpec((tm, tn), lambda i, j, k: (i, j))
        out_shape = jax.ShapeDtypeStruct((m_dim, n_dim), out_dtype)
    in_specs = a_specs + b_specs + ([pl.BlockSpec((tm, tn), lambda i, j, k: (i, j))] if add is not None else [])
    args = tuple(a_parts) + tuple(b_parts) + ((add,) if add is not None else ())
    acc_scratch = [pltpu.VMEM((tm, tn), F32)] if nk > 1 else []
    if carry is None:
        return _pcall(
            body, name=name, grid=grid, in_specs=in_specs, out_specs=o_spec, out_shape=out_shape,
            scratch_shapes=acc_scratch, compiler_params=_params(("parallel", "parallel", "arbitrary")),
        )(*args)
    return _pcall(
        body, name=name, grid=grid, in_specs=in_specs + ex_in_specs, out_specs=[o_spec] + ex_out_specs,
        out_shape=[out_shape] + ex_out_shapes, scratch_shapes=acc_scratch + ex_scratch,
        compiler_params=pltpu.CompilerParams(vmem_limit_bytes=VMEM_LIMIT, has_side_effects=True,
                                             dimension_semantics=("arbitrary", "arbitrary", "arbitrary")),
    )(*args, *ex_args)


def _rows(fn, ins, outs, *, name, consts=(), accs=(), carry=None):
    n_rows = ins[0][0].shape[0]
    tr = min(ROW_TILE, n_rows)
    assert n_rows % tr == 0
    n_in, n_c, n_out = len(ins), len(consts), len(outs)
    n_steps = n_rows // tr
    ex_args, ex_in_specs, ex_out_specs, ex_out_shapes, ex_scratch = _carry_specs(carry)

    def body(*refs):
        refs, start_when, finish_when = _carry(carry, refs, n_in + n_c, n_out + len(accs))
        start_when(pl.program_id(0) == 0)
        _compute(*refs)
        finish_when(pl.program_id(0) == n_steps - 1)

    def _compute(*refs):
        vals = [r[...] for r in refs[:n_in + n_c]]
        res = fn(*vals)
        res = res if isinstance(res, (tuple, list)) else (res,)
        o_refs = refs[n_in + n_c:]
        for r, v in zip(o_refs[:n_out], res[:n_out]):
            r[...] = v.astype(r.dtype)
        if accs:
            first = pl.program_id(0) == 0
            for r, v in zip(o_refs[n_out:], res[n_out:]):
                @pl.when(first)
                def _(r=r, v=v):
                    r[...] = v

                @pl.when(jnp.logical_not(first))
                def _(r=r, v=v):
                    r[...] += v

    in_specs = [pl.BlockSpec((tr, w), functools.partial(lambda i, cb: (i, cb), cb=cb)) for _, w, cb in ins]
    in_specs += [pl.BlockSpec(c.shape, functools.partial(lambda i, nd: (0,) * nd, nd=c.ndim)) for c in consts]
    out_specs = [pl.BlockSpec((tr, w), lambda i: (i, 0)) for w, _ in outs]
    out_specs += [pl.BlockSpec(s, lambda i: (0, 0)) for s in accs]
    out_shape = [jax.ShapeDtypeStruct((n_rows, w), dt) for w, dt in outs]
    out_shape += [jax.ShapeDtypeStruct(s, F32) for s in accs]
    sequential = bool(accs) or carry is not None
    params = _params(("arbitrary",) if sequential else ("parallel",))
    if carry is not None:
        params = pltpu.CompilerParams(vmem_limit_bytes=VMEM_LIMIT, dimension_semantics=("arbitrary",), has_side_effects=True)
    return _pcall(
        body, name=name, grid=(n_steps,), in_specs=in_specs + ex_in_specs, out_specs=out_specs + ex_out_specs,
        out_shape=out_shape + ex_out_shapes, scratch_shapes=ex_scratch, compiler_params=params,
    )(*[a for a, _, _ in ins], *consts, *ex_args)


def _full(a):
    return (a, a.shape[1], 0)


def _cmul(ar, ai, br, bi):
    return ar * br - ai * bi, ar * bi + ai * br


def _seg_scans(scans, seg):
    assert seg & (seg - 1) == 0
    chains = []
    for re_ref, im_ref, a_re, a_im, reverse in scans:
        for k in range(len(a_re)):
            chains.append((re_ref, im_ref, k, jnp.broadcast_to(a_re[k], (SUBLANES, LANES)),
                           jnp.broadcast_to(a_im[k], (SUBLANES, LANES)), reverse))

    def slab(i, reverse):
        j = seg - 1 - i if reverse else i
        return pl.ds(pl.multiple_of(j * SUBLANES, SUBLANES), SUBLANES)

    def local(i, carry):
        out = []
        for n, (re_ref, im_ref, k, ar, ai, reverse) in enumerate(chains):
            hr, hi = _cmul(ar, ai, carry[2 * n], carry[2 * n + 1])
            hr = hr + re_ref[k, slab(i, reverse), :]
            hi = hi + im_ref[k, slab(i, reverse), :]
            re_ref[k, slab(i, reverse), :] = hr
            im_ref[k, slab(i, reverse), :] = hi
            out += [hr, hi]
        return tuple(out)

    zero = jnp.zeros((SUBLANES, LANES), F32)
    end = lax.fori_loop(0, seg, local, (zero,) * (2 * len(chains)))

    row = lax.broadcasted_iota(jnp.int32, (SUBLANES, LANES), 0)
    enter = []
    for n, (_, _, _, ar, ai, reverse) in enumerate(chains):
        edge = SUBLANES - 1 if reverse else 0
        shift = SUBLANES - 1 if reverse else 1
        pr, pi = ar, ai
        for _ in range(seg.bit_length() - 1):
            pr, pi = _cmul(pr, pi, pr, pi)
        tr_, ti_ = zero, zero
        for _ in range(SUBLANES - 1):
            vr, vi = _cmul(pr, pi, tr_, ti_)
            tr_ = jnp.where(row == edge, 0.0, pltpu.roll(vr + end[2 * n], shift, 0))
            ti_ = jnp.where(row == edge, 0.0, pltpu.roll(vi + end[2 * n + 1], shift, 0))
        enter += [tr_, ti_]

    def fix(i, carry):
        out = []
        for n, (re_ref, im_ref, k, ar, ai, reverse) in enumerate(chains):
            er, ei = _cmul(ar, ai, carry[2 * n], carry[2 * n + 1])
            re_ref[k, slab(i, reverse), :] += er
            im_ref[k, slab(i, reverse), :] += ei
            out += [er, ei]
        return tuple(out)

    lax.fori_loop(0, seg, fix, tuple(enter))
    per_scan, off = [], 0
    for scan in scans:
        per_scan.append(enter[off:off + 2 * len(scan[2])])
        off += 2 * len(scan[2])
    return per_scan


def _to_slab(a):
    s, w = a.shape
    return a.reshape(SUBLANES, s // SUBLANES, w).swapaxes(0, 1).reshape(s, w)


def _from_slab(a):
    s, w = a.shape
    return a.reshape(s // SUBLANES, SUBLANES, w).swapaxes(0, 1).reshape(s, w)


def _lane_blocks(v, n_k):
    return [v[:, k * LANES:(k + 1) * LANES] for k in range(n_k)]


def _gather_k(ref, rows, n_k):
    return jnp.concatenate([ref[k, rows, :] for k in range(n_k)], axis=1)


def _dot(a, b, dims=(((1,), (0,)), ((), ()))):
    return lax.dot_general(a.astype(BF16), b.astype(BF16), dims, preferred_element_type=F32)


_NT = (((1,), (1,)), ((), ()))
_TN = (((0,), (0,)), ((), ()))


def _ssm_fwd(uz, b_blk, c_blk, par, d_skip, carry=None):
    seq = uz.shape[0]
    width = d_skip.shape[1]
    ns = SSM_CH // S5_GROUP * S5_STATE
    n_k = ns // LANES
    seg = seq // SUBLANES
    tb = min(SCAN_ROWS, seq)
    n_cb = width // SSM_CH
    ex_args, ex_in_specs, ex_out_specs, ex_out_shapes, ex_scratch = _carry_specs(carry)

    def body(*refs):
        refs, start_when, finish_when = _carry(carry, refs, 5, 2)
        start_when(pl.program_id(0) == 0)
        compute(*refs)
        finish_when(pl.program_id(0) == n_cb - 1)

    def compute(u_ref, b_ref, c_ref, par_ref, d_ref, y_ref, g_ref, hre, him):
        coef_r, coef_i = par_ref[0, 2:3, :], par_ref[0, 3:4, :]
        for c0 in range(0, seq, tb):
            rows = pl.ds(c0, tb)
            ub = u_ref[rows, :]
            bur, bui = _dot(ub, b_ref[0, 0]), _dot(ub, b_ref[0, 1])
            xr, xi = coef_r * bur - coef_i * bui, coef_r * bui + coef_i * bur
            for k in range(n_k):
                hre[k, rows, :] = xr[:, k * LANES:(k + 1) * LANES]
                him[k, rows, :] = xi[:, k * LANES:(k + 1) * LANES]
        _seg_scans([(hre, him, _lane_blocks(par_ref[0, 0:1, :], n_k), _lane_blocks(par_ref[0, 1:2, :], n_k), False)], seg)
        for c0 in range(0, seq, tb):
            rows = pl.ds(c0, tb)
            y = _dot(_gather_k(hre, rows, n_k), c_ref[0, 0]) - _dot(_gather_k(him, rows, n_k), c_ref[0, 1])
            y = y + d_ref[...] * u_ref[rows, :]
            y_ref[rows, :] = y
            g_ref[rows, :] = _gelu(y).astype(BF16)

    blk = pl.BlockSpec((seq, SSM_CH), lambda i: (0, i))
    params = _params(("parallel",)) if carry is None else pltpu.CompilerParams(
        vmem_limit_bytes=VMEM_LIMIT, dimension_semantics=("arbitrary",), has_side_effects=True)
    return _pcall(
        body, name="ssm_fwd", grid=(n_cb,),
        in_specs=[blk,
                  pl.BlockSpec((1, 2, SSM_CH, ns), lambda i: (i, 0, 0, 0)),
                  pl.BlockSpec((1, 2, ns, SSM_CH), lambda i: (i, 0, 0, 0)),
                  pl.BlockSpec((1, 4, ns), lambda i: (i, 0, 0)),
                  pl.BlockSpec((1, SSM_CH), lambda i: (0, i))] + ex_in_specs,
        out_specs=[blk, blk] + ex_out_specs,
        out_shape=[jax.ShapeDtypeStruct((seq, width), F32), jax.ShapeDtypeStruct((seq, width), BF16)] + ex_out_shapes,
        scratch_shapes=[pltpu.VMEM((n_k, seq, LANES), F32), pltpu.VMEM((n_k, seq, LANES), F32)] + ex_scratch,
        compiler_params=params,
    )(uz, b_blk, c_blk, par, d_skip, *ex_args)


def _ssm_bwd(uz, dy1, ys, b_blk, ct_blk, par, d_skip, carry=None):
    seq = uz.shape[0]
    width = d_skip.shape[1]
    ns_all = SSM_CH // S5_GROUP * S5_STATE
    n_half = 2
    ns = ns_all // n_half
    n_k = ns // LANES
    seg = seq // SUBLANES
    tb = min(SCAN_ROWS, seq)
    n_cb = width // SSM_CH
    ex_args, ex_in_specs, ex_out_specs, ex_out_shapes, ex_scratch = _carry_specs(carry)

    def body(*refs):
        refs, start_when, finish_when = _carry(carry, refs, 7, 5)
        step = pl.program_id(0) * n_half + pl.program_id(1)
        start_when(step == 0)
        compute(*refs)
        finish_when(step == n_cb * n_half - 1)

    def compute(u_ref, dy_ref, ys_ref, b_ref, ct_ref, par_ref, d_ref,
                du_ref, dbt_ref, dct_ref, dpar_ref, dd_ref, hre, him, gre, gim):
        half = pl.program_id(1)
        a_r, a_i = par_ref[0, 0:1, :], par_ref[0, 1:2, :]
        coef_r, coef_i = par_ref[0, 2:3, :], par_ref[0, 3:4, :]

        def dys_of(rows):
            return dy_ref[rows, :] * _dgelu(ys_ref[rows, :])

        for c0 in range(0, seq, tb):
            rows = pl.ds(c0, tb)
            ub = u_ref[rows, :]
            bur, bui = _dot(ub, b_ref[0, 0]), _dot(ub, b_ref[0, 1])
            xr, xi = coef_r * bur - coef_i * bui, coef_r * bui + coef_i * bur
            dys = dys_of(rows)
            gr, gi = _dot(dys, ct_ref[0, 0]), -_dot(dys, ct_ref[0, 1])
            for k in range(n_k):
                lanes = slice(k * LANES, (k + 1) * LANES)
                hre[k, rows, :] = xr[:, lanes]
                him[k, rows, :] = xi[:, lanes]
                gre[k, rows, :] = gr[:, lanes]
                gim[k, rows, :] = gi[:, lanes]
        enter, _ = _seg_scans([(hre, him, _lane_blocks(a_r, n_k), _lane_blocks(a_i, n_k), False),
                               (gre, gim, _lane_blocks(a_r, n_k), _lane_blocks(-a_i, n_k), True)], seg)

        def corr(j, carry):
            acc, prev = carry
            acc_o, prev_o = [], []
            for k in range(n_k):
                sl = pl.ds(pl.multiple_of(j * SUBLANES, SUBLANES), SUBLANES)
                g_r, g_i = gre[k, sl, :], gim[k, sl, :]
                p_r, p_i = prev[2 * k], prev[2 * k + 1]
                acc_o += [acc[2 * k] + g_r * p_r + g_i * p_i, acc[2 * k + 1] + g_i * p_r - g_r * p_i]
                prev_o += [hre[k, sl, :], him[k, sl, :]]
            return tuple(acc_o), tuple(prev_o)

        zero = jnp.zeros((SUBLANES, LANES), F32)
        acc, _ = lax.fori_loop(0, seg, corr, ((zero,) * (2 * n_k), tuple(enter)), unroll=SCAN_UNROLL)
        da_r = jnp.concatenate([_colsum(acc[2 * k]) for k in range(n_k)], axis=1)
        da_i = jnp.concatenate([_colsum(acc[2 * k + 1]) for k in range(n_k)], axis=1)

        zeros_cn = jnp.zeros((SSM_CH, ns), F32)
        qt_r, qt_i, dct_r, dct_i = zeros_cn, zeros_cn, zeros_cn, zeros_cn
        dd = jnp.zeros((1, SSM_CH), F32)
        first = half == 0
        for c0 in range(0, seq, tb):
            rows = pl.ds(c0, tb)
            ub = u_ref[rows, :]
            dys = dys_of(rows)
            dct_r = dct_r + _dot(dys, _gather_k(hre, rows, n_k), _TN)
            dct_i = dct_i - _dot(dys, _gather_k(him, rows, n_k), _TN)
            g_r, g_i = _gather_k(gre, rows, n_k), _gather_k(gim, rows, n_k)
            qt_r = qt_r + _dot(ub, g_r, _TN)
            qt_i = qt_i + _dot(ub, g_i, _TN)
            dbu_r, dbu_i = coef_r * g_r + coef_i * g_i, coef_r * g_i - coef_i * g_r
            du = _dot(dbu_r, b_ref[0, 0], _NT) + _dot(dbu_i, b_ref[0, 1], _NT)
            dd = dd + _colsum(dys * ub)

            @pl.when(first)
            def _(du=du, dys=dys, rows=rows):
                du_ref[rows, :] = du + d_ref[...] * dys

            @pl.when(jnp.logical_not(first))
            def _(du=du, rows=rows):
                du_ref[rows, :] += du

        @pl.when(first)
        def _():
            dd_ref[...] = dd

        b_r, b_i = b_ref[0, 0], b_ref[0, 1]
        dbt_ref[0, 0] = coef_r * qt_r + coef_i * qt_i
        dbt_ref[0, 1] = coef_r * qt_i - coef_i * qt_r
        dct_ref[0, 0] = dct_r
        dct_ref[0, 1] = dct_i
        dpar_ref[0, 0:1, :] = da_r
        dpar_ref[0, 1:2, :] = da_i
        dpar_ref[0, 2:3, :] = _colsum(b_r * qt_r + b_i * qt_i)
        dpar_ref[0, 3:4, :] = _colsum(b_r * qt_i - b_i * qt_r)

    blk = lambda i, h: (0, i)
    params = _params(("parallel", "arbitrary")) if carry is None else pltpu.CompilerParams(
        vmem_limit_bytes=VMEM_LIMIT, dimension_semantics=("arbitrary", "arbitrary"), has_side_effects=True)
    return _pcall(
        body, name="ssm_bwd", grid=(n_cb, n_half),
        in_specs=[pl.BlockSpec((seq, SSM_CH), blk), pl.BlockSpec((seq, SSM_CH), blk), pl.BlockSpec((seq, SSM_CH), blk),
                  pl.BlockSpec((1, 2, SSM_CH, ns), lambda i, h: (i, 0, 0, h)),
                  pl.BlockSpec((1, 2, SSM_CH, ns), lambda i, h: (i, 0, 0, h)),
                  pl.BlockSpec((1, 4, ns), lambda i, h: (i, 0, h)),
                  pl.BlockSpec((1, SSM_CH), blk)] + ex_in_specs,
        out_specs=[pl.BlockSpec((seq, SSM_CH), blk),
                   pl.BlockSpec((1, 2, SSM_CH, ns), lambda i, h: (i, 0, 0, h)),
                   pl.BlockSpec((1, 2, SSM_CH, ns), lambda i, h: (i, 0, 0, h)),
                   pl.BlockSpec((1, 4, ns), lambda i, h: (i, 0, h)),
                   pl.BlockSpec((1, SSM_CH), blk)] + ex_out_specs,
        out_shape=[jax.ShapeDtypeStruct((seq, width), F32),
                   jax.ShapeDtypeStruct((n_cb, 2, SSM_CH, ns_all), F32),
                   jax.ShapeDtypeStruct((n_cb, 2, SSM_CH, ns_all), F32),
                   jax.ShapeDtypeStruct((n_cb, 4, ns_all), F32),
                   jax.ShapeDtypeStruct((1, width), F32)] + ex_out_shapes,
        scratch_shapes=[pltpu.VMEM((n_k, seq, LANES), F32) for _ in range(4)] + ex_scratch,
        compiler_params=params,
    )(uz, dy1, ys, b_blk, ct_blk, par, d_skip, *ex_args)


def _ssm_discretize(a_re, a_im, log_dt):
    dt = jnp.exp(log_dt)[:, None]
    mag = jnp.exp(a_re * dt)
    abar_re = mag * jnp.cos(a_im * dt)
    abar_im = mag * jnp.sin(a_im * dt)
    den = a_re * a_re + a_im * a_im
    nr = abar_re - 1.0
    coef_re = (nr * a_re + abar_im * a_im) / den
    coef_im = (abar_im * a_re - nr * a_im) / den
    return abar_re, abar_im, coef_re, coef_im


def _block_diag(w_gcp):
    gpb = SSM_CH // S5_GROUP
    n_cb = w_gcp.shape[0] // gpb
    w = w_gcp.reshape(n_cb, gpb, S5_GROUP, 1, S5_STATE)
    eye = jnp.eye(gpb, dtype=w.dtype)[None, :, None, :, None]
    return (w * eye).reshape(n_cb, SSM_CH, gpb * S5_STATE)


def _block_diag_extract(w_blk):
    gpb = SSM_CH // S5_GROUP
    n_cb = w_blk.shape[0]
    w = w_blk.reshape(n_cb, gpb, S5_GROUP, gpb, S5_STATE)
    w = jnp.moveaxis(jnp.diagonal(w, axis1=1, axis2=3), -1, 1)
    return w.reshape(n_cb * gpb, S5_GROUP, S5_STATE)


def _split3(x):
    hi = x.astype(BF16)
    mid = (x - hi.astype(F32)).astype(BF16)
    lo = (x - hi.astype(F32) - mid.astype(F32)).astype(BF16)
    return hi, mid, lo


def _tri_sum(tri, x):
    hi, mid, lo = _split3(x)
    return (jnp.dot(tri, hi, preferred_element_type=F32) + jnp.dot(tri, mid, preferred_element_type=F32)
            + jnp.dot(tri, lo, preferred_element_type=F32))


def _log_sigmoid(x):
    return jnp.minimum(x, 0.0) - jnp.log(1.0 + jnp.exp(-jnp.abs(x)))


def _cum_fwd(fl, b_f):
    seq = fl.shape[0]
    t = min(CUM_TILE, seq)

    def body(fl_ref, b_ref, o_ref, carry):
        @pl.when(pl.program_id(0) == 0)
        def _():
            carry[...] = jnp.zeros_like(carry)

        r = lax.broadcasted_iota(jnp.int32, (t, t), 0)
        c = lax.broadcasted_iota(jnp.int32, (t, t), 1)
        tri = (c <= r).astype(BF16)
        cum = _tri_sum(tri, _log_sigmoid(fl_ref[...] + b_ref[...])) + carry[...]
        o_ref[...] = cum
        carry[...] = cum[t - 1:t, :]

    return _pcall(
        body, name="cum_fwd", grid=(seq // t,),
        in_specs=[pl.BlockSpec((t, LANES), lambda i: (i, 0)), pl.BlockSpec((1, LANES), lambda i: (0, 0))],
        out_specs=pl.BlockSpec((t, LANES), lambda i: (i, 0)),
        out_shape=jax.ShapeDtypeStruct((seq, LANES), F32),
        scratch_shapes=[pltpu.VMEM((1, LANES), F32)],
        compiler_params=_params(("arbitrary",)),
    )(fl, b_f)


def _cum_bwd(dcum, fl, b_f):
    seq = fl.shape[0]
    t = min(CUM_TILE, seq)
    nb = seq // t

    def body(dc_ref, fl_ref, b_ref, o_ref, db_ref, carry):
        @pl.when(pl.program_id(0) == 0)
        def _():
            carry[...] = jnp.zeros_like(carry)
            db_ref[...] = jnp.zeros_like(db_ref)

        r = lax.broadcasted_iota(jnp.int32, (t, t), 0)
        c = lax.broadcasted_iota(jnp.int32, (t, t), 1)
        tri = (c >= r).astype(BF16)
        rev = _tri_sum(tri, dc_ref[...]) + carry[...]
        carry[...] = rev[0:1, :]
        dfl = rev * _sigmoid(-(fl_ref[...] + b_ref[...]))
        o_ref[...] = dfl
        db_ref[...] += _colsum(dfl)

    return _pcall(
        body, name="cum_bwd", grid=(nb,),
        in_specs=[pl.BlockSpec((t, LANES), lambda i: (nb - 1 - i, 0)), pl.BlockSpec((t, LANES), lambda i: (nb - 1 - i, 0)),
                  pl.BlockSpec((1, LANES), lambda i: (0, 0))],
        out_specs=[pl.BlockSpec((t, LANES), lambda i: (nb - 1 - i, 0)), pl.BlockSpec((1, LANES), lambda i: (0, 0))],
        out_shape=[jax.ShapeDtypeStruct((seq, LANES), F32), jax.ShapeDtypeStruct((1, LANES), F32)],
        scratch_shapes=[pltpu.VMEM((1, LANES), F32)],
        compiler_params=_params(("arbitrary",)),
    )(dcum, fl, b_f)


def _att_scores(q, kb, ck, row0, col0, masked):
    s = _dot(q, kb, _NT) - ck
    if masked:
        rows = row0 + lax.broadcasted_iota(jnp.int32, s.shape, 0)
        cols = col0 + lax.broadcasted_iota(jnp.int32, s.shape, 1)
        s = jnp.where(cols <= rows, s, NEG_INF)
    return s


def _pairwise_loop(lo, hi, step_fn, init):
    n = hi - lo
    w = ATT_UNROLL

    def several(p, carry):
        for u in range(w):
            carry = step_fn(lo + w * p + u, carry)
        return carry

    carry = lax.fori_loop(0, n // w, several, init)
    return lax.fori_loop(lo + (n // w) * w, hi, step_fn, carry)


def _att_fwd(qz, kv, ck):
    seq = qz.shape[0]
    heads = ck.shape[0]
    t = min(ATT_TILE, seq)
    scale = HEAD_DIM ** -0.5

    def body(q_ref, z_ref, k_ref, v_ref, ck_ref, o_ref, og_ref, lse_ref):
        i = pl.program_id(1)
        q = (q_ref[...] * scale).astype(BF16)

        def block(j, carry, masked):
            m, l, acc = carry
            rows = pl.ds(pl.multiple_of(j * t, t), t)
            s = _att_scores(q, k_ref[rows, :], ck_ref[0, j], i * t, j * t, masked)
            m_new = jnp.maximum(m, jnp.max(s, axis=1, keepdims=True))
            p = jnp.exp(s - m_new)
            alpha = jnp.exp(m - m_new)
            return m_new, alpha * l + jnp.sum(p, axis=1, keepdims=True), alpha * acc + _dot(p, v_ref[rows, :])

        init = (jnp.full((t, 1), NEG_INF, F32), jnp.zeros((t, 1), F32), jnp.zeros((t, HEAD_DIM), F32))
        carry = _pairwise_loop(0, i, functools.partial(block, masked=False), init)
        m, l, acc = block(i, carry, True)
        o = acc / l
        o_ref[...] = o
        og_ref[...] = (o * _silu(z_ref[...])).astype(BF16)
        lse_ref[0] = m + jnp.log(l)

    qblk = pl.BlockSpec((t, HEAD_DIM), lambda h, i: (i, h))
    return _pcall(
        body, name="att_fwd", grid=(heads, seq // t),
        in_specs=[qblk, pl.BlockSpec((t, HEAD_DIM), lambda h, i: (i, heads + h)),
                  pl.BlockSpec((seq, HEAD_DIM), lambda h, i: (0, h)),
                  pl.BlockSpec((seq, HEAD_DIM), lambda h, i: (0, heads + h)),
                  pl.BlockSpec((1, seq // t, 1, t), lambda h, i: (h, 0, 0, 0))],
        out_specs=[qblk, qblk, pl.BlockSpec((1, t, 1), lambda h, i: (h, i, 0))],
        out_shape=[jax.ShapeDtypeStruct((seq, heads * HEAD_DIM), F32), jax.ShapeDtypeStruct((seq, heads * HEAD_DIM), BF16),
                   jax.ShapeDtypeStruct((heads, seq, 1), F32)],
        compiler_params=_params(("parallel", "parallel")),
    )(qz, qz, kv, kv, ck)


def _att_bwd_q(qz, kv, do, o, lse, ck):
    seq = qz.shape[0]
    heads = ck.shape[0]
    t = min(ATT_TILE, seq)
    scale = HEAD_DIM ** -0.5

    def body(q_ref, k_ref, v_ref, do_ref, o_ref, lse_ref, ck_ref, dq_ref, delta_ref):
        i = pl.program_id(1)
        q = (q_ref[...] * scale).astype(BF16)
        dob = do_ref[...].astype(BF16)
        delta = jnp.sum(do_ref[...] * o_ref[...], axis=1, keepdims=True)
        lse_v = lse_ref[0]

        def block(j, carry, masked):
            dq, pdp = carry
            rows = pl.ds(pl.multiple_of(j * t, t), t)
            kb = k_ref[rows, :]
            s = _att_scores(q, kb, ck_ref[0, j], i * t, j * t, masked)
            p = jnp.exp(s - lse_v)
            dp = _dot(dob, v_ref[rows, :], _NT)
            ds = p * (dp - delta)
            return dq + _dot(ds, kb), pdp + jnp.sum(p * dp, axis=1, keepdims=True)

        init = (jnp.zeros((t, HEAD_DIM), F32), jnp.zeros((t, 1), F32))
        carry = _pairwise_loop(0, i, functools.partial(block, masked=False), init)
        dq, pdp = block(i, carry, True)
        dq_ref[...] = (dq * scale).astype(dq_ref.dtype)
        delta_ref[0] = pdp

    qblk = pl.BlockSpec((t, HEAD_DIM), lambda h, i: (i, h))
    col = pl.BlockSpec((1, t, 1), lambda h, i: (h, i, 0))
    return _pcall(
        body, name="att_bwd_q", grid=(heads, seq // t),
        in_specs=[qblk, pl.BlockSpec((seq, HEAD_DIM), lambda h, i: (0, h)),
                  pl.BlockSpec((seq, HEAD_DIM), lambda h, i: (0, heads + h)), qblk, qblk, col,
                  pl.BlockSpec((1, seq // t, 1, t), lambda h, i: (h, 0, 0, 0))],
        out_specs=[qblk, col],
        out_shape=[jax.ShapeDtypeStruct((seq, heads * HEAD_DIM), BF16), jax.ShapeDtypeStruct((heads, seq, 1), F32)],
        compiler_params=_params(("parallel", "parallel")),
    )(qz, kv, kv, do, o, lse, ck)


def _att_bwd_kv(qz, kv, do, lse, delta, ck):
    seq = qz.shape[0]
    heads = ck.shape[0]
    t = min(ATT_TILE, seq)
    nq = seq // t
    scale = HEAD_DIM ** -0.5

    def body(q_ref, k_ref, v_ref, do_ref, lse_ref, delta_ref, ck_ref, dk_ref, dv_ref, dck_ref):
        j = pl.program_id(1)
        kb, vb = k_ref[...], v_ref[...]
        ckv = ck_ref[0, 0]

        def block(i, carry, masked):
            dk, dv, dck = carry
            rows = pl.ds(pl.multiple_of(i * t, t), t)
            qb = (q_ref[rows, :] * scale).astype(BF16)
            dob = do_ref[rows, :].astype(BF16)
            s = _att_scores(qb, kb, ckv, i * t, j * t, masked)
            p = jnp.exp(s - lse_ref[0, rows, :])
            ds = p * (_dot(dob, vb, _NT) - delta_ref[0, rows, :])
            return dk + _dot(ds, qb, _TN), dv + _dot(p, dob, _TN), dck - _colsum(ds)

        init = (jnp.zeros((t, HEAD_DIM), F32), jnp.zeros((t, HEAD_DIM), F32), jnp.zeros((1, t), F32))
        carry = block(j, init, True)
        dk, dv, dck = _pairwise_loop(j + 1, nq, functools.partial(block, masked=False), carry)
        dk_ref[...] = dk.astype(dk_ref.dtype)
        dv_ref[...] = dv.astype(dv_ref.dtype)
        dck_ref[0, 0] = dck

    head = pl.BlockSpec((seq, HEAD_DIM), lambda h, j: (0, h))
    col = pl.BlockSpec((1, seq, 1), lambda h, j: (h, 0, 0))
    kblk = pl.BlockSpec((t, HEAD_DIM), lambda h, j: (j, h))
    row = pl.BlockSpec((1, 1, 1, t), lambda h, j: (h, j, 0, 0))
    return _pcall(
        body, name="att_bwd_kv", grid=(heads, nq),
        in_specs=[head, kblk, pl.BlockSpec((t, HEAD_DIM), lambda h, j: (j, heads + h)), head, col, col, row],
        out_specs=[kblk, kblk, row],
        out_shape=[jax.ShapeDtypeStruct((seq, heads * HEAD_DIM), BF16), jax.ShapeDtypeStruct((seq, heads * HEAD_DIM), BF16),
                   jax.ShapeDtypeStruct((heads, nq, 1, t), F32)],
        compiler_params=_params(("parallel", "parallel")),
    )(qz, kv, kv, do, lse, delta, ck)


def _mesh_pos():
    return lax.axis_index("x"), lax.axis_index("y"), lax.axis_index("c")


def _other_chips(x, y):
    return [(1 - x, y), (x, 1 - y), (1 - x, 1 - y)]


def _all_gather_weights(big, small):
    nb, ns = len(big), len(small)
    n_remote = 3 * (nb + ns)

    def plan(ins, outs, sems):
        send_sems, recv_sems, fwd_send, fwd_recv = sems
        x, y, c = _mesh_pos()
        chips = _other_chips(x, y)
        slots = [2 * cx + cy for cx, cy in chips]

        def half(ref, hc):
            rh = ref.shape[-2] // 2
            return ref.at[pl.ds(hc * rh, rh), :]

        def remote(i, j, src_chip, from_in):
            if i < nb:
                src = half(ins[i], c) if from_in else half(outs[i].at[src_chip], c)
                dst = half(outs[i].at[src_chip], c)
            else:
                src = ins[i] if from_in else outs[i].at[src_chip]
                dst = outs[i].at[src_chip]
            k = 3 * i + j
            return pltpu.make_async_remote_copy(src_ref=src, dst_ref=dst, send_sem=send_sems.at[k],
                                                recv_sem=recv_sems.at[k], device_id=(*chips[j], c),
                                                device_id_type=MESH_ID)

        def forward(i, j, hc):
            part = half(outs[i].at[slots[j]], hc)
            k = 3 * i + j
            return pltpu.make_async_remote_copy(src_ref=part, dst_ref=part, send_sem=fwd_send.at[k],
                                                recv_sem=fwd_recv.at[k], device_id=(x, y, 1 - c),
                                                device_id_type=MESH_ID)

        return remote, forward, 2 * x + y, slots, c

    def start(ins, outs, sems):
        remote, _, me, _, _ = plan(ins, outs, sems)
        for i in range(nb + ns):
            for j in range(3):
                remote(i, j, me, True).start()

    def finish(ins, outs, sems):
        remote, forward, me, slots, c = plan(ins, outs, sems)
        for i in range(nb + ns):
            for j in range(3):
                remote(i, j, slots[j], False).wait_recv()
                if i < nb:
                    forward(i, j, c).start()
        for i in range(nb):
            for j in range(3):
                forward(i, j, 1 - c).wait_recv()
        for i in range(nb + ns):
            for j in range(3):
                remote(i, j, me, True).wait_send()
                if i < nb:
                    forward(i, j, c).wait_send()

    arrays = tuple(big) + tuple(small)
    return _Exchange(
        arrays=arrays,
        out_shapes=tuple(jax.ShapeDtypeStruct((N_CHIPS,) + a.shape, a.dtype) for a in arrays),
        scratch=(pltpu.SemaphoreType.DMA((n_remote,)), pltpu.SemaphoreType.DMA((n_remote,)),
                 pltpu.SemaphoreType.DMA((3 * max(nb, 1),)), pltpu.SemaphoreType.DMA((3 * max(nb, 1),))),
        start=start, finish=finish)


def _swap_halves(grads):
    n = len(grads)

    def copies(ins, outs, sems):
        x, y, c = _mesh_pos()
        cps = []
        for i in range(n):
            rh = ins[i].shape[1] // 2
            cps.append(pltpu.make_async_remote_copy(
                src_ref=ins[i].at[:, pl.ds((1 - c) * rh, rh), :], dst_ref=outs[i], send_sem=sems[0].at[i],
                recv_sem=sems[1].at[i], device_id=(x, y, 1 - c), device_id_type=MESH_ID))
        return cps

    def start(ins, outs, sems):
        for cp in copies(ins, outs, sems):
            cp.start()

    def finish(ins, outs, sems):
        for cp in copies(ins, outs, sems):
            cp.wait()

    return _Exchange(
        arrays=tuple(grads),
        out_shapes=tuple(jax.ShapeDtypeStruct((g.shape[0], g.shape[1] // 2, g.shape[2]), g.dtype) for g in grads),
        scratch=(pltpu.SemaphoreType.DMA((n,)), pltpu.SemaphoreType.DMA((n,))),
        start=start, finish=finish)


def _pair_sum_bf16(g, theirs, pos, name):
    n, rh, cdim = theirs.shape
    tr = min(ROW_TILE, rh)
    nb = rh // tr

    def body(pos_ref, g_ref, t_ref, o_ref):
        o_ref[...] = (g_ref[...] + t_ref[...]).astype(BF16)

    slot = lambda s, pos: (pos[0] + 1 + s) % n
    grid_spec = pltpu.PrefetchScalarGridSpec(
        num_scalar_prefetch=1, grid=(n - 1, nb),
        in_specs=[pl.BlockSpec((None, tr, cdim), lambda s, i, pos: (slot(s, pos), pos[1] * nb + i, 0)),
                  pl.BlockSpec((None, tr, cdim), lambda s, i, pos: (slot(s, pos), i, 0))],
        out_specs=pl.BlockSpec((None, tr, cdim), lambda s, i, pos: (slot(s, pos), i, 0)))
    return _pcall(body, name=name, grid_spec=grid_spec, out_shape=jax.ShapeDtypeStruct(theirs.shape, BF16),
                  compiler_params=_params(("parallel", "parallel")))(pos, g, theirs)


def _chip_sum(g, theirs, recv, pos, name):
    n, rh, cdim = theirs.shape
    tr = min(ROW_TILE, rh)
    nb = rh // tr

    def body(pos_ref, g_ref, t_ref, r0, r1, r2, o_ref):
        o_ref[...] = (((g_ref[...] + t_ref[...]) + r0[...]) + r1[...]) + r2[...]

    grid_spec = pltpu.PrefetchScalarGridSpec(
        num_scalar_prefetch=1, grid=(nb,),
        in_specs=[pl.BlockSpec((None, tr, cdim), lambda i, pos: (pos[0], pos[1] * nb + i, 0)),
                  pl.BlockSpec((None, tr, cdim), lambda i, pos: (pos[0], i, 0))]
        + [pl.BlockSpec((None, tr, cdim), functools.partial(lambda i, pos, j: (j, i, 0), j=j)) for j in range(3)],
        out_specs=pl.BlockSpec((tr, cdim), lambda i, pos: (pos[1] * nb + i, 0)))
    return _pcall(body, name=name, grid_spec=grid_spec, out_shape=jax.ShapeDtypeStruct((2 * rh, cdim), F32),
                  compiler_params=_params(("parallel",)))(pos, g, theirs, recv, recv, recv)


def _scatter_to_owner(parts):
    n = len(parts)

    def copies(ins, outs, sems):
        x, y, c = _mesh_pos()
        chips = _other_chips(x, y)
        cps = []
        for i in range(n):
            for j in range(3):
                k = 3 * i + j
                cps.append(pltpu.make_async_remote_copy(
                    src_ref=ins[i].at[2 * chips[j][0] + chips[j][1]], dst_ref=outs[i].at[j],
                    send_sem=sems[0].at[k], recv_sem=sems[1].at[k], device_id=(*chips[j], c),
                    device_id_type=MESH_ID))
        return cps

    def start(ins, outs, sems):
        for cp in copies(ins, outs, sems):
            cp.start()

    def finish(ins, outs, sems):
        for cp in copies(ins, outs, sems):
            cp.wait()

    return _Exchange(
        arrays=tuple(parts),
        out_shapes=tuple(jax.ShapeDtypeStruct((3,) + p.shape[1:], p.dtype) for p in parts),
        scratch=(pltpu.SemaphoreType.DMA((3 * n,)), pltpu.SemaphoreType.DMA((3 * n,))),
        start=start, finish=finish)


def _join_halves(shards):
    n = len(shards)

    def body(*refs):
        outs, send_sems, recv_sems = refs[n:2 * n], refs[2 * n], refs[2 * n + 1]
        x, y, c = _mesh_pos()

        def copy(i, hc):
            rh = outs[i].shape[0] // 2
            rows = outs[i].at[pl.ds(hc * rh, rh), :]
            return pltpu.make_async_remote_copy(src_ref=rows, dst_ref=rows, send_sem=send_sems.at[i],
                                                recv_sem=recv_sems.at[i], device_id=(x, y, 1 - c),
                                                device_id_type=MESH_ID)

        for i in range(n):
            copy(i, c).start()
        for i in range(n):
            copy(i, c).wait_send()
            copy(i, 1 - c).wait_recv()

    return _pcall(
        body, name="grad_join_halves", in_specs=[ANY] * n, out_specs=[ANY] * n,
        out_shape=[jax.ShapeDtypeStruct(s.shape, s.dtype) for s in shards],
        input_output_aliases={i: i for i in range(n)},
        scratch_shapes=[pltpu.SemaphoreType.DMA((n,)), pltpu.SemaphoreType.DMA((n,))],
        compiler_params=pltpu.CompilerParams(has_side_effects=True),
    )(*shards)


def _all_reduce_small(v):
    n_rows = v.shape[0]
    rh = n_rows // 2
    assert rh % SUBLANES == 0

    def body(v_ref, o_ref, part, recv, send_sems, recv_sems):
        x, y, c = _mesh_pos()
        sibling = (x, y, 1 - c)
        mine = pl.ds(pl.multiple_of(c * rh, SUBLANES), rh)
        theirs = pl.ds(pl.multiple_of((1 - c) * rh, SUBLANES), rh)

        def exchange(s, src, dst, peer):
            cp = pltpu.make_async_remote_copy(src_ref=src, dst_ref=dst, send_sem=send_sems.at[s],
                                              recv_sem=recv_sems.at[s], device_id=peer, device_id_type=MESH_ID)
            cp.start()
            cp.wait()

        exchange(0, v_ref.at[theirs, :], recv.at[0], sibling)
        part[...] = v_ref[mine, :] + recv[0]
        for s, peer in ((1, (1 - x, y, c)), (2, (x, 1 - y, c))):
            exchange(s, part, recv.at[s], peer)
            part[...] = part[...] + recv[s]
        o_ref[mine, :] = part[...]
        exchange(3, part, o_ref.at[mine, :], sibling)

    vm = pl.BlockSpec(memory_space=pltpu.VMEM)
    return _pcall(
        body, name="all_reduce_small", in_specs=[vm], out_specs=vm,
        out_shape=jax.ShapeDtypeStruct(v.shape, v.dtype),
        scratch_shapes=[pltpu.VMEM((rh, LANES), v.dtype), pltpu.VMEM((3, rh, LANES), v.dtype),
                        pltpu.SemaphoreType.DMA((4,)), pltpu.SemaphoreType.DMA((4,))],
        compiler_params=pltpu.CompilerParams(vmem_limit_bytes=VMEM_LIMIT, has_side_effects=True),
    )(v)


def _adamw_math(w, g, m, v):
    m = ADAM_B1 * m + (1.0 - ADAM_B1) * g
    v = ADAM_B2 * v + (1.0 - ADAM_B2) * (g * g)
    m_hat = m / (1.0 - ADAM_B1 ** ADAM_STEP)
    v_hat = v / (1.0 - ADAM_B2 ** ADAM_STEP)
    delta = -ADAM_LR * (m_hat / (jnp.sqrt(v_hat) + ADAM_EPS) + ADAM_WD * w)
    return delta, m, v


def _adamw(w, g, m, v, name):
    wd = w.shape[1]
    return _rows(_adamw_math, [_full(w), _full(g), _full(m), _full(v)], [(wd, F32)] * 3, name=name)


def _rows_of(a):
    return -(-a.size // (LANES * SUBLANES)) * SUBLANES


def _pack(arrs, fill=0.0):
    parts = []
    for a in arrs:
        flat = a.reshape(-1)
        flat = jnp.pad(flat, (0, _rows_of(a) * LANES - a.size), constant_values=fill)
        parts.append(flat.reshape(-1, LANES))
    used = sum(p.shape[0] for p in parts)
    rows = -(-used // ROW_TILE) * ROW_TILE
    parts.append(jnp.full((rows - used, LANES), fill, F32))
    return jnp.concatenate(parts, axis=0)


def _unpack(buf, like):
    out, off = [], 0
    for a in like:
        out.append(buf[off:off + _rows_of(a)].reshape(-1)[:a.size].reshape(a.shape))
        off += _rows_of(a)
    return out


def kernel(x, norm_pre, norm_post, s5_w_in, s5_a_re, s5_a_im, s5_log_dt, s5_b_re, s5_b_im, s5_c_re, s5_c_im, s5_d, s5_w_glu, s5_b_glu, s5_w_out, kv_norm, kv_w, kv_b_f, fox_w_in, fox_w_out, loss_target, m_norm_pre, m_norm_post, m_s5_w_in, m_s5_a_re, m_s5_a_im, m_s5_log_dt, m_s5_b_re, m_s5_b_im, m_s5_c_re, m_s5_c_im, m_s5_d, m_s5_w_glu, m_s5_b_glu, m_s5_w_out, m_kv_norm, m_kv_w, m_kv_b_f, m_fox_w_in, m_fox_w_out, v_norm_pre, v_norm_post, v_s5_w_in, v_s5_a_re, v_s5_a_im, v_s5_log_dt, v_s5_b_re, v_s5_b_im, v_s5_c_re, v_s5_c_im, v_s5_d, v_s5_w_glu, v_s5_b_glu, v_s5_w_out, v_kv_norm, v_kv_w, v_kv_b_f, v_fox_w_in, v_fox_w_out):
    seq, dm = x.shape[1], x.shape[2]
    width = dm
    heads = dm // HEAD_DIM
    fw = heads * HEAD_DIM
    groups = width // S5_GROUP
    chip = 2 * lax.axis_index("x") + lax.axis_index("y")

    big_shards = [s5_w_in[0], s5_w_glu[0], s5_w_out[0], kv_w, fox_w_in[0], fox_w_out[0]]
    own_shards = [w.astype(BF16) for w in big_shards] + [s5_d, s5_b_glu]
    fill_own = lambda gs, owns: [lax.dynamic_update_slice(g, own[None], (chip, 0, 0)) for g, own in zip(gs, owns)]
    c_idx = lax.axis_index("c")
    pos = jnp.stack([chip, c_idx]).astype(jnp.int32)
    h0 = x[0]
    target = loss_target[0]
    g_pre0, g_pre1 = norm_pre[0:1], norm_pre[1:2]
    g_post0, g_post1 = norm_post[0:1], norm_post[1:2]
    g_kv = kv_norm.reshape(1, dm)
    first_owns = [own_shards[0], s5_d, s5_b_glu]
    xn1, *first_gathered = _rows(lambda h, g: (h * _rstd(h) * g,), [_full(h0)], [(dm, BF16)], consts=[g_pre0], name="norm_pre0",
                                 carry=_all_gather_weights(first_owns[:1], first_owns[1:]))
    g_win, g_d, g_bglu = fill_own(first_gathered, first_owns)
    gather_rest = _all_gather_weights(own_shards[1:5], [])
    gather_last = _all_gather_weights(own_shards[5:6], [])
    cols = lambda g: jnp.moveaxis(g, 0, 1).reshape(g.shape[1], -1)
    rows = lambda g: g.reshape(-1, g.shape[2])
    w_in = g_win
    d_skip, b_glu = cols(g_d), cols(g_bglu)
    b_f = jnp.pad(kv_b_f, (0, LANES - heads)).reshape(1, LANES)

    a_re, a_im, log_dt = s5_a_re[0], s5_a_im[0], s5_log_dt[0]
    disc, disc_vjp = jax.vjp(_ssm_discretize, a_re, a_im, log_dt)
    gpb = SSM_CH // S5_GROUP
    n_cb = groups // gpb
    par = jnp.stack([p.reshape(n_cb, gpb * S5_STATE) for p in disc], axis=1)
    b_t = lambda b: jnp.swapaxes(b, 1, 2)
    b_blk = jnp.stack([_block_diag(b_t(s5_b_re[0])), _block_diag(b_t(s5_b_im[0]))], axis=1)
    ct_blk = jnp.stack([_block_diag(s5_c_re[0]), _block_diag(s5_c_im[0])], axis=1)
    c_blk = jnp.swapaxes(ct_blk, 2, 3)

    xn1 = _to_slab(xn1)
    uz = _mm(xn1, w_in, name="s5_in")
    ys, y1b, *rest = _ssm_fwd(uz, b_blk, c_blk, par, d_skip, carry=gather_rest)
    g_wglu, g_wout, g_kvw, g_fwin = fill_own(rest, own_shards[1:5])
    w_glu, w_out = rows(g_wglu), rows(g_wout)
    kvw_full = cols(g_kvw)
    w_kv = kvw_full[:, :2 * fw]
    w_f = jnp.pad(kvw_full[:, 2 * fw:], ((0, 0), (0, LANES - heads)))
    fw_in = g_fwin
    glu_a = _mm(y1b, w_glu, name="s5_glu")

    def gate_fn(y, a, z, b):
        return (_gelu(y) * _sigmoid(a + b) * _silu(z),)

    y3b = _rows(gate_fn, [_full(ys), _full(glu_a), (uz, width, 1)], [(width, BF16)], consts=[b_glu], name="s5_gate")[0]
    o1 = _from_slab(_mm(y3b, w_out, name="s5_out"))

    def mid_fn(h, o, gp, gk, gq):
        h1 = h + o * _rstd(o) * gp
        r = _rstd(h1)
        return h1, h1 * r * gk, h1 * r * gq

    h1, xk, xn2 = _rows(mid_fn, [_full(h0), _full(o1)], [(dm, F32), (dm, BF16), (dm, BF16)],
                        consts=[g_post0, g_kv, g_pre1], name="mid_norms")

    kv, g_fwout = _mm(xk, w_kv, out_dtype=BF16, name="kv_proj", carry=gather_last)
    fw_out = rows(fill_own([g_fwout], own_shards[5:6])[0])
    fl = _mm(xk, w_f, name="f_proj")
    qz = _mm(xn2, fw_in, name="fox_in")
    cum = _cum_fwd(fl, b_f)
    t_att = min(ATT_TILE, seq)
    cum_t = cum[:, :heads].T
    ck = cum_t.reshape(heads, seq // t_att, 1, t_att)
    o, o2b, lse = _att_fwd(qz, kv, ck)
    o3 = _mm(o2b, fw_out, name="fox_out")

    def loss_fn(h, o, t, g):
        r = _rstd(o)
        err = h + o * r * g - t
        dh = err * (1.0 / dm)
        do, dg = _rms_bwd(o, g, dh)
        part = 0.5 * jnp.sum(jnp.mean(err * err, axis=-1, keepdims=True), axis=0, keepdims=True)
        return dh, do, jnp.broadcast_to(part, (1, LANES)), _colsum(dg)

    dh2, do3, loss_part, dg_post1 = _rows(loss_fn, [_full(h1), _full(o3), _full(target)], [(dm, F32), (dm, BF16)],
                                          consts=[g_post1], accs=[(1, LANES), (1, dm)], name="loss_head")
    loss = lax.psum(loss_part[0, 0], MESH_AXES)

    do2 = _mm(do3, fw_out, tb=True, name="fox_out_dx")
    dw_fout = _mm(o2b, do3, ta=True, name="fox_out_dw")

    def fox_gate_bwd(d, a, z):
        return d * _silu(z), d * a * _dsilu(z)

    do, dz2 = _rows(fox_gate_bwd, [_full(do2), _full(o), (qz, fw, 1)], [(fw, F32), (fw, BF16)], name="fox_gate_bwd")
    dq, delta = _att_bwd_q(qz, kv, do, o, lse, ck)
    dk, dv, dck = _att_bwd_kv(qz, kv, do, lse, delta, ck)
    dcum = jnp.pad(dck.reshape(heads, seq).T, ((0, 0), (0, LANES - heads)))
    dfl, db_f = _cum_bwd(dcum, fl, b_f)
    dqz = (dq, dz2)
    dkv = (dk, dv)
    dxn2 = _mm(dqz, fw_in, tb=True, name="fox_in_dx")
    dw_fin = _mm(xn2, dqz, ta=True, out_split=N_CHIPS, name="fox_in_dw")
    dxk_f = _mm(dfl, w_f, tb=True, name="f_proj_dx")
    dxk = _mm(dkv, w_kv, tb=True, add=dxk_f, name="kv_proj_dx")
    dw_kv = _mm(xk, dkv, ta=True, name="kv_proj_dw")
    dw_f = _mm(xk, dfl, ta=True, name="f_proj_dw")

    def mid_bwd(d2, h, dq_, dk_, o, gq, gk, gp):
        dxa, dga = _rms_bwd(h, gq, dq_)
        dxb, dgb = _rms_bwd(h, gk, dk_)
        dh = d2 + dxa + dxb
        do_, dgp = _rms_bwd(o, gp, dh)
        return dh, do_, _colsum(dga), _colsum(dgb), _colsum(dgp)

    to_cols = lambda g: jnp.moveaxis(g.reshape(g.shape[0], N_CHIPS, -1), 1, 0)
    to_rows = lambda g: g.reshape(N_CHIPS, -1, g.shape[1])
    dw_kv_full = jnp.concatenate([dw_kv, dw_f[:, :heads]], axis=1)
    early_grads = [to_cols(dw_kv_full), dw_fin, to_rows(dw_fout)]
    dh1, do1, dg_pre1, dg_kv, dg_post0, *early_theirs = _rows(
        mid_bwd, [_full(dh2), _full(h1), _full(dxn2), _full(dxk), _full(o1)], [(dm, F32), (dm, BF16)],
        consts=[g_pre1, g_kv, g_post0], accs=[(1, dm)] * 3, name="mid_norms_bwd", carry=_swap_halves(early_grads))
    early_sums = [_pair_sum_bf16(g, t, pos, f"grad_pair_sum_{3 + i}") for i, (g, t) in enumerate(zip(early_grads, early_theirs))]

    do1 = _to_slab(do1)
    dy3 = _mm(do1, w_out, tb=True, name="s5_out_dx")
    dw_out = _mm(y3b, do1, ta=True, name="s5_out_dw")

    def gate_bwd(d3, y, a, z, b):
        y1 = _gelu(y)
        gate = _sigmoid(a + b)
        dy2 = d3 * _silu(z)
        da = dy2 * y1 * gate * (1.0 - gate)
        return dy2 * gate, da, d3 * (y1 * gate) * _dsilu(z), _colsum(da)

    dy1_direct, da, dz, db_glu = _rows(gate_bwd, [_full(dy3), _full(ys), _full(glu_a), (uz, width, 1)],
                                       [(width, F32), (width, BF16), (width, BF16)], consts=[b_glu],
                                       accs=[(1, width)], name="s5_gate_bwd")
    dy1 = _mm(da, w_glu, tb=True, add=dy1_direct, name="s5_glu_dx")
    dw_glu = _mm(y1b, da, ta=True, name="s5_glu_dw")
    mid_grads = [to_rows(dw_glu), to_rows(dw_out)]
    du, dbt_blk, dct_blk, dpar, dd, *carried = _ssm_bwd(
        uz, dy1, ys, b_blk, ct_blk, par, d_skip, carry=_together(_scatter_to_owner(early_sums), _swap_halves(mid_grads)))
    early_recv, mid_theirs = carried[:3], carried[3:]
    mid_sums = [_pair_sum_bf16(g, t, pos, f"grad_pair_sum_{1 + i}") for i, (g, t) in enumerate(zip(mid_grads, mid_theirs))]
    duz = (du, dz)
    dw_in, *mid_recv = _mm(xn1, duz, ta=True, out_split=N_CHIPS, name="s5_in_dw", carry=_scatter_to_owner(mid_sums))
    late_grads = [dw_in]
    dxn1, *late_theirs = _mm(duz, w_in, tb=True, name="s5_in_dx", carry=_swap_halves(late_grads))
    dxn1 = _from_slab(dxn1)
    late_sums = [_pair_sum_bf16(dw_in, late_theirs[0], pos, "grad_pair_sum_0")]

    def first_bwd(d1, h, dxn, g):
        dx, dg = _rms_bwd(h, g, dxn)
        return d1 + dx, _colsum(dg)

    grad_x, dg_pre0, *late_recv = _rows(first_bwd, [_full(dh1), _full(h0), _full(dxn1)], [(dm, F32)], consts=[g_pre0],
                                        accs=[(1, dm)], name="norm_pre0_bwd", carry=_scatter_to_owner(late_sums))

    dpar_g = [dpar[:, i, :].reshape(groups, S5_STATE) for i in range(4)]
    da_re, da_im, dlog_dt = disc_vjp(tuple(dpar_g))
    db_re = jnp.swapaxes(_block_diag_extract(dbt_blk[:, 0]), 1, 2)
    db_im = jnp.swapaxes(_block_diag_extract(dbt_blk[:, 1]), 1, 2)
    dc_re = _block_diag_extract(dct_blk[:, 0])
    dc_im = _block_diag_extract(dct_blk[:, 1])

    small_local = [jnp.concatenate([dg_pre0, dg_pre1]), jnp.concatenate([dg_post0, dg_post1]),
                   da_re[None], da_im[None], dlog_dt[None], db_re[None], db_im[None], dc_re[None], dc_im[None],
                   dd, db_glu, dg_kv.reshape(dm), db_f[0, :heads]]
    small_sum = _unpack(_all_reduce_small(_pack(small_local)), small_local)
    (g_norm_pre, g_norm_post, g_a_re, g_a_im, g_log_dt, g_b_re, g_b_im, g_c_re, g_c_im, g_d_full, g_bglu_full,
     g_kv_norm, g_b_f) = small_sum
    shard = width // N_CHIPS
    g_d_own = lax.dynamic_slice(g_d_full, (0, chip * shard), (1, shard))
    g_bglu_own = lax.dynamic_slice(g_bglu_full, (0, chip * shard), (1, shard))

    big_grads = late_grads + mid_grads + early_grads
    theirs = list(late_theirs) + list(mid_theirs) + list(early_theirs)
    received = list(late_recv) + list(mid_recv) + list(early_recv)
    halves = [_chip_sum(g, t, r, pos, f"grad_chip_sum_{i}") for i, (g, t, r) in enumerate(zip(big_grads, theirs, received))]
    g_win_s, g_wglu_s, g_wout_s, g_kvw_s, g_fwin_s, g_fwout_s = _join_halves(halves)

    big_w = big_shards
    big_g = [g_win_s, g_wglu_s, g_wout_s, g_kvw_s, g_fwin_s, g_fwout_s]
    big_m = [m_s5_w_in[0], m_s5_w_glu[0], m_s5_w_out[0], m_kv_w, m_fox_w_in[0], m_fox_w_out[0]]
    big_v = [v_s5_w_in[0], v_s5_w_glu[0], v_s5_w_out[0], v_kv_w, v_fox_w_in[0], v_fox_w_out[0]]
    big_upd = [_adamw(w, g, m, v, f"adamw_{i}") for i, (w, g, m, v) in enumerate(zip(big_w, big_g, big_m, big_v))]

    small_names = ["norm_pre", "norm_post", "s5_a_re", "s5_a_im", "s5_log_dt", "s5_b_re", "s5_b_im", "s5_c_re", "s5_c_im",
                   "s5_d", "s5_b_glu", "kv_norm", "kv_b_f"]
    small_w = [norm_pre, norm_post, s5_a_re, s5_a_im, s5_log_dt, s5_b_re, s5_b_im, s5_c_re, s5_c_im, s5_d, s5_b_glu, kv_norm, kv_b_f]
    small_m = [m_norm_pre, m_norm_post, m_s5_a_re, m_s5_a_im, m_s5_log_dt, m_s5_b_re, m_s5_b_im, m_s5_c_re, m_s5_c_im, m_s5_d, m_s5_b_glu, m_kv_norm, m_kv_b_f]
    small_v = [v_norm_pre, v_norm_post, v_s5_a_re, v_s5_a_im, v_s5_log_dt, v_s5_b_re, v_s5_b_im, v_s5_c_re, v_s5_c_im, v_s5_d, v_s5_b_glu, v_kv_norm, v_kv_b_f]
    small_g = [g_norm_pre, g_norm_post, g_a_re, g_a_im, g_log_dt, g_b_re, g_b_im, g_c_re, g_c_im, g_d_own, g_bglu_own, g_kv_norm, g_b_f]
    small_g = [g.reshape(w.shape) for g, w in zip(small_g, small_w)]
    sd, sm, sv = _adamw(_pack(small_w), _pack(small_g), _pack(small_m), _pack(small_v, fill=1.0), "adamw_small")
    small_delta, small_newm, small_newv = _unpack(sd, small_w), _unpack(sm, small_w), _unpack(sv, small_w)

    order = ["norm_pre", "norm_post", "s5_w_in", "s5_a_re", "s5_a_im", "s5_log_dt", "s5_b_re", "s5_b_im", "s5_c_re", "s5_c_im",
             "s5_d", "s5_w_glu", "s5_b_glu", "s5_w_out", "kv_norm", "kv_w", "kv_b_f", "fox_w_in", "fox_w_out"]
    big_names = ["s5_w_in", "s5_w_glu", "s5_w_out", "kv_w", "fox_w_in", "fox_w_out"]
    big_like = [s5_w_in, s5_w_glu, s5_w_out, kv_w, fox_w_in, fox_w_out]
    grads, deltas, new_m, new_v = {}, {}, {}, {}
    for i, n in enumerate(big_names):
        shp = big_like[i].shape
        grads[n] = big_g[i].reshape(shp)
        deltas[n], new_m[n], new_v[n] = (a.reshape(shp) for a in big_upd[i])
    for i, n in enumerate(small_names):
        grads[n], deltas[n], new_m[n], new_v[n] = small_g[i], small_delta[i], small_newm[i], small_newv[i]

    return (loss, grad_x[None], *[grads[n] for n in order], *[deltas[n] for n in order],
            *[new_m[n] for n in order], *[new_v[n] for n in order])
```

```python
import functools
import math
from typing import Callable, NamedTuple

import jax
import jax.numpy as jnp
from jax import lax
from jax.experimental import pallas as pl
from jax.experimental.pallas import tpu as pltpu

F32 = jnp.float32
BF16 = jnp.bfloat16

D_MODEL = 2048
SEQ = 4096
S5_GROUP = 16
S5_STATE = 64
HEAD_DIM = 128
RMS_EPS = 1e-6
NEG_INF = -1e30
ADAM_LR = 0.001
ADAM_B1 = 0.9
ADAM_B2 = 0.999
ADAM_EPS = 1e-08
ADAM_WD = 0.01
ADAM_STEP = 10

LANES = 128
SUBLANES = 8
VMEM_LIMIT = 56 * 1024 * 1024
N_CHIPS = 4
MESH_AXES = ("x", "y", "c")
MESH_ID = pl.DeviceIdType.MESH

SSM_CH = 128
ROW_TILE = 256
SCAN_ROWS = 512
SCAN_UNROLL = 4
ATT_TILE = 512
ATT_UNROLL = 2
CUM_TILE = 512


def _pcall(body, **kw):
    return pl.pallas_call(body, **kw)


def _params(sem=None):
    if sem is None:
        return pltpu.CompilerParams(vmem_limit_bytes=VMEM_LIMIT)
    return pltpu.CompilerParams(vmem_limit_bytes=VMEM_LIMIT, dimension_semantics=sem)


def _sigmoid(x):
    return 1.0 / (1.0 + jnp.exp(-x))


def _silu(z):
    return z * _sigmoid(z)


def _dsilu(z):
    s = _sigmoid(z)
    return s * (1.0 + z * (1.0 - s))


_GELU_C = math.sqrt(2.0 / math.pi)


def _gelu(x):
    return 0.5 * x * (1.0 + jnp.tanh(_GELU_C * (x + 0.044715 * x * x * x)))


def _dgelu(x):
    t = jnp.tanh(_GELU_C * (x + 0.044715 * x * x * x))
    return 0.5 * (1.0 + t) + 0.5 * x * (1.0 - t * t) * _GELU_C * (1.0 + 3.0 * 0.044715 * x * x)


def _rstd(x):
    return lax.rsqrt(jnp.mean(x * x, axis=-1, keepdims=True) + RMS_EPS)


def _rms_bwd(x, g, dy):
    r = _rstd(x)
    dyg = dy * g
    dx = r * dyg - x * (r * r * r) * jnp.mean(dyg * x, axis=-1, keepdims=True)
    return dx, dy * (x * r)


def _colsum(v):
    return jnp.sum(v, axis=0, keepdims=True)


ANY = pl.BlockSpec(memory_space=pl.ANY)


class _Exchange(NamedTuple):
    arrays: tuple
    out_shapes: tuple
    scratch: tuple
    start: Callable
    finish: Callable


def _together(*exs):
    def parts(seq, field):
        out, off = [], 0
        for e in exs:
            n = len(getattr(e, field))
            out.append(seq[off:off + n])
            off += n
        return out

    def start(ins, outs, sems):
        for e, i, o, s in zip(exs, parts(ins, "arrays"), parts(outs, "out_shapes"), parts(sems, "scratch")):
            e.start(i, o, s)

    def finish(ins, outs, sems):
        for e, i, o, s in zip(exs, parts(ins, "arrays"), parts(outs, "out_shapes"), parts(sems, "scratch")):
            e.finish(i, o, s)

    return _Exchange(arrays=sum((tuple(e.arrays) for e in exs), ()), out_shapes=sum((tuple(e.out_shapes) for e in exs), ()),
                     scratch=sum((tuple(e.scratch) for e in exs), ()), start=start, finish=finish)


def _exchange_call(ex, name):
    n_in, n_out = len(ex.arrays), len(ex.out_shapes)

    def body(*refs):
        ins, outs, sems = refs[:n_in], refs[n_in:n_in + n_out], refs[n_in + n_out:]
        ex.start(ins, outs, sems)
        ex.finish(ins, outs, sems)

    return _pcall(body, name=name, in_specs=[ANY] * n_in, out_specs=[ANY] * n_out, out_shape=list(ex.out_shapes),
                  scratch_shapes=list(ex.scratch), compiler_params=pltpu.CompilerParams(has_side_effects=True))(*ex.arrays)


def _carry(ex, refs, n_fixed_in, n_fixed_out):
    if ex is None:
        return refs, lambda cond: None, lambda cond: None
    n_in, n_out, n_sem = len(ex.arrays), len(ex.out_shapes), len(ex.scratch)
    fixed_in = refs[:n_fixed_in]
    ex_in = refs[n_fixed_in:n_fixed_in + n_in]
    rest = refs[n_fixed_in + n_in:]
    fixed_out = rest[:n_fixed_out]
    ex_out = rest[n_fixed_out:n_fixed_out + n_out]
    scratch = rest[n_fixed_out + n_out:]
    sems = scratch[len(scratch) - n_sem:]

    def start_when(cond):
        pl.when(cond)(lambda: ex.start(ex_in, ex_out, sems))

    def finish_when(cond):
        pl.when(cond)(lambda: ex.finish(ex_in, ex_out, sems))

    return tuple(fixed_in) + tuple(fixed_out) + tuple(scratch[:len(scratch) - n_sem]), start_when, finish_when


def _carry_specs(ex):
    if ex is None:
        return (), [], [], [], []
    return ex.arrays, [ANY] * len(ex.arrays), [ANY] * len(ex.out_shapes), list(ex.out_shapes), list(ex.scratch)


def _mm(a, b, *, ta=False, tb=False, out_dtype=F32, add=None, out_split=1, tm=1024, tn=1024, tk=2048, name, carry=None):
    def describe(op):
        if isinstance(op, (tuple, list)):
            assert all(p.ndim == 2 and p.shape == op[0].shape for p in op)
            return list(op), op[0].shape[0], op[0].shape[1], False
        if op.ndim == 3:
            return [op], op.shape[1], op.shape[2], True
        return [op], op.shape[0], op.shape[1], False

    a_parts, a_rows, a_pc, a_stack = describe(a)
    b_parts, b_rows, b_pc, b_stack = describe(b)
    a_cols = a_pc * (a.shape[0] if a_stack else len(a_parts))
    b_cols = b_pc * (b.shape[0] if b_stack else len(b_parts))
    k_dim, m_dim = (a_rows, a_cols) if ta else (a_cols, a_rows)
    n_dim, kb = (b_rows, b_cols) if tb else (b_cols, b_rows)
    assert kb == k_dim, (k_dim, kb)
    tm = min(tm, a_pc) if ta else min(tm, m_dim)
    tk = min(tk, k_dim, k_dim if ta else a_pc, b_pc if tb else k_dim)
    tn = min(tn, n_dim // out_split, n_dim if tb else b_pc)
    a_ct, b_ct = (tm if ta else tk), (tk if tb else tn)
    assert m_dim % tm == 0 and n_dim % tn == 0 and k_dim % tk == 0 and a_pc % a_ct == 0 and b_pc % b_ct == 0
    assert (n_dim // out_split) % tn == 0
    nk = k_dim // tk
    dims = (((0 if ta else 1,), (1 if tb else 0,)), ((), ()))
    n_a, n_b = len(a_parts), len(b_parts)
    assert n_a == 1 or n_b == 1

    def operand_specs(parts, stack, rows_t, cols_t, per, row_of, col_of):
        specs = []
        for p in range(len(parts)):
            def col(i, j, k, p=p):
                return jnp.clip(col_of(i, j, k) - p * per, 0, per - 1) if len(parts) > 1 else col_of(i, j, k)
            if stack:
                specs.append(pl.BlockSpec((None, rows_t, cols_t),
                                          lambda i, j, k, col=col: (col(i, j, k) // per, row_of(i, j, k), col(i, j, k) % per)))
            else:
                specs.append(pl.BlockSpec((rows_t, cols_t), lambda i, j, k, col=col: (row_of(i, j, k), col(i, j, k))))
        return specs

    if ta:
        a_specs = operand_specs(a_parts, a_stack, tk, tm, a_pc // tm, lambda i, j, k: k, lambda i, j, k: i)
    else:
        a_specs = operand_specs(a_parts, a_stack, tm, tk, a_pc // tk, lambda i, j, k: i, lambda i, j, k: k)
    if tb:
        b_specs = operand_specs(b_parts, b_stack, tn, tk, b_pc // tk, lambda i, j, k: j, lambda i, j, k: k)
    else:
        b_specs = operand_specs(b_parts, b_stack, tk, tn, b_pc // tn, lambda i, j, k: k, lambda i, j, k: j)

    n_fixed_in = n_a + n_b + (1 if add is not None else 0)
    grid = (m_dim // tm, n_dim // tn, nk)
    ex_args, ex_in_specs, ex_out_specs, ex_out_shapes, ex_scratch = _carry_specs(carry)

    def body(*refs):
        refs, start_when, finish_when = _carry(carry, refs, n_fixed_in, 1)
        i, j, k = pl.program_id(0), pl.program_id(1), pl.program_id(2)
        step = (i * grid[1] + j) * grid[2] + k
        start_when(step == 0)
        compute(*refs)
        finish_when(step == grid[0] * grid[1] * grid[2] - 1)

    def compute(*refs):
        a_refs, b_refs = refs[:n_a], refs[n_a:n_a + n_b]
        rest = refs[n_a + n_b:]
        c_ref = rest[0] if add is not None else None
        o_ref = rest[1] if add is not None else rest[0]
        acc = None if nk == 1 else rest[-1]
        i, j, k = pl.program_id(0), pl.program_id(1), pl.program_id(2)

        def finish(res):
            if add is not None:
                res = res + c_ref[...]
            o_ref[...] = res.astype(out_dtype)

        def accumulate(a_ref, b_ref):
            prod = lax.dot_general(a_ref[...].astype(BF16), b_ref[...].astype(BF16), dims,
                                   preferred_element_type=F32)
            if nk == 1:
                finish(prod)
                return

            @pl.when(k == 0)
            def _():
                acc[...] = prod

            @pl.when(jnp.logical_and(k > 0, k < nk - 1))
            def _():
                acc[...] += prod

            @pl.when(k == nk - 1)
            def _():
                finish(acc[...] + prod)

        if n_a == 1 and n_b == 1:
            accumulate(a_refs[0], b_refs[0])
        else:
            many, block, per = (a_refs, (i if ta else k), a_pc // a_ct) if n_a > 1 else (b_refs, (k if tb else j), b_pc // b_ct)
            for p, ref in enumerate(many):
                @pl.when(block // per == p)
                def _(ref=ref):
                    accumulate(ref, b_refs[0]) if n_a > 1 else accumulate(a_refs[0], ref)

    per_out = n_dim // out_split // tn
    if out_split > 1:
        o_spec = pl.BlockSpec((None, tm, tn), lambda i, j, k: (j // per_out, i, j % per_out))
        out_shape = jax.ShapeDtypeStruct((out_split, m_dim, n_dim // out_split), out_dtype)
    else:
        o_spec = pl.BlockSpec((tm, tn), lambda i, j, k: (i, j))
        out_shape = jax.ShapeDtypeStruct((m_dim, n_dim), out_dtype)
    in_specs = a_specs + b_specs + ([pl.BlockSpec((tm, tn), lambda i, j, k: (i, j))] if add is not None else [])
    args = tuple(a_parts) + tuple(b_parts) + ((add,) if add is not None else ())
    acc_scratch = [pltpu.VMEM((tm, tn), F32)] if nk > 1 else []
    if carry is None:
        return _pcall(
            body, name=name, grid=grid, in_specs=in_specs, out_specs=o_spec, out_shape=out_shape,
            scratch_shapes=acc_scratch, compiler_params=_params(("parallel", "parallel", "arbitrary")),
        )(*args)
    return _pcall(
        body, name=name, grid=grid, in_specs=in_specs + ex_in_specs, out_specs=[o_spec] + ex_out_specs,
        out_shape=[out_shape] + ex_out_shapes, scratch_shapes=acc_scratch + ex_scratch,
        compiler_params=pltpu.CompilerParams(vmem_limit_bytes=VMEM_LIMIT, has_side_effects=True,
                                             dimension_semantics=("arbitrary", "arbitrary", "arbitrary")),
    )(*args, *ex_args)


def _rows(fn, ins, outs, *, name, consts=(), accs=(), carry=None):
    n_rows = ins[0][0].shape[0]
    tr = min(ROW_TILE, n_rows)
    assert n_rows % tr == 0
    n_in, n_c, n_out = len(ins), len(consts), len(outs)
    n_steps = n_rows // tr
    ex_args, ex_in_specs, ex_out_specs, ex_out_shapes, ex_scratch = _carry_specs(carry)

    def body(*refs):
        refs, start_when, finish_when = _carry(carry, refs, n_in + n_c, n_out + len(accs))
        start_when(pl.program_id(0) == 0)
        _compute(*refs)
        finish_when(pl.program_id(0) == n_steps - 1)

    def _compute(*refs):
        vals = [r[...] for r in refs[:n_in + n_c]]
        res = fn(*vals)
        res = res if isinstance(res, (tuple, list)) else (res,)
        o_refs = refs[n_in + n_c:]
        for r, v in zip(o_refs[:n_out], res[:n_out]):
            r[...] = v.astype(r.dtype)
        if accs:
            first = pl.program_id(0) == 0
            for r, v in zip(o_refs[n_out:], res[n_out:]):
                @pl.when(first)
                def _(r=r, v=v):
                    r[...] = v

                @pl.when(jnp.logical_not(first))
                def _(r=r, v=v):
                    r[...] += v

    in_specs = [pl.BlockSpec((tr, w), functools.partial(lambda i, cb: (i, cb), cb=cb)) for _, w, cb in ins]
    in_specs += [pl.BlockSpec(c.shape, functools.partial(lambda i, nd: (0,) * nd, nd=c.ndim)) for c in consts]
    out_specs = [pl.BlockSpec((tr, w), lambda i: (i, 0)) for w, _ in outs]
    out_specs += [pl.BlockSpec(s, lambda i: (0, 0)) for s in accs]
    out_shape = [jax.ShapeDtypeStruct((n_rows, w), dt) for w, dt in outs]
    out_shape += [jax.ShapeDtypeStruct(s, F32) for s in accs]
    sequential = bool(accs) or carry is not None
    params = _params(("arbitrary",) if sequential else ("parallel",))
    if carry is not None:
        params = pltpu.CompilerParams(vmem_limit_bytes=VMEM_LIMIT, dimension_semantics=("arbitrary",), has_side_effects=True)
    return _pcall(
        body, name=name, grid=(n_steps,), in_specs=in_specs + ex_in_specs, out_specs=out_specs + ex_out_specs,
        out_shape=out_shape + ex_out_shapes, scratch_shapes=ex_scratch, compiler_params=params,
    )(*[a for a, _, _ in ins], *consts, *ex_args)


def _full(a):
    return (a, a.shape[1], 0)


def _cmul(ar, ai, br, bi):
    return ar * br - ai * bi, ar * bi + ai * br


def _seg_scans(scans, seg):
    assert seg & (seg - 1) == 0
    chains = []
    for re_ref, im_ref, a_re, a_im, reverse in scans:
        for k in range(len(a_re)):
            chains.append((re_ref, im_ref, k, jnp.broadcast_to(a_re[k], (SUBLANES, LANES)),
                           jnp.broadcast_to(a_im[k], (SUBLANES, LANES)), reverse))

    def slab(i, reverse):
        j = seg - 1 - i if reverse else i
        return pl.ds(pl.multiple_of(j * SUBLANES, SUBLANES), SUBLANES)

    def local(i, carry):
        out = []
        for n, (re_ref, im_ref, k, ar, ai, reverse) in enumerate(chains):
            hr, hi = _cmul(ar, ai, carry[2 * n], carry[2 * n + 1])
            hr = hr + re_ref[k, slab(i, reverse), :]
            hi = hi + im_ref[k, slab(i, reverse), :]
            re_ref[k, slab(i, reverse), :] = hr
            im_ref[k, slab(i, reverse), :] = hi
            out += [hr, hi]
        return tuple(out)

    zero = jnp.zeros((SUBLANES, LANES), F32)
    end = lax.fori_loop(0, seg, local, (zero,) * (2 * len(chains)))

    row = lax.broadcasted_iota(jnp.int32, (SUBLANES, LANES), 0)
    enter = []
    for n, (_, _, _, ar, ai, reverse) in enumerate(chains):
        edge = SUBLANES - 1 if reverse else 0
        shift = SUBLANES - 1 if reverse else 1
        pr, pi = ar, ai
        for _ in range(seg.bit_length() - 1):
            pr, pi = _cmul(pr, pi, pr, pi)
        tr_, ti_ = zero, zero
        for _ in range(SUBLANES - 1):
            vr, vi = _cmul(pr, pi, tr_, ti_)
            tr_ = jnp.where(row == edge, 0.0, pltpu.roll(vr + end[2 * n], shift, 0))
            ti_ = jnp.where(row == edge, 0.0, pltpu.roll(vi + end[2 * n + 1], shift, 0))
        enter += [tr_, ti_]

    def fix(i, carry):
        out = []
        for n, (re_ref, im_ref, k, ar, ai, reverse) in enumerate(chains):
            er, ei = _cmul(ar, ai, carry[2 * n], carry[2 * n + 1])
            re_ref[k, slab(i, reverse), :] += er
            im_ref[k, slab(i, reverse), :] += ei
            out += [er, ei]
        return tuple(out)

    lax.fori_loop(0, seg, fix, tuple(enter))
    per_scan, off = [], 0
    for scan in scans:
        per_scan.append(enter[off:off + 2 * len(scan[2])])
        off += 2 * len(scan[2])
    return per_scan


def _to_slab(a):
    s, w = a.shape
    return a.reshape(SUBLANES, s // SUBLANES, w).swapaxes(0, 1).reshape(s, w)


def _from_slab(a):
    s, w = a.shape
    return a.reshape(s // SUBLANES, SUBLANES, w).swapaxes(0, 1).reshape(s, w)


def _lane_blocks(v, n_k):
    return [v[:, k * LANES:(k + 1) * LANES] for k in range(n_k)]


def _gather_k(ref, rows, n_k):
    return jnp.concatenate([ref[k, rows, :] for k in range(n_k)], axis=1)


def _dot(a, b, dims=(((1,), (0,)), ((), ()))):
    return lax.dot_general(a.astype(BF16), b.astype(BF16), dims, preferred_element_type=F32)


_NT = (((1,), (1,)), ((), ()))
_TN = (((0,), (0,)), ((), ()))


def _ssm_fwd(uz, b_blk, c_blk, par, d_skip, carry=None):
    seq = uz.shape[0]
    width = d_skip.shape[1]
    ns = SSM_CH // S5_GROUP * S5_STATE
    n_k = ns // LANES
    seg = seq // SUBLANES
    tb = min(SCAN_ROWS, seq)
    n_cb = width // SSM_CH
    ex_args, ex_in_specs, ex_out_specs, ex_out_shapes, ex_scratch = _carry_specs(carry)

    def body(*refs):
        refs, start_when, finish_when = _carry(carry, refs, 5, 2)
        start_when(pl.program_id(0) == 0)
        compute(*refs)
        finish_when(pl.program_id(0) == n_cb - 1)

    def compute(u_ref, b_ref, c_ref, par_ref, d_ref, y_ref, g_ref, hre, him):
        coef_r, coef_i = par_ref[0, 2:3, :], par_ref[0, 3:4, :]
        for c0 in range(0, seq, tb):
            rows = pl.ds(c0, tb)
            ub = u_ref[rows, :]
            bur, bui = _dot(ub, b_ref[0, 0]), _dot(ub, b_ref[0, 1])
            xr, xi = coef_r * bur - coef_i * bui, coef_r * bui + coef_i * bur
            for k in range(n_k):
                hre[k, rows, :] = xr[:, k * LANES:(k + 1) * LANES]
                him[k, rows, :] = xi[:, k * LANES:(k + 1) * LANES]
        _seg_scans([(hre, him, _lane_blocks(par_ref[0, 0:1, :], n_k), _lane_blocks(par_ref[0, 1:2, :], n_k), False)], seg)
        for c0 in range(0, seq, tb):
            rows = pl.ds(c0, tb)
            y = _dot(_gather_k(hre, rows, n_k), c_ref[0, 0]) - _dot(_gather_k(him, rows, n_k), c_ref[0, 1])
            y = y + d_ref[...] * u_ref[rows, :]
            y_ref[rows, :] = y
            g_ref[rows, :] = _gelu(y).astype(BF16)

    blk = pl.BlockSpec((seq, SSM_CH), lambda i: (0, i))
    params = _params(("parallel",)) if carry is None else pltpu.CompilerParams(
        vmem_limit_bytes=VMEM_LIMIT, dimension_semantics=("arbitrary",), has_side_effects=True)
    return _pcall(
        body, name="ssm_fwd", grid=(n_cb,),
        in_specs=[blk,
                  pl.BlockSpec((1, 2, SSM_CH, ns), lambda i: (i, 0, 0, 0)),
                  pl.BlockSpec((1, 2, ns, SSM_CH), lambda i: (i, 0, 0, 0)),
                  pl.BlockSpec((1, 4, ns), lambda i: (i, 0, 0)),
                  pl.BlockSpec((1, SSM_CH), lambda i: (0, i))] + ex_in_specs,
        out_specs=[blk, blk] + ex_out_specs,
        out_shape=[jax.ShapeDtypeStruct((seq, width), F32), jax.ShapeDtypeStruct((seq, width), BF16)] + ex_out_shapes,
        scratch_shapes=[pltpu.VMEM((n_k, seq, LANES), F32), pltpu.VMEM((n_k, seq, LANES), F32)] + ex_scratch,
        compiler_params=params,
    )(uz, b_blk, c_blk, par, d_skip, *ex_args)


def _ssm_bwd(uz, dy1, ys, b_blk, ct_blk, par, d_skip, carry=None):
    seq = uz.shape[0]
    width = d_skip.shape[1]
    ns_all = SSM_CH // S5_GROUP * S5_STATE
    n_half = 2
    ns = ns_all // n_half
    n_k = ns // LANES
    seg = seq // SUBLANES
    tb = min(SCAN_ROWS, seq)
    n_cb = width // SSM_CH
    ex_args, ex_in_specs, ex_out_specs, ex_out_shapes, ex_scratch = _carry_specs(carry)

    def body(*refs):
        refs, start_when, finish_when = _carry(carry, refs, 7, 5)
        step = pl.program_id(0) * n_half + pl.program_id(1)
        start_when(step == 0)
        compute(*refs)
        finish_when(step == n_cb * n_half - 1)

    def compute(u_ref, dy_ref, ys_ref, b_ref, ct_ref, par_ref, d_ref,
                du_ref, dbt_ref, dct_ref, dpar_ref, dd_ref, hre, him, gre, gim):
        half = pl.program_id(1)
        a_r, a_i = par_ref[0, 0:1, :], par_ref[0, 1:2, :]
        coef_r, coef_i = par_ref[0, 2:3, :], par_ref[0, 3:4, :]

        def dys_of(rows):
            return dy_ref[rows, :] * _dgelu(ys_ref[rows, :])

        for c0 in range(0, seq, tb):
            rows = pl.ds(c0, tb)
            ub = u_ref[rows, :]
            bur, bui = _dot(ub, b_ref[0, 0]), _dot(ub, b_ref[0, 1])
            xr, xi = coef_r * bur - coef_i * bui, coef_r * bui + coef_i * bur
            dys = dys_of(rows)
            gr, gi = _dot(dys, ct_ref[0, 0]), -_dot(dys, ct_ref[0, 1])
            for k in range(n_k):
                lanes = slice(k * LANES, (k + 1) * LANES)
                hre[k, rows, :] = xr[:, lanes]
                him[k, rows, :] = xi[:, lanes]
                gre[k, rows, :] = gr[:, lanes]
                gim[k, rows, :] = gi[:, lanes]
        enter, _ = _seg_scans([(hre, him, _lane_blocks(a_r, n_k), _lane_blocks(a_i, n_k), False),
                               (gre, gim, _lane_blocks(a_r, n_k), _lane_blocks(-a_i, n_k), True)], seg)

        def corr(j, carry):
            acc, prev = carry
            acc_o, prev_o = [], []
            for k in range(n_k):
                sl = pl.ds(pl.multiple_of(j * SUBLANES, SUBLANES), SUBLANES)
                g_r, g_i = gre[k, sl, :], gim[k, sl, :]
                p_r, p_i = prev[2 * k], prev[2 * k + 1]
                acc_o += [acc[2 * k] + g_r * p_r + g_i * p_i, acc[2 * k + 1] + g_i * p_r - g_r * p_i]
                prev_o += [hre[k, sl, :], him[k, sl, :]]
            return tuple(acc_o), tuple(prev_o)

        zero = jnp.zeros((SUBLANES, LANES), F32)
        acc, _ = lax.fori_loop(0, seg, corr, ((zero,) * (2 * n_k), tuple(enter)), unroll=SCAN_UNROLL)
        da_r = jnp.concatenate([_colsum(acc[2 * k]) for k in range(n_k)], axis=1)
        da_i = jnp.concatenate([_colsum(acc[2 * k + 1]) for k in range(n_k)], axis=1)

        zeros_cn = jnp.zeros((SSM_CH, ns), F32)
        qt_r, qt_i, dct_r, dct_i = zeros_cn, zeros_cn, zeros_cn, zeros_cn
        dd = jnp.zeros((1, SSM_CH), F32)
        first = half == 0
        for c0 in range(0, seq, tb):
            rows = pl.ds(c0, tb)
            ub = u_ref[rows, :]
            dys = dys_of(rows)
            dct_r = dct_r + _dot(dys, _gather_k(hre, rows, n_k), _TN)
            dct_i = dct_i - _dot(dys, _gather_k(him, rows, n_k), _TN)
            g_r, g_i = _gather_k(gre, rows, n_k), _gather_k(gim, rows, n_k)
            qt_r = qt_r + _dot(ub, g_r, _TN)
            qt_i = qt_i + _dot(ub, g_i, _TN)
            dbu_r, dbu_i = coef_r * g_r + coef_i * g_i, coef_r * g_i - coef_i * g_r
            du = _dot(dbu_r, b_ref[0, 0], _NT) + _dot(dbu_i, b_ref[0, 1], _NT)
            dd = dd + _colsum(dys * ub)

            @pl.when(first)
            def _(du=du, dys=dys, rows=rows):
                du_ref[rows, :] = du + d_ref[...] * dys

            @pl.when(jnp.logical_not(first))
            def _(du=du, rows=rows):
                du_ref[rows, :] += du

        @pl.when(first)
        def _():
            dd_ref[...] = dd

        b_r, b_i = b_ref[0, 0], b_ref[0, 1]
        dbt_ref[0, 0] = coef_r * qt_r + coef_i * qt_i
        dbt_ref[0, 1] = coef_r * qt_i - coef_i * qt_r
        dct_ref[0, 0] = dct_r
        dct_ref[0, 1] = dct_i
        dpar_ref[0, 0:1, :] = da_r
        dpar_ref[0, 1:2, :] = da_i
        dpar_ref[0, 2:3, :] = _colsum(b_r * qt_r + b_i * qt_i)
        dpar_ref[0, 3:4, :] = _colsum(b_r * qt_i - b_i * qt_r)

    blk = lambda i, h: (0, i)
    params = _params(("parallel", "arbitrary")) if carry is None else pltpu.CompilerParams(
        vmem_limit_bytes=VMEM_LIMIT, dimension_semantics=("arbitrary", "arbitrary"), has_side_effects=True)
    return _pcall(
        body, name="ssm_bwd", grid=(n_cb, n_half),
        in_specs=[pl.BlockSpec((seq, SSM_CH), blk), pl.BlockSpec((seq, SSM_CH), blk), pl.BlockSpec((seq, SSM_CH), blk),
                  pl.BlockSpec((1, 2, SSM_CH, ns), lambda i, h: (i, 0, 0, h)),
                  pl.BlockSpec((1, 2, SSM_CH, ns), lambda i, h: (i, 0, 0, h)),
                  pl.BlockSpec((1, 4, ns), lambda i, h: (i, 0, h)),
                  pl.BlockSpec((1, SSM_CH), blk)] + ex_in_specs,
        out_specs=[pl.BlockSpec((seq, SSM_CH), blk),
                   pl.BlockSpec((1, 2, SSM_CH, ns), lambda i, h: (i, 0, 0, h)),
                   pl.BlockSpec((1, 2, SSM_CH, ns), lambda i, h: (i, 0, 0, h)),
                   pl.BlockSpec((1, 4, ns), lambda i, h: (i, 0, h)),
                   pl.BlockSpec((1, SSM_CH), blk)] + ex_out_specs,
        out_shape=[jax.ShapeDtypeStruct((seq, width), F32),
                   jax.ShapeDtypeStruct((n_cb, 2, SSM_CH, ns_all), F32),
                   jax.ShapeDtypeStruct((n_cb, 2, SSM_CH, ns_all), F32),
                   jax.ShapeDtypeStruct((n_cb, 4, ns_all), F32),
                   jax.ShapeDtypeStruct((1, width), F32)] + ex_out_shapes,
        scratch_shapes=[pltpu.VMEM((n_k, seq, LANES), F32) for _ in range(4)] + ex_scratch,
        compiler_params=params,
    )(uz, dy1, ys, b_blk, ct_blk, par, d_skip, *ex_args)


def _ssm_discretize(a_re, a_im, log_dt):
    dt = jnp.exp(log_dt)[:, None]
    mag = jnp.exp(a_re * dt)
    abar_re = mag * jnp.cos(a_im * dt)
    abar_im = mag * jnp.sin(a_im * dt)
    den = a_re * a_re + a_im * a_im
    nr = abar_re - 1.0
    coef_re = (nr * a_re + abar_im * a_im) / den
    coef_im = (abar_im * a_re - nr * a_im) / den
    return abar_re, abar_im, coef_re, coef_im


def _block_diag(w_gcp):
    gpb = SSM_CH // S5_GROUP
    n_cb = w_gcp.shape[0] // gpb
    w = w_gcp.reshape(n_cb, gpb, S5_GROUP, 1, S5_STATE)
    eye = jnp.eye(gpb, dtype=w.dtype)[None, :, None, :, None]
    return (w * eye).reshape(n_cb, SSM_CH, gpb * S5_STATE)


def _block_diag_extract(w_blk):
    gpb = SSM_CH // S5_GROUP
    n_cb = w_blk.shape[0]
    w = w_blk.reshape(n_cb, gpb, S5_GROUP, gpb, S5_STATE)
    w = jnp.moveaxis(jnp.diagonal(w, axis1=1, axis2=3), -1, 1)
    return w.reshape(n_cb * gpb, S5_GROUP, S5_STATE)


def _split3(x):
    hi = x.astype(BF16)
    mid = (x - hi.astype(F32)).astype(BF16)
    lo = (x - hi.astype(F32) - mid.astype(F32)).astype(BF16)
    return hi, mid, lo


def _tri_sum(tri, x):
    hi, mid, lo = _split3(x)
    return (jnp.dot(tri, hi, preferred_element_type=F32) + jnp.dot(tri, mid, preferred_element_type=F32)
            + jnp.dot(tri, lo, preferred_element_type=F32))


def _log_sigmoid(x):
    return jnp.minimum(x, 0.0) - jnp.log(1.0 + jnp.exp(-jnp.abs(x)))


def _cum_fwd(fl, b_f):
    seq = fl.shape[0]
    t = min(CUM_TILE, seq)

    def body(fl_ref, b_ref, o_ref, carry):
        @pl.when(pl.program_id(0) == 0)
        def _():
            carry[...] = jnp.zeros_like(carry)

        r = lax.broadcasted_iota(jnp.int32, (t, t), 0)
        c = lax.broadcasted_iota(jnp.int32, (t, t), 1)
        tri = (c <= r).astype(BF16)
        cum = _tri_sum(tri, _log_sigmoid(fl_ref[...] + b_ref[...])) + carry[...]
        o_ref[...] = cum
        carry[...] = cum[t - 1:t, :]

    return _pcall(
        body, name="cum_fwd", grid=(seq // t,),
        in_specs=[pl.BlockSpec((t, LANES), lambda i: (i, 0)), pl.BlockSpec((1, LANES), lambda i: (0, 0))],
        out_specs=pl.BlockSpec((t, LANES), lambda i: (i, 0)),
        out_shape=jax.ShapeDtypeStruct((seq, LANES), F32),
        scratch_shapes=[pltpu.VMEM((1, LANES), F32)],
        compiler_params=_params(("arbitrary",)),
    )(fl, b_f)


def _cum_bwd(dcum, fl, b_f):
    seq = fl.shape[0]
    t = min(CUM_TILE, seq)
    nb = seq // t

    def body(dc_ref, fl_ref, b_ref, o_ref, db_ref, carry):
        @pl.when(pl.program_id(0) == 0)
        def _():
            carry[...] = jnp.zeros_like(carry)
            db_ref[...] = jnp.zeros_like(db_ref)

        r = lax.broadcasted_iota(jnp.int32, (t, t), 0)
        c = lax.broadcasted_iota(jnp.int32, (t, t), 1)
        tri = (c >= r).astype(BF16)
        rev = _tri_sum(tri, dc_ref[...]) + carry[...]
        carry[...] = rev[0:1, :]
        dfl = rev * _sigmoid(-(fl_ref[...] + b_ref[...]))
        o_ref[...] = dfl
        db_ref[...] += _colsum(dfl)

    return _pcall(
        body, name="cum_bwd", grid=(nb,),
        in_specs=[pl.BlockSpec((t, LANES), lambda i: (nb - 1 - i, 0)), pl.BlockSpec((t, LANES), lambda i: (nb - 1 - i, 0)),
                  pl.BlockSpec((1, LANES), lambda i: (0, 0))],
        out_specs=[pl.BlockSpec((t, LANES), lambda i: (nb - 1 - i, 0)), pl.BlockSpec((1, LANES), lambda i: (0, 0))],
        out_shape=[jax.ShapeDtypeStruct((seq, LANES), F32), jax.ShapeDtypeStruct((1, LANES), F32)],
        scratch_shapes=[pltpu.VMEM((1, LANES), F32)],
        compiler_params=_params(("arbitrary",)),
    )(dcum, fl, b_f)


def _att_scores(q, kb, ck, row0, col0, masked):
    s = _dot(q, kb, _NT) - ck
    if masked:
        rows = row0 + lax.broadcasted_iota(jnp.int32, s.shape, 0)
        cols = col0 + lax.broadcasted_iota(jnp.int32, s.shape, 1)
        s = jnp.where(cols <= rows, s, NEG_INF)
    return s


def _pairwise_loop(lo, hi, step_fn, init):
    n = hi - lo
    w = ATT_UNROLL

    def several(p, carry):
        for u in range(w):
            carry = step_fn(lo + w * p + u, carry)
        return carry

    carry = lax.fori_loop(0, n // w, several, init)
    return lax.fori_loop(lo + (n // w) * w, hi, step_fn, carry)


def _att_fwd(qz, kv, ck):
    seq = qz.shape[0]
    heads = ck.shape[0]
    t = min(ATT_TILE, seq)
    scale = HEAD_DIM ** -0.5

    def body(q_ref, z_ref, k_ref, v_ref, ck_ref, o_ref, ox_ref, og_ref, lse_ref):
        i = pl.program_id(1)
        q = (q_ref[...] * scale).astype(BF16)

        def block(j, carry, masked):
            m, l, acc, acc_lo = carry
            rows = pl.ds(pl.multiple_of(j * t, t), t)
            s = _att_scores(q, k_ref[rows, :], ck_ref[0, j], i * t, j * t, masked)
            m_new = jnp.maximum(m, jnp.max(s, axis=1, keepdims=True))
            p = jnp.exp(s - m_new)
            alpha = jnp.exp(m - m_new)
            p_hi = p.astype(BF16)
            p_lo = (p - p_hi.astype(F32)).astype(BF16)
            vb = v_ref[rows, :]
            return (m_new, alpha * l + jnp.sum(p, axis=1, keepdims=True), alpha * acc + _dot(p_hi, vb),
                    alpha * acc_lo + _dot(p_lo, vb))

        zeros = jnp.zeros((t, HEAD_DIM), F32)
        init = (jnp.full((t, 1), NEG_INF, F32), jnp.zeros((t, 1), F32), zeros, zeros)
        carry = _pairwise_loop(0, i, functools.partial(block, masked=False), init)
        m, l, acc, acc_lo = block(i, carry, True)
        o = acc / l
        o_ref[...] = o
        ox_ref[...] = (acc + acc_lo) / l
        og_ref[...] = (o * _silu(z_ref[...])).astype(BF16)
        lse_ref[0] = m + jnp.log(l)

    qblk = pl.BlockSpec((t, HEAD_DIM), lambda h, i: (i, h))
    return _pcall(
        body, name="att_fwd", grid=(heads, seq // t),
        in_specs=[qblk, pl.BlockSpec((t, HEAD_DIM), lambda h, i: (i, heads + h)),
                  pl.BlockSpec((seq, HEAD_DIM), lambda h, i: (0, h)),
                  pl.BlockSpec((seq, HEAD_DIM), lambda h, i: (0, heads + h)),
                  pl.BlockSpec((1, seq // t, 1, t), lambda h, i: (h, 0, 0, 0))],
        out_specs=[qblk, qblk, qblk, pl.BlockSpec((1, t, 1), lambda h, i: (h, i, 0))],
        out_shape=[jax.ShapeDtypeStruct((seq, heads * HEAD_DIM), F32), jax.ShapeDtypeStruct((seq, heads * HEAD_DIM), F32),
                   jax.ShapeDtypeStruct((seq, heads * HEAD_DIM), BF16), jax.ShapeDtypeStruct((heads, seq, 1), F32)],
        compiler_params=_params(("parallel", "parallel")),
    )(qz, qz, kv, kv, ck)


def _att_bwd_q(qz, kv, do, o, lse, ck):
    seq = qz.shape[0]
    heads = ck.shape[0]
    t = min(ATT_TILE, seq)
    scale = HEAD_DIM ** -0.5

    def body(q_ref, k_ref, v_ref, do_ref, o_ref, lse_ref, ck_ref, dq_ref, delta_ref):
        i = pl.program_id(1)
        q = (q_ref[...] * scale).astype(BF16)
        dob = do_ref[...].astype(BF16)
        delta = jnp.sum(do_ref[...] * o_ref[...], axis=1, keepdims=True)
        lse_v = lse_ref[0]

        def block(j, carry, masked):
            dq, pdp = carry
            rows = pl.ds(pl.multiple_of(j * t, t), t)
            kb = k_ref[rows, :]
            s = _att_scores(q, kb, ck_ref[0, j], i * t, j * t, masked)
            p = jnp.exp(s - lse_v)
            dp = _dot(dob, v_ref[rows, :], _NT)
            ds = p * (dp - delta)
            return dq + _dot(ds, kb), pdp + jnp.sum(p * dp, axis=1, keepdims=True)

        init = (jnp.zeros((t, HEAD_DIM), F32), jnp.zeros((t, 1), F32))
        carry = _pairwise_loop(0, i, functools.partial(block, masked=False), init)
        dq, pdp = block(i, carry, True)
        dq_ref[...] = (dq * scale).astype(dq_ref.dtype)
        delta_ref[0] = pdp

    qblk = pl.BlockSpec((t, HEAD_DIM), lambda h, i: (i, h))
    col = pl.BlockSpec((1, t, 1), lambda h, i: (h, i, 0))
    return _pcall(
        body, name="att_bwd_q", grid=(heads, seq // t),
        in_specs=[qblk, pl.BlockSpec((seq, HEAD_DIM), lambda h, i: (0, h)),
                  pl.BlockSpec((seq, HEAD_DIM), lambda h, i: (0, heads + h)), qblk, qblk, col,
                  pl.BlockSpec((1, seq // t, 1, t), lambda h, i: (h, 0, 0, 0))],
        out_specs=[qblk, col],
        out_shape=[jax.ShapeDtypeStruct((seq, heads * HEAD_DIM), BF16), jax.ShapeDtypeStruct((heads, seq, 1), F32)],
        compiler_params=_params(("parallel", "parallel")),
    )(qz, kv, kv, do, o, lse, ck)


def _att_bwd_kv(qz, kv, do, lse, delta, ck):
    seq = qz.shape[0]
    heads = ck.shape[0]
    t = min(ATT_TILE, seq)
    nq = seq // t
    scale = HEAD_DIM ** -0.5

    def body(q_ref, k_ref, v_ref, do_ref, lse_ref, delta_ref, ck_ref, dk_ref, dv_ref, dck_ref):
        j = pl.program_id(1)
        kb, vb = k_ref[...], v_ref[...]
        ckv = ck_ref[0, 0]

        def block(i, carry, masked):
            dk, dv, dck = carry
            rows = pl.ds(pl.multiple_of(i * t, t), t)
            qb = (q_ref[rows, :] * scale).astype(BF16)
            dob = do_ref[rows, :].astype(BF16)
            s = _att_scores(qb, kb, ckv, i * t, j * t, masked)
            p = jnp.exp(s - lse_ref[0, rows, :])
            ds = p * (_dot(dob, vb, _NT) - delta_ref[0, rows, :])
            return dk + _dot(ds, qb, _TN), dv + _dot(p, dob, _TN), dck - _colsum(ds)

        init = (jnp.zeros((t, HEAD_DIM), F32), jnp.zeros((t, HEAD_DIM), F32), jnp.zeros((1, t), F32))
        carry = block(j, init, True)
        dk, dv, dck = _pairwise_loop(j + 1, nq, functools.partial(block, masked=False), carry)
        dk_ref[...] = dk.astype(dk_ref.dtype)
        dv_ref[...] = dv.astype(dv_ref.dtype)
        dck_ref[0, 0] = dck

    head = pl.BlockSpec((seq, HEAD_DIM), lambda h, j: (0, h))
    col = pl.BlockSpec((1, seq, 1), lambda h, j: (h, 0, 0))
    kblk = pl.BlockSpec((t, HEAD_DIM), lambda h, j: (j, h))
    row = pl.BlockSpec((1, 1, 1, t), lambda h, j: (h, j, 0, 0))
    return _pcall(
        body, name="att_bwd_kv", grid=(heads, nq),
        in_specs=[head, kblk, pl.BlockSpec((t, HEAD_DIM), lambda h, j: (j, heads + h)), head, col, col, row],
        out_specs=[kblk, kblk, row],
        out_shape=[jax.ShapeDtypeStruct((seq, heads * HEAD_DIM), BF16), jax.ShapeDtypeStruct((seq, heads * HEAD_DIM), BF16),
                   jax.ShapeDtypeStruct((heads, nq, 1, t), F32)],
        compiler_params=_params(("parallel", "parallel")),
    )(qz, kv, kv, do, lse, delta, ck)


def _att_bwd(qz, kv, do, ox, lse, ck):
    seq = qz.shape[0]
    heads = ck.shape[0]
    t = min(ATT_TILE, seq)
    nq = seq // t
    scale = HEAD_DIM ** -0.5

    def body(q_ref, k_ref, v_ref, do_ref, ox_ref, lse_ref, ck_ref, dq_ref, dk_ref, dv_ref, dck_ref, dq_acc, delta):
        j = pl.program_id(1)

        @pl.when(j == 0)
        def _():
            for c0 in range(0, seq, t):
                rows = pl.ds(c0, t)
                delta[rows, :] = jnp.sum(do_ref[rows, :].astype(BF16).astype(F32) * ox_ref[rows, :], axis=1, keepdims=True)
                dq_acc[rows, :] = jnp.zeros((t, HEAD_DIM), F32)

        kb, vb = k_ref[...], v_ref[...]
        ckv = ck_ref[0, 0]

        def block(i, carry, masked):
            dk, dv, dck = carry
            rows = pl.ds(pl.multiple_of(i * t, t), t)
            qb = (q_ref[rows, :] * scale).astype(BF16)
            dob = do_ref[rows, :].astype(BF16)
            s = _att_scores(qb, kb, ckv, i * t, j * t, masked)
            p = jnp.exp(s - lse_ref[0, rows, :])
            ds = p * (_dot(dob, vb, _NT) - delta[rows, :])
            dq_acc[rows, :] += _dot(ds, kb)
            return dk + _dot(ds, qb, _TN), dv + _dot(p, dob, _TN), dck - _colsum(ds)

        init = (jnp.zeros((t, HEAD_DIM), F32), jnp.zeros((t, HEAD_DIM), F32), jnp.zeros((1, t), F32))
        carry = block(j, init, True)
        dk, dv, dck = _pairwise_loop(j + 1, nq, functools.partial(block, masked=False), carry)
        dk_ref[...] = dk.astype(dk_ref.dtype)
        dv_ref[...] = dv.astype(dv_ref.dtype)
        dck_ref[0, 0] = dck

        @pl.when(j == nq - 1)
        def _():
            for c0 in range(0, seq, t):
                rows = pl.ds(c0, t)
                dq_ref[rows, :] = (dq_acc[rows, :] * scale).astype(dq_ref.dtype)

    head = pl.BlockSpec((seq, HEAD_DIM), lambda h, j: (0, h))
    col = pl.BlockSpec((1, seq, 1), lambda h, j: (h, 0, 0))
    kblk = pl.BlockSpec((t, HEAD_DIM), lambda h, j: (j, h))
    row = pl.BlockSpec((1, 1, 1, t), lambda h, j: (h, j, 0, 0))
    return _pcall(
        body, name="att_bwd", grid=(heads, nq),
        in_specs=[head, kblk, pl.BlockSpec((t, HEAD_DIM), lambda h, j: (j, heads + h)), head, head, col, row],
        out_specs=[head, kblk, kblk, row],
        out_shape=[jax.ShapeDtypeStruct((seq, heads * HEAD_DIM), BF16)] * 3 + [jax.ShapeDtypeStruct((heads, nq, 1, t), F32)],
        scratch_shapes=[pltpu.VMEM((seq, HEAD_DIM), F32), pltpu.VMEM((seq, 1), F32)],
        compiler_params=_params(("parallel", "arbitrary")),
    )(qz, kv, kv, do, ox, lse, ck)


def _mesh_pos():
    return lax.axis_index("x"), lax.axis_index("y"), lax.axis_index("c")


def _other_chips(x, y):
    return [(1 - x, y), (x, 1 - y), (1 - x, 1 - y)]


def _all_gather_weights(big, small):
    nb, ns = len(big), len(small)
    n_remote = 3 * (nb + ns)

    def plan(ins, outs, sems):
        send_sems, recv_sems, fwd_send, fwd_recv = sems
        x, y, c = _mesh_pos()
        chips = _other_chips(x, y)
        slots = [2 * cx + cy for cx, cy in chips]

        def half(ref, hc):
            rh = ref.shape[-2] // 2
            return ref.at[pl.ds(hc * rh, rh), :]

        def remote(i, j, src_chip, from_in):
            if i < nb:
                src = half(ins[i], c) if from_in else half(outs[i].at[src_chip], c)
                dst = half(outs[i].at[src_chip], c)
            else:
                src = ins[i] if from_in else outs[i].at[src_chip]
                dst = outs[i].at[src_chip]
            k = 3 * i + j
            return pltpu.make_async_remote_copy(src_ref=src, dst_ref=dst, send_sem=send_sems.at[k],
                                                recv_sem=recv_sems.at[k], device_id=(*chips[j], c),
                                                device_id_type=MESH_ID)

        def forward(i, j, hc):
            part = half(outs[i].at[slots[j]], hc)
            k = 3 * i + j
            return pltpu.make_async_remote_copy(src_ref=part, dst_ref=part, send_sem=fwd_send.at[k],
                                                recv_sem=fwd_recv.at[k], device_id=(x, y, 1 - c),
                                                device_id_type=MESH_ID)

        return remote, forward, 2 * x + y, slots, c

    def start(ins, outs, sems):
        remote, _, me, _, _ = plan(ins, outs, sems)
        for i in range(nb + ns):
            for j in range(3):
                remote(i, j, me, True).start()

    def finish(ins, outs, sems):
        remote, forward, me, slots, c = plan(ins, outs, sems)
        for i in range(nb + ns):
            for j in range(3):
                remote(i, j, slots[j], False).wait_recv()
                if i < nb:
                    forward(i, j, c).start()
        for i in range(nb):
            for j in range(3):
                forward(i, j, 1 - c).wait_recv()
        for i in range(nb + ns):
            for j in range(3):
                remote(i, j, me, True).wait_send()
                if i < nb:
                    forward(i, j, c).wait_send()

    arrays = tuple(big) + tuple(small)
    return _Exchange(
        arrays=arrays,
        out_shapes=tuple(jax.ShapeDtypeStruct((N_CHIPS,) + a.shape, a.dtype) for a in arrays),
        scratch=(pltpu.SemaphoreType.DMA((n_remote,)), pltpu.SemaphoreType.DMA((n_remote,)),
                 pltpu.SemaphoreType.DMA((3 * max(nb, 1),)), pltpu.SemaphoreType.DMA((3 * max(nb, 1),))),
        start=start, finish=finish)


def _swap_halves(grads):
    n = len(grads)

    def copies(ins, outs, sems):
        x, y, c = _mesh_pos()
        cps = []
        for i in range(n):
            rh = ins[i].shape[1] // 2
            cps.append(pltpu.make_async_remote_copy(
                src_ref=ins[i].at[:, pl.ds((1 - c) * rh, rh), :], dst_ref=outs[i], send_sem=sems[0].at[i],
                recv_sem=sems[1].at[i], device_id=(x, y, 1 - c), device_id_type=MESH_ID))
        return cps

    def start(ins, outs, sems):
        for cp in copies(ins, outs, sems):
            cp.start()

    def finish(ins, outs, sems):
        for cp in copies(ins, outs, sems):
            cp.wait()

    return _Exchange(
        arrays=tuple(grads),
        out_shapes=tuple(jax.ShapeDtypeStruct((g.shape[0], g.shape[1] // 2, g.shape[2]), g.dtype) for g in grads),
        scratch=(pltpu.SemaphoreType.DMA((n,)), pltpu.SemaphoreType.DMA((n,))),
        start=start, finish=finish)


def _pair_sum_bf16(g, theirs, pos, name):
    n, rh, cdim = theirs.shape
    tr = min(ROW_TILE, rh)
    nb = rh // tr

    def body(pos_ref, g_ref, t_ref, o_ref):
        o_ref[...] = (g_ref[...] + t_ref[...]).astype(BF16)

    slot = lambda s, pos: (pos[0] + 1 + s) % n
    grid_spec = pltpu.PrefetchScalarGridSpec(
        num_scalar_prefetch=1, grid=(n - 1, nb),
        in_specs=[pl.BlockSpec((None, tr, cdim), lambda s, i, pos: (slot(s, pos), pos[1] * nb + i, 0)),
                  pl.BlockSpec((None, tr, cdim), lambda s, i, pos: (slot(s, pos), i, 0))],
        out_specs=pl.BlockSpec((None, tr, cdim), lambda s, i, pos: (slot(s, pos), i, 0)))
    return _pcall(body, name=name, grid_spec=grid_spec, out_shape=jax.ShapeDtypeStruct(theirs.shape, BF16),
                  compiler_params=_params(("parallel", "parallel")))(pos, g, theirs)


def _chip_sum(g, theirs, recv, pos, name):
    n, rh, cdim = theirs.shape
    tr = min(ROW_TILE, rh)
    nb = rh // tr

    def body(pos_ref, g_ref, t_ref, r0, r1, r2, o_ref):
        o_ref[...] = (((g_ref[...] + t_ref[...]) + r0[...]) + r1[...]) + r2[...]

    grid_spec = pltpu.PrefetchScalarGridSpec(
        num_scalar_prefetch=1, grid=(nb,),
        in_specs=[pl.BlockSpec((None, tr, cdim), lambda i, pos: (pos[0], pos[1] * nb + i, 0)),
                  pl.BlockSpec((None, tr, cdim), lambda i, pos: (pos[0], i, 0))]
        + [pl.BlockSpec((None, tr, cdim), functools.partial(lambda i, pos, j: (j, i, 0), j=j)) for j in range(3)],
        out_specs=pl.BlockSpec((tr, cdim), lambda i, pos: (pos[1] * nb + i, 0)))
    return _pcall(body, name=name, grid_spec=grid_spec, out_shape=jax.ShapeDtypeStruct((2 * rh, cdim), F32),
                  compiler_params=_params(("parallel",)))(pos, g, theirs, recv, recv, recv)


def _scatter_to_owner(parts):
    n = len(parts)

    def copies(ins, outs, sems):
        x, y, c = _mesh_pos()
        chips = _other_chips(x, y)
        cps = []
        for i in range(n):
            for j in range(3):
                k = 3 * i + j
                cps.append(pltpu.make_async_remote_copy(
                    src_ref=ins[i].at[2 * chips[j][0] + chips[j][1]], dst_ref=outs[i].at[j],
                    send_sem=sems[0].at[k], recv_sem=sems[1].at[k], device_id=(*chips[j], c),
                    device_id_type=MESH_ID))
        return cps

    def start(ins, outs, sems):
        for cp in copies(ins, outs, sems):
            cp.start()

    def finish(ins, outs, sems):
        for cp in copies(ins, outs, sems):
            cp.wait()

    return _Exchange(
        arrays=tuple(parts),
        out_shapes=tuple(jax.ShapeDtypeStruct((3,) + p.shape[1:], p.dtype) for p in parts),
        scratch=(pltpu.SemaphoreType.DMA((3 * n,)), pltpu.SemaphoreType.DMA((3 * n,))),
        start=start, finish=finish)


def _join_halves(shards):
    n = len(shards)

    def body(*refs):
        outs, send_sems, recv_sems = refs[n:2 * n], refs[2 * n], refs[2 * n + 1]
        x, y, c = _mesh_pos()

        def copy(i, hc):
            rh = outs[i].shape[0] // 2
            rows = outs[i].at[pl.ds(hc * rh, rh), :]
            return pltpu.make_async_remote_copy(src_ref=rows, dst_ref=rows, send_sem=send_sems.at[i],
                                                recv_sem=recv_sems.at[i], device_id=(x, y, 1 - c),
                                                device_id_type=MESH_ID)

        for i in range(n):
            copy(i, c).start()
        for i in range(n):
            copy(i, c).wait_send()
            copy(i, 1 - c).wait_recv()

    return _pcall(
        body, name="grad_join_halves", in_specs=[ANY] * n, out_specs=[ANY] * n,
        out_shape=[jax.ShapeDtypeStruct(s.shape, s.dtype) for s in shards],
        input_output_aliases={i: i for i in range(n)},
        scratch_shapes=[pltpu.SemaphoreType.DMA((n,)), pltpu.SemaphoreType.DMA((n,))],
        compiler_params=pltpu.CompilerParams(has_side_effects=True),
    )(*shards)


def _all_reduce_small(v):
    n_rows = v.shape[0]
    rh = n_rows // 2
    assert rh % SUBLANES == 0

    def body(v_ref, o_ref, part, recv, send_sems, recv_sems):
        x, y, c = _mesh_pos()
        sibling = (x, y, 1 - c)
        mine = pl.ds(pl.multiple_of(c * rh, SUBLANES), rh)
        theirs = pl.ds(pl.multiple_of((1 - c) * rh, SUBLANES), rh)

        def exchange(s, src, dst, peer):
            cp = pltpu.make_async_remote_copy(src_ref=src, dst_ref=dst, send_sem=send_sems.at[s],
                                              recv_sem=recv_sems.at[s], device_id=peer, device_id_type=MESH_ID)
            cp.start()
            cp.wait()

        exchange(0, v_ref.at[theirs, :], recv.at[0], sibling)
        part[...] = v_ref[mine, :] + recv[0]
        for s, peer in ((1, (1 - x, y, c)), (2, (x, 1 - y, c))):
            exchange(s, part, recv.at[s], peer)
            part[...] = part[...] + recv[s]
        o_ref[mine, :] = part[...]
        exchange(3, part, o_ref.at[mine, :], sibling)

    vm = pl.BlockSpec(memory_space=pltpu.VMEM)
    return _pcall(
        body, name="all_reduce_small", in_specs=[vm], out_specs=vm,
        out_shape=jax.ShapeDtypeStruct(v.shape, v.dtype),
        scratch_shapes=[pltpu.VMEM((rh, LANES), v.dtype), pltpu.VMEM((3, rh, LANES), v.dtype),
                        pltpu.SemaphoreType.DMA((4,)), pltpu.SemaphoreType.DMA((4,))],
        compiler_params=pltpu.CompilerParams(vmem_limit_bytes=VMEM_LIMIT, has_side_effects=True),
    )(v)


def _adamw_math(w, g, m, v):
    m = ADAM_B1 * m + (1.0 - ADAM_B1) * g
    v = ADAM_B2 * v + (1.0 - ADAM_B2) * (g * g)
    m_hat = m / (1.0 - ADAM_B1 ** ADAM_STEP)
    v_hat = v / (1.0 - ADAM_B2 ** ADAM_STEP)
    delta = -ADAM_LR * (m_hat / (jnp.sqrt(v_hat) + ADAM_EPS) + ADAM_WD * w)
    return delta, m, v


def _adamw(w, g, m, v, name):
    wd = w.shape[1]
    return _rows(_adamw_math, [_full(w), _full(g), _full(m), _full(v)], [(wd, F32)] * 3, name=name)


def _rows_of(a):
    return -(-a.size // (LANES * SUBLANES)) * SUBLANES


def _pack(arrs, fill=0.0):
    parts = []
    for a in arrs:
        flat = a.reshape(-1)
        flat = jnp.pad(flat, (0, _rows_of(a) * LANES - a.size), constant_values=fill)
        parts.append(flat.reshape(-1, LANES))
    used = sum(p.shape[0] for p in parts)
    rows = -(-used // ROW_TILE) * ROW_TILE
    parts.append(jnp.full((rows - used, LANES), fill, F32))
    return jnp.concatenate(parts, axis=0)


def _unpack(buf, like):
    out, off = [], 0
    for a in like:
        out.append(buf[off:off + _rows_of(a)].reshape(-1)[:a.size].reshape(a.shape))
        off += _rows_of(a)
    return out


def kernel(x, norm_pre, norm_post, s5_w_in, s5_a_re, s5_a_im, s5_log_dt, s5_b_re, s5_b_im, s5_c_re, s5_c_im, s5_d, s5_w_glu, s5_b_glu, s5_w_out, kv_norm, kv_w, kv_b_f, fox_w_in, fox_w_out, loss_target, m_norm_pre, m_norm_post, m_s5_w_in, m_s5_a_re, m_s5_a_im, m_s5_log_dt, m_s5_b_re, m_s5_b_im, m_s5_c_re, m_s5_c_im, m_s5_d, m_s5_w_glu, m_s5_b_glu, m_s5_w_out, m_kv_norm, m_kv_w, m_kv_b_f, m_fox_w_in, m_fox_w_out, v_norm_pre, v_norm_post, v_s5_w_in, v_s5_a_re, v_s5_a_im, v_s5_log_dt, v_s5_b_re, v_s5_b_im, v_s5_c_re, v_s5_c_im, v_s5_d, v_s5_w_glu, v_s5_b_glu, v_s5_w_out, v_kv_norm, v_kv_w, v_kv_b_f, v_fox_w_in, v_fox_w_out):
    seq, dm = x.shape[1], x.shape[2]
    width = dm
    heads = dm // HEAD_DIM
    fw = heads * HEAD_DIM
    groups = width // S5_GROUP
    chip = 2 * lax.axis_index("x") + lax.axis_index("y")

    big_shards = [s5_w_in[0], s5_w_glu[0], s5_w_out[0], kv_w, fox_w_in[0], fox_w_out[0]]
    own_shards = [w.astype(BF16) for w in big_shards] + [s5_d, s5_b_glu]
    fill_own = lambda gs, owns: [lax.dynamic_update_slice(g, own[None], (chip, 0, 0)) for g, own in zip(gs, owns)]
    c_idx = lax.axis_index("c")
    pos = jnp.stack([chip, c_idx]).astype(jnp.int32)
    h0 = x[0]
    target = loss_target[0]
    g_pre0, g_pre1 = norm_pre[0:1], norm_pre[1:2]
    g_post0, g_post1 = norm_post[0:1], norm_post[1:2]
    g_kv = kv_norm.reshape(1, dm)
    first_owns = [own_shards[0], s5_d, s5_b_glu]
    xn1, *first_gathered = _rows(lambda h, g: (h * _rstd(h) * g,), [_full(h0)], [(dm, BF16)], consts=[g_pre0], name="norm_pre0",
                                 carry=_all_gather_weights(first_owns[:1], first_owns[1:]))
    g_win, g_d, g_bglu = fill_own(first_gathered, first_owns)
    gather_rest = _all_gather_weights(own_shards[1:5], [])
    gather_last = _all_gather_weights(own_shards[5:6], [])
    cols = lambda g: jnp.moveaxis(g, 0, 1).reshape(g.shape[1], -1)
    rows = lambda g: g.reshape(-1, g.shape[2])
    w_in = g_win
    d_skip, b_glu = cols(g_d), cols(g_bglu)
    b_f = jnp.pad(kv_b_f, (0, LANES - heads)).reshape(1, LANES)

    a_re, a_im, log_dt = s5_a_re[0], s5_a_im[0], s5_log_dt[0]
    disc, disc_vjp = jax.vjp(_ssm_discretize, a_re, a_im, log_dt)
    gpb = SSM_CH // S5_GROUP
    n_cb = groups // gpb
    par = jnp.stack([p.reshape(n_cb, gpb * S5_STATE) for p in disc], axis=1)
    b_t = lambda b: jnp.swapaxes(b, 1, 2)
    b_blk = jnp.stack([_block_diag(b_t(s5_b_re[0])), _block_diag(b_t(s5_b_im[0]))], axis=1)
    ct_blk = jnp.stack([_block_diag(s5_c_re[0]), _block_diag(s5_c_im[0])], axis=1)
    c_blk = jnp.swapaxes(ct_blk, 2, 3)

    xn1 = _to_slab(xn1)
    uz = _mm(xn1, w_in, name="s5_in")
    ys, y1b, *rest = _ssm_fwd(uz, b_blk, c_blk, par, d_skip, carry=gather_rest)
    g_wglu, g_wout, g_kvw, g_fwin = fill_own(rest, own_shards[1:5])
    w_glu, w_out = rows(g_wglu), rows(g_wout)
    kvw_full = cols(g_kvw)
    w_kv = kvw_full[:, :2 * fw]
    w_f = jnp.pad(kvw_full[:, 2 * fw:], ((0, 0), (0, LANES - heads)))
    fw_in = g_fwin
    glu_a = _mm(y1b, w_glu, name="s5_glu")

    def gate_fn(y, a, z, b):
        return (_gelu(y) * _sigmoid(a + b) * _silu(z),)

    y3b = _rows(gate_fn, [_full(ys), _full(glu_a), (uz, width, 1)], [(width, BF16)], consts=[b_glu], name="s5_gate")[0]
    o1 = _from_slab(_mm(y3b, w_out, name="s5_out"))

    def mid_fn(h, o, gp, gk, gq):
        h1 = h + o * _rstd(o) * gp
        r = _rstd(h1)
        return h1, h1 * r * gk, h1 * r * gq

    h1, xk, xn2 = _rows(mid_fn, [_full(h0), _full(o1)], [(dm, F32), (dm, BF16), (dm, BF16)],
                        consts=[g_post0, g_kv, g_pre1], name="mid_norms")

    kv, g_fwout = _mm(xk, w_kv, out_dtype=BF16, name="kv_proj", carry=gather_last)
    fw_out = rows(fill_own([g_fwout], own_shards[5:6])[0])
    fl = _mm(xk, w_f, name="f_proj")
    qz = _mm(xn2, fw_in, name="fox_in")
    cum = _cum_fwd(fl, b_f)
    t_att = min(ATT_TILE, seq)
    cum_t = cum[:, :heads].T
    ck = cum_t.reshape(heads, seq // t_att, 1, t_att)
    o, ox, o2b, lse = _att_fwd(qz, kv, ck)
    o3 = _mm(o2b, fw_out, name="fox_out")

    def loss_fn(h, o, t, g):
        r = _rstd(o)
        err = h + o * r * g - t
        dh = err * (1.0 / dm)
        do, dg = _rms_bwd(o, g, dh)
        part = 0.5 * jnp.sum(jnp.mean(err * err, axis=-1, keepdims=True), axis=0, keepdims=True)
        return dh, do, jnp.broadcast_to(part, (1, LANES)), _colsum(dg)

    dh2, do3, loss_part, dg_post1 = _rows(loss_fn, [_full(h1), _full(o3), _full(target)], [(dm, F32), (dm, BF16)],
                                          consts=[g_post1], accs=[(1, LANES), (1, dm)], name="loss_head")
    loss = lax.psum(loss_part[0, 0], MESH_AXES)

    do2 = _mm(do3, fw_out, tb=True, name="fox_out_dx")
    dw_fout = _mm(o2b, do3, ta=True, name="fox_out_dw")

    def fox_gate_bwd(d, a, z):
        return d * _silu(z), d * a * _dsilu(z)

    do, dz2 = _rows(fox_gate_bwd, [_full(do2), _full(o), (qz, fw, 1)], [(fw, F32), (fw, BF16)], name="fox_gate_bwd")
    dq, dk, dv, dck = _att_bwd(qz, kv, do, ox, lse, ck)
    dcum = jnp.pad(dck.reshape(heads, seq).T, ((0, 0), (0, LANES - heads)))
    dfl, db_f = _cum_bwd(dcum, fl, b_f)
    dqz = (dq, dz2)
    dkv = (dk, dv)
    dxn2 = _mm(dqz, fw_in, tb=True, name="fox_in_dx")
    dw_fin = _mm(xn2, dqz, ta=True, out_split=N_CHIPS, name="fox_in_dw")
    dxk_f = _mm(dfl, w_f, tb=True, name="f_proj_dx")
    dxk = _mm(dkv, w_kv, tb=True, add=dxk_f, name="kv_proj_dx")
    dw_kv = _mm(xk, dkv, ta=True, name="kv_proj_dw")
    dw_f = _mm(xk, dfl, ta=True, name="f_proj_dw")

    def mid_bwd(d2, h, dq_, dk_, o, gq, gk, gp):
        dxa, dga = _rms_bwd(h, gq, dq_)
        dxb, dgb = _rms_bwd(h, gk, dk_)
        dh = d2 + dxa + dxb
        do_, dgp = _rms_bwd(o, gp, dh)
        return dh, do_, _colsum(dga), _colsum(dgb), _colsum(dgp)

    to_cols = lambda g: jnp.moveaxis(g.reshape(g.shape[0], N_CHIPS, -1), 1, 0)
    to_rows = lambda g: g.reshape(N_CHIPS, -1, g.shape[1])
    dw_kv_full = jnp.concatenate([dw_kv, dw_f[:, :heads]], axis=1)
    early_grads = [to_cols(dw_kv_full), dw_fin, to_rows(dw_fout)]
    dh1, do1, dg_pre1, dg_kv, dg_post0, *early_theirs = _rows(
        mid_bwd, [_full(dh2), _full(h1), _full(dxn2), _full(dxk), _full(o1)], [(dm, F32), (dm, BF16)],
        consts=[g_pre1, g_kv, g_post0], accs=[(1, dm)] * 3, name="mid_norms_bwd", carry=_swap_halves(early_grads))
    early_sums = [_pair_sum_bf16(g, t, pos, f"grad_pair_sum_{3 + i}") for i, (g, t) in enumerate(zip(early_grads, early_theirs))]

    do1 = _to_slab(do1)
    dy3 = _mm(do1, w_out, tb=True, name="s5_out_dx")
    dw_out = _mm(y3b, do1, ta=True, name="s5_out_dw")

    def gate_bwd(d3, y, a, z, b):
        y1 = _gelu(y)
        gate = _sigmoid(a + b)
        dy2 = d3 * _silu(z)
        da = dy2 * y1 * gate * (1.0 - gate)
        return dy2 * gate, da, d3 * (y1 * gate) * _dsilu(z), _colsum(da)

    dy1_direct, da, dz, db_glu = _rows(gate_bwd, [_full(dy3), _full(ys), _full(glu_a), (uz, width, 1)],
                                       [(width, F32), (width, BF16), (width, BF16)], consts=[b_glu],
                                       accs=[(1, width)], name="s5_gate_bwd")
    dy1 = _mm(da, w_glu, tb=True, add=dy1_direct, name="s5_glu_dx")
    dw_glu = _mm(y1b, da, ta=True, name="s5_glu_dw")
    mid_grads = [to_rows(dw_glu), to_rows(dw_out)]
    du, dbt_blk, dct_blk, dpar, dd, *carried = _ssm_bwd(
        uz, dy1, ys, b_blk, ct_blk, par, d_skip, carry=_together(_scatter_to_owner(early_sums), _swap_halves(mid_grads)))
    early_recv, mid_theirs = carried[:3], carried[3:]
    mid_sums = [_pair_sum_bf16(g, t, pos, f"grad_pair_sum_{1 + i}") for i, (g, t) in enumerate(zip(mid_grads, mid_theirs))]
    duz = (du, dz)
    dw_in, *mid_recv = _mm(xn1, duz, ta=True, out_split=N_CHIPS, name="s5_in_dw", carry=_scatter_to_owner(mid_sums))
    late_grads = [dw_in]
    dxn1, *late_theirs = _mm(duz, w_in, tb=True, name="s5_in_dx", carry=_swap_halves(late_grads))
    dxn1 = _from_slab(dxn1)
    late_sums = [_pair_sum_bf16(dw_in, late_theirs[0], pos, "grad_pair_sum_0")]

    def first_bwd(d1, h, dxn, g):
        dx, dg = _rms_bwd(h, g, dxn)
        return d1 + dx, _colsum(dg)

    grad_x, dg_pre0, *late_recv = _rows(first_bwd, [_full(dh1), _full(h0), _full(dxn1)], [(dm, F32)], consts=[g_pre0],
                                        accs=[(1, dm)], name="norm_pre0_bwd", carry=_scatter_to_owner(late_sums))

    dpar_g = [dpar[:, i, :].reshape(groups, S5_STATE) for i in range(4)]
    da_re, da_im, dlog_dt = disc_vjp(tuple(dpar_g))
    db_re = jnp.swapaxes(_block_diag_extract(dbt_blk[:, 0]), 1, 2)
    db_im = jnp.swapaxes(_block_diag_extract(dbt_blk[:, 1]), 1, 2)
    dc_re = _block_diag_extract(dct_blk[:, 0])
    dc_im = _block_diag_extract(dct_blk[:, 1])

    small_local = [jnp.concatenate([dg_pre0, dg_pre1]), jnp.concatenate([dg_post0, dg_post1]),
                   da_re[None], da_im[None], dlog_dt[None], db_re[None], db_im[None], dc_re[None], dc_im[None],
                   dd, db_glu, dg_kv.reshape(dm), db_f[0, :heads]]
    small_sum = _unpack(_all_reduce_small(_pack(small_local)), small_local)
    (g_norm_pre, g_norm_post, g_a_re, g_a_im, g_log_dt, g_b_re, g_b_im, g_c_re, g_c_im, g_d_full, g_bglu_full,
     g_kv_norm, g_b_f) = small_sum
    shard = width // N_CHIPS
    g_d_own = lax.dynamic_slice(g_d_full, (0, chip * shard), (1, shard))
    g_bglu_own = lax.dynamic_slice(g_bglu_full, (0, chip * shard), (1, shard))

    big_grads = late_grads + mid_grads + early_grads
    theirs = list(late_theirs) + list(mid_theirs) + list(early_theirs)
    received = list(late_recv) + list(mid_recv) + list(early_recv)
    halves = [_chip_sum(g, t, r, pos, f"grad_chip_sum_{i}") for i, (g, t, r) in enumerate(zip(big_grads, theirs, received))]
    g_win_s, g_wglu_s, g_wout_s, g_kvw_s, g_fwin_s, g_fwout_s = _join_halves(halves)

    big_w = big_shards
    big_g = [g_win_s, g_wglu_s, g_wout_s, g_kvw_s, g_fwin_s, g_fwout_s]
    big_m = [m_s5_w_in[0], m_s5_w_glu[0], m_s5_w_out[0], m_kv_w, m_fox_w_in[0], m_fox_w_out[0]]
    big_v = [v_s5_w_in[0], v_s5_w_glu[0], v_s5_w_out[0], v_kv_w, v_fox_w_in[0], v_fox_w_out[0]]
    big_upd = [_adamw(w, g, m, v, f"adamw_{i}") for i, (w, g, m, v) in enumerate(zip(big_w, big_g, big_m, big_v))]

    small_names = ["norm_pre", "norm_post", "s5_a_re", "s5_a_im", "s5_log_dt", "s5_b_re", "s5_b_im", "s5_c_re", "s5_c_im",
                   "s5_d", "s5_b_glu", "kv_norm", "kv_b_f"]
    small_w = [norm_pre, norm_post, s5_a_re, s5_a_im, s5_log_dt, s5_b_re, s5_b_im, s5_c_re, s5_c_im, s5_d, s5_b_glu, kv_norm, kv_b_f]
    small_m = [m_norm_pre, m_norm_post, m_s5_a_re, m_s5_a_im, m_s5_log_dt, m_s5_b_re, m_s5_b_im, m_s5_c_re, m_s5_c_im, m_s5_d, m_s5_b_glu, m_kv_norm, m_kv_b_f]
    small_v = [v_norm_pre, v_norm_post, v_s5_a_re, v_s5_a_im, v_s5_log_dt, v_s5_b_re, v_s5_b_im, v_s5_c_re, v_s5_c_im, v_s5_d, v_s5_b_glu, v_kv_norm, v_kv_b_f]
    small_g = [g_norm_pre, g_norm_post, g_a_re, g_a_im, g_log_dt, g_b_re, g_b_im, g_c_re, g_c_im, g_d_own, g_bglu_own, g_kv_norm, g_b_f]
    small_g = [g.reshape(w.shape) for g, w in zip(small_g, small_w)]
    sd, sm, sv = _adamw(_pack(small_w), _pack(small_g), _pack(small_m), _pack(small_v, fill=1.0), "adamw_small")
    small_delta, small_newm, small_newv = _unpack(sd, small_w), _unpack(sm, small_w), _unpack(sv, small_w)

    order = ["norm_pre", "norm_post", "s5_w_in", "s5_a_re", "s5_a_im", "s5_log_dt", "s5_b_re", "s5_b_im", "s5_c_re", "s5_c_im",
             "s5_d", "s5_w_glu", "s5_b_glu", "s5_w_out", "kv_norm", "kv_w", "kv_b_f", "fox_w_in", "fox_w_out"]
    big_names = ["s5_w_in", "s5_w_glu", "s5_w_out", "kv_w", "fox_w_in", "fox_w_out"]
    big_like = [s5_w_in, s5_w_glu, s5_w_out, kv_w, fox_w_in, fox_w_out]
    grads, deltas, new_m, new_v = {}, {}, {}, {}
    for i, n in enumerate(big_names):
        shp = big_like[i].shape
        grads[n] = big_g[i].reshape(shp)
        deltas[n], new_m[n], new_v[n] = (a.reshape(shp) for a in big_upd[i])
    for i, n in enumerate(small_names):
        grads[n], deltas[n], new_m[n], new_v[n] = small_g[i], small_delta[i], small_newm[i], small_newv[i]

    return (loss, grad_x[None], *[grads[n] for n in order], *[deltas[n] for n in order],
            *[new_m[n] for n in order], *[new_v[n] for n in order])
```

```python
import functools
import math
from typing import Callable, NamedTuple

import jax
import jax.numpy as jnp
from jax import lax
from jax.experimental import pallas as pl
from jax.experimental.pallas import tpu as pltpu

F32 = jnp.float32
BF16 = jnp.bfloat16

D_MODEL = 2048
SEQ = 4096
S5_GROUP = 16
S5_STATE = 64
HEAD_DIM = 128
RMS_EPS = 1e-6
NEG_INF = -1e30
ADAM_LR = 0.001
ADAM_B1 = 0.9
ADAM_B2 = 0.999
ADAM_EPS = 1e-08
ADAM_WD = 0.01
ADAM_STEP = 10

LANES = 128
SUBLANES = 8
VMEM_LIMIT = 56 * 1024 * 1024
N_CHIPS = 4
MESH_AXES = ("x", "y", "c")
MESH_ID = pl.DeviceIdType.MESH

SSM_CH = 128
ROW_TILE = 256
SCAN_ROWS = 512
SCAN_UNROLL = 4
ATT_TILE = 512
ATT_UNROLL = 2
CUM_TILE = 512


def _pcall(body, **kw):
    return pl.pallas_call(body, **kw)


def _params(sem=None):
    if sem is None:
        return pltpu.CompilerParams(vmem_limit_bytes=VMEM_LIMIT)
    return pltpu.CompilerParams(vmem_limit_bytes=VMEM_LIMIT, dimension_semantics=sem)


def _sigmoid(x):
    return 1.0 / (1.0 + jnp.exp(-x))


def _silu(z):
    return z * _sigmoid(z)


def _dsilu(z):
    s = _sigmoid(z)
    return s * (1.0 + z * (1.0 - s))


_GELU_C = math.sqrt(2.0 / math.pi)


def _gelu(x):
    return 0.5 * x * (1.0 + jnp.tanh(_GELU_C * (x + 0.044715 * x * x * x)))


def _dgelu(x):
    t = jnp.tanh(_GELU_C * (x + 0.044715 * x * x * x))
    return 0.5 * (1.0 + t) + 0.5 * x * (1.0 - t * t) * _GELU_C * (1.0 + 3.0 * 0.044715 * x * x)


def _rstd(x):
    return lax.rsqrt(jnp.mean(x * x, axis=-1, keepdims=True) + RMS_EPS)


def _rms_bwd(x, g, dy):
    r = _rstd(x)
    dyg = dy * g
    dx = r * dyg - x * (r * r * r) * jnp.mean(dyg * x, axis=-1, keepdims=True)
    return dx, dy * (x * r)


def _colsum(v):
    return jnp.sum(v, axis=0, keepdims=True)


ANY = pl.BlockSpec(memory_space=pl.ANY)


class _Exchange(NamedTuple):
    arrays: tuple
    out_shapes: tuple
    scratch: tuple
    start: Callable
    finish: Callable


def _together(*exs):
    def parts(seq, field):
        out, off = [], 0
        for e in exs:
            n = len(getattr(e, field))
            out.append(seq[off:off + n])
            off += n
        return out

    def start(ins, outs, sems):
        for e, i, o, s in zip(exs, parts(ins, "arrays"), parts(outs, "out_shapes"), parts(sems, "scratch")):
            e.start(i, o, s)

    def finish(ins, outs, sems):
        for e, i, o, s in zip(exs, parts(ins, "arrays"), parts(outs, "out_shapes"), parts(sems, "scratch")):
            e.finish(i, o, s)

    return _Exchange(arrays=sum((tuple(e.arrays) for e in exs), ()), out_shapes=sum((tuple(e.out_shapes) for e in exs), ()),
                     scratch=sum((tuple(e.scratch) for e in exs), ()), start=start, finish=finish)


def _exchange_call(ex, name):
    n_in, n_out = len(ex.arrays), len(ex.out_shapes)

    def body(*refs):
        ins, outs, sems = refs[:n_in], refs[n_in:n_in + n_out], refs[n_in + n_out:]
        ex.start(ins, outs, sems)
        ex.finish(ins, outs, sems)

    return _pcall(body, name=name, in_specs=[ANY] * n_in, out_specs=[ANY] * n_out, out_shape=list(ex.out_shapes),
                  scratch_shapes=list(ex.scratch), compiler_params=pltpu.CompilerParams(has_side_effects=True))(*ex.arrays)


def _carry(ex, refs, n_fixed_in, n_fixed_out):
    if ex is None:
        return refs, lambda cond: None, lambda cond: None
    n_in, n_out, n_sem = len(ex.arrays), len(ex.out_shapes), len(ex.scratch)
    fixed_in = refs[:n_fixed_in]
    ex_in = refs[n_fixed_in:n_fixed_in + n_in]
    rest = refs[n_fixed_in + n_in:]
    fixed_out = rest[:n_fixed_out]
    ex_out = rest[n_fixed_out:n_fixed_out + n_out]
    scratch = rest[n_fixed_out + n_out:]
    sems = scratch[len(scratch) - n_sem:]

    def start_when(cond):
        pl.when(cond)(lambda: ex.start(ex_in, ex_out, sems))

    def finish_when(cond):
        pl.when(cond)(lambda: ex.finish(ex_in, ex_out, sems))

    return tuple(fixed_in) + tuple(fixed_out) + tuple(scratch[:len(scratch) - n_sem]), start_when, finish_when


def _carry_specs(ex):
    if ex is None:
        return (), [], [], [], []
    return ex.arrays, [ANY] * len(ex.arrays), [ANY] * len(ex.out_shapes), list(ex.out_shapes), list(ex.scratch)


def _mm(a, b, *, ta=False, tb=False, out_dtype=F32, add=None, post=None, out_split=1, tm=1024, tn=1024, tk=2048, name,
        carry=None):
    def describe(op):
        if isinstance(op, (tuple, list)):
            assert all(p.ndim == 2 and p.shape == op[0].shape for p in op)
            return list(op), op[0].shape[0], op[0].shape[1], False
        if op.ndim == 3:
            return [op], op.shape[1], op.shape[2], True
        return [op], op.shape[0], op.shape[1], False

    a_parts, a_rows, a_pc, a_stack = describe(a)
    b_parts, b_rows, b_pc, b_stack = describe(b)
    a_cols = a_pc * (a.shape[0] if a_stack else len(a_parts))
    b_cols = b_pc * (b.shape[0] if b_stack else len(b_parts))
    k_dim, m_dim = (a_rows, a_cols) if ta else (a_cols, a_rows)
    n_dim, kb = (b_rows, b_cols) if tb else (b_cols, b_rows)
    assert kb == k_dim, (k_dim, kb)
    tm = min(tm, a_pc) if ta else min(tm, m_dim)
    tk = min(tk, k_dim, k_dim if ta else a_pc, b_pc if tb else k_dim)
    tn = min(tn, n_dim // out_split, n_dim if tb else b_pc)
    a_ct, b_ct = (tm if ta else tk), (tk if tb else tn)
    assert m_dim % tm == 0 and n_dim % tn == 0 and k_dim % tk == 0 and a_pc % a_ct == 0 and b_pc % b_ct == 0
    assert (n_dim // out_split) % tn == 0
    nk = k_dim // tk
    dims = (((0 if ta else 1,), (1 if tb else 0,)), ((), ()))
    n_a, n_b = len(a_parts), len(b_parts)
    assert n_a == 1 or n_b == 1

    def operand_specs(parts, stack, rows_t, cols_t, per, row_of, col_of):
        specs = []
        for p in range(len(parts)):
            def col(i, j, k, p=p):
                return jnp.clip(col_of(i, j, k) - p * per, 0, per - 1) if len(parts) > 1 else col_of(i, j, k)
            if stack:
                specs.append(pl.BlockSpec((None, rows_t, cols_t),
                                          lambda i, j, k, col=col: (col(i, j, k) // per, row_of(i, j, k), col(i, j, k) % per)))
            else:
                specs.append(pl.BlockSpec((rows_t, cols_t), lambda i, j, k, col=col: (row_of(i, j, k), col(i, j, k))))
        return specs

    if ta:
        a_specs = operand_specs(a_parts, a_stack, tk, tm, a_pc // tm, lambda i, j, k: k, lambda i, j, k: i)
    else:
        a_specs = operand_specs(a_parts, a_stack, tm, tk, a_pc // tk, lambda i, j, k: i, lambda i, j, k: k)
    if tb:
        b_specs = operand_specs(b_parts, b_stack, tn, tk, b_pc // tk, lambda i, j, k: j, lambda i, j, k: k)
    else:
        b_specs = operand_specs(b_parts, b_stack, tk, tn, b_pc // tn, lambda i, j, k: k, lambda i, j, k: j)

    post_fn, post_arrays = post if post is not None else (None, ())
    extras = ((add,) if add is not None else ()) + tuple(post_arrays)
    n_fixed_in = n_a + n_b + len(extras)
    grid = (m_dim // tm, n_dim // tn, nk)
    ex_args, ex_in_specs, ex_out_specs, ex_out_shapes, ex_scratch = _carry_specs(carry)

    def body(*refs):
        refs, start_when, finish_when = _carry(carry, refs, n_fixed_in, 1)
        i, j, k = pl.program_id(0), pl.program_id(1), pl.program_id(2)
        step = (i * grid[1] + j) * grid[2] + k
        start_when(step == 0)
        compute(*refs)
        finish_when(step == grid[0] * grid[1] * grid[2] - 1)

    def compute(*refs):
        a_refs, b_refs = refs[:n_a], refs[n_a:n_a + n_b]
        rest = refs[n_a + n_b:]
        extra_refs, o_ref = rest[:len(extras)], rest[len(extras)]
        acc = None if nk == 1 else rest[-1]
        i, j, k = pl.program_id(0), pl.program_id(1), pl.program_id(2)

        def finish(res):
            tiles = [r[...] for r in extra_refs]
            if add is not None:
                res = res + tiles.pop(0)
            if post_fn is not None:
                res = post_fn(res, *tiles)
            o_ref[...] = res.astype(out_dtype)

        def accumulate(a_ref, b_ref):
            prod = lax.dot_general(a_ref[...].astype(BF16), b_ref[...].astype(BF16), dims,
                                   preferred_element_type=F32)
            if nk == 1:
                finish(prod)
                return

            @pl.when(k == 0)
            def _():
                acc[...] = prod

            @pl.when(jnp.logical_and(k > 0, k < nk - 1))
            def _():
                acc[...] += prod

            @pl.when(k == nk - 1)
            def _():
                finish(acc[...] + prod)

        if n_a == 1 and n_b == 1:
            accumulate(a_refs[0], b_refs[0])
        else:
            many, block, per = (a_refs, (i if ta else k), a_pc // a_ct) if n_a > 1 else (b_refs, (k if tb else j), b_pc // b_ct)
            for p, ref in enumerate(many):
                @pl.when(block // per == p)
                def _(ref=ref):
                    accumulate(ref, b_refs[0]) if n_a > 1 else accumulate(a_refs[0], ref)

    per_out = n_dim // out_split // tn
    if out_split > 1:
        o_spec = pl.BlockSpec((None, tm, tn), lambda i, j, k: (j // per_out, i, j % per_out))
        out_shape = jax.ShapeDtypeStruct((out_split, m_dim, n_dim // out_split), out_dtype)
    else:
        o_spec = pl.BlockSpec((tm, tn), lambda i, j, k: (i, j))
        out_shape = jax.ShapeDtypeStruct((m_dim, n_dim), out_dtype)
    in_specs = a_specs + b_specs + [pl.BlockSpec((tm, tn), lambda i, j, k: (i, j)) for _ in extras]
    args = tuple(a_parts) + tuple(b_parts) + extras
    acc_scratch = [pltpu.VMEM((tm, tn), F32)] if nk > 1 else []
    if carry is None:
        return _pcall(
            body, name=name, grid=grid, in_specs=in_specs, out_specs=o_spec, out_shape=out_shape,
            scratch_shapes=acc_scratch, compiler_params=_params(("parallel", "parallel", "arbitrary")),
        )(*args)
    return _pcall(
        body, name=name, grid=grid, in_specs=in_specs + ex_in_specs, out_specs=[o_spec] + ex_out_specs,
        out_shape=[out_shape] + ex_out_shapes, scratch_shapes=acc_scratch + ex_scratch,
        compiler_params=pltpu.CompilerParams(vmem_limit_bytes=VMEM_LIMIT, has_side_effects=True,
                                             dimension_semantics=("arbitrary", "arbitrary", "arbitrary")),
    )(*args, *ex_args)


def _rows(fn, ins, outs, *, name, consts=(), accs=(), carry=None):
    n_rows = ins[0][0].shape[0]
    tr = min(ROW_TILE, n_rows)
    assert n_rows % tr == 0
    n_in, n_c, n_out = len(ins), len(consts), len(outs)
    n_steps = n_rows // tr
    ex_args, ex_in_specs, ex_out_specs, ex_out_shapes, ex_scratch = _carry_specs(carry)

    def body(*refs):
        refs, start_when, finish_when = _carry(carry, refs, n_in + n_c, n_out + len(accs))
        start_when(pl.program_id(0) == 0)
        _compute(*refs)
        finish_when(pl.program_id(0) == n_steps - 1)

    def _compute(*refs):
        vals = [r[...] for r in refs[:n_in + n_c]]
        res = fn(*vals)
        res = res if isinstance(res, (tuple, list)) else (res,)
        o_refs = refs[n_in + n_c:]
        for r, v in zip(o_refs[:n_out], res[:n_out]):
            r[...] = v.astype(r.dtype)
        if accs:
            first = pl.program_id(0) == 0
            for r, v in zip(o_refs[n_out:], res[n_out:]):
                @pl.when(first)
                def _(r=r, v=v):
                    r[...] = v

                @pl.when(jnp.logical_not(first))
                def _(r=r, v=v):
                    r[...] += v

    in_specs = [pl.BlockSpec((tr, w), functools.partial(lambda i, cb: (i, cb), cb=cb)) for _, w, cb in ins]
    in_specs += [pl.BlockSpec(c.shape, functools.partial(lambda i, nd: (0,) * nd, nd=c.ndim)) for c in consts]
    out_specs = [pl.BlockSpec((tr, w), lambda i: (i, 0)) for w, _ in outs]
    out_specs += [pl.BlockSpec(s, lambda i: (0, 0)) for s in accs]
    out_shape = [jax.ShapeDtypeStruct((n_rows, w), dt) for w, dt in outs]
    out_shape += [jax.ShapeDtypeStruct(s, F32) for s in accs]
    sequential = bool(accs) or carry is not None
    params = _params(("arbitrary",) if sequential else ("parallel",))
    if carry is not None:
        params = pltpu.CompilerParams(vmem_limit_bytes=VMEM_LIMIT, dimension_semantics=("arbitrary",), has_side_effects=True)
    return _pcall(
        body, name=name, grid=(n_steps,), in_specs=in_specs + ex_in_specs, out_specs=out_specs + ex_out_specs,
        out_shape=out_shape + ex_out_shapes, scratch_shapes=ex_scratch, compiler_params=params,
    )(*[a for a, _, _ in ins], *consts, *ex_args)


def _full(a):
    return (a, a.shape[1], 0)


def _cmul(ar, ai, br, bi):
    return ar * br - ai * bi, ar * bi + ai * br


def _seg_scans(scans, seg):
    assert seg & (seg - 1) == 0
    chains = []
    for re_ref, im_ref, a_re, a_im, reverse in scans:
        for k in range(len(a_re)):
            chains.append((re_ref, im_ref, k, jnp.broadcast_to(a_re[k], (SUBLANES, LANES)),
                           jnp.broadcast_to(a_im[k], (SUBLANES, LANES)), reverse))

    def slab(i, reverse):
        j = seg - 1 - i if reverse else i
        return pl.ds(pl.multiple_of(j * SUBLANES, SUBLANES), SUBLANES)

    def local(i, carry):
        out = []
        for n, (re_ref, im_ref, k, ar, ai, reverse) in enumerate(chains):
            hr, hi = _cmul(ar, ai, carry[2 * n], carry[2 * n + 1])
            hr = hr + re_ref[k, slab(i, reverse), :]
            hi = hi + im_ref[k, slab(i, reverse), :]
            re_ref[k, slab(i, reverse), :] = hr
            im_ref[k, slab(i, reverse), :] = hi
            out += [hr, hi]
        return tuple(out)

    zero = jnp.zeros((SUBLANES, LANES), F32)
    end = lax.fori_loop(0, seg, local, (zero,) * (2 * len(chains)))

    row = lax.broadcasted_iota(jnp.int32, (SUBLANES, LANES), 0)
    enter = []
    for n, (_, _, _, ar, ai, reverse) in enumerate(chains):
        edge = SUBLANES - 1 if reverse else 0
        shift = SUBLANES - 1 if reverse else 1
        pr, pi = ar, ai
        for _ in range(seg.bit_length() - 1):
            pr, pi = _cmul(pr, pi, pr, pi)
        tr_, ti_ = zero, zero
        for _ in range(SUBLANES - 1):
            vr, vi = _cmul(pr, pi, tr_, ti_)
            tr_ = jnp.where(row == edge, 0.0, pltpu.roll(vr + end[2 * n], shift, 0))
            ti_ = jnp.where(row == edge, 0.0, pltpu.roll(vi + end[2 * n + 1], shift, 0))
        enter += [tr_, ti_]

    def fix(i, carry):
        out = []
        for n, (re_ref, im_ref, k, ar, ai, reverse) in enumerate(chains):
            er, ei = _cmul(ar, ai, carry[2 * n], carry[2 * n + 1])
            re_ref[k, slab(i, reverse), :] += er
            im_ref[k, slab(i, reverse), :] += ei
            out += [er, ei]
        return tuple(out)

    lax.fori_loop(0, seg, fix, tuple(enter))
    per_scan, off = [], 0
    for scan in scans:
        per_scan.append(enter[off:off + 2 * len(scan[2])])
        off += 2 * len(scan[2])
    return per_scan


def _to_slab(a):
    s, w = a.shape
    return a.reshape(SUBLANES, s // SUBLANES, w).swapaxes(0, 1).reshape(s, w)


def _from_slab(a):
    s, w = a.shape
    return a.reshape(s // SUBLANES, SUBLANES, w).swapaxes(0, 1).reshape(s, w)


def _lane_blocks(v, n_k):
    return [v[:, k * LANES:(k + 1) * LANES] for k in range(n_k)]


def _gather_k(ref, rows, n_k):
    return jnp.concatenate([ref[k, rows, :] for k in range(n_k)], axis=1)


def _dot(a, b, dims=(((1,), (0,)), ((), ()))):
    return lax.dot_general(a.astype(BF16), b.astype(BF16), dims, preferred_element_type=F32)


_NT = (((1,), (1,)), ((), ()))
_TN = (((0,), (0,)), ((), ()))


def _ssm_fwd(uz, b_blk, c_blk, par, d_skip, carry=None):
    seq = uz.shape[0]
    width = d_skip.shape[1]
    ns = SSM_CH // S5_GROUP * S5_STATE
    n_k = ns // LANES
    seg = seq // SUBLANES
    tb = min(SCAN_ROWS, seq)
    n_cb = width // SSM_CH
    ex_args, ex_in_specs, ex_out_specs, ex_out_shapes, ex_scratch = _carry_specs(carry)

    def body(*refs):
        refs, start_when, finish_when = _carry(carry, refs, 5, 2)
        start_when(pl.program_id(0) == 0)
        compute(*refs)
        finish_when(pl.program_id(0) == n_cb - 1)

    def compute(u_ref, b_ref, c_ref, par_ref, d_ref, y_ref, g_ref, hre, him):
        coef_r, coef_i = par_ref[0, 2:3, :], par_ref[0, 3:4, :]
        for c0 in range(0, seq, tb):
            rows = pl.ds(c0, tb)
            ub = u_ref[rows, :]
            bur, bui = _dot(ub, b_ref[0, 0]), _dot(ub, b_ref[0, 1])
            xr, xi = coef_r * bur - coef_i * bui, coef_r * bui + coef_i * bur
            for k in range(n_k):
                hre[k, rows, :] = xr[:, k * LANES:(k + 1) * LANES]
                him[k, rows, :] = xi[:, k * LANES:(k + 1) * LANES]
        _seg_scans([(hre, him, _lane_blocks(par_ref[0, 0:1, :], n_k), _lane_blocks(par_ref[0, 1:2, :], n_k), False)], seg)
        for c0 in range(0, seq, tb):
            rows = pl.ds(c0, tb)
            y = _dot(_gather_k(hre, rows, n_k), c_ref[0, 0]) - _dot(_gather_k(him, rows, n_k), c_ref[0, 1])
            y = y + d_ref[...] * u_ref[rows, :]
            y_ref[rows, :] = y
            g_ref[rows, :] = _gelu(y).astype(BF16)

    blk = pl.BlockSpec((seq, SSM_CH), lambda i: (0, i))
    params = _params(("parallel",)) if carry is None else pltpu.CompilerParams(
        vmem_limit_bytes=VMEM_LIMIT, dimension_semantics=("arbitrary",), has_side_effects=True)
    return _pcall(
        body, name="ssm_fwd", grid=(n_cb,),
        in_specs=[blk,
                  pl.BlockSpec((1, 2, SSM_CH, ns), lambda i: (i, 0, 0, 0)),
                  pl.BlockSpec((1, 2, ns, SSM_CH), lambda i: (i, 0, 0, 0)),
                  pl.BlockSpec((1, 4, ns), lambda i: (i, 0, 0)),
                  pl.BlockSpec((1, SSM_CH), lambda i: (0, i))] + ex_in_specs,
        out_specs=[blk, blk] + ex_out_specs,
        out_shape=[jax.ShapeDtypeStruct((seq, width), F32), jax.ShapeDtypeStruct((seq, width), BF16)] + ex_out_shapes,
        scratch_shapes=[pltpu.VMEM((n_k, seq, LANES), F32), pltpu.VMEM((n_k, seq, LANES), F32)] + ex_scratch,
        compiler_params=params,
    )(uz, b_blk, c_blk, par, d_skip, *ex_args)


def _ssm_bwd(uz, dys, b_blk, ct_blk, par, d_skip, carry=None):
    seq = uz.shape[0]
    width = d_skip.shape[1]
    ns_all = SSM_CH // S5_GROUP * S5_STATE
    n_half = 2
    ns = ns_all // n_half
    n_k = ns // LANES
    seg = seq // SUBLANES
    tb = min(SCAN_ROWS, seq)
    n_cb = width // SSM_CH
    ex_args, ex_in_specs, ex_out_specs, ex_out_shapes, ex_scratch = _carry_specs(carry)

    def body(*refs):
        refs, start_when, finish_when = _carry(carry, refs, 6, 5)
        step = pl.program_id(0) * n_half + pl.program_id(1)
        start_when(step == 0)
        compute(*refs)
        finish_when(step == n_cb * n_half - 1)

    def compute(u_ref, dys_ref, b_ref, ct_ref, par_ref, d_ref,
                du_ref, dbt_ref, dct_ref, dpar_ref, dd_ref, hre, him, gre, gim):
        half = pl.program_id(1)
        a_r, a_i = par_ref[0, 0:1, :], par_ref[0, 1:2, :]
        coef_r, coef_i = par_ref[0, 2:3, :], par_ref[0, 3:4, :]

        def dys_of(rows):
            return dys_ref[rows, :]

        for c0 in range(0, seq, tb):
            rows = pl.ds(c0, tb)
            ub = u_ref[rows, :]
            bur, bui = _dot(ub, b_ref[0, 0]), _dot(ub, b_ref[0, 1])
            xr, xi = coef_r * bur - coef_i * bui, coef_r * bui + coef_i * bur
            dys = dys_of(rows)
            gr, gi = _dot(dys, ct_ref[0, 0]), -_dot(dys, ct_ref[0, 1])
            for k in range(n_k):
                lanes = slice(k * LANES, (k + 1) * LANES)
                hre[k, rows, :] = xr[:, lanes]
                him[k, rows, :] = xi[:, lanes]
                gre[k, rows, :] = gr[:, lanes]
                gim[k, rows, :] = gi[:, lanes]
        enter, _ = _seg_scans([(hre, him, _lane_blocks(a_r, n_k), _lane_blocks(a_i, n_k), False),
                               (gre, gim, _lane_blocks(a_r, n_k), _lane_blocks(-a_i, n_k), True)], seg)

        def corr(j, carry):
            acc, prev = carry
            acc_o, prev_o = [], []
            for k in range(n_k):
                sl = pl.ds(pl.multiple_of(j * SUBLANES, SUBLANES), SUBLANES)
                g_r, g_i = gre[k, sl, :], gim[k, sl, :]
                p_r, p_i = prev[2 * k], prev[2 * k + 1]
                acc_o += [acc[2 * k] + g_r * p_r + g_i * p_i, acc[2 * k + 1] + g_i * p_r - g_r * p_i]
                prev_o += [hre[k, sl, :], him[k, sl, :]]
            return tuple(acc_o), tuple(prev_o)

        zero = jnp.zeros((SUBLANES, LANES), F32)
        acc, _ = lax.fori_loop(0, seg, corr, ((zero,) * (2 * n_k), tuple(enter)), unroll=SCAN_UNROLL)
        da_r = jnp.concatenate([_colsum(acc[2 * k]) for k in range(n_k)], axis=1)
        da_i = jnp.concatenate([_colsum(acc[2 * k + 1]) for k in range(n_k)], axis=1)

        zeros_cn = jnp.zeros((SSM_CH, ns), F32)
        qt_r, qt_i, dct_r, dct_i = zeros_cn, zeros_cn, zeros_cn, zeros_cn
        dd = jnp.zeros((1, SSM_CH), F32)
        first = half == 0
        for c0 in range(0, seq, tb):
            rows = pl.ds(c0, tb)
            ub = u_ref[rows, :]
            dys = dys_of(rows)
            dct_r = dct_r + _dot(dys, _gather_k(hre, rows, n_k), _TN)
            dct_i = dct_i - _dot(dys, _gather_k(him, rows, n_k), _TN)
            g_r, g_i = _gather_k(gre, rows, n_k), _gather_k(gim, rows, n_k)
            qt_r = qt_r + _dot(ub, g_r, _TN)
            qt_i = qt_i + _dot(ub, g_i, _TN)
            dbu_r, dbu_i = coef_r * g_r + coef_i * g_i, coef_r * g_i - coef_i * g_r
            du = _dot(dbu_r, b_ref[0, 0], _NT) + _dot(dbu_i, b_ref[0, 1], _NT)
            dd = dd + _colsum(dys * ub)

            @pl.when(first)
            def _(du=du, dys=dys, rows=rows):
                du_ref[rows, :] = du + d_ref[...] * dys

            @pl.when(jnp.logical_not(first))
            def _(du=du, rows=rows):
                du_ref[rows, :] += du

        @pl.when(first)
        def _():
            dd_ref[...] = dd

        b_r, b_i = b_ref[0, 0], b_ref[0, 1]
        dbt_ref[0, 0] = coef_r * qt_r + coef_i * qt_i
        dbt_ref[0, 1] = coef_r * qt_i - coef_i * qt_r
        dct_ref[0, 0] = dct_r
        dct_ref[0, 1] = dct_i
        dpar_ref[0, 0:1, :] = da_r
        dpar_ref[0, 1:2, :] = da_i
        dpar_ref[0, 2:3, :] = _colsum(b_r * qt_r + b_i * qt_i)
        dpar_ref[0, 3:4, :] = _colsum(b_r * qt_i - b_i * qt_r)

    blk = lambda i, h: (0, i)
    params = _params(("parallel", "arbitrary")) if carry is None else pltpu.CompilerParams(
        vmem_limit_bytes=VMEM_LIMIT, dimension_semantics=("arbitrary", "arbitrary"), has_side_effects=True)
    return _pcall(
        body, name="ssm_bwd", grid=(n_cb, n_half),
        in_specs=[pl.BlockSpec((seq, SSM_CH), blk), pl.BlockSpec((seq, SSM_CH), blk),
                  pl.BlockSpec((1, 2, SSM_CH, ns), lambda i, h: (i, 0, 0, h)),
                  pl.BlockSpec((1, 2, SSM_CH, ns), lambda i, h: (i, 0, 0, h)),
                  pl.BlockSpec((1, 4, ns), lambda i, h: (i, 0, h)),
                  pl.BlockSpec((1, SSM_CH), blk)] + ex_in_specs,
        out_specs=[pl.BlockSpec((seq, SSM_CH), blk),
                   pl.BlockSpec((1, 2, SSM_CH, ns), lambda i, h: (i, 0, 0, h)),
                   pl.BlockSpec((1, 2, SSM_CH, ns), lambda i, h: (i, 0, 0, h)),
                   pl.BlockSpec((1, 4, ns), lambda i, h: (i, 0, h)),
                   pl.BlockSpec((1, SSM_CH), blk)] + ex_out_specs,
        out_shape=[jax.ShapeDtypeStruct((seq, width), F32),
                   jax.ShapeDtypeStruct((n_cb, 2, SSM_CH, ns_all), F32),
                   jax.ShapeDtypeStruct((n_cb, 2, SSM_CH, ns_all), F32),
                   jax.ShapeDtypeStruct((n_cb, 4, ns_all), F32),
                   jax.ShapeDtypeStruct((1, width), F32)] + ex_out_shapes,
        scratch_shapes=[pltpu.VMEM((n_k, seq, LANES), F32) for _ in range(4)] + ex_scratch,
        compiler_params=params,
    )(uz, dys, b_blk, ct_blk, par, d_skip, *ex_args)


def _ssm_discretize(a_re, a_im, log_dt):
    dt = jnp.exp(log_dt)[:, None]
    mag = jnp.exp(a_re * dt)
    abar_re = mag * jnp.cos(a_im * dt)
    abar_im = mag * jnp.sin(a_im * dt)
    den = a_re * a_re + a_im * a_im
    nr = abar_re - 1.0
    coef_re = (nr * a_re + abar_im * a_im) / den
    coef_im = (abar_im * a_re - nr * a_im) / den
    return abar_re, abar_im, coef_re, coef_im


def _block_diag(w_gcp):
    gpb = SSM_CH // S5_GROUP
    n_cb = w_gcp.shape[0] // gpb
    w = w_gcp.reshape(n_cb, gpb, S5_GROUP, 1, S5_STATE)
    eye = jnp.eye(gpb, dtype=w.dtype)[None, :, None, :, None]
    return (w * eye).reshape(n_cb, SSM_CH, gpb * S5_STATE)


def _block_diag_extract(w_blk):
    gpb = SSM_CH // S5_GROUP
    n_cb = w_blk.shape[0]
    w = w_blk.reshape(n_cb, gpb, S5_GROUP, gpb, S5_STATE)
    w = jnp.moveaxis(jnp.diagonal(w, axis1=1, axis2=3), -1, 1)
    return w.reshape(n_cb * gpb, S5_GROUP, S5_STATE)


def _split3(x):
    hi = x.astype(BF16)
    mid = (x - hi.astype(F32)).astype(BF16)
    lo = (x - hi.astype(F32) - mid.astype(F32)).astype(BF16)
    return hi, mid, lo


def _tri_sum(tri, x):
    hi, mid, lo = _split3(x)
    return (jnp.dot(tri, hi, preferred_element_type=F32) + jnp.dot(tri, mid, preferred_element_type=F32)
            + jnp.dot(tri, lo, preferred_element_type=F32))


def _log_sigmoid(x):
    return jnp.minimum(x, 0.0) - jnp.log(1.0 + jnp.exp(-jnp.abs(x)))


def _cum_fwd(fl, b_f):
    seq = fl.shape[0]
    t = min(CUM_TILE, seq)

    def body(fl_ref, b_ref, o_ref, carry):
        @pl.when(pl.program_id(0) == 0)
        def _():
            carry[...] = jnp.zeros_like(carry)

        r = lax.broadcasted_iota(jnp.int32, (t, t), 0)
        c = lax.broadcasted_iota(jnp.int32, (t, t), 1)
        tri = (c <= r).astype(BF16)
        cum = _tri_sum(tri, _log_sigmoid(fl_ref[...] + b_ref[...])) + carry[...]
        o_ref[...] = cum
        carry[...] = cum[t - 1:t, :]

    return _pcall(
        body, name="cum_fwd", grid=(seq // t,),
        in_specs=[pl.BlockSpec((t, LANES), lambda i: (i, 0)), pl.BlockSpec((1, LANES), lambda i: (0, 0))],
        out_specs=pl.BlockSpec((t, LANES), lambda i: (i, 0)),
        out_shape=jax.ShapeDtypeStruct((seq, LANES), F32),
        scratch_shapes=[pltpu.VMEM((1, LANES), F32)],
        compiler_params=_params(("arbitrary",)),
    )(fl, b_f)


def _cum_bwd(dcum, fl, b_f):
    seq = fl.shape[0]
    t = min(CUM_TILE, seq)
    nb = seq // t

    def body(dc_ref, fl_ref, b_ref, o_ref, db_ref, carry):
        @pl.when(pl.program_id(0) == 0)
        def _():
            carry[...] = jnp.zeros_like(carry)
            db_ref[...] = jnp.zeros_like(db_ref)

        r = lax.broadcasted_iota(jnp.int32, (t, t), 0)
        c = lax.broadcasted_iota(jnp.int32, (t, t), 1)
        tri = (c >= r).astype(BF16)
        rev = _tri_sum(tri, dc_ref[...]) + carry[...]
        carry[...] = rev[0:1, :]
        dfl = rev * _sigmoid(-(fl_ref[...] + b_ref[...]))
        o_ref[...] = dfl
        db_ref[...] += _colsum(dfl)

    return _pcall(
        body, name="cum_bwd", grid=(nb,),
        in_specs=[pl.BlockSpec((t, LANES), lambda i: (nb - 1 - i, 0)), pl.BlockSpec((t, LANES), lambda i: (nb - 1 - i, 0)),
                  pl.BlockSpec((1, LANES), lambda i: (0, 0))],
        out_specs=[pl.BlockSpec((t, LANES), lambda i: (nb - 1 - i, 0)), pl.BlockSpec((1, LANES), lambda i: (0, 0))],
        out_shape=[jax.ShapeDtypeStruct((seq, LANES), F32), jax.ShapeDtypeStruct((1, LANES), F32)],
        scratch_shapes=[pltpu.VMEM((1, LANES), F32)],
        compiler_params=_params(("arbitrary",)),
    )(dcum, fl, b_f)


def _att_scores(q, kb, ck, row0, col0, masked):
    s = _dot(q, kb, _NT) - ck
    if masked:
        rows = row0 + lax.broadcasted_iota(jnp.int32, s.shape, 0)
        cols = col0 + lax.broadcasted_iota(jnp.int32, s.shape, 1)
        s = jnp.where(cols <= rows, s, NEG_INF)
    return s


def _pairwise_loop(lo, hi, step_fn, init):
    n = hi - lo
    w = ATT_UNROLL

    def several(p, carry):
        for u in range(w):
            carry = step_fn(lo + w * p + u, carry)
        return carry

    carry = lax.fori_loop(0, n // w, several, init)
    return lax.fori_loop(lo + (n // w) * w, hi, step_fn, carry)


def _att_fwd(qz, kv, ck):
    seq = qz.shape[0]
    heads = ck.shape[0]
    t = min(ATT_TILE, seq)
    scale = HEAD_DIM ** -0.5

    def body(q_ref, z_ref, k_ref, v_ref, ck_ref, o_ref, og_ref, lse_ref):
        i = pl.program_id(1)
        q = (q_ref[...] * scale).astype(BF16)

        def block(j, carry, masked):
            m, l, acc = carry
            rows = pl.ds(pl.multiple_of(j * t, t), t)
            s = _att_scores(q, k_ref[rows, :], ck_ref[0, j], i * t, j * t, masked)
            m_new = jnp.maximum(m, jnp.max(s, axis=1, keepdims=True))
            p = jnp.exp(s - m_new)
            alpha = jnp.exp(m - m_new)
            p_hi = p.astype(BF16)
            p_lo = (p - p_hi.astype(F32)).astype(BF16)
            vb = v_ref[rows, :]
            return (m_new, alpha * l + jnp.sum(p, axis=1, keepdims=True),
                    alpha * acc + (_dot(p_hi, vb) + _dot(p_lo, vb)))

        init = (jnp.full((t, 1), NEG_INF, F32), jnp.zeros((t, 1), F32), jnp.zeros((t, HEAD_DIM), F32))
        carry = _pairwise_loop(0, i, functools.partial(block, masked=False), init)
        m, l, acc = block(i, carry, True)
        o = acc / l
        o_ref[...] = o
        og_ref[...] = (o * _silu(z_ref[...])).astype(BF16)
        lse_ref[0] = m + jnp.log(l)

    qblk = pl.BlockSpec((t, HEAD_DIM), lambda h, i: (i, h))
    return _pcall(
        body, name="att_fwd", grid=(heads, seq // t),
        in_specs=[qblk, pl.BlockSpec((t, HEAD_DIM), lambda h, i: (i, heads + h)),
                  pl.BlockSpec((seq, HEAD_DIM), lambda h, i: (0, h)),
                  pl.BlockSpec((seq, HEAD_DIM), lambda h, i: (0, heads + h)),
                  pl.BlockSpec((1, seq // t, 1, t), lambda h, i: (h, 0, 0, 0))],
        out_specs=[qblk, qblk, pl.BlockSpec((1, t, 1), lambda h, i: (h, i, 0))],
        out_shape=[jax.ShapeDtypeStruct((seq, heads * HEAD_DIM), F32), jax.ShapeDtypeStruct((seq, heads * HEAD_DIM), BF16),
                   jax.ShapeDtypeStruct((heads, seq, 1), F32)],
        compiler_params=_params(("parallel", "parallel")),
    )(qz, qz, kv, kv, ck)


def _att_bwd(qz, kv, do, ox, lse, ck):
    seq = qz.shape[0]
    heads = ck.shape[0]
    t = min(ATT_TILE, seq)
    nq = seq // t
    scale = HEAD_DIM ** -0.5

    def body(q_ref, k_ref, v_ref, do_ref, ox_ref, lse_ref, ck_ref, dq_ref, dk_ref, dv_ref, dck_ref, dq_acc, delta):
        j = pl.program_id(1)

        @pl.when(j == 0)
        def _():
            for c0 in range(0, seq, t):
                rows = pl.ds(c0, t)
                delta[rows, :] = jnp.sum(do_ref[rows, :].astype(BF16).astype(F32) * ox_ref[rows, :], axis=1, keepdims=True)
                dq_acc[rows, :] = jnp.zeros((t, HEAD_DIM), F32)

        kb, vb = k_ref[...], v_ref[...]
        ckv = ck_ref[0, 0]

        def block(i, carry, masked):
            dk, dv, dck = carry
            rows = pl.ds(pl.multiple_of(i * t, t), t)
            qb = (q_ref[rows, :] * scale).astype(BF16)
            dob = do_ref[rows, :].astype(BF16)
            s = _att_scores(qb, kb, ckv, i * t, j * t, masked)
            p = jnp.exp(s - lse_ref[0, rows, :])
            ds = p * (_dot(dob, vb, _NT) - delta[rows, :])
            dq_acc[rows, :] += _dot(ds, kb)
            return dk + _dot(ds, qb, _TN), dv + _dot(p, dob, _TN), dck - _colsum(ds)

        init = (jnp.zeros((t, HEAD_DIM), F32), jnp.zeros((t, HEAD_DIM), F32), jnp.zeros((1, t), F32))
        carry = block(j, init, True)
        dk, dv, dck = _pairwise_loop(j + 1, nq, functools.partial(block, masked=False), carry)
        dk_ref[...] = dk.astype(dk_ref.dtype)
        dv_ref[...] = dv.astype(dv_ref.dtype)
        dck_ref[0, 0] = dck

        @pl.when(j == nq - 1)
        def _():
            for c0 in range(0, seq, t):
                rows = pl.ds(c0, t)
                dq_ref[rows, :] = (dq_acc[rows, :] * scale).astype(dq_ref.dtype)

    head = pl.BlockSpec((seq, HEAD_DIM), lambda h, j: (0, h))
    col = pl.BlockSpec((1, seq, 1), lambda h, j: (h, 0, 0))
    kblk = pl.BlockSpec((t, HEAD_DIM), lambda h, j: (j, h))
    row = pl.BlockSpec((1, 1, 1, t), lambda h, j: (h, j, 0, 0))
    return _pcall(
        body, name="att_bwd", grid=(heads, nq),
        in_specs=[head, kblk, pl.BlockSpec((t, HEAD_DIM), lambda h, j: (j, heads + h)), head, head, col, row],
        out_specs=[head, kblk, kblk, row],
        out_shape=[jax.ShapeDtypeStruct((seq, heads * HEAD_DIM), BF16)] * 3 + [jax.ShapeDtypeStruct((heads, nq, 1, t), F32)],
        scratch_shapes=[pltpu.VMEM((seq, HEAD_DIM), F32), pltpu.VMEM((seq, 1), F32)],
        compiler_params=_params(("parallel", "arbitrary")),
    )(qz, kv, kv, do, ox, lse, ck)


def _mesh_pos():
    return lax.axis_index("x"), lax.axis_index("y"), lax.axis_index("c")


def _other_chips(x, y):
    return [(1 - x, y), (x, 1 - y), (1 - x, 1 - y)]


def _all_gather_weights(big, small):
    nb, ns = len(big), len(small)
    n_remote = 3 * (nb + ns)

    def plan(ins, outs, sems):
        send_sems, recv_sems, fwd_send, fwd_recv = sems
        x, y, c = _mesh_pos()
        chips = _other_chips(x, y)
        slots = [2 * cx + cy for cx, cy in chips]

        def half(ref, hc):
            rh = ref.shape[-2] // 2
            return ref.at[pl.ds(hc * rh, rh), :]

        def remote(i, j, src_chip, from_in):
            if i < nb:
                src = half(ins[i], c) if from_in else half(outs[i].at[src_chip], c)
                dst = half(outs[i].at[src_chip], c)
            else:
                src = ins[i] if from_in else outs[i].at[src_chip]
                dst = outs[i].at[src_chip]
            k = 3 * i + j
            return pltpu.make_async_remote_copy(src_ref=src, dst_ref=dst, send_sem=send_sems.at[k],
                                                recv_sem=recv_sems.at[k], device_id=(*chips[j], c),
                                                device_id_type=MESH_ID)

        def forward(i, j, hc):
            part = half(outs[i].at[slots[j]], hc)
            k = 3 * i + j
            return pltpu.make_async_remote_copy(src_ref=part, dst_ref=part, send_sem=fwd_send.at[k],
                                                recv_sem=fwd_recv.at[k], device_id=(x, y, 1 - c),
                                                device_id_type=MESH_ID)

        return remote, forward, 2 * x + y, slots, c

    def start(ins, outs, sems):
        remote, _, me, _, _ = plan(ins, outs, sems)
        for i in range(nb + ns):
            for j in range(3):
                remote(i, j, me, True).start()

    def finish(ins, outs, sems):
        remote, forward, me, slots, c = plan(ins, outs, sems)
        for i in range(nb + ns):
            for j in range(3):
                remote(i, j, slots[j], False).wait_recv()
                if i < nb:
                    forward(i, j, c).start()
        for i in range(nb):
            for j in range(3):
                forward(i, j, 1 - c).wait_recv()
        for i in range(nb + ns):
            for j in range(3):
                remote(i, j, me, True).wait_send()
                if i < nb:
                    forward(i, j, c).wait_send()

    arrays = tuple(big) + tuple(small)
    return _Exchange(
        arrays=arrays,
        out_shapes=tuple(jax.ShapeDtypeStruct((N_CHIPS,) + a.shape, a.dtype) for a in arrays),
        scratch=(pltpu.SemaphoreType.DMA((n_remote,)), pltpu.SemaphoreType.DMA((n_remote,)),
                 pltpu.SemaphoreType.DMA((3 * max(nb, 1),)), pltpu.SemaphoreType.DMA((3 * max(nb, 1),))),
        start=start, finish=finish)


def _swap_halves(grads):
    n = len(grads)

    def copies(ins, outs, sems):
        x, y, c = _mesh_pos()
        cps = []
        for i in range(n):
            rh = ins[i].shape[1] // 2
            cps.append(pltpu.make_async_remote_copy(
                src_ref=ins[i].at[:, pl.ds((1 - c) * rh, rh), :], dst_ref=outs[i], send_sem=sems[0].at[i],
                recv_sem=sems[1].at[i], device_id=(x, y, 1 - c), device_id_type=MESH_ID))
        return cps

    def start(ins, outs, sems):
        for cp in copies(ins, outs, sems):
            cp.start()

    def finish(ins, outs, sems):
        for cp in copies(ins, outs, sems):
            cp.wait()

    return _Exchange(
        arrays=tuple(grads),
        out_shapes=tuple(jax.ShapeDtypeStruct((g.shape[0], g.shape[1] // 2, g.shape[2]), g.dtype) for g in grads),
        scratch=(pltpu.SemaphoreType.DMA((n,)), pltpu.SemaphoreType.DMA((n,))),
        start=start, finish=finish)


def _pair_sum_bf16(g, theirs, pos, name):
    n, rh, cdim = theirs.shape
    tr = min(ROW_TILE, rh)
    nb = rh // tr

    def body(pos_ref, g_ref, t_ref, o_ref):
        o_ref[...] = (g_ref[...] + t_ref[...]).astype(BF16)

    slot = lambda s, pos: (pos[0] + 1 + s) % n
    grid_spec = pltpu.PrefetchScalarGridSpec(
        num_scalar_prefetch=1, grid=(n - 1, nb),
        in_specs=[pl.BlockSpec((None, tr, cdim), lambda s, i, pos: (slot(s, pos), pos[1] * nb + i, 0)),
                  pl.BlockSpec((None, tr, cdim), lambda s, i, pos: (slot(s, pos), i, 0))],
        out_specs=pl.BlockSpec((None, tr, cdim), lambda s, i, pos: (slot(s, pos), i, 0)))
    return _pcall(body, name=name, grid_spec=grid_spec, out_shape=jax.ShapeDtypeStruct(theirs.shape, BF16),
                  compiler_params=_params(("parallel", "parallel")))(pos, g, theirs)


def _chip_sum(g, theirs, recv, pos, name):
    n, rh, cdim = theirs.shape
    tr = min(ROW_TILE, rh)
    nb = rh // tr

    def body(pos_ref, g_ref, t_ref, r0, r1, r2, o_ref):
        o_ref[...] = (((g_ref[...] + t_ref[...]) + r0[...]) + r1[...]) + r2[...]

    grid_spec = pltpu.PrefetchScalarGridSpec(
        num_scalar_prefetch=1, grid=(nb,),
        in_specs=[pl.BlockSpec((None, tr, cdim), lambda i, pos: (pos[0], pos[1] * nb + i, 0)),
                  pl.BlockSpec((None, tr, cdim), lambda i, pos: (pos[0], i, 0))]
        + [pl.BlockSpec((None, tr, cdim), functools.partial(lambda i, pos, j: (j, i, 0), j=j)) for j in range(3)],
        out_specs=pl.BlockSpec((tr, cdim), lambda i, pos: (pos[1] * nb + i, 0)))
    return _pcall(body, name=name, grid_spec=grid_spec, out_shape=jax.ShapeDtypeStruct((2 * rh, cdim), F32),
                  compiler_params=_params(("parallel",)))(pos, g, theirs, recv, recv, recv)


def _scatter_to_owner(parts):
    n = len(parts)

    def copies(ins, outs, sems):
        x, y, c = _mesh_pos()
        chips = _other_chips(x, y)
        cps = []
        for i in range(n):
            for j in range(3):
                k = 3 * i + j
                cps.append(pltpu.make_async_remote_copy(
                    src_ref=ins[i].at[2 * chips[j][0] + chips[j][1]], dst_ref=outs[i].at[j],
                    send_sem=sems[0].at[k], recv_sem=sems[1].at[k], device_id=(*chips[j], c),
                    device_id_type=MESH_ID))
        return cps

    def start(ins, outs, sems):
        for cp in copies(ins, outs, sems):
            cp.start()

    def finish(ins, outs, sems):
        for cp in copies(ins, outs, sems):
            cp.wait()

    return _Exchange(
        arrays=tuple(parts),
        out_shapes=tuple(jax.ShapeDtypeStruct((3,) + p.shape[1:], p.dtype) for p in parts),
        scratch=(pltpu.SemaphoreType.DMA((3 * n,)), pltpu.SemaphoreType.DMA((3 * n,))),
        start=start, finish=finish)


def _join_halves(shards):
    n = len(shards)

    def body(*refs):
        outs, send_sems, recv_sems = refs[n:2 * n], refs[2 * n], refs[2 * n + 1]
        x, y, c = _mesh_pos()

        def copy(i, hc):
            rh = outs[i].shape[0] // 2
            rows = outs[i].at[pl.ds(hc * rh, rh), :]
            return pltpu.make_async_remote_copy(src_ref=rows, dst_ref=rows, send_sem=send_sems.at[i],
                                                recv_sem=recv_sems.at[i], device_id=(x, y, 1 - c),
                                                device_id_type=MESH_ID)

        for i in range(n):
            copy(i, c).start()
        for i in range(n):
            copy(i, c).wait_send()
            copy(i, 1 - c).wait_recv()

    return _pcall(
        body, name="grad_join_halves", in_specs=[ANY] * n, out_specs=[ANY] * n,
        out_shape=[jax.ShapeDtypeStruct(s.shape, s.dtype) for s in shards],
        input_output_aliases={i: i for i in range(n)},
        scratch_shapes=[pltpu.SemaphoreType.DMA((n,)), pltpu.SemaphoreType.DMA((n,))],
        compiler_params=pltpu.CompilerParams(has_side_effects=True),
    )(*shards)


def _all_reduce_small(v):
    n_rows = v.shape[0]
    rh = n_rows // 2
    assert rh % SUBLANES == 0

    def body(v_ref, o_ref, part, recv, send_sems, recv_sems):
        x, y, c = _mesh_pos()
        sibling = (x, y, 1 - c)
        mine = pl.ds(pl.multiple_of(c * rh, SUBLANES), rh)
        theirs = pl.ds(pl.multiple_of((1 - c) * rh, SUBLANES), rh)

        def exchange(s, src, dst, peer):
            cp = pltpu.make_async_remote_copy(src_ref=src, dst_ref=dst, send_sem=send_sems.at[s],
                                              recv_sem=recv_sems.at[s], device_id=peer, device_id_type=MESH_ID)
            cp.start()
            cp.wait()

        exchange(0, v_ref.at[theirs, :], recv.at[0], sibling)
        part[...] = v_ref[mine, :] + recv[0]
        for s, peer in ((1, (1 - x, y, c)), (2, (x, 1 - y, c))):
            exchange(s, part, recv.at[s], peer)
            part[...] = part[...] + recv[s]
        o_ref[mine, :] = part[...]
        exchange(3, part, o_ref.at[mine, :], sibling)

    vm = pl.BlockSpec(memory_space=pltpu.VMEM)
    return _pcall(
        body, name="all_reduce_small", in_specs=[vm], out_specs=vm,
        out_shape=jax.ShapeDtypeStruct(v.shape, v.dtype),
        scratch_shapes=[pltpu.VMEM((rh, LANES), v.dtype), pltpu.VMEM((3, rh, LANES), v.dtype),
                        pltpu.SemaphoreType.DMA((4,)), pltpu.SemaphoreType.DMA((4,))],
        compiler_params=pltpu.CompilerParams(vmem_limit_bytes=VMEM_LIMIT, has_side_effects=True),
    )(v)


def _adamw_math(w, g, m, v):
    m = ADAM_B1 * m + (1.0 - ADAM_B1) * g
    v = ADAM_B2 * v + (1.0 - ADAM_B2) * (g * g)
    m_hat = m / (1.0 - ADAM_B1 ** ADAM_STEP)
    v_hat = v / (1.0 - ADAM_B2 ** ADAM_STEP)
    delta = -ADAM_LR * (m_hat / (jnp.sqrt(v_hat) + ADAM_EPS) + ADAM_WD * w)
    return delta, m, v


def _adamw(w, g, m, v, name):
    wd = w.shape[1]
    return _rows(_adamw_math, [_full(w), _full(g), _full(m), _full(v)], [(wd, F32)] * 3, name=name)


def _rows_of(a):
    return -(-a.size // (LANES * SUBLANES)) * SUBLANES


def _pack(arrs, fill=0.0):
    parts = []
    for a in arrs:
        flat = a.reshape(-1)
        flat = jnp.pad(flat, (0, _rows_of(a) * LANES - a.size), constant_values=fill)
        parts.append(flat.reshape(-1, LANES))
    used = sum(p.shape[0] for p in parts)
    rows = -(-used // ROW_TILE) * ROW_TILE
    parts.append(jnp.full((rows - used, LANES), fill, F32))
    return jnp.concatenate(parts, axis=0)


def _unpack(buf, like):
    out, off = [], 0
    for a in like:
        out.append(buf[off:off + _rows_of(a)].reshape(-1)[:a.size].reshape(a.shape))
        off += _rows_of(a)
    return out


def kernel(x, norm_pre, norm_post, s5_w_in, s5_a_re, s5_a_im, s5_log_dt, s5_b_re, s5_b_im, s5_c_re, s5_c_im, s5_d, s5_w_glu, s5_b_glu, s5_w_out, kv_norm, kv_w, kv_b_f, fox_w_in, fox_w_out, loss_target, m_norm_pre, m_norm_post, m_s5_w_in, m_s5_a_re, m_s5_a_im, m_s5_log_dt, m_s5_b_re, m_s5_b_im, m_s5_c_re, m_s5_c_im, m_s5_d, m_s5_w_glu, m_s5_b_glu, m_s5_w_out, m_kv_norm, m_kv_w, m_kv_b_f, m_fox_w_in, m_fox_w_out, v_norm_pre, v_norm_post, v_s5_w_in, v_s5_a_re, v_s5_a_im, v_s5_log_dt, v_s5_b_re, v_s5_b_im, v_s5_c_re, v_s5_c_im, v_s5_d, v_s5_w_glu, v_s5_b_glu, v_s5_w_out, v_kv_norm, v_kv_w, v_kv_b_f, v_fox_w_in, v_fox_w_out):
    seq, dm = x.shape[1], x.shape[2]
    width = dm
    heads = dm // HEAD_DIM
    fw = heads * HEAD_DIM
    groups = width // S5_GROUP
    chip = 2 * lax.axis_index("x") + lax.axis_index("y")

    big_shards = [s5_w_in[0], s5_w_glu[0], s5_w_out[0], kv_w, fox_w_in[0], fox_w_out[0]]
    own_shards = [w.astype(BF16) for w in big_shards] + [s5_d, s5_b_glu]
    fill_own = lambda gs, owns: [lax.dynamic_update_slice(g, own[None], (chip, 0, 0)) for g, own in zip(gs, owns)]
    c_idx = lax.axis_index("c")
    pos = jnp.stack([chip, c_idx]).astype(jnp.int32)
    h0 = x[0]
    target = loss_target[0]
    g_pre0, g_pre1 = norm_pre[0:1], norm_pre[1:2]
    g_post0, g_post1 = norm_post[0:1], norm_post[1:2]
    g_kv = kv_norm.reshape(1, dm)
    first_owns = [own_shards[0], s5_d, s5_b_glu]
    xn1, *first_gathered = _rows(lambda h, g: (h * _rstd(h) * g,), [_full(h0)], [(dm, BF16)], consts=[g_pre0], name="norm_pre0",
                                 carry=_all_gather_weights(first_owns[:1], first_owns[1:]))
    g_win, g_d, g_bglu = fill_own(first_gathered, first_owns)
    gather_rest = _all_gather_weights(own_shards[1:5], [])
    gather_last = _all_gather_weights(own_shards[5:6], [])
    cols = lambda g: jnp.moveaxis(g, 0, 1).reshape(g.shape[1], -1)
    rows = lambda g: g.reshape(-1, g.shape[2])
    w_in = g_win
    d_skip, b_glu = cols(g_d), cols(g_bglu)
    b_f = jnp.pad(kv_b_f, (0, LANES - heads)).reshape(1, LANES)

    a_re, a_im, log_dt = s5_a_re[0], s5_a_im[0], s5_log_dt[0]
    disc, disc_vjp = jax.vjp(_ssm_discretize, a_re, a_im, log_dt)
    gpb = SSM_CH // S5_GROUP
    n_cb = groups // gpb
    par = jnp.stack([p.reshape(n_cb, gpb * S5_STATE) for p in disc], axis=1)
    b_t = lambda b: jnp.swapaxes(b, 1, 2)
    b_blk = jnp.stack([_block_diag(b_t(s5_b_re[0])), _block_diag(b_t(s5_b_im[0]))], axis=1)
    ct_blk = jnp.stack([_block_diag(s5_c_re[0]), _block_diag(s5_c_im[0])], axis=1)
    c_blk = jnp.swapaxes(ct_blk, 2, 3)

    xn1 = _to_slab(xn1)
    uz = _mm(xn1, w_in, name="s5_in")
    ys, y1b, *rest = _ssm_fwd(uz, b_blk, c_blk, par, d_skip, carry=gather_rest)
    g_wglu, g_wout, g_kvw, g_fwin = fill_own(rest, own_shards[1:5])
    w_glu, w_out = rows(g_wglu), rows(g_wout)
    kvw_full = cols(g_kvw)
    w_kv = kvw_full[:, :2 * fw]
    w_f = jnp.pad(kvw_full[:, 2 * fw:], ((0, 0), (0, LANES - heads)))
    fw_in = g_fwin
    glu_a = _mm(y1b, w_glu, name="s5_glu")

    def gate_fn(y, a, z, b):
        return (_gelu(y) * _sigmoid(a + b) * _silu(z),)

    y3b = _rows(gate_fn, [_full(ys), _full(glu_a), (uz, width, 1)], [(width, BF16)], consts=[b_glu], name="s5_gate")[0]
    o1 = _from_slab(_mm(y3b, w_out, name="s5_out"))

    def mid_fn(h, o, gp, gk, gq):
        h1 = h + o * _rstd(o) * gp
        r = _rstd(h1)
        return h1, h1 * r * gk, h1 * r * gq

    h1, xk, xn2 = _rows(mid_fn, [_full(h0), _full(o1)], [(dm, F32), (dm, BF16), (dm, BF16)],
                        consts=[g_post0, g_kv, g_pre1], name="mid_norms")

    kv, g_fwout = _mm(xk, w_kv, out_dtype=BF16, name="kv_proj", carry=gather_last)
    fw_out = rows(fill_own([g_fwout], own_shards[5:6])[0])
    fl = _mm(xk, w_f, name="f_proj")
    qz = _mm(xn2, fw_in, name="fox_in")
    cum = _cum_fwd(fl, b_f)
    t_att = min(ATT_TILE, seq)
    cum_t = cum[:, :heads].T
    ck = cum_t.reshape(heads, seq // t_att, 1, t_att)
    o, o2b, lse = _att_fwd(qz, kv, ck)
    o3 = _mm(o2b, fw_out, name="fox_out")

    def loss_fn(h, o, t, g):
        r = _rstd(o)
        err = h + o * r * g - t
        dh = err * (1.0 / dm)
        do, dg = _rms_bwd(o, g, dh)
        part = 0.5 * jnp.sum(jnp.mean(err * err, axis=-1, keepdims=True), axis=0, keepdims=True)
        return dh, do, jnp.broadcast_to(part, (1, LANES)), _colsum(dg)

    dh2, do3, loss_part, dg_post1 = _rows(loss_fn, [_full(h1), _full(o3), _full(target)], [(dm, F32), (dm, BF16)],
                                          consts=[g_post1], accs=[(1, LANES), (1, dm)], name="loss_head")
    loss = lax.psum(loss_part[0, 0], MESH_AXES)

    do2 = _mm(do3, fw_out, tb=True, name="fox_out_dx")
    dw_fout = _mm(o2b, do3, ta=True, name="fox_out_dw")

    def fox_gate_bwd(d, a, z):
        return d * _silu(z), d * a * _dsilu(z)

    do, dz2 = _rows(fox_gate_bwd, [_full(do2), _full(o), (qz, fw, 1)], [(fw, F32), (fw, BF16)], name="fox_gate_bwd")
    dq, dk, dv, dck = _att_bwd(qz, kv, do, o, lse, ck)
    dcum = jnp.pad(dck.reshape(heads, seq).T, ((0, 0), (0, LANES - heads)))
    dfl, db_f = _cum_bwd(dcum, fl, b_f)
    dqz = (dq, dz2)
    dkv = (dk, dv)
    dxn2 = _mm(dqz, fw_in, tb=True, name="fox_in_dx")
    dw_fin = _mm(xn2, dqz, ta=True, out_split=N_CHIPS, name="fox_in_dw")
    dxk_f = _mm(dfl, w_f, tb=True, name="f_proj_dx")
    dxk = _mm(dkv, w_kv, tb=True, add=dxk_f, name="kv_proj_dx")
    dw_kv = _mm(xk, dkv, ta=True, name="kv_proj_dw")
    dw_f = _mm(xk, dfl, ta=True, name="f_proj_dw")

    def mid_bwd(d2, h, dq_, dk_, o, gq, gk, gp):
        dxa, dga = _rms_bwd(h, gq, dq_)
        dxb, dgb = _rms_bwd(h, gk, dk_)
        dh = d2 + dxa + dxb
        do_, dgp = _rms_bwd(o, gp, dh)
        return dh, do_, _colsum(dga), _colsum(dgb), _colsum(dgp)

    to_cols = lambda g: jnp.moveaxis(g.reshape(g.shape[0], N_CHIPS, -1), 1, 0)
    to_rows = lambda g: g.reshape(N_CHIPS, -1, g.shape[1])
    dw_kv_full = jnp.concatenate([dw_kv, dw_f[:, :heads]], axis=1)
    early_grads = [to_cols(dw_kv_full), dw_fin, to_rows(dw_fout)]
    dh1, do1, dg_pre1, dg_kv, dg_post0, *early_theirs = _rows(
        mid_bwd, [_full(dh2), _full(h1), _full(dxn2), _full(dxk), _full(o1)], [(dm, F32), (dm, BF16)],
        consts=[g_pre1, g_kv, g_post0], accs=[(1, dm)] * 3, name="mid_norms_bwd", carry=_swap_halves(early_grads))
    early_sums = [_pair_sum_bf16(g, t, pos, f"grad_pair_sum_{3 + i}") for i, (g, t) in enumerate(zip(early_grads, early_theirs))]

    do1 = _to_slab(do1)
    dy3 = _mm(do1, w_out, tb=True, name="s5_out_dx")
    dw_out = _mm(y3b, do1, ta=True, name="s5_out_dw")

    def gate_bwd(d3, y, a, z, b):
        y1 = _gelu(y)
        gate = _sigmoid(a + b)
        dy2 = d3 * _silu(z)
        da = dy2 * y1 * gate * (1.0 - gate)
        return dy2 * gate, da, d3 * (y1 * gate) * _dsilu(z), _colsum(da)

    dy1_direct, da, dz, db_glu = _rows(gate_bwd, [_full(dy3), _full(ys), _full(glu_a), (uz, width, 1)],
                                       [(width, F32), (width, BF16), (width, BF16)], consts=[b_glu],
                                       accs=[(1, width)], name="s5_gate_bwd")
    dys = _mm(da, w_glu, tb=True, add=dy1_direct, post=(lambda d, y: d * _dgelu(y), [ys]), name="s5_glu_dx")
    dw_glu = _mm(y1b, da, ta=True, name="s5_glu_dw")
    mid_grads = [to_rows(dw_glu), to_rows(dw_out)]
    du, dbt_blk, dct_blk, dpar, dd, *carried = _ssm_bwd(
        uz, dys, b_blk, ct_blk, par, d_skip, carry=_together(_scatter_to_owner(early_sums), _swap_halves(mid_grads)))
    early_recv, mid_theirs = carried[:3], carried[3:]
    mid_sums = [_pair_sum_bf16(g, t, pos, f"grad_pair_sum_{1 + i}") for i, (g, t) in enumerate(zip(mid_grads, mid_theirs))]
    duz = (du, dz)
    dw_in, *mid_recv = _mm(xn1, duz, ta=True, out_split=N_CHIPS, name="s5_in_dw", carry=_scatter_to_owner(mid_sums))
    late_grads = [dw_in]
    dxn1, *late_theirs = _mm(duz, w_in, tb=True, name="s5_in_dx", carry=_swap_halves(late_grads))
    dxn1 = _from_slab(dxn1)
    late_sums = [_pair_sum_bf16(dw_in, late_theirs[0], pos, "grad_pair_sum_0")]

    def first_bwd(d1, h, dxn, g):
        dx, dg = _rms_bwd(h, g, dxn)
        return d1 + dx, _colsum(dg)

    grad_x, dg_pre0, *late_recv = _rows(first_bwd, [_full(dh1), _full(h0), _full(dxn1)], [(dm, F32)], consts=[g_pre0],
                                        accs=[(1, dm)], name="norm_pre0_bwd", carry=_scatter_to_owner(late_sums))

    dpar_g = [dpar[:, i, :].reshape(groups, S5_STATE) for i in range(4)]
    da_re, da_im, dlog_dt = disc_vjp(tuple(dpar_g))
    db_re = jnp.swapaxes(_block_diag_extract(dbt_blk[:, 0]), 1, 2)
    db_im = jnp.swapaxes(_block_diag_extract(dbt_blk[:, 1]), 1, 2)
    dc_re = _block_diag_extract(dct_blk[:, 0])
    dc_im = _block_diag_extract(dct_blk[:, 1])

    small_local = [jnp.concatenate([dg_pre0, dg_pre1]), jnp.concatenate([dg_post0, dg_post1]),
                   da_re[None], da_im[None], dlog_dt[None], db_re[None], db_im[None], dc_re[None], dc_im[None],
                   dd, db_glu, dg_kv.reshape(dm), db_f[0, :heads]]
    small_sum = _unpack(_all_reduce_small(_pack(small_local)), small_local)
    (g_norm_pre, g_norm_post, g_a_re, g_a_im, g_log_dt, g_b_re, g_b_im, g_c_re, g_c_im, g_d_full, g_bglu_full,
     g_kv_norm, g_b_f) = small_sum
    shard = width // N_CHIPS
    g_d_own = lax.dynamic_slice(g_d_full, (0, chip * shard), (1, shard))
    g_bglu_own = lax.dynamic_slice(g_bglu_full, (0, chip * shard), (1, shard))

    big_grads = late_grads + mid_grads + early_grads
    theirs = list(late_theirs) + list(mid_theirs) + list(early_theirs)
    received = list(late_recv) + list(mid_recv) + list(early_recv)
    halves = [_chip_sum(g, t, r, pos, f"grad_chip_sum_{i}") for i, (g, t, r) in enumerate(zip(big_grads, theirs, received))]
    g_win_s, g_wglu_s, g_wout_s, g_kvw_s, g_fwin_s, g_fwout_s = _join_halves(halves)

    big_w = big_shards
    big_g = [g_win_s, g_wglu_s, g_wout_s, g_kvw_s, g_fwin_s, g_fwout_s]
    big_m = [m_s5_w_in[0], m_s5_w_glu[0], m_s5_w_out[0], m_kv_w, m_fox_w_in[0], m_fox_w_out[0]]
    big_v = [v_s5_w_in[0], v_s5_w_glu[0], v_s5_w_out[0], v_kv_w, v_fox_w_in[0], v_fox_w_out[0]]
    big_upd = [_adamw(w, g, m, v, f"adamw_{i}") for i, (w, g, m, v) in enumerate(zip(big_w, big_g, big_m, big_v))]

    small_names = ["norm_pre", "norm_post", "s5_a_re", "s5_a_im", "s5_log_dt", "s5_b_re", "s5_b_im", "s5_c_re", "s5_c_im",
                   "s5_d", "s5_b_glu", "kv_norm", "kv_b_f"]
    small_w = [norm_pre, norm_post, s5_a_re, s5_a_im, s5_log_dt, s5_b_re, s5_b_im, s5_c_re, s5_c_im, s5_d, s5_b_glu, kv_norm, kv_b_f]
    small_m = [m_norm_pre, m_norm_post, m_s5_a_re, m_s5_a_im, m_s5_log_dt, m_s5_b_re, m_s5_b_im, m_s5_c_re, m_s5_c_im, m_s5_d, m_s5_b_glu, m_kv_norm, m_kv_b_f]
    small_v = [v_norm_pre, v_norm_post, v_s5_a_re, v_s5_a_im, v_s5_log_dt, v_s5_b_re, v_s5_b_im, v_s5_c_re, v_s5_c_im, v_s5_d, v_s5_b_glu, v_kv_norm, v_kv_b_f]
    small_g = [g_norm_pre, g_norm_post, g_a_re, g_a_im, g_log_dt, g_b_re, g_b_im, g_c_re, g_c_im, g_d_own, g_bglu_own, g_kv_norm, g_b_f]
    small_g = [g.reshape(w.shape) for g, w in zip(small_g, small_w)]
    sd, sm, sv = _adamw(_pack(small_w), _pack(small_g), _pack(small_m), _pack(small_v, fill=1.0), "adamw_small")
    small_delta, small_newm, small_newv = _unpack(sd, small_w), _unpack(sm, small_w), _unpack(sv, small_w)

    order = ["norm_pre", "norm_post", "s5_w_in", "s5_a_re", "s5_a_im", "s5_log_dt", "s5_b_re", "s5_b_im", "s5_c_re", "s5_c_im",
             "s5_d", "s5_w_glu", "s5_b_glu", "s5_w_out", "kv_norm", "kv_w", "kv_b_f", "fox_w_in", "fox_w_out"]
    big_names = ["s5_w_in", "s5_w_glu", "s5_w_out", "kv_w", "fox_w_in", "fox_w_out"]
    big_like = [s5_w_in, s5_w_glu, s5_w_out, kv_w, fox_w_in, fox_w_out]
    grads, deltas, new_m, new_v = {}, {}, {}, {}
    for i, n in enumerate(big_names):
        shp = big_like[i].shape
        grads[n] = big_g[i].reshape(shp)
        deltas[n], new_m[n], new_v[n] = (a.reshape(shp) for a in big_upd[i])
    for i, n in enumerate(small_names):
        grads[n], deltas[n], new_m[n], new_v[n] = small_g[i], small_delta[i], small_newm[i], small_newv[i]

    return (loss, grad_x[None], *[grads[n] for n in order], *[deltas[n] for n in order],
            *[new_m[n] for n in order], *[new_v[n] for n in order])
```

```python
import functools
import math
from typing import Callable, NamedTuple

import jax
import jax.numpy as jnp
from jax import lax
from jax.experimental import pallas as pl
from jax.experimental.pallas import tpu as pltpu

F32 = jnp.float32
BF16 = jnp.bfloat16

D_MODEL = 2048
SEQ = 4096
S5_GROUP = 16
S5_STATE = 64
HEAD_DIM = 128
RMS_EPS = 1e-6
NEG_INF = -1e30
ADAM_LR = 0.001
ADAM_B1 = 0.9
ADAM_B2 = 0.999
ADAM_EPS = 1e-08
ADAM_WD = 0.01
ADAM_STEP = 10

LANES = 128
SUBLANES = 8
VMEM_LIMIT = 56 * 1024 * 1024
N_CHIPS = 4
MESH_AXES = ("x", "y", "c")
MESH_ID = pl.DeviceIdType.MESH

SSM_CH = 128
ROW_TILE = 256
SCAN_ROWS = 512
SCAN_UNROLL = 4
ATT_TILE = 512
ATT_UNROLL = 2
CUM_TILE = 512


def _pcall(body, **kw):
    return pl.pallas_call(body, **kw)


def _params(sem=None):
    if sem is None:
        return pltpu.CompilerParams(vmem_limit_bytes=VMEM_LIMIT)
    return pltpu.CompilerParams(vmem_limit_bytes=VMEM_LIMIT, dimension_semantics=sem)


def _sigmoid(x):
    return 1.0 / (1.0 + jnp.exp(-x))


def _silu(z):
    return z * _sigmoid(z)


def _dsilu(z):
    s = _sigmoid(z)
    return s * (1.0 + z * (1.0 - s))


_GELU_C = math.sqrt(2.0 / math.pi)


def _gelu(x):
    return 0.5 * x * (1.0 + jnp.tanh(_GELU_C * (x + 0.044715 * x * x * x)))


def _dgelu(x):
    t = jnp.tanh(_GELU_C * (x + 0.044715 * x * x * x))
    return 0.5 * (1.0 + t) + 0.5 * x * (1.0 - t * t) * _GELU_C * (1.0 + 3.0 * 0.044715 * x * x)


def _rstd(x):
    return lax.rsqrt(jnp.mean(x * x, axis=-1, keepdims=True) + RMS_EPS)


def _rms_bwd(x, g, dy):
    r = _rstd(x)
    dyg = dy * g
    dx = r * dyg - x * (r * r * r) * jnp.mean(dyg * x, axis=-1, keepdims=True)
    return dx, dy * (x * r)


def _colsum(v):
    return jnp.sum(v, axis=0, keepdims=True)


ANY = pl.BlockSpec(memory_space=pl.ANY)


class _Exchange(NamedTuple):
    arrays: tuple
    out_shapes: tuple
    scratch: tuple
    start: Callable
    finish: Callable


def _together(*exs):
    def parts(seq, field):
        out, off = [], 0
        for e in exs:
            n = len(getattr(e, field))
            out.append(seq[off:off + n])
            off += n
        return out

    def start(ins, outs, sems):
        for e, i, o, s in zip(exs, parts(ins, "arrays"), parts(outs, "out_shapes"), parts(sems, "scratch")):
            e.start(i, o, s)

    def finish(ins, outs, sems):
        for e, i, o, s in zip(exs, parts(ins, "arrays"), parts(outs, "out_shapes"), parts(sems, "scratch")):
            e.finish(i, o, s)

    return _Exchange(arrays=sum((tuple(e.arrays) for e in exs), ()), out_shapes=sum((tuple(e.out_shapes) for e in exs), ()),
                     scratch=sum((tuple(e.scratch) for e in exs), ()), start=start, finish=finish)


def _exchange_call(ex, name):
    n_in, n_out = len(ex.arrays), len(ex.out_shapes)

    def body(*refs):
        ins, outs, sems = refs[:n_in], refs[n_in:n_in + n_out], refs[n_in + n_out:]
        ex.start(ins, outs, sems)
        ex.finish(ins, outs, sems)

    return _pcall(body, name=name, in_specs=[ANY] * n_in, out_specs=[ANY] * n_out, out_shape=list(ex.out_shapes),
                  scratch_shapes=list(ex.scratch), compiler_params=pltpu.CompilerParams(has_side_effects=True))(*ex.arrays)


def _carry(ex, refs, n_fixed_in, n_fixed_out):
    if ex is None:
        return refs, lambda cond: None, lambda cond: None
    n_in, n_out, n_sem = len(ex.arrays), len(ex.out_shapes), len(ex.scratch)
    fixed_in = refs[:n_fixed_in]
    ex_in = refs[n_fixed_in:n_fixed_in + n_in]
    rest = refs[n_fixed_in + n_in:]
    fixed_out = rest[:n_fixed_out]
    ex_out = rest[n_fixed_out:n_fixed_out + n_out]
    scratch = rest[n_fixed_out + n_out:]
    sems = scratch[len(scratch) - n_sem:]

    def start_when(cond):
        pl.when(cond)(lambda: ex.start(ex_in, ex_out, sems))

    def finish_when(cond):
        pl.when(cond)(lambda: ex.finish(ex_in, ex_out, sems))

    return tuple(fixed_in) + tuple(fixed_out) + tuple(scratch[:len(scratch) - n_sem]), start_when, finish_when


def _carry_specs(ex):
    if ex is None:
        return (), [], [], [], []
    return ex.arrays, [ANY] * len(ex.arrays), [ANY] * len(ex.out_shapes), list(ex.out_shapes), list(ex.scratch)


def _mm(a, b, *, ta=False, tb=False, out_dtype=F32, add=None, post=None, out_split=1, tm=1024, tn=1024, tk=2048, name,
        carry=None):
    def describe(op):
        if isinstance(op, (tuple, list)):
            assert all(p.ndim == 2 and p.shape == op[0].shape for p in op)
            return list(op), op[0].shape[0], op[0].shape[1], False
        if op.ndim == 3:
            return [op], op.shape[1], op.shape[2], True
        return [op], op.shape[0], op.shape[1], False

    a_parts, a_rows, a_pc, a_stack = describe(a)
    b_parts, b_rows, b_pc, b_stack = describe(b)
    a_cols = a_pc * (a.shape[0] if a_stack else len(a_parts))
    b_cols = b_pc * (b.shape[0] if b_stack else len(b_parts))
    k_dim, m_dim = (a_rows, a_cols) if ta else (a_cols, a_rows)
    n_dim, kb = (b_rows, b_cols) if tb else (b_cols, b_rows)
    assert kb == k_dim, (k_dim, kb)
    tm = min(tm, a_pc) if ta else min(tm, m_dim)
    tk = min(tk, k_dim, k_dim if ta else a_pc, b_pc if tb else k_dim)
    tn = min(tn, n_dim // out_split, n_dim if tb else b_pc)
    a_ct, b_ct = (tm if ta else tk), (tk if tb else tn)
    assert m_dim % tm == 0 and n_dim % tn == 0 and k_dim % tk == 0 and a_pc % a_ct == 0 and b_pc % b_ct == 0
    assert (n_dim // out_split) % tn == 0
    nk = k_dim // tk
    dims = (((0 if ta else 1,), (1 if tb else 0,)), ((), ()))
    n_a, n_b = len(a_parts), len(b_parts)
    assert n_a == 1 or n_b == 1

    def operand_specs(parts, stack, rows_t, cols_t, per, row_of, col_of):
        specs = []
        for p in range(len(parts)):
            def col(i, j, k, p=p):
                return jnp.clip(col_of(i, j, k) - p * per, 0, per - 1) if len(parts) > 1 else col_of(i, j, k)
            if stack:
                specs.append(pl.BlockSpec((None, rows_t, cols_t),
                                          lambda i, j, k, col=col: (col(i, j, k) // per, row_of(i, j, k), col(i, j, k) % per)))
            else:
                specs.append(pl.BlockSpec((rows_t, cols_t), lambda i, j, k, col=col: (row_of(i, j, k), col(i, j, k))))
        return specs

    if ta:
        a_specs = operand_specs(a_parts, a_stack, tk, tm, a_pc // tm, lambda i, j, k: k, lambda i, j, k: i)
    else:
        a_specs = operand_specs(a_parts, a_stack, tm, tk, a_pc // tk, lambda i, j, k: i, lambda i, j, k: k)
    if tb:
        b_specs = operand_specs(b_parts, b_stack, tn, tk, b_pc // tk, lambda i, j, k: j, lambda i, j, k: k)
    else:
        b_specs = operand_specs(b_parts, b_stack, tk, tn, b_pc // tn, lambda i, j, k: k, lambda i, j, k: j)

    post_fn, post_arrays = post if post is not None else (None, ())
    extras = ((add,) if add is not None else ()) + tuple(post_arrays)
    n_fixed_in = n_a + n_b + len(extras)
    grid = (m_dim // tm, n_dim // tn, nk)
    ex_args, ex_in_specs, ex_out_specs, ex_out_shapes, ex_scratch = _carry_specs(carry)

    def body(*refs):
        refs, start_when, finish_when = _carry(carry, refs, n_fixed_in, 1)
        i, j, k = pl.program_id(0), pl.program_id(1), pl.program_id(2)
        step = (i * grid[1] + j) * grid[2] + k
        start_when(step == 0)
        compute(*refs)
        finish_when(step == grid[0] * grid[1] * grid[2] - 1)

    def compute(*refs):
        a_refs, b_refs = refs[:n_a], refs[n_a:n_a + n_b]
        rest = refs[n_a + n_b:]
        extra_refs, o_ref = rest[:len(extras)], rest[len(extras)]
        acc = None if nk == 1 else rest[-1]
        i, j, k = pl.program_id(0), pl.program_id(1), pl.program_id(2)

        def finish(res):
            tiles = [r[...] for r in extra_refs]
            if add is not None:
                res = res + tiles.pop(0)
            if post_fn is not None:
                res = post_fn(res, *tiles)
            o_ref[...] = res.astype(out_dtype)

        def accumulate(a_ref, b_ref):
            prod = lax.dot_general(a_ref[...].astype(BF16), b_ref[...].astype(BF16), dims,
                                   preferred_element_type=F32)
            if nk == 1:
                finish(prod)
                return

            @pl.when(k == 0)
            def _():
                acc[...] = prod

            @pl.when(jnp.logical_and(k > 0, k < nk - 1))
            def _():
                acc[...] += prod

            @pl.when(k == nk - 1)
            def _():
                finish(acc[...] + prod)

        if n_a == 1 and n_b == 1:
            accumulate(a_refs[0], b_refs[0])
        else:
            many, block, per = (a_refs, (i if ta else k), a_pc // a_ct) if n_a > 1 else (b_refs, (k if tb else j), b_pc // b_ct)
            for p, ref in enumerate(many):
                @pl.when(block // per == p)
                def _(ref=ref):
                    accumulate(ref, b_refs[0]) if n_a > 1 else accumulate(a_refs[0], ref)

    per_out = n_dim // out_split // tn
    if out_split > 1:
        o_spec = pl.BlockSpec((None, tm, tn), lambda i, j, k: (j // per_out, i, j % per_out))
        out_shape = jax.ShapeDtypeStruct((out_split, m_dim, n_dim // out_split), out_dtype)
    else:
        o_spec = pl.BlockSpec((tm, tn), lambda i, j, k: (i, j))
        out_shape = jax.ShapeDtypeStruct((m_dim, n_dim), out_dtype)
    in_specs = a_specs + b_specs + [pl.BlockSpec((tm, tn), lambda i, j, k: (i, j)) for _ in extras]
    args = tuple(a_parts) + tuple(b_parts) + extras
    acc_scratch = [pltpu.VMEM((tm, tn), F32)] if nk > 1 else []
    if carry is None:
        return _pcall(
            body, name=name, grid=grid, in_specs=in_specs, out_specs=o_spec, out_shape=out_shape,
            scratch_shapes=acc_scratch, compiler_params=_params(("parallel", "parallel", "arbitrary")),
        )(*args)
    return _pcall(
        body, name=name, grid=grid, in_specs=in_specs + ex_in_specs, out_specs=[o_spec] + ex_out_specs,
        out_shape=[out_shape] + ex_out_shapes, scratch_shapes=acc_scratch + ex_scratch,
        compiler_params=pltpu.CompilerParams(vmem_limit_bytes=VMEM_LIMIT, has_side_effects=True,
                                             dimension_semantics=("arbitrary", "arbitrary", "arbitrary")),
    )(*args, *ex_args)


def _rows(fn, ins, outs, *, name, consts=(), accs=(), carry=None):
    n_rows = ins[0][0].shape[0]
    tr = min(ROW_TILE, n_rows)
    assert n_rows % tr == 0
    n_in, n_c, n_out = len(ins), len(consts), len(outs)
    n_steps = n_rows // tr
    ex_args, ex_in_specs, ex_out_specs, ex_out_shapes, ex_scratch = _carry_specs(carry)

    def body(*refs):
        refs, start_when, finish_when = _carry(carry, refs, n_in + n_c, n_out + len(accs))
        start_when(pl.program_id(0) == 0)
        _compute(*refs)
        finish_when(pl.program_id(0) == n_steps - 1)

    def _compute(*refs):
        vals = [r[...] for r in refs[:n_in + n_c]]
        res = fn(*vals)
        res = res if isinstance(res, (tuple, list)) else (res,)
        o_refs = refs[n_in + n_c:]
        for r, v in zip(o_refs[:n_out], res[:n_out]):
            r[...] = v.astype(r.dtype)
        if accs:
            first = pl.program_id(0) == 0
            for r, v in zip(o_refs[n_out:], res[n_out:]):
                @pl.when(first)
                def _(r=r, v=v):
                    r[...] = v

                @pl.when(jnp.logical_not(first))
                def _(r=r, v=v):
                    r[...] += v

    in_specs = [pl.BlockSpec((tr, w), functools.partial(lambda i, cb: (i, cb), cb=cb)) for _, w, cb in ins]
    in_specs += [pl.BlockSpec(c.shape, functools.partial(lambda i, nd: (0,) * nd, nd=c.ndim)) for c in consts]
    out_specs = [pl.BlockSpec((tr, w), lambda i: (i, 0)) for w, _ in outs]
    out_specs += [pl.BlockSpec(s, lambda i: (0, 0)) for s in accs]
    out_shape = [jax.ShapeDtypeStruct((n_rows, w), dt) for w, dt in outs]
    out_shape += [jax.ShapeDtypeStruct(s, F32) for s in accs]
    sequential = bool(accs) or carry is not None
    params = _params(("arbitrary",) if sequential else ("parallel",))
    if carry is not None:
        params = pltpu.CompilerParams(vmem_limit_bytes=VMEM_LIMIT, dimension_semantics=("arbitrary",), has_side_effects=True)
    return _pcall(
        body, name=name, grid=(n_steps,), in_specs=in_specs + ex_in_specs, out_specs=out_specs + ex_out_specs,
        out_shape=out_shape + ex_out_shapes, scratch_shapes=ex_scratch, compiler_params=params,
    )(*[a for a, _, _ in ins], *consts, *ex_args)


def _full(a):
    return (a, a.shape[1], 0)


def _cmul(ar, ai, br, bi):
    return ar * br - ai * bi, ar * bi + ai * br


def _seg_scans(scans, seg):
    assert seg & (seg - 1) == 0
    chains = []
    for re_ref, im_ref, a_re, a_im, reverse in scans:
        for k in range(len(a_re)):
            chains.append((re_ref, im_ref, k, jnp.broadcast_to(a_re[k], (SUBLANES, LANES)),
                           jnp.broadcast_to(a_im[k], (SUBLANES, LANES)), reverse))

    def slab(i, reverse):
        j = seg - 1 - i if reverse else i
        return pl.ds(pl.multiple_of(j * SUBLANES, SUBLANES), SUBLANES)

    def local(i, carry):
        out = []
        for n, (re_ref, im_ref, k, ar, ai, reverse) in enumerate(chains):
            hr, hi = _cmul(ar, ai, carry[2 * n], carry[2 * n + 1])
            hr = hr + re_ref[k, slab(i, reverse), :]
            hi = hi + im_ref[k, slab(i, reverse), :]
            re_ref[k, slab(i, reverse), :] = hr
            im_ref[k, slab(i, reverse), :] = hi
            out += [hr, hi]
        return tuple(out)

    zero = jnp.zeros((SUBLANES, LANES), F32)
    end = lax.fori_loop(0, seg, local, (zero,) * (2 * len(chains)))

    row = lax.broadcasted_iota(jnp.int32, (SUBLANES, LANES), 0)
    enter = []
    for n, (_, _, _, ar, ai, reverse) in enumerate(chains):
        edge = SUBLANES - 1 if reverse else 0
        shift = SUBLANES - 1 if reverse else 1
        pr, pi = ar, ai
        for _ in range(seg.bit_length() - 1):
            pr, pi = _cmul(pr, pi, pr, pi)
        tr_, ti_ = zero, zero
        for _ in range(SUBLANES - 1):
            vr, vi = _cmul(pr, pi, tr_, ti_)
            tr_ = jnp.where(row == edge, 0.0, pltpu.roll(vr + end[2 * n], shift, 0))
            ti_ = jnp.where(row == edge, 0.0, pltpu.roll(vi + end[2 * n + 1], shift, 0))
        enter += [tr_, ti_]

    def fix(i, carry):
        out = []
        for n, (re_ref, im_ref, k, ar, ai, reverse) in enumerate(chains):
            er, ei = _cmul(ar, ai, carry[2 * n], carry[2 * n + 1])
            re_ref[k, slab(i, reverse), :] += er
            im_ref[k, slab(i, reverse), :] += ei
            out += [er, ei]
        return tuple(out)

    lax.fori_loop(0, seg, fix, tuple(enter))
    per_scan, off = [], 0
    for scan in scans:
        per_scan.append(enter[off:off + 2 * len(scan[2])])
        off += 2 * len(scan[2])
    return per_scan


def _to_slab(a):
    s, w = a.shape
    return a.reshape(SUBLANES, s // SUBLANES, w).swapaxes(0, 1).reshape(s, w)


def _from_slab(a):
    s, w = a.shape
    return a.reshape(s // SUBLANES, SUBLANES, w).swapaxes(0, 1).reshape(s, w)


def _lane_blocks(v, n_k):
    return [v[:, k * LANES:(k + 1) * LANES] for k in range(n_k)]


def _gather_k(ref, rows, n_k):
    return jnp.concatenate([ref[k, rows, :] for k in range(n_k)], axis=1)


def _dot(a, b, dims=(((1,), (0,)), ((), ()))):
    return lax.dot_general(a.astype(BF16), b.astype(BF16), dims, preferred_element_type=F32)


_NT = (((1,), (1,)), ((), ()))
_TN = (((0,), (0,)), ((), ()))


def _ssm_fwd(uz, b_blk, c_blk, par, d_skip, carry=None):
    seq = uz.shape[0]
    width = d_skip.shape[1]
    ns = SSM_CH // S5_GROUP * S5_STATE
    n_k = ns // LANES
    seg = seq // SUBLANES
    tb = min(SCAN_ROWS, seq)
    n_cb = width // SSM_CH
    ex_args, ex_in_specs, ex_out_specs, ex_out_shapes, ex_scratch = _carry_specs(carry)

    def body(*refs):
        refs, start_when, finish_when = _carry(carry, refs, 5, 2)
        start_when(pl.program_id(0) == 0)
        compute(*refs)
        finish_when(pl.program_id(0) == n_cb - 1)

    def compute(u_ref, b_ref, c_ref, par_ref, d_ref, y_ref, g_ref, hre, him):
        coef_r, coef_i = par_ref[0, 2:3, :], par_ref[0, 3:4, :]
        for c0 in range(0, seq, tb):
            rows = pl.ds(c0, tb)
            ub = u_ref[rows, :]
            bur, bui = _dot(ub, b_ref[0, 0]), _dot(ub, b_ref[0, 1])
            xr, xi = coef_r * bur - coef_i * bui, coef_r * bui + coef_i * bur
            for k in range(n_k):
                hre[k, rows, :] = xr[:, k * LANES:(k + 1) * LANES]
                him[k, rows, :] = xi[:, k * LANES:(k + 1) * LANES]
        _seg_scans([(hre, him, _lane_blocks(par_ref[0, 0:1, :], n_k), _lane_blocks(par_ref[0, 1:2, :], n_k), False)], seg)
        for c0 in range(0, seq, tb):
            rows = pl.ds(c0, tb)
            y = _dot(_gather_k(hre, rows, n_k), c_ref[0, 0]) - _dot(_gather_k(him, rows, n_k), c_ref[0, 1])
            y = y + d_ref[...] * u_ref[rows, :]
            y_ref[rows, :] = y
            g_ref[rows, :] = _gelu(y).astype(BF16)

    blk = pl.BlockSpec((seq, SSM_CH), lambda i: (0, i))
    params = _params(("parallel",)) if carry is None else pltpu.CompilerParams(
        vmem_limit_bytes=VMEM_LIMIT, dimension_semantics=("arbitrary",), has_side_effects=True)
    return _pcall(
        body, name="ssm_fwd", grid=(n_cb,),
        in_specs=[blk,
                  pl.BlockSpec((1, 2, SSM_CH, ns), lambda i: (i, 0, 0, 0)),
                  pl.BlockSpec((1, 2, ns, SSM_CH), lambda i: (i, 0, 0, 0)),
                  pl.BlockSpec((1, 4, ns), lambda i: (i, 0, 0)),
                  pl.BlockSpec((1, SSM_CH), lambda i: (0, i))] + ex_in_specs,
        out_specs=[blk, blk] + ex_out_specs,
        out_shape=[jax.ShapeDtypeStruct((seq, width), F32), jax.ShapeDtypeStruct((seq, width), BF16)] + ex_out_shapes,
        scratch_shapes=[pltpu.VMEM((n_k, seq, LANES), F32), pltpu.VMEM((n_k, seq, LANES), F32)] + ex_scratch,
        compiler_params=params,
    )(uz, b_blk, c_blk, par, d_skip, *ex_args)


def _ssm_bwd(uz, dys, b_blk, ct_blk, par, d_skip, carry=None):
    seq = uz.shape[0]
    width = d_skip.shape[1]
    ns_all = SSM_CH // S5_GROUP * S5_STATE
    n_half = 2
    ns = ns_all // n_half
    n_k = ns // LANES
    seg = seq // SUBLANES
    tb = min(SCAN_ROWS, seq)
    n_cb = width // SSM_CH
    ex_args, ex_in_specs, ex_out_specs, ex_out_shapes, ex_scratch = _carry_specs(carry)

    def body(*refs):
        refs, start_when, finish_when = _carry(carry, refs, 6, 5)
        step = pl.program_id(0) * n_half + pl.program_id(1)
        start_when(step == 0)
        compute(*refs)
        finish_when(step == n_cb * n_half - 1)

    def compute(u_ref, dys_ref, b_ref, ct_ref, par_ref, d_ref,
                du_ref, dbt_ref, dct_ref, dpar_ref, dd_ref, hre, him, gre, gim):
        half = pl.program_id(1)
        a_r, a_i = par_ref[0, 0:1, :], par_ref[0, 1:2, :]
        coef_r, coef_i = par_ref[0, 2:3, :], par_ref[0, 3:4, :]

        def dys_of(rows):
            return dys_ref[rows, :]

        for c0 in range(0, seq, tb):
            rows = pl.ds(c0, tb)
            ub = u_ref[rows, :]
            bur, bui = _dot(ub, b_ref[0, 0]), _dot(ub, b_ref[0, 1])
            xr, xi = coef_r * bur - coef_i * bui, coef_r * bui + coef_i * bur
            dys = dys_of(rows)
            gr, gi = _dot(dys, ct_ref[0, 0]), -_dot(dys, ct_ref[0, 1])
            for k in range(n_k):
                lanes = slice(k * LANES, (k + 1) * LANES)
                hre[k, rows, :] = xr[:, lanes]
                him[k, rows, :] = xi[:, lanes]
                gre[k, rows, :] = gr[:, lanes]
                gim[k, rows, :] = gi[:, lanes]
        enter, _ = _seg_scans([(hre, him, _lane_blocks(a_r, n_k), _lane_blocks(a_i, n_k), False),
                               (gre, gim, _lane_blocks(a_r, n_k), _lane_blocks(-a_i, n_k), True)], seg)

        def corr(j, carry):
            acc, prev = carry
            acc_o, prev_o = [], []
            for k in range(n_k):
                sl = pl.ds(pl.multiple_of(j * SUBLANES, SUBLANES), SUBLANES)
                g_r, g_i = gre[k, sl, :], gim[k, sl, :]
                p_r, p_i = prev[2 * k], prev[2 * k + 1]
                acc_o += [acc[2 * k] + g_r * p_r + g_i * p_i, acc[2 * k + 1] + g_i * p_r - g_r * p_i]
                prev_o += [hre[k, sl, :], him[k, sl, :]]
            return tuple(acc_o), tuple(prev_o)

        zero = jnp.zeros((SUBLANES, LANES), F32)
        acc, _ = lax.fori_loop(0, seg, corr, ((zero,) * (2 * n_k), tuple(enter)), unroll=SCAN_UNROLL)
        da_r = jnp.concatenate([_colsum(acc[2 * k]) for k in range(n_k)], axis=1)
        da_i = jnp.concatenate([_colsum(acc[2 * k + 1]) for k in range(n_k)], axis=1)

        zeros_cn = jnp.zeros((SSM_CH, ns), F32)
        qt_r, qt_i, dct_r, dct_i = zeros_cn, zeros_cn, zeros_cn, zeros_cn
        dd = jnp.zeros((1, SSM_CH), F32)
        first = half == 0
        for c0 in range(0, seq, tb):
            rows = pl.ds(c0, tb)
            ub = u_ref[rows, :]
            dys = dys_of(rows)
            dct_r = dct_r + _dot(dys, _gather_k(hre, rows, n_k), _TN)
            dct_i = dct_i - _dot(dys, _gather_k(him, rows, n_k), _TN)
            g_r, g_i = _gather_k(gre, rows, n_k), _gather_k(gim, rows, n_k)
            qt_r = qt_r + _dot(ub, g_r, _TN)
            qt_i = qt_i + _dot(ub, g_i, _TN)
            dbu_r, dbu_i = coef_r * g_r + coef_i * g_i, coef_r * g_i - coef_i * g_r
            du = _dot(dbu_r, b_ref[0, 0], _NT) + _dot(dbu_i, b_ref[0, 1], _NT)
            dd = dd + _colsum(dys * ub)

            @pl.when(first)
            def _(du=du, dys=dys, rows=rows):
                du_ref[rows, :] = du + d_ref[...] * dys

            @pl.when(jnp.logical_not(first))
            def _(du=du, rows=rows):
                du_ref[rows, :] += du

        @pl.when(first)
        def _():
            dd_ref[...] = dd

        b_r, b_i = b_ref[0, 0], b_ref[0, 1]
        dbt_ref[0, 0] = coef_r * qt_r + coef_i * qt_i
        dbt_ref[0, 1] = coef_r * qt_i - coef_i * qt_r
        dct_ref[0, 0] = dct_r
        dct_ref[0, 1] = dct_i
        dpar_ref[0, 0:1, :] = da_r
        dpar_ref[0, 1:2, :] = da_i
        dpar_ref[0, 2:3, :] = _colsum(b_r * qt_r + b_i * qt_i)
        dpar_ref[0, 3:4, :] = _colsum(b_r * qt_i - b_i * qt_r)

    blk = lambda i, h: (0, i)
    params = _params(("parallel", "arbitrary")) if carry is None else pltpu.CompilerParams(
        vmem_limit_bytes=VMEM_LIMIT, dimension_semantics=("arbitrary", "arbitrary"), has_side_effects=True)
    return _pcall(
        body, name="ssm_bwd", grid=(n_cb, n_half),
        in_specs=[pl.BlockSpec((seq, SSM_CH), blk), pl.BlockSpec((seq, SSM_CH), blk),
                  pl.BlockSpec((1, 2, SSM_CH, ns), lambda i, h: (i, 0, 0, h)),
                  pl.BlockSpec((1, 2, SSM_CH, ns), lambda i, h: (i, 0, 0, h)),
                  pl.BlockSpec((1, 4, ns), lambda i, h: (i, 0, h)),
                  pl.BlockSpec((1, SSM_CH), blk)] + ex_in_specs,
        out_specs=[pl.BlockSpec((seq, SSM_CH), blk),
                   pl.BlockSpec((1, 2, SSM_CH, ns), lambda i, h: (i, 0, 0, h)),
                   pl.BlockSpec((1, 2, SSM_CH, ns), lambda i, h: (i, 0, 0, h)),
                   pl.BlockSpec((1, 4, ns), lambda i, h: (i, 0, h)),
                   pl.BlockSpec((1, SSM_CH), blk)] + ex_out_specs,
        out_shape=[jax.ShapeDtypeStruct((seq, width), F32),
                   jax.ShapeDtypeStruct((n_cb, 2, SSM_CH, ns_all), F32),
                   jax.ShapeDtypeStruct((n_cb, 2, SSM_CH, ns_all), F32),
                   jax.ShapeDtypeStruct((n_cb, 4, ns_all), F32),
                   jax.ShapeDtypeStruct((1, width), F32)] + ex_out_shapes,
        scratch_shapes=[pltpu.VMEM((n_k, seq, LANES), F32) for _ in range(4)] + ex_scratch,
        compiler_params=params,
    )(uz, dys, b_blk, ct_blk, par, d_skip, *ex_args)


def _ssm_discretize(a_re, a_im, log_dt):
    dt = jnp.exp(log_dt)[:, None]
    mag = jnp.exp(a_re * dt)
    abar_re = mag * jnp.cos(a_im * dt)
    abar_im = mag * jnp.sin(a_im * dt)
    den = a_re * a_re + a_im * a_im
    nr = abar_re - 1.0
    coef_re = (nr * a_re + abar_im * a_im) / den
    coef_im = (abar_im * a_re - nr * a_im) / den
    return abar_re, abar_im, coef_re, coef_im


def _block_diag(w_gcp):
    gpb = SSM_CH // S5_GROUP
    n_cb = w_gcp.shape[0] // gpb
    w = w_gcp.reshape(n_cb, gpb, S5_GROUP, 1, S5_STATE)
    eye = jnp.eye(gpb, dtype=w.dtype)[None, :, None, :, None]
    return (w * eye).reshape(n_cb, SSM_CH, gpb * S5_STATE)


def _block_diag_extract(w_blk):
    gpb = SSM_CH // S5_GROUP
    n_cb = w_blk.shape[0]
    w = w_blk.reshape(n_cb, gpb, S5_GROUP, gpb, S5_STATE)
    w = jnp.moveaxis(jnp.diagonal(w, axis1=1, axis2=3), -1, 1)
    return w.reshape(n_cb * gpb, S5_GROUP, S5_STATE)


def _split3(x):
    hi = x.astype(BF16)
    mid = (x - hi.astype(F32)).astype(BF16)
    lo = (x - hi.astype(F32) - mid.astype(F32)).astype(BF16)
    return hi, mid, lo


def _tri_sum(tri, x):
    hi, mid, lo = _split3(x)
    return (jnp.dot(tri, hi, preferred_element_type=F32) + jnp.dot(tri, mid, preferred_element_type=F32)
            + jnp.dot(tri, lo, preferred_element_type=F32))


def _log_sigmoid(x):
    return jnp.minimum(x, 0.0) - jnp.log(1.0 + jnp.exp(-jnp.abs(x)))


def _cum_fwd(fl, b_f):
    seq = fl.shape[0]
    t = min(CUM_TILE, seq)

    def body(fl_ref, b_ref, o_ref, carry):
        @pl.when(pl.program_id(0) == 0)
        def _():
            carry[...] = jnp.zeros_like(carry)

        r = lax.broadcasted_iota(jnp.int32, (t, t), 0)
        c = lax.broadcasted_iota(jnp.int32, (t, t), 1)
        tri = (c <= r).astype(BF16)
        cum = _tri_sum(tri, _log_sigmoid(fl_ref[...] + b_ref[...])) + carry[...]
        o_ref[...] = cum
        carry[...] = cum[t - 1:t, :]

    return _pcall(
        body, name="cum_fwd", grid=(seq // t,),
        in_specs=[pl.BlockSpec((t, LANES), lambda i: (i, 0)), pl.BlockSpec((1, LANES), lambda i: (0, 0))],
        out_specs=pl.BlockSpec((t, LANES), lambda i: (i, 0)),
        out_shape=jax.ShapeDtypeStruct((seq, LANES), F32),
        scratch_shapes=[pltpu.VMEM((1, LANES), F32)],
        compiler_params=_params(("arbitrary",)),
    )(fl, b_f)


def _cum_bwd(dcum, fl, b_f):
    seq = fl.shape[0]
    t = min(CUM_TILE, seq)
    nb = seq // t

    def body(dc_ref, fl_ref, b_ref, o_ref, db_ref, carry):
        @pl.when(pl.program_id(0) == 0)
        def _():
            carry[...] = jnp.zeros_like(carry)
            db_ref[...] = jnp.zeros_like(db_ref)

        r = lax.broadcasted_iota(jnp.int32, (t, t), 0)
        c = lax.broadcasted_iota(jnp.int32, (t, t), 1)
        tri = (c >= r).astype(BF16)
        rev = _tri_sum(tri, dc_ref[...]) + carry[...]
        carry[...] = rev[0:1, :]
        dfl = rev * _sigmoid(-(fl_ref[...] + b_ref[...]))
        o_ref[...] = dfl
        db_ref[...] += _colsum(dfl)

    return _pcall(
        body, name="cum_bwd", grid=(nb,),
        in_specs=[pl.BlockSpec((t, LANES), lambda i: (nb - 1 - i, 0)), pl.BlockSpec((t, LANES), lambda i: (nb - 1 - i, 0)),
                  pl.BlockSpec((1, LANES), lambda i: (0, 0))],
        out_specs=[pl.BlockSpec((t, LANES), lambda i: (nb - 1 - i, 0)), pl.BlockSpec((1, LANES), lambda i: (0, 0))],
        out_shape=[jax.ShapeDtypeStruct((seq, LANES), F32), jax.ShapeDtypeStruct((1, LANES), F32)],
        scratch_shapes=[pltpu.VMEM((1, LANES), F32)],
        compiler_params=_params(("arbitrary",)),
    )(dcum, fl, b_f)


def _att_scores(q, kb, ck, row0, col0, masked):
    s = _dot(q, kb, _NT) - ck
    if masked:
        rows = row0 + lax.broadcasted_iota(jnp.int32, s.shape, 0)
        cols = col0 + lax.broadcasted_iota(jnp.int32, s.shape, 1)
        s = jnp.where(cols <= rows, s, NEG_INF)
    return s


def _pairwise_loop(lo, hi, step_fn, init):
    n = hi - lo
    w = ATT_UNROLL

    def several(p, carry):
        for u in range(w):
            carry = step_fn(lo + w * p + u, carry)
        return carry

    carry = lax.fori_loop(0, n // w, several, init)
    return lax.fori_loop(lo + (n // w) * w, hi, step_fn, carry)


def _att_fwd(qz, kv, ck):
    seq = qz.shape[0]
    heads = ck.shape[0]
    t = min(ATT_TILE, seq)
    scale = HEAD_DIM ** -0.5

    def body(q_ref, z_ref, k_ref, v_ref, ck_ref, o_ref, og_ref, lse_ref):
        i = pl.program_id(1)
        q = (q_ref[...] * scale).astype(BF16)

        def block(j, carry, masked):
            m, l, acc = carry
            rows = pl.ds(pl.multiple_of(j * t, t), t)
            s = _att_scores(q, k_ref[rows, :], ck_ref[0, j], i * t, j * t, masked)
            m_new = jnp.maximum(m, jnp.max(s, axis=1, keepdims=True))
            p = jnp.exp(s - m_new)
            alpha = jnp.exp(m - m_new)
            p_hi = p.astype(BF16)
            p_lo = (p - p_hi.astype(F32)).astype(BF16)
            vb = v_ref[rows, :]
            return (m_new, alpha * l + jnp.sum(p, axis=1, keepdims=True),
                    alpha * acc + (_dot(p_hi, vb) + _dot(p_lo, vb)))

        init = (jnp.full((t, 1), NEG_INF, F32), jnp.zeros((t, 1), F32), jnp.zeros((t, HEAD_DIM), F32))
        carry = _pairwise_loop(0, i, functools.partial(block, masked=False), init)
        m, l, acc = block(i, carry, True)
        o = acc / l
        o_ref[...] = o
        og_ref[...] = (o * _silu(z_ref[...])).astype(BF16)
        lse_ref[0] = m + jnp.log(l)

    qblk = pl.BlockSpec((t, HEAD_DIM), lambda h, i: (i, h))
    return _pcall(
        body, name="att_fwd", grid=(heads, seq // t),
        in_specs=[qblk, pl.BlockSpec((t, HEAD_DIM), lambda h, i: (i, heads + h)),
                  pl.BlockSpec((seq, HEAD_DIM), lambda h, i: (0, h)),
                  pl.BlockSpec((seq, HEAD_DIM), lambda h, i: (0, heads + h)),
                  pl.BlockSpec((1, seq // t, 1, t), lambda h, i: (h, 0, 0, 0))],
        out_specs=[qblk, qblk, pl.BlockSpec((1, t, 1), lambda h, i: (h, i, 0))],
        out_shape=[jax.ShapeDtypeStruct((seq, heads * HEAD_DIM), F32), jax.ShapeDtypeStruct((seq, heads * HEAD_DIM), BF16),
                   jax.ShapeDtypeStruct((heads, seq, 1), F32)],
        compiler_params=_params(("parallel", "parallel")),
    )(qz, qz, kv, kv, ck)


def _att_bwd(qz, kv, do, ox, lse, ck):
    seq = qz.shape[0]
    heads = ck.shape[0]
    t = min(ATT_TILE, seq)
    nq = seq // t
    scale = HEAD_DIM ** -0.5

    def body(q_ref, k_ref, v_ref, do_ref, ox_ref, lse_ref, ck_ref, dq_ref, dk_ref, dv_ref, dck_ref, dq_acc, delta):
        j = pl.program_id(1)

        @pl.when(j == 0)
        def _():
            for c0 in range(0, seq, t):
                rows = pl.ds(c0, t)
                delta[rows, :] = jnp.sum(do_ref[rows, :].astype(BF16).astype(F32) * ox_ref[rows, :], axis=1, keepdims=True)
                dq_acc[rows, :] = jnp.zeros((t, HEAD_DIM), F32)

        kb, vb = k_ref[...], v_ref[...]
        ckv = ck_ref[0, 0]

        def block(i, carry, masked):
            dk, dv, dck = carry
            rows = pl.ds(pl.multiple_of(i * t, t), t)
            qb = (q_ref[rows, :] * scale).astype(BF16)
            dob = do_ref[rows, :].astype(BF16)
            s = _att_scores(qb, kb, ckv, i * t, j * t, masked)
            p = jnp.exp(s - lse_ref[0, rows, :])
            ds = p * (_dot(dob, vb, _NT) - delta[rows, :])
            dq_acc[rows, :] += _dot(ds, kb)
            return dk + _dot(ds, qb, _TN), dv + _dot(p, dob, _TN), dck - _colsum(ds)

        init = (jnp.zeros((t, HEAD_DIM), F32), jnp.zeros((t, HEAD_DIM), F32), jnp.zeros((1, t), F32))
        carry = block(j, init, True)
        dk, dv, dck = _pairwise_loop(j + 1, nq, functools.partial(block, masked=False), carry)
        dk_ref[...] = dk.astype(dk_ref.dtype)
        dv_ref[...] = dv.astype(dv_ref.dtype)
        dck_ref[0, 0] = dck

        @pl.when(j == nq - 1)
        def _():
            for c0 in range(0, seq, t):
                rows = pl.ds(c0, t)
                dq_ref[rows, :] = (dq_acc[rows, :] * scale).astype(dq_ref.dtype)

    head = pl.BlockSpec((seq, HEAD_DIM), lambda h, j: (0, h))
    col = pl.BlockSpec((1, seq, 1), lambda h, j: (h, 0, 0))
    kblk = pl.BlockSpec((t, HEAD_DIM), lambda h, j: (j, h))
    row = pl.BlockSpec((1, 1, 1, t), lambda h, j: (h, j, 0, 0))
    return _pcall(
        body, name="att_bwd", grid=(heads, nq),
        in_specs=[head, kblk, pl.BlockSpec((t, HEAD_DIM), lambda h, j: (j, heads + h)), head, head, col, row],
        out_specs=[head, kblk, kblk, row],
        out_shape=[jax.ShapeDtypeStruct((seq, heads * HEAD_DIM), BF16)] * 3 + [jax.ShapeDtypeStruct((heads, nq, 1, t), F32)],
        scratch_shapes=[pltpu.VMEM((seq, HEAD_DIM), F32), pltpu.VMEM((seq, 1), F32)],
        compiler_params=_params(("parallel", "arbitrary")),
    )(qz, kv, kv, do, ox, lse, ck)


def _mesh_pos():
    return lax.axis_index("x"), lax.axis_index("y"), lax.axis_index("c")


def _other_chips(x, y):
    return [(1 - x, y), (x, 1 - y), (1 - x, 1 - y)]


def _all_gather_weights(big, small):
    nb, ns = len(big), len(small)
    n_remote = 3 * (nb + ns)

    def plan(ins, outs, sems):
        send_sems, recv_sems, fwd_send, fwd_recv = sems
        x, y, c = _mesh_pos()
        chips = _other_chips(x, y)
        slots = [2 * cx + cy for cx, cy in chips]

        def half(ref, hc):
            rh = ref.shape[-2] // 2
            return ref.at[pl.ds(hc * rh, rh), :]

        def remote(i, j, src_chip, from_in):
            if i < nb:
                src = half(ins[i], c) if from_in else half(outs[i].at[src_chip], c)
                dst = half(outs[i].at[src_chip], c)
            else:
                src = ins[i] if from_in else outs[i].at[src_chip]
                dst = outs[i].at[src_chip]
            k = 3 * i + j
            return pltpu.make_async_remote_copy(src_ref=src, dst_ref=dst, send_sem=send_sems.at[k],
                                                recv_sem=recv_sems.at[k], device_id=(*chips[j], c),
                                                device_id_type=MESH_ID)

        def forward(i, j, hc):
            part = half(outs[i].at[slots[j]], hc)
            k = 3 * i + j
            return pltpu.make_async_remote_copy(src_ref=part, dst_ref=part, send_sem=fwd_send.at[k],
                                                recv_sem=fwd_recv.at[k], device_id=(x, y, 1 - c),
                                                device_id_type=MESH_ID)

        return remote, forward, 2 * x + y, slots, c

    def start(ins, outs, sems):
        remote, _, me, _, _ = plan(ins, outs, sems)
        for i in range(nb + ns):
            for j in range(3):
                remote(i, j, me, True).start()

    def finish(ins, outs, sems):
        remote, forward, me, slots, c = plan(ins, outs, sems)
        for i in range(nb + ns):
            for j in range(3):
                remote(i, j, slots[j], False).wait_recv()
                if i < nb:
                    forward(i, j, c).start()
        for i in range(nb):
            for j in range(3):
                forward(i, j, 1 - c).wait_recv()
        for i in range(nb + ns):
            for j in range(3):
                remote(i, j, me, True).wait_send()
                if i < nb:
                    forward(i, j, c).wait_send()

    arrays = tuple(big) + tuple(small)
    return _Exchange(
        arrays=arrays,
        out_shapes=tuple(jax.ShapeDtypeStruct((N_CHIPS,) + a.shape, a.dtype) for a in arrays),
        scratch=(pltpu.SemaphoreType.DMA((n_remote,)), pltpu.SemaphoreType.DMA((n_remote,)),
                 pltpu.SemaphoreType.DMA((3 * max(nb, 1),)), pltpu.SemaphoreType.DMA((3 * max(nb, 1),))),
        start=start, finish=finish)


def _swap_halves(grads):
    n = len(grads)

    def copies(ins, outs, sems):
        x, y, c = _mesh_pos()
        cps = []
        for i in range(n):
            rh = ins[i].shape[1] // 2
            cps.append(pltpu.make_async_remote_copy(
                src_ref=ins[i].at[:, pl.ds((1 - c) * rh, rh), :], dst_ref=outs[i], send_sem=sems[0].at[i],
                recv_sem=sems[1].at[i], device_id=(x, y, 1 - c), device_id_type=MESH_ID))
        return cps

    def start(ins, outs, sems):
        for cp in copies(ins, outs, sems):
            cp.start()

    def finish(ins, outs, sems):
        for cp in copies(ins, outs, sems):
            cp.wait()

    return _Exchange(
        arrays=tuple(grads),
        out_shapes=tuple(jax.ShapeDtypeStruct((g.shape[0], g.shape[1] // 2, g.shape[2]), g.dtype) for g in grads),
        scratch=(pltpu.SemaphoreType.DMA((n,)), pltpu.SemaphoreType.DMA((n,))),
        start=start, finish=finish)


def _pair_sum_bf16(g, theirs, pos, name):
    n, rh, cdim = theirs.shape
    tr = min(ROW_TILE, rh)
    nb = rh // tr

    def body(pos_ref, g_ref, t_ref, o_ref):
        o_ref[...] = (g_ref[...] + t_ref[...]).astype(BF16)

    slot = lambda s, pos: (pos[0] + 1 + s) % n
    grid_spec = pltpu.PrefetchScalarGridSpec(
        num_scalar_prefetch=1, grid=(n - 1, nb),
        in_specs=[pl.BlockSpec((None, tr, cdim), lambda s, i, pos: (slot(s, pos), pos[1] * nb + i, 0)),
                  pl.BlockSpec((None, tr, cdim), lambda s, i, pos: (slot(s, pos), i, 0))],
        out_specs=pl.BlockSpec((None, tr, cdim), lambda s, i, pos: (slot(s, pos), i, 0)))
    return _pcall(body, name=name, grid_spec=grid_spec, out_shape=jax.ShapeDtypeStruct(theirs.shape, BF16),
                  compiler_params=_params(("parallel", "parallel")))(pos, g, theirs)


def _chip_sum(g, theirs, recv, pos, name):
    n, rh, cdim = theirs.shape
    tr = min(ROW_TILE, rh)
    nb = rh // tr

    def body(pos_ref, g_ref, t_ref, r0, r1, r2, o_ref):
        o_ref[...] = (((g_ref[...] + t_ref[...]) + r0[...]) + r1[...]) + r2[...]

    grid_spec = pltpu.PrefetchScalarGridSpec(
        num_scalar_prefetch=1, grid=(nb,),
        in_specs=[pl.BlockSpec((None, tr, cdim), lambda i, pos: (pos[0], pos[1] * nb + i, 0)),
                  pl.BlockSpec((None, tr, cdim), lambda i, pos: (pos[0], i, 0))]
        + [pl.BlockSpec((None, tr, cdim), functools.partial(lambda i, pos, j: (j, i, 0), j=j)) for j in range(3)],
        out_specs=pl.BlockSpec((tr, cdim), lambda i, pos: (pos[1] * nb + i, 0)))
    return _pcall(body, name=name, grid_spec=grid_spec, out_shape=jax.ShapeDtypeStruct((2 * rh, cdim), F32),
                  compiler_params=_params(("parallel",)))(pos, g, theirs, recv, recv, recv)


def _scatter_to_owner(parts):
    n = len(parts)

    def copies(ins, outs, sems):
        x, y, c = _mesh_pos()
        chips = _other_chips(x, y)
        cps = []
        for i in range(n):
            for j in range(3):
                k = 3 * i + j
                cps.append(pltpu.make_async_remote_copy(
                    src_ref=ins[i].at[2 * chips[j][0] + chips[j][1]], dst_ref=outs[i].at[j],
                    send_sem=sems[0].at[k], recv_sem=sems[1].at[k], device_id=(*chips[j], c),
                    device_id_type=MESH_ID))
        return cps

    def start(ins, outs, sems):
        for cp in copies(ins, outs, sems):
            cp.start()

    def finish(ins, outs, sems):
        for cp in copies(ins, outs, sems):
            cp.wait()

    return _Exchange(
        arrays=tuple(parts),
        out_shapes=tuple(jax.ShapeDtypeStruct((3,) + p.shape[1:], p.dtype) for p in parts),
        scratch=(pltpu.SemaphoreType.DMA((3 * n,)), pltpu.SemaphoreType.DMA((3 * n,))),
        start=start, finish=finish)


def _finish_reductions(v, shards):
    n = len(shards)
    n_rows = v.shape[0]
    rh = n_rows // 2
    assert rh % SUBLANES == 0

    def body(*refs):
        v_ref, o_ref, outs = refs[0], refs[1 + n], refs[2 + n:2 + 2 * n]
        part, recv, send_sems, recv_sems, join_send, join_recv = refs[2 + 2 * n:]
        x, y, c = _mesh_pos()
        sibling = (x, y, 1 - c)

        def join(i, hc):
            half = outs[i].shape[0] // 2
            rows = outs[i].at[pl.ds(hc * half, half), :]
            return pltpu.make_async_remote_copy(src_ref=rows, dst_ref=rows, send_sem=join_send.at[i],
                                                recv_sem=join_recv.at[i], device_id=sibling, device_id_type=MESH_ID)

        for i in range(n):
            join(i, c).start()

        mine = pl.ds(pl.multiple_of(c * rh, SUBLANES), rh)
        theirs = pl.ds(pl.multiple_of((1 - c) * rh, SUBLANES), rh)

        def exchange(s, src, dst, peer):
            cp = pltpu.make_async_remote_copy(src_ref=src, dst_ref=dst, send_sem=send_sems.at[s],
                                              recv_sem=recv_sems.at[s], device_id=peer, device_id_type=MESH_ID)
            cp.start()
            cp.wait()

        exchange(0, v_ref.at[theirs, :], recv.at[0], sibling)
        part[...] = v_ref[mine, :] + recv[0]
        for s, peer in ((1, (1 - x, y, c)), (2, (x, 1 - y, c))):
            exchange(s, part, recv.at[s], peer)
            part[...] = part[...] + recv[s]
        o_ref[mine, :] = part[...]
        exchange(3, part, o_ref.at[mine, :], sibling)

        for i in range(n):
            join(i, c).wait_send()
            join(i, 1 - c).wait_recv()

    vm = pl.BlockSpec(memory_space=pltpu.VMEM)
    res = _pcall(
        body, name="finish_reductions", in_specs=[vm] + [ANY] * n, out_specs=[vm] + [ANY] * n,
        out_shape=[jax.ShapeDtypeStruct(v.shape, v.dtype)] + [jax.ShapeDtypeStruct(s.shape, s.dtype) for s in shards],
        input_output_aliases={1 + i: 1 + i for i in range(n)},
        scratch_shapes=[pltpu.VMEM((rh, LANES), v.dtype), pltpu.VMEM((3, rh, LANES), v.dtype),
                        pltpu.SemaphoreType.DMA((4,)), pltpu.SemaphoreType.DMA((4,)),
                        pltpu.SemaphoreType.DMA((n,)), pltpu.SemaphoreType.DMA((n,))],
        compiler_params=pltpu.CompilerParams(vmem_limit_bytes=VMEM_LIMIT, has_side_effects=True),
    )(v, *shards)
    return res[0], res[1:]


def _adamw_math(w, g, m, v):
    m = ADAM_B1 * m + (1.0 - ADAM_B1) * g
    v = ADAM_B2 * v + (1.0 - ADAM_B2) * (g * g)
    m_hat = m / (1.0 - ADAM_B1 ** ADAM_STEP)
    v_hat = v / (1.0 - ADAM_B2 ** ADAM_STEP)
    delta = -ADAM_LR * (m_hat / (jnp.sqrt(v_hat) + ADAM_EPS) + ADAM_WD * w)
    return delta, m, v


def _adamw(w, g, m, v, name):
    wd = w.shape[1]
    return _rows(_adamw_math, [_full(w), _full(g), _full(m), _full(v)], [(wd, F32)] * 3, name=name)


def _rows_of(a):
    return -(-a.size // (LANES * SUBLANES)) * SUBLANES


def _pack(arrs, fill=0.0):
    parts = []
    for a in arrs:
        flat = a.reshape(-1)
        flat = jnp.pad(flat, (0, _rows_of(a) * LANES - a.size), constant_values=fill)
        parts.append(flat.reshape(-1, LANES))
    used = sum(p.shape[0] for p in parts)
    rows = -(-used // ROW_TILE) * ROW_TILE
    parts.append(jnp.full((rows - used, LANES), fill, F32))
    return jnp.concatenate(parts, axis=0)


def _unpack(buf, like):
    out, off = [], 0
    for a in like:
        out.append(buf[off:off + _rows_of(a)].reshape(-1)[:a.size].reshape(a.shape))
        off += _rows_of(a)
    return out


def kernel(x, norm_pre, norm_post, s5_w_in, s5_a_re, s5_a_im, s5_log_dt, s5_b_re, s5_b_im, s5_c_re, s5_c_im, s5_d, s5_w_glu, s5_b_glu, s5_w_out, kv_norm, kv_w, kv_b_f, fox_w_in, fox_w_out, loss_target, m_norm_pre, m_norm_post, m_s5_w_in, m_s5_a_re, m_s5_a_im, m_s5_log_dt, m_s5_b_re, m_s5_b_im, m_s5_c_re, m_s5_c_im, m_s5_d, m_s5_w_glu, m_s5_b_glu, m_s5_w_out, m_kv_norm, m_kv_w, m_kv_b_f, m_fox_w_in, m_fox_w_out, v_norm_pre, v_norm_post, v_s5_w_in, v_s5_a_re, v_s5_a_im, v_s5_log_dt, v_s5_b_re, v_s5_b_im, v_s5_c_re, v_s5_c_im, v_s5_d, v_s5_w_glu, v_s5_b_glu, v_s5_w_out, v_kv_norm, v_kv_w, v_kv_b_f, v_fox_w_in, v_fox_w_out):
    seq, dm = x.shape[1], x.shape[2]
    width = dm
    heads = dm // HEAD_DIM
    fw = heads * HEAD_DIM
    groups = width // S5_GROUP
    chip = 2 * lax.axis_index("x") + lax.axis_index("y")

    big_shards = [s5_w_in[0], s5_w_glu[0], s5_w_out[0], kv_w, fox_w_in[0], fox_w_out[0]]
    own_shards = [w.astype(BF16) for w in big_shards] + [s5_d, s5_b_glu]
    fill_own = lambda gs, owns: [lax.dynamic_update_slice(g, own[None], (chip, 0, 0)) for g, own in zip(gs, owns)]
    c_idx = lax.axis_index("c")
    pos = jnp.stack([chip, c_idx]).astype(jnp.int32)
    h0 = x[0]
    target = loss_target[0]
    g_pre0, g_pre1 = norm_pre[0:1], norm_pre[1:2]
    g_post0, g_post1 = norm_post[0:1], norm_post[1:2]
    g_kv = kv_norm.reshape(1, dm)
    first_owns = [own_shards[0], s5_d, s5_b_glu]
    xn1, *first_gathered = _rows(lambda h, g: (h * _rstd(h) * g,), [_full(h0)], [(dm, BF16)], consts=[g_pre0], name="norm_pre0",
                                 carry=_all_gather_weights(first_owns[:1], first_owns[1:]))
    g_win, g_d, g_bglu = fill_own(first_gathered, first_owns)
    gather_rest = _all_gather_weights(own_shards[1:5], [])
    gather_last = _all_gather_weights(own_shards[5:6], [])
    cols = lambda g: jnp.moveaxis(g, 0, 1).reshape(g.shape[1], -1)
    rows = lambda g: g.reshape(-1, g.shape[2])
    w_in = g_win
    d_skip, b_glu = cols(g_d), cols(g_bglu)
    b_f = jnp.pad(kv_b_f, (0, LANES - heads)).reshape(1, LANES)

    a_re, a_im, log_dt = s5_a_re[0], s5_a_im[0], s5_log_dt[0]
    disc, disc_vjp = jax.vjp(_ssm_discretize, a_re, a_im, log_dt)
    gpb = SSM_CH // S5_GROUP
    n_cb = groups // gpb
    par = jnp.stack([p.reshape(n_cb, gpb * S5_STATE) for p in disc], axis=1)
    b_t = lambda b: jnp.swapaxes(b, 1, 2)
    b_blk = jnp.stack([_block_diag(b_t(s5_b_re[0])), _block_diag(b_t(s5_b_im[0]))], axis=1)
    ct_blk = jnp.stack([_block_diag(s5_c_re[0]), _block_diag(s5_c_im[0])], axis=1)
    c_blk = jnp.swapaxes(ct_blk, 2, 3)

    xn1 = _to_slab(xn1)
    uz = _mm(xn1, w_in, name="s5_in")
    ys, y1b, *rest = _ssm_fwd(uz, b_blk, c_blk, par, d_skip, carry=gather_rest)
    g_wglu, g_wout, g_kvw, g_fwin = fill_own(rest, own_shards[1:5])
    w_glu, w_out = rows(g_wglu), rows(g_wout)
    kvw_full = cols(g_kvw)
    w_kv = kvw_full[:, :2 * fw]
    w_f = jnp.pad(kvw_full[:, 2 * fw:], ((0, 0), (0, LANES - heads)))
    fw_in = g_fwin
    glu_a = _mm(y1b, w_glu, name="s5_glu")

    def gate_fn(y, a, z, b):
        return (_gelu(y) * _sigmoid(a + b) * _silu(z),)

    y3b = _rows(gate_fn, [_full(ys), _full(glu_a), (uz, width, 1)], [(width, BF16)], consts=[b_glu], name="s5_gate")[0]
    o1 = _from_slab(_mm(y3b, w_out, name="s5_out"))

    def mid_fn(h, o, gp, gk, gq):
        h1 = h + o * _rstd(o) * gp
        r = _rstd(h1)
        return h1, h1 * r * gk, h1 * r * gq

    h1, xk, xn2 = _rows(mid_fn, [_full(h0), _full(o1)], [(dm, F32), (dm, BF16), (dm, BF16)],
                        consts=[g_post0, g_kv, g_pre1], name="mid_norms")

    kv, g_fwout = _mm(xk, w_kv, out_dtype=BF16, name="kv_proj", carry=gather_last)
    fw_out = rows(fill_own([g_fwout], own_shards[5:6])[0])
    fl = _mm(xk, w_f, name="f_proj")
    qz = _mm(xn2, fw_in, name="fox_in")
    cum = _cum_fwd(fl, b_f)
    t_att = min(ATT_TILE, seq)
    cum_t = cum[:, :heads].T
    ck = cum_t.reshape(heads, seq // t_att, 1, t_att)
    o, o2b, lse = _att_fwd(qz, kv, ck)
    o3 = _mm(o2b, fw_out, name="fox_out")

    def loss_fn(h, o, t, g):
        r = _rstd(o)
        err = h + o * r * g - t
        dh = err * (1.0 / dm)
        do, dg = _rms_bwd(o, g, dh)
        part = 0.5 * jnp.sum(jnp.mean(err * err, axis=-1, keepdims=True), axis=0, keepdims=True)
        return dh, do, jnp.broadcast_to(part, (1, LANES)), _colsum(dg)

    dh2, do3, loss_part, dg_post1 = _rows(loss_fn, [_full(h1), _full(o3), _full(target)], [(dm, F32), (dm, BF16)],
                                          consts=[g_post1], accs=[(1, LANES), (1, dm)], name="loss_head")
    loss = lax.psum(loss_part[0, 0], MESH_AXES)

    do2 = _mm(do3, fw_out, tb=True, name="fox_out_dx")
    dw_fout = _mm(o2b, do3, ta=True, name="fox_out_dw")

    def fox_gate_bwd(d, a, z):
        return d * _silu(z), d * a * _dsilu(z)

    do, dz2 = _rows(fox_gate_bwd, [_full(do2), _full(o), (qz, fw, 1)], [(fw, F32), (fw, BF16)], name="fox_gate_bwd")
    dq, dk, dv, dck = _att_bwd(qz, kv, do, o, lse, ck)
    dcum = jnp.pad(dck.reshape(heads, seq).T, ((0, 0), (0, LANES - heads)))
    dfl, db_f = _cum_bwd(dcum, fl, b_f)
    dqz = (dq, dz2)
    dkv = (dk, dv)
    dxn2 = _mm(dqz, fw_in, tb=True, name="fox_in_dx")
    dw_fin = _mm(xn2, dqz, ta=True, out_split=N_CHIPS, name="fox_in_dw")
    dxk_f = _mm(dfl, w_f, tb=True, name="f_proj_dx")
    dxk = _mm(dkv, w_kv, tb=True, add=dxk_f, name="kv_proj_dx")
    dw_kv = _mm(xk, dkv, ta=True, name="kv_proj_dw")
    dw_f = _mm(xk, dfl, ta=True, name="f_proj_dw")

    def mid_bwd(d2, h, dq_, dk_, o, gq, gk, gp):
        dxa, dga = _rms_bwd(h, gq, dq_)
        dxb, dgb = _rms_bwd(h, gk, dk_)
        dh = d2 + dxa + dxb
        do_, dgp = _rms_bwd(o, gp, dh)
        return dh, do_, _colsum(dga), _colsum(dgb), _colsum(dgp)

    to_cols = lambda g: jnp.moveaxis(g.reshape(g.shape[0], N_CHIPS, -1), 1, 0)
    to_rows = lambda g: g.reshape(N_CHIPS, -1, g.shape[1])
    kv_cols = kv_w.shape[1]
    kv_col_block = lambda b: (dw_kv[:, b * kv_cols:(b + 1) * kv_cols] if (b + 1) * kv_cols <= 2 * fw else
                              jnp.concatenate([dw_kv[:, b * kv_cols:], dw_f[:, :heads]], axis=1))
    early_grads = [jnp.stack([kv_col_block(b) for b in range(N_CHIPS)]), dw_fin, to_rows(dw_fout)]
    dh1, do1, dg_pre1, dg_kv, dg_post0, *early_theirs = _rows(
        mid_bwd, [_full(dh2), _full(h1), _full(dxn2), _full(dxk), _full(o1)], [(dm, F32), (dm, BF16)],
        consts=[g_pre1, g_kv, g_post0], accs=[(1, dm)] * 3, name="mid_norms_bwd", carry=_swap_halves(early_grads))
    early_sums = [_pair_sum_bf16(g, t, pos, f"grad_pair_sum_{3 + i}") for i, (g, t) in enumerate(zip(early_grads, early_theirs))]

    do1 = _to_slab(do1)
    dy3 = _mm(do1, w_out, tb=True, name="s5_out_dx")
    dw_out = _mm(y3b, do1, ta=True, name="s5_out_dw")

    def gate_bwd(d3, y, a, z, b):
        y1 = _gelu(y)
        gate = _sigmoid(a + b)
        dy2 = d3 * _silu(z)
        da = dy2 * y1 * gate * (1.0 - gate)
        return dy2 * gate, da, d3 * (y1 * gate) * _dsilu(z), _colsum(da)

    dy1_direct, da, dz, db_glu = _rows(gate_bwd, [_full(dy3), _full(ys), _full(glu_a), (uz, width, 1)],
                                       [(width, F32), (width, BF16), (width, BF16)], consts=[b_glu],
                                       accs=[(1, width)], name="s5_gate_bwd")
    dys = _mm(da, w_glu, tb=True, add=dy1_direct, post=(lambda d, y: d * _dgelu(y), [ys]), name="s5_glu_dx")
    dw_glu = _mm(y1b, da, ta=True, name="s5_glu_dw")
    mid_grads = [to_rows(dw_glu), to_rows(dw_out)]
    du, dbt_blk, dct_blk, dpar, dd, *carried = _ssm_bwd(
        uz, dys, b_blk, ct_blk, par, d_skip, carry=_together(_scatter_to_owner(early_sums), _swap_halves(mid_grads)))
    early_recv, mid_theirs = carried[:3], carried[3:]
    mid_sums = [_pair_sum_bf16(g, t, pos, f"grad_pair_sum_{1 + i}") for i, (g, t) in enumerate(zip(mid_grads, mid_theirs))]
    duz = (du, dz)
    dw_in, *mid_recv = _mm(xn1, duz, ta=True, out_split=N_CHIPS, name="s5_in_dw", carry=_scatter_to_owner(mid_sums))
    late_grads = [dw_in]
    dxn1, *late_theirs = _mm(duz, w_in, tb=True, name="s5_in_dx", carry=_swap_halves(late_grads))
    dxn1 = _from_slab(dxn1)
    late_sums = [_pair_sum_bf16(dw_in, late_theirs[0], pos, "grad_pair_sum_0")]

    def first_bwd(d1, h, dxn, g):
        dx, dg = _rms_bwd(h, g, dxn)
        return d1 + dx, _colsum(dg)

    grad_x, dg_pre0, *late_recv = _rows(first_bwd, [_full(dh1), _full(h0), _full(dxn1)], [(dm, F32)], consts=[g_pre0],
                                        accs=[(1, dm)], name="norm_pre0_bwd", carry=_scatter_to_owner(late_sums))

    dpar_g = [dpar[:, i, :].reshape(groups, S5_STATE) for i in range(4)]
    da_re, da_im, dlog_dt = disc_vjp(tuple(dpar_g))
    db_re = jnp.swapaxes(_block_diag_extract(dbt_blk[:, 0]), 1, 2)
    db_im = jnp.swapaxes(_block_diag_extract(dbt_blk[:, 1]), 1, 2)
    dc_re = _block_diag_extract(dct_blk[:, 0])
    dc_im = _block_diag_extract(dct_blk[:, 1])

    big_grads = late_grads + mid_grads + early_grads
    theirs = list(late_theirs) + list(mid_theirs) + list(early_theirs)
    received = list(late_recv) + list(mid_recv) + list(early_recv)
    halves = [_chip_sum(g, t, r, pos, f"grad_chip_sum_{i}") for i, (g, t, r) in enumerate(zip(big_grads, theirs, received))]

    small_local = [jnp.concatenate([dg_pre0, dg_pre1]), jnp.concatenate([dg_post0, dg_post1]),
                   da_re[None], da_im[None], dlog_dt[None], db_re[None], db_im[None], dc_re[None], dc_im[None],
                   dd, db_glu, dg_kv.reshape(dm), db_f[0, :heads]]
    small_buf, (g_win_s, g_wglu_s, g_wout_s, g_kvw_s, g_fwin_s, g_fwout_s) = _finish_reductions(_pack(small_local), halves)
    (g_norm_pre, g_norm_post, g_a_re, g_a_im, g_log_dt, g_b_re, g_b_im, g_c_re, g_c_im, g_d_full, g_bglu_full,
     g_kv_norm, g_b_f) = _unpack(small_buf, small_local)
    shard = width // N_CHIPS
    g_d_own = lax.dynamic_slice(g_d_full, (0, chip * shard), (1, shard))
    g_bglu_own = lax.dynamic_slice(g_bglu_full, (0, chip * shard), (1, shard))

    big_w = big_shards
    big_g = [g_win_s, g_wglu_s, g_wout_s, g_kvw_s, g_fwin_s, g_fwout_s]
    big_m = [m_s5_w_in[0], m_s5_w_glu[0], m_s5_w_out[0], m_kv_w, m_fox_w_in[0], m_fox_w_out[0]]
    big_v = [v_s5_w_in[0], v_s5_w_glu[0], v_s5_w_out[0], v_kv_w, v_fox_w_in[0], v_fox_w_out[0]]
    big_upd = [_adamw(w, g, m, v, f"adamw_{i}") for i, (w, g, m, v) in enumerate(zip(big_w, big_g, big_m, big_v))]

    small_names = ["norm_pre", "norm_post", "s5_a_re", "s5_a_im", "s5_log_dt", "s5_b_re", "s5_b_im", "s5_c_re", "s5_c_im",
                   "s5_d", "s5_b_glu", "kv_norm", "kv_b_f"]
    small_w = [norm_pre, norm_post, s5_a_re, s5_a_im, s5_log_dt, s5_b_re, s5_b_im, s5_c_re, s5_c_im, s5_d, s5_b_glu, kv_norm, kv_b_f]
    small_m = [m_norm_pre, m_norm_post, m_s5_a_re, m_s5_a_im, m_s5_log_dt, m_s5_b_re, m_s5_b_im, m_s5_c_re, m_s5_c_im, m_s5_d, m_s5_b_glu, m_kv_norm, m_kv_b_f]
    small_v = [v_norm_pre, v_norm_post, v_s5_a_re, v_s5_a_im, v_s5_log_dt, v_s5_b_re, v_s5_b_im, v_s5_c_re, v_s5_c_im, v_s5_d, v_s5_b_glu, v_kv_norm, v_kv_b_f]
    small_g = [g_norm_pre, g_norm_post, g_a_re, g_a_im, g_log_dt, g_b_re, g_b_im, g_c_re, g_c_im, g_d_own, g_bglu_own, g_kv_norm, g_b_f]
    small_g = [g.reshape(w.shape) for g, w in zip(small_g, small_w)]
    sd, sm, sv = _adamw(_pack(small_w), _pack(small_g), _pack(small_m), _pack(small_v, fill=1.0), "adamw_small")
    small_delta, small_newm, small_newv = _unpack(sd, small_w), _unpack(sm, small_w), _unpack(sv, small_w)

    order = ["norm_pre", "norm_post", "s5_w_in", "s5_a_re", "s5_a_im", "s5_log_dt", "s5_b_re", "s5_b_im", "s5_c_re", "s5_c_im",
             "s5_d", "s5_w_glu", "s5_b_glu", "s5_w_out", "kv_norm", "kv_w", "kv_b_f", "fox_w_in", "fox_w_out"]
    big_names = ["s5_w_in", "s5_w_glu", "s5_w_out", "kv_w", "fox_w_in", "fox_w_out"]
    big_like = [s5_w_in, s5_w_glu, s5_w_out, kv_w, fox_w_in, fox_w_out]
    grads, deltas, new_m, new_v = {}, {}, {}, {}
    for i, n in enumerate(big_names):
        shp = big_like[i].shape
        grads[n] = big_g[i].reshape(shp)
        deltas[n], new_m[n], new_v[n] = (a.reshape(shp) for a in big_upd[i])
    for i, n in enumerate(small_names):
        grads[n], deltas[n], new_m[n], new_v[n] = small_g[i], small_delta[i], small_newm[i], small_newv[i]

    return (loss, grad_x[None], *[grads[n] for n in order], *[deltas[n] for n in order],
            *[new_m[n] for n in order], *[new_v[n] for n in order])
```

```python
import functools
import math
from typing import Callable, NamedTuple

import jax
import jax.numpy as jnp
from jax import lax
from jax.experimental import pallas as pl
from jax.experimental.pallas import tpu as pltpu

F32 = jnp.float32
BF16 = jnp.bfloat16

D_MODEL = 2048
SEQ = 4096
S5_GROUP = 16
S5_STATE = 64
HEAD_DIM = 128
RMS_EPS = 1e-6
NEG_INF = -1e30
ADAM_LR = 0.001
ADAM_B1 = 0.9
ADAM_B2 = 0.999
ADAM_EPS = 1e-08
ADAM_WD = 0.01
ADAM_STEP = 10

LANES = 128
SUBLANES = 8
VMEM_LIMIT = 56 * 1024 * 1024
N_CHIPS = 4
MESH_AXES = ("x", "y", "c")
MESH_ID = pl.DeviceIdType.MESH

SSM_CH = 128
ROW_TILE = 256
SCAN_ROWS = 512
SCAN_UNROLL = 4
ATT_TILE = 512
ATT_UNROLL = 2
CUM_TILE = 512


def _pcall(body, **kw):
    return pl.pallas_call(body, **kw)


def _params(sem=None):
    if sem is None:
        return pltpu.CompilerParams(vmem_limit_bytes=VMEM_LIMIT)
    return pltpu.CompilerParams(vmem_limit_bytes=VMEM_LIMIT, dimension_semantics=sem)


def _sigmoid(x):
    return 1.0 / (1.0 + jnp.exp(-x))


def _silu(z):
    return z * _sigmoid(z)


def _dsilu(z):
    s = _sigmoid(z)
    return s * (1.0 + z * (1.0 - s))


_GELU_C = math.sqrt(2.0 / math.pi)


def _gelu(x):
    return 0.5 * x * (1.0 + jnp.tanh(_GELU_C * (x + 0.044715 * x * x * x)))


def _dgelu(x):
    t = jnp.tanh(_GELU_C * (x + 0.044715 * x * x * x))
    return 0.5 * (1.0 + t) + 0.5 * x * (1.0 - t * t) * _GELU_C * (1.0 + 3.0 * 0.044715 * x * x)


def _rstd(x):
    return lax.rsqrt(jnp.mean(x * x, axis=-1, keepdims=True) + RMS_EPS)


def _rms_bwd(x, g, dy):
    r = _rstd(x)
    dyg = dy * g
    dx = r * dyg - x * (r * r * r) * jnp.mean(dyg * x, axis=-1, keepdims=True)
    return dx, dy * (x * r)


def _colsum(v):
    return jnp.sum(v, axis=0, keepdims=True)


ANY = pl.BlockSpec(memory_space=pl.ANY)


class _Exchange(NamedTuple):
    arrays: tuple
    out_shapes: tuple
    scratch: tuple
    start: Callable
    finish: Callable


def _together(*exs):
    def parts(seq, field):
        out, off = [], 0
        for e in exs:
            n = len(getattr(e, field))
            out.append(seq[off:off + n])
            off += n
        return out

    def start(ins, outs, sems):
        for e, i, o, s in zip(exs, parts(ins, "arrays"), parts(outs, "out_shapes"), parts(sems, "scratch")):
            e.start(i, o, s)

    def finish(ins, outs, sems):
        for e, i, o, s in zip(exs, parts(ins, "arrays"), parts(outs, "out_shapes"), parts(sems, "scratch")):
            e.finish(i, o, s)

    return _Exchange(arrays=sum((tuple(e.arrays) for e in exs), ()), out_shapes=sum((tuple(e.out_shapes) for e in exs), ()),
                     scratch=sum((tuple(e.scratch) for e in exs), ()), start=start, finish=finish)


def _exchange_call(ex, name):
    n_in, n_out = len(ex.arrays), len(ex.out_shapes)

    def body(*refs):
        ins, outs, sems = refs[:n_in], refs[n_in:n_in + n_out], refs[n_in + n_out:]
        ex.start(ins, outs, sems)
        ex.finish(ins, outs, sems)

    return _pcall(body, name=name, in_specs=[ANY] * n_in, out_specs=[ANY] * n_out, out_shape=list(ex.out_shapes),
                  scratch_shapes=list(ex.scratch), compiler_params=pltpu.CompilerParams(has_side_effects=True))(*ex.arrays)


def _carry(ex, refs, n_fixed_in, n_fixed_out):
    if ex is None:
        return refs, lambda cond: None, lambda cond: None
    n_in, n_out, n_sem = len(ex.arrays), len(ex.out_shapes), len(ex.scratch)
    fixed_in = refs[:n_fixed_in]
    ex_in = refs[n_fixed_in:n_fixed_in + n_in]
    rest = refs[n_fixed_in + n_in:]
    fixed_out = rest[:n_fixed_out]
    ex_out = rest[n_fixed_out:n_fixed_out + n_out]
    scratch = rest[n_fixed_out + n_out:]
    sems = scratch[len(scratch) - n_sem:]

    def start_when(cond):
        pl.when(cond)(lambda: ex.start(ex_in, ex_out, sems))

    def finish_when(cond):
        pl.when(cond)(lambda: ex.finish(ex_in, ex_out, sems))

    return tuple(fixed_in) + tuple(fixed_out) + tuple(scratch[:len(scratch) - n_sem]), start_when, finish_when


def _carry_specs(ex):
    if ex is None:
        return (), [], [], [], []
    return ex.arrays, [ANY] * len(ex.arrays), [ANY] * len(ex.out_shapes), list(ex.out_shapes), list(ex.scratch)


def _mm(a, b, *, ta=False, tb=False, out_dtype=F32, add=None, post=None, out_split=1, tm=1024, tn=1024, tk=2048, name,
        carry=None):
    def describe(op):
        if isinstance(op, (tuple, list)):
            assert all(p.ndim == 2 and p.shape == op[0].shape for p in op)
            return list(op), op[0].shape[0], op[0].shape[1], False
        if op.ndim == 3:
            return [op], op.shape[1], op.shape[2], True
        return [op], op.shape[0], op.shape[1], False

    a_parts, a_rows, a_pc, a_stack = describe(a)
    b_parts, b_rows, b_pc, b_stack = describe(b)
    a_cols = a_pc * (a.shape[0] if a_stack else len(a_parts))
    b_cols = b_pc * (b.shape[0] if b_stack else len(b_parts))
    k_dim, m_dim = (a_rows, a_cols) if ta else (a_cols, a_rows)
    n_dim, kb = (b_rows, b_cols) if tb else (b_cols, b_rows)
    assert kb == k_dim, (k_dim, kb)
    tm = min(tm, a_pc) if ta else min(tm, m_dim)
    tk = min(tk, k_dim, k_dim if ta else a_pc, b_pc if tb else k_dim)
    tn = min(tn, n_dim // out_split, n_dim if tb else b_pc)
    a_ct, b_ct = (tm if ta else tk), (tk if tb else tn)
    assert m_dim % tm == 0 and n_dim % tn == 0 and k_dim % tk == 0 and a_pc % a_ct == 0 and b_pc % b_ct == 0
    assert (n_dim // out_split) % tn == 0
    nk = k_dim // tk
    dims = (((0 if ta else 1,), (1 if tb else 0,)), ((), ()))
    n_a, n_b = len(a_parts), len(b_parts)
    assert n_a == 1 or n_b == 1

    def operand_specs(parts, stack, rows_t, cols_t, per, row_of, col_of):
        specs = []
        for p in range(len(parts)):
            def col(i, j, k, p=p):
                return jnp.clip(col_of(i, j, k) - p * per, 0, per - 1) if len(parts) > 1 else col_of(i, j, k)
            if stack:
                specs.append(pl.BlockSpec((None, rows_t, cols_t),
                                          lambda i, j, k, col=col: (col(i, j, k) // per, row_of(i, j, k), col(i, j, k) % per)))
            else:
                specs.append(pl.BlockSpec((rows_t, cols_t), lambda i, j, k, col=col: (row_of(i, j, k), col(i, j, k))))
        return specs

    if ta:
        a_specs = operand_specs(a_parts, a_stack, tk, tm, a_pc // tm, lambda i, j, k: k, lambda i, j, k: i)
    else:
        a_specs = operand_specs(a_parts, a_stack, tm, tk, a_pc // tk, lambda i, j, k: i, lambda i, j, k: k)
    if tb:
        b_specs = operand_specs(b_parts, b_stack, tn, tk, b_pc // tk, lambda i, j, k: j, lambda i, j, k: k)
    else:
        b_specs = operand_specs(b_parts, b_stack, tk, tn, b_pc // tn, lambda i, j, k: k, lambda i, j, k: j)

    post_fn, post_arrays, post_dtypes = post if post is not None else (None, (), None)
    out_dtypes = tuple(post_dtypes) if post_dtypes else (out_dtype,)
    extras = ((add, 0),) * (add is not None) + tuple(e if isinstance(e, tuple) else (e, 0) for e in post_arrays)
    n_fixed_in = n_a + n_b + len(extras)
    n_out = len(out_dtypes)
    grid = (m_dim // tm, n_dim // tn, nk)
    ex_args, ex_in_specs, ex_out_specs, ex_out_shapes, ex_scratch = _carry_specs(carry)

    def body(*refs):
        refs, start_when, finish_when = _carry(carry, refs, n_fixed_in, n_out)
        i, j, k = pl.program_id(0), pl.program_id(1), pl.program_id(2)
        step = (i * grid[1] + j) * grid[2] + k
        start_when(step == 0)
        compute(*refs)
        finish_when(step == grid[0] * grid[1] * grid[2] - 1)

    def compute(*refs):
        a_refs, b_refs = refs[:n_a], refs[n_a:n_a + n_b]
        rest = refs[n_a + n_b:]
        extra_refs, o_refs = rest[:len(extras)], rest[len(extras):len(extras) + n_out]
        acc = None if nk == 1 else rest[-1]
        i, j, k = pl.program_id(0), pl.program_id(1), pl.program_id(2)

        def finish(res):
            tiles = [r[...] for r in extra_refs]
            if add is not None:
                res = res + tiles.pop(0)
            if post_fn is not None:
                res = post_fn(res, *tiles)
            res = res if isinstance(res, (tuple, list)) else (res,)
            for o_ref, r in zip(o_refs, res):
                o_ref[...] = r.astype(o_ref.dtype)

        def accumulate(a_ref, b_ref):
            prod = lax.dot_general(a_ref[...].astype(BF16), b_ref[...].astype(BF16), dims,
                                   preferred_element_type=F32)
            if nk == 1:
                finish(prod)
                return

            @pl.when(k == 0)
            def _():
                acc[...] = prod

            @pl.when(jnp.logical_and(k > 0, k < nk - 1))
            def _():
                acc[...] += prod

            @pl.when(k == nk - 1)
            def _():
                finish(acc[...] + prod)

        if n_a == 1 and n_b == 1:
            accumulate(a_refs[0], b_refs[0])
        else:
            many, block, per = (a_refs, (i if ta else k), a_pc // a_ct) if n_a > 1 else (b_refs, (k if tb else j), b_pc // b_ct)
            for p, ref in enumerate(many):
                @pl.when(block // per == p)
                def _(ref=ref):
                    accumulate(ref, b_refs[0]) if n_a > 1 else accumulate(a_refs[0], ref)

    per_out = n_dim // out_split // tn
    if out_split > 1:
        o_spec = pl.BlockSpec((None, tm, tn), lambda i, j, k: (j // per_out, i, j % per_out))
        out_shapes = [jax.ShapeDtypeStruct((out_split, m_dim, n_dim // out_split), dt) for dt in out_dtypes]
    else:
        o_spec = pl.BlockSpec((tm, tn), lambda i, j, k: (i, j))
        out_shapes = [jax.ShapeDtypeStruct((m_dim, n_dim), dt) for dt in out_dtypes]

    def extra_spec(arr, first_col):
        assert first_col % tn == 0
        off = first_col // tn
        if arr.shape[0] == 1:
            return pl.BlockSpec((1, tn), lambda i, j, k: (0, j + off))
        return pl.BlockSpec((tm, tn), lambda i, j, k: (i, j + off))

    in_specs = a_specs + b_specs + [extra_spec(arr, off) for arr, off in extras]
    args = tuple(a_parts) + tuple(b_parts) + tuple(arr for arr, _ in extras)
    acc_scratch = [pltpu.VMEM((tm, tn), F32)] if nk > 1 else []
    if carry is None:
        res = _pcall(
            body, name=name, grid=grid, in_specs=in_specs, out_specs=[o_spec] * n_out, out_shape=out_shapes,
            scratch_shapes=acc_scratch, compiler_params=_params(("parallel", "parallel", "arbitrary")),
        )(*args)
        return res[0] if n_out == 1 else res
    return _pcall(
        body, name=name, grid=grid, in_specs=in_specs + ex_in_specs, out_specs=[o_spec] * n_out + ex_out_specs,
        out_shape=out_shapes + ex_out_shapes, scratch_shapes=acc_scratch + ex_scratch,
        compiler_params=pltpu.CompilerParams(vmem_limit_bytes=VMEM_LIMIT, has_side_effects=True,
                                             dimension_semantics=("arbitrary", "arbitrary", "arbitrary")),
    )(*args, *ex_args)


def _rows(fn, ins, outs, *, name, consts=(), accs=(), carry=None):
    n_rows = ins[0][0].shape[0]
    tr = min(ROW_TILE, n_rows)
    assert n_rows % tr == 0
    n_in, n_c, n_out = len(ins), len(consts), len(outs)
    n_steps = n_rows // tr
    ex_args, ex_in_specs, ex_out_specs, ex_out_shapes, ex_scratch = _carry_specs(carry)

    def body(*refs):
        refs, start_when, finish_when = _carry(carry, refs, n_in + n_c, n_out + len(accs))
        start_when(pl.program_id(0) == 0)
        _compute(*refs)
        finish_when(pl.program_id(0) == n_steps - 1)

    def _compute(*refs):
        vals = [r[...] for r in refs[:n_in + n_c]]
        res = fn(*vals)
        res = res if isinstance(res, (tuple, list)) else (res,)
        o_refs = refs[n_in + n_c:]
        for r, v in zip(o_refs[:n_out], res[:n_out]):
            r[...] = v.astype(r.dtype)
        if accs:
            first = pl.program_id(0) == 0
            for r, v in zip(o_refs[n_out:], res[n_out:]):
                @pl.when(first)
                def _(r=r, v=v):
                    r[...] = v

                @pl.when(jnp.logical_not(first))
                def _(r=r, v=v):
                    r[...] += v

    in_specs = [pl.BlockSpec((tr, w), functools.partial(lambda i, cb: (i, cb), cb=cb)) for _, w, cb in ins]
    in_specs += [pl.BlockSpec(c.shape, functools.partial(lambda i, nd: (0,) * nd, nd=c.ndim)) for c in consts]
    out_specs = [pl.BlockSpec((tr, w), lambda i: (i, 0)) for w, _ in outs]
    out_specs += [pl.BlockSpec(s, lambda i: (0, 0)) for s in accs]
    out_shape = [jax.ShapeDtypeStruct((n_rows, w), dt) for w, dt in outs]
    out_shape += [jax.ShapeDtypeStruct(s, F32) for s in accs]
    sequential = bool(accs) or carry is not None
    params = _params(("arbitrary",) if sequential else ("parallel",))
    if carry is not None:
        params = pltpu.CompilerParams(vmem_limit_bytes=VMEM_LIMIT, dimension_semantics=("arbitrary",), has_side_effects=True)
    return _pcall(
        body, name=name, grid=(n_steps,), in_specs=in_specs + ex_in_specs, out_specs=out_specs + ex_out_specs,
        out_shape=out_shape + ex_out_shapes, scratch_shapes=ex_scratch, compiler_params=params,
    )(*[a for a, _, _ in ins], *consts, *ex_args)


def _full(a):
    return (a, a.shape[1], 0)


def _cmul(ar, ai, br, bi):
    return ar * br - ai * bi, ar * bi + ai * br


def _seg_scans(scans, seg):
    assert seg & (seg - 1) == 0
    chains = []
    for re_ref, im_ref, a_re, a_im, reverse in scans:
        for k in range(len(a_re)):
            chains.append((re_ref, im_ref, k, jnp.broadcast_to(a_re[k], (SUBLANES, LANES)),
                           jnp.broadcast_to(a_im[k], (SUBLANES, LANES)), reverse))

    def slab(i, reverse):
        j = seg - 1 - i if reverse else i
        return pl.ds(pl.multiple_of(j * SUBLANES, SUBLANES), SUBLANES)

    def local(i, carry):
        out = []
        for n, (re_ref, im_ref, k, ar, ai, reverse) in enumerate(chains):
            hr, hi = _cmul(ar, ai, carry[2 * n], carry[2 * n + 1])
            hr = hr + re_ref[k, slab(i, reverse), :]
            hi = hi + im_ref[k, slab(i, reverse), :]
            re_ref[k, slab(i, reverse), :] = hr
            im_ref[k, slab(i, reverse), :] = hi
            out += [hr, hi]
        return tuple(out)

    zero = jnp.zeros((SUBLANES, LANES), F32)
    end = lax.fori_loop(0, seg, local, (zero,) * (2 * len(chains)))

    row = lax.broadcasted_iota(jnp.int32, (SUBLANES, LANES), 0)
    enter = []
    for n, (_, _, _, ar, ai, reverse) in enumerate(chains):
        edge = SUBLANES - 1 if reverse else 0
        shift = SUBLANES - 1 if reverse else 1
        pr, pi = ar, ai
        for _ in range(seg.bit_length() - 1):
            pr, pi = _cmul(pr, pi, pr, pi)
        tr_, ti_ = zero, zero
        for _ in range(SUBLANES - 1):
            vr, vi = _cmul(pr, pi, tr_, ti_)
            tr_ = jnp.where(row == edge, 0.0, pltpu.roll(vr + end[2 * n], shift, 0))
            ti_ = jnp.where(row == edge, 0.0, pltpu.roll(vi + end[2 * n + 1], shift, 0))
        enter += [tr_, ti_]

    def fix(i, carry):
        out = []
        for n, (re_ref, im_ref, k, ar, ai, reverse) in enumerate(chains):
            er, ei = _cmul(ar, ai, carry[2 * n], carry[2 * n + 1])
            re_ref[k, slab(i, reverse), :] += er
            im_ref[k, slab(i, reverse), :] += ei
            out += [er, ei]
        return tuple(out)

    lax.fori_loop(0, seg, fix, tuple(enter))
    per_scan, off = [], 0
    for scan in scans:
        per_scan.append(enter[off:off + 2 * len(scan[2])])
        off += 2 * len(scan[2])
    return per_scan


def _to_slab(a):
    s, w = a.shape
    return a.reshape(SUBLANES, s // SUBLANES, w).swapaxes(0, 1).reshape(s, w)


def _from_slab(a):
    s, w = a.shape
    return a.reshape(s // SUBLANES, SUBLANES, w).swapaxes(0, 1).reshape(s, w)


def _lane_blocks(v, n_k):
    return [v[:, k * LANES:(k + 1) * LANES] for k in range(n_k)]


def _gather_k(ref, rows, n_k):
    return jnp.concatenate([ref[k, rows, :] for k in range(n_k)], axis=1)


def _dot(a, b, dims=(((1,), (0,)), ((), ()))):
    return lax.dot_general(a.astype(BF16), b.astype(BF16), dims, preferred_element_type=F32)


_NT = (((1,), (1,)), ((), ()))
_TN = (((0,), (0,)), ((), ()))


def _ssm_fwd(uz, b_blk, c_blk, par, d_skip, carry=None):
    seq = uz.shape[0]
    width = d_skip.shape[1]
    ns = SSM_CH // S5_GROUP * S5_STATE
    n_k = ns // LANES
    seg = seq // SUBLANES
    tb = min(SCAN_ROWS, seq)
    n_cb = width // SSM_CH
    ex_args, ex_in_specs, ex_out_specs, ex_out_shapes, ex_scratch = _carry_specs(carry)

    def body(*refs):
        refs, start_when, finish_when = _carry(carry, refs, 5, 2)
        start_when(pl.program_id(0) == 0)
        compute(*refs)
        finish_when(pl.program_id(0) == n_cb - 1)

    def compute(u_ref, b_ref, c_ref, par_ref, d_ref, y_ref, g_ref, hre, him):
        coef_r, coef_i = par_ref[0, 2:3, :], par_ref[0, 3:4, :]
        for c0 in range(0, seq, tb):
            rows = pl.ds(c0, tb)
            ub = u_ref[rows, :]
            bur, bui = _dot(ub, b_ref[0, 0]), _dot(ub, b_ref[0, 1])
            xr, xi = coef_r * bur - coef_i * bui, coef_r * bui + coef_i * bur
            for k in range(n_k):
                hre[k, rows, :] = xr[:, k * LANES:(k + 1) * LANES]
                him[k, rows, :] = xi[:, k * LANES:(k + 1) * LANES]
        _seg_scans([(hre, him, _lane_blocks(par_ref[0, 0:1, :], n_k), _lane_blocks(par_ref[0, 1:2, :], n_k), False)], seg)
        for c0 in range(0, seq, tb):
            rows = pl.ds(c0, tb)
            y = _dot(_gather_k(hre, rows, n_k), c_ref[0, 0]) - _dot(_gather_k(him, rows, n_k), c_ref[0, 1])
            y = y + d_ref[...] * u_ref[rows, :]
            y_ref[rows, :] = y
            g_ref[rows, :] = _gelu(y).astype(BF16)

    blk = pl.BlockSpec((seq, SSM_CH), lambda i: (0, i))
    params = _params(("parallel",)) if carry is None else pltpu.CompilerParams(
        vmem_limit_bytes=VMEM_LIMIT, dimension_semantics=("arbitrary",), has_side_effects=True)
    return _pcall(
        body, name="ssm_fwd", grid=(n_cb,),
        in_specs=[blk,
                  pl.BlockSpec((1, 2, SSM_CH, ns), lambda i: (i, 0, 0, 0)),
                  pl.BlockSpec((1, 2, ns, SSM_CH), lambda i: (i, 0, 0, 0)),
                  pl.BlockSpec((1, 4, ns), lambda i: (i, 0, 0)),
                  pl.BlockSpec((1, SSM_CH), lambda i: (0, i))] + ex_in_specs,
        out_specs=[blk, blk] + ex_out_specs,
        out_shape=[jax.ShapeDtypeStruct((seq, width), F32), jax.ShapeDtypeStruct((seq, width), BF16)] + ex_out_shapes,
        scratch_shapes=[pltpu.VMEM((n_k, seq, LANES), F32), pltpu.VMEM((n_k, seq, LANES), F32)] + ex_scratch,
        compiler_params=params,
    )(uz, b_blk, c_blk, par, d_skip, *ex_args)


def _ssm_bwd(uz, dys, b_blk, ct_blk, par, d_skip, carry=None):
    seq = uz.shape[0]
    width = d_skip.shape[1]
    ns_all = SSM_CH // S5_GROUP * S5_STATE
    n_half = 2
    ns = ns_all // n_half
    n_k = ns // LANES
    seg = seq // SUBLANES
    tb = min(SCAN_ROWS, seq)
    n_cb = width // SSM_CH
    ex_args, ex_in_specs, ex_out_specs, ex_out_shapes, ex_scratch = _carry_specs(carry)

    def body(*refs):
        refs, start_when, finish_when = _carry(carry, refs, 6, 5)
        step = pl.program_id(0) * n_half + pl.program_id(1)
        start_when(step == 0)
        compute(*refs)
        finish_when(step == n_cb * n_half - 1)

    def compute(u_ref, dys_ref, b_ref, ct_ref, par_ref, d_ref,
                du_ref, dbt_ref, dct_ref, dpar_ref, dd_ref, hre, him, gre, gim):
        half = pl.program_id(1)
        a_r, a_i = par_ref[0, 0:1, :], par_ref[0, 1:2, :]
        coef_r, coef_i = par_ref[0, 2:3, :], par_ref[0, 3:4, :]

        def dys_of(rows):
            return dys_ref[rows, :]

        for c0 in range(0, seq, tb):
            rows = pl.ds(c0, tb)
            ub = u_ref[rows, :]
            bur, bui = _dot(ub, b_ref[0, 0]), _dot(ub, b_ref[0, 1])
            xr, xi = coef_r * bur - coef_i * bui, coef_r * bui + coef_i * bur
            dys = dys_of(rows)
            gr, gi = _dot(dys, ct_ref[0, 0]), -_dot(dys, ct_ref[0, 1])
            for k in range(n_k):
                lanes = slice(k * LANES, (k + 1) * LANES)
                hre[k, rows, :] = xr[:, lanes]
                him[k, rows, :] = xi[:, lanes]
                gre[k, rows, :] = gr[:, lanes]
                gim[k, rows, :] = gi[:, lanes]
        enter, _ = _seg_scans([(hre, him, _lane_blocks(a_r, n_k), _lane_blocks(a_i, n_k), False),
                               (gre, gim, _lane_blocks(a_r, n_k), _lane_blocks(-a_i, n_k), True)], seg)

        def corr(j, carry):
            acc, prev = carry
            acc_o, prev_o = [], []
            for k in range(n_k):
                sl = pl.ds(pl.multiple_of(j * SUBLANES, SUBLANES), SUBLANES)
                g_r, g_i = gre[k, sl, :], gim[k, sl, :]
                p_r, p_i = prev[2 * k], prev[2 * k + 1]
                acc_o += [acc[2 * k] + g_r * p_r + g_i * p_i, acc[2 * k + 1] + g_i * p_r - g_r * p_i]
                prev_o += [hre[k, sl, :], him[k, sl, :]]
            return tuple(acc_o), tuple(prev_o)

        zero = jnp.zeros((SUBLANES, LANES), F32)
        acc, _ = lax.fori_loop(0, seg, corr, ((zero,) * (2 * n_k), tuple(enter)), unroll=SCAN_UNROLL)
        da_r = jnp.concatenate([_colsum(acc[2 * k]) for k in range(n_k)], axis=1)
        da_i = jnp.concatenate([_colsum(acc[2 * k + 1]) for k in range(n_k)], axis=1)

        zeros_cn = jnp.zeros((SSM_CH, ns), F32)
        qt_r, qt_i, dct_r, dct_i = zeros_cn, zeros_cn, zeros_cn, zeros_cn
        dd = jnp.zeros((1, SSM_CH), F32)
        first = half == 0
        for c0 in range(0, seq, tb):
            rows = pl.ds(c0, tb)
            ub = u_ref[rows, :]
            dys = dys_of(rows)
            dct_r = dct_r + _dot(dys, _gather_k(hre, rows, n_k), _TN)
            dct_i = dct_i - _dot(dys, _gather_k(him, rows, n_k), _TN)
            g_r, g_i = _gather_k(gre, rows, n_k), _gather_k(gim, rows, n_k)
            qt_r = qt_r + _dot(ub, g_r, _TN)
            qt_i = qt_i + _dot(ub, g_i, _TN)
            dbu_r, dbu_i = coef_r * g_r + coef_i * g_i, coef_r * g_i - coef_i * g_r
            du = _dot(dbu_r, b_ref[0, 0], _NT) + _dot(dbu_i, b_ref[0, 1], _NT)
            dd = dd + _colsum(dys * ub)

            @pl.when(first)
            def _(du=du, dys=dys, rows=rows):
                du_ref[rows, :] = du + d_ref[...] * dys

            @pl.when(jnp.logical_not(first))
            def _(du=du, rows=rows):
                du_ref[rows, :] += du

        @pl.when(first)
        def _():
            dd_ref[...] = dd

        b_r, b_i = b_ref[0, 0], b_ref[0, 1]
        dbt_ref[0, 0] = coef_r * qt_r + coef_i * qt_i
        dbt_ref[0, 1] = coef_r * qt_i - coef_i * qt_r
        dct_ref[0, 0] = dct_r
        dct_ref[0, 1] = dct_i
        dpar_ref[0, 0:1, :] = da_r
        dpar_ref[0, 1:2, :] = da_i
        dpar_ref[0, 2:3, :] = _colsum(b_r * qt_r + b_i * qt_i)
        dpar_ref[0, 3:4, :] = _colsum(b_r * qt_i - b_i * qt_r)

    blk = lambda i, h: (0, i)
    params = _params(("parallel", "arbitrary")) if carry is None else pltpu.CompilerParams(
        vmem_limit_bytes=VMEM_LIMIT, dimension_semantics=("arbitrary", "arbitrary"), has_side_effects=True)
    return _pcall(
        body, name="ssm_bwd", grid=(n_cb, n_half),
        in_specs=[pl.BlockSpec((seq, SSM_CH), blk), pl.BlockSpec((seq, SSM_CH), blk),
                  pl.BlockSpec((1, 2, SSM_CH, ns), lambda i, h: (i, 0, 0, h)),
                  pl.BlockSpec((1, 2, SSM_CH, ns), lambda i, h: (i, 0, 0, h)),
                  pl.BlockSpec((1, 4, ns), lambda i, h: (i, 0, h)),
                  pl.BlockSpec((1, SSM_CH), blk)] + ex_in_specs,
        out_specs=[pl.BlockSpec((seq, SSM_CH), blk),
                   pl.BlockSpec((1, 2, SSM_CH, ns), lambda i, h: (i, 0, 0, h)),
                   pl.BlockSpec((1, 2, SSM_CH, ns), lambda i, h: (i, 0, 0, h)),
                   pl.BlockSpec((1, 4, ns), lambda i, h: (i, 0, h)),
                   pl.BlockSpec((1, SSM_CH), blk)] + ex_out_specs,
        out_shape=[jax.ShapeDtypeStruct((seq, width), F32),
                   jax.ShapeDtypeStruct((n_cb, 2, SSM_CH, ns_all), F32),
                   jax.ShapeDtypeStruct((n_cb, 2, SSM_CH, ns_all), F32),
                   jax.ShapeDtypeStruct((n_cb, 4, ns_all), F32),
                   jax.ShapeDtypeStruct((1, width), F32)] + ex_out_shapes,
        scratch_shapes=[pltpu.VMEM((n_k, seq, LANES), F32) for _ in range(4)] + ex_scratch,
        compiler_params=params,
    )(uz, dys, b_blk, ct_blk, par, d_skip, *ex_args)


def _ssm_discretize(a_re, a_im, log_dt):
    dt = jnp.exp(log_dt)[:, None]
    mag = jnp.exp(a_re * dt)
    abar_re = mag * jnp.cos(a_im * dt)
    abar_im = mag * jnp.sin(a_im * dt)
    den = a_re * a_re + a_im * a_im
    nr = abar_re - 1.0
    coef_re = (nr * a_re + abar_im * a_im) / den
    coef_im = (abar_im * a_re - nr * a_im) / den
    return abar_re, abar_im, coef_re, coef_im


def _block_diag(w_gcp):
    gpb = SSM_CH // S5_GROUP
    n_cb = w_gcp.shape[0] // gpb
    w = w_gcp.reshape(n_cb, gpb, S5_GROUP, 1, S5_STATE)
    eye = jnp.eye(gpb, dtype=w.dtype)[None, :, None, :, None]
    return (w * eye).reshape(n_cb, SSM_CH, gpb * S5_STATE)


def _block_diag_extract(w_blk):
    gpb = SSM_CH // S5_GROUP
    n_cb = w_blk.shape[0]
    w = w_blk.reshape(n_cb, gpb, S5_GROUP, gpb, S5_STATE)
    w = jnp.moveaxis(jnp.diagonal(w, axis1=1, axis2=3), -1, 1)
    return w.reshape(n_cb * gpb, S5_GROUP, S5_STATE)


def _split3(x):
    hi = x.astype(BF16)
    mid = (x - hi.astype(F32)).astype(BF16)
    lo = (x - hi.astype(F32) - mid.astype(F32)).astype(BF16)
    return hi, mid, lo


def _tri_sum(tri, x):
    hi, mid, lo = _split3(x)
    return (jnp.dot(tri, hi, preferred_element_type=F32) + jnp.dot(tri, mid, preferred_element_type=F32)
            + jnp.dot(tri, lo, preferred_element_type=F32))


def _log_sigmoid(x):
    return jnp.minimum(x, 0.0) - jnp.log(1.0 + jnp.exp(-jnp.abs(x)))


def _cum_fwd(fl, b_f):
    seq = fl.shape[0]
    t = min(CUM_TILE, seq)

    def body(fl_ref, b_ref, o_ref, carry):
        @pl.when(pl.program_id(0) == 0)
        def _():
            carry[...] = jnp.zeros_like(carry)

        r = lax.broadcasted_iota(jnp.int32, (t, t), 0)
        c = lax.broadcasted_iota(jnp.int32, (t, t), 1)
        tri = (c <= r).astype(BF16)
        cum = _tri_sum(tri, _log_sigmoid(fl_ref[...] + b_ref[...])) + carry[...]
        o_ref[...] = cum
        carry[...] = cum[t - 1:t, :]

    return _pcall(
        body, name="cum_fwd", grid=(seq // t,),
        in_specs=[pl.BlockSpec((t, LANES), lambda i: (i, 0)), pl.BlockSpec((1, LANES), lambda i: (0, 0))],
        out_specs=pl.BlockSpec((t, LANES), lambda i: (i, 0)),
        out_shape=jax.ShapeDtypeStruct((seq, LANES), F32),
        scratch_shapes=[pltpu.VMEM((1, LANES), F32)],
        compiler_params=_params(("arbitrary",)),
    )(fl, b_f)


def _cum_bwd(dcum, fl, b_f):
    seq = fl.shape[0]
    t = min(CUM_TILE, seq)
    nb = seq // t

    def body(dc_ref, fl_ref, b_ref, o_ref, db_ref, carry):
        @pl.when(pl.program_id(0) == 0)
        def _():
            carry[...] = jnp.zeros_like(carry)
            db_ref[...] = jnp.zeros_like(db_ref)

        r = lax.broadcasted_iota(jnp.int32, (t, t), 0)
        c = lax.broadcasted_iota(jnp.int32, (t, t), 1)
        tri = (c >= r).astype(BF16)
        rev = _tri_sum(tri, dc_ref[...]) + carry[...]
        carry[...] = rev[0:1, :]
        dfl = rev * _sigmoid(-(fl_ref[...] + b_ref[...]))
        o_ref[...] = dfl
        db_ref[...] += _colsum(dfl)

    return _pcall(
        body, name="cum_bwd", grid=(nb,),
        in_specs=[pl.BlockSpec((t, LANES), lambda i: (nb - 1 - i, 0)), pl.BlockSpec((t, LANES), lambda i: (nb - 1 - i, 0)),
                  pl.BlockSpec((1, LANES), lambda i: (0, 0))],
        out_specs=[pl.BlockSpec((t, LANES), lambda i: (nb - 1 - i, 0)), pl.BlockSpec((1, LANES), lambda i: (0, 0))],
        out_shape=[jax.ShapeDtypeStruct((seq, LANES), F32), jax.ShapeDtypeStruct((1, LANES), F32)],
        scratch_shapes=[pltpu.VMEM((1, LANES), F32)],
        compiler_params=_params(("arbitrary",)),
    )(dcum, fl, b_f)


def _att_scores(q, kb, ck, row0, col0, masked):
    s = _dot(q, kb, _NT) - ck
    if masked:
        rows = row0 + lax.broadcasted_iota(jnp.int32, s.shape, 0)
        cols = col0 + lax.broadcasted_iota(jnp.int32, s.shape, 1)
        s = jnp.where(cols <= rows, s, NEG_INF)
    return s


def _pairwise_loop(lo, hi, step_fn, init):
    n = hi - lo
    w = ATT_UNROLL

    def several(p, carry):
        for u in range(w):
            carry = step_fn(lo + w * p + u, carry)
        return carry

    carry = lax.fori_loop(0, n // w, several, init)
    return lax.fori_loop(lo + (n // w) * w, hi, step_fn, carry)


def _att_fwd(qz, kv, ck):
    seq = qz.shape[0]
    heads = ck.shape[0]
    t = min(ATT_TILE, seq)
    scale = HEAD_DIM ** -0.5

    def body(q_ref, z_ref, k_ref, v_ref, ck_ref, o_ref, og_ref, lse_ref):
        i = pl.program_id(1)
        q = (q_ref[...] * scale).astype(BF16)

        def block(j, carry, masked):
            m, l, acc = carry
            rows = pl.ds(pl.multiple_of(j * t, t), t)
            s = _att_scores(q, k_ref[rows, :], ck_ref[0, j], i * t, j * t, masked)
            m_new = jnp.maximum(m, jnp.max(s, axis=1, keepdims=True))
            p = jnp.exp(s - m_new)
            alpha = jnp.exp(m - m_new)
            p_hi = p.astype(BF16)
            p_lo = (p - p_hi.astype(F32)).astype(BF16)
            vb = v_ref[rows, :]
            return (m_new, alpha * l + jnp.sum(p, axis=1, keepdims=True),
                    alpha * acc + (_dot(p_hi, vb) + _dot(p_lo, vb)))

        init = (jnp.full((t, 1), NEG_INF, F32), jnp.zeros((t, 1), F32), jnp.zeros((t, HEAD_DIM), F32))
        carry = _pairwise_loop(0, i, functools.partial(block, masked=False), init)
        m, l, acc = block(i, carry, True)
        o = acc / l
        o_ref[...] = o
        og_ref[...] = (o * _silu(z_ref[...])).astype(BF16)
        lse_ref[0] = m + jnp.log(l)

    qblk = pl.BlockSpec((t, HEAD_DIM), lambda h, i: (i, h))
    return _pcall(
        body, name="att_fwd", grid=(heads, seq // t),
        in_specs=[qblk, pl.BlockSpec((t, HEAD_DIM), lambda h, i: (i, heads + h)),
                  pl.BlockSpec((seq, HEAD_DIM), lambda h, i: (0, h)),
                  pl.BlockSpec((seq, HEAD_DIM), lambda h, i: (0, heads + h)),
                  pl.BlockSpec((1, seq // t, 1, t), lambda h, i: (h, 0, 0, 0))],
        out_specs=[qblk, qblk, pl.BlockSpec((1, t, 1), lambda h, i: (h, i, 0))],
        out_shape=[jax.ShapeDtypeStruct((seq, heads * HEAD_DIM), F32), jax.ShapeDtypeStruct((seq, heads * HEAD_DIM), BF16),
                   jax.ShapeDtypeStruct((heads, seq, 1), F32)],
        compiler_params=_params(("parallel", "parallel")),
    )(qz, qz, kv, kv, ck)


def _att_bwd(qz, kv, do, ox, lse, ck):
    seq = qz.shape[0]
    heads = ck.shape[0]
    t = min(ATT_TILE, seq)
    nq = seq // t
    scale = HEAD_DIM ** -0.5

    def body(q_ref, k_ref, v_ref, do_ref, ox_ref, lse_ref, ck_ref, dq_ref, dk_ref, dv_ref, dck_ref, dq_acc, delta):
        j = pl.program_id(1)

        @pl.when(j == 0)
        def _():
            for c0 in range(0, seq, t):
                rows = pl.ds(c0, t)
                delta[rows, :] = jnp.sum(do_ref[rows, :].astype(BF16).astype(F32) * ox_ref[rows, :], axis=1, keepdims=True)
                dq_acc[rows, :] = jnp.zeros((t, HEAD_DIM), F32)

        kb, vb = k_ref[...], v_ref[...]
        ckv = ck_ref[0, 0]

        def block(i, carry, masked):
            dk, dv, dck = carry
            rows = pl.ds(pl.multiple_of(i * t, t), t)
            qb = (q_ref[rows, :] * scale).astype(BF16)
            dob = do_ref[rows, :].astype(BF16)
            s = _att_scores(qb, kb, ckv, i * t, j * t, masked)
            p = jnp.exp(s - lse_ref[0, rows, :])
            ds = p * (_dot(dob, vb, _NT) - delta[rows, :])
            dq_acc[rows, :] += _dot(ds, kb)
            return dk + _dot(ds, qb, _TN), dv + _dot(p, dob, _TN), dck - _colsum(ds)

        init = (jnp.zeros((t, HEAD_DIM), F32), jnp.zeros((t, HEAD_DIM), F32), jnp.zeros((1, t), F32))
        carry = block(j, init, True)
        dk, dv, dck = _pairwise_loop(j + 1, nq, functools.partial(block, masked=False), carry)
        dk_ref[...] = dk.astype(dk_ref.dtype)
        dv_ref[...] = dv.astype(dv_ref.dtype)
        dck_ref[0, 0] = dck

        @pl.when(j == nq - 1)
        def _():
            for c0 in range(0, seq, t):
                rows = pl.ds(c0, t)
                dq_ref[rows, :] = (dq_acc[rows, :] * scale).astype(dq_ref.dtype)

    head = pl.BlockSpec((seq, HEAD_DIM), lambda h, j: (0, h))
    col = pl.BlockSpec((1, seq, 1), lambda h, j: (h, 0, 0))
    kblk = pl.BlockSpec((t, HEAD_DIM), lambda h, j: (j, h))
    row = pl.BlockSpec((1, 1, 1, t), lambda h, j: (h, j, 0, 0))
    return _pcall(
        body, name="att_bwd", grid=(heads, nq),
        in_specs=[head, kblk, pl.BlockSpec((t, HEAD_DIM), lambda h, j: (j, heads + h)), head, head, col, row],
        out_specs=[head, kblk, kblk, row],
        out_shape=[jax.ShapeDtypeStruct((seq, heads * HEAD_DIM), BF16)] * 3 + [jax.ShapeDtypeStruct((heads, nq, 1, t), F32)],
        scratch_shapes=[pltpu.VMEM((seq, HEAD_DIM), F32), pltpu.VMEM((seq, 1), F32)],
        compiler_params=_params(("parallel", "arbitrary")),
    )(qz, kv, kv, do, ox, lse, ck)


def _mesh_pos():
    return lax.axis_index("x"), lax.axis_index("y"), lax.axis_index("c")


def _other_chips(x, y):
    return [(1 - x, y), (x, 1 - y), (1 - x, 1 - y)]


def _all_gather_weights(big, small):
    nb, ns = len(big), len(small)
    n_remote = 3 * (nb + ns)

    def plan(ins, outs, sems):
        send_sems, recv_sems, fwd_send, fwd_recv = sems
        x, y, c = _mesh_pos()
        chips = _other_chips(x, y)
        slots = [2 * cx + cy for cx, cy in chips]

        def half(ref, hc):
            rh = ref.shape[-2] // 2
            return ref.at[pl.ds(hc * rh, rh), :]

        def remote(i, j, src_chip, from_in):
            if i < nb:
                src = half(ins[i], c) if from_in else half(outs[i].at[src_chip], c)
                dst = half(outs[i].at[src_chip], c)
            else:
                src = ins[i] if from_in else outs[i].at[src_chip]
                dst = outs[i].at[src_chip]
            k = 3 * i + j
            return pltpu.make_async_remote_copy(src_ref=src, dst_ref=dst, send_sem=send_sems.at[k],
                                                recv_sem=recv_sems.at[k], device_id=(*chips[j], c),
                                                device_id_type=MESH_ID)

        def forward(i, j, hc):
            part = half(outs[i].at[slots[j]], hc)
            k = 3 * i + j
            return pltpu.make_async_remote_copy(src_ref=part, dst_ref=part, send_sem=fwd_send.at[k],
                                                recv_sem=fwd_recv.at[k], device_id=(x, y, 1 - c),
                                                device_id_type=MESH_ID)

        return remote, forward, 2 * x + y, slots, c

    def start(ins, outs, sems):
        remote, _, me, _, _ = plan(ins, outs, sems)
        for i in range(nb + ns):
            for j in range(3):
                remote(i, j, me, True).start()

    def finish(ins, outs, sems):
        remote, forward, me, slots, c = plan(ins, outs, sems)
        for i in range(nb + ns):
            for j in range(3):
                remote(i, j, slots[j], False).wait_recv()
                if i < nb:
                    forward(i, j, c).start()
        for i in range(nb):
            for j in range(3):
                forward(i, j, 1 - c).wait_recv()
        for i in range(nb + ns):
            for j in range(3):
                remote(i, j, me, True).wait_send()
                if i < nb:
                    forward(i, j, c).wait_send()

    arrays = tuple(big) + tuple(small)
    return _Exchange(
        arrays=arrays,
        out_shapes=tuple(jax.ShapeDtypeStruct((N_CHIPS,) + a.shape, a.dtype) for a in arrays),
        scratch=(pltpu.SemaphoreType.DMA((n_remote,)), pltpu.SemaphoreType.DMA((n_remote,)),
                 pltpu.SemaphoreType.DMA((3 * max(nb, 1),)), pltpu.SemaphoreType.DMA((3 * max(nb, 1),))),
        start=start, finish=finish)


def _swap_halves(grads):
    n = len(grads)

    def copies(ins, outs, sems):
        x, y, c = _mesh_pos()
        cps = []
        for i in range(n):
            rh = ins[i].shape[1] // 2
            cps.append(pltpu.make_async_remote_copy(
                src_ref=ins[i].at[:, pl.ds((1 - c) * rh, rh), :], dst_ref=outs[i], send_sem=sems[0].at[i],
                recv_sem=sems[1].at[i], device_id=(x, y, 1 - c), device_id_type=MESH_ID))
        return cps

    def start(ins, outs, sems):
        for cp in copies(ins, outs, sems):
            cp.start()

    def finish(ins, outs, sems):
        for cp in copies(ins, outs, sems):
            cp.wait()

    return _Exchange(
        arrays=tuple(grads),
        out_shapes=tuple(jax.ShapeDtypeStruct((g.shape[0], g.shape[1] // 2, g.shape[2]), g.dtype) for g in grads),
        scratch=(pltpu.SemaphoreType.DMA((n,)), pltpu.SemaphoreType.DMA((n,))),
        start=start, finish=finish)


def _pair_sum_bf16(g, theirs, pos, name):
    n, rh, cdim = theirs.shape
    tr = min(ROW_TILE, rh)
    nb = rh // tr

    def body(pos_ref, g_ref, t_ref, o_ref):
        o_ref[...] = (g_ref[...] + t_ref[...]).astype(BF16)

    slot = lambda s, pos: (pos[0] + 1 + s) % n
    grid_spec = pltpu.PrefetchScalarGridSpec(
        num_scalar_prefetch=1, grid=(n - 1, nb),
        in_specs=[pl.BlockSpec((None, tr, cdim), lambda s, i, pos: (slot(s, pos), pos[1] * nb + i, 0)),
                  pl.BlockSpec((None, tr, cdim), lambda s, i, pos: (slot(s, pos), i, 0))],
        out_specs=pl.BlockSpec((None, tr, cdim), lambda s, i, pos: (slot(s, pos), i, 0)))
    return _pcall(body, name=name, grid_spec=grid_spec, out_shape=jax.ShapeDtypeStruct(theirs.shape, BF16),
                  compiler_params=_params(("parallel", "parallel")))(pos, g, theirs)


def _chip_sum(g, theirs, recv, pos, name):
    n, rh, cdim = theirs.shape
    tr = min(ROW_TILE, rh)
    nb = rh // tr

    def body(pos_ref, g_ref, t_ref, r0, r1, r2, o_ref):
        o_ref[...] = (((g_ref[...] + t_ref[...]) + r0[...]) + r1[...]) + r2[...]

    grid_spec = pltpu.PrefetchScalarGridSpec(
        num_scalar_prefetch=1, grid=(nb,),
        in_specs=[pl.BlockSpec((None, tr, cdim), lambda i, pos: (pos[0], pos[1] * nb + i, 0)),
                  pl.BlockSpec((None, tr, cdim), lambda i, pos: (pos[0], i, 0))]
        + [pl.BlockSpec((None, tr, cdim), functools.partial(lambda i, pos, j: (j, i, 0), j=j)) for j in range(3)],
        out_specs=pl.BlockSpec((tr, cdim), lambda i, pos: (pos[1] * nb + i, 0)))
    return _pcall(body, name=name, grid_spec=grid_spec, out_shape=jax.ShapeDtypeStruct((2 * rh, cdim), F32),
                  compiler_params=_params(("parallel",)))(pos, g, theirs, recv, recv, recv)


def _scatter_to_owner(parts):
    n = len(parts)

    def copies(ins, outs, sems):
        x, y, c = _mesh_pos()
        chips = _other_chips(x, y)
        cps = []
        for i in range(n):
            for j in range(3):
                k = 3 * i + j
                cps.append(pltpu.make_async_remote_copy(
                    src_ref=ins[i].at[2 * chips[j][0] + chips[j][1]], dst_ref=outs[i].at[j],
                    send_sem=sems[0].at[k], recv_sem=sems[1].at[k], device_id=(*chips[j], c),
                    device_id_type=MESH_ID))
        return cps

    def start(ins, outs, sems):
        for cp in copies(ins, outs, sems):
            cp.start()

    def finish(ins, outs, sems):
        for cp in copies(ins, outs, sems):
            cp.wait()

    return _Exchange(
        arrays=tuple(parts),
        out_shapes=tuple(jax.ShapeDtypeStruct((3,) + p.shape[1:], p.dtype) for p in parts),
        scratch=(pltpu.SemaphoreType.DMA((3 * n,)), pltpu.SemaphoreType.DMA((3 * n,))),
        start=start, finish=finish)


def _finish_reductions(v, shards):
    n = len(shards)
    n_rows = v.shape[0]
    rh = n_rows // 2
    assert rh % SUBLANES == 0

    def body(*refs):
        v_ref, o_ref, outs = refs[0], refs[1 + n], refs[2 + n:2 + 2 * n]
        part, recv, send_sems, recv_sems, join_send, join_recv = refs[2 + 2 * n:]
        x, y, c = _mesh_pos()
        sibling = (x, y, 1 - c)

        def join(i, hc):
            half = outs[i].shape[0] // 2
            rows = outs[i].at[pl.ds(hc * half, half), :]
            return pltpu.make_async_remote_copy(src_ref=rows, dst_ref=rows, send_sem=join_send.at[i],
                                                recv_sem=join_recv.at[i], device_id=sibling, device_id_type=MESH_ID)

        for i in range(n):
            join(i, c).start()

        mine = pl.ds(pl.multiple_of(c * rh, SUBLANES), rh)
        theirs = pl.ds(pl.multiple_of((1 - c) * rh, SUBLANES), rh)

        def exchange(s, src, dst, peer):
            cp = pltpu.make_async_remote_copy(src_ref=src, dst_ref=dst, send_sem=send_sems.at[s],
                                              recv_sem=recv_sems.at[s], device_id=peer, device_id_type=MESH_ID)
            cp.start()
            cp.wait()

        exchange(0, v_ref.at[theirs, :], recv.at[0], sibling)
        part[...] = v_ref[mine, :] + recv[0]
        for s, peer in ((1, (1 - x, y, c)), (2, (x, 1 - y, c))):
            exchange(s, part, recv.at[s], peer)
            part[...] = part[...] + recv[s]
        o_ref[mine, :] = part[...]
        exchange(3, part, o_ref.at[mine, :], sibling)

        for i in range(n):
            join(i, c).wait_send()
            join(i, 1 - c).wait_recv()

    vm = pl.BlockSpec(memory_space=pltpu.VMEM)
    res = _pcall(
        body, name="finish_reductions", in_specs=[vm] + [ANY] * n, out_specs=[vm] + [ANY] * n,
        out_shape=[jax.ShapeDtypeStruct(v.shape, v.dtype)] + [jax.ShapeDtypeStruct(s.shape, s.dtype) for s in shards],
        input_output_aliases={1 + i: 1 + i for i in range(n)},
        scratch_shapes=[pltpu.VMEM((rh, LANES), v.dtype), pltpu.VMEM((3, rh, LANES), v.dtype),
                        pltpu.SemaphoreType.DMA((4,)), pltpu.SemaphoreType.DMA((4,)),
                        pltpu.SemaphoreType.DMA((n,)), pltpu.SemaphoreType.DMA((n,))],
        compiler_params=pltpu.CompilerParams(vmem_limit_bytes=VMEM_LIMIT, has_side_effects=True),
    )(v, *shards)
    return res[0], res[1:]


def _adamw_math(w, g, m, v):
    m = ADAM_B1 * m + (1.0 - ADAM_B1) * g
    v = ADAM_B2 * v + (1.0 - ADAM_B2) * (g * g)
    m_hat = m / (1.0 - ADAM_B1 ** ADAM_STEP)
    v_hat = v / (1.0 - ADAM_B2 ** ADAM_STEP)
    delta = -ADAM_LR * (m_hat / (jnp.sqrt(v_hat) + ADAM_EPS) + ADAM_WD * w)
    return delta, m, v


def _adamw(w, g, m, v, name):
    wd = w.shape[1]
    return _rows(_adamw_math, [_full(w), _full(g), _full(m), _full(v)], [(wd, F32)] * 3, name=name)


def _rows_of(a):
    return -(-a.size // (LANES * SUBLANES)) * SUBLANES


def _pack(arrs, fill=0.0):
    parts = []
    for a in arrs:
        flat = a.reshape(-1)
        flat = jnp.pad(flat, (0, _rows_of(a) * LANES - a.size), constant_values=fill)
        parts.append(flat.reshape(-1, LANES))
    used = sum(p.shape[0] for p in parts)
    rows = -(-used // ROW_TILE) * ROW_TILE
    parts.append(jnp.full((rows - used, LANES), fill, F32))
    return jnp.concatenate(parts, axis=0)


def _unpack(buf, like):
    out, off = [], 0
    for a in like:
        out.append(buf[off:off + _rows_of(a)].reshape(-1)[:a.size].reshape(a.shape))
        off += _rows_of(a)
    return out


def kernel(x, norm_pre, norm_post, s5_w_in, s5_a_re, s5_a_im, s5_log_dt, s5_b_re, s5_b_im, s5_c_re, s5_c_im, s5_d, s5_w_glu, s5_b_glu, s5_w_out, kv_norm, kv_w, kv_b_f, fox_w_in, fox_w_out, loss_target, m_norm_pre, m_norm_post, m_s5_w_in, m_s5_a_re, m_s5_a_im, m_s5_log_dt, m_s5_b_re, m_s5_b_im, m_s5_c_re, m_s5_c_im, m_s5_d, m_s5_w_glu, m_s5_b_glu, m_s5_w_out, m_kv_norm, m_kv_w, m_kv_b_f, m_fox_w_in, m_fox_w_out, v_norm_pre, v_norm_post, v_s5_w_in, v_s5_a_re, v_s5_a_im, v_s5_log_dt, v_s5_b_re, v_s5_b_im, v_s5_c_re, v_s5_c_im, v_s5_d, v_s5_w_glu, v_s5_b_glu, v_s5_w_out, v_kv_norm, v_kv_w, v_kv_b_f, v_fox_w_in, v_fox_w_out):
    seq, dm = x.shape[1], x.shape[2]
    width = dm
    heads = dm // HEAD_DIM
    fw = heads * HEAD_DIM
    groups = width // S5_GROUP
    chip = 2 * lax.axis_index("x") + lax.axis_index("y")

    big_shards = [s5_w_in[0], s5_w_glu[0], s5_w_out[0], kv_w, fox_w_in[0], fox_w_out[0]]
    own_shards = [w.astype(BF16) for w in big_shards] + [s5_d, s5_b_glu]
    fill_own = lambda gs, owns: [lax.dynamic_update_slice(g, own[None], (chip, 0, 0)) for g, own in zip(gs, owns)]
    c_idx = lax.axis_index("c")
    pos = jnp.stack([chip, c_idx]).astype(jnp.int32)
    h0 = x[0]
    target = loss_target[0]
    g_pre0, g_pre1 = norm_pre[0:1], norm_pre[1:2]
    g_post0, g_post1 = norm_post[0:1], norm_post[1:2]
    g_kv = kv_norm.reshape(1, dm)
    first_owns = [own_shards[0], s5_d, s5_b_glu]
    xn1, *first_gathered = _rows(lambda h, g: (h * _rstd(h) * g,), [_full(h0)], [(dm, BF16)], consts=[g_pre0], name="norm_pre0",
                                 carry=_all_gather_weights(first_owns[:1], first_owns[1:]))
    g_win, g_d, g_bglu = fill_own(first_gathered, first_owns)
    gather_rest = _all_gather_weights(own_shards[1:4], [])
    gather_fox_in = _all_gather_weights(own_shards[4:5], [])
    gather_fox_out = _all_gather_weights(own_shards[5:6], [])
    cols = lambda g: jnp.moveaxis(g, 0, 1).reshape(g.shape[1], -1)
    rows = lambda g: g.reshape(-1, g.shape[2])
    w_in = g_win
    d_skip, b_glu = cols(g_d), cols(g_bglu)
    b_f = jnp.pad(kv_b_f, (0, LANES - heads)).reshape(1, LANES)

    a_re, a_im, log_dt = s5_a_re[0], s5_a_im[0], s5_log_dt[0]
    disc, disc_vjp = jax.vjp(_ssm_discretize, a_re, a_im, log_dt)
    gpb = SSM_CH // S5_GROUP
    n_cb = groups // gpb
    par = jnp.stack([p.reshape(n_cb, gpb * S5_STATE) for p in disc], axis=1)
    b_t = lambda b: jnp.swapaxes(b, 1, 2)
    b_blk = jnp.stack([_block_diag(b_t(s5_b_re[0])), _block_diag(b_t(s5_b_im[0]))], axis=1)
    ct_blk = jnp.stack([_block_diag(s5_c_re[0]), _block_diag(s5_c_im[0])], axis=1)
    c_blk = jnp.swapaxes(ct_blk, 2, 3)

    xn1 = _to_slab(xn1)
    uz = _mm(xn1, w_in, name="s5_in")
    ys, y1b, *rest = _ssm_fwd(uz, b_blk, c_blk, par, d_skip, carry=gather_rest)
    g_wglu, g_wout, g_kvw = fill_own(rest, own_shards[1:4])
    w_glu, w_out = rows(g_wglu), rows(g_wout)
    kvw_full = cols(g_kvw)
    w_kv = kvw_full[:, :2 * fw]
    w_f = jnp.pad(kvw_full[:, 2 * fw:], ((0, 0), (0, LANES - heads)))

    def gate_fn(a, y, z, b):
        return a, _gelu(y) * _sigmoid(a + b) * _silu(z)

    glu_a, y3b = _mm(y1b, w_glu, name="s5_glu", post=(gate_fn, [ys, (uz, width), b_glu], (F32, BF16)))
    o1 = _from_slab(_mm(y3b, w_out, name="s5_out"))

    def mid_fn(h, o, gp, gk, gq):
        h1 = h + o * _rstd(o) * gp
        r = _rstd(h1)
        return h1, h1 * r * gk, h1 * r * gq

    h1, xk, xn2 = _rows(mid_fn, [_full(h0), _full(o1)], [(dm, F32), (dm, BF16), (dm, BF16)],
                        consts=[g_post0, g_kv, g_pre1], name="mid_norms")

    kv, g_fwin = _mm(xk, w_kv, out_dtype=BF16, name="kv_proj", carry=gather_fox_in)
    fw_in = fill_own([g_fwin], own_shards[4:5])[0]
    fl = _mm(xk, w_f, name="f_proj")
    qz, g_fwout = _mm(xn2, fw_in, name="fox_in", carry=gather_fox_out)
    fw_out = rows(fill_own([g_fwout], own_shards[5:6])[0])
    cum = _cum_fwd(fl, b_f)
    t_att = min(ATT_TILE, seq)
    cum_t = cum[:, :heads].T
    ck = cum_t.reshape(heads, seq // t_att, 1, t_att)
    o, o2b, lse = _att_fwd(qz, kv, ck)
    o3 = _mm(o2b, fw_out, name="fox_out")

    def loss_fn(h, o, t, g):
        r = _rstd(o)
        err = h + o * r * g - t
        dh = err * (1.0 / dm)
        do, dg = _rms_bwd(o, g, dh)
        part = 0.5 * jnp.sum(jnp.mean(err * err, axis=-1, keepdims=True), axis=0, keepdims=True)
        return dh, do, jnp.broadcast_to(part, (1, LANES)), _colsum(dg)

    dh2, do3, loss_part, dg_post1 = _rows(loss_fn, [_full(h1), _full(o3), _full(target)], [(dm, F32), (dm, BF16)],
                                          consts=[g_post1], accs=[(1, LANES), (1, dm)], name="loss_head")
    loss = lax.psum(loss_part[0, 0], MESH_AXES)

    def fox_gate_bwd(d, a, z):
        return d * _silu(z), d * a * _dsilu(z)

    do, dz2 = _mm(do3, fw_out, tb=True, name="fox_out_dx", post=(fox_gate_bwd, [o, (qz, fw)], (F32, BF16)))
    dw_fout = _mm(o2b, do3, ta=True, name="fox_out_dw")
    dq, dk, dv, dck = _att_bwd(qz, kv, do, o, lse, ck)
    dcum = jnp.pad(dck.reshape(heads, seq).T, ((0, 0), (0, LANES - heads)))
    dfl, db_f = _cum_bwd(dcum, fl, b_f)
    dqz = (dq, dz2)
    dkv = (dk, dv)
    dxn2 = _mm(dqz, fw_in, tb=True, name="fox_in_dx")
    dw_fin = _mm(xn2, dqz, ta=True, out_split=N_CHIPS, name="fox_in_dw")
    dxk_f = _mm(dfl, w_f, tb=True, name="f_proj_dx")
    dxk = _mm(dkv, w_kv, tb=True, add=dxk_f, name="kv_proj_dx")
    dw_kv = _mm(xk, dkv, ta=True, name="kv_proj_dw")
    dw_f = _mm(xk, dfl, ta=True, name="f_proj_dw")

    def mid_bwd(d2, h, dq_, dk_, o, gq, gk, gp):
        dxa, dga = _rms_bwd(h, gq, dq_)
        dxb, dgb = _rms_bwd(h, gk, dk_)
        dh = d2 + dxa + dxb
        do_, dgp = _rms_bwd(o, gp, dh)
        return dh, do_, _colsum(dga), _colsum(dgb), _colsum(dgp)

    to_cols = lambda g: jnp.moveaxis(g.reshape(g.shape[0], N_CHIPS, -1), 1, 0)
    to_rows = lambda g: g.reshape(N_CHIPS, -1, g.shape[1])
    kv_cols = kv_w.shape[1]
    kv_col_block = lambda b: (dw_kv[:, b * kv_cols:(b + 1) * kv_cols] if (b + 1) * kv_cols <= 2 * fw else
                              jnp.concatenate([dw_kv[:, b * kv_cols:], dw_f[:, :heads]], axis=1))
    early_grads = [jnp.stack([kv_col_block(b) for b in range(N_CHIPS)]), dw_fin, to_rows(dw_fout)]
    dh1, do1, dg_pre1, dg_kv, dg_post0, *early_theirs = _rows(
        mid_bwd, [_full(dh2), _full(h1), _full(dxn2), _full(dxk), _full(o1)], [(dm, F32), (dm, BF16)],
        consts=[g_pre1, g_kv, g_post0], accs=[(1, dm)] * 3, name="mid_norms_bwd", carry=_swap_halves(early_grads))
    early_sums = [_pair_sum_bf16(g, t, pos, f"grad_pair_sum_{3 + i}") for i, (g, t) in enumerate(zip(early_grads, early_theirs))]

    do1 = _to_slab(do1)
    dy3 = _mm(do1, w_out, tb=True, name="s5_out_dx")
    dw_out = _mm(y3b, do1, ta=True, name="s5_out_dw")

    def gate_bwd(d3, y, a, z, b):
        y1 = _gelu(y)
        gate = _sigmoid(a + b)
        dy2 = d3 * _silu(z)
        da = dy2 * y1 * gate * (1.0 - gate)
        return dy2 * gate, da, d3 * (y1 * gate) * _dsilu(z), _colsum(da)

    dy1_direct, da, dz, db_glu = _rows(gate_bwd, [_full(dy3), _full(ys), _full(glu_a), (uz, width, 1)],
                                       [(width, F32), (width, BF16), (width, BF16)], consts=[b_glu],
                                       accs=[(1, width)], name="s5_gate_bwd")
    dys = _mm(da, w_glu, tb=True, add=dy1_direct, post=(lambda d, y: d * _dgelu(y), [ys], None), name="s5_glu_dx")
    dw_glu = _mm(y1b, da, ta=True, name="s5_glu_dw")
    mid_grads = [to_rows(dw_glu), to_rows(dw_out)]
    du, dbt_blk, dct_blk, dpar, dd, *carried = _ssm_bwd(
        uz, dys, b_blk, ct_blk, par, d_skip, carry=_together(_scatter_to_owner(early_sums), _swap_halves(mid_grads)))
    early_recv, mid_theirs = carried[:3], carried[3:]
    mid_sums = [_pair_sum_bf16(g, t, pos, f"grad_pair_sum_{1 + i}") for i, (g, t) in enumerate(zip(mid_grads, mid_theirs))]
    duz = (du, dz)
    dw_in, *mid_recv = _mm(xn1, duz, ta=True, out_split=N_CHIPS, name="s5_in_dw", carry=_scatter_to_owner(mid_sums))
    late_grads = [dw_in]
    dxn1, *late_theirs = _mm(duz, w_in, tb=True, name="s5_in_dx", carry=_swap_halves(late_grads))
    dxn1 = _from_slab(dxn1)
    late_sums = [_pair_sum_bf16(dw_in, late_theirs[0], pos, "grad_pair_sum_0")]

    def first_bwd(d1, h, dxn, g):
        dx, dg = _rms_bwd(h, g, dxn)
        return d1 + dx, _colsum(dg)

    grad_x, dg_pre0, *late_recv = _rows(first_bwd, [_full(dh1), _full(h0), _full(dxn1)], [(dm, F32)], consts=[g_pre0],
                                        accs=[(1, dm)], name="norm_pre0_bwd", carry=_scatter_to_owner(late_sums))

    dpar_g = [dpar[:, i, :].reshape(groups, S5_STATE) for i in range(4)]
    da_re, da_im, dlog_dt = disc_vjp(tuple(dpar_g))
    db_re = jnp.swapaxes(_block_diag_extract(dbt_blk[:, 0]), 1, 2)
    db_im = jnp.swapaxes(_block_diag_extract(dbt_blk[:, 1]), 1, 2)
    dc_re = _block_diag_extract(dct_blk[:, 0])
    dc_im = _block_diag_extract(dct_blk[:, 1])

    big_grads = late_grads + mid_grads + early_grads
    theirs = list(late_theirs) + list(mid_theirs) + list(early_theirs)
    received = list(late_recv) + list(mid_recv) + list(early_recv)
    halves = [_chip_sum(g, t, r, pos, f"grad_chip_sum_{i}") for i, (g, t, r) in enumerate(zip(big_grads, theirs, received))]

    small_local = [jnp.concatenate([dg_pre0, dg_pre1]), jnp.concatenate([dg_post0, dg_post1]),
                   da_re[None], da_im[None], dlog_dt[None], db_re[None], db_im[None], dc_re[None], dc_im[None],
                   dd, db_glu, dg_kv.reshape(dm), db_f[0, :heads]]
    small_buf, (g_win_s, g_wglu_s, g_wout_s, g_kvw_s, g_fwin_s, g_fwout_s) = _finish_reductions(_pack(small_local), halves)
    (g_norm_pre, g_norm_post, g_a_re, g_a_im, g_log_dt, g_b_re, g_b_im, g_c_re, g_c_im, g_d_full, g_bglu_full,
     g_kv_norm, g_b_f) = _unpack(small_buf, small_local)
    shard = width // N_CHIPS
    g_d_own = lax.dynamic_slice(g_d_full, (0, chip * shard), (1, shard))
    g_bglu_own = lax.dynamic_slice(g_bglu_full, (0, chip * shard), (1, shard))

    big_w = big_shards
    big_g = [g_win_s, g_wglu_s, g_wout_s, g_kvw_s, g_fwin_s, g_fwout_s]
    big_m = [m_s5_w_in[0], m_s5_w_glu[0], m_s5_w_out[0], m_kv_w, m_fox_w_in[0], m_fox_w_out[0]]
    big_v = [v_s5_w_in[0], v_s5_w_glu[0], v_s5_w_out[0], v_kv_w, v_fox_w_in[0], v_fox_w_out[0]]
    big_upd = [_adamw(w, g, m, v, f"adamw_{i}") for i, (w, g, m, v) in enumerate(zip(big_w, big_g, big_m, big_v))]

    small_names = ["norm_pre", "norm_post", "s5_a_re", "s5_a_im", "s5_log_dt", "s5_b_re", "s5_b_im", "s5_c_re", "s5_c_im",
                   "s5_d", "s5_b_glu", "kv_norm", "kv_b_f"]
    small_w = [norm_pre, norm_post, s5_a_re, s5_a_im, s5_log_dt, s5_b_re, s5_b_im, s5_c_re, s5_c_im, s5_d, s5_b_glu, kv_norm, kv_b_f]
    small_m = [m_norm_pre, m_norm_post, m_s5_a_re, m_s5_a_im, m_s5_log_dt, m_s5_b_re, m_s5_b_im, m_s5_c_re, m_s5_c_im, m_s5_d, m_s5_b_glu, m_kv_norm, m_kv_b_f]
    small_v = [v_norm_pre, v_norm_post, v_s5_a_re, v_s5_a_im, v_s5_log_dt, v_s5_b_re, v_s5_b_im, v_s5_c_re, v_s5_c_im, v_s5_d, v_s5_b_glu, v_kv_norm, v_kv_b_f]
    small_g = [g_norm_pre, g_norm_post, g_a_re, g_a_im, g_log_dt, g_b_re, g_b_im, g_c_re, g_c_im, g_d_own, g_bglu_own, g_kv_norm, g_b_f]
    small_g = [g.reshape(w.shape) for g, w in zip(small_g, small_w)]
    sd, sm, sv = _adamw(_pack(small_w), _pack(small_g), _pack(small_m), _pack(small_v, fill=1.0), "adamw_small")
    small_delta, small_newm, small_newv = _unpack(sd, small_w), _unpack(sm, small_w), _unpack(sv, small_w)

    order = ["norm_pre", "norm_post", "s5_w_in", "s5_a_re", "s5_a_im", "s5_log_dt", "s5_b_re", "s5_b_im", "s5_c_re", "s5_c_im",
             "s5_d", "s5_w_glu", "s5_b_glu", "s5_w_out", "kv_norm", "kv_w", "kv_b_f", "fox_w_in", "fox_w_out"]
    big_names = ["s5_w_in", "s5_w_glu", "s5_w_out", "kv_w", "fox_w_in", "fox_w_out"]
    big_like = [s5_w_in, s5_w_glu, s5_w_out, kv_w, fox_w_in, fox_w_out]
    grads, deltas, new_m, new_v = {}, {}, {}, {}
    for i, n in enumerate(big_names):
        shp = big_like[i].shape
        grads[n] = big_g[i].reshape(shp)
        deltas[n], new_m[n], new_v[n] = (a.reshape(shp) for a in big_upd[i])
    for i, n in enumerate(small_names):
        grads[n], deltas[n], new_m[n], new_v[n] = small_g[i], small_delta[i], small_newm[i], small_newv[i]

    return (loss, grad_x[None], *[grads[n] for n in order], *[deltas[n] for n in order],
            *[new_m[n] for n in order], *[new_v[n] for n in order])
```

```python
import functools
import math
from typing import Callable, NamedTuple

import jax
import jax.numpy as jnp
from jax import lax
from jax.experimental import pallas as pl
from jax.experimental.pallas import tpu as pltpu

F32 = jnp.float32
BF16 = jnp.bfloat16

S5_GROUP = 16
S5_STATE = 64
HEAD_DIM = 128
RMS_EPS = 1e-6
NEG_INF = -1e30
ADAM_LR = 0.001
ADAM_B1 = 0.9
ADAM_B2 = 0.999
ADAM_EPS = 1e-08
ADAM_WD = 0.01
ADAM_STEP = 10

LANES = 128
SUBLANES = 8
VMEM_LIMIT = 56 * 1024 * 1024
N_CHIPS = 4
MESH_AXES = ("x", "y", "c")
MESH_ID = pl.DeviceIdType.MESH

SSM_CH = 128
ROW_TILE = 256
SCAN_ROWS = 512
SCAN_UNROLL = 4
ATT_TILE = 512
ATT_UNROLL = 2
CUM_TILE = 512


def _pcall(body, **kw):
    return pl.pallas_call(body, **kw)


def _params(sem=None):
    if sem is None:
        return pltpu.CompilerParams(vmem_limit_bytes=VMEM_LIMIT)
    return pltpu.CompilerParams(vmem_limit_bytes=VMEM_LIMIT, dimension_semantics=sem)


def _sigmoid(x):
    return 1.0 / (1.0 + jnp.exp(-x))


def _silu(z):
    return z * _sigmoid(z)


def _dsilu(z):
    s = _sigmoid(z)
    return s * (1.0 + z * (1.0 - s))


_GELU_C = math.sqrt(2.0 / math.pi)


def _gelu(x):
    return 0.5 * x * (1.0 + jnp.tanh(_GELU_C * (x + 0.044715 * x * x * x)))


def _dgelu(x):
    t = jnp.tanh(_GELU_C * (x + 0.044715 * x * x * x))
    return 0.5 * (1.0 + t) + 0.5 * x * (1.0 - t * t) * _GELU_C * (1.0 + 3.0 * 0.044715 * x * x)


def _rstd(x):
    return lax.rsqrt(jnp.mean(x * x, axis=-1, keepdims=True) + RMS_EPS)


def _rms_bwd(x, g, dy):
    r = _rstd(x)
    dyg = dy * g
    dx = r * dyg - x * (r * r * r) * jnp.mean(dyg * x, axis=-1, keepdims=True)
    return dx, dy * (x * r)


def _colsum(v):
    return jnp.sum(v, axis=0, keepdims=True)


ANY = pl.BlockSpec(memory_space=pl.ANY)


class _Exchange(NamedTuple):
    arrays: tuple
    out_shapes: tuple
    scratch: tuple
    start: Callable
    finish: Callable


def _together(*exs):
    def parts(seq, field):
        out, off = [], 0
        for e in exs:
            n = len(getattr(e, field))
            out.append(seq[off:off + n])
            off += n
        return out

    def start(ins, outs, sems):
        for e, i, o, s in zip(exs, parts(ins, "arrays"), parts(outs, "out_shapes"), parts(sems, "scratch")):
            e.start(i, o, s)

    def finish(ins, outs, sems):
        for e, i, o, s in zip(exs, parts(ins, "arrays"), parts(outs, "out_shapes"), parts(sems, "scratch")):
            e.finish(i, o, s)

    return _Exchange(arrays=sum((tuple(e.arrays) for e in exs), ()), out_shapes=sum((tuple(e.out_shapes) for e in exs), ()),
                     scratch=sum((tuple(e.scratch) for e in exs), ()), start=start, finish=finish)


def _carry(ex, refs, n_fixed_in, n_fixed_out):
    if ex is None:
        return refs, lambda cond: None, lambda cond: None
    n_in, n_out, n_sem = len(ex.arrays), len(ex.out_shapes), len(ex.scratch)
    fixed_in = refs[:n_fixed_in]
    ex_in = refs[n_fixed_in:n_fixed_in + n_in]
    rest = refs[n_fixed_in + n_in:]
    fixed_out = rest[:n_fixed_out]
    ex_out = rest[n_fixed_out:n_fixed_out + n_out]
    scratch = rest[n_fixed_out + n_out:]
    sems = scratch[len(scratch) - n_sem:]

    def start_when(cond):
        pl.when(cond)(lambda: ex.start(ex_in, ex_out, sems))

    def finish_when(cond):
        pl.when(cond)(lambda: ex.finish(ex_in, ex_out, sems))

    return tuple(fixed_in) + tuple(fixed_out) + tuple(scratch[:len(scratch) - n_sem]), start_when, finish_when


def _carry_specs(ex):
    if ex is None:
        return (), [], [], [], []
    return ex.arrays, [ANY] * len(ex.arrays), [ANY] * len(ex.out_shapes), list(ex.out_shapes), list(ex.scratch)


def _mm(a, b, *, ta=False, tb=False, out_dtype=F32, add=None, post=None, out_split=1, tm=1024, tn=1024, tk=2048, name,
        carry=None):
    def describe(op):
        if isinstance(op, (tuple, list)):
            assert all(p.ndim == 2 and p.shape == op[0].shape for p in op)
            return list(op), op[0].shape[0], op[0].shape[1], False
        if op.ndim == 3:
            return [op], op.shape[1], op.shape[2], True
        return [op], op.shape[0], op.shape[1], False

    a_parts, a_rows, a_pc, a_stack = describe(a)
    b_parts, b_rows, b_pc, b_stack = describe(b)
    a_cols = a_pc * (a.shape[0] if a_stack else len(a_parts))
    b_cols = b_pc * (b.shape[0] if b_stack else len(b_parts))
    k_dim, m_dim = (a_rows, a_cols) if ta else (a_cols, a_rows)
    n_dim, kb = (b_rows, b_cols) if tb else (b_cols, b_rows)
    assert kb == k_dim, (k_dim, kb)
    tm = min(tm, a_pc) if ta else min(tm, m_dim)
    tk = min(tk, k_dim, k_dim if ta else a_pc, b_pc if tb else k_dim)
    tn = min(tn, n_dim // out_split, n_dim if tb else b_pc)
    a_ct, b_ct = (tm if ta else tk), (tk if tb else tn)
    assert m_dim % tm == 0 and n_dim % tn == 0 and k_dim % tk == 0 and a_pc % a_ct == 0 and b_pc % b_ct == 0
    assert (n_dim // out_split) % tn == 0
    nk = k_dim // tk
    dims = (((0 if ta else 1,), (1 if tb else 0,)), ((), ()))
    n_a, n_b = len(a_parts), len(b_parts)
    assert n_a == 1 or n_b == 1

    def operand_specs(parts, stack, rows_t, cols_t, per, row_of, col_of):
        specs = []
        for p in range(len(parts)):
            def col(i, j, k, p=p):
                return jnp.clip(col_of(i, j, k) - p * per, 0, per - 1) if len(parts) > 1 else col_of(i, j, k)
            if stack:
                specs.append(pl.BlockSpec((None, rows_t, cols_t),
                                          lambda i, j, k, col=col: (col(i, j, k) // per, row_of(i, j, k), col(i, j, k) % per)))
            else:
                specs.append(pl.BlockSpec((rows_t, cols_t), lambda i, j, k, col=col: (row_of(i, j, k), col(i, j, k))))
        return specs

    if ta:
        a_specs = operand_specs(a_parts, a_stack, tk, tm, a_pc // tm, lambda i, j, k: k, lambda i, j, k: i)
    else:
        a_specs = operand_specs(a_parts, a_stack, tm, tk, a_pc // tk, lambda i, j, k: i, lambda i, j, k: k)
    if tb:
        b_specs = operand_specs(b_parts, b_stack, tn, tk, b_pc // tk, lambda i, j, k: j, lambda i, j, k: k)
    else:
        b_specs = operand_specs(b_parts, b_stack, tk, tn, b_pc // tn, lambda i, j, k: k, lambda i, j, k: j)

    post_fn, post_arrays, post_dtypes = post if post is not None else (None, (), None)
    out_dtypes = tuple(post_dtypes) if post_dtypes else (out_dtype,)
    extras = ((add, 0),) * (add is not None) + tuple(e if isinstance(e, tuple) else (e, 0) for e in post_arrays)
    n_fixed_in = n_a + n_b + len(extras)
    n_out = len(out_dtypes)
    grid = (m_dim // tm, n_dim // tn, nk)
    ex_args, ex_in_specs, ex_out_specs, ex_out_shapes, ex_scratch = _carry_specs(carry)

    def body(*refs):
        refs, start_when, finish_when = _carry(carry, refs, n_fixed_in, n_out)
        i, j, k = pl.program_id(0), pl.program_id(1), pl.program_id(2)
        step = (i * grid[1] + j) * grid[2] + k
        start_when(step == 0)
        compute(*refs)
        finish_when(step == grid[0] * grid[1] * grid[2] - 1)

    def compute(*refs):
        a_refs, b_refs = refs[:n_a], refs[n_a:n_a + n_b]
        rest = refs[n_a + n_b:]
        extra_refs, o_refs = rest[:len(extras)], rest[len(extras):len(extras) + n_out]
        acc = None if nk == 1 else rest[-1]
        i, j, k = pl.program_id(0), pl.program_id(1), pl.program_id(2)

        def finish(res):
            tiles = [r[...] for r in extra_refs]
            if add is not None:
                res = res + tiles.pop(0)
            if post_fn is not None:
                res = post_fn(res, *tiles)
            res = res if isinstance(res, (tuple, list)) else (res,)
            for o_ref, r in zip(o_refs, res):
                o_ref[...] = r.astype(o_ref.dtype)

        def accumulate(a_ref, b_ref):
            prod = lax.dot_general(a_ref[...].astype(BF16), b_ref[...].astype(BF16), dims,
                                   preferred_element_type=F32)
            if nk == 1:
                finish(prod)
                return

            @pl.when(k == 0)
            def _():
                acc[...] = prod

            @pl.when(jnp.logical_and(k > 0, k < nk - 1))
            def _():
                acc[...] += prod

            @pl.when(k == nk - 1)
            def _():
                finish(acc[...] + prod)

        if n_a == 1 and n_b == 1:
            accumulate(a_refs[0], b_refs[0])
        else:
            many, block, per = (a_refs, (i if ta else k), a_pc // a_ct) if n_a > 1 else (b_refs, (k if tb else j), b_pc // b_ct)
            for p, ref in enumerate(many):
                @pl.when(block // per == p)
                def _(ref=ref):
                    accumulate(ref, b_refs[0]) if n_a > 1 else accumulate(a_refs[0], ref)

    per_out = n_dim // out_split // tn
    if out_split > 1:
        o_spec = pl.BlockSpec((None, tm, tn), lambda i, j, k: (j // per_out, i, j % per_out))
        out_shapes = [jax.ShapeDtypeStruct((out_split, m_dim, n_dim // out_split), dt) for dt in out_dtypes]
    else:
        o_spec = pl.BlockSpec((tm, tn), lambda i, j, k: (i, j))
        out_shapes = [jax.ShapeDtypeStruct((m_dim, n_dim), dt) for dt in out_dtypes]

    def extra_spec(arr, first_col):
        assert first_col % tn == 0
        off = first_col // tn
        if arr.shape[0] == 1:
            return pl.BlockSpec((1, tn), lambda i, j, k: (0, j + off))
        return pl.BlockSpec((tm, tn), lambda i, j, k: (i, j + off))

    in_specs = a_specs + b_specs + [extra_spec(arr, off) for arr, off in extras]
    args = tuple(a_parts) + tuple(b_parts) + tuple(arr for arr, _ in extras)
    acc_scratch = [pltpu.VMEM((tm, tn), F32)] if nk > 1 else []
    if carry is None:
        res = _pcall(
            body, name=name, grid=grid, in_specs=in_specs, out_specs=[o_spec] * n_out, out_shape=out_shapes,
            scratch_shapes=acc_scratch, compiler_params=_params(("parallel", "parallel", "arbitrary")),
        )(*args)
        return res[0] if n_out == 1 else res
    return _pcall(
        body, name=name, grid=grid, in_specs=in_specs + ex_in_specs, out_specs=[o_spec] * n_out + ex_out_specs,
        out_shape=out_shapes + ex_out_shapes, scratch_shapes=acc_scratch + ex_scratch,
        compiler_params=pltpu.CompilerParams(vmem_limit_bytes=VMEM_LIMIT, has_side_effects=True,
                                             dimension_semantics=("arbitrary", "arbitrary", "arbitrary")),
    )(*args, *ex_args)


def _rows(fn, ins, outs, *, name, consts=(), accs=(), carry=None):
    n_rows = ins[0][0].shape[0]
    tr = min(ROW_TILE, n_rows)
    assert n_rows % tr == 0
    n_in, n_c, n_out = len(ins), len(consts), len(outs)
    n_steps = n_rows // tr
    ex_args, ex_in_specs, ex_out_specs, ex_out_shapes, ex_scratch = _carry_specs(carry)

    def body(*refs):
        refs, start_when, finish_when = _carry(carry, refs, n_in + n_c, n_out + len(accs))
        start_when(pl.program_id(0) == 0)
        _compute(*refs)
        finish_when(pl.program_id(0) == n_steps - 1)

    def _compute(*refs):
        vals = [r[...] for r in refs[:n_in + n_c]]
        res = fn(*vals)
        res = res if isinstance(res, (tuple, list)) else (res,)
        o_refs = refs[n_in + n_c:]
        for r, v in zip(o_refs[:n_out], res[:n_out]):
            r[...] = v.astype(r.dtype)
        if accs:
            first = pl.program_id(0) == 0
            for r, v in zip(o_refs[n_out:], res[n_out:]):
                @pl.when(first)
                def _(r=r, v=v):
                    r[...] = v

                @pl.when(jnp.logical_not(first))
                def _(r=r, v=v):
                    r[...] += v

    in_specs = [pl.BlockSpec((tr, w), functools.partial(lambda i, cb: (i, cb), cb=cb)) for _, w, cb in ins]
    in_specs += [pl.BlockSpec(c.shape, functools.partial(lambda i, nd: (0,) * nd, nd=c.ndim)) for c in consts]
    out_specs = [pl.BlockSpec((tr, w), lambda i: (i, 0)) for w, _ in outs]
    out_specs += [pl.BlockSpec(s, lambda i: (0, 0)) for s in accs]
    out_shape = [jax.ShapeDtypeStruct((n_rows, w), dt) for w, dt in outs]
    out_shape += [jax.ShapeDtypeStruct(s, F32) for s in accs]
    sequential = bool(accs) or carry is not None
    params = _params(("arbitrary",) if sequential else ("parallel",))
    if carry is not None:
        params = pltpu.CompilerParams(vmem_limit_bytes=VMEM_LIMIT, dimension_semantics=("arbitrary",), has_side_effects=True)
    return _pcall(
        body, name=name, grid=(n_steps,), in_specs=in_specs + ex_in_specs, out_specs=out_specs + ex_out_specs,
        out_shape=out_shape + ex_out_shapes, scratch_shapes=ex_scratch, compiler_params=params,
    )(*[a for a, _, _ in ins], *consts, *ex_args)


def _full(a):
    return (a, a.shape[1], 0)


def _cmul(ar, ai, br, bi):
    return ar * br - ai * bi, ar * bi + ai * br


def _seg_scans(scans, seg):
    assert seg & (seg - 1) == 0
    chains = []
    for re_ref, im_ref, a_re, a_im, reverse in scans:
        for k in range(len(a_re)):
            chains.append((re_ref, im_ref, k, jnp.broadcast_to(a_re[k], (SUBLANES, LANES)),
                           jnp.broadcast_to(a_im[k], (SUBLANES, LANES)), reverse))

    def slab(i, reverse):
        j = seg - 1 - i if reverse else i
        return pl.ds(pl.multiple_of(j * SUBLANES, SUBLANES), SUBLANES)

    def local(i, carry):
        out = []
        for n, (re_ref, im_ref, k, ar, ai, reverse) in enumerate(chains):
            hr, hi = _cmul(ar, ai, carry[2 * n], carry[2 * n + 1])
            hr = hr + re_ref[k, slab(i, reverse), :]
            hi = hi + im_ref[k, slab(i, reverse), :]
            re_ref[k, slab(i, reverse), :] = hr
            im_ref[k, slab(i, reverse), :] = hi
            out += [hr, hi]
        return tuple(out)

    zero = jnp.zeros((SUBLANES, LANES), F32)
    end = lax.fori_loop(0, seg, local, (zero,) * (2 * len(chains)))

    row = lax.broadcasted_iota(jnp.int32, (SUBLANES, LANES), 0)
    enter = []
    for n, (_, _, _, ar, ai, reverse) in enumerate(chains):
        edge = SUBLANES - 1 if reverse else 0
        shift = SUBLANES - 1 if reverse else 1
        pr, pi = ar, ai
        for _ in range(seg.bit_length() - 1):
            pr, pi = _cmul(pr, pi, pr, pi)
        tr_, ti_ = zero, zero
        for _ in range(SUBLANES - 1):
            vr, vi = _cmul(pr, pi, tr_, ti_)
            tr_ = jnp.where(row == edge, 0.0, pltpu.roll(vr + end[2 * n], shift, 0))
            ti_ = jnp.where(row == edge, 0.0, pltpu.roll(vi + end[2 * n + 1], shift, 0))
        enter += [tr_, ti_]

    def fix(i, carry):
        out = []
        for n, (re_ref, im_ref, k, ar, ai, reverse) in enumerate(chains):
            er, ei = _cmul(ar, ai, carry[2 * n], carry[2 * n + 1])
            re_ref[k, slab(i, reverse), :] += er
            im_ref[k, slab(i, reverse), :] += ei
            out += [er, ei]
        return tuple(out)

    lax.fori_loop(0, seg, fix, tuple(enter))
    per_scan, off = [], 0
    for scan in scans:
        per_scan.append(enter[off:off + 2 * len(scan[2])])
        off += 2 * len(scan[2])
    return per_scan


def _to_slab(a):
    s, w = a.shape
    return a.reshape(SUBLANES, s // SUBLANES, w).swapaxes(0, 1).reshape(s, w)


def _from_slab(a):
    s, w = a.shape
    return a.reshape(s // SUBLANES, SUBLANES, w).swapaxes(0, 1).reshape(s, w)


def _lane_blocks(v, n_k):
    return [v[:, k * LANES:(k + 1) * LANES] for k in range(n_k)]


def _gather_k(ref, rows, n_k):
    return jnp.concatenate([ref[k, rows, :] for k in range(n_k)], axis=1)


def _dot(a, b, dims=(((1,), (0,)), ((), ()))):
    return lax.dot_general(a.astype(BF16), b.astype(BF16), dims, preferred_element_type=F32)


_NT = (((1,), (1,)), ((), ()))
_TN = (((0,), (0,)), ((), ()))


def _ssm_fwd(uz, b_blk, c_blk, par, d_skip, carry=None):
    seq = uz.shape[0]
    width = d_skip.shape[1]
    ns = SSM_CH // S5_GROUP * S5_STATE
    n_k = ns // LANES
    seg = seq // SUBLANES
    tb = min(SCAN_ROWS, seq)
    n_cb = width // SSM_CH
    ex_args, ex_in_specs, ex_out_specs, ex_out_shapes, ex_scratch = _carry_specs(carry)

    def body(*refs):
        refs, start_when, finish_when = _carry(carry, refs, 5, 2)
        start_when(pl.program_id(0) == 0)
        compute(*refs)
        finish_when(pl.program_id(0) == n_cb - 1)

    def compute(u_ref, b_ref, c_ref, par_ref, d_ref, y_ref, g_ref, hre, him):
        coef_r, coef_i = par_ref[0, 2:3, :], par_ref[0, 3:4, :]
        for c0 in range(0, seq, tb):
            rows = pl.ds(c0, tb)
            ub = u_ref[rows, :]
            bur, bui = _dot(ub, b_ref[0, 0]), _dot(ub, b_ref[0, 1])
            xr, xi = coef_r * bur - coef_i * bui, coef_r * bui + coef_i * bur
            for k in range(n_k):
                hre[k, rows, :] = xr[:, k * LANES:(k + 1) * LANES]
                him[k, rows, :] = xi[:, k * LANES:(k + 1) * LANES]
        _seg_scans([(hre, him, _lane_blocks(par_ref[0, 0:1, :], n_k), _lane_blocks(par_ref[0, 1:2, :], n_k), False)], seg)
        for c0 in range(0, seq, tb):
            rows = pl.ds(c0, tb)
            y = _dot(_gather_k(hre, rows, n_k), c_ref[0, 0]) - _dot(_gather_k(him, rows, n_k), c_ref[0, 1])
            y = y + d_ref[...] * u_ref[rows, :]
            y_ref[rows, :] = y
            g_ref[rows, :] = _gelu(y).astype(BF16)

    blk = pl.BlockSpec((seq, SSM_CH), lambda i: (0, i))
    params = _params(("parallel",)) if carry is None else pltpu.CompilerParams(
        vmem_limit_bytes=VMEM_LIMIT, dimension_semantics=("arbitrary",), has_side_effects=True)
    return _pcall(
        body, name="ssm_fwd", grid=(n_cb,),
        in_specs=[blk,
                  pl.BlockSpec((1, 2, SSM_CH, ns), lambda i: (i, 0, 0, 0)),
                  pl.BlockSpec((1, 2, ns, SSM_CH), lambda i: (i, 0, 0, 0)),
                  pl.BlockSpec((1, 4, ns), lambda i: (i, 0, 0)),
                  pl.BlockSpec((1, SSM_CH), lambda i: (0, i))] + ex_in_specs,
        out_specs=[blk, blk] + ex_out_specs,
        out_shape=[jax.ShapeDtypeStruct((seq, width), F32), jax.ShapeDtypeStruct((seq, width), BF16)] + ex_out_shapes,
        scratch_shapes=[pltpu.VMEM((n_k, seq, LANES), F32), pltpu.VMEM((n_k, seq, LANES), F32)] + ex_scratch,
        compiler_params=params,
    )(uz, b_blk, c_blk, par, d_skip, *ex_args)


def _ssm_bwd(uz, dys, b_blk, ct_blk, par, d_skip, carry=None):
    seq = uz.shape[0]
    width = d_skip.shape[1]
    ns_all = SSM_CH // S5_GROUP * S5_STATE
    n_half = 2
    ns = ns_all // n_half
    n_k = ns // LANES
    seg = seq // SUBLANES
    tb = min(SCAN_ROWS, seq)
    n_cb = width // SSM_CH
    ex_args, ex_in_specs, ex_out_specs, ex_out_shapes, ex_scratch = _carry_specs(carry)

    def body(*refs):
        refs, start_when, finish_when = _carry(carry, refs, 6, 5)
        step = pl.program_id(0) * n_half + pl.program_id(1)
        start_when(step == 0)
        compute(*refs)
        finish_when(step == n_cb * n_half - 1)

    def compute(u_ref, dys_ref, b_ref, ct_ref, par_ref, d_ref,
                du_ref, dbt_ref, dct_ref, dpar_ref, dd_ref, hre, him, gre, gim):
        half = pl.program_id(1)
        a_r, a_i = par_ref[0, 0:1, :], par_ref[0, 1:2, :]
        coef_r, coef_i = par_ref[0, 2:3, :], par_ref[0, 3:4, :]

        def dys_of(rows):
            return dys_ref[rows, :]

        for c0 in range(0, seq, tb):
            rows = pl.ds(c0, tb)
            ub = u_ref[rows, :]
            bur, bui = _dot(ub, b_ref[0, 0]), _dot(ub, b_ref[0, 1])
            xr, xi = coef_r * bur - coef_i * bui, coef_r * bui + coef_i * bur
            dys = dys_of(rows)
            gr, gi = _dot(dys, ct_ref[0, 0]), -_dot(dys, ct_ref[0, 1])
            for k in range(n_k):
                lanes = slice(k * LANES, (k + 1) * LANES)
                hre[k, rows, :] = xr[:, lanes]
                him[k, rows, :] = xi[:, lanes]
                gre[k, rows, :] = gr[:, lanes]
                gim[k, rows, :] = gi[:, lanes]
        enter, _ = _seg_scans([(hre, him, _lane_blocks(a_r, n_k), _lane_blocks(a_i, n_k), False),
                               (gre, gim, _lane_blocks(a_r, n_k), _lane_blocks(-a_i, n_k), True)], seg)

        def corr(j, carry):
            acc, prev = carry
            acc_o, prev_o = [], []
            for k in range(n_k):
                sl = pl.ds(pl.multiple_of(j * SUBLANES, SUBLANES), SUBLANES)
                g_r, g_i = gre[k, sl, :], gim[k, sl, :]
                p_r, p_i = prev[2 * k], prev[2 * k + 1]
                acc_o += [acc[2 * k] + g_r * p_r + g_i * p_i, acc[2 * k + 1] + g_i * p_r - g_r * p_i]
                prev_o += [hre[k, sl, :], him[k, sl, :]]
            return tuple(acc_o), tuple(prev_o)

        zero = jnp.zeros((SUBLANES, LANES), F32)
        acc, _ = lax.fori_loop(0, seg, corr, ((zero,) * (2 * n_k), tuple(enter)), unroll=SCAN_UNROLL)
        da_r = jnp.concatenate([_colsum(acc[2 * k]) for k in range(n_k)], axis=1)
        da_i = jnp.concatenate([_colsum(acc[2 * k + 1]) for k in range(n_k)], axis=1)

        zeros_cn = jnp.zeros((SSM_CH, ns), F32)
        qt_r, qt_i, dct_r, dct_i = zeros_cn, zeros_cn, zeros_cn, zeros_cn
        dd = jnp.zeros((1, SSM_CH), F32)
        first = half == 0
        for c0 in range(0, seq, tb):
            rows = pl.ds(c0, tb)
            ub = u_ref[rows, :]
            dys = dys_of(rows)
            dct_r = dct_r + _dot(dys, _gather_k(hre, rows, n_k), _TN)
            dct_i = dct_i - _dot(dys, _gather_k(him, rows, n_k), _TN)
            g_r, g_i = _gather_k(gre, rows, n_k), _gather_k(gim, rows, n_k)
            qt_r = qt_r + _dot(ub, g_r, _TN)
            qt_i = qt_i + _dot(ub, g_i, _TN)
            dbu_r, dbu_i = coef_r * g_r + coef_i * g_i, coef_r * g_i - coef_i * g_r
            du = _dot(dbu_r, b_ref[0, 0], _NT) + _dot(dbu_i, b_ref[0, 1], _NT)
            dd = dd + _colsum(dys * ub)

            @pl.when(first)
            def _(du=du, dys=dys, rows=rows):
                du_ref[rows, :] = du + d_ref[...] * dys

            @pl.when(jnp.logical_not(first))
            def _(du=du, rows=rows):
                du_ref[rows, :] += du

        @pl.when(first)
        def _():
            dd_ref[...] = dd

        b_r, b_i = b_ref[0, 0], b_ref[0, 1]
        dbt_ref[0, 0] = coef_r * qt_r + coef_i * qt_i
        dbt_ref[0, 1] = coef_r * qt_i - coef_i * qt_r
        dct_ref[0, 0] = dct_r
        dct_ref[0, 1] = dct_i
        dpar_ref[0, 0:1, :] = da_r
        dpar_ref[0, 1:2, :] = da_i
        dpar_ref[0, 2:3, :] = _colsum(b_r * qt_r + b_i * qt_i)
        dpar_ref[0, 3:4, :] = _colsum(b_r * qt_i - b_i * qt_r)

    blk = lambda i, h: (0, i)
    params = _params(("parallel", "arbitrary")) if carry is None else pltpu.CompilerParams(
        vmem_limit_bytes=VMEM_LIMIT, dimension_semantics=("arbitrary", "arbitrary"), has_side_effects=True)
    return _pcall(
        body, name="ssm_bwd", grid=(n_cb, n_half),
        in_specs=[pl.BlockSpec((seq, SSM_CH), blk), pl.BlockSpec((seq, SSM_CH), blk),
                  pl.BlockSpec((1, 2, SSM_CH, ns), lambda i, h: (i, 0, 0, h)),
                  pl.BlockSpec((1, 2, SSM_CH, ns), lambda i, h: (i, 0, 0, h)),
                  pl.BlockSpec((1, 4, ns), lambda i, h: (i, 0, h)),
                  pl.BlockSpec((1, SSM_CH), blk)] + ex_in_specs,
        out_specs=[pl.BlockSpec((seq, SSM_CH), blk),
                   pl.BlockSpec((1, 2, SSM_CH, ns), lambda i, h: (i, 0, 0, h)),
                   pl.BlockSpec((1, 2, SSM_CH, ns), lambda i, h: (i, 0, 0, h)),
                   pl.BlockSpec((1, 4, ns), lambda i, h: (i, 0, h)),
                   pl.BlockSpec((1, SSM_CH), blk)] + ex_out_specs,
        out_shape=[jax.ShapeDtypeStruct((seq, width), F32),
                   jax.ShapeDtypeStruct((n_cb, 2, SSM_CH, ns_all), F32),
                   jax.ShapeDtypeStruct((n_cb, 2, SSM_CH, ns_all), F32),
                   jax.ShapeDtypeStruct((n_cb, 4, ns_all), F32),
                   jax.ShapeDtypeStruct((1, width), F32)] + ex_out_shapes,
        scratch_shapes=[pltpu.VMEM((n_k, seq, LANES), F32) for _ in range(4)] + ex_scratch,
        compiler_params=params,
    )(uz, dys, b_blk, ct_blk, par, d_skip, *ex_args)


def _ssm_discretize(a_re, a_im, log_dt):
    dt = jnp.exp(log_dt)[:, None]
    mag = jnp.exp(a_re * dt)
    abar_re = mag * jnp.cos(a_im * dt)
    abar_im = mag * jnp.sin(a_im * dt)
    den = a_re * a_re + a_im * a_im
    nr = abar_re - 1.0
    coef_re = (nr * a_re + abar_im * a_im) / den
    coef_im = (abar_im * a_re - nr * a_im) / den
    return abar_re, abar_im, coef_re, coef_im


def _block_diag(w_gcp):
    gpb = SSM_CH // S5_GROUP
    n_cb = w_gcp.shape[0] // gpb
    w = w_gcp.reshape(n_cb, gpb, S5_GROUP, 1, S5_STATE)
    eye = jnp.eye(gpb, dtype=w.dtype)[None, :, None, :, None]
    return (w * eye).reshape(n_cb, SSM_CH, gpb * S5_STATE)


def _block_diag_extract(w_blk):
    gpb = SSM_CH // S5_GROUP
    n_cb = w_blk.shape[0]
    w = w_blk.reshape(n_cb, gpb, S5_GROUP, gpb, S5_STATE)
    w = jnp.moveaxis(jnp.diagonal(w, axis1=1, axis2=3), -1, 1)
    return w.reshape(n_cb * gpb, S5_GROUP, S5_STATE)


def _split3(x):
    hi = x.astype(BF16)
    mid = (x - hi.astype(F32)).astype(BF16)
    lo = (x - hi.astype(F32) - mid.astype(F32)).astype(BF16)
    return hi, mid, lo


def _tri_sum(tri, x):
    hi, mid, lo = _split3(x)
    return (jnp.dot(tri, hi, preferred_element_type=F32) + jnp.dot(tri, mid, preferred_element_type=F32)
            + jnp.dot(tri, lo, preferred_element_type=F32))


def _log_sigmoid(x):
    return jnp.minimum(x, 0.0) - jnp.log(1.0 + jnp.exp(-jnp.abs(x)))


def _cum_fwd(fl, b_f):
    seq = fl.shape[0]
    t = min(CUM_TILE, seq)

    def body(fl_ref, b_ref, o_ref, carry):
        @pl.when(pl.program_id(0) == 0)
        def _():
            carry[...] = jnp.zeros_like(carry)

        r = lax.broadcasted_iota(jnp.int32, (t, t), 0)
        c = lax.broadcasted_iota(jnp.int32, (t, t), 1)
        tri = (c <= r).astype(BF16)
        cum = _tri_sum(tri, _log_sigmoid(fl_ref[...] + b_ref[...])) + carry[...]
        o_ref[...] = cum
        carry[...] = cum[t - 1:t, :]

    return _pcall(
        body, name="cum_fwd", grid=(seq // t,),
        in_specs=[pl.BlockSpec((t, LANES), lambda i: (i, 0)), pl.BlockSpec((1, LANES), lambda i: (0, 0))],
        out_specs=pl.BlockSpec((t, LANES), lambda i: (i, 0)),
        out_shape=jax.ShapeDtypeStruct((seq, LANES), F32),
        scratch_shapes=[pltpu.VMEM((1, LANES), F32)],
        compiler_params=_params(("arbitrary",)),
    )(fl, b_f)


def _cum_bwd(dcum, fl, b_f):
    seq = fl.shape[0]
    t = min(CUM_TILE, seq)
    nb = seq // t

    def body(dc_ref, fl_ref, b_ref, o_ref, db_ref, carry):
        @pl.when(pl.program_id(0) == 0)
        def _():
            carry[...] = jnp.zeros_like(carry)
            db_ref[...] = jnp.zeros_like(db_ref)

        r = lax.broadcasted_iota(jnp.int32, (t, t), 0)
        c = lax.broadcasted_iota(jnp.int32, (t, t), 1)
        tri = (c >= r).astype(BF16)
        rev = _tri_sum(tri, dc_ref[...]) + carry[...]
        carry[...] = rev[0:1, :]
        dfl = rev * _sigmoid(-(fl_ref[...] + b_ref[...]))
        o_ref[...] = dfl
        db_ref[...] += _colsum(dfl)

    return _pcall(
        body, name="cum_bwd", grid=(nb,),
        in_specs=[pl.BlockSpec((t, LANES), lambda i: (nb - 1 - i, 0)), pl.BlockSpec((t, LANES), lambda i: (nb - 1 - i, 0)),
                  pl.BlockSpec((1, LANES), lambda i: (0, 0))],
        out_specs=[pl.BlockSpec((t, LANES), lambda i: (nb - 1 - i, 0)), pl.BlockSpec((1, LANES), lambda i: (0, 0))],
        out_shape=[jax.ShapeDtypeStruct((seq, LANES), F32), jax.ShapeDtypeStruct((1, LANES), F32)],
        scratch_shapes=[pltpu.VMEM((1, LANES), F32)],
        compiler_params=_params(("arbitrary",)),
    )(dcum, fl, b_f)


def _att_scores(q, kb, ck, row0, col0, masked):
    s = _dot(q, kb, _NT) - ck
    if masked:
        rows = row0 + lax.broadcasted_iota(jnp.int32, s.shape, 0)
        cols = col0 + lax.broadcasted_iota(jnp.int32, s.shape, 1)
        s = jnp.where(cols <= rows, s, NEG_INF)
    return s


def _pairwise_loop(lo, hi, step_fn, init):
    n = hi - lo
    w = ATT_UNROLL

    def several(p, carry):
        for u in range(w):
            carry = step_fn(lo + w * p + u, carry)
        return carry

    carry = lax.fori_loop(0, n // w, several, init)
    return lax.fori_loop(lo + (n // w) * w, hi, step_fn, carry)


def _att_fwd(qz, kv, ck):
    seq = qz.shape[0]
    heads = ck.shape[0]
    t = min(ATT_TILE, seq)
    scale = HEAD_DIM ** -0.5

    def body(q_ref, z_ref, k_ref, v_ref, ck_ref, o_ref, og_ref, lse_ref):
        i = pl.program_id(1)
        q = (q_ref[...] * scale).astype(BF16)

        def block(j, carry, masked):
            m, l, acc = carry
            rows = pl.ds(pl.multiple_of(j * t, t), t)
            s = _att_scores(q, k_ref[rows, :], ck_ref[0, j], i * t, j * t, masked)
            m_new = jnp.maximum(m, jnp.max(s, axis=1, keepdims=True))
            p = jnp.exp(s - m_new)
            alpha = jnp.exp(m - m_new)
            p_hi = p.astype(BF16)
            p_lo = (p - p_hi.astype(F32)).astype(BF16)
            vb = v_ref[rows, :]
            return (m_new, alpha * l + jnp.sum(p, axis=1, keepdims=True),
                    alpha * acc + (_dot(p_hi, vb) + _dot(p_lo, vb)))

        init = (jnp.full((t, 1), NEG_INF, F32), jnp.zeros((t, 1), F32), jnp.zeros((t, HEAD_DIM), F32))
        carry = _pairwise_loop(0, i, functools.partial(block, masked=False), init)
        m, l, acc = block(i, carry, True)
        o = acc / l
        o_ref[...] = o
        og_ref[...] = (o * _silu(z_ref[...])).astype(BF16)
        lse_ref[0] = m + jnp.log(l)

    qblk = pl.BlockSpec((t, HEAD_DIM), lambda h, i: (i, h))
    return _pcall(
        body, name="att_fwd", grid=(heads, seq // t),
        in_specs=[qblk, pl.BlockSpec((t, HEAD_DIM), lambda h, i: (i, heads + h)),
                  pl.BlockSpec((seq, HEAD_DIM), lambda h, i: (0, h)),
                  pl.BlockSpec((seq, HEAD_DIM), lambda h, i: (0, heads + h)),
                  pl.BlockSpec((1, seq // t, 1, t), lambda h, i: (h, 0, 0, 0))],
        out_specs=[qblk, qblk, pl.BlockSpec((1, t, 1), lambda h, i: (h, i, 0))],
        out_shape=[jax.ShapeDtypeStruct((seq, heads * HEAD_DIM), F32), jax.ShapeDtypeStruct((seq, heads * HEAD_DIM), BF16),
                   jax.ShapeDtypeStruct((heads, seq, 1), F32)],
        compiler_params=_params(("parallel", "parallel")),
    )(qz, qz, kv, kv, ck)


def _att_bwd(qz, kv, do, ox, lse, ck):
    seq = qz.shape[0]
    heads = ck.shape[0]
    t = min(ATT_TILE, seq)
    nq = seq // t
    scale = HEAD_DIM ** -0.5

    def body(q_ref, k_ref, v_ref, do_ref, ox_ref, lse_ref, ck_ref, dq_ref, dk_ref, dv_ref, dck_ref, dq_acc, delta):
        j = pl.program_id(1)

        @pl.when(j == 0)
        def _():
            for c0 in range(0, seq, t):
                rows = pl.ds(c0, t)
                delta[rows, :] = jnp.sum(do_ref[rows, :].astype(BF16).astype(F32) * ox_ref[rows, :], axis=1, keepdims=True)
                dq_acc[rows, :] = jnp.zeros((t, HEAD_DIM), F32)

        kb, vb = k_ref[...], v_ref[...]
        ckv = ck_ref[0, 0]

        def block(i, carry, masked):
            dk, dv, dck = carry
            rows = pl.ds(pl.multiple_of(i * t, t), t)
            qb = (q_ref[rows, :] * scale).astype(BF16)
            dob = do_ref[rows, :].astype(BF16)
            s = _att_scores(qb, kb, ckv, i * t, j * t, masked)
            p = jnp.exp(s - lse_ref[0, rows, :])
            ds = p * (_dot(dob, vb, _NT) - delta[rows, :])
            dq_acc[rows, :] += _dot(ds, kb)
            return dk + _dot(ds, qb, _TN), dv + _dot(p, dob, _TN), dck - _colsum(ds)

        init = (jnp.zeros((t, HEAD_DIM), F32), jnp.zeros((t, HEAD_DIM), F32), jnp.zeros((1, t), F32))
        carry = block(j, init, True)
        dk, dv, dck = _pairwise_loop(j + 1, nq, functools.partial(block, masked=False), carry)
        dk_ref[...] = dk.astype(dk_ref.dtype)
        dv_ref[...] = dv.astype(dv_ref.dtype)
        dck_ref[0, 0] = dck

        @pl.when(j == nq - 1)
        def _():
            for c0 in range(0, seq, t):
                rows = pl.ds(c0, t)
                dq_ref[rows, :] = (dq_acc[rows, :] * scale).astype(dq_ref.dtype)

    head = pl.BlockSpec((seq, HEAD_DIM), lambda h, j: (0, h))
    col = pl.BlockSpec((1, seq, 1), lambda h, j: (h, 0, 0))
    kblk = pl.BlockSpec((t, HEAD_DIM), lambda h, j: (j, h))
    row = pl.BlockSpec((1, 1, 1, t), lambda h, j: (h, j, 0, 0))
    return _pcall(
        body, name="att_bwd", grid=(heads, nq),
        in_specs=[head, kblk, pl.BlockSpec((t, HEAD_DIM), lambda h, j: (j, heads + h)), head, head, col, row],
        out_specs=[head, kblk, kblk, row],
        out_shape=[jax.ShapeDtypeStruct((seq, heads * HEAD_DIM), BF16)] * 3 + [jax.ShapeDtypeStruct((heads, nq, 1, t), F32)],
        scratch_shapes=[pltpu.VMEM((seq, HEAD_DIM), F32), pltpu.VMEM((seq, 1), F32)],
        compiler_params=_params(("parallel", "arbitrary")),
    )(qz, kv, kv, do, ox, lse, ck)


def _mesh_pos():
    return lax.axis_index("x"), lax.axis_index("y"), lax.axis_index("c")


def _other_chips(x, y):
    return [(1 - x, y), (x, 1 - y), (1 - x, 1 - y)]


def _all_gather_weights(big, small):
    nb, ns = len(big), len(small)
    n_remote = 3 * (nb + ns)

    def plan(ins, outs, sems):
        send_sems, recv_sems, fwd_send, fwd_recv = sems
        x, y, c = _mesh_pos()
        chips = _other_chips(x, y)
        slots = [2 * cx + cy for cx, cy in chips]

        def half(ref, hc):
            rh = ref.shape[-2] // 2
            return ref.at[pl.ds(hc * rh, rh), :]

        def remote(i, j, src_chip, from_in):
            if i < nb:
                src = half(ins[i], c) if from_in else half(outs[i].at[src_chip], c)
                dst = half(outs[i].at[src_chip], c)
            else:
                src = ins[i] if from_in else outs[i].at[src_chip]
                dst = outs[i].at[src_chip]
            k = 3 * i + j
            return pltpu.make_async_remote_copy(src_ref=src, dst_ref=dst, send_sem=send_sems.at[k],
                                                recv_sem=recv_sems.at[k], device_id=(*chips[j], c),
                                                device_id_type=MESH_ID)

        def forward(i, j, hc):
            part = half(outs[i].at[slots[j]], hc)
            k = 3 * i + j
            return pltpu.make_async_remote_copy(src_ref=part, dst_ref=part, send_sem=fwd_send.at[k],
                                                recv_sem=fwd_recv.at[k], device_id=(x, y, 1 - c),
                                                device_id_type=MESH_ID)

        return remote, forward, 2 * x + y, slots, c

    def start(ins, outs, sems):
        remote, _, me, _, _ = plan(ins, outs, sems)
        for i in range(nb + ns):
            for j in range(3):
                remote(i, j, me, True).start()

    def finish(ins, outs, sems):
        remote, forward, me, slots, c = plan(ins, outs, sems)
        for i in range(nb + ns):
            for j in range(3):
                remote(i, j, slots[j], False).wait_recv()
                if i < nb:
                    forward(i, j, c).start()
        for i in range(nb):
            for j in range(3):
                forward(i, j, 1 - c).wait_recv()
        for i in range(nb + ns):
            for j in range(3):
                remote(i, j, me, True).wait_send()
                if i < nb:
                    forward(i, j, c).wait_send()

    arrays = tuple(big) + tuple(small)
    return _Exchange(
        arrays=arrays,
        out_shapes=tuple(jax.ShapeDtypeStruct((N_CHIPS,) + a.shape, a.dtype) for a in arrays),
        scratch=(pltpu.SemaphoreType.DMA((n_remote,)), pltpu.SemaphoreType.DMA((n_remote,)),
                 pltpu.SemaphoreType.DMA((3 * max(nb, 1),)), pltpu.SemaphoreType.DMA((3 * max(nb, 1),))),
        start=start, finish=finish)


def _swap_halves(grads):
    n = len(grads)

    def copies(ins, outs, sems):
        x, y, c = _mesh_pos()
        cps = []
        for i in range(n):
            rh = ins[i].shape[1] // 2
            cps.append(pltpu.make_async_remote_copy(
                src_ref=ins[i].at[:, pl.ds((1 - c) * rh, rh), :], dst_ref=outs[i], send_sem=sems[0].at[i],
                recv_sem=sems[1].at[i], device_id=(x, y, 1 - c), device_id_type=MESH_ID))
        return cps

    def start(ins, outs, sems):
        for cp in copies(ins, outs, sems):
            cp.start()

    def finish(ins, outs, sems):
        for cp in copies(ins, outs, sems):
            cp.wait()

    return _Exchange(
        arrays=tuple(grads),
        out_shapes=tuple(jax.ShapeDtypeStruct((g.shape[0], g.shape[1] // 2, g.shape[2]), g.dtype) for g in grads),
        scratch=(pltpu.SemaphoreType.DMA((n,)), pltpu.SemaphoreType.DMA((n,))),
        start=start, finish=finish)


def _pair_sum_bf16(g, theirs, pos, name):
    n, rh, cdim = theirs.shape
    tr = min(ROW_TILE, rh)
    nb = rh // tr

    def body(pos_ref, g_ref, t_ref, o_ref):
        o_ref[...] = (g_ref[...] + t_ref[...]).astype(BF16)

    slot = lambda s, pos: (pos[0] + 1 + s) % n
    grid_spec = pltpu.PrefetchScalarGridSpec(
        num_scalar_prefetch=1, grid=(n - 1, nb),
        in_specs=[pl.BlockSpec((None, tr, cdim), lambda s, i, pos: (slot(s, pos), pos[1] * nb + i, 0)),
                  pl.BlockSpec((None, tr, cdim), lambda s, i, pos: (slot(s, pos), i, 0))],
        out_specs=pl.BlockSpec((None, tr, cdim), lambda s, i, pos: (slot(s, pos), i, 0)))
    return _pcall(body, name=name, grid_spec=grid_spec, out_shape=jax.ShapeDtypeStruct(theirs.shape, BF16),
                  compiler_params=_params(("parallel", "parallel")))(pos, g, theirs)


def _chip_sum(g, theirs, recv, pos, name):
    n, rh, cdim = theirs.shape
    tr = min(ROW_TILE, rh)
    nb = rh // tr

    def body(pos_ref, g_ref, t_ref, r0, r1, r2, o_ref):
        o_ref[...] = (((g_ref[...] + t_ref[...]) + r0[...]) + r1[...]) + r2[...]

    grid_spec = pltpu.PrefetchScalarGridSpec(
        num_scalar_prefetch=1, grid=(nb,),
        in_specs=[pl.BlockSpec((None, tr, cdim), lambda i, pos: (pos[0], pos[1] * nb + i, 0)),
                  pl.BlockSpec((None, tr, cdim), lambda i, pos: (pos[0], i, 0))]
        + [pl.BlockSpec((None, tr, cdim), functools.partial(lambda i, pos, j: (j, i, 0), j=j)) for j in range(3)],
        out_specs=pl.BlockSpec((tr, cdim), lambda i, pos: (pos[1] * nb + i, 0)))
    return _pcall(body, name=name, grid_spec=grid_spec, out_shape=jax.ShapeDtypeStruct((2 * rh, cdim), F32),
                  compiler_params=_params(("parallel",)))(pos, g, theirs, recv, recv, recv)


def _scatter_to_owner(parts):
    n = len(parts)

    def copies(ins, outs, sems):
        x, y, c = _mesh_pos()
        chips = _other_chips(x, y)
        cps = []
        for i in range(n):
            for j in range(3):
                k = 3 * i + j
                cps.append(pltpu.make_async_remote_copy(
                    src_ref=ins[i].at[2 * chips[j][0] + chips[j][1]], dst_ref=outs[i].at[j],
                    send_sem=sems[0].at[k], recv_sem=sems[1].at[k], device_id=(*chips[j], c),
                    device_id_type=MESH_ID))
        return cps

    def start(ins, outs, sems):
        for cp in copies(ins, outs, sems):
            cp.start()

    def finish(ins, outs, sems):
        for cp in copies(ins, outs, sems):
            cp.wait()

    return _Exchange(
        arrays=tuple(parts),
        out_shapes=tuple(jax.ShapeDtypeStruct((3,) + p.shape[1:], p.dtype) for p in parts),
        scratch=(pltpu.SemaphoreType.DMA((3 * n,)), pltpu.SemaphoreType.DMA((3 * n,))),
        start=start, finish=finish)


def _finish_reductions(v, shards):
    n = len(shards)
    n_rows = v.shape[0]
    rh = n_rows // 2
    assert rh % SUBLANES == 0

    def body(*refs):
        v_ref, o_ref, outs = refs[0], refs[1 + n], refs[2 + n:2 + 2 * n]
        part, recv, send_sems, recv_sems, join_send, join_recv = refs[2 + 2 * n:]
        x, y, c = _mesh_pos()
        sibling = (x, y, 1 - c)

        def join(i, hc):
            half = outs[i].shape[0] // 2
            rows = outs[i].at[pl.ds(hc * half, half), :]
            return pltpu.make_async_remote_copy(src_ref=rows, dst_ref=rows, send_sem=join_send.at[i],
                                                recv_sem=join_recv.at[i], device_id=sibling, device_id_type=MESH_ID)

        for i in range(n):
            join(i, c).start()

        mine = pl.ds(pl.multiple_of(c * rh, SUBLANES), rh)
        theirs = pl.ds(pl.multiple_of((1 - c) * rh, SUBLANES), rh)

        def exchange(s, src, dst, peer):
            cp = pltpu.make_async_remote_copy(src_ref=src, dst_ref=dst, send_sem=send_sems.at[s],
                                              recv_sem=recv_sems.at[s], device_id=peer, device_id_type=MESH_ID)
            cp.start()
            cp.wait()

        exchange(0, v_ref.at[theirs, :], recv.at[0], sibling)
        part[...] = v_ref[mine, :] + recv[0]
        for s, peer in ((1, (1 - x, y, c)), (2, (x, 1 - y, c))):
            exchange(s, part, recv.at[s], peer)
            part[...] = part[...] + recv[s]
        o_ref[mine, :] = part[...]
        exchange(3, part, o_ref.at[mine, :], sibling)

        for i in range(n):
            join(i, c).wait_send()
            join(i, 1 - c).wait_recv()

    vm = pl.BlockSpec(memory_space=pltpu.VMEM)
    res = _pcall(
        body, name="finish_reductions", in_specs=[vm] + [ANY] * n, out_specs=[vm] + [ANY] * n,
        out_shape=[jax.ShapeDtypeStruct(v.shape, v.dtype)] + [jax.ShapeDtypeStruct(s.shape, s.dtype) for s in shards],
        input_output_aliases={1 + i: 1 + i for i in range(n)},
        scratch_shapes=[pltpu.VMEM((rh, LANES), v.dtype), pltpu.VMEM((3, rh, LANES), v.dtype),
                        pltpu.SemaphoreType.DMA((4,)), pltpu.SemaphoreType.DMA((4,)),
                        pltpu.SemaphoreType.DMA((n,)), pltpu.SemaphoreType.DMA((n,))],
        compiler_params=pltpu.CompilerParams(vmem_limit_bytes=VMEM_LIMIT, has_side_effects=True),
    )(v, *shards)
    return res[0], res[1:]


def _adamw_math(w, g, m, v):
    m = ADAM_B1 * m + (1.0 - ADAM_B1) * g
    v = ADAM_B2 * v + (1.0 - ADAM_B2) * (g * g)
    m_hat = m / (1.0 - ADAM_B1 ** ADAM_STEP)
    v_hat = v / (1.0 - ADAM_B2 ** ADAM_STEP)
    delta = -ADAM_LR * (m_hat / (jnp.sqrt(v_hat) + ADAM_EPS) + ADAM_WD * w)
    return delta, m, v


def _adamw(w, g, m, v, name):
    wd = w.shape[1]
    return _rows(_adamw_math, [_full(w), _full(g), _full(m), _full(v)], [(wd, F32)] * 3, name=name)


def _rows_of(a):
    return -(-a.size // (LANES * SUBLANES)) * SUBLANES


def _pack(arrs, fill=0.0):
    parts = []
    for a in arrs:
        flat = a.reshape(-1)
        flat = jnp.pad(flat, (0, _rows_of(a) * LANES - a.size), constant_values=fill)
        parts.append(flat.reshape(-1, LANES))
    used = sum(p.shape[0] for p in parts)
    rows = -(-used // ROW_TILE) * ROW_TILE
    parts.append(jnp.full((rows - used, LANES), fill, F32))
    return jnp.concatenate(parts, axis=0)


def _unpack(buf, like):
    out, off = [], 0
    for a in like:
        out.append(buf[off:off + _rows_of(a)].reshape(-1)[:a.size].reshape(a.shape))
        off += _rows_of(a)
    return out


def kernel(x, norm_pre, norm_post, s5_w_in, s5_a_re, s5_a_im, s5_log_dt, s5_b_re, s5_b_im, s5_c_re, s5_c_im, s5_d, s5_w_glu, s5_b_glu, s5_w_out, kv_norm, kv_w, kv_b_f, fox_w_in, fox_w_out, loss_target, m_norm_pre, m_norm_post, m_s5_w_in, m_s5_a_re, m_s5_a_im, m_s5_log_dt, m_s5_b_re, m_s5_b_im, m_s5_c_re, m_s5_c_im, m_s5_d, m_s5_w_glu, m_s5_b_glu, m_s5_w_out, m_kv_norm, m_kv_w, m_kv_b_f, m_fox_w_in, m_fox_w_out, v_norm_pre, v_norm_post, v_s5_w_in, v_s5_a_re, v_s5_a_im, v_s5_log_dt, v_s5_b_re, v_s5_b_im, v_s5_c_re, v_s5_c_im, v_s5_d, v_s5_w_glu, v_s5_b_glu, v_s5_w_out, v_kv_norm, v_kv_w, v_kv_b_f, v_fox_w_in, v_fox_w_out):
    seq, dm = x.shape[1], x.shape[2]
    width = dm
    heads = dm // HEAD_DIM
    fw = heads * HEAD_DIM
    groups = width // S5_GROUP
    chip = 2 * lax.axis_index("x") + lax.axis_index("y")

    big_shards = [s5_w_in[0], s5_w_glu[0], s5_w_out[0], kv_w, fox_w_in[0], fox_w_out[0]]
    own_shards = [w.astype(BF16) for w in big_shards] + [s5_d, s5_b_glu]
    fill_own = lambda gs, owns: [lax.dynamic_update_slice(g, own[None], (chip, 0, 0)) for g, own in zip(gs, owns)]
    c_idx = lax.axis_index("c")
    pos = jnp.stack([chip, c_idx]).astype(jnp.int32)
    h0 = x[0]
    target = loss_target[0]
    g_pre0, g_pre1 = norm_pre[0:1], norm_pre[1:2]
    g_post0, g_post1 = norm_post[0:1], norm_post[1:2]
    g_kv = kv_norm.reshape(1, dm)
    first_owns = [own_shards[0], s5_d, s5_b_glu]
    xn1, *first_gathered = _rows(lambda h, g: (h * _rstd(h) * g,), [_full(h0)], [(dm, BF16)], consts=[g_pre0], name="norm_pre0",
                                 carry=_all_gather_weights(first_owns[:1], first_owns[1:]))
    g_win, g_d, g_bglu = fill_own(first_gathered, first_owns)
    gather_rest = _all_gather_weights(own_shards[1:4], [])
    gather_fox_in = _all_gather_weights(own_shards[4:5], [])
    gather_fox_out = _all_gather_weights(own_shards[5:6], [])
    cols = lambda g: jnp.moveaxis(g, 0, 1).reshape(g.shape[1], -1)
    rows = lambda g: g.reshape(-1, g.shape[2])
    w_in = g_win
    d_skip, b_glu = cols(g_d), cols(g_bglu)
    b_f = jnp.pad(kv_b_f, (0, LANES - heads)).reshape(1, LANES)

    a_re, a_im, log_dt = s5_a_re[0], s5_a_im[0], s5_log_dt[0]
    disc, disc_vjp = jax.vjp(_ssm_discretize, a_re, a_im, log_dt)
    gpb = SSM_CH // S5_GROUP
    n_cb = groups // gpb
    par = jnp.stack([p.reshape(n_cb, gpb * S5_STATE) for p in disc], axis=1)
    b_t = lambda b: jnp.swapaxes(b, 1, 2)
    b_blk = jnp.stack([_block_diag(b_t(s5_b_re[0])), _block_diag(b_t(s5_b_im[0]))], axis=1)
    ct_blk = jnp.stack([_block_diag(s5_c_re[0]), _block_diag(s5_c_im[0])], axis=1)
    c_blk = jnp.swapaxes(ct_blk, 2, 3)

    xn1 = _to_slab(xn1)
    uz = _mm(xn1, w_in, name="s5_in")
    ys, y1b, *rest = _ssm_fwd(uz, b_blk, c_blk, par, d_skip, carry=gather_rest)
    g_wglu, g_wout, g_kvw = fill_own(rest, own_shards[1:4])
    w_glu, w_out = rows(g_wglu), rows(g_wout)
    kvw_full = cols(g_kvw)
    w_kv = kvw_full[:, :2 * fw]
    w_f = jnp.pad(kvw_full[:, 2 * fw:], ((0, 0), (0, LANES - heads)))

    def gate_fn(a, y, z, b):
        return a, _gelu(y) * _sigmoid(a + b) * _silu(z)

    glu_a, y3b = _mm(y1b, w_glu, name="s5_glu", post=(gate_fn, [ys, (uz, width), b_glu], (F32, BF16)))
    o1 = _from_slab(_mm(y3b, w_out, name="s5_out"))

    def mid_fn(h, o, gp, gk, gq):
        h1 = h + o * _rstd(o) * gp
        r = _rstd(h1)
        return h1, h1 * r * gk, h1 * r * gq

    h1, xk, xn2 = _rows(mid_fn, [_full(h0), _full(o1)], [(dm, F32), (dm, BF16), (dm, BF16)],
                        consts=[g_post0, g_kv, g_pre1], name="mid_norms")

    kv, g_fwin = _mm(xk, w_kv, out_dtype=BF16, name="kv_proj", carry=gather_fox_in)
    fw_in = fill_own([g_fwin], own_shards[4:5])[0]
    fl = _mm(xk, w_f, name="f_proj")
    qz, g_fwout = _mm(xn2, fw_in, name="fox_in", carry=gather_fox_out)
    fw_out = rows(fill_own([g_fwout], own_shards[5:6])[0])
    cum = _cum_fwd(fl, b_f)
    t_att = min(ATT_TILE, seq)
    cum_t = cum[:, :heads].T
    ck = cum_t.reshape(heads, seq // t_att, 1, t_att)
    o, o2b, lse = _att_fwd(qz, kv, ck)
    o3 = _mm(o2b, fw_out, name="fox_out")

    def loss_fn(h, o, t, g):
        r = _rstd(o)
        err = h + o * r * g - t
        dh = err * (1.0 / dm)
        do, dg = _rms_bwd(o, g, dh)
        part = 0.5 * jnp.sum(jnp.mean(err * err, axis=-1, keepdims=True), axis=0, keepdims=True)
        return dh, do, jnp.broadcast_to(part, (1, LANES)), _colsum(dg)

    dh2, do3, loss_part, dg_post1 = _rows(loss_fn, [_full(h1), _full(o3), _full(target)], [(dm, F32), (dm, BF16)],
                                          consts=[g_post1], accs=[(1, LANES), (1, dm)], name="loss_head")
    loss = lax.psum(loss_part[0, 0], MESH_AXES)

    def fox_gate_bwd(d, a, z):
        return d * _silu(z), d * a * _dsilu(z)

    do, dz2 = _mm(do3, fw_out, tb=True, name="fox_out_dx", post=(fox_gate_bwd, [o, (qz, fw)], (F32, BF16)))
    dw_fout = _mm(o2b, do3, ta=True, name="fox_out_dw")
    dq, dk, dv, dck = _att_bwd(qz, kv, do, o, lse, ck)
    dcum = jnp.pad(dck.reshape(heads, seq).T, ((0, 0), (0, LANES - heads)))
    dfl, db_f = _cum_bwd(dcum, fl, b_f)
    dqz = (dq, dz2)
    dkv = (dk, dv)
    dxn2 = _mm(dqz, fw_in, tb=True, name="fox_in_dx")
    dw_fin = _mm(xn2, dqz, ta=True, out_split=N_CHIPS, name="fox_in_dw")
    dxk_f = _mm(dfl, w_f, tb=True, name="f_proj_dx")
    dxk = _mm(dkv, w_kv, tb=True, add=dxk_f, name="kv_proj_dx")
    dw_kv = _mm(xk, dkv, ta=True, name="kv_proj_dw")
    dw_f = _mm(xk, dfl, ta=True, name="f_proj_dw")

    def mid_bwd(d2, h, dq_, dk_, o, gq, gk, gp):
        dxa, dga = _rms_bwd(h, gq, dq_)
        dxb, dgb = _rms_bwd(h, gk, dk_)
        dh = d2 + dxa + dxb
        do_, dgp = _rms_bwd(o, gp, dh)
        return dh, do_, _colsum(dga), _colsum(dgb), _colsum(dgp)

    to_rows = lambda g: g.reshape(N_CHIPS, -1, g.shape[1])
    kv_cols = kv_w.shape[1]
    kv_col_block = lambda b: (dw_kv[:, b * kv_cols:(b + 1) * kv_cols] if (b + 1) * kv_cols <= 2 * fw else
                              jnp.concatenate([dw_kv[:, b * kv_cols:], dw_f[:, :heads]], axis=1))
    early_grads = [jnp.stack([kv_col_block(b) for b in range(N_CHIPS)]), dw_fin, to_rows(dw_fout)]
    dh1, do1, dg_pre1, dg_kv, dg_post0, *early_theirs = _rows(
        mid_bwd, [_full(dh2), _full(h1), _full(dxn2), _full(dxk), _full(o1)], [(dm, F32), (dm, BF16)],
        consts=[g_pre1, g_kv, g_post0], accs=[(1, dm)] * 3, name="mid_norms_bwd", carry=_swap_halves(early_grads))
    early_sums = [_pair_sum_bf16(g, t, pos, f"grad_pair_sum_{3 + i}") for i, (g, t) in enumerate(zip(early_grads, early_theirs))]

    do1 = _to_slab(do1)
    def gate_bwd(d3, y, a, z, b):
        y1 = _gelu(y)
        gate = _sigmoid(a + b)
        dy2 = d3 * _silu(z)
        return dy2 * gate, dy2 * y1 * gate * (1.0 - gate), d3 * (y1 * gate) * _dsilu(z)

    dy1_direct, da, dz = _mm(do1, w_out, tb=True, tm=512, name="s5_out_dx",
                             post=(gate_bwd, [ys, glu_a, (uz, width), b_glu], (F32, BF16, BF16)))
    dw_out = _mm(y3b, do1, ta=True, name="s5_out_dw")
    db_glu = _rows(lambda t: (_colsum(t.astype(F32)),), [_full(da)], [], accs=[(1, width)], name="s5_glu_bias_bwd")[0]
    dys = _mm(da, w_glu, tb=True, add=dy1_direct, post=(lambda d, y: d * _dgelu(y), [ys], None), name="s5_glu_dx")
    dw_glu = _mm(y1b, da, ta=True, name="s5_glu_dw")
    mid_grads = [to_rows(dw_glu), to_rows(dw_out)]
    du, dbt_blk, dct_blk, dpar, dd, *carried = _ssm_bwd(
        uz, dys, b_blk, ct_blk, par, d_skip, carry=_together(_scatter_to_owner(early_sums), _swap_halves(mid_grads)))
    early_recv, mid_theirs = carried[:3], carried[3:]
    mid_sums = [_pair_sum_bf16(g, t, pos, f"grad_pair_sum_{1 + i}") for i, (g, t) in enumerate(zip(mid_grads, mid_theirs))]
    duz = (du, dz)
    dw_in, *mid_recv = _mm(xn1, duz, ta=True, out_split=N_CHIPS, name="s5_in_dw", carry=_scatter_to_owner(mid_sums))
    late_grads = [dw_in]
    dxn1, *late_theirs = _mm(duz, w_in, tb=True, name="s5_in_dx", carry=_swap_halves(late_grads))
    dxn1 = _from_slab(dxn1)
    late_sums = [_pair_sum_bf16(dw_in, late_theirs[0], pos, "grad_pair_sum_0")]

    def first_bwd(d1, h, dxn, g):
        dx, dg = _rms_bwd(h, g, dxn)
        return d1 + dx, _colsum(dg)

    grad_x, dg_pre0, *late_recv = _rows(first_bwd, [_full(dh1), _full(h0), _full(dxn1)], [(dm, F32)], consts=[g_pre0],
                                        accs=[(1, dm)], name="norm_pre0_bwd", carry=_scatter_to_owner(late_sums))

    dpar_g = [dpar[:, i, :].reshape(groups, S5_STATE) for i in range(4)]
    da_re, da_im, dlog_dt = disc_vjp(tuple(dpar_g))
    db_re = jnp.swapaxes(_block_diag_extract(dbt_blk[:, 0]), 1, 2)
    db_im = jnp.swapaxes(_block_diag_extract(dbt_blk[:, 1]), 1, 2)
    dc_re = _block_diag_extract(dct_blk[:, 0])
    dc_im = _block_diag_extract(dct_blk[:, 1])

    big_grads = late_grads + mid_grads + early_grads
    theirs = list(late_theirs) + list(mid_theirs) + list(early_theirs)
    received = list(late_recv) + list(mid_recv) + list(early_recv)
    halves = [_chip_sum(g, t, r, pos, f"grad_chip_sum_{i}") for i, (g, t, r) in enumerate(zip(big_grads, theirs, received))]

    small_local = [jnp.concatenate([dg_pre0, dg_pre1]), jnp.concatenate([dg_post0, dg_post1]),
                   da_re[None], da_im[None], dlog_dt[None], db_re[None], db_im[None], dc_re[None], dc_im[None],
                   dd, db_glu, dg_kv.reshape(dm), db_f[0, :heads]]
    small_buf, (g_win_s, g_wglu_s, g_wout_s, g_kvw_s, g_fwin_s, g_fwout_s) = _finish_reductions(_pack(small_local), halves)
    (g_norm_pre, g_norm_post, g_a_re, g_a_im, g_log_dt, g_b_re, g_b_im, g_c_re, g_c_im, g_d_full, g_bglu_full,
     g_kv_norm, g_b_f) = _unpack(small_buf, small_local)
    shard = width // N_CHIPS
    g_d_own = lax.dynamic_slice(g_d_full, (0, chip * shard), (1, shard))
    g_bglu_own = lax.dynamic_slice(g_bglu_full, (0, chip * shard), (1, shard))

    big_w = big_shards
    big_g = [g_win_s, g_wglu_s, g_wout_s, g_kvw_s, g_fwin_s, g_fwout_s]
    big_m = [m_s5_w_in[0], m_s5_w_glu[0], m_s5_w_out[0], m_kv_w, m_fox_w_in[0], m_fox_w_out[0]]
    big_v = [v_s5_w_in[0], v_s5_w_glu[0], v_s5_w_out[0], v_kv_w, v_fox_w_in[0], v_fox_w_out[0]]
    big_upd = [_adamw(w, g, m, v, f"adamw_{i}") for i, (w, g, m, v) in enumerate(zip(big_w, big_g, big_m, big_v))]

    small_names = ["norm_pre", "norm_post", "s5_a_re", "s5_a_im", "s5_log_dt", "s5_b_re", "s5_b_im", "s5_c_re", "s5_c_im",
                   "s5_d", "s5_b_glu", "kv_norm", "kv_b_f"]
    small_w = [norm_pre, norm_post, s5_a_re, s5_a_im, s5_log_dt, s5_b_re, s5_b_im, s5_c_re, s5_c_im, s5_d, s5_b_glu, kv_norm, kv_b_f]
    small_m = [m_norm_pre, m_norm_post, m_s5_a_re, m_s5_a_im, m_s5_log_dt, m_s5_b_re, m_s5_b_im, m_s5_c_re, m_s5_c_im, m_s5_d, m_s5_b_glu, m_kv_norm, m_kv_b_f]
    small_v = [v_norm_pre, v_norm_post, v_s5_a_re, v_s5_a_im, v_s5_log_dt, v_s5_b_re, v_s5_b_im, v_s5_c_re, v_s5_c_im, v_s5_d, v_s5_b_glu, v_kv_norm, v_kv_b_f]
    small_g = [g_norm_pre, g_norm_post, g_a_re, g_a_im, g_log_dt, g_b_re, g_b_im, g_c_re, g_c_im, g_d_own, g_bglu_own, g_kv_norm, g_b_f]
    small_g = [g.reshape(w.shape) for g, w in zip(small_g, small_w)]
    sd, sm, sv = _adamw(_pack(small_w), _pack(small_g), _pack(small_m), _pack(small_v, fill=1.0), "adamw_small")
    small_delta, small_newm, small_newv = _unpack(sd, small_w), _unpack(sm, small_w), _unpack(sv, small_w)

    order = ["norm_pre", "norm_post", "s5_w_in", "s5_a_re", "s5_a_im", "s5_log_dt", "s5_b_re", "s5_b_im", "s5_c_re", "s5_c_im",
             "s5_d", "s5_w_glu", "s5_b_glu", "s5_w_out", "kv_norm", "kv_w", "kv_b_f", "fox_w_in", "fox_w_out"]
    big_names = ["s5_w_in", "s5_w_glu", "s5_w_out", "kv_w", "fox_w_in", "fox_w_out"]
    big_like = [s5_w_in, s5_w_glu, s5_w_out, kv_w, fox_w_in, fox_w_out]
    grads, deltas, new_m, new_v = {}, {}, {}, {}
    for i, n in enumerate(big_names):
        shp = big_like[i].shape
        grads[n] = big_g[i].reshape(shp)
        deltas[n], new_m[n], new_v[n] = (a.reshape(shp) for a in big_upd[i])
    for i, n in enumerate(small_names):
        grads[n], deltas[n], new_m[n], new_v[n] = small_g[i], small_delta[i], small_newm[i], small_newv[i]

    return (loss, grad_x[None], *[grads[n] for n in order], *[deltas[n] for n in order],
            *[new_m[n] for n in order], *[new_v[n] for n in order])
```

```python
import functools
import math
from typing import Callable, NamedTuple

import jax
import jax.numpy as jnp
from jax import lax
from jax.experimental import pallas as pl
from jax.experimental.pallas import tpu as pltpu

F32 = jnp.float32
BF16 = jnp.bfloat16

S5_GROUP = 16
S5_STATE = 64
HEAD_DIM = 128
RMS_EPS = 1e-6
NEG_INF = -1e30
ADAM_LR = 0.001
ADAM_B1 = 0.9
ADAM_B2 = 0.999
ADAM_EPS = 1e-08
ADAM_WD = 0.01
ADAM_STEP = 10

LANES = 128
SUBLANES = 8
VMEM_LIMIT = 56 * 1024 * 1024
N_CHIPS = 4
MESH_AXES = ("x", "y", "c")
MESH_ID = pl.DeviceIdType.MESH

SSM_CH = 128
ROW_TILE = 256
SCAN_ROWS = 512
SCAN_UNROLL = 4
ATT_TILE = 512
CUM_TILE = 512


def _pcall(body, **kw):
    return pl.pallas_call(body, **kw)


def _params(sem=None):
    if sem is None:
        return pltpu.CompilerParams(vmem_limit_bytes=VMEM_LIMIT)
    return pltpu.CompilerParams(vmem_limit_bytes=VMEM_LIMIT, dimension_semantics=sem)


def _sigmoid(x):
    return 1.0 / (1.0 + jnp.exp(-x))


def _silu(z):
    return z * _sigmoid(z)


def _dsilu(z):
    s = _sigmoid(z)
    return s * (1.0 + z * (1.0 - s))


_GELU_C = math.sqrt(2.0 / math.pi)


def _gelu(x):
    return 0.5 * x * (1.0 + jnp.tanh(_GELU_C * (x + 0.044715 * x * x * x)))


def _dgelu(x):
    t = jnp.tanh(_GELU_C * (x + 0.044715 * x * x * x))
    return 0.5 * (1.0 + t) + 0.5 * x * (1.0 - t * t) * _GELU_C * (1.0 + 3.0 * 0.044715 * x * x)


def _rstd(x):
    return lax.rsqrt(jnp.mean(x * x, axis=-1, keepdims=True) + RMS_EPS)


def _rms_bwd(x, g, dy):
    r = _rstd(x)
    dyg = dy * g
    dx = r * dyg - x * (r * r * r) * jnp.mean(dyg * x, axis=-1, keepdims=True)
    return dx, dy * (x * r)


def _colsum(v):
    return jnp.sum(v, axis=0, keepdims=True)


ANY = pl.BlockSpec(memory_space=pl.ANY)


class _Exchange(NamedTuple):
    arrays: tuple
    out_shapes: tuple
    scratch: tuple
    start: Callable
    finish: Callable


def _together(*exs):
    def parts(seq, field):
        out, off = [], 0
        for e in exs:
            n = len(getattr(e, field))
            out.append(seq[off:off + n])
            off += n
        return out

    def start(ins, outs, sems):
        for e, i, o, s in zip(exs, parts(ins, "arrays"), parts(outs, "out_shapes"), parts(sems, "scratch")):
            e.start(i, o, s)

    def finish(ins, outs, sems):
        for e, i, o, s in zip(exs, parts(ins, "arrays"), parts(outs, "out_shapes"), parts(sems, "scratch")):
            e.finish(i, o, s)

    return _Exchange(arrays=sum((tuple(e.arrays) for e in exs), ()), out_shapes=sum((tuple(e.out_shapes) for e in exs), ()),
                     scratch=sum((tuple(e.scratch) for e in exs), ()), start=start, finish=finish)


def _carry(ex, refs, n_fixed_in, n_fixed_out):
    if ex is None:
        return refs, lambda cond: None, lambda cond: None
    n_in, n_out, n_sem = len(ex.arrays), len(ex.out_shapes), len(ex.scratch)
    fixed_in = refs[:n_fixed_in]
    ex_in = refs[n_fixed_in:n_fixed_in + n_in]
    rest = refs[n_fixed_in + n_in:]
    fixed_out = rest[:n_fixed_out]
    ex_out = rest[n_fixed_out:n_fixed_out + n_out]
    scratch = rest[n_fixed_out + n_out:]
    sems = scratch[len(scratch) - n_sem:]

    def start_when(cond):
        pl.when(cond)(lambda: ex.start(ex_in, ex_out, sems))

    def finish_when(cond):
        pl.when(cond)(lambda: ex.finish(ex_in, ex_out, sems))

    return tuple(fixed_in) + tuple(fixed_out) + tuple(scratch[:len(scratch) - n_sem]), start_when, finish_when


def _carry_specs(ex):
    if ex is None:
        return (), [], [], [], []
    return ex.arrays, [ANY] * len(ex.arrays), [ANY] * len(ex.out_shapes), list(ex.out_shapes), list(ex.scratch)


def _mm(a, b, *, ta=False, tb=False, out_dtype=F32, add=None, post=None, out_split=1, tm=1024, tn=1024, tk=2048, name,
        carry=None):
    def describe(op):
        if isinstance(op, (tuple, list)):
            assert all(p.ndim == 2 and p.shape == op[0].shape for p in op)
            return list(op), op[0].shape[0], op[0].shape[1], False
        if op.ndim == 3:
            return [op], op.shape[1], op.shape[2], True
        return [op], op.shape[0], op.shape[1], False

    a_parts, a_rows, a_pc, a_stack = describe(a)
    b_parts, b_rows, b_pc, b_stack = describe(b)
    a_cols = a_pc * (a.shape[0] if a_stack else len(a_parts))
    b_cols = b_pc * (b.shape[0] if b_stack else len(b_parts))
    k_dim, m_dim = (a_rows, a_cols) if ta else (a_cols, a_rows)
    n_dim, kb = (b_rows, b_cols) if tb else (b_cols, b_rows)
    assert kb == k_dim, (k_dim, kb)
    tm = min(tm, a_pc) if ta else min(tm, m_dim)
    tk = min(tk, k_dim, k_dim if ta else a_pc, b_pc if tb else k_dim)
    tn = min(tn, n_dim // out_split, n_dim if tb else b_pc)
    a_ct, b_ct = (tm if ta else tk), (tk if tb else tn)
    assert m_dim % tm == 0 and n_dim % tn == 0 and k_dim % tk == 0 and a_pc % a_ct == 0 and b_pc % b_ct == 0
    assert (n_dim // out_split) % tn == 0
    nk = k_dim // tk
    dims = (((0 if ta else 1,), (1 if tb else 0,)), ((), ()))
    n_a, n_b = len(a_parts), len(b_parts)
    assert n_a == 1 or n_b == 1

    def operand_specs(parts, stack, rows_t, cols_t, per, row_of, col_of):
        specs = []
        for p in range(len(parts)):
            def col(i, j, k, p=p):
                return jnp.clip(col_of(i, j, k) - p * per, 0, per - 1) if len(parts) > 1 else col_of(i, j, k)
            if stack:
                specs.append(pl.BlockSpec((None, rows_t, cols_t),
                                          lambda i, j, k, col=col: (col(i, j, k) // per, row_of(i, j, k), col(i, j, k) % per)))
            else:
                specs.append(pl.BlockSpec((rows_t, cols_t), lambda i, j, k, col=col: (row_of(i, j, k), col(i, j, k))))
        return specs

    if ta:
        a_specs = operand_specs(a_parts, a_stack, tk, tm, a_pc // tm, lambda i, j, k: k, lambda i, j, k: i)
    else:
        a_specs = operand_specs(a_parts, a_stack, tm, tk, a_pc // tk, lambda i, j, k: i, lambda i, j, k: k)
    if tb:
        b_specs = operand_specs(b_parts, b_stack, tn, tk, b_pc // tk, lambda i, j, k: j, lambda i, j, k: k)
    else:
        b_specs = operand_specs(b_parts, b_stack, tk, tn, b_pc // tn, lambda i, j, k: k, lambda i, j, k: j)

    post_fn, post_arrays, post_dtypes = post if post is not None else (None, (), None)
    out_dtypes = tuple(post_dtypes) if post_dtypes else (out_dtype,)
    extras = ((add, 0),) * (add is not None) + tuple(e if isinstance(e, tuple) else (e, 0) for e in post_arrays)
    n_fixed_in = n_a + n_b + len(extras)
    n_out = len(out_dtypes)
    grid = (m_dim // tm, n_dim // tn, nk)
    ex_args, ex_in_specs, ex_out_specs, ex_out_shapes, ex_scratch = _carry_specs(carry)

    def body(*refs):
        refs, start_when, finish_when = _carry(carry, refs, n_fixed_in, n_out)
        i, j, k = pl.program_id(0), pl.program_id(1), pl.program_id(2)
        step = (i * grid[1] + j) * grid[2] + k
        start_when(step == 0)
        compute(*refs)
        finish_when(step == grid[0] * grid[1] * grid[2] - 1)

    def compute(*refs):
        a_refs, b_refs = refs[:n_a], refs[n_a:n_a + n_b]
        rest = refs[n_a + n_b:]
        extra_refs, o_refs = rest[:len(extras)], rest[len(extras):len(extras) + n_out]
        acc = None if nk == 1 else rest[-1]
        i, j, k = pl.program_id(0), pl.program_id(1), pl.program_id(2)

        def finish(res):
            tiles = [r[...] for r in extra_refs]
            if add is not None:
                res = res + tiles.pop(0)
            if post_fn is not None:
                res = post_fn(res, *tiles)
            res = res if isinstance(res, (tuple, list)) else (res,)
            for o_ref, r in zip(o_refs, res):
                o_ref[...] = r.astype(o_ref.dtype)

        def accumulate(a_ref, b_ref):
            prod = lax.dot_general(a_ref[...].astype(BF16), b_ref[...].astype(BF16), dims,
                                   preferred_element_type=F32)
            if nk == 1:
                finish(prod)
                return

            @pl.when(k == 0)
            def _():
                acc[...] = prod

            @pl.when(jnp.logical_and(k > 0, k < nk - 1))
            def _():
                acc[...] += prod

            @pl.when(k == nk - 1)
            def _():
                finish(acc[...] + prod)

        if n_a == 1 and n_b == 1:
            accumulate(a_refs[0], b_refs[0])
        else:
            many, block, per = (a_refs, (i if ta else k), a_pc // a_ct) if n_a > 1 else (b_refs, (k if tb else j), b_pc // b_ct)
            for p, ref in enumerate(many):
                @pl.when(block // per == p)
                def _(ref=ref):
                    accumulate(ref, b_refs[0]) if n_a > 1 else accumulate(a_refs[0], ref)

    per_out = n_dim // out_split // tn
    if out_split > 1:
        o_spec = pl.BlockSpec((None, tm, tn), lambda i, j, k: (j // per_out, i, j % per_out))
        out_shapes = [jax.ShapeDtypeStruct((out_split, m_dim, n_dim // out_split), dt) for dt in out_dtypes]
    else:
        o_spec = pl.BlockSpec((tm, tn), lambda i, j, k: (i, j))
        out_shapes = [jax.ShapeDtypeStruct((m_dim, n_dim), dt) for dt in out_dtypes]

    def extra_spec(arr, first_col):
        assert first_col % tn == 0
        off = first_col // tn
        if arr.shape[0] == 1:
            return pl.BlockSpec((1, tn), lambda i, j, k: (0, j + off))
        return pl.BlockSpec((tm, tn), lambda i, j, k: (i, j + off))

    in_specs = a_specs + b_specs + [extra_spec(arr, off) for arr, off in extras]
    args = tuple(a_parts) + tuple(b_parts) + tuple(arr for arr, _ in extras)
    acc_scratch = [pltpu.VMEM((tm, tn), F32)] if nk > 1 else []
    if carry is None:
        res = _pcall(
            body, name=name, grid=grid, in_specs=in_specs, out_specs=[o_spec] * n_out, out_shape=out_shapes,
            scratch_shapes=acc_scratch, compiler_params=_params(("parallel", "parallel", "arbitrary")),
        )(*args)
        return res[0] if n_out == 1 else res
    return _pcall(
        body, name=name, grid=grid, in_specs=in_specs + ex_in_specs, out_specs=[o_spec] * n_out + ex_out_specs,
        out_shape=out_shapes + ex_out_shapes, scratch_shapes=acc_scratch + ex_scratch,
        compiler_params=pltpu.CompilerParams(vmem_limit_bytes=VMEM_LIMIT, has_side_effects=True,
                                             dimension_semantics=("arbitrary", "arbitrary", "arbitrary")),
    )(*args, *ex_args)


def _rows(fn, ins, outs, *, name, consts=(), accs=(), carry=None):
    n_rows = ins[0][0].shape[0]
    tr = min(ROW_TILE, n_rows)
    assert n_rows % tr == 0
    n_in, n_c, n_out = len(ins), len(consts), len(outs)
    n_steps = n_rows // tr
    ex_args, ex_in_specs, ex_out_specs, ex_out_shapes, ex_scratch = _carry_specs(carry)

    def body(*refs):
        refs, start_when, finish_when = _carry(carry, refs, n_in + n_c, n_out + len(accs))
        start_when(pl.program_id(0) == 0)
        _compute(*refs)
        finish_when(pl.program_id(0) == n_steps - 1)

    def _compute(*refs):
        vals = [r[...] for r in refs[:n_in + n_c]]
        res = fn(*vals)
        res = res if isinstance(res, (tuple, list)) else (res,)
        o_refs = refs[n_in + n_c:]
        for r, v in zip(o_refs[:n_out], res[:n_out]):
            r[...] = v.astype(r.dtype)
        if accs:
            first = pl.program_id(0) == 0
            for r, v in zip(o_refs[n_out:], res[n_out:]):
                @pl.when(first)
                def _(r=r, v=v):
                    r[...] = v

                @pl.when(jnp.logical_not(first))
                def _(r=r, v=v):
                    r[...] += v

    in_specs = [pl.BlockSpec((tr, w), functools.partial(lambda i, cb: (i, cb), cb=cb)) for _, w, cb in ins]
    in_specs += [pl.BlockSpec(c.shape, functools.partial(lambda i, nd: (0,) * nd, nd=c.ndim)) for c in consts]
    out_specs = [pl.BlockSpec((tr, w), lambda i: (i, 0)) for w, _ in outs]
    out_specs += [pl.BlockSpec(s, lambda i: (0, 0)) for s in accs]
    out_shape = [jax.ShapeDtypeStruct((n_rows, w), dt) for w, dt in outs]
    out_shape += [jax.ShapeDtypeStruct(s, F32) for s in accs]
    sequential = bool(accs) or carry is not None
    params = _params(("arbitrary",) if sequential else ("parallel",))
    if carry is not None:
        params = pltpu.CompilerParams(vmem_limit_bytes=VMEM_LIMIT, dimension_semantics=("arbitrary",), has_side_effects=True)
    return _pcall(
        body, name=name, grid=(n_steps,), in_specs=in_specs + ex_in_specs, out_specs=out_specs + ex_out_specs,
        out_shape=out_shape + ex_out_shapes, scratch_shapes=ex_scratch, compiler_params=params,
    )(*[a for a, _, _ in ins], *consts, *ex_args)


def _full(a):
    return (a, a.shape[1], 0)


def _cmul(ar, ai, br, bi):
    return ar * br - ai * bi, ar * bi + ai * br


def _seg_scans(scans, seg):
    assert seg & (seg - 1) == 0
    chains = []
    for re_ref, im_ref, a_re, a_im, reverse in scans:
        for k in range(len(a_re)):
            chains.append((re_ref, im_ref, k, jnp.broadcast_to(a_re[k], (SUBLANES, LANES)),
                           jnp.broadcast_to(a_im[k], (SUBLANES, LANES)), reverse))

    def slab(i, reverse):
        j = seg - 1 - i if reverse else i
        return pl.ds(pl.multiple_of(j * SUBLANES, SUBLANES), SUBLANES)

    def local(i, carry):
        out = []
        for n, (re_ref, im_ref, k, ar, ai, reverse) in enumerate(chains):
            hr, hi = _cmul(ar, ai, carry[2 * n], carry[2 * n + 1])
            hr = hr + re_ref[k, slab(i, reverse), :]
            hi = hi + im_ref[k, slab(i, reverse), :]
            re_ref[k, slab(i, reverse), :] = hr
            im_ref[k, slab(i, reverse), :] = hi
            out += [hr, hi]
        return tuple(out)

    zero = jnp.zeros((SUBLANES, LANES), F32)
    end = lax.fori_loop(0, seg, local, (zero,) * (2 * len(chains)))

    row = lax.broadcasted_iota(jnp.int32, (SUBLANES, LANES), 0)
    enter = []
    for n, (_, _, _, ar, ai, reverse) in enumerate(chains):
        edge = SUBLANES - 1 if reverse else 0
        shift = SUBLANES - 1 if reverse else 1
        pr, pi = ar, ai
        for _ in range(seg.bit_length() - 1):
            pr, pi = _cmul(pr, pi, pr, pi)
        tr_, ti_ = zero, zero
        for _ in range(SUBLANES - 1):
            vr, vi = _cmul(pr, pi, tr_, ti_)
            tr_ = jnp.where(row == edge, 0.0, pltpu.roll(vr + end[2 * n], shift, 0))
            ti_ = jnp.where(row == edge, 0.0, pltpu.roll(vi + end[2 * n + 1], shift, 0))
        enter += [tr_, ti_]

    def fix(i, carry):
        out = []
        for n, (re_ref, im_ref, k, ar, ai, reverse) in enumerate(chains):
            er, ei = _cmul(ar, ai, carry[2 * n], carry[2 * n + 1])
            re_ref[k, slab(i, reverse), :] += er
            im_ref[k, slab(i, reverse), :] += ei
            out += [er, ei]
        return tuple(out)

    lax.fori_loop(0, seg, fix, tuple(enter))
    per_scan, off = [], 0
    for scan in scans:
        per_scan.append(enter[off:off + 2 * len(scan[2])])
        off += 2 * len(scan[2])
    return per_scan


def _to_slab(a):
    s, w = a.shape
    return a.reshape(SUBLANES, s // SUBLANES, w).swapaxes(0, 1).reshape(s, w)


def _from_slab(a):
    s, w = a.shape
    return a.reshape(s // SUBLANES, SUBLANES, w).swapaxes(0, 1).reshape(s, w)


def _lane_blocks(v, n_k):
    return [v[:, k * LANES:(k + 1) * LANES] for k in range(n_k)]


def _gather_k(ref, rows, n_k):
    return jnp.concatenate([ref[k, rows, :] for k in range(n_k)], axis=1)


def _dot(a, b, dims=(((1,), (0,)), ((), ()))):
    return lax.dot_general(a.astype(BF16), b.astype(BF16), dims, preferred_element_type=F32)


_NT = (((1,), (1,)), ((), ()))
_TN = (((0,), (0,)), ((), ()))


def _ssm_fwd(uz, b_blk, c_blk, par, d_skip, carry=None):
    seq = uz.shape[0]
    width = d_skip.shape[1]
    ns = SSM_CH // S5_GROUP * S5_STATE
    n_k = ns // LANES
    seg = seq // SUBLANES
    tb = min(SCAN_ROWS, seq)
    n_cb = width // SSM_CH
    ex_args, ex_in_specs, ex_out_specs, ex_out_shapes, ex_scratch = _carry_specs(carry)

    def body(*refs):
        refs, start_when, finish_when = _carry(carry, refs, 5, 2)
        start_when(pl.program_id(0) == 0)
        compute(*refs)
        finish_when(pl.program_id(0) == n_cb - 1)

    def compute(u_ref, b_ref, c_ref, par_ref, d_ref, y_ref, g_ref, hre, him):
        coef_r, coef_i = par_ref[0, 2:3, :], par_ref[0, 3:4, :]
        for c0 in range(0, seq, tb):
            rows = pl.ds(c0, tb)
            ub = u_ref[rows, :]
            bur, bui = _dot(ub, b_ref[0, 0]), _dot(ub, b_ref[0, 1])
            xr, xi = coef_r * bur - coef_i * bui, coef_r * bui + coef_i * bur
            for k in range(n_k):
                hre[k, rows, :] = xr[:, k * LANES:(k + 1) * LANES]
                him[k, rows, :] = xi[:, k * LANES:(k + 1) * LANES]
        _seg_scans([(hre, him, _lane_blocks(par_ref[0, 0:1, :], n_k), _lane_blocks(par_ref[0, 1:2, :], n_k), False)], seg)
        for c0 in range(0, seq, tb):
            rows = pl.ds(c0, tb)
            y = _dot(_gather_k(hre, rows, n_k), c_ref[0, 0]) - _dot(_gather_k(him, rows, n_k), c_ref[0, 1])
            y = y + d_ref[...] * u_ref[rows, :]
            y_ref[rows, :] = y
            g_ref[rows, :] = _gelu(y).astype(BF16)

    blk = pl.BlockSpec((seq, SSM_CH), lambda i: (0, i))
    params = _params(("parallel",)) if carry is None else pltpu.CompilerParams(
        vmem_limit_bytes=VMEM_LIMIT, dimension_semantics=("arbitrary",), has_side_effects=True)
    return _pcall(
        body, name="ssm_fwd", grid=(n_cb,),
        in_specs=[blk,
                  pl.BlockSpec((1, 2, SSM_CH, ns), lambda i: (i, 0, 0, 0)),
                  pl.BlockSpec((1, 2, ns, SSM_CH), lambda i: (i, 0, 0, 0)),
                  pl.BlockSpec((1, 4, ns), lambda i: (i, 0, 0)),
                  pl.BlockSpec((1, SSM_CH), lambda i: (0, i))] + ex_in_specs,
        out_specs=[blk, blk] + ex_out_specs,
        out_shape=[jax.ShapeDtypeStruct((seq, width), F32), jax.ShapeDtypeStruct((seq, width), BF16)] + ex_out_shapes,
        scratch_shapes=[pltpu.VMEM((n_k, seq, LANES), F32), pltpu.VMEM((n_k, seq, LANES), F32)] + ex_scratch,
        compiler_params=params,
    )(uz, b_blk, c_blk, par, d_skip, *ex_args)


def _ssm_bwd(uz, dys, b_blk, ct_blk, par, d_skip, carry=None):
    seq = uz.shape[0]
    width = d_skip.shape[1]
    ns_all = SSM_CH // S5_GROUP * S5_STATE
    n_half = 2
    ns = ns_all // n_half
    n_k = ns // LANES
    seg = seq // SUBLANES
    tb = min(SCAN_ROWS, seq)
    n_cb = width // SSM_CH
    ex_args, ex_in_specs, ex_out_specs, ex_out_shapes, ex_scratch = _carry_specs(carry)

    def body(*refs):
        refs, start_when, finish_when = _carry(carry, refs, 6, 5)
        step = pl.program_id(0) * n_half + pl.program_id(1)
        start_when(step == 0)
        compute(*refs)
        finish_when(step == n_cb * n_half - 1)

    def compute(u_ref, dys_ref, b_ref, ct_ref, par_ref, d_ref,
                du_ref, dbt_ref, dct_ref, dpar_ref, dd_ref, hre, him, gre, gim):
        half = pl.program_id(1)
        a_r, a_i = par_ref[0, 0:1, :], par_ref[0, 1:2, :]
        coef_r, coef_i = par_ref[0, 2:3, :], par_ref[0, 3:4, :]

        def dys_of(rows):
            return dys_ref[rows, :]

        for c0 in range(0, seq, tb):
            rows = pl.ds(c0, tb)
            ub = u_ref[rows, :]
            bur, bui = _dot(ub, b_ref[0, 0]), _dot(ub, b_ref[0, 1])
            xr, xi = coef_r * bur - coef_i * bui, coef_r * bui + coef_i * bur
            dys = dys_of(rows)
            gr, gi = _dot(dys, ct_ref[0, 0]), -_dot(dys, ct_ref[0, 1])
            for k in range(n_k):
                lanes = slice(k * LANES, (k + 1) * LANES)
                hre[k, rows, :] = xr[:, lanes]
                him[k, rows, :] = xi[:, lanes]
                gre[k, rows, :] = gr[:, lanes]
                gim[k, rows, :] = gi[:, lanes]
        enter, _ = _seg_scans([(hre, him, _lane_blocks(a_r, n_k), _lane_blocks(a_i, n_k), False),
                               (gre, gim, _lane_blocks(a_r, n_k), _lane_blocks(-a_i, n_k), True)], seg)

        def corr(j, carry):
            acc, prev = carry
            acc_o, prev_o = [], []
            for k in range(n_k):
                sl = pl.ds(pl.multiple_of(j * SUBLANES, SUBLANES), SUBLANES)
                g_r, g_i = gre[k, sl, :], gim[k, sl, :]
                p_r, p_i = prev[2 * k], prev[2 * k + 1]
                acc_o += [acc[2 * k] + g_r * p_r + g_i * p_i, acc[2 * k + 1] + g_i * p_r - g_r * p_i]
                prev_o += [hre[k, sl, :], him[k, sl, :]]
            return tuple(acc_o), tuple(prev_o)

        zero = jnp.zeros((SUBLANES, LANES), F32)
        acc, _ = lax.fori_loop(0, seg, corr, ((zero,) * (2 * n_k), tuple(enter)), unroll=SCAN_UNROLL)
        da_r = jnp.concatenate([_colsum(acc[2 * k]) for k in range(n_k)], axis=1)
        da_i = jnp.concatenate([_colsum(acc[2 * k + 1]) for k in range(n_k)], axis=1)

        zeros_cn = jnp.zeros((SSM_CH, ns), F32)
        qt_r, qt_i, dct_r, dct_i = zeros_cn, zeros_cn, zeros_cn, zeros_cn
        dd = jnp.zeros((1, SSM_CH), F32)
        first = half == 0
        for c0 in range(0, seq, tb):
            rows = pl.ds(c0, tb)
            ub = u_ref[rows, :]
            dys = dys_of(rows)
            dct_r = dct_r + _dot(dys, _gather_k(hre, rows, n_k), _TN)
            dct_i = dct_i - _dot(dys, _gather_k(him, rows, n_k), _TN)
            g_r, g_i = _gather_k(gre, rows, n_k), _gather_k(gim, rows, n_k)
            qt_r = qt_r + _dot(ub, g_r, _TN)
            qt_i = qt_i + _dot(ub, g_i, _TN)
            dbu_r, dbu_i = coef_r * g_r + coef_i * g_i, coef_r * g_i - coef_i * g_r
            du = _dot(dbu_r, b_ref[0, 0], _NT) + _dot(dbu_i, b_ref[0, 1], _NT)
            dd = dd + _colsum(dys * ub)

            @pl.when(first)
            def _(du=du, dys=dys, rows=rows):
                du_ref[rows, :] = du + d_ref[...] * dys

            @pl.when(jnp.logical_not(first))
            def _(du=du, rows=rows):
                du_ref[rows, :] += du

        @pl.when(first)
        def _():
            dd_ref[...] = dd

        b_r, b_i = b_ref[0, 0], b_ref[0, 1]
        dbt_ref[0, 0] = coef_r * qt_r + coef_i * qt_i
        dbt_ref[0, 1] = coef_r * qt_i - coef_i * qt_r
        dct_ref[0, 0] = dct_r
        dct_ref[0, 1] = dct_i
        dpar_ref[0, 0:1, :] = da_r
        dpar_ref[0, 1:2, :] = da_i
        dpar_ref[0, 2:3, :] = _colsum(b_r * qt_r + b_i * qt_i)
        dpar_ref[0, 3:4, :] = _colsum(b_r * qt_i - b_i * qt_r)

    blk = lambda i, h: (0, i)
    params = _params(("parallel", "arbitrary")) if carry is None else pltpu.CompilerParams(
        vmem_limit_bytes=VMEM_LIMIT, dimension_semantics=("arbitrary", "arbitrary"), has_side_effects=True)
    return _pcall(
        body, name="ssm_bwd", grid=(n_cb, n_half),
        in_specs=[pl.BlockSpec((seq, SSM_CH), blk), pl.BlockSpec((seq, SSM_CH), blk),
                  pl.BlockSpec((1, 2, SSM_CH, ns), lambda i, h: (i, 0, 0, h)),
                  pl.BlockSpec((1, 2, SSM_CH, ns), lambda i, h: (i, 0, 0, h)),
                  pl.BlockSpec((1, 4, ns), lambda i, h: (i, 0, h)),
                  pl.BlockSpec((1, SSM_CH), blk)] + ex_in_specs,
        out_specs=[pl.BlockSpec((seq, SSM_CH), blk),
                   pl.BlockSpec((1, 2, SSM_CH, ns), lambda i, h: (i, 0, 0, h)),
                   pl.BlockSpec((1, 2, SSM_CH, ns), lambda i, h: (i, 0, 0, h)),
                   pl.BlockSpec((1, 4, ns), lambda i, h: (i, 0, h)),
                   pl.BlockSpec((1, SSM_CH), blk)] + ex_out_specs,
        out_shape=[jax.ShapeDtypeStruct((seq, width), F32),
                   jax.ShapeDtypeStruct((n_cb, 2, SSM_CH, ns_all), F32),
                   jax.ShapeDtypeStruct((n_cb, 2, SSM_CH, ns_all), F32),
                   jax.ShapeDtypeStruct((n_cb, 4, ns_all), F32),
                   jax.ShapeDtypeStruct((1, width), F32)] + ex_out_shapes,
        scratch_shapes=[pltpu.VMEM((n_k, seq, LANES), F32) for _ in range(4)] + ex_scratch,
        compiler_params=params,
    )(uz, dys, b_blk, ct_blk, par, d_skip, *ex_args)


def _ssm_discretize(a_re, a_im, log_dt):
    dt = jnp.exp(log_dt)[:, None]
    mag = jnp.exp(a_re * dt)
    abar_re = mag * jnp.cos(a_im * dt)
    abar_im = mag * jnp.sin(a_im * dt)
    den = a_re * a_re + a_im * a_im
    nr = abar_re - 1.0
    coef_re = (nr * a_re + abar_im * a_im) / den
    coef_im = (abar_im * a_re - nr * a_im) / den
    return abar_re, abar_im, coef_re, coef_im


def _block_diag(w_gcp):
    gpb = SSM_CH // S5_GROUP
    n_cb = w_gcp.shape[0] // gpb
    w = w_gcp.reshape(n_cb, gpb, S5_GROUP, 1, S5_STATE)
    eye = jnp.eye(gpb, dtype=w.dtype)[None, :, None, :, None]
    return (w * eye).reshape(n_cb, SSM_CH, gpb * S5_STATE)


def _block_diag_extract(w_blk):
    gpb = SSM_CH // S5_GROUP
    n_cb = w_blk.shape[0]
    w = w_blk.reshape(n_cb, gpb, S5_GROUP, gpb, S5_STATE)
    w = jnp.moveaxis(jnp.diagonal(w, axis1=1, axis2=3), -1, 1)
    return w.reshape(n_cb * gpb, S5_GROUP, S5_STATE)


def _split3(x):
    hi = x.astype(BF16)
    mid = (x - hi.astype(F32)).astype(BF16)
    lo = (x - hi.astype(F32) - mid.astype(F32)).astype(BF16)
    return hi, mid, lo


def _tri_sum(tri, x):
    hi, mid, lo = _split3(x)
    return (jnp.dot(tri, hi, preferred_element_type=F32) + jnp.dot(tri, mid, preferred_element_type=F32)
            + jnp.dot(tri, lo, preferred_element_type=F32))


def _log_sigmoid(x):
    return jnp.minimum(x, 0.0) - jnp.log(1.0 + jnp.exp(-jnp.abs(x)))


def _cum_fwd(fl, b_f):
    seq = fl.shape[0]
    t = min(CUM_TILE, seq)

    def body(fl_ref, b_ref, o_ref, carry):
        @pl.when(pl.program_id(0) == 0)
        def _():
            carry[...] = jnp.zeros_like(carry)

        r = lax.broadcasted_iota(jnp.int32, (t, t), 0)
        c = lax.broadcasted_iota(jnp.int32, (t, t), 1)
        tri = (c <= r).astype(BF16)
        cum = _tri_sum(tri, _log_sigmoid(fl_ref[...] + b_ref[...])) + carry[...]
        o_ref[...] = cum
        carry[...] = cum[t - 1:t, :]

    return _pcall(
        body, name="cum_fwd", grid=(seq // t,),
        in_specs=[pl.BlockSpec((t, LANES), lambda i: (i, 0)), pl.BlockSpec((1, LANES), lambda i: (0, 0))],
        out_specs=pl.BlockSpec((t, LANES), lambda i: (i, 0)),
        out_shape=jax.ShapeDtypeStruct((seq, LANES), F32),
        scratch_shapes=[pltpu.VMEM((1, LANES), F32)],
        compiler_params=_params(("arbitrary",)),
    )(fl, b_f)


def _cum_bwd(dcum, fl, b_f):
    seq = fl.shape[0]
    t = min(CUM_TILE, seq)
    nb = seq // t

    def body(dc_ref, fl_ref, b_ref, o_ref, db_ref, carry):
        @pl.when(pl.program_id(0) == 0)
        def _():
            carry[...] = jnp.zeros_like(carry)
            db_ref[...] = jnp.zeros_like(db_ref)

        r = lax.broadcasted_iota(jnp.int32, (t, t), 0)
        c = lax.broadcasted_iota(jnp.int32, (t, t), 1)
        tri = (c >= r).astype(BF16)
        rev = _tri_sum(tri, dc_ref[...]) + carry[...]
        carry[...] = rev[0:1, :]
        dfl = rev * _sigmoid(-(fl_ref[...] + b_ref[...]))
        o_ref[...] = dfl
        db_ref[...] += _colsum(dfl)

    return _pcall(
        body, name="cum_bwd", grid=(nb,),
        in_specs=[pl.BlockSpec((t, LANES), lambda i: (nb - 1 - i, 0)), pl.BlockSpec((t, LANES), lambda i: (nb - 1 - i, 0)),
                  pl.BlockSpec((1, LANES), lambda i: (0, 0))],
        out_specs=[pl.BlockSpec((t, LANES), lambda i: (nb - 1 - i, 0)), pl.BlockSpec((1, LANES), lambda i: (0, 0))],
        out_shape=[jax.ShapeDtypeStruct((seq, LANES), F32), jax.ShapeDtypeStruct((1, LANES), F32)],
        scratch_shapes=[pltpu.VMEM((1, LANES), F32)],
        compiler_params=_params(("arbitrary",)),
    )(dcum, fl, b_f)


def _att_scores(q, kb, ck, row0, col0, masked):
    s = _dot(q, kb, _NT) - ck
    if masked:
        rows = row0 + lax.broadcasted_iota(jnp.int32, s.shape, 0)
        cols = col0 + lax.broadcasted_iota(jnp.int32, s.shape, 1)
        s = jnp.where(cols <= rows, s, NEG_INF)
    return s


def _pairs(lo, hi, step_fn, init):
    return lax.fori_loop(0, (hi - lo) // 2, lambda p, c: step_fn(lo + 2 * p + 1, step_fn(lo + 2 * p, c)), init)


def _att_fwd(qz, kv, ck):
    seq = qz.shape[0]
    heads = ck.shape[0]
    t = min(ATT_TILE, seq)
    scale = HEAD_DIM ** -0.5

    def body(q_ref, z_ref, k_ref, v_ref, ck_ref, o_ref, og_ref, lse_ref):
        i = pl.program_id(1)
        q = (q_ref[...] * scale).astype(BF16)

        def block(j, carry, masked):
            m, l, acc = carry
            rows = pl.ds(pl.multiple_of(j * t, t), t)
            s = _att_scores(q, k_ref[rows, :], ck_ref[0, j], i * t, j * t, masked)
            m_new = jnp.maximum(m, jnp.max(s, axis=1, keepdims=True))
            p = jnp.exp(s - m_new)
            alpha = jnp.exp(m - m_new)
            p_hi = p.astype(BF16)
            p_lo = (p - p_hi.astype(F32)).astype(BF16)
            vb = v_ref[rows, :]
            return (m_new, alpha * l + jnp.sum(p, axis=1, keepdims=True),
                    alpha * acc + (_dot(p_hi, vb) + _dot(p_lo, vb)))

        init = (jnp.full((t, 1), NEG_INF, F32), jnp.zeros((t, 1), F32), jnp.zeros((t, HEAD_DIM), F32))
        carry = _pairs(0, i - i % 2, functools.partial(block, masked=False), init)
        m, l, acc = lax.cond(i % 2 == 1, lambda c: block(i, block(i - 1, c, False), True),
                             lambda c: block(i, c, True), carry)
        o = acc / l
        o_ref[...] = o
        og_ref[...] = (o * _silu(z_ref[...])).astype(BF16)
        lse_ref[0] = m + jnp.log(l)

    qblk = pl.BlockSpec((t, HEAD_DIM), lambda h, i: (i, h))
    return _pcall(
        body, name="att_fwd", grid=(heads, seq // t),
        in_specs=[qblk, pl.BlockSpec((t, HEAD_DIM), lambda h, i: (i, heads + h)),
                  pl.BlockSpec((seq, HEAD_DIM), lambda h, i: (0, h)),
                  pl.BlockSpec((seq, HEAD_DIM), lambda h, i: (0, heads + h)),
                  pl.BlockSpec((1, seq // t, 1, t), lambda h, i: (h, 0, 0, 0))],
        out_specs=[qblk, qblk, pl.BlockSpec((1, t, 1), lambda h, i: (h, i, 0))],
        out_shape=[jax.ShapeDtypeStruct((seq, heads * HEAD_DIM), F32), jax.ShapeDtypeStruct((seq, heads * HEAD_DIM), BF16),
                   jax.ShapeDtypeStruct((heads, seq, 1), F32)],
        compiler_params=_params(("parallel", "parallel")),
    )(qz, qz, kv, kv, ck)


def _att_bwd(qz, kv, do, ox, lse, ck):
    seq = qz.shape[0]
    heads = ck.shape[0]
    t = min(ATT_TILE, seq)
    nq = seq // t
    scale = HEAD_DIM ** -0.5

    def body(q_ref, k_ref, v_ref, do_ref, ox_ref, lse_ref, ck_ref, dq_ref, dk_ref, dv_ref, dck_ref, dq_acc, delta):
        j = pl.program_id(1)

        @pl.when(j == 0)
        def _():
            for c0 in range(0, seq, t):
                rows = pl.ds(c0, t)
                delta[rows, :] = jnp.sum(do_ref[rows, :].astype(BF16).astype(F32) * ox_ref[rows, :], axis=1, keepdims=True)
                dq_acc[rows, :] = jnp.zeros((t, HEAD_DIM), F32)

        kb, vb = k_ref[...], v_ref[...]
        ckv = ck_ref[0, 0]

        def block(i, carry, masked):
            dk, dv, dck = carry
            rows = pl.ds(pl.multiple_of(i * t, t), t)
            qb = (q_ref[rows, :] * scale).astype(BF16)
            dob = do_ref[rows, :].astype(BF16)
            s = _att_scores(qb, kb, ckv, i * t, j * t, masked)
            p = jnp.exp(s - lse_ref[0, rows, :])
            ds = p * (_dot(dob, vb, _NT) - delta[rows, :])
            dq_acc[rows, :] += _dot(ds, kb)
            return dk + _dot(ds, qb, _TN), dv + _dot(p, dob, _TN), dck - _colsum(ds)

        init = (jnp.zeros((t, HEAD_DIM), F32), jnp.zeros((t, HEAD_DIM), F32), jnp.zeros((1, t), F32))
        odd = (nq - 1 - j) % 2
        carry = lax.cond(odd == 1, lambda c: block(j + 1, block(j, c, True), False), lambda c: block(j, c, True), init)
        dk, dv, dck = _pairs(j + 1 + odd, nq, functools.partial(block, masked=False), carry)
        dk_ref[...] = dk.astype(dk_ref.dtype)
        dv_ref[...] = dv.astype(dv_ref.dtype)
        dck_ref[0, 0] = dck

        @pl.when(j == nq - 1)
        def _():
            for c0 in range(0, seq, t):
                rows = pl.ds(c0, t)
                dq_ref[rows, :] = (dq_acc[rows, :] * scale).astype(dq_ref.dtype)

    head = pl.BlockSpec((seq, HEAD_DIM), lambda h, j: (0, h))
    col = pl.BlockSpec((1, seq, 1), lambda h, j: (h, 0, 0))
    kblk = pl.BlockSpec((t, HEAD_DIM), lambda h, j: (j, h))
    row = pl.BlockSpec((1, 1, 1, t), lambda h, j: (h, j, 0, 0))
    return _pcall(
        body, name="att_bwd", grid=(heads, nq),
        in_specs=[head, kblk, pl.BlockSpec((t, HEAD_DIM), lambda h, j: (j, heads + h)), head, head, col, row],
        out_specs=[head, kblk, kblk, row],
        out_shape=[jax.ShapeDtypeStruct((seq, heads * HEAD_DIM), BF16)] * 3 + [jax.ShapeDtypeStruct((heads, nq, 1, t), F32)],
        scratch_shapes=[pltpu.VMEM((seq, HEAD_DIM), F32), pltpu.VMEM((seq, 1), F32)],
        compiler_params=_params(("parallel", "arbitrary")),
    )(qz, kv, kv, do, ox, lse, ck)


def _mesh_pos():
    return lax.axis_index("x"), lax.axis_index("y"), lax.axis_index("c")


def _other_chips(x, y):
    return [(1 - x, y), (x, 1 - y), (1 - x, 1 - y)]


def _all_gather_weights(big, small):
    nb, ns = len(big), len(small)
    n_remote = 3 * (nb + ns)

    def plan(ins, outs, sems):
        send_sems, recv_sems, fwd_send, fwd_recv = sems
        x, y, c = _mesh_pos()
        chips = _other_chips(x, y)
        slots = [2 * cx + cy for cx, cy in chips]

        def half(ref, hc):
            rh = ref.shape[-2] // 2
            return ref.at[pl.ds(hc * rh, rh), :]

        def remote(i, j, src_chip, from_in):
            if i < nb:
                src = half(ins[i], c) if from_in else half(outs[i].at[src_chip], c)
                dst = half(outs[i].at[src_chip], c)
            else:
                src = ins[i] if from_in else outs[i].at[src_chip]
                dst = outs[i].at[src_chip]
            k = 3 * i + j
            return pltpu.make_async_remote_copy(src_ref=src, dst_ref=dst, send_sem=send_sems.at[k],
                                                recv_sem=recv_sems.at[k], device_id=(*chips[j], c),
                                                device_id_type=MESH_ID)

        def forward(i, j, hc):
            part = half(outs[i].at[slots[j]], hc)
            k = 3 * i + j
            return pltpu.make_async_remote_copy(src_ref=part, dst_ref=part, send_sem=fwd_send.at[k],
                                                recv_sem=fwd_recv.at[k], device_id=(x, y, 1 - c),
                                                device_id_type=MESH_ID)

        return remote, forward, 2 * x + y, slots, c

    def start(ins, outs, sems):
        remote, _, me, _, _ = plan(ins, outs, sems)
        for i in range(nb + ns):
            for j in range(3):
                remote(i, j, me, True).start()

    def finish(ins, outs, sems):
        remote, forward, me, slots, c = plan(ins, outs, sems)
        for i in range(nb + ns):
            for j in range(3):
                remote(i, j, slots[j], False).wait_recv()
                if i < nb:
                    forward(i, j, c).start()
        for i in range(nb):
            for j in range(3):
                forward(i, j, 1 - c).wait_recv()
        for i in range(nb + ns):
            for j in range(3):
                remote(i, j, me, True).wait_send()
                if i < nb:
                    forward(i, j, c).wait_send()

    arrays = tuple(big) + tuple(small)
    return _Exchange(
        arrays=arrays,
        out_shapes=tuple(jax.ShapeDtypeStruct((N_CHIPS,) + a.shape, a.dtype) for a in arrays),
        scratch=(pltpu.SemaphoreType.DMA((n_remote,)), pltpu.SemaphoreType.DMA((n_remote,)),
                 pltpu.SemaphoreType.DMA((3 * max(nb, 1),)), pltpu.SemaphoreType.DMA((3 * max(nb, 1),))),
        start=start, finish=finish)


def _swap_halves(grads):
    n = len(grads)

    def copies(ins, outs, sems):
        x, y, c = _mesh_pos()
        cps = []
        for i in range(n):
            rh = ins[i].shape[1] // 2
            cps.append(pltpu.make_async_remote_copy(
                src_ref=ins[i].at[:, pl.ds((1 - c) * rh, rh), :], dst_ref=outs[i], send_sem=sems[0].at[i],
                recv_sem=sems[1].at[i], device_id=(x, y, 1 - c), device_id_type=MESH_ID))
        return cps

    def start(ins, outs, sems):
        for cp in copies(ins, outs, sems):
            cp.start()

    def finish(ins, outs, sems):
        for cp in copies(ins, outs, sems):
            cp.wait()

    return _Exchange(
        arrays=tuple(grads),
        out_shapes=tuple(jax.ShapeDtypeStruct((g.shape[0], g.shape[1] // 2, g.shape[2]), g.dtype) for g in grads),
        scratch=(pltpu.SemaphoreType.DMA((n,)), pltpu.SemaphoreType.DMA((n,))),
        start=start, finish=finish)


def _pair_sum_bf16(g, theirs, pos, name):
    n, rh, cdim = theirs.shape
    tr = min(ROW_TILE, rh)
    nb = rh // tr

    def body(pos_ref, g_ref, t_ref, o_ref):
        o_ref[...] = (g_ref[...] + t_ref[...]).astype(BF16)

    slot = lambda s, pos: (pos[0] + 1 + s) % n
    grid_spec = pltpu.PrefetchScalarGridSpec(
        num_scalar_prefetch=1, grid=(n - 1, nb),
        in_specs=[pl.BlockSpec((None, tr, cdim), lambda s, i, pos: (slot(s, pos), pos[1] * nb + i, 0)),
                  pl.BlockSpec((None, tr, cdim), lambda s, i, pos: (slot(s, pos), i, 0))],
        out_specs=pl.BlockSpec((None, tr, cdim), lambda s, i, pos: (slot(s, pos), i, 0)))
    return _pcall(body, name=name, grid_spec=grid_spec, out_shape=jax.ShapeDtypeStruct(theirs.shape, BF16),
                  compiler_params=_params(("parallel", "parallel")))(pos, g, theirs)


def _chip_sum(g, theirs, recv, pos, name):
    n, rh, cdim = theirs.shape
    tr = min(ROW_TILE, rh)
    nb = rh // tr

    def body(pos_ref, g_ref, t_ref, r0, r1, r2, o_ref):
        o_ref[...] = (((g_ref[...] + t_ref[...]) + r0[...]) + r1[...]) + r2[...]

    grid_spec = pltpu.PrefetchScalarGridSpec(
        num_scalar_prefetch=1, grid=(nb,),
        in_specs=[pl.BlockSpec((None, tr, cdim), lambda i, pos: (pos[0], pos[1] * nb + i, 0)),
                  pl.BlockSpec((None, tr, cdim), lambda i, pos: (pos[0], i, 0))]
        + [pl.BlockSpec((None, tr, cdim), functools.partial(lambda i, pos, j: (j, i, 0), j=j)) for j in range(3)],
        out_specs=pl.BlockSpec((tr, cdim), lambda i, pos: (pos[1] * nb + i, 0)))
    return _pcall(body, name=name, grid_spec=grid_spec, out_shape=jax.ShapeDtypeStruct((2 * rh, cdim), F32),
                  compiler_params=_params(("parallel",)))(pos, g, theirs, recv, recv, recv)


def _scatter_to_owner(parts):
    n = len(parts)

    def copies(ins, outs, sems):
        x, y, c = _mesh_pos()
        chips = _other_chips(x, y)
        cps = []
        for i in range(n):
            for j in range(3):
                k = 3 * i + j
                cps.append(pltpu.make_async_remote_copy(
                    src_ref=ins[i].at[2 * chips[j][0] + chips[j][1]], dst_ref=outs[i].at[j],
                    send_sem=sems[0].at[k], recv_sem=sems[1].at[k], device_id=(*chips[j], c),
                    device_id_type=MESH_ID))
        return cps

    def start(ins, outs, sems):
        for cp in copies(ins, outs, sems):
            cp.start()

    def finish(ins, outs, sems):
        for cp in copies(ins, outs, sems):
            cp.wait()

    return _Exchange(
        arrays=tuple(parts),
        out_shapes=tuple(jax.ShapeDtypeStruct((3,) + p.shape[1:], p.dtype) for p in parts),
        scratch=(pltpu.SemaphoreType.DMA((3 * n,)), pltpu.SemaphoreType.DMA((3 * n,))),
        start=start, finish=finish)


def _finish_reductions(v, shards):
    n = len(shards)
    n_rows = v.shape[0]
    rh = n_rows // 2
    assert rh % SUBLANES == 0

    def body(*refs):
        v_ref, o_ref, outs = refs[0], refs[1 + n], refs[2 + n:2 + 2 * n]
        part, recv, send_sems, recv_sems, join_send, join_recv = refs[2 + 2 * n:]
        x, y, c = _mesh_pos()
        sibling = (x, y, 1 - c)

        def join(i, hc):
            half = outs[i].shape[0] // 2
            rows = outs[i].at[pl.ds(hc * half, half), :]
            return pltpu.make_async_remote_copy(src_ref=rows, dst_ref=rows, send_sem=join_send.at[i],
                                                recv_sem=join_recv.at[i], device_id=sibling, device_id_type=MESH_ID)

        for i in range(n):
            join(i, c).start()

        mine = pl.ds(pl.multiple_of(c * rh, SUBLANES), rh)
        theirs = pl.ds(pl.multiple_of((1 - c) * rh, SUBLANES), rh)

        def exchange(s, src, dst, peer):
            cp = pltpu.make_async_remote_copy(src_ref=src, dst_ref=dst, send_sem=send_sems.at[s],
                                              recv_sem=recv_sems.at[s], device_id=peer, device_id_type=MESH_ID)
            cp.start()
            cp.wait()

        exchange(0, v_ref.at[theirs, :], recv.at[0], sibling)
        part[...] = v_ref[mine, :] + recv[0]
        for s, peer in ((1, (1 - x, y, c)), (2, (x, 1 - y, c))):
            exchange(s, part, recv.at[s], peer)
            part[...] = part[...] + recv[s]
        o_ref[mine, :] = part[...]
        exchange(3, part, o_ref.at[mine, :], sibling)

        for i in range(n):
            join(i, c).wait_send()
            join(i, 1 - c).wait_recv()

    vm = pl.BlockSpec(memory_space=pltpu.VMEM)
    res = _pcall(
        body, name="finish_reductions", in_specs=[vm] + [ANY] * n, out_specs=[vm] + [ANY] * n,
        out_shape=[jax.ShapeDtypeStruct(v.shape, v.dtype)] + [jax.ShapeDtypeStruct(s.shape, s.dtype) for s in shards],
        input_output_aliases={1 + i: 1 + i for i in range(n)},
        scratch_shapes=[pltpu.VMEM((rh, LANES), v.dtype), pltpu.VMEM((3, rh, LANES), v.dtype),
                        pltpu.SemaphoreType.DMA((4,)), pltpu.SemaphoreType.DMA((4,)),
                        pltpu.SemaphoreType.DMA((n,)), pltpu.SemaphoreType.DMA((n,))],
        compiler_params=pltpu.CompilerParams(vmem_limit_bytes=VMEM_LIMIT, has_side_effects=True),
    )(v, *shards)
    return res[0], res[1:]


def _adamw_math(w, g, m, v):
    m = ADAM_B1 * m + (1.0 - ADAM_B1) * g
    v = ADAM_B2 * v + (1.0 - ADAM_B2) * (g * g)
    m_hat = m / (1.0 - ADAM_B1 ** ADAM_STEP)
    v_hat = v / (1.0 - ADAM_B2 ** ADAM_STEP)
    delta = -ADAM_LR * (m_hat / (jnp.sqrt(v_hat) + ADAM_EPS) + ADAM_WD * w)
    return delta, m, v


def _adamw(w, g, m, v, name):
    wd = w.shape[1]
    return _rows(_adamw_math, [_full(w), _full(g), _full(m), _full(v)], [(wd, F32)] * 3, name=name)


def _rows_of(a):
    return -(-a.size // (LANES * SUBLANES)) * SUBLANES


def _pack(arrs, fill=0.0):
    parts = []
    for a in arrs:
        flat = a.reshape(-1)
        flat = jnp.pad(flat, (0, _rows_of(a) * LANES - a.size), constant_values=fill)
        parts.append(flat.reshape(-1, LANES))
    used = sum(p.shape[0] for p in parts)
    rows = -(-used // ROW_TILE) * ROW_TILE
    parts.append(jnp.full((rows - used, LANES), fill, F32))
    return jnp.concatenate(parts, axis=0)


def _unpack(buf, like):
    out, off = [], 0
    for a in like:
        out.append(buf[off:off + _rows_of(a)].reshape(-1)[:a.size].reshape(a.shape))
        off += _rows_of(a)
    return out


def kernel(x, norm_pre, norm_post, s5_w_in, s5_a_re, s5_a_im, s5_log_dt, s5_b_re, s5_b_im, s5_c_re, s5_c_im, s5_d, s5_w_glu, s5_b_glu, s5_w_out, kv_norm, kv_w, kv_b_f, fox_w_in, fox_w_out, loss_target, m_norm_pre, m_norm_post, m_s5_w_in, m_s5_a_re, m_s5_a_im, m_s5_log_dt, m_s5_b_re, m_s5_b_im, m_s5_c_re, m_s5_c_im, m_s5_d, m_s5_w_glu, m_s5_b_glu, m_s5_w_out, m_kv_norm, m_kv_w, m_kv_b_f, m_fox_w_in, m_fox_w_out, v_norm_pre, v_norm_post, v_s5_w_in, v_s5_a_re, v_s5_a_im, v_s5_log_dt, v_s5_b_re, v_s5_b_im, v_s5_c_re, v_s5_c_im, v_s5_d, v_s5_w_glu, v_s5_b_glu, v_s5_w_out, v_kv_norm, v_kv_w, v_kv_b_f, v_fox_w_in, v_fox_w_out):
    seq, dm = x.shape[1], x.shape[2]
    width = dm
    heads = dm // HEAD_DIM
    fw = heads * HEAD_DIM
    groups = width // S5_GROUP
    chip = 2 * lax.axis_index("x") + lax.axis_index("y")

    big_shards = [s5_w_in[0], s5_w_glu[0], s5_w_out[0], kv_w, fox_w_in[0], fox_w_out[0]]
    own_shards = [w.astype(BF16) for w in big_shards] + [s5_d, s5_b_glu]
    fill_own = lambda gs, owns: [lax.dynamic_update_slice(g, own[None], (chip, 0, 0)) for g, own in zip(gs, owns)]
    c_idx = lax.axis_index("c")
    pos = jnp.stack([chip, c_idx]).astype(jnp.int32)
    h0 = x[0]
    target = loss_target[0]
    g_pre0, g_pre1 = norm_pre[0:1], norm_pre[1:2]
    g_post0, g_post1 = norm_post[0:1], norm_post[1:2]
    g_kv = kv_norm.reshape(1, dm)
    first_owns = [own_shards[0], s5_d, s5_b_glu]
    xn1, *first_gathered = _rows(lambda h, g: (h * _rstd(h) * g,), [_full(h0)], [(dm, BF16)], consts=[g_pre0], name="norm_pre0",
                                 carry=_all_gather_weights(first_owns[:1], first_owns[1:]))
    g_win, g_d, g_bglu = fill_own(first_gathered, first_owns)
    gather_rest = _all_gather_weights(own_shards[1:4], [])
    gather_fox_in = _all_gather_weights(own_shards[4:5], [])
    gather_fox_out = _all_gather_weights(own_shards[5:6], [])
    cols = lambda g: jnp.moveaxis(g, 0, 1).reshape(g.shape[1], -1)
    rows = lambda g: g.reshape(-1, g.shape[2])
    w_in = g_win
    d_skip, b_glu = cols(g_d), cols(g_bglu)
    b_f = jnp.pad(kv_b_f, (0, LANES - heads)).reshape(1, LANES)

    a_re, a_im, log_dt = s5_a_re[0], s5_a_im[0], s5_log_dt[0]
    disc, disc_vjp = jax.vjp(_ssm_discretize, a_re, a_im, log_dt)
    gpb = SSM_CH // S5_GROUP
    n_cb = groups // gpb
    par = jnp.stack([p.reshape(n_cb, gpb * S5_STATE) for p in disc], axis=1)
    b_t = lambda b: jnp.swapaxes(b, 1, 2)
    b_blk = jnp.stack([_block_diag(b_t(s5_b_re[0])), _block_diag(b_t(s5_b_im[0]))], axis=1)
    ct_blk = jnp.stack([_block_diag(s5_c_re[0]), _block_diag(s5_c_im[0])], axis=1)
    c_blk = jnp.swapaxes(ct_blk, 2, 3)

    xn1 = _to_slab(xn1)
    uz = _mm(xn1, w_in, name="s5_in")
    ys, y1b, *rest = _ssm_fwd(uz, b_blk, c_blk, par, d_skip, carry=gather_rest)
    g_wglu, g_wout, g_kvw = fill_own(rest, own_shards[1:4])
    w_glu, w_out = rows(g_wglu), rows(g_wout)
    kvw_full = cols(g_kvw)
    w_kv = kvw_full[:, :2 * fw]
    w_f = jnp.pad(kvw_full[:, 2 * fw:], ((0, 0), (0, LANES - heads)))

    def gate_fn(a, y, z, b):
        return a, _gelu(y) * _sigmoid(a + b) * _silu(z)

    glu_a, y3b = _mm(y1b, w_glu, name="s5_glu", post=(gate_fn, [ys, (uz, width), b_glu], (F32, BF16)))
    o1 = _from_slab(_mm(y3b, w_out, name="s5_out"))

    def mid_fn(h, o, gp, gk, gq):
        h1 = h + o * _rstd(o) * gp
        r = _rstd(h1)
        return h1, h1 * r * gk, h1 * r * gq

    h1, xk, xn2 = _rows(mid_fn, [_full(h0), _full(o1)], [(dm, F32), (dm, BF16), (dm, BF16)],
                        consts=[g_post0, g_kv, g_pre1], name="mid_norms")

    kv, g_fwin = _mm(xk, w_kv, out_dtype=BF16, name="kv_proj", carry=gather_fox_in)
    fw_in = fill_own([g_fwin], own_shards[4:5])[0]
    fl = _mm(xk, w_f, name="f_proj")
    qz, g_fwout = _mm(xn2, fw_in, name="fox_in", carry=gather_fox_out)
    fw_out = rows(fill_own([g_fwout], own_shards[5:6])[0])
    cum = _cum_fwd(fl, b_f)
    t_att = min(ATT_TILE, seq)
    cum_t = cum[:, :heads].T
    ck = cum_t.reshape(heads, seq // t_att, 1, t_att)
    o, o2b, lse = _att_fwd(qz, kv, ck)
    o3 = _mm(o2b, fw_out, name="fox_out")

    def loss_fn(h, o, t, g):
        r = _rstd(o)
        err = h + o * r * g - t
        dh = err * (1.0 / dm)
        do, dg = _rms_bwd(o, g, dh)
        part = 0.5 * jnp.sum(jnp.mean(err * err, axis=-1, keepdims=True), axis=0, keepdims=True)
        return dh, do, jnp.broadcast_to(part, (1, LANES)), _colsum(dg)

    dh2, do3, loss_part, dg_post1 = _rows(loss_fn, [_full(h1), _full(o3), _full(target)], [(dm, F32), (dm, BF16)],
                                          consts=[g_post1], accs=[(1, LANES), (1, dm)], name="loss_head")
    loss = lax.psum(loss_part[0, 0], MESH_AXES)

    def fox_gate_bwd(d, a, z):
        return d * _silu(z), d * a * _dsilu(z)

    do, dz2 = _mm(do3, fw_out, tb=True, name="fox_out_dx", post=(fox_gate_bwd, [o, (qz, fw)], (F32, BF16)))
    dw_fout = _mm(o2b, do3, ta=True, name="fox_out_dw")
    dq, dk, dv, dck = _att_bwd(qz, kv, do, o, lse, ck)
    dcum = jnp.pad(dck.reshape(heads, seq).T, ((0, 0), (0, LANES - heads)))
    dfl, db_f = _cum_bwd(dcum, fl, b_f)
    dqz = (dq, dz2)
    dkv = (dk, dv)
    dxn2 = _mm(dqz, fw_in, tb=True, name="fox_in_dx")
    dw_fin = _mm(xn2, dqz, ta=True, out_split=N_CHIPS, name="fox_in_dw")
    dxk_f = _mm(dfl, w_f, tb=True, name="f_proj_dx")
    dxk = _mm(dkv, w_kv, tb=True, add=dxk_f, name="kv_proj_dx")
    dw_kv = _mm(xk, dkv, ta=True, name="kv_proj_dw")
    dw_f = _mm(xk, dfl, ta=True, name="f_proj_dw")

    def mid_bwd(d2, h, dq_, dk_, o, gq, gk, gp):
        dxa, dga = _rms_bwd(h, gq, dq_)
        dxb, dgb = _rms_bwd(h, gk, dk_)
        dh = d2 + dxa + dxb
        do_, dgp = _rms_bwd(o, gp, dh)
        return dh, do_, _colsum(dga), _colsum(dgb), _colsum(dgp)

    to_rows = lambda g: g.reshape(N_CHIPS, -1, g.shape[1])
    kv_cols = kv_w.shape[1]
    kv_col_block = lambda b: (dw_kv[:, b * kv_cols:(b + 1) * kv_cols] if (b + 1) * kv_cols <= 2 * fw else
                              jnp.concatenate([dw_kv[:, b * kv_cols:], dw_f[:, :heads]], axis=1))
    early_grads = [jnp.stack([kv_col_block(b) for b in range(N_CHIPS)]), dw_fin, to_rows(dw_fout)]
    dh1, do1, dg_pre1, dg_kv, dg_post0, *early_theirs = _rows(
        mid_bwd, [_full(dh2), _full(h1), _full(dxn2), _full(dxk), _full(o1)], [(dm, F32), (dm, BF16)],
        consts=[g_pre1, g_kv, g_post0], accs=[(1, dm)] * 3, name="mid_norms_bwd", carry=_swap_halves(early_grads))
    early_sums = [_pair_sum_bf16(g, t, pos, f"grad_pair_sum_{3 + i}") for i, (g, t) in enumerate(zip(early_grads, early_theirs))]

    do1 = _to_slab(do1)
    dy3 = _mm(do1, w_out, tb=True, name="s5_out_dx")
    dw_out = _mm(y3b, do1, ta=True, name="s5_out_dw")

    def gate_bwd(d3, y, a, z, b):
        y1 = _gelu(y)
        gate = _sigmoid(a + b)
        dy2 = d3 * _silu(z)
        da = dy2 * y1 * gate * (1.0 - gate)
        return dy2 * gate, da, d3 * (y1 * gate) * _dsilu(z), _colsum(da)

    dy1_direct, da, dz, db_glu = _rows(gate_bwd, [_full(dy3), _full(ys), _full(glu_a), (uz, width, 1)],
                                       [(width, F32), (width, BF16), (width, BF16)], consts=[b_glu],
                                       accs=[(1, width)], name="s5_gate_bwd")
    dys = _mm(da, w_glu, tb=True, add=dy1_direct, post=(lambda d, y: d * _dgelu(y), [ys], None), name="s5_glu_dx")
    dw_glu = _mm(y1b, da, ta=True, name="s5_glu_dw")
    mid_grads = [to_rows(dw_glu), to_rows(dw_out)]
    du, dbt_blk, dct_blk, dpar, dd, *carried = _ssm_bwd(
        uz, dys, b_blk, ct_blk, par, d_skip, carry=_together(_scatter_to_owner(early_sums), _swap_halves(mid_grads)))
    early_recv, mid_theirs = carried[:3], carried[3:]
    mid_sums = [_pair_sum_bf16(g, t, pos, f"grad_pair_sum_{1 + i}") for i, (g, t) in enumerate(zip(mid_grads, mid_theirs))]
    duz = (du, dz)
    dw_in, *mid_recv = _mm(xn1, duz, ta=True, out_split=N_CHIPS, name="s5_in_dw", carry=_scatter_to_owner(mid_sums))
    late_grads = [dw_in]
    dxn1, *late_theirs = _mm(duz, w_in, tb=True, name="s5_in_dx", carry=_swap_halves(late_grads))
    dxn1 = _from_slab(dxn1)
    late_sums = [_pair_sum_bf16(dw_in, late_theirs[0], pos, "grad_pair_sum_0")]

    def first_bwd(d1, h, dxn, g):
        dx, dg = _rms_bwd(h, g, dxn)
        return d1 + dx, _colsum(dg)

    grad_x, dg_pre0, *late_recv = _rows(first_bwd, [_full(dh1), _full(h0), _full(dxn1)], [(dm, F32)], consts=[g_pre0],
                                        accs=[(1, dm)], name="norm_pre0_bwd", carry=_scatter_to_owner(late_sums))

    dpar_g = [dpar[:, i, :].reshape(groups, S5_STATE) for i in range(4)]
    da_re, da_im, dlog_dt = disc_vjp(tuple(dpar_g))
    db_re = jnp.swapaxes(_block_diag_extract(dbt_blk[:, 0]), 1, 2)
    db_im = jnp.swapaxes(_block_diag_extract(dbt_blk[:, 1]), 1, 2)
    dc_re = _block_diag_extract(dct_blk[:, 0])
    dc_im = _block_diag_extract(dct_blk[:, 1])

    big_grads = late_grads + mid_grads + early_grads
    theirs = list(late_theirs) + list(mid_theirs) + list(early_theirs)
    received = list(late_recv) + list(mid_recv) + list(early_recv)
    halves = [_chip_sum(g, t, r, pos, f"grad_chip_sum_{i}") for i, (g, t, r) in enumerate(zip(big_grads, theirs, received))]

    small_local = [jnp.concatenate([dg_pre0, dg_pre1]), jnp.concatenate([dg_post0, dg_post1]),
                   da_re[None], da_im[None], dlog_dt[None], db_re[None], db_im[None], dc_re[None], dc_im[None],
                   dd, db_glu, dg_kv.reshape(dm), db_f[0, :heads]]
    small_buf, (g_win_s, g_wglu_s, g_wout_s, g_kvw_s, g_fwin_s, g_fwout_s) = _finish_reductions(_pack(small_local), halves)
    (g_norm_pre, g_norm_post, g_a_re, g_a_im, g_log_dt, g_b_re, g_b_im, g_c_re, g_c_im, g_d_full, g_bglu_full,
     g_kv_norm, g_b_f) = _unpack(small_buf, small_local)
    shard = width // N_CHIPS
    g_d_own = lax.dynamic_slice(g_d_full, (0, chip * shard), (1, shard))
    g_bglu_own = lax.dynamic_slice(g_bglu_full, (0, chip * shard), (1, shard))

    big_w = big_shards
    big_g = [g_win_s, g_wglu_s, g_wout_s, g_kvw_s, g_fwin_s, g_fwout_s]
    big_m = [m_s5_w_in[0], m_s5_w_glu[0], m_s5_w_out[0], m_kv_w, m_fox_w_in[0], m_fox_w_out[0]]
    big_v = [v_s5_w_in[0], v_s5_w_glu[0], v_s5_w_out[0], v_kv_w, v_fox_w_in[0], v_fox_w_out[0]]
    big_upd = [_adamw(w, g, m, v, f"adamw_{i}") for i, (w, g, m, v) in enumerate(zip(big_w, big_g, big_m, big_v))]

    small_names = ["norm_pre", "norm_post", "s5_a_re", "s5_a_im", "s5_log_dt", "s5_b_re", "s5_b_im", "s5_c_re", "s5_c_im",
                   "s5_d", "s5_b_glu", "kv_norm", "kv_b_f"]
    small_w = [norm_pre, norm_post, s5_a_re, s5_a_im, s5_log_dt, s5_b_re, s5_b_im, s5_c_re, s5_c_im, s5_d, s5_b_glu, kv_norm, kv_b_f]
    small_m = [m_norm_pre, m_norm_post, m_s5_a_re, m_s5_a_im, m_s5_log_dt, m_s5_b_re, m_s5_b_im, m_s5_c_re, m_s5_c_im, m_s5_d, m_s5_b_glu, m_kv_norm, m_kv_b_f]
    small_v = [v_norm_pre, v_norm_post, v_s5_a_re, v_s5_a_im, v_s5_log_dt, v_s5_b_re, v_s5_b_im, v_s5_c_re, v_s5_c_im, v_s5_d, v_s5_b_glu, v_kv_norm, v_kv_b_f]
    small_g = [g_norm_pre, g_norm_post, g_a_re, g_a_im, g_log_dt, g_b_re, g_b_im, g_c_re, g_c_im, g_d_own, g_bglu_own, g_kv_norm, g_b_f]
    small_g = [g.reshape(w.shape) for g, w in zip(small_g, small_w)]
    sd, sm, sv = _adamw(_pack(small_w), _pack(small_g), _pack(small_m), _pack(small_v, fill=1.0), "adamw_small")
    small_delta, small_newm, small_newv = _unpack(sd, small_w), _unpack(sm, small_w), _unpack(sv, small_w)

    order = ["norm_pre", "norm_post", "s5_w_in", "s5_a_re", "s5_a_im", "s5_log_dt", "s5_b_re", "s5_b_im", "s5_c_re", "s5_c_im",
             "s5_d", "s5_w_glu", "s5_b_glu", "s5_w_out", "kv_norm", "kv_w", "kv_b_f", "fox_w_in", "fox_w_out"]
    big_names = ["s5_w_in", "s5_w_glu", "s5_w_out", "kv_w", "fox_w_in", "fox_w_out"]
    big_like = [s5_w_in, s5_w_glu, s5_w_out, kv_w, fox_w_in, fox_w_out]
    grads, deltas, new_m, new_v = {}, {}, {}, {}
    for i, n in enumerate(big_names):
        shp = big_like[i].shape
        grads[n] = big_g[i].reshape(shp)
        deltas[n], new_m[n], new_v[n] = (a.reshape(shp) for a in big_upd[i])
    for i, n in enumerate(small_names):
        grads[n], deltas[n], new_m[n], new_v[n] = small_g[i], small_delta[i], small_newm[i], small_newv[i]

    return (loss, grad_x[None], *[grads[n] for n in order], *[deltas[n] for n in order],
            *[new_m[n] for n in order], *[new_v[n] for n in order])
```
